```python
import jax, jax.numpy as jnp
from jax import lax
import numpy as np

D_MODEL = 2048
BATCH = 8
SEQ = 4096
DEPTH = 4

CHUNK = 64
Q_BLOCK = 128
N_MIXERS = 2
CONV_WIDTH = 31
N_HEADS = 16
QK_NOPE_DIM = 128
QK_ROPE_DIM = 64
V_HEAD_DIM = 128
Q_LORA_RANK = 512
KV_LORA_RANK = 512
D_FF = 4 * D_MODEL
ROPE_THETA = 10000.0
NORM_EPS = 1e-6
LN_EPS = 1e-5
N_CONV_LAYERS = (DEPTH + 1) // 2
N_MLA_LAYERS = DEPTH // 2

kernel_name = "hybrid_conformer_conv_mla_sqrelu_trunk"


def rms_norm(x, g):
    xf = x.astype(jnp.float32)
    y = xf * lax.rsqrt(jnp.mean(xf * xf, axis=-1, keepdims=True) + NORM_EPS)
    return (y * g.astype(jnp.float32)).astype(x.dtype)


def layer_norm(x, g, b):
    xf = x.astype(jnp.float32)
    mu = jnp.mean(xf, axis=-1, keepdims=True)
    xc = xf - mu
    var = jnp.mean(xc * xc, axis=-1, keepdims=True)
    y = xc * lax.rsqrt(var + LN_EPS) * g.astype(jnp.float32) + b.astype(jnp.float32)
    return y.astype(x.dtype)


def conv_module(h, w_pw1, b_pw1, w_dw, b_dw, ln_g, ln_b, w_pw2, b_pw2):
    u = h @ w_pw1 + b_pw1
    a, gate = jnp.split(u, 2, axis=-1)
    u = a * jax.nn.sigmoid(gate)
    u = lax.conv_general_dilated(
        u, w_dw[:, None, :].astype(u.dtype), window_strides=(1,),
        padding=[(CONV_WIDTH - 1, 0)],
        dimension_numbers=("NWC", "WIO", "NWC"),
        feature_group_count=D_MODEL) + b_dw
    u = jax.nn.silu(layer_norm(u, ln_g, ln_b))
    return u @ w_pw2 + b_pw2


def rope_tables(positions):
    inv_freq = ROPE_THETA ** (-jnp.arange(0, QK_ROPE_DIM, 2, dtype=jnp.float32) / QK_ROPE_DIM)
    ang = positions.astype(jnp.float32)[..., None] * inv_freq
    return jnp.cos(ang), jnp.sin(ang)


def apply_rope(x, cos, sin):
    xf = x.astype(jnp.float32)
    x1, x2 = jnp.split(xf, 2, axis=-1)
    out = jnp.concatenate([x1 * cos - x2 * sin, x2 * cos + x1 * sin], axis=-1)
    return out.astype(x.dtype)


def mla(h, cos, sin, w_in, q_norm_g, kv_norm_g, w_q_up, w_kv_up, w_o):
    B, S, _ = h.shape
    down = h @ w_in
    c_q, c_kv, k_pe = jnp.split(down, [Q_LORA_RANK, Q_LORA_RANK + KV_LORA_RANK], axis=-1)
    q = (rms_norm(c_q, q_norm_g) @ w_q_up).reshape(B, S, N_HEADS, QK_NOPE_DIM + QK_ROPE_DIM)
    q_nope, q_pe = jnp.split(q, [QK_NOPE_DIM], axis=-1)
    q_pe = apply_rope(q_pe, cos[:, :, None, :], sin[:, :, None, :])
    k_pe = apply_rope(k_pe, cos, sin)
    kv = (rms_norm(c_kv, kv_norm_g) @ w_kv_up).reshape(B, S, N_HEADS, QK_NOPE_DIM + V_HEAD_DIM)
    k_nope, v = jnp.split(kv, [QK_NOPE_DIM], axis=-1)

    n_blk = S // Q_BLOCK
    qn_blocks = q_nope.reshape(B, n_blk, Q_BLOCK, N_HEADS, QK_NOPE_DIM).transpose(1, 0, 2, 3, 4)
    qp_blocks = q_pe.reshape(B, n_blk, Q_BLOCK, N_HEADS, QK_ROPE_DIM).transpose(1, 0, 2, 3, 4)
    k_chunk = jnp.arange(S) // CHUNK
    scale = (QK_NOPE_DIM + QK_ROPE_DIM) ** -0.5

    def attend_block(args):
        blk, qn, qp = args
        s = (jnp.einsum('bqhd,bkhd->bhqk', qn, k_nope)
             + jnp.einsum('bqhr,bkr->bhqk', qp, k_pe)).astype(jnp.float32) * scale
        q_chunk = (blk * Q_BLOCK + jnp.arange(Q_BLOCK)) // CHUNK
        mask = k_chunk[None, :] <= q_chunk[:, None]
        s = jnp.where(mask[None, None], s, -jnp.inf)
        p = jax.nn.softmax(s, axis=-1).astype(v.dtype)
        return jnp.einsum('bhqk,bkhd->bqhd', p, v)

    o = lax.map(attend_block, (jnp.arange(n_blk), qn_blocks, qp_blocks))
    o = o.transpose(1, 0, 2, 3, 4).reshape(B, S, N_HEADS * V_HEAD_DIM)
    return o @ w_o


def sq_relu_mlp(h, w1, w2):
    return jnp.square(jax.nn.relu(h @ w1)) @ w2


def _fwd_setup_inputs(seed: int = 0) -> dict:
    key = jax.random.key(seed)
    ks = iter(jax.random.split(key, 32))
    f32 = jnp.float32

    def dense(shape, fan_in):
        return jax.random.normal(next(ks), shape, f32) * (fan_in ** -0.5)

    def gain(shape):
        return 1.0 + 0.02 * jax.random.normal(next(ks), shape, f32)

    def bias(shape):
        return 0.02 * jax.random.normal(next(ks), shape, f32)

    Nc, Nm, D = N_CONV_LAYERS, N_MLA_LAYERS, D_MODEL
    x = jax.random.normal(next(ks), (BATCH, SEQ, D), f32)
    positions = jnp.broadcast_to(jnp.arange(SEQ, dtype=jnp.int32)[None, :], (BATCH, SEQ))
    return {
        "x": x,
        "positions": positions,
        "norm_mixer_g": gain((DEPTH, D)),
        "norm_mlp_g": gain((DEPTH, D)),
        "conv_w_pw1": dense((Nc, D, 2 * D), D),
        "conv_b_pw1": bias((Nc, 2 * D)),
        "conv_w_dw": dense((Nc, CONV_WIDTH, D), CONV_WIDTH),
        "conv_b_dw": bias((Nc, D)),
        "conv_ln_g": gain((Nc, D)),
        "conv_ln_b": bias((Nc, D)),
        "conv_w_pw2": dense((Nc, D, D), D),
        "conv_b_pw2": bias((Nc, D)),
        "mla_w_in": dense((Nm, D, Q_LORA_RANK + KV_LORA_RANK + QK_ROPE_DIM), D),
        "mla_q_norm_g": gain((Nm, Q_LORA_RANK)),
        "mla_kv_norm_g": gain((Nm, KV_LORA_RANK)),
        "mla_w_q_up": dense((Nm, Q_LORA_RANK, N_HEADS * (QK_NOPE_DIM + QK_ROPE_DIM)), Q_LORA_RANK),
        "mla_w_kv_up": dense((Nm, KV_LORA_RANK, N_HEADS * (QK_NOPE_DIM + V_HEAD_DIM)), KV_LORA_RANK),
        "mla_w_o": dense((Nm, N_HEADS * V_HEAD_DIM, D), N_HEADS * V_HEAD_DIM),
        "mlp_w1": dense((DEPTH, D, D_FF), D),
        "mlp_w2": dense((DEPTH, D_FF, D), D_FF),
        "final_norm_g": gain((D,)),
    }


def _fwd_reference(x, positions, norm_mixer_g, norm_mlp_g,
              conv_w_pw1, conv_b_pw1, conv_w_dw, conv_b_dw, conv_ln_g, conv_ln_b,
              conv_w_pw2, conv_b_pw2,
              mla_w_in, mla_q_norm_g, mla_kv_norm_g, mla_w_q_up, mla_w_kv_up, mla_w_o,
              mlp_w1, mlp_w2, final_norm_g):
    cos, sin = rope_tables(positions)
    for layer in range(DEPTH):
        j = layer // N_MIXERS
        h = rms_norm(x, norm_mixer_g[layer])
        if layer % N_MIXERS == 0:
            x = x + conv_module(h, conv_w_pw1[j], conv_b_pw1[j], conv_w_dw[j], conv_b_dw[j],
                                conv_ln_g[j], conv_ln_b[j], conv_w_pw2[j], conv_b_pw2[j])
        else:
            x = x + mla(h, cos, sin, mla_w_in[j], mla_q_norm_g[j], mla_kv_norm_g[j],
                        mla_w_q_up[j], mla_w_kv_up[j], mla_w_o[j])
        h = rms_norm(x, norm_mlp_g[layer])
        x = x + sq_relu_mlp(h, mlp_w1[layer], mlp_w2[layer])
    return rms_norm(x, final_norm_g)


import jax as _jax
import jax.numpy as _jnp

TWIN_FORMAT = 'train_step'
FWD_PARAMS = ['x', 'positions', 'norm_mixer_g', 'norm_mlp_g', 'conv_w_pw1', 'conv_b_pw1', 'conv_w_dw', 'conv_b_dw', 'conv_ln_g', 'conv_ln_b', 'conv_w_pw2', 'conv_b_pw2', 'mla_w_in', 'mla_q_norm_g', 'mla_kv_norm_g', 'mla_w_q_up', 'mla_w_kv_up', 'mla_w_o', 'mlp_w1', 'mlp_w2', 'final_norm_g']
TWIN_WEIGHTS = ['norm_mixer_g', 'norm_mlp_g', 'conv_w_pw1', 'conv_b_pw1', 'conv_w_dw', 'conv_b_dw', 'conv_ln_g', 'conv_ln_b', 'conv_w_pw2', 'conv_b_pw2', 'mla_w_in', 'mla_q_norm_g', 'mla_kv_norm_g', 'mla_w_q_up', 'mla_w_kv_up', 'mla_w_o', 'mlp_w1', 'mlp_w2', 'final_norm_g']
TWIN_DIFF_INPUT = 'x'
TWIN_INPUTS = ['x', 'positions', 'norm_mixer_g', 'norm_mlp_g', 'conv_w_pw1', 'conv_b_pw1', 'conv_w_dw', 'conv_b_dw', 'conv_ln_g', 'conv_ln_b', 'conv_w_pw2', 'conv_b_pw2', 'mla_w_in', 'mla_q_norm_g', 'mla_kv_norm_g', 'mla_w_q_up', 'mla_w_kv_up', 'mla_w_o', 'mlp_w1', 'mlp_w2', 'final_norm_g', 'loss_target', 'm_norm_mixer_g', 'm_norm_mlp_g', 'm_conv_w_pw1', 'm_conv_b_pw1', 'm_conv_w_dw', 'm_conv_b_dw', 'm_conv_ln_g', 'm_conv_ln_b', 'm_conv_w_pw2', 'm_conv_b_pw2', 'm_mla_w_in', 'm_mla_q_norm_g', 'm_mla_kv_norm_g', 'm_mla_w_q_up', 'm_mla_w_kv_up', 'm_mla_w_o', 'm_mlp_w1', 'm_mlp_w2', 'm_final_norm_g', 'v_norm_mixer_g', 'v_norm_mlp_g', 'v_conv_w_pw1', 'v_conv_b_pw1', 'v_conv_w_dw', 'v_conv_b_dw', 'v_conv_ln_g', 'v_conv_ln_b', 'v_conv_w_pw2', 'v_conv_b_pw2', 'v_mla_w_in', 'v_mla_q_norm_g', 'v_mla_kv_norm_g', 'v_mla_w_q_up', 'v_mla_w_kv_up', 'v_mla_w_o', 'v_mlp_w1', 'v_mlp_w2', 'v_final_norm_g']
TWIN_OUTPUTS = ['loss', 'grad_x', 'grad_norm_mixer_g', 'grad_norm_mlp_g', 'grad_conv_w_pw1', 'grad_conv_b_pw1', 'grad_conv_w_dw', 'grad_conv_b_dw', 'grad_conv_ln_g', 'grad_conv_ln_b', 'grad_conv_w_pw2', 'grad_conv_b_pw2', 'grad_mla_w_in', 'grad_mla_q_norm_g', 'grad_mla_kv_norm_g', 'grad_mla_w_q_up', 'grad_mla_w_kv_up', 'grad_mla_w_o', 'grad_mlp_w1', 'grad_mlp_w2', 'grad_final_norm_g', 'delta_norm_mixer_g', 'delta_norm_mlp_g', 'delta_conv_w_pw1', 'delta_conv_b_pw1', 'delta_conv_w_dw', 'delta_conv_b_dw', 'delta_conv_ln_g', 'delta_conv_ln_b', 'delta_conv_w_pw2', 'delta_conv_b_pw2', 'delta_mla_w_in', 'delta_mla_q_norm_g', 'delta_mla_kv_norm_g', 'delta_mla_w_q_up', 'delta_mla_w_kv_up', 'delta_mla_w_o', 'delta_mlp_w1', 'delta_mlp_w2', 'delta_final_norm_g', 'new_m_norm_mixer_g', 'new_m_norm_mlp_g', 'new_m_conv_w_pw1', 'new_m_conv_b_pw1', 'new_m_conv_w_dw', 'new_m_conv_b_dw', 'new_m_conv_ln_g', 'new_m_conv_ln_b', 'new_m_conv_w_pw2', 'new_m_conv_b_pw2', 'new_m_mla_w_in', 'new_m_mla_q_norm_g', 'new_m_mla_kv_norm_g', 'new_m_mla_w_q_up', 'new_m_mla_w_kv_up', 'new_m_mla_w_o', 'new_m_mlp_w1', 'new_m_mlp_w2', 'new_m_final_norm_g', 'new_v_norm_mixer_g', 'new_v_norm_mlp_g', 'new_v_conv_w_pw1', 'new_v_conv_b_pw1', 'new_v_conv_w_dw', 'new_v_conv_b_dw', 'new_v_conv_ln_g', 'new_v_conv_ln_b', 'new_v_conv_w_pw2', 'new_v_conv_b_pw2', 'new_v_mla_w_in', 'new_v_mla_q_norm_g', 'new_v_mla_kv_norm_g', 'new_v_mla_w_q_up', 'new_v_mla_w_kv_up', 'new_v_mla_w_o', 'new_v_mlp_w1', 'new_v_mlp_w2', 'new_v_final_norm_g']
TWIN_LEAF_KINDS = {'loss': 'loss', 'grad_x': 'grad_x', 'grad_norm_mixer_g': 'grad_w', 'grad_norm_mlp_g': 'grad_w', 'grad_conv_w_pw1': 'grad_w', 'grad_conv_b_pw1': 'grad_w', 'grad_conv_w_dw': 'grad_w', 'grad_conv_b_dw': 'grad_w', 'grad_conv_ln_g': 'grad_w', 'grad_conv_ln_b': 'grad_w', 'grad_conv_w_pw2': 'grad_w', 'grad_conv_b_pw2': 'grad_w', 'grad_mla_w_in': 'grad_w', 'grad_mla_q_norm_g': 'grad_w', 'grad_mla_kv_norm_g': 'grad_w', 'grad_mla_w_q_up': 'grad_w', 'grad_mla_w_kv_up': 'grad_w', 'grad_mla_w_o': 'grad_w', 'grad_mlp_w1': 'grad_w', 'grad_mlp_w2': 'grad_w', 'grad_final_norm_g': 'grad_w', 'delta_norm_mixer_g': 'delta_w', 'delta_norm_mlp_g': 'delta_w', 'delta_conv_w_pw1': 'delta_w', 'delta_conv_b_pw1': 'delta_w', 'delta_conv_w_dw': 'delta_w', 'delta_conv_b_dw': 'delta_w', 'delta_conv_ln_g': 'delta_w', 'delta_conv_ln_b': 'delta_w', 'delta_conv_w_pw2': 'delta_w', 'delta_conv_b_pw2': 'delta_w', 'delta_mla_w_in': 'delta_w', 'delta_mla_q_norm_g': 'delta_w', 'delta_mla_kv_norm_g': 'delta_w', 'delta_mla_w_q_up': 'delta_w', 'delta_mla_w_kv_up': 'delta_w', 'delta_mla_w_o': 'delta_w', 'delta_mlp_w1': 'delta_w', 'delta_mlp_w2': 'delta_w', 'delta_final_norm_g': 'delta_w', 'new_m_norm_mixer_g': 'new_m', 'new_m_norm_mlp_g': 'new_m', 'new_m_conv_w_pw1': 'new_m', 'new_m_conv_b_pw1': 'new_m', 'new_m_conv_w_dw': 'new_m', 'new_m_conv_b_dw': 'new_m', 'new_m_conv_ln_g': 'new_m', 'new_m_conv_ln_b': 'new_m', 'new_m_conv_w_pw2': 'new_m', 'new_m_conv_b_pw2': 'new_m', 'new_m_mla_w_in': 'new_m', 'new_m_mla_q_norm_g': 'new_m', 'new_m_mla_kv_norm_g': 'new_m', 'new_m_mla_w_q_up': 'new_m', 'new_m_mla_w_kv_up': 'new_m', 'new_m_mla_w_o': 'new_m', 'new_m_mlp_w1': 'new_m', 'new_m_mlp_w2': 'new_m', 'new_m_final_norm_g': 'new_m', 'new_v_norm_mixer_g': 'new_v', 'new_v_norm_mlp_g': 'new_v', 'new_v_conv_w_pw1': 'new_v', 'new_v_conv_b_pw1': 'new_v', 'new_v_conv_w_dw': 'new_v', 'new_v_conv_b_dw': 'new_v', 'new_v_conv_ln_g': 'new_v', 'new_v_conv_ln_b': 'new_v', 'new_v_conv_w_pw2': 'new_v', 'new_v_conv_b_pw2': 'new_v', 'new_v_mla_w_in': 'new_v', 'new_v_mla_q_norm_g': 'new_v', 'new_v_mla_kv_norm_g': 'new_v', 'new_v_mla_w_q_up': 'new_v', 'new_v_mla_w_kv_up': 'new_v', 'new_v_mla_w_o': 'new_v', 'new_v_mlp_w1': 'new_v', 'new_v_mlp_w2': 'new_v', 'new_v_final_norm_g': 'new_v'}


def _forward(args):
    return _fwd_reference(*[args[k] for k in FWD_PARAMS])


def _output_shape():
    def fwd():
        inp = _fwd_setup_inputs(0)
        return _fwd_reference(*[inp[k] for k in FWD_PARAMS])
    out = _jax.eval_shape(fwd)
    return out.shape, out.dtype

N_MICROBATCH = 1
ADAM_LR = 0.001
ADAM_B1 = 0.9
ADAM_B2 = 0.999
ADAM_EPS = 1e-08
ADAM_WD = 0.01
ADAM_STEP = 10
PER_EXAMPLE_BATCH_AXIS = {'x': 0, 'positions': 0, 'loss_target': 0}
SHARED_INPUTS = []
_WEIGHT_DTYPES = {'norm_mixer_g': _jnp.float32, 'norm_mlp_g': _jnp.float32, 'conv_w_pw1': _jnp.float32, 'conv_b_pw1': _jnp.float32, 'conv_w_dw': _jnp.float32, 'conv_b_dw': _jnp.float32, 'conv_ln_g': _jnp.float32, 'conv_ln_b': _jnp.float32, 'conv_w_pw2': _jnp.float32, 'conv_b_pw2': _jnp.float32, 'mla_w_in': _jnp.float32, 'mla_q_norm_g': _jnp.float32, 'mla_kv_norm_g': _jnp.float32, 'mla_w_q_up': _jnp.float32, 'mla_w_kv_up': _jnp.float32, 'mla_w_o': _jnp.float32, 'mlp_w1': _jnp.float32, 'mlp_w2': _jnp.float32, 'final_norm_g': _jnp.float32}
MOMENT_SCALE = {'norm_mixer_g': 4.540348e-02, 'norm_mlp_g': 6.978168e-02, 'conv_w_pw1': 3.497649e-02, 'conv_b_pw1': 6.226128e-02, 'conv_w_dw': 4.726817e-02, 'conv_b_dw': 1.480007e-01, 'conv_ln_g': 7.473883e-02, 'conv_ln_b': 8.665336e-02, 'conv_w_pw2': 5.317124e-02, 'conv_b_pw2': 1.655894e-01, 'mla_w_in': 5.849085e-02, 'mla_q_norm_g': 1.241465e-02, 'mla_kv_norm_g': 8.622839e-02, 'mla_w_q_up': 4.943283e-03, 'mla_w_kv_up': 3.029303e-02, 'mla_w_o': 4.333291e-02, 'mlp_w1': 3.428195e-02, 'mlp_w2': 8.093293e-02, 'final_norm_g': 1.677227e+01}


def _to_microbatches(a, axis):
    t = _jnp.moveaxis(a, axis, 0)
    t = t.reshape((N_MICROBATCH, t.shape[0] // N_MICROBATCH) + t.shape[1:])
    return _jnp.moveaxis(t, 1, axis + 1)


def setup_inputs(seed: int = 0) -> dict:
    inp = _fwd_setup_inputs(seed)
    key = _jax.random.fold_in(_jax.random.key(seed), 7919)
    shape, _ = _output_shape()
    out = dict(inp)
    out["loss_target"] = _jax.random.normal(_jax.random.fold_in(key, 0), shape, _jnp.float32)
    for i, name in enumerate(TWIN_WEIGHTS):
        w = inp[name].astype(_jnp.float32)
        if MOMENT_SCALE is None:
            s = _jnp.sqrt(_jnp.mean(_jnp.square(w)) + 1e-30)
        else:
            s = MOMENT_SCALE[name]
        km, kv = _jax.random.split(_jax.random.fold_in(key, i + 1))
        out[name] = w
        out["m_" + name] = s * _jax.random.normal(km, w.shape, _jnp.float32)
        out["v_" + name] = (s * s) * _jax.random.uniform(kv, w.shape, _jnp.float32, 0.5, 1.5)
    if N_MICROBATCH > 1:
        for name, axis in PER_EXAMPLE_BATCH_AXIS.items():
            out[name] = _to_microbatches(out[name], axis)
    return {'x': out['x'], 'positions': out['positions'], 'norm_mixer_g': out['norm_mixer_g'], 'norm_mlp_g': out['norm_mlp_g'], 'conv_w_pw1': out['conv_w_pw1'], 'conv_b_pw1': out['conv_b_pw1'], 'conv_w_dw': out['conv_w_dw'], 'conv_b_dw': out['conv_b_dw'], 'conv_ln_g': out['conv_ln_g'], 'conv_ln_b': out['conv_ln_b'], 'conv_w_pw2': out['conv_w_pw2'], 'conv_b_pw2': out['conv_b_pw2'], 'mla_w_in': out['mla_w_in'], 'mla_q_norm_g': out['mla_q_norm_g'], 'mla_kv_norm_g': out['mla_kv_norm_g'], 'mla_w_q_up': out['mla_w_q_up'], 'mla_w_kv_up': out['mla_w_kv_up'], 'mla_w_o': out['mla_w_o'], 'mlp_w1': out['mlp_w1'], 'mlp_w2': out['mlp_w2'], 'final_norm_g': out['final_norm_g'], 'loss_target': out['loss_target'], 'm_norm_mixer_g': out['m_norm_mixer_g'], 'm_norm_mlp_g': out['m_norm_mlp_g'], 'm_conv_w_pw1': out['m_conv_w_pw1'], 'm_conv_b_pw1': out['m_conv_b_pw1'], 'm_conv_w_dw': out['m_conv_w_dw'], 'm_conv_b_dw': out['m_conv_b_dw'], 'm_conv_ln_g': out['m_conv_ln_g'], 'm_conv_ln_b': out['m_conv_ln_b'], 'm_conv_w_pw2': out['m_conv_w_pw2'], 'm_conv_b_pw2': out['m_conv_b_pw2'], 'm_mla_w_in': out['m_mla_w_in'], 'm_mla_q_norm_g': out['m_mla_q_norm_g'], 'm_mla_kv_norm_g': out['m_mla_kv_norm_g'], 'm_mla_w_q_up': out['m_mla_w_q_up'], 'm_mla_w_kv_up': out['m_mla_w_kv_up'], 'm_mla_w_o': out['m_mla_w_o'], 'm_mlp_w1': out['m_mlp_w1'], 'm_mlp_w2': out['m_mlp_w2'], 'm_final_norm_g': out['m_final_norm_g'], 'v_norm_mixer_g': out['v_norm_mixer_g'], 'v_norm_mlp_g': out['v_norm_mlp_g'], 'v_conv_w_pw1': out['v_conv_w_pw1'], 'v_conv_b_pw1': out['v_conv_b_pw1'], 'v_conv_w_dw': out['v_conv_w_dw'], 'v_conv_b_dw': out['v_conv_b_dw'], 'v_conv_ln_g': out['v_conv_ln_g'], 'v_conv_ln_b': out['v_conv_ln_b'], 'v_conv_w_pw2': out['v_conv_w_pw2'], 'v_conv_b_pw2': out['v_conv_b_pw2'], 'v_mla_w_in': out['v_mla_w_in'], 'v_mla_q_norm_g': out['v_mla_q_norm_g'], 'v_mla_kv_norm_g': out['v_mla_kv_norm_g'], 'v_mla_w_q_up': out['v_mla_w_q_up'], 'v_mla_w_kv_up': out['v_mla_w_kv_up'], 'v_mla_w_o': out['v_mla_w_o'], 'v_mlp_w1': out['v_mlp_w1'], 'v_mlp_w2': out['v_mlp_w2'], 'v_final_norm_g': out['v_final_norm_g']}


def _loss(weights, diff, rest, loss_target):
    with _jax.named_scope("forward"):
        args = {**rest, TWIN_DIFF_INPUT: diff, **{k: w.astype(_WEIGHT_DTYPES[k]) for k, w in weights.items()}}
        y = _forward(args)
    with _jax.named_scope("loss_head"):
        err = _jnp.square(y.astype(_jnp.float32) - loss_target)
        return 0.5 * _jnp.sum(_jnp.mean(err, axis=-1)) if err.ndim else 0.5 * err


def _adamw(w, g, m, v):
    m = ADAM_B1 * m + (1.0 - ADAM_B1) * g
    v = ADAM_B2 * v + (1.0 - ADAM_B2) * _jnp.square(g)
    m_hat = m / (1.0 - ADAM_B1 ** ADAM_STEP)
    v_hat = v / (1.0 - ADAM_B2 ** ADAM_STEP)
    delta = -ADAM_LR * (m_hat / (_jnp.sqrt(v_hat) + ADAM_EPS) + ADAM_WD * w)
    return delta, m, v


def reference(x, positions, norm_mixer_g, norm_mlp_g, conv_w_pw1, conv_b_pw1, conv_w_dw, conv_b_dw, conv_ln_g, conv_ln_b, conv_w_pw2, conv_b_pw2, mla_w_in, mla_q_norm_g, mla_kv_norm_g, mla_w_q_up, mla_w_kv_up, mla_w_o, mlp_w1, mlp_w2, final_norm_g, loss_target, m_norm_mixer_g, m_norm_mlp_g, m_conv_w_pw1, m_conv_b_pw1, m_conv_w_dw, m_conv_b_dw, m_conv_ln_g, m_conv_ln_b, m_conv_w_pw2, m_conv_b_pw2, m_mla_w_in, m_mla_q_norm_g, m_mla_kv_norm_g, m_mla_w_q_up, m_mla_w_kv_up, m_mla_w_o, m_mlp_w1, m_mlp_w2, m_final_norm_g, v_norm_mixer_g, v_norm_mlp_g, v_conv_w_pw1, v_conv_b_pw1, v_conv_w_dw, v_conv_b_dw, v_conv_ln_g, v_conv_ln_b, v_conv_w_pw2, v_conv_b_pw2, v_mla_w_in, v_mla_q_norm_g, v_mla_kv_norm_g, v_mla_w_q_up, v_mla_w_kv_up, v_mla_w_o, v_mlp_w1, v_mlp_w2, v_final_norm_g):
    given = dict(x=x, positions=positions, norm_mixer_g=norm_mixer_g, norm_mlp_g=norm_mlp_g, conv_w_pw1=conv_w_pw1, conv_b_pw1=conv_b_pw1, conv_w_dw=conv_w_dw, conv_b_dw=conv_b_dw, conv_ln_g=conv_ln_g, conv_ln_b=conv_ln_b, conv_w_pw2=conv_w_pw2, conv_b_pw2=conv_b_pw2, mla_w_in=mla_w_in, mla_q_norm_g=mla_q_norm_g, mla_kv_norm_g=mla_kv_norm_g, mla_w_q_up=mla_w_q_up, mla_w_kv_up=mla_w_kv_up, mla_w_o=mla_w_o, mlp_w1=mlp_w1, mlp_w2=mlp_w2, final_norm_g=final_norm_g, loss_target=loss_target, m_norm_mixer_g=m_norm_mixer_g, m_norm_mlp_g=m_norm_mlp_g, m_conv_w_pw1=m_conv_w_pw1, m_conv_b_pw1=m_conv_b_pw1, m_conv_w_dw=m_conv_w_dw, m_conv_b_dw=m_conv_b_dw, m_conv_ln_g=m_conv_ln_g, m_conv_ln_b=m_conv_ln_b, m_conv_w_pw2=m_conv_w_pw2, m_conv_b_pw2=m_conv_b_pw2, m_mla_w_in=m_mla_w_in, m_mla_q_norm_g=m_mla_q_norm_g, m_mla_kv_norm_g=m_mla_kv_norm_g, m_mla_w_q_up=m_mla_w_q_up, m_mla_w_kv_up=m_mla_w_kv_up, m_mla_w_o=m_mla_w_o, m_mlp_w1=m_mlp_w1, m_mlp_w2=m_mlp_w2, m_final_norm_g=m_final_norm_g, v_norm_mixer_g=v_norm_mixer_g, v_norm_mlp_g=v_norm_mlp_g, v_conv_w_pw1=v_conv_w_pw1, v_conv_b_pw1=v_conv_b_pw1, v_conv_w_dw=v_conv_w_dw, v_conv_b_dw=v_conv_b_dw, v_conv_ln_g=v_conv_ln_g, v_conv_ln_b=v_conv_ln_b, v_conv_w_pw2=v_conv_w_pw2, v_conv_b_pw2=v_conv_b_pw2, v_mla_w_in=v_mla_w_in, v_mla_q_norm_g=v_mla_q_norm_g, v_mla_kv_norm_g=v_mla_kv_norm_g, v_mla_w_q_up=v_mla_w_q_up, v_mla_w_kv_up=v_mla_w_kv_up, v_mla_w_o=v_mla_w_o, v_mlp_w1=v_mlp_w1, v_mlp_w2=v_mlp_w2, v_final_norm_g=v_final_norm_g)
    weights = {n: given[n] for n in TWIN_WEIGHTS}
    shared = {n: given[n] for n in SHARED_INPUTS}
    per_example = {n: given[n] for n in ['x', 'positions']}
    grad_fn = _jax.value_and_grad(_loss, argnums=(0, 1))

    def one_microbatch(ex, loss_target):
        ex = dict(ex)
        diff = ex.pop(TWIN_DIFF_INPUT)
        return grad_fn(weights, diff, {**shared, **ex}, loss_target)

    if N_MICROBATCH == 1:
        loss, (grad_w, grad_x) = one_microbatch(per_example, given["loss_target"])
    else:
        def body(carry, xs):
            loss_sum, grad_sum = carry
            l_k, (gw_k, gx_k) = one_microbatch(xs[0], xs[1])
            with _jax.named_scope("update"):
                return (loss_sum + l_k, _jax.tree.map(_jnp.add, grad_sum, gw_k)), gx_k

        init = (_jnp.zeros((), _jnp.float32), _jax.tree.map(_jnp.zeros_like, weights))
        (loss, grad_w), grad_x = _jax.lax.scan(body, init, (per_example, given["loss_target"]))
    with _jax.named_scope("update"):
        delta_w, new_m, new_v = {}, {}, {}
        for n in TWIN_WEIGHTS:
            delta_w[n], new_m[n], new_v[n] = _adamw(weights[n], grad_w[n], given["m_" + n], given["v_" + n])
    return (loss, grad_x, *[grad_w[n] for n in TWIN_WEIGHTS], *[delta_w[n] for n in TWIN_WEIGHTS],
            *[new_m[n] for n in TWIN_WEIGHTS], *[new_v[n] for n in TWIN_WEIGHTS])
```

```python
import functools

import jax
import jax.numpy as jnp
from jax import lax
from jax.experimental import pallas as pl
from jax.experimental.pallas import tpu as pltpu

F32 = jnp.float32
BF16 = jnp.bfloat16

NORM_EPS = 1e-6
LN_EPS = 1e-5
ROPE_THETA = 10000.0
CHUNK_BITS = 6
HEAD_NOPE = 128
HEAD_ROPE = 64
HEAD_V = 128
HEAD_QK_PAD = 256
CONV_W = 31
HALO = 32
N_DEV = 8

ADAM_LR = 0.001
ADAM_B1 = 0.9
ADAM_B2 = 0.999
ADAM_EPS = 1e-08
ADAM_WD = 0.01
ADAM_STEP = 10

V7X_VMEM_BYTES = 64 * 1024 * 1024
VMEM_LIMIT = (V7X_VMEM_BYTES * 3) // 4
LANE = 128

MESH = pl.DeviceIdType.MESH
ANY = pl.BlockSpec(memory_space=pl.ANY)
VMEM_SPEC = pl.BlockSpec(memory_space=pltpu.VMEM)


def _cp(**kw):
    return pltpu.CompilerParams(vmem_limit_bytes=VMEM_LIMIT, **kw)


SUBLANE_BF16 = 16

TM_PREF = 1024
TN_PREF = 1024
TK_PREF = 2048


def _tile(n, pref, mult=SUBLANE_BF16):
    if n <= pref + pref // 2:
        return n
    t = (pref // mult) * mult
    while t >= mult:
        if n % t == 0:
            return t
        t -= mult
    return n


def _sigmoid(x):
    return 1.0 / (1.0 + jnp.exp(-x))


def _rot_half(x):
    n = x.shape[-1]
    lane = lax.broadcasted_iota(jnp.int32, x.shape, x.ndim - 1)
    first = (lane & 63) < 32
    return jnp.where(first, pltpu.roll(x, n - 32, x.ndim - 1), pltpu.roll(x, 32, x.ndim - 1))


def _rope(x, c, s):
    return x * c + _rot_half(x) * s


def _rope_t(d, c, s):
    return d * c + _rot_half(d * s)


def _chunk_mask(t):
    row = lax.broadcasted_iota(jnp.int32, (t, t), 0)
    col = lax.broadcasted_iota(jnp.int32, (t, t), 1)
    return jnp.right_shift(col, CHUNK_BITS) <= jnp.right_shift(row, CHUNK_BITS)


def _rms_fwd(name, x, g):
    t, d = x.shape
    tm = _tile(t, 512)

    def body(x_ref, g_ref, o_ref):
        xf = x_ref[...]
        r = lax.rsqrt(jnp.mean(xf * xf, axis=-1, keepdims=True) + NORM_EPS)
        o_ref[...] = (xf * r * g_ref[...]).astype(o_ref.dtype)

    return pl.pallas_call(
        body, name=name, grid=(t // tm,),
        in_specs=[pl.BlockSpec((tm, d), lambda i: (i, 0)), pl.BlockSpec((1, d), lambda i: (0, 0))],
        out_specs=pl.BlockSpec((tm, d), lambda i: (i, 0)),
        out_shape=jax.ShapeDtypeStruct((t, d), BF16),
        compiler_params=_cp(),
    )(x, g)


def _rms_bwd_math(xf, g, dy):
    r = lax.rsqrt(jnp.mean(xf * xf, axis=-1, keepdims=True) + NORM_EPS)
    xh = xf * r
    dg = jnp.sum(dy * xh, axis=0, keepdims=True)
    dxh = dy * g
    dx = r * (dxh - xh * jnp.mean(dxh * xh, axis=-1, keepdims=True))
    return dx, dg


def _rms_bwd(name, x, g, dy, resid):
    t, d = x.shape
    tm = _tile(t, 256)

    def body(x_ref, g_ref, dy_ref, r_ref, dx_ref, dxb_ref, dg_ref, cs_ref):
        @pl.when(pl.program_id(0) == 0)
        def _():
            dg_ref[...] = jnp.zeros_like(dg_ref)
            cs_ref[...] = jnp.zeros_like(cs_ref)

        dx, dg = _rms_bwd_math(x_ref[...], g_ref[...], dy_ref[...])
        tot = r_ref[...] + dx
        dx_ref[...] = tot
        dxb_ref[...] = tot.astype(BF16)
        dg_ref[...] += dg
        cs_ref[...] += jnp.sum(tot, axis=0, keepdims=True)

    row = pl.BlockSpec((tm, d), lambda i: (i, 0))
    vec = pl.BlockSpec((1, d), lambda i: (0, 0))
    return pl.pallas_call(
        body, name=name, grid=(t // tm,),
        in_specs=[row, vec, row, row],
        out_specs=[row, row, vec, vec],
        out_shape=[jax.ShapeDtypeStruct((t, d), F32), jax.ShapeDtypeStruct((t, d), BF16),
                   jax.ShapeDtypeStruct((1, d), F32), jax.ShapeDtypeStruct((1, d), F32)],
        compiler_params=_cp(dimension_semantics=("arbitrary",)),
    )(x, g, dy, resid)


def _final_loss(name, x, g, target):
    t, d = x.shape
    tm = _tile(t, 256)

    def body(x_ref, g_ref, t_ref, loss_ref, dx_ref, dxb_ref, dg_ref, cs_ref):
        @pl.when(pl.program_id(0) == 0)
        def _():
            loss_ref[...] = jnp.zeros_like(loss_ref)
            dg_ref[...] = jnp.zeros_like(dg_ref)
            cs_ref[...] = jnp.zeros_like(cs_ref)

        xf = x_ref[...]
        gg = g_ref[...]
        r = lax.rsqrt(jnp.mean(xf * xf, axis=-1, keepdims=True) + NORM_EPS)
        err = xf * r * gg - t_ref[...]
        part = 0.5 * jnp.sum(jnp.mean(err * err, axis=-1, keepdims=True), axis=0, keepdims=True)
        loss_ref[...] += jnp.broadcast_to(part, loss_ref.shape)
        dx, dg = _rms_bwd_math(xf, gg, err * (1.0 / d))
        dx_ref[...] = dx
        dxb_ref[...] = dx.astype(BF16)
        dg_ref[...] += dg
        cs_ref[...] += jnp.sum(dx, axis=0, keepdims=True)

    row = pl.BlockSpec((tm, d), lambda i: (i, 0))
    vec = pl.BlockSpec((1, d), lambda i: (0, 0))
    return pl.pallas_call(
        body, name=name, grid=(t // tm,),
        in_specs=[row, vec, row],
        out_specs=[pl.BlockSpec((1, LANE), lambda i: (0, 0)), row, row, vec, vec],
        out_shape=[jax.ShapeDtypeStruct((1, LANE), F32), jax.ShapeDtypeStruct((t, d), F32),
                   jax.ShapeDtypeStruct((t, d), BF16), jax.ShapeDtypeStruct((1, d), F32),
                   jax.ShapeDtypeStruct((1, d), F32)],
        compiler_params=_cp(dimension_semantics=("arbitrary",)),
    )(x, g, target)


_DIMS = {
    "nn": (((1,), (0,)), ((), ())),
    "nt": (((1,), (1,)), ((), ())),
    "tn": (((0,), (0,)), ((), ())),
}


def _mm(name, a, bs, *, mode, m, n, k, epilogue, out_shape, out_specs, extras=(), extra_specs=(),
        a_lead=None, aliases=None, tn_div=1):
    tm, tn, tk = _tiles(m, n, k, tn_div)
    nk = k // tk
    nb, ne = len(bs), len(extras)
    no = len(out_shape)
    dims = _DIMS[mode]

    def with_lead(shape, idx, lead):
        if lead is None:
            return pl.BlockSpec(shape, idx)
        return pl.BlockSpec((None,) + shape, lambda i, j, kk: (lead,) + idx(i, j, kk))

    if mode == "tn":
        a_spec = with_lead((tk, tm), lambda i, j, kk: (kk, i), a_lead)
    else:
        a_spec = with_lead((tm, tk), lambda i, j, kk: (i, kk), a_lead)
    b_specs = []
    for _, lead, off in bs:
        if mode == "nt":
            b_specs.append(with_lead((tn, tk), lambda i, j, kk, off=off: (j + off, kk), lead))
        else:
            b_specs.append(with_lead((tk, tn), lambda i, j, kk, off=off: (kk, j + off), lead))

    def body(*refs):
        a_ref = refs[0]
        b_refs = refs[1:1 + nb]
        ex = refs[1 + nb:1 + nb + ne]
        outs = refs[1 + nb + ne:1 + nb + ne + no]
        accs = refs[1 + nb + ne + no:]

        def part(b_ref):
            return lax.dot_general(a_ref[...], b_ref[...], dims, preferred_element_type=F32)

        if nk == 1:
            epilogue([part(b_ref) for b_ref in b_refs], ex, outs)
            return
        kk = pl.program_id(2)

        @pl.when(kk == 0)
        def _():
            for acc, b_ref in zip(accs, b_refs):
                acc[...] = part(b_ref)

        @pl.when(kk > 0)
        def _():
            for acc, b_ref in zip(accs, b_refs):
                acc[...] += part(b_ref)

        @pl.when(kk == nk - 1)
        def _():
            epilogue([acc[...] for acc in accs], ex, outs)

    scratch = [pltpu.VMEM((tm, tn), F32) for _ in range(nb)] if nk > 1 else []
    return pl.pallas_call(
        body, name=name, grid=(m // tm, n // tn, nk),
        in_specs=[a_spec] + b_specs + list(extra_specs),
        out_specs=list(out_specs), out_shape=list(out_shape), scratch_shapes=scratch,
        input_output_aliases=aliases or {},
        compiler_params=_cp(dimension_semantics=("arbitrary", "arbitrary", "arbitrary")),
    )(a, *[b for b, _, _ in bs], *extras), (tm, tn, tk)


def _ij(tm, tn):
    return pl.BlockSpec((tm, tn), lambda i, j, kk: (i, j))


def _tiles(m, n, k, tn_div=1):
    return _tile(m, TM_PREF), _tile(n, TN_PREF // tn_div, LANE), _tile(k, TK_PREF, LANE)


def _mm_plain(name, a, b, b_lead, mode, out_dtype):
    m, k = a.shape
    n = b.shape[-1] if mode == "nn" else b.shape[-2]
    tm, tn, _ = _tiles(m, n, k)

    def epilogue(accs, ex, outs):
        outs[0][...] = accs[0].astype(out_dtype)

    return _mm(name, a, [(b, b_lead, 0)], mode=mode, m=m, n=n, k=k, epilogue=epilogue,
               out_shape=[jax.ShapeDtypeStruct((m, n), out_dtype)], out_specs=[_ij(tm, tn)])[0][0]


def _mm_res(name, a, b, b_lead, resid, bias=None):
    m, k = a.shape
    n = b.shape[-1]
    tm, tn, _ = _tiles(m, n, k)
    extras, specs = [resid], [_ij(tm, tn)]
    if bias is not None:
        extras.append(bias)
        specs.append(pl.BlockSpec((1, tn), lambda i, j, kk: (0, j)))

    def epilogue(accs, ex, outs):
        y = ex[0][...] + accs[0]
        if bias is not None:
            y = y + ex[1][...]
        outs[0][...] = y

    return _mm(name, a, [(b, b_lead, 0)], mode="nn", m=m, n=n, k=k, epilogue=epilogue,
               extras=extras, extra_specs=specs,
               out_shape=[jax.ShapeDtypeStruct((m, n), F32)], out_specs=[_ij(tm, tn)])[0][0]


def _mm_mlp_up(name, h, w1, lead):
    m, k = h.shape
    n = w1.shape[-1]
    tm, tn, _ = _tiles(m, n, k)

    def epilogue(accs, ex, outs):
        z = accs[0]
        outs[0][...] = z.astype(BF16)
        r = jnp.maximum(z, 0.0)
        outs[1][...] = (r * r).astype(BF16)

    sh = jax.ShapeDtypeStruct((m, n), BF16)
    return _mm(name, h, [(w1, lead, 0)], mode="nn", m=m, n=n, k=k, epilogue=epilogue,
               out_shape=[sh, sh], out_specs=[_ij(tm, tn), _ij(tm, tn)])[0]


def _mm_mlp_dz(name, g, w2, lead, z):
    m, k = g.shape
    n = w2.shape[-2]
    tm, tn, _ = _tiles(m, n, k)

    def epilogue(accs, ex, outs):
        outs[0][...] = (accs[0] * (2.0 * jnp.maximum(ex[0][...].astype(F32), 0.0))).astype(BF16)

    return _mm(name, g, [(w2, lead, 0)], mode="nt", m=m, n=n, k=k, epilogue=epilogue,
               extras=[z], extra_specs=[_ij(tm, tn)],
               out_shape=[jax.ShapeDtypeStruct((m, n), BF16)], out_specs=[_ij(tm, tn)])[0][0]


def _mm_glu(name, h, w, lead, bias):
    m, k = h.shape
    n = w.shape[-1] // 2
    tm, tn, _ = _tiles(m, n, k, 2)
    off = n // tn

    def epilogue(accs, ex, outs):
        a = accs[0] + ex[0][...]
        gate = accs[1] + ex[1][...]
        outs[0][...] = a.astype(BF16)
        outs[1][...] = gate.astype(BF16)
        outs[2][...] = a * _sigmoid(gate)

    shb = jax.ShapeDtypeStruct((m, n), BF16)
    return _mm(name, h, [(w, lead, 0), (w, lead, off)], mode="nn", m=m, n=n, k=k, epilogue=epilogue,
               extras=[bias, bias],
               extra_specs=[pl.BlockSpec((1, tn), lambda i, j, kk: (0, j)),
                            pl.BlockSpec((1, tn), lambda i, j, kk: (0, j + off))],
               out_shape=[shb, shb, jax.ShapeDtypeStruct((m, n), F32)],
               out_specs=[_ij(tm, tn)] * 3, tn_div=2)[0]


def _mm_q(name, qn, wq_pad, lead, cq, sq):
    m, k = qn.shape
    n = wq_pad.shape[-1]
    tm, tn, _ = _tiles(m, n, k)
    rep = tn // HEAD_QK_PAD

    def epilogue(accs, ex, outs):
        c = jnp.tile(ex[0][...], (1, rep))
        s = jnp.tile(ex[1][...], (1, rep))
        outs[0][...] = _rope(accs[0], c, s).astype(BF16)

    tab = pl.BlockSpec((tm, HEAD_QK_PAD), lambda i, j, kk: (i, 0))
    return _mm(name, qn, [(wq_pad, lead, 0)], mode="nn", m=m, n=n, k=k, epilogue=epilogue,
               extras=[cq, sq], extra_specs=[tab, tab],
               out_shape=[jax.ShapeDtypeStruct((m, n), BF16)], out_specs=[_ij(tm, tn)])[0][0]


def _mm_kv(name, kvn, wkv, lead, kpe):
    m, k = kvn.shape
    n = wkv.shape[-1]
    tm, tn, _ = _tiles(m, n, k)
    heads = tn // (HEAD_NOPE + HEAD_V)

    def epilogue(accs, ex, outs):
        acc = accs[0]
        pe = ex[0][...].astype(F32)
        kparts, vparts = [], []
        for hh in range(heads):
            base = hh * (HEAD_NOPE + HEAD_V)
            kparts += [acc[:, base:base + HEAD_NOPE], pe]
            vparts.append(acc[:, base + HEAD_NOPE:base + HEAD_NOPE + HEAD_V])
        outs[0][...] = jnp.concatenate(kparts, axis=1).astype(BF16)
        outs[1][...] = jnp.concatenate(vparts, axis=1).astype(BF16) if heads > 1 else vparts[0].astype(BF16)

    return _mm(name, kvn, [(wkv, lead, 0)], mode="nn", m=m, n=n, k=k, epilogue=epilogue,
               extras=[kpe], extra_specs=[pl.BlockSpec((tm, LANE), lambda i, j, kk: (i, 0))],
               out_shape=[jax.ShapeDtypeStruct((m, n), BF16),
                          jax.ShapeDtypeStruct((m, n // 2), BF16)],
               out_specs=[_ij(tm, heads * HEAD_QK_PAD), _ij(tm, tn // 2)])[0]


def _mm_wgrad(name, a, b, buf, lead):
    t, m = a.shape
    n = b.shape[-1]
    tm, tn, _ = _tiles(m, n, t)

    def epilogue(accs, ex, outs):
        outs[0][...] = accs[0].astype(BF16)

    return _mm(name, a, [(b, None, 0)], mode="tn", m=m, n=n, k=t, epilogue=epilogue,
               extras=[buf], extra_specs=[ANY], aliases={2: 0},
               out_shape=[jax.ShapeDtypeStruct(buf.shape, BF16)],
               out_specs=[pl.BlockSpec((None, tm, tn), lambda i, j, kk: (lead, i, j))])[0][0]


CONV_ROWS = 256
CONV_RT = 64
CONV_CW = 256
CONV_LR = 32


def _ln_stats(c):
    mu = jnp.mean(c, axis=-1, keepdims=True)
    xc = c - mu
    rstd = lax.rsqrt(jnp.mean(xc * xc, axis=-1, keepdims=True) + LN_EPS)
    return xc * rstd, rstd


def _conv_fwd(name, glu, w_dw, b_dw, ln_g, ln_b):
    t, d = glu.shape
    tt = _tile(t, CONV_ROWS)
    rt, cw, lr = min(CONV_RT, tt), min(CONV_CW, d), min(CONV_LR, tt)
    hb = tt // HALO

    def body(gc_ref, gp_ref, w_ref, b_ref, lg_ref, lb_ref, c_ref, s_ref, buf):
        i = pl.program_id(0)
        buf[0:HALO, :] = jnp.where(i > 0, gp_ref[...], 0.0)
        buf[HALO:HALO + tt, :] = gc_ref[...]

        def chunk(cb, carry):
            col = pl.ds(pl.multiple_of(cb * cw, cw), cw)
            for r0 in range(0, tt, rt):
                acc = jnp.broadcast_to(b_ref[:, col], (rt, cw))
                for k in range(CONV_W):
                    lo = r0 + HALO - (CONV_W - 1) + k
                    acc = acc + w_ref[k:k + 1, col] * buf[lo:lo + rt, col]
                c_ref[r0:r0 + rt, col] = acc
            return carry

        lax.fori_loop(0, d // cw, chunk, 0)

        def ln(r, carry):
            rows = pl.ds(pl.multiple_of(r * lr, lr), lr)
            xh, _ = _ln_stats(c_ref[rows, :])
            y = xh * lg_ref[...] + lb_ref[...]
            s_ref[rows, :] = (y * _sigmoid(y)).astype(BF16)
            return carry

        lax.fori_loop(0, tt // lr, ln, 0)

    row = pl.BlockSpec((tt, d), lambda i: (i, 0))
    vec = pl.BlockSpec((1, d), lambda i: (0, 0))
    return pl.pallas_call(
        body, name=name, grid=(t // tt,),
        in_specs=[row, pl.BlockSpec((HALO, d), lambda i: (jnp.maximum(i * hb - 1, 0), 0)),
                  pl.BlockSpec((HALO, d), lambda i: (0, 0)), vec, vec, vec],
        out_specs=[row, row],
        out_shape=[jax.ShapeDtypeStruct((t, d), F32), jax.ShapeDtypeStruct((t, d), BF16)],
        scratch_shapes=[pltpu.VMEM((HALO + tt, d), F32)],
        compiler_params=_cp(dimension_semantics=("arbitrary",)),
    )(glu, glu, w_dw, b_dw, ln_g, ln_b)


def _conv_bwd_ln(name, ds, c, ln_g, ln_b):
    t, d = c.shape
    tt = _tile(t, CONV_ROWS)
    lr = min(CONV_LR, tt)

    def body(ds_ref, c_ref, lg_ref, lb_ref, dc_ref, dg_ref, db_ref, dbdw_ref):
        @pl.when(pl.program_id(0) == 0)
        def _():
            dg_ref[...] = jnp.zeros_like(dg_ref)
            db_ref[...] = jnp.zeros_like(db_ref)
            dbdw_ref[...] = jnp.zeros_like(dbdw_ref)

        def chunk(r, carry):
            rows = pl.ds(pl.multiple_of(r * lr, lr), lr)
            xh, rstd = _ln_stats(c_ref[rows, :])
            g = lg_ref[...]
            y = xh * g + lb_ref[...]
            sg = _sigmoid(y)
            dy = ds_ref[rows, :] * (sg * (1.0 + y * (1.0 - sg)))
            dxh = dy * g
            dc = rstd * (dxh - jnp.mean(dxh, axis=-1, keepdims=True)
                         - xh * jnp.mean(dxh * xh, axis=-1, keepdims=True))
            dc_ref[rows, :] = dc
            dg_ref[...] += jnp.sum(dy * xh, axis=0, keepdims=True)
            db_ref[...] += jnp.sum(dy, axis=0, keepdims=True)
            dbdw_ref[...] += jnp.sum(dc, axis=0, keepdims=True)
            return carry

        lax.fori_loop(0, tt // lr, chunk, 0)

    row = pl.BlockSpec((tt, d), lambda i: (i, 0))
    vec = pl.BlockSpec((1, d), lambda i: (0, 0))
    vsh = jax.ShapeDtypeStruct((1, d), F32)
    return pl.pallas_call(
        body, name=name, grid=(t // tt,),
        in_specs=[row, row, vec, vec], out_specs=[row, vec, vec, vec],
        out_shape=[jax.ShapeDtypeStruct((t, d), F32), vsh, vsh, vsh],
        compiler_params=_cp(dimension_semantics=("arbitrary",)),
    )(ds, c, ln_g, ln_b)


def _conv_bwd_dw(name, dc, glu, ua, ug, w_dw):
    t, d = dc.shape
    tt = _tile(t, CONV_ROWS)
    rt, cw = min(CONV_RT, tt), min(CONV_CW, d)
    hb = tt // HALO
    nt = t // tt

    def body(dcc_ref, dcn_ref, gc_ref, gp_ref, ua_ref, ug_ref, w_ref,
             du_ref, dw_ref, dbu_ref, dbuf, gbuf, wacc):
        i = pl.program_id(0)

        @pl.when(i == 0)
        def _():
            wacc[...] = jnp.zeros_like(wacc)
            dbu_ref[...] = jnp.zeros_like(dbu_ref)

        dbuf[0:tt, :] = dcc_ref[...]
        dbuf[tt:tt + HALO, :] = jnp.where(i < nt - 1, dcn_ref[...], 0.0)
        gbuf[0:HALO, :] = jnp.where(i > 0, gp_ref[...], 0.0)
        gbuf[HALO:HALO + tt, :] = gc_ref[...]

        def chunk(cb, carry):
            c0 = pl.multiple_of(cb * cw, cw)
            col = pl.ds(c0, cw)
            colg = pl.ds(pl.multiple_of(d + cb * cw, cw), cw)
            for r0 in range(0, tt, rt):
                dcr = dbuf[r0:r0 + rt, col]
                dgl = jnp.zeros((rt, cw), F32)
                for k in range(CONV_W):
                    hi = r0 + (CONV_W - 1) - k
                    dgl = dgl + w_ref[k:k + 1, col] * dbuf[hi:hi + rt, col]
                    lo = r0 + HALO - (CONV_W - 1) + k
                    prod = dcr * gbuf[lo:lo + rt, col]
                    part = prod[0:8, :]
                    for r in range(8, rt, 8):
                        part = part + prod[r:r + 8, :]
                    wacc[8 * k:8 * k + 8, col] += part
                a = ua_ref[r0:r0 + rt, col].astype(F32)
                sg = _sigmoid(ug_ref[r0:r0 + rt, col].astype(F32))
                da = dgl * sg
                dgate = dgl * a * sg * (1.0 - sg)
                du_ref[r0:r0 + rt, col] = da.astype(BF16)
                du_ref[r0:r0 + rt, colg] = dgate.astype(BF16)
                dbu_ref[:, col] += jnp.sum(da, axis=0, keepdims=True)
                dbu_ref[:, colg] += jnp.sum(dgate, axis=0, keepdims=True)
            return carry

        lax.fori_loop(0, d // cw, chunk, 0)

        @pl.when(i == nt - 1)
        def _():
            for k in range(CONV_W):
                dw_ref[k:k + 1, :] = jnp.sum(wacc[8 * k:8 * k + 8, :], axis=0, keepdims=True)
            dw_ref[CONV_W:HALO, :] = jnp.zeros((HALO - CONV_W, d), F32)

    row = pl.BlockSpec((tt, d), lambda i: (i, 0))
    return pl.pallas_call(
        body, name=name, grid=(nt,),
        in_specs=[row, pl.BlockSpec((HALO, d), lambda i: (jnp.minimum((i + 1) * hb, t // HALO - 1), 0)),
                  row, pl.BlockSpec((HALO, d), lambda i: (jnp.maximum(i * hb - 1, 0), 0)),
                  row, row, pl.BlockSpec((HALO, d), lambda i: (0, 0))],
        out_specs=[pl.BlockSpec((tt, 2 * d), lambda i: (i, 0)),
                   pl.BlockSpec((HALO, d), lambda i: (0, 0)),
                   pl.BlockSpec((1, 2 * d), lambda i: (0, 0))],
        out_shape=[jax.ShapeDtypeStruct((t, 2 * d), BF16), jax.ShapeDtypeStruct((HALO, d), F32),
                   jax.ShapeDtypeStruct((1, 2 * d), F32)],
        scratch_shapes=[pltpu.VMEM((tt + HALO, d), F32), pltpu.VMEM((HALO + tt, d), F32),
                        pltpu.VMEM((8 * HALO, d), F32)],
        compiler_params=_cp(dimension_semantics=("arbitrary",)),
    )(dc, dc, glu, glu, ua, ug, w_dw)


def _mla_mid_fwd(name, down, qg, kvg, ck, sk):
    t, w = down.shape
    rq, rkv = qg.shape[-1], kvg.shape[-1]
    tm = _tile(t, 512)

    def body(dn_ref, qg_ref, kvg_ref, ck_ref, sk_ref, qn_ref, kvn_ref, kpe_ref):
        cq = dn_ref[:, 0:rq]
        ckv = dn_ref[:, rq:rq + rkv]
        pe = dn_ref[:, rq + rkv:rq + rkv + LANE]
        qn_ref[...] = (cq * lax.rsqrt(jnp.mean(cq * cq, axis=-1, keepdims=True) + NORM_EPS)
                       * qg_ref[...]).astype(BF16)
        kvn_ref[...] = (ckv * lax.rsqrt(jnp.mean(ckv * ckv, axis=-1, keepdims=True) + NORM_EPS)
                        * kvg_ref[...]).astype(BF16)
        kpe_ref[...] = _rope(pe, ck_ref[...], sk_ref[...]).astype(BF16)

    def row(n):
        return pl.BlockSpec((tm, n), lambda i: (i, 0))

    def vec(n):
        return pl.BlockSpec((1, n), lambda i: (0, 0))

    return pl.pallas_call(
        body, name=name, grid=(t // tm,),
        in_specs=[row(w), vec(rq), vec(rkv), row(LANE), row(LANE)],
        out_specs=[row(rq), row(rkv), row(LANE)],
        out_shape=[jax.ShapeDtypeStruct((t, rq), BF16), jax.ShapeDtypeStruct((t, rkv), BF16),
                   jax.ShapeDtypeStruct((t, LANE), BF16)],
        compiler_params=_cp(),
    )(down, qg, kvg, ck, sk)


def _mla_mid_bwd(name, down, qg, kvg, dqn, dkvn, dkpe):
    t, w = down.shape
    rq, rkv = qg.shape[-1], kvg.shape[-1]
    tm = _tile(t, 256)

    def body(dn_ref, qg_ref, kvg_ref, dqn_ref, dkvn_ref, dkpe_ref, dd_ref, dqg_ref, dkvg_ref):
        @pl.when(pl.program_id(0) == 0)
        def _():
            dqg_ref[...] = jnp.zeros_like(dqg_ref)
            dkvg_ref[...] = jnp.zeros_like(dkvg_ref)

        dcq, dqg = _rms_bwd_math(dn_ref[:, 0:rq], qg_ref[...], dqn_ref[...])
        dckv, dkvg = _rms_bwd_math(dn_ref[:, rq:rq + rkv], kvg_ref[...], dkvn_ref[...])
        dd_ref[:, 0:rq] = dcq.astype(BF16)
        dd_ref[:, rq:rq + rkv] = dckv.astype(BF16)
        dd_ref[:, rq + rkv:rq + rkv + LANE] = dkpe_ref[...].astype(BF16)
        dqg_ref[...] += dqg
        dkvg_ref[...] += dkvg

    def row(n):
        return pl.BlockSpec((tm, n), lambda i: (i, 0))

    def vec(n):
        return pl.BlockSpec((1, n), lambda i: (0, 0))

    return pl.pallas_call(
        body, name=name, grid=(t // tm,),
        in_specs=[row(w), vec(rq), vec(rkv), row(rq), row(rkv), row(LANE)],
        out_specs=[row(w), vec(rq), vec(rkv)],
        out_shape=[jax.ShapeDtypeStruct((t, w), BF16), jax.ShapeDtypeStruct((1, rq), F32),
                   jax.ShapeDtypeStruct((1, rkv), F32)],
        compiler_params=_cp(dimension_semantics=("arbitrary",)),
    )(down, qg, kvg, dqn, dkvn, dkpe)


ATT_TILE = 512
_NT = (((1,), (1,)), ((), ()))
_TN = (((0,), (0,)), ((), ()))


def _flash_fwd(name, qf, kf, v, heads):
    s = qf.shape[0]
    t = _tile(s, ATT_TILE)
    n = s // t

    def body(q_ref, k_ref, v_ref, o_ref, lse_ref, m_sc, l_sc, acc_sc):
        i, j = pl.program_id(1), pl.program_id(2)

        @pl.when(j == 0)
        def _():
            m_sc[...] = jnp.full(m_sc.shape, -jnp.inf, F32)
            l_sc[...] = jnp.zeros_like(l_sc)
            acc_sc[...] = jnp.zeros_like(acc_sc)

        def step(diag):
            sc = lax.dot_general(q_ref[...], k_ref[...], _NT, preferred_element_type=F32)
            if diag:
                sc = jnp.where(_chunk_mask(t), sc, -jnp.inf)
            m_old = m_sc[...]
            m_new = jnp.maximum(m_old, jnp.max(sc, axis=1, keepdims=True))
            alpha = jnp.exp(m_old - m_new)
            p = jnp.exp(sc - m_new)
            l_sc[...] = alpha * l_sc[...] + jnp.sum(p, axis=1, keepdims=True)
            acc_sc[...] = alpha * acc_sc[...] + jnp.dot(p.astype(BF16), v_ref[...],
                                                        preferred_element_type=F32)
            m_sc[...] = m_new

        @pl.when(j < i)
        def _():
            step(False)

        @pl.when(j == i)
        def _():
            step(True)
            l = l_sc[...]
            o_ref[...] = (acc_sc[...] / l).astype(BF16)
            lse_ref[...] = m_sc[...] + jnp.log(l)

    return pl.pallas_call(
        body, name=name, grid=(heads, n, n),
        in_specs=[pl.BlockSpec((t, HEAD_QK_PAD), lambda h, i, j: (i, h)),
                  pl.BlockSpec((t, HEAD_QK_PAD), lambda h, i, j: (jnp.minimum(j, i), h)),
                  pl.BlockSpec((t, HEAD_V), lambda h, i, j: (jnp.minimum(j, i), h))],
        out_specs=[pl.BlockSpec((t, HEAD_V), lambda h, i, j: (i, h)),
                   pl.BlockSpec((None, t, 1), lambda h, i, j: (h, i, 0))],
        out_shape=[jax.ShapeDtypeStruct((s, heads * HEAD_V), BF16),
                   jax.ShapeDtypeStruct((heads, s, 1), F32)],
        scratch_shapes=[pltpu.VMEM((t, 1), F32), pltpu.VMEM((t, 1), F32), pltpu.VMEM((t, HEAD_V), F32)],
        compiler_params=_cp(dimension_semantics=("arbitrary", "arbitrary", "arbitrary")),
    )(qf, kf, v)


def _flash_bwd_dq(name, qf, kf, v, do, o, lse, cq, sq, heads):
    s = qf.shape[0]
    t = _tile(s, ATT_TILE)
    n = s // t

    def body(q_ref, k_ref, v_ref, do_ref, o_ref, lse_ref, c_ref, s_ref, dq_ref, dl_ref, acc_sc):
        i, j = pl.program_id(1), pl.program_id(2)

        @pl.when(j == 0)
        def _():
            acc_sc[...] = jnp.zeros_like(acc_sc)
            dl_ref[...] = jnp.sum(do_ref[...].astype(F32) * o_ref[...].astype(F32), axis=1, keepdims=True)

        def step(diag):
            sc = lax.dot_general(q_ref[...], k_ref[...], _NT, preferred_element_type=F32)
            p = jnp.exp(sc - lse_ref[...])
            if diag:
                p = jnp.where(_chunk_mask(t), p, 0.0)
            dp = lax.dot_general(do_ref[...], v_ref[...], _NT, preferred_element_type=F32)
            ds = (p * (dp - dl_ref[...])).astype(BF16)
            acc_sc[...] += jnp.dot(ds, k_ref[...], preferred_element_type=F32)

        @pl.when(j < i)
        def _():
            step(False)

        @pl.when(j == i)
        def _():
            step(True)
            dq_ref[...] = _rope_t(acc_sc[...], c_ref[...], s_ref[...]).astype(BF16)

    qspec = pl.BlockSpec((t, HEAD_QK_PAD), lambda h, i, j: (i, h))
    ospec = pl.BlockSpec((t, HEAD_V), lambda h, i, j: (i, h))
    vspec = pl.BlockSpec((None, t, 1), lambda h, i, j: (h, i, 0))
    tab = pl.BlockSpec((t, HEAD_QK_PAD), lambda h, i, j: (i, 0))
    return pl.pallas_call(
        body, name=name, grid=(heads, n, n),
        in_specs=[qspec,
                  pl.BlockSpec((t, HEAD_QK_PAD), lambda h, i, j: (jnp.minimum(j, i), h)),
                  pl.BlockSpec((t, HEAD_V), lambda h, i, j: (jnp.minimum(j, i), h)),
                  ospec, ospec, vspec, tab, tab],
        out_specs=[qspec, vspec],
        out_shape=[jax.ShapeDtypeStruct(qf.shape, BF16), jax.ShapeDtypeStruct((heads, s, 1), F32)],
        scratch_shapes=[pltpu.VMEM((t, HEAD_QK_PAD), F32)],
        compiler_params=_cp(dimension_semantics=("arbitrary", "arbitrary", "arbitrary")),
    )(qf, kf, v, do, o, lse, cq, sq)


def _flash_bwd_dkv(name, qf, kf, v, do, lse, delta, ck, sk, heads):
    s = qf.shape[0]
    t = _tile(s, ATT_TILE)
    n = s // t

    def body(q_ref, k_ref, v_ref, do_ref, lse_ref, dl_ref, c_ref, s_ref, dkv_ref, dpe_ref, dk_sc, dv_sc):
        j, h, i = pl.program_id(0), pl.program_id(1), pl.program_id(2)

        @pl.when(i == 0)
        def _():
            dk_sc[...] = jnp.zeros_like(dk_sc)
            dv_sc[...] = jnp.zeros_like(dv_sc)

        def step(diag):
            sc = lax.dot_general(q_ref[...], k_ref[...], _NT, preferred_element_type=F32)
            p = jnp.exp(sc - lse_ref[...])
            if diag:
                p = jnp.where(_chunk_mask(t), p, 0.0)
            dv_sc[...] += lax.dot_general(p.astype(BF16), do_ref[...], _TN, preferred_element_type=F32)
            dp = lax.dot_general(do_ref[...], v_ref[...], _NT, preferred_element_type=F32)
            ds = (p * (dp - dl_ref[...])).astype(BF16)
            dk_sc[...] += lax.dot_general(ds, q_ref[...], _TN, preferred_element_type=F32)

        @pl.when(i > j)
        def _():
            step(False)

        @pl.when(i == j)
        def _():
            step(True)

        @pl.when(i == n - 1)
        def _():
            dk = dk_sc[...]
            dkv_ref[...] = jnp.concatenate([dk[:, 0:HEAD_NOPE], dv_sc[...]], axis=1).astype(BF16)
            pe = dk[:, HEAD_NOPE:HEAD_QK_PAD]

            @pl.when(h == 0)
            def _():
                dpe_ref[...] = pe

            @pl.when(h > 0)
            def _():
                dpe_ref[...] += pe

            @pl.when(h == heads - 1)
            def _():
                dpe_ref[...] = _rope_t(dpe_ref[...], c_ref[...], s_ref[...])

    qrow = lambda j, h, i: (jnp.maximum(i, j), h)
    return pl.pallas_call(
        body, name=name, grid=(n, heads, n),
        in_specs=[pl.BlockSpec((t, HEAD_QK_PAD), qrow),
                  pl.BlockSpec((t, HEAD_QK_PAD), lambda j, h, i: (j, h)),
                  pl.BlockSpec((t, HEAD_V), lambda j, h, i: (j, h)),
                  pl.BlockSpec((t, HEAD_V), qrow),
                  pl.BlockSpec((None, t, 1), lambda j, h, i: (h, jnp.maximum(i, j), 0)),
                  pl.BlockSpec((None, t, 1), lambda j, h, i: (h, jnp.maximum(i, j), 0)),
                  pl.BlockSpec((t, LANE), lambda j, h, i: (j, 0)),
                  pl.BlockSpec((t, LANE), lambda j, h, i: (j, 0))],
        out_specs=[pl.BlockSpec((t, HEAD_NOPE + HEAD_V), lambda j, h, i: (j, h)),
                   pl.BlockSpec((t, LANE), lambda j, h, i: (j, 0))],
        out_shape=[jax.ShapeDtypeStruct((s, heads * (HEAD_NOPE + HEAD_V)), BF16),
                   jax.ShapeDtypeStruct((s, LANE), F32)],
        scratch_shapes=[pltpu.VMEM((t, HEAD_QK_PAD), F32), pltpu.VMEM((t, HEAD_V), F32)],
        compiler_params=_cp(dimension_semantics=("arbitrary", "arbitrary", "arbitrary")),
    )(qf, kf, v, do, lse, delta, ck, sk)


def _adamw(name, parts, w, m, v):
    p, r, c = parts.shape
    tr = _tile(r, max(8, (256 * 1024) // max(c, 1)))
    bc1 = 1.0 - ADAM_B1 ** ADAM_STEP
    bc2 = 1.0 - ADAM_B2 ** ADAM_STEP

    def body(p_ref, w_ref, m_ref, v_ref, g_ref, d_ref, nm_ref, nv_ref):
        g = p_ref[0].astype(F32)
        for q in range(1, p):
            g = g + p_ref[q].astype(F32)
        nm = ADAM_B1 * m_ref[...] + (1.0 - ADAM_B1) * g
        nv = ADAM_B2 * v_ref[...] + (1.0 - ADAM_B2) * (g * g)
        g_ref[...] = g
        nm_ref[...] = nm
        nv_ref[...] = nv
        d_ref[...] = -ADAM_LR * ((nm / bc1) / (jnp.sqrt(nv / bc2) + ADAM_EPS) + ADAM_WD * w_ref[...])

    blk = pl.BlockSpec((tr, c), lambda i: (i, 0))
    sh = jax.ShapeDtypeStruct((r, c), F32)
    return pl.pallas_call(
        body, name=name, grid=(r // tr,),
        in_specs=[pl.BlockSpec((p, tr, c), lambda i: (0, i, 0)), blk, blk, blk],
        out_specs=[blk] * 4, out_shape=[sh] * 4,
        compiler_params=_cp(),
    )(parts, w, m, v)


def _my_place():
    x, y, c = lax.axis_index("x"), lax.axis_index("y"), lax.axis_index("c")
    return x, y, c


def _flip(v, bit):
    return 1 - v if bit else v


def _block(ref, axis, idx, size):
    return ref.at[(slice(None),) * axis + (pl.ds(idx * size, size),)]


def _all_gather_big(name, shards, axes):
    nt = len(shards)
    sizes = [sh.shape[ax] for sh, ax in zip(shards, axes)]
    out_shape = [jax.ShapeDtypeStruct(sh.shape[:ax] + (N_DEV * sh.shape[ax],) + sh.shape[ax + 1:], sh.dtype)
                 for sh, ax in zip(shards, axes)]

    def body(*refs):
        x_refs, o_refs = refs[:nt], refs[nt:2 * nt]
        send_sems, recv_sems, local_sems = refs[2 * nt:]
        x, y, c = _my_place()
        me, sibling = (x, y, c), (x, y, 1 - c)
        chips = [(1 - x, y), (x, 1 - y), (1 - x, 1 - y)]

        def rows(t, px, py, pc):
            return _block(o_refs[t], axes[t], 4 * px + 2 * py + pc, sizes[t])

        def copy(t, k, block, to, src=None):
            return pltpu.make_async_remote_copy(
                src_ref=rows(t, *block) if src is None else src, dst_ref=rows(t, *block),
                send_sem=send_sems.at[7 * t + k], recv_sem=recv_sems.at[7 * t + k],
                device_id=to, device_id_type=MESH)

        local, sent = [], []
        for t in range(nt):
            mine = pltpu.make_async_copy(x_refs[t], rows(t, *me), local_sems.at[t])
            mine.start()
            local.append(mine)
            for j, chip in enumerate(chips):
                cp = copy(t, 1 + j, me, (*chip, c), src=x_refs[t])
                cp.start()
                sent.append(cp)
            cp = copy(t, 0, me, sibling, src=x_refs[t])
            cp.start()
            sent.append(cp)
        for t in range(nt):
            for j, chip in enumerate(chips):
                copy(t, 1 + j, (*chip, c), me).wait_recv()
                cp = copy(t, 4 + j, (*chip, c), sibling)
                cp.start()
                sent.append(cp)
        for t in range(nt):
            copy(t, 0, sibling, me).wait_recv()
            for j, chip in enumerate(chips):
                copy(t, 4 + j, (*chip, 1 - c), me).wait_recv()
        for cp in sent:
            cp.wait_send()
        for cp in local:
            cp.wait()

    return pl.pallas_call(
        body, name=name, in_specs=[ANY] * nt, out_specs=[ANY] * nt, out_shape=out_shape,
        scratch_shapes=[pltpu.SemaphoreType.DMA((7 * nt,)), pltpu.SemaphoreType.DMA((7 * nt,)),
                        pltpu.SemaphoreType.DMA((nt,))],
        compiler_params=_cp(has_side_effects=True),
    )(*shards)


def _exchange_grads(name, grads, axes):
    nt = len(grads)
    sizes = [g.shape[ax] // N_DEV for g, ax in zip(grads, axes)]
    out_shape = [jax.ShapeDtypeStruct((N_DEV,) + g.shape[:ax] + (g.shape[ax] // N_DEV,) + g.shape[ax + 1:],
                                      g.dtype) for g, ax in zip(grads, axes)]

    def body(*refs):
        g_refs, r_refs = refs[:nt], refs[nt:2 * nt]
        send_sems, recv_sems, local_sems = refs[2 * nt:]
        x, y, c = _my_place()
        copies = []
        for t in range(nt):
            own = pltpu.make_async_copy(_block(g_refs[t], axes[t], 4 * x + 2 * y + c, sizes[t]),
                                        r_refs[t].at[0], local_sems.at[t])
            own.start()
            copies.append(own)
        for k in range(1, N_DEV):
            px, py, pc = _flip(x, k & 4), _flip(y, k & 2), _flip(c, k & 1)
            for t in range(nt):
                cp = pltpu.make_async_remote_copy(
                    src_ref=_block(g_refs[t], axes[t], 4 * px + 2 * py + pc, sizes[t]),
                    dst_ref=r_refs[t].at[k],
                    send_sem=send_sems.at[7 * t + k - 1], recv_sem=recv_sems.at[7 * t + k - 1],
                    device_id=(px, py, pc), device_id_type=MESH)
                cp.start()
                copies.append(cp)
        for cp in copies:
            cp.wait()

    return pl.pallas_call(
        body, name=name, in_specs=[ANY] * nt, out_specs=[ANY] * nt, out_shape=out_shape,
        scratch_shapes=[pltpu.SemaphoreType.DMA((7 * nt,)), pltpu.SemaphoreType.DMA((7 * nt,)),
                        pltpu.SemaphoreType.DMA((nt,))],
        compiler_params=_cp(has_side_effects=True),
    )(*grads)


def _all_gather_small(name, vec, reduce):
    r = vec.shape[0]

    def body(v_ref, o_ref, *rest):
        if reduce:
            buf, send_sems, recv_sems = rest
        else:
            buf = o_ref
            send_sems, recv_sems = rest
        x, y, c = _my_place()
        mine = 4 * x + 2 * y + c
        buf[mine] = v_ref[...]
        copies = []
        for k in range(1, N_DEV):
            px, py, pc = _flip(x, k & 4), _flip(y, k & 2), _flip(c, k & 1)
            cp = pltpu.make_async_remote_copy(
                src_ref=v_ref, dst_ref=buf.at[mine], send_sem=send_sems.at[k - 1],
                recv_sem=recv_sems.at[k - 1], device_id=(px, py, pc), device_id_type=MESH)
            cp.start()
            copies.append(cp)
        for cp in copies:
            cp.wait()
        if reduce:
            acc = buf[0]
            for q in range(1, N_DEV):
                acc = acc + buf[q]
            o_ref[...] = acc

    scratch = [pltpu.SemaphoreType.DMA((N_DEV - 1,)), pltpu.SemaphoreType.DMA((N_DEV - 1,))]
    if reduce:
        scratch = [pltpu.VMEM((N_DEV, r, LANE), F32)] + scratch
        out_shape = jax.ShapeDtypeStruct((r, LANE), F32)
    else:
        out_shape = jax.ShapeDtypeStruct((N_DEV, r, LANE), F32)
    return pl.pallas_call(
        body, name=name, in_specs=[VMEM_SPEC], out_specs=VMEM_SPEC, out_shape=out_shape,
        scratch_shapes=scratch, compiler_params=_cp(has_side_effects=True),
    )(vec)


def _pack(arrs, row_mult=8):
    flat = jnp.concatenate([a.reshape(-1).astype(F32) for a in arrs])
    n = flat.shape[0]
    rows = -(-n // LANE)
    rows = -(-rows // row_mult) * row_mult
    return jnp.pad(flat, (0, rows * LANE - n)).reshape(rows, LANE)


def _unpack(vec, shapes):
    flat = vec.reshape(-1)
    out, pos = [], 0
    for sh in shapes:
        n = 1
        for s in sh:
            n *= s
        out.append(flat[pos:pos + n].reshape(sh))
        pos += n
    return out


BIG = ["conv_w_pw1", "conv_w_pw2", "mla_w_in", "mla_w_q_up", "mla_w_kv_up", "mla_w_o", "mlp_w1", "mlp_w2"]
BIG_AXIS = {"conv_w_pw1": 2, "conv_w_pw2": 1, "mla_w_in": 1, "mla_w_q_up": 2, "mla_w_kv_up": 2,
            "mla_w_o": 1, "mlp_w1": 2, "mlp_w2": 1}
SMALL_SHARDED = ["conv_w_dw", "mla_q_norm_g", "mla_kv_norm_g"]
REPLICATED = ["norm_mixer_g", "norm_mlp_g", "conv_b_pw1", "conv_b_dw", "conv_ln_g", "conv_ln_b",
              "conv_b_pw2", "final_norm_g"]
WEIGHTS = ["norm_mixer_g", "norm_mlp_g", "conv_w_pw1", "conv_b_pw1", "conv_w_dw", "conv_b_dw",
           "conv_ln_g", "conv_ln_b", "conv_w_pw2", "conv_b_pw2", "mla_w_in", "mla_q_norm_g",
           "mla_kv_norm_g", "mla_w_q_up", "mla_w_kv_up", "mla_w_o", "mlp_w1", "mlp_w2", "final_norm_g"]


def _unshard_last(g, lead):
    nd = g.ndim
    perm = tuple(range(1, nd - 1)) + (0, nd - 1)
    return g.transpose(perm).reshape(lead + (N_DEV * g.shape[-1],))


def _step(w, m, v, x, positions, target):
    s, d = x.shape
    depth = w["norm_mixer_g"].shape[0]
    n_conv, n_mla = w["conv_w_pw1"].shape[0], w["mla_w_in"].shape[0]
    heads = (w["mla_w_q_up"].shape[-1] * N_DEV) // (HEAD_NOPE + HEAD_ROPE)
    rq, rkv = w["mla_w_q_up"].shape[1], w["mla_w_kv_up"].shape[1]
    xi, yi, ci = _my_place()
    mine = 4 * xi + 2 * yi + ci

    full = dict(zip(BIG, _all_gather_big("gather_weights", [w[n].astype(BF16) for n in BIG],
                                          [BIG_AXIS[n] for n in BIG])))
    small_shapes = [w[n].shape for n in SMALL_SHARDED]
    gathered = _all_gather_small("gather_small", _pack([w[n] for n in SMALL_SHARDED]), False)
    per_dev = [_unpack(gathered[q], small_shapes) for q in range(N_DEV)]
    w_dw = _unshard_last(jnp.stack([p[0] for p in per_dev]), (n_conv, CONV_W))
    q_gain = _unshard_last(jnp.stack([p[1] for p in per_dev]), (n_mla,))
    kv_gain = _unshard_last(jnp.stack([p[2] for p in per_dev]), (n_mla,))
    w_dw_pad = jnp.pad(w_dw, ((0, 0), (0, HALO - CONV_W), (0, 0)))

    w_in_cols = rq + rkv + HEAD_ROPE
    w_in_pad = jnp.pad(full["mla_w_in"], ((0, 0), (0, 0), (0, rq + rkv + LANE - w_in_cols)))
    wq_pad = jnp.pad(full["mla_w_q_up"].reshape(n_mla, rq, heads, HEAD_NOPE + HEAD_ROPE),
                     ((0, 0), (0, 0), (0, 0), (0, HEAD_QK_PAD - HEAD_NOPE - HEAD_ROPE))
                     ).reshape(n_mla, rq, heads * HEAD_QK_PAD)

    inv_freq = ROPE_THETA ** (-jnp.arange(0, HEAD_ROPE, 2, dtype=F32) / HEAD_ROPE)
    ang = positions.reshape(s).astype(F32)[:, None] * inv_freq
    cos, sin = jnp.cos(ang), jnp.sin(ang)
    c64 = jnp.concatenate([cos, cos], axis=1)
    s64 = jnp.concatenate([-sin, sin], axis=1)
    zeros64 = jnp.zeros((s, LANE - HEAD_ROPE), F32)
    ck = jnp.concatenate([c64, zeros64], axis=1)
    sk = jnp.concatenate([s64, zeros64], axis=1)
    scale = (HEAD_NOPE + HEAD_ROPE) ** -0.5
    cq = scale * jnp.concatenate([jnp.ones((s, HEAD_NOPE), F32), ck], axis=1)
    sq = scale * jnp.concatenate([jnp.zeros((s, HEAD_NOPE), F32), sk], axis=1)

    def vec(a):
        return a.reshape(1, -1)

    saved = []
    for layer in range(depth):
        jl = layer // 2
        h = _rms_fwd(f"rms_mixer_{layer}", x, vec(w["norm_mixer_g"][layer]))
        if layer % 2 == 0:
            ua, ug, glu = _mm_glu(f"conv_pw1_{layer}", h, full["conv_w_pw1"], jl, vec(w["conv_b_pw1"][jl]))
            cc, sw = _conv_fwd(f"conv_dw_{layer}", glu, w_dw_pad[jl], vec(w["conv_b_dw"][jl]),
                               vec(w["conv_ln_g"][jl]), vec(w["conv_ln_b"][jl]))
            x1 = _mm_res(f"conv_pw2_{layer}", sw, full["conv_w_pw2"], jl, x, vec(w["conv_b_pw2"][jl]))
            mix = (h, ua, ug, glu, cc, sw)
        else:
            down = _mm_plain(f"mla_down_{layer}", h, w_in_pad, jl, "nn", F32)
            qn, kvn, kpe = _mla_mid_fwd(f"mla_mid_{layer}", down, vec(q_gain[jl]), vec(kv_gain[jl]), ck, sk)
            qf = _mm_q(f"mla_q_{layer}", qn, wq_pad, jl, cq, sq)
            kf, vv = _mm_kv(f"mla_kv_{layer}", kvn, full["mla_w_kv_up"], jl, kpe)
            o, lse = _flash_fwd(f"mla_attn_{layer}", qf, kf, vv, heads)
            x1 = _mm_res(f"mla_out_{layer}", o, full["mla_w_o"], jl, x)
            mix = (h, down, qn, kvn, qf, kf, vv, o, lse)
        h2 = _rms_fwd(f"rms_mlp_{layer}", x1, vec(w["norm_mlp_g"][layer]))
        z, a = _mm_mlp_up(f"mlp_up_{layer}", h2, full["mlp_w1"], layer)
        x2 = _mm_res(f"mlp_down_{layer}", a, full["mlp_w2"], layer, x1)
        saved.append((x, mix, x1, h2, z, a))
        x = x2

    loss_row, g, gb, d_final, _ = _final_loss("final_loss", x, vec(w["final_norm_g"]), target)

    dbuf = {n: lax.empty(full[n].shape, BF16) for n in BIG}
    dbuf["mla_w_in"] = lax.empty(w_in_pad.shape, BF16)
    dbuf["mla_w_q_up"] = lax.empty(wq_pad.shape, BF16)
    d_mixer, d_mlp = [None] * depth, [None] * depth
    d_small = {n: [None] * n_conv for n in ["conv_b_pw1", "conv_w_dw", "conv_b_dw", "conv_ln_g",
                                           "conv_ln_b", "conv_b_pw2"]}
    d_qg, d_kvg = [None] * n_mla, [None] * n_mla
    for layer in reversed(range(depth)):
        jl = layer // 2
        x0, mix, x1, h2, z, a = saved[layer]
        colsum_g = None
        dz = _mm_mlp_dz(f"mlp_dz_{layer}", gb, full["mlp_w2"], layer, z)
        dbuf["mlp_w2"] = _mm_wgrad(f"mlp_dw2_{layer}", a, gb, dbuf["mlp_w2"], layer)
        dh2 = _mm_plain(f"mlp_dh_{layer}", dz, full["mlp_w1"], layer, "nt", F32)
        dbuf["mlp_w1"] = _mm_wgrad(f"mlp_dw1_{layer}", h2, dz, dbuf["mlp_w1"], layer)
        g, gb, d_mlp[layer], colsum_g = _rms_bwd(f"rms_mlp_bwd_{layer}", x1, vec(w["norm_mlp_g"][layer]),
                                                 dh2, g)
        if layer % 2 == 0:
            h, ua, ug, glu, cc, sw = mix
            d_small["conv_b_pw2"][jl] = colsum_g.reshape(-1)
            dsw = _mm_plain(f"conv_ds_{layer}", gb, full["conv_w_pw2"], jl, "nt", F32)
            dbuf["conv_w_pw2"] = _mm_wgrad(f"conv_dw2_{layer}", sw, gb, dbuf["conv_w_pw2"], jl)
            dc, dlg, dlb, dbdw = _conv_bwd_ln(f"conv_ln_bwd_{layer}", dsw, cc, vec(w["conv_ln_g"][jl]),
                                              vec(w["conv_ln_b"][jl]))
            du, dwdw, dbu = _conv_bwd_dw(f"conv_dw_bwd_{layer}", dc, glu, ua, ug, w_dw_pad[jl])
            d_small["conv_ln_g"][jl] = dlg.reshape(-1)
            d_small["conv_ln_b"][jl] = dlb.reshape(-1)
            d_small["conv_b_dw"][jl] = dbdw.reshape(-1)
            d_small["conv_w_dw"][jl] = dwdw[:CONV_W]
            d_small["conv_b_pw1"][jl] = dbu.reshape(-1)
            dh = _mm_plain(f"conv_dh_{layer}", du, full["conv_w_pw1"], jl, "nt", F32)
            dbuf["conv_w_pw1"] = _mm_wgrad(f"conv_dw1_{layer}", h, du, dbuf["conv_w_pw1"], jl)
        else:
            h, down, qn, kvn, qf, kf, vv, o, lse = mix
            do = _mm_plain(f"mla_do_{layer}", gb, full["mla_w_o"], jl, "nt", BF16)
            dbuf["mla_w_o"] = _mm_wgrad(f"mla_dwo_{layer}", o, gb, dbuf["mla_w_o"], jl)
            dq, delta = _flash_bwd_dq(f"mla_attn_dq_{layer}", qf, kf, vv, do, o, lse, cq, sq, heads)
            dkv, dkpe = _flash_bwd_dkv(f"mla_attn_dkv_{layer}", qf, kf, vv, do, lse, delta, ck, sk, heads)
            dqn = _mm_plain(f"mla_dqn_{layer}", dq, wq_pad, jl, "nt", F32)
            dbuf["mla_w_q_up"] = _mm_wgrad(f"mla_dwq_{layer}", qn, dq, dbuf["mla_w_q_up"], jl)
            dkvn = _mm_plain(f"mla_dkvn_{layer}", dkv, full["mla_w_kv_up"], jl, "nt", F32)
            dbuf["mla_w_kv_up"] = _mm_wgrad(f"mla_dwkv_{layer}", kvn, dkv, dbuf["mla_w_kv_up"], jl)
            ddown, d_qg[jl], d_kvg[jl] = _mla_mid_bwd(f"mla_mid_bwd_{layer}", down, vec(q_gain[jl]),
                                                      vec(kv_gain[jl]), dqn, dkvn, dkpe)
            dh = _mm_plain(f"mla_dh_{layer}", ddown, w_in_pad, jl, "nt", F32)
            dbuf["mla_w_in"] = _mm_wgrad(f"mla_dwin_{layer}", h, ddown, dbuf["mla_w_in"], jl)
        g, gb, d_mixer[layer], _ = _rms_bwd(f"rms_mixer_bwd_{layer}", x0, vec(w["norm_mixer_g"][layer]), dh, g)
    grad_x = g

    dbuf["mla_w_in"] = dbuf["mla_w_in"][:, :, :w_in_cols]
    dbuf["mla_w_q_up"] = dbuf["mla_w_q_up"].reshape(n_mla, rq, heads, HEAD_QK_PAD)[
        :, :, :, :HEAD_NOPE + HEAD_ROPE].reshape(n_mla, rq, heads * (HEAD_NOPE + HEAD_ROPE))
    parts = dict(zip(BIG, _exchange_grads("exchange_grads", [dbuf[n] for n in BIG],
                                          [BIG_AXIS[n] for n in BIG])))
    out = {}
    for n in BIG:
        sh = w[n].shape
        r, c = sh[0] * sh[1], sh[2]
        res = _adamw(f"adamw_{n}", parts[n].reshape(N_DEV, r, c), w[n].reshape(r, c),
                     m[n].reshape(r, c), v[n].reshape(r, c))
        out[n] = [t.reshape(sh) for t in res]

    small_full = {
        "norm_mixer_g": jnp.concatenate(d_mixer, axis=0), "norm_mlp_g": jnp.concatenate(d_mlp, axis=0),
        "conv_b_pw1": jnp.stack(d_small["conv_b_pw1"]), "conv_b_dw": jnp.stack(d_small["conv_b_dw"]),
        "conv_ln_g": jnp.stack(d_small["conv_ln_g"]), "conv_ln_b": jnp.stack(d_small["conv_ln_b"]),
        "conv_b_pw2": jnp.stack(d_small["conv_b_pw2"]), "final_norm_g": d_final.reshape(-1),
        "conv_w_dw": jnp.stack(d_small["conv_w_dw"]),
        "mla_q_norm_g": jnp.concatenate(d_qg, axis=0), "mla_kv_norm_g": jnp.concatenate(d_kvg, axis=0),
    }
    names = REPLICATED + SMALL_SHARDED
    summed = _unpack(_all_gather_small("reduce_small", _pack([small_full[n] for n in names]), True),
                     [small_full[n].shape for n in names])
    summed = dict(zip(names, summed))
    for n in SMALL_SHARDED:
        width = w[n].shape[-1]
        summed[n] = lax.dynamic_slice_in_dim(summed[n], mine * width, width, axis=summed[n].ndim - 1)
    for group, tag in ((REPLICATED, "replicated"), (SMALL_SHARDED, "small_sharded")):
        shapes = [w[n].shape for n in group]
        res = _adamw(f"adamw_{tag}", _pack([summed[n] for n in group])[None],
                     _pack([w[n] for n in group]), _pack([m[n] for n in group]), _pack([v[n] for n in group]))
        unpacked = [_unpack(t, shapes) for t in res]
        for q, n in enumerate(group):
            out[n] = [unpacked[0][q], unpacked[1][q], unpacked[2][q], unpacked[3][q]]

    loss = lax.psum(loss_row[0, 0], ("x", "y", "c"))
    return loss, grad_x, out


def kernel(x, positions, norm_mixer_g, norm_mlp_g, conv_w_pw1, conv_b_pw1, conv_w_dw, conv_b_dw, conv_ln_g, conv_ln_b, conv_w_pw2, conv_b_pw2, mla_w_in, mla_q_norm_g, mla_kv_norm_g, mla_w_q_up, mla_w_kv_up, mla_w_o, mlp_w1, mlp_w2, final_norm_g, loss_target, m_norm_mixer_g, m_norm_mlp_g, m_conv_w_pw1, m_conv_b_pw1, m_conv_w_dw, m_conv_b_dw, m_conv_ln_g, m_conv_ln_b, m_conv_w_pw2, m_conv_b_pw2, m_mla_w_in, m_mla_q_norm_g, m_mla_kv_norm_g, m_mla_w_q_up, m_mla_w_kv_up, m_mla_w_o, m_mlp_w1, m_mlp_w2, m_final_norm_g, v_norm_mixer_g, v_norm_mlp_g, v_conv_w_pw1, v_conv_b_pw1, v_conv_w_dw, v_conv_b_dw, v_conv_ln_g, v_conv_ln_b, v_conv_w_pw2, v_conv_b_pw2, v_mla_w_in, v_mla_q_norm_g, v_mla_kv_norm_g, v_mla_w_q_up, v_mla_w_kv_up, v_mla_w_o, v_mlp_w1, v_mlp_w2, v_final_norm_g):
    ws = (norm_mixer_g, norm_mlp_g, conv_w_pw1, conv_b_pw1, conv_w_dw, conv_b_dw, conv_ln_g, conv_ln_b,
          conv_w_pw2, conv_b_pw2, mla_w_in, mla_q_norm_g, mla_kv_norm_g, mla_w_q_up, mla_w_kv_up, mla_w_o,
          mlp_w1, mlp_w2, final_norm_g)
    ms = (m_norm_mixer_g, m_norm_mlp_g, m_conv_w_pw1, m_conv_b_pw1, m_conv_w_dw, m_conv_b_dw, m_conv_ln_g,
          m_conv_ln_b, m_conv_w_pw2, m_conv_b_pw2, m_mla_w_in, m_mla_q_norm_g, m_mla_kv_norm_g,
          m_mla_w_q_up, m_mla_w_kv_up, m_mla_w_o, m_mlp_w1, m_mlp_w2, m_final_norm_g)
    vs = (v_norm_mixer_g, v_norm_mlp_g, v_conv_w_pw1, v_conv_b_pw1, v_conv_w_dw, v_conv_b_dw, v_conv_ln_g,
          v_conv_ln_b, v_conv_w_pw2, v_conv_b_pw2, v_mla_w_in, v_mla_q_norm_g, v_mla_kv_norm_g,
          v_mla_w_q_up, v_mla_w_kv_up, v_mla_w_o, v_mlp_w1, v_mlp_w2, v_final_norm_g)
    w, m, v = dict(zip(WEIGHTS, ws)), dict(zip(WEIGHTS, ms)), dict(zip(WEIGHTS, vs))
    s, d = x.shape[-2], x.shape[-1]
    loss, grad_x, out = _step(w, m, v, x.reshape(s, d), positions, loss_target.reshape(s, d))
    grads = [out[n][0] for n in WEIGHTS]
    deltas = [out[n][1] for n in WEIGHTS]
    new_m = [out[n][2] for n in WEIGHTS]
    new_v = [out[n][3] for n in WEIGHTS]
    return (loss, grad_x.reshape(x.shape), *grads, *deltas, *new_m, *new_v)
```

```python
import functools

import jax
import jax.numpy as jnp
from jax import lax
from jax.experimental import pallas as pl
from jax.experimental.pallas import tpu as pltpu

F32 = jnp.float32
BF16 = jnp.bfloat16

NORM_EPS = 1e-6
LN_EPS = 1e-5
ROPE_THETA = 10000.0
CHUNK_BITS = 6
HEAD_NOPE = 128
HEAD_ROPE = 64
HEAD_V = 128
HEAD_QK_PAD = 256
CONV_W = 31
HALO = 32
N_DEV = 8

ADAM_LR = 0.001
ADAM_B1 = 0.9
ADAM_B2 = 0.999
ADAM_EPS = 1e-08
ADAM_WD = 0.01
ADAM_STEP = 10

V7X_VMEM_BYTES = 64 * 1024 * 1024
VMEM_LIMIT = (V7X_VMEM_BYTES * 3) // 4
LANE = 128

MESH = pl.DeviceIdType.MESH
ANY = pl.BlockSpec(memory_space=pl.ANY)
VMEM_SPEC = pl.BlockSpec(memory_space=pltpu.VMEM)


def _cp(**kw):
    return pltpu.CompilerParams(vmem_limit_bytes=VMEM_LIMIT, **kw)


SUBLANE_BF16 = 16

TM_PREF = 1024
TN_PREF = 1024
TK_PREF = 2048


def _tile(n, pref, mult=SUBLANE_BF16):
    if n <= pref + pref // 2:
        return n
    t = (pref // mult) * mult
    while t >= mult:
        if n % t == 0:
            return t
        t -= mult
    return n


def _sigmoid(x):
    return 1.0 / (1.0 + jnp.exp(-x))


def _rot_half(x):
    n = x.shape[-1]
    lane = lax.broadcasted_iota(jnp.int32, x.shape, x.ndim - 1)
    first = (lane & 63) < 32
    return jnp.where(first, pltpu.roll(x, n - 32, x.ndim - 1), pltpu.roll(x, 32, x.ndim - 1))


def _rope(x, c, s):
    return x * c + _rot_half(x) * s


def _rope_t(d, c, s):
    return d * c + _rot_half(d * s)


def _chunk_mask(t):
    row = lax.broadcasted_iota(jnp.int32, (t, t), 0)
    col = lax.broadcasted_iota(jnp.int32, (t, t), 1)
    return jnp.right_shift(col, CHUNK_BITS) <= jnp.right_shift(row, CHUNK_BITS)


def _rms_fwd(name, x, g):
    t, d = x.shape
    tm = _tile(t, 512)

    def body(x_ref, g_ref, o_ref):
        xf = x_ref[...]
        r = lax.rsqrt(jnp.mean(xf * xf, axis=-1, keepdims=True) + NORM_EPS)
        o_ref[...] = (xf * r * g_ref[...]).astype(o_ref.dtype)

    return pl.pallas_call(
        body, name=name, grid=(t // tm,),
        in_specs=[pl.BlockSpec((tm, d), lambda i: (i, 0)), pl.BlockSpec((1, d), lambda i: (0, 0))],
        out_specs=pl.BlockSpec((tm, d), lambda i: (i, 0)),
        out_shape=jax.ShapeDtypeStruct((t, d), BF16),
        compiler_params=_cp(),
    )(x, g)


def _rms_bwd_math(xf, g, dy):
    r = lax.rsqrt(jnp.mean(xf * xf, axis=-1, keepdims=True) + NORM_EPS)
    xh = xf * r
    dg = jnp.sum(dy * xh, axis=0, keepdims=True)
    dxh = dy * g
    dx = r * (dxh - xh * jnp.mean(dxh * xh, axis=-1, keepdims=True))
    return dx, dg


def _rms_bwd(name, x, g, dy, resid):
    t, d = x.shape
    tm = _tile(t, 256)

    def body(x_ref, g_ref, dy_ref, r_ref, dx_ref, dxb_ref, dg_ref, cs_ref):
        @pl.when(pl.program_id(0) == 0)
        def _():
            dg_ref[...] = jnp.zeros_like(dg_ref)
            cs_ref[...] = jnp.zeros_like(cs_ref)

        dx, dg = _rms_bwd_math(x_ref[...], g_ref[...], dy_ref[...])
        tot = r_ref[...] + dx
        dx_ref[...] = tot
        dxb_ref[...] = tot.astype(BF16)
        dg_ref[...] += dg
        cs_ref[...] += jnp.sum(tot, axis=0, keepdims=True)

    row = pl.BlockSpec((tm, d), lambda i: (i, 0))
    vec = pl.BlockSpec((1, d), lambda i: (0, 0))
    return pl.pallas_call(
        body, name=name, grid=(t // tm,),
        in_specs=[row, vec, row, row],
        out_specs=[row, row, vec, vec],
        out_shape=[jax.ShapeDtypeStruct((t, d), F32), jax.ShapeDtypeStruct((t, d), BF16),
                   jax.ShapeDtypeStruct((1, d), F32), jax.ShapeDtypeStruct((1, d), F32)],
        compiler_params=_cp(dimension_semantics=("arbitrary",)),
    )(x, g, dy, resid)


def _final_loss(name, x, g, target):
    t, d = x.shape
    tm = _tile(t, 256)

    def body(x_ref, g_ref, t_ref, loss_ref, dx_ref, dxb_ref, dg_ref, cs_ref):
        @pl.when(pl.program_id(0) == 0)
        def _():
            loss_ref[...] = jnp.zeros_like(loss_ref)
            dg_ref[...] = jnp.zeros_like(dg_ref)
            cs_ref[...] = jnp.zeros_like(cs_ref)

        xf = x_ref[...]
        gg = g_ref[...]
        r = lax.rsqrt(jnp.mean(xf * xf, axis=-1, keepdims=True) + NORM_EPS)
        err = xf * r * gg - t_ref[...]
        part = 0.5 * jnp.sum(jnp.mean(err * err, axis=-1, keepdims=True), axis=0, keepdims=True)
        loss_ref[...] += jnp.broadcast_to(part, loss_ref.shape)
        dx, dg = _rms_bwd_math(xf, gg, err * (1.0 / d))
        dx_ref[...] = dx
        dxb_ref[...] = dx.astype(BF16)
        dg_ref[...] += dg
        cs_ref[...] += jnp.sum(dx, axis=0, keepdims=True)

    row = pl.BlockSpec((tm, d), lambda i: (i, 0))
    vec = pl.BlockSpec((1, d), lambda i: (0, 0))
    return pl.pallas_call(
        body, name=name, grid=(t // tm,),
        in_specs=[row, vec, row],
        out_specs=[pl.BlockSpec((1, LANE), lambda i: (0, 0)), row, row, vec, vec],
        out_shape=[jax.ShapeDtypeStruct((1, LANE), F32), jax.ShapeDtypeStruct((t, d), F32),
                   jax.ShapeDtypeStruct((t, d), BF16), jax.ShapeDtypeStruct((1, d), F32),
                   jax.ShapeDtypeStruct((1, d), F32)],
        compiler_params=_cp(dimension_semantics=("arbitrary",)),
    )(x, g, target)


_DIMS = {
    "nn": (((1,), (0,)), ((), ())),
    "nt": (((1,), (1,)), ((), ())),
    "tn": (((0,), (0,)), ((), ())),
}


def _mm(name, a, bs, *, mode, m, n, k, epilogue, out_shape, out_specs, extras=(), extra_specs=(),
        a_lead=None, aliases=None, tn_div=1):
    tm, tn, tk = _tiles(m, n, k, tn_div)
    nk = k // tk
    nb, ne = len(bs), len(extras)
    no = len(out_shape)
    dims = _DIMS[mode]

    def with_lead(shape, idx, lead):
        if lead is None:
            return pl.BlockSpec(shape, idx)
        return pl.BlockSpec((None,) + shape, lambda i, j, kk: (lead,) + idx(i, j, kk))

    if mode == "tn":
        a_spec = with_lead((tk, tm), lambda i, j, kk: (kk, i), a_lead)
    else:
        a_spec = with_lead((tm, tk), lambda i, j, kk: (i, kk), a_lead)
    b_specs = []
    for _, lead, off in bs:
        if mode == "nt":
            b_specs.append(with_lead((tn, tk), lambda i, j, kk, off=off: (j + off, kk), lead))
        else:
            b_specs.append(with_lead((tk, tn), lambda i, j, kk, off=off: (kk, j + off), lead))

    def body(*refs):
        a_ref = refs[0]
        b_refs = refs[1:1 + nb]
        ex = refs[1 + nb:1 + nb + ne]
        outs = refs[1 + nb + ne:1 + nb + ne + no]
        accs = refs[1 + nb + ne + no:]

        def part(b_ref):
            return lax.dot_general(a_ref[...], b_ref[...], dims, preferred_element_type=F32)

        if nk == 1:
            epilogue([part(b_ref) for b_ref in b_refs], ex, outs)
            return
        kk = pl.program_id(2)

        @pl.when(kk == 0)
        def _():
            for acc, b_ref in zip(accs, b_refs):
                acc[...] = part(b_ref)

        @pl.when(kk > 0)
        def _():
            for acc, b_ref in zip(accs, b_refs):
                acc[...] += part(b_ref)

        @pl.when(kk == nk - 1)
        def _():
            epilogue([acc[...] for acc in accs], ex, outs)

    scratch = [pltpu.VMEM((tm, tn), F32) for _ in range(nb)] if nk > 1 else []
    return pl.pallas_call(
        body, name=name, grid=(m // tm, n // tn, nk),
        in_specs=[a_spec] + b_specs + list(extra_specs),
        out_specs=list(out_specs), out_shape=list(out_shape), scratch_shapes=scratch,
        input_output_aliases=aliases or {},
        compiler_params=_cp(dimension_semantics=("arbitrary", "arbitrary", "arbitrary")),
    )(a, *[b for b, _, _ in bs], *extras), (tm, tn, tk)


def _ij(tm, tn):
    return pl.BlockSpec((tm, tn), lambda i, j, kk: (i, j))


def _tiles(m, n, k, tn_div=1):
    return _tile(m, TM_PREF), _tile(n, TN_PREF // tn_div, LANE), _tile(k, TK_PREF, LANE)


def _mm_plain(name, a, b, b_lead, mode, out_dtype):
    m, k = a.shape
    n = b.shape[-1] if mode == "nn" else b.shape[-2]
    tm, tn, _ = _tiles(m, n, k)

    def epilogue(accs, ex, outs):
        outs[0][...] = accs[0].astype(out_dtype)

    return _mm(name, a, [(b, b_lead, 0)], mode=mode, m=m, n=n, k=k, epilogue=epilogue,
               out_shape=[jax.ShapeDtypeStruct((m, n), out_dtype)], out_specs=[_ij(tm, tn)])[0][0]


def _mm_res(name, a, b, b_lead, resid, bias=None):
    m, k = a.shape
    n = b.shape[-1]
    tm, tn, _ = _tiles(m, n, k)
    extras, specs = [resid], [_ij(tm, tn)]
    if bias is not None:
        extras.append(bias)
        specs.append(pl.BlockSpec((1, tn), lambda i, j, kk: (0, j)))

    def epilogue(accs, ex, outs):
        y = ex[0][...] + accs[0]
        if bias is not None:
            y = y + ex[1][...]
        outs[0][...] = y

    return _mm(name, a, [(b, b_lead, 0)], mode="nn", m=m, n=n, k=k, epilogue=epilogue,
               extras=extras, extra_specs=specs,
               out_shape=[jax.ShapeDtypeStruct((m, n), F32)], out_specs=[_ij(tm, tn)])[0][0]


def _mm_mlp_up(name, h, w1, lead):
    m, k = h.shape
    n = w1.shape[-1]
    tm, tn, _ = _tiles(m, n, k)

    def epilogue(accs, ex, outs):
        z = accs[0]
        outs[0][...] = z.astype(BF16)
        r = jnp.maximum(z, 0.0)
        outs[1][...] = (r * r).astype(BF16)

    sh = jax.ShapeDtypeStruct((m, n), BF16)
    return _mm(name, h, [(w1, lead, 0)], mode="nn", m=m, n=n, k=k, epilogue=epilogue,
               out_shape=[sh, sh], out_specs=[_ij(tm, tn), _ij(tm, tn)])[0]


def _mm_mlp_dz(name, g, w2, lead, z):
    m, k = g.shape
    n = w2.shape[-2]
    tm, tn, _ = _tiles(m, n, k)

    def epilogue(accs, ex, outs):
        outs[0][...] = (accs[0] * (2.0 * jnp.maximum(ex[0][...].astype(F32), 0.0))).astype(BF16)

    return _mm(name, g, [(w2, lead, 0)], mode="nt", m=m, n=n, k=k, epilogue=epilogue,
               extras=[z], extra_specs=[_ij(tm, tn)],
               out_shape=[jax.ShapeDtypeStruct((m, n), BF16)], out_specs=[_ij(tm, tn)])[0][0]


def _mm_glu(name, h, w, lead, bias):
    m, k = h.shape
    n = w.shape[-1] // 2
    tm, tn, _ = _tiles(m, n, k, 2)
    off = n // tn

    def epilogue(accs, ex, outs):
        a = accs[0] + ex[0][...]
        gate = accs[1] + ex[1][...]
        outs[0][...] = a.astype(BF16)
        outs[1][...] = gate.astype(BF16)
        outs[2][...] = a * _sigmoid(gate)

    shb = jax.ShapeDtypeStruct((m, n), BF16)
    return _mm(name, h, [(w, lead, 0), (w, lead, off)], mode="nn", m=m, n=n, k=k, epilogue=epilogue,
               extras=[bias, bias],
               extra_specs=[pl.BlockSpec((1, tn), lambda i, j, kk: (0, j)),
                            pl.BlockSpec((1, tn), lambda i, j, kk: (0, j + off))],
               out_shape=[shb, shb, jax.ShapeDtypeStruct((m, n), F32)],
               out_specs=[_ij(tm, tn)] * 3, tn_div=2)[0]


def _mm_q(name, qn, wq_pad, lead, cq, sq):
    m, k = qn.shape
    n = wq_pad.shape[-1]
    tm, tn, _ = _tiles(m, n, k)
    rep = tn // HEAD_QK_PAD

    def epilogue(accs, ex, outs):
        c = jnp.tile(ex[0][...], (1, rep))
        s = jnp.tile(ex[1][...], (1, rep))
        outs[0][...] = _rope(accs[0], c, s).astype(BF16)

    tab = pl.BlockSpec((tm, HEAD_QK_PAD), lambda i, j, kk: (i, 0))
    return _mm(name, qn, [(wq_pad, lead, 0)], mode="nn", m=m, n=n, k=k, epilogue=epilogue,
               extras=[cq, sq], extra_specs=[tab, tab],
               out_shape=[jax.ShapeDtypeStruct((m, n), BF16)], out_specs=[_ij(tm, tn)])[0][0]


def _mm_kv(name, kvn, wkv, lead, kpe):
    m, k = kvn.shape
    n = wkv.shape[-1]
    tm, tn, _ = _tiles(m, n, k)
    heads = tn // (HEAD_NOPE + HEAD_V)

    def epilogue(accs, ex, outs):
        acc = accs[0]
        pe = ex[0][...].astype(F32)
        kparts, vparts = [], []
        for hh in range(heads):
            base = hh * (HEAD_NOPE + HEAD_V)
            kparts += [acc[:, base:base + HEAD_NOPE], pe]
            vparts.append(acc[:, base + HEAD_NOPE:base + HEAD_NOPE + HEAD_V])
        outs[0][...] = jnp.concatenate(kparts, axis=1).astype(BF16)
        outs[1][...] = jnp.concatenate(vparts, axis=1).astype(BF16) if heads > 1 else vparts[0].astype(BF16)

    return _mm(name, kvn, [(wkv, lead, 0)], mode="nn", m=m, n=n, k=k, epilogue=epilogue,
               extras=[kpe], extra_specs=[pl.BlockSpec((tm, LANE), lambda i, j, kk: (i, 0))],
               out_shape=[jax.ShapeDtypeStruct((m, n), BF16),
                          jax.ShapeDtypeStruct((m, n // 2), BF16)],
               out_specs=[_ij(tm, heads * HEAD_QK_PAD), _ij(tm, tn // 2)])[0]


def _mm_wgrad(name, a, b):
    t, m = a.shape
    n = b.shape[-1]
    tm, tn, _ = _tiles(m, n, t)

    def epilogue(accs, ex, outs):
        outs[0][...] = accs[0].astype(BF16)

    return _mm(name, a, [(b, None, 0)], mode="tn", m=m, n=n, k=t, epilogue=epilogue,
               out_shape=[jax.ShapeDtypeStruct((m, n), BF16)], out_specs=[_ij(tm, tn)])[0][0]


CONV_ROWS = 256
CONV_RT = 64
CONV_CW = 256
CONV_LR = 32


def _ln_stats(c):
    mu = jnp.mean(c, axis=-1, keepdims=True)
    xc = c - mu
    rstd = lax.rsqrt(jnp.mean(xc * xc, axis=-1, keepdims=True) + LN_EPS)
    return xc * rstd, rstd


def _conv_fwd(name, glu, w_dw, b_dw, ln_g, ln_b):
    t, d = glu.shape
    tt = _tile(t, CONV_ROWS)
    rt, cw, lr = min(CONV_RT, tt), min(CONV_CW, d), min(CONV_LR, tt)
    hb = tt // HALO

    def body(gc_ref, gp_ref, w_ref, b_ref, lg_ref, lb_ref, c_ref, s_ref, buf):
        i = pl.program_id(0)
        buf[0:HALO, :] = jnp.where(i > 0, gp_ref[...], 0.0)
        buf[HALO:HALO + tt, :] = gc_ref[...]

        def chunk(cb, carry):
            col = pl.ds(pl.multiple_of(cb * cw, cw), cw)
            for r0 in range(0, tt, rt):
                acc = jnp.broadcast_to(b_ref[:, col], (rt, cw))
                for k in range(CONV_W):
                    lo = r0 + HALO - (CONV_W - 1) + k
                    acc = acc + w_ref[k:k + 1, col] * buf[lo:lo + rt, col]
                c_ref[r0:r0 + rt, col] = acc
            return carry

        lax.fori_loop(0, d // cw, chunk, 0)

        def ln(r, carry):
            rows = pl.ds(pl.multiple_of(r * lr, lr), lr)
            xh, _ = _ln_stats(c_ref[rows, :])
            y = xh * lg_ref[...] + lb_ref[...]
            s_ref[rows, :] = (y * _sigmoid(y)).astype(BF16)
            return carry

        lax.fori_loop(0, tt // lr, ln, 0)

    row = pl.BlockSpec((tt, d), lambda i: (i, 0))
    vec = pl.BlockSpec((1, d), lambda i: (0, 0))
    return pl.pallas_call(
        body, name=name, grid=(t // tt,),
        in_specs=[row, pl.BlockSpec((HALO, d), lambda i: (jnp.maximum(i * hb - 1, 0), 0)),
                  pl.BlockSpec((HALO, d), lambda i: (0, 0)), vec, vec, vec],
        out_specs=[row, row],
        out_shape=[jax.ShapeDtypeStruct((t, d), F32), jax.ShapeDtypeStruct((t, d), BF16)],
        scratch_shapes=[pltpu.VMEM((HALO + tt, d), F32)],
        compiler_params=_cp(dimension_semantics=("arbitrary",)),
    )(glu, glu, w_dw, b_dw, ln_g, ln_b)


def _conv_bwd_ln(name, ds, c, ln_g, ln_b):
    t, d = c.shape
    tt = _tile(t, CONV_ROWS)
    lr = min(CONV_LR, tt)

    def body(ds_ref, c_ref, lg_ref, lb_ref, dc_ref, dg_ref, db_ref, dbdw_ref):
        @pl.when(pl.program_id(0) == 0)
        def _():
            dg_ref[...] = jnp.zeros_like(dg_ref)
            db_ref[...] = jnp.zeros_like(db_ref)
            dbdw_ref[...] = jnp.zeros_like(dbdw_ref)

        def chunk(r, carry):
            rows = pl.ds(pl.multiple_of(r * lr, lr), lr)
            xh, rstd = _ln_stats(c_ref[rows, :])
            g = lg_ref[...]
            y = xh * g + lb_ref[...]
            sg = _sigmoid(y)
            dy = ds_ref[rows, :] * (sg * (1.0 + y * (1.0 - sg)))
            dxh = dy * g
            dc = rstd * (dxh - jnp.mean(dxh, axis=-1, keepdims=True)
                         - xh * jnp.mean(dxh * xh, axis=-1, keepdims=True))
            dc_ref[rows, :] = dc
            dg_ref[...] += jnp.sum(dy * xh, axis=0, keepdims=True)
            db_ref[...] += jnp.sum(dy, axis=0, keepdims=True)
            dbdw_ref[...] += jnp.sum(dc, axis=0, keepdims=True)
            return carry

        lax.fori_loop(0, tt // lr, chunk, 0)

    row = pl.BlockSpec((tt, d), lambda i: (i, 0))
    vec = pl.BlockSpec((1, d), lambda i: (0, 0))
    vsh = jax.ShapeDtypeStruct((1, d), F32)
    return pl.pallas_call(
        body, name=name, grid=(t // tt,),
        in_specs=[row, row, vec, vec], out_specs=[row, vec, vec, vec],
        out_shape=[jax.ShapeDtypeStruct((t, d), F32), vsh, vsh, vsh],
        compiler_params=_cp(dimension_semantics=("arbitrary",)),
    )(ds, c, ln_g, ln_b)


def _conv_bwd_dw(name, dc, glu, ua, ug, w_dw):
    t, d = dc.shape
    tt = _tile(t, CONV_ROWS)
    rt, cw = min(CONV_RT, tt), min(CONV_CW, d)
    hb = tt // HALO
    nt = t // tt

    def body(dcc_ref, dcn_ref, gc_ref, gp_ref, ua_ref, ug_ref, w_ref,
             du_ref, dw_ref, dbu_ref, dbuf, gbuf, wacc):
        i = pl.program_id(0)

        @pl.when(i == 0)
        def _():
            wacc[...] = jnp.zeros_like(wacc)
            dbu_ref[...] = jnp.zeros_like(dbu_ref)

        dbuf[0:tt, :] = dcc_ref[...]
        dbuf[tt:tt + HALO, :] = jnp.where(i < nt - 1, dcn_ref[...], 0.0)
        gbuf[0:HALO, :] = jnp.where(i > 0, gp_ref[...], 0.0)
        gbuf[HALO:HALO + tt, :] = gc_ref[...]

        def chunk(cb, carry):
            c0 = pl.multiple_of(cb * cw, cw)
            col = pl.ds(c0, cw)
            colg = pl.ds(pl.multiple_of(d + cb * cw, cw), cw)
            for r0 in range(0, tt, rt):
                dcr = dbuf[r0:r0 + rt, col]
                dgl = jnp.zeros((rt, cw), F32)
                for k in range(CONV_W):
                    hi = r0 + (CONV_W - 1) - k
                    dgl = dgl + w_ref[k:k + 1, col] * dbuf[hi:hi + rt, col]
                    lo = r0 + HALO - (CONV_W - 1) + k
                    prod = dcr * gbuf[lo:lo + rt, col]
                    part = prod[0:8, :]
                    for r in range(8, rt, 8):
                        part = part + prod[r:r + 8, :]
                    wacc[8 * k:8 * k + 8, col] += part
                a = ua_ref[r0:r0 + rt, col].astype(F32)
                sg = _sigmoid(ug_ref[r0:r0 + rt, col].astype(F32))
                da = dgl * sg
                dgate = dgl * a * sg * (1.0 - sg)
                du_ref[r0:r0 + rt, col] = da.astype(BF16)
                du_ref[r0:r0 + rt, colg] = dgate.astype(BF16)
                dbu_ref[:, col] += jnp.sum(da, axis=0, keepdims=True)
                dbu_ref[:, colg] += jnp.sum(dgate, axis=0, keepdims=True)
            return carry

        lax.fori_loop(0, d // cw, chunk, 0)

        @pl.when(i == nt - 1)
        def _():
            for k in range(CONV_W):
                dw_ref[k:k + 1, :] = jnp.sum(wacc[8 * k:8 * k + 8, :], axis=0, keepdims=True)
            dw_ref[CONV_W:HALO, :] = jnp.zeros((HALO - CONV_W, d), F32)

    row = pl.BlockSpec((tt, d), lambda i: (i, 0))
    return pl.pallas_call(
        body, name=name, grid=(nt,),
        in_specs=[row, pl.BlockSpec((HALO, d), lambda i: (jnp.minimum((i + 1) * hb, t // HALO - 1), 0)),
                  row, pl.BlockSpec((HALO, d), lambda i: (jnp.maximum(i * hb - 1, 0), 0)),
                  row, row, pl.BlockSpec((HALO, d), lambda i: (0, 0))],
        out_specs=[pl.BlockSpec((tt, 2 * d), lambda i: (i, 0)),
                   pl.BlockSpec((HALO, d), lambda i: (0, 0)),
                   pl.BlockSpec((1, 2 * d), lambda i: (0, 0))],
        out_shape=[jax.ShapeDtypeStruct((t, 2 * d), BF16), jax.ShapeDtypeStruct((HALO, d), F32),
                   jax.ShapeDtypeStruct((1, 2 * d), F32)],
        scratch_shapes=[pltpu.VMEM((tt + HALO, d), F32), pltpu.VMEM((HALO + tt, d), F32),
                        pltpu.VMEM((8 * HALO, d), F32)],
        compiler_params=_cp(dimension_semantics=("arbitrary",)),
    )(dc, dc, glu, glu, ua, ug, w_dw)


def _mla_mid_fwd(name, down, qg, kvg, ck, sk):
    t, w = down.shape
    rq, rkv = qg.shape[-1], kvg.shape[-1]
    tm = _tile(t, 512)

    def body(dn_ref, qg_ref, kvg_ref, ck_ref, sk_ref, qn_ref, kvn_ref, kpe_ref):
        cq = dn_ref[:, 0:rq]
        ckv = dn_ref[:, rq:rq + rkv]
        pe = dn_ref[:, rq + rkv:rq + rkv + LANE]
        qn_ref[...] = (cq * lax.rsqrt(jnp.mean(cq * cq, axis=-1, keepdims=True) + NORM_EPS)
                       * qg_ref[...]).astype(BF16)
        kvn_ref[...] = (ckv * lax.rsqrt(jnp.mean(ckv * ckv, axis=-1, keepdims=True) + NORM_EPS)
                        * kvg_ref[...]).astype(BF16)
        kpe_ref[...] = _rope(pe, ck_ref[...], sk_ref[...]).astype(BF16)

    def row(n):
        return pl.BlockSpec((tm, n), lambda i: (i, 0))

    def vec(n):
        return pl.BlockSpec((1, n), lambda i: (0, 0))

    return pl.pallas_call(
        body, name=name, grid=(t // tm,),
        in_specs=[row(w), vec(rq), vec(rkv), row(LANE), row(LANE)],
        out_specs=[row(rq), row(rkv), row(LANE)],
        out_shape=[jax.ShapeDtypeStruct((t, rq), BF16), jax.ShapeDtypeStruct((t, rkv), BF16),
                   jax.ShapeDtypeStruct((t, LANE), BF16)],
        compiler_params=_cp(),
    )(down, qg, kvg, ck, sk)


def _mla_mid_bwd(name, down, qg, kvg, dqn, dkvn, dkpe):
    t, w = down.shape
    rq, rkv = qg.shape[-1], kvg.shape[-1]
    tm = _tile(t, 256)

    def body(dn_ref, qg_ref, kvg_ref, dqn_ref, dkvn_ref, dkpe_ref, dd_ref, dqg_ref, dkvg_ref):
        @pl.when(pl.program_id(0) == 0)
        def _():
            dqg_ref[...] = jnp.zeros_like(dqg_ref)
            dkvg_ref[...] = jnp.zeros_like(dkvg_ref)

        dcq, dqg = _rms_bwd_math(dn_ref[:, 0:rq], qg_ref[...], dqn_ref[...])
        dckv, dkvg = _rms_bwd_math(dn_ref[:, rq:rq + rkv], kvg_ref[...], dkvn_ref[...])
        dd_ref[:, 0:rq] = dcq.astype(BF16)
        dd_ref[:, rq:rq + rkv] = dckv.astype(BF16)
        dd_ref[:, rq + rkv:rq + rkv + LANE] = dkpe_ref[...].astype(BF16)
        dqg_ref[...] += dqg
        dkvg_ref[...] += dkvg

    def row(n):
        return pl.BlockSpec((tm, n), lambda i: (i, 0))

    def vec(n):
        return pl.BlockSpec((1, n), lambda i: (0, 0))

    return pl.pallas_call(
        body, name=name, grid=(t // tm,),
        in_specs=[row(w), vec(rq), vec(rkv), row(rq), row(rkv), row(LANE)],
        out_specs=[row(w), vec(rq), vec(rkv)],
        out_shape=[jax.ShapeDtypeStruct((t, w), BF16), jax.ShapeDtypeStruct((1, rq), F32),
                   jax.ShapeDtypeStruct((1, rkv), F32)],
        compiler_params=_cp(dimension_semantics=("arbitrary",)),
    )(down, qg, kvg, dqn, dkvn, dkpe)


ATT_TILE = 512
_NT = (((1,), (1,)), ((), ()))
_TN = (((0,), (0,)), ((), ()))


def _flash_fwd(name, qf, kf, v, heads):
    s = qf.shape[0]
    t = _tile(s, ATT_TILE)
    n = s // t

    def body(q_ref, k_ref, v_ref, o_ref, lse_ref, m_sc, l_sc, acc_sc):
        i, j = pl.program_id(1), pl.program_id(2)

        @pl.when(j == 0)
        def _():
            m_sc[...] = jnp.full(m_sc.shape, -jnp.inf, F32)
            l_sc[...] = jnp.zeros_like(l_sc)
            acc_sc[...] = jnp.zeros_like(acc_sc)

        def step(diag):
            sc = lax.dot_general(q_ref[...], k_ref[...], _NT, preferred_element_type=F32)
            if diag:
                sc = jnp.where(_chunk_mask(t), sc, -jnp.inf)
            m_old = m_sc[...]
            m_new = jnp.maximum(m_old, jnp.max(sc, axis=1, keepdims=True))
            alpha = jnp.exp(m_old - m_new)
            p = jnp.exp(sc - m_new)
            l_sc[...] = alpha * l_sc[...] + jnp.sum(p, axis=1, keepdims=True)
            acc_sc[...] = alpha * acc_sc[...] + jnp.dot(p.astype(BF16), v_ref[...],
                                                        preferred_element_type=F32)
            m_sc[...] = m_new

        @pl.when(j < i)
        def _():
            step(False)

        @pl.when(j == i)
        def _():
            step(True)
            l = l_sc[...]
            o_ref[...] = (acc_sc[...] / l).astype(BF16)
            lse_ref[...] = m_sc[...] + jnp.log(l)

    return pl.pallas_call(
        body, name=name, grid=(heads, n, n),
        in_specs=[pl.BlockSpec((t, HEAD_QK_PAD), lambda h, i, j: (i, h)),
                  pl.BlockSpec((t, HEAD_QK_PAD), lambda h, i, j: (jnp.minimum(j, i), h)),
                  pl.BlockSpec((t, HEAD_V), lambda h, i, j: (jnp.minimum(j, i), h))],
        out_specs=[pl.BlockSpec((t, HEAD_V), lambda h, i, j: (i, h)),
                   pl.BlockSpec((None, t, 1), lambda h, i, j: (h, i, 0))],
        out_shape=[jax.ShapeDtypeStruct((s, heads * HEAD_V), BF16),
                   jax.ShapeDtypeStruct((heads, s, 1), F32)],
        scratch_shapes=[pltpu.VMEM((t, 1), F32), pltpu.VMEM((t, 1), F32), pltpu.VMEM((t, HEAD_V), F32)],
        compiler_params=_cp(dimension_semantics=("arbitrary", "arbitrary", "arbitrary")),
    )(qf, kf, v)


def _flash_bwd_dq(name, qf, kf, v, do, o, lse, cq, sq, heads):
    s = qf.shape[0]
    t = _tile(s, ATT_TILE)
    n = s // t

    def body(q_ref, k_ref, v_ref, do_ref, o_ref, lse_ref, c_ref, s_ref, dq_ref, dl_ref, acc_sc):
        i, j = pl.program_id(1), pl.program_id(2)

        @pl.when(j == 0)
        def _():
            acc_sc[...] = jnp.zeros_like(acc_sc)
            dl_ref[...] = jnp.sum(do_ref[...].astype(F32) * o_ref[...].astype(F32), axis=1, keepdims=True)

        def step(diag):
            sc = lax.dot_general(q_ref[...], k_ref[...], _NT, preferred_element_type=F32)
            p = jnp.exp(sc - lse_ref[...])
            if diag:
                p = jnp.where(_chunk_mask(t), p, 0.0)
            dp = lax.dot_general(do_ref[...], v_ref[...], _NT, preferred_element_type=F32)
            ds = (p * (dp - dl_ref[...])).astype(BF16)
            acc_sc[...] += jnp.dot(ds, k_ref[...], preferred_element_type=F32)

        @pl.when(j < i)
        def _():
            step(False)

        @pl.when(j == i)
        def _():
            step(True)
            dq_ref[...] = _rope_t(acc_sc[...], c_ref[...], s_ref[...]).astype(BF16)

    qspec = pl.BlockSpec((t, HEAD_QK_PAD), lambda h, i, j: (i, h))
    ospec = pl.BlockSpec((t, HEAD_V), lambda h, i, j: (i, h))
    vspec = pl.BlockSpec((None, t, 1), lambda h, i, j: (h, i, 0))
    tab = pl.BlockSpec((t, HEAD_QK_PAD), lambda h, i, j: (i, 0))
    return pl.pallas_call(
        body, name=name, grid=(heads, n, n),
        in_specs=[qspec,
                  pl.BlockSpec((t, HEAD_QK_PAD), lambda h, i, j: (jnp.minimum(j, i), h)),
                  pl.BlockSpec((t, HEAD_V), lambda h, i, j: (jnp.minimum(j, i), h)),
                  ospec, ospec, vspec, tab, tab],
        out_specs=[qspec, vspec],
        out_shape=[jax.ShapeDtypeStruct(qf.shape, BF16), jax.ShapeDtypeStruct((heads, s, 1), F32)],
        scratch_shapes=[pltpu.VMEM((t, HEAD_QK_PAD), F32)],
        compiler_params=_cp(dimension_semantics=("arbitrary", "arbitrary", "arbitrary")),
    )(qf, kf, v, do, o, lse, cq, sq)


def _flash_bwd_dkv(name, qf, kf, v, do, lse, delta, ck, sk, heads):
    s = qf.shape[0]
    t = _tile(s, ATT_TILE)
    n = s // t

    def body(q_ref, k_ref, v_ref, do_ref, lse_ref, dl_ref, c_ref, s_ref, dkv_ref, dpe_ref, dk_sc, dv_sc):
        j, h, i = pl.program_id(0), pl.program_id(1), pl.program_id(2)

        @pl.when(i == 0)
        def _():
            dk_sc[...] = jnp.zeros_like(dk_sc)
            dv_sc[...] = jnp.zeros_like(dv_sc)

        def step(diag):
            sc = lax.dot_general(q_ref[...], k_ref[...], _NT, preferred_element_type=F32)
            p = jnp.exp(sc - lse_ref[...])
            if diag:
                p = jnp.where(_chunk_mask(t), p, 0.0)
            dv_sc[...] += lax.dot_general(p.astype(BF16), do_ref[...], _TN, preferred_element_type=F32)
            dp = lax.dot_general(do_ref[...], v_ref[...], _NT, preferred_element_type=F32)
            ds = (p * (dp - dl_ref[...])).astype(BF16)
            dk_sc[...] += lax.dot_general(ds, q_ref[...], _TN, preferred_element_type=F32)

        @pl.when(i > j)
        def _():
            step(False)

        @pl.when(i == j)
        def _():
            step(True)

        @pl.when(i == n - 1)
        def _():
            dk = dk_sc[...]
            dkv_ref[...] = jnp.concatenate([dk[:, 0:HEAD_NOPE], dv_sc[...]], axis=1).astype(BF16)
            pe = dk[:, HEAD_NOPE:HEAD_QK_PAD]

            @pl.when(h == 0)
            def _():
                dpe_ref[...] = pe

            @pl.when(h > 0)
            def _():
                dpe_ref[...] += pe

            @pl.when(h == heads - 1)
            def _():
                dpe_ref[...] = _rope_t(dpe_ref[...], c_ref[...], s_ref[...])

    qrow = lambda j, h, i: (jnp.maximum(i, j), h)
    return pl.pallas_call(
        body, name=name, grid=(n, heads, n),
        in_specs=[pl.BlockSpec((t, HEAD_QK_PAD), qrow),
                  pl.BlockSpec((t, HEAD_QK_PAD), lambda j, h, i: (j, h)),
                  pl.BlockSpec((t, HEAD_V), lambda j, h, i: (j, h)),
                  pl.BlockSpec((t, HEAD_V), qrow),
                  pl.BlockSpec((None, t, 1), lambda j, h, i: (h, jnp.maximum(i, j), 0)),
                  pl.BlockSpec((None, t, 1), lambda j, h, i: (h, jnp.maximum(i, j), 0)),
                  pl.BlockSpec((t, LANE), lambda j, h, i: (j, 0)),
                  pl.BlockSpec((t, LANE), lambda j, h, i: (j, 0))],
        out_specs=[pl.BlockSpec((t, HEAD_NOPE + HEAD_V), lambda j, h, i: (j, h)),
                   pl.BlockSpec((t, LANE), lambda j, h, i: (j, 0))],
        out_shape=[jax.ShapeDtypeStruct((s, heads * (HEAD_NOPE + HEAD_V)), BF16),
                   jax.ShapeDtypeStruct((s, LANE), F32)],
        scratch_shapes=[pltpu.VMEM((t, HEAD_QK_PAD), F32), pltpu.VMEM((t, HEAD_V), F32)],
        compiler_params=_cp(dimension_semantics=("arbitrary", "arbitrary", "arbitrary")),
    )(qf, kf, v, do, lse, delta, ck, sk)


def _adamw(name, parts, w, m, v):
    p, r, c = parts.shape
    tr = _tile(r, max(8, (256 * 1024) // max(c, 1)))
    bc1 = 1.0 - ADAM_B1 ** ADAM_STEP
    bc2 = 1.0 - ADAM_B2 ** ADAM_STEP

    def body(p_ref, w_ref, m_ref, v_ref, g_ref, d_ref, nm_ref, nv_ref):
        g = p_ref[0].astype(F32)
        for q in range(1, p):
            g = g + p_ref[q].astype(F32)
        nm = ADAM_B1 * m_ref[...] + (1.0 - ADAM_B1) * g
        nv = ADAM_B2 * v_ref[...] + (1.0 - ADAM_B2) * (g * g)
        g_ref[...] = g
        nm_ref[...] = nm
        nv_ref[...] = nv
        d_ref[...] = -ADAM_LR * ((nm / bc1) / (jnp.sqrt(nv / bc2) + ADAM_EPS) + ADAM_WD * w_ref[...])

    blk = pl.BlockSpec((tr, c), lambda i: (i, 0))
    sh = jax.ShapeDtypeStruct((r, c), F32)
    return pl.pallas_call(
        body, name=name, grid=(r // tr,),
        in_specs=[pl.BlockSpec((p, tr, c), lambda i: (0, i, 0)), blk, blk, blk],
        out_specs=[blk] * 4, out_shape=[sh] * 4,
        compiler_params=_cp(),
    )(parts, w, m, v)


def _my_place():
    x, y, c = lax.axis_index("x"), lax.axis_index("y"), lax.axis_index("c")
    return x, y, c


def _flip(v, bit):
    return 1 - v if bit else v


def _block(ref, axis, idx, size):
    return ref.at[(slice(None),) * axis + (pl.ds(idx * size, size),)]


HBM_SPEC = pl.BlockSpec(memory_space=pltpu.HBM)
SEM_SPEC = pl.BlockSpec(memory_space=pltpu.SEMAPHORE)
DATAFLOW = pltpu.SideEffectType.DATAFLOW_SIDE_EFFECTING


def _hbm(a):
    return pltpu.with_memory_space_constraint(a, pltpu.HBM)


def _remote_copies(jobs, bufs, send_sems, recv_sems):
    return [pltpu.make_async_remote_copy(src_ref=src, dst_ref=dst, send_sem=send_sems.at[q],
                                         recv_sem=recv_sems.at[q], device_id=dev, device_id_type=MESH)
            for q, (src, dst, dev) in enumerate(jobs(bufs))]


def _split_start(name, bufs, jobs, n_jobs, after):
    nb = len(bufs)

    def body(*refs):
        send_sems, recv_sems = refs[nb + 1], refs[nb + 2]
        for cp in _remote_copies(jobs, refs[:nb], send_sems, recv_sems):
            cp.start()
        refs[-1][...] = jnp.zeros_like(refs[-1])

    outs = pl.pallas_call(
        body, name=name,
        out_shape=(pltpu.SemaphoreType.DMA((n_jobs,)), pltpu.SemaphoreType.DMA((n_jobs,)),
                   *[pltpu.HBM(b.shape, b.dtype) for b in bufs], jax.ShapeDtypeStruct((8, LANE), F32)),
        in_specs=[HBM_SPEC] * nb + [ANY],
        out_specs=(SEM_SPEC, SEM_SPEC, *[HBM_SPEC] * nb, VMEM_SPEC),
        input_output_aliases={q: 2 + q for q in range(nb)},
        compiler_params=pltpu.CompilerParams(has_side_effects=DATAFLOW),
    )(*[_hbm(b) for b in bufs], after)
    return outs[0], outs[1], list(outs[2:2 + nb]), outs[-1]


def _split_wait(name, bufs, send_sems, recv_sems, jobs, after):
    nb = len(bufs)

    def body(*refs):
        for cp in _remote_copies(jobs, refs[:nb], refs[nb], refs[nb + 1]):
            cp.wait_send()
            cp.wait_recv()

    outs = pl.pallas_call(
        body, name=name,
        out_shape=tuple(pltpu.HBM(b.shape, b.dtype) for b in bufs),
        in_specs=[HBM_SPEC] * nb + [SEM_SPEC, SEM_SPEC, ANY],
        out_specs=tuple([HBM_SPEC] * nb),
        input_output_aliases={q: q for q in range(nb)},
        compiler_params=pltpu.CompilerParams(has_side_effects=DATAFLOW),
    )(*bufs, send_sems, recv_sems, after)
    return list(outs)


def _place(name, srcs, lands, jobs):
    ns, nl = len(srcs), len(lands)

    def body(*refs):
        sems = refs[-1]
        copies = [pltpu.make_async_copy(src, dst, sems.at[q])
                  for q, (src, dst) in enumerate(jobs(refs[:ns], refs[ns:ns + nl]))]
        for cp in copies:
            cp.start()
        for cp in copies:
            cp.wait()

    return list(pl.pallas_call(
        body, name=name, in_specs=[ANY] * (ns + nl), out_specs=[ANY] * nl,
        out_shape=[jax.ShapeDtypeStruct(b.shape, b.dtype) for b in lands],
        input_output_aliases={ns + q: q for q in range(nl)},
        scratch_shapes=[pltpu.SemaphoreType.DMA((ns,))],
        compiler_params=_cp(has_side_effects=True),
    )(*srcs, *lands))


def _gather_place_jobs(axes, sizes):
    def jobs(srcs, lands):
        x, y, c = _my_place()
        return [(srcs[t], _block(lands[t], axes[t], 4 * x + 2 * y + c, sizes[t])) for t in range(len(srcs))]
    return jobs


def _gather_jobs_a(axes, sizes):
    nt = len(axes)

    def jobs(bufs):
        x, y, c = _my_place()
        out = []
        for t in range(nt):
            dst = _block(bufs[nt + t], axes[t], 4 * x + 2 * y + c, sizes[t])
            for dev in [(x, y, 1 - c), (1 - x, y, c), (x, 1 - y, c), (1 - x, 1 - y, c)]:
                out.append((bufs[t], dst, dev))
        return out
    return jobs


def _gather_jobs_b(axes, sizes):
    nt = len(axes)

    def jobs(bufs):
        x, y, c = _my_place()
        out = []
        for t in range(nt):
            for px, py in [(1 - x, y), (x, 1 - y), (1 - x, 1 - y)]:
                blk = _block(bufs[t], axes[t], 4 * px + 2 * py + c, sizes[t])
                out.append((blk, blk, (x, y, 1 - c)))
        return out
    return jobs


def _exchange_place_jobs(axes, sizes, layers):
    def jobs(srcs, lands):
        x, y, c = _my_place()
        return [(_block(srcs[t], axes[t], 4 * x + 2 * y + c, sizes[t]), lands[t].at[0, layers[t]])
                for t in range(len(srcs))]
    return jobs


def _exchange_jobs(axes, sizes, layers):
    nt = len(axes)

    def jobs(bufs):
        x, y, c = _my_place()
        out = []
        for k in range(1, N_DEV):
            px, py, pc = _flip(x, k & 4), _flip(y, k & 2), _flip(c, k & 1)
            for t in range(nt):
                out.append((_block(bufs[t], axes[t], 4 * px + 2 * py + pc, sizes[t]),
                            bufs[nt + t].at[k, layers[t]], (px, py, pc)))
        return out
    return jobs


def _gather_begin(name, shards, axes, after):
    sizes = [sh.shape[ax] for sh, ax in zip(shards, axes)]
    lands = [lax.empty(sh.shape[:ax] + (N_DEV * sh.shape[ax],) + sh.shape[ax + 1:], sh.dtype)
             for sh, ax in zip(shards, axes)]
    lands = _place(name + "_place", shards, lands, _gather_place_jobs(axes, sizes))
    jobs = _gather_jobs_a(axes, sizes)
    send, recv, bufs, token = _split_start(name + "_a", list(shards) + lands, jobs, 4 * len(shards), after)
    return dict(name=name, axes=axes, sizes=sizes, send=send, recv=recv, bufs=bufs, jobs=jobs), token


def _gather_mid(h, after):
    nt = len(h["axes"])
    bufs = _split_wait(h["name"] + "_aw", h["bufs"], h["send"], h["recv"], h["jobs"], after)
    jobs = _gather_jobs_b(h["axes"], h["sizes"])
    send, recv, lands, token = _split_start(h["name"] + "_b", bufs[nt:], jobs, 3 * nt, bufs[0])
    return dict(h, send=send, recv=recv, bufs=lands, jobs=jobs), token


def _gather_end(h, after):
    return _split_wait(h["name"] + "_bw", h["bufs"], h["send"], h["recv"], h["jobs"], after)


def _exchange_begin(name, grads, axes, lands, layers, after):
    sizes = [g.shape[ax] // N_DEV for g, ax in zip(grads, axes)]
    lands = _place(name + "_place", grads, lands, _exchange_place_jobs(axes, sizes, layers))
    jobs = _exchange_jobs(axes, sizes, layers)
    send, recv, bufs, token = _split_start(name + "_s", list(grads) + lands, jobs, 7 * len(grads), after)
    return dict(name=name, n=len(grads), send=send, recv=recv, bufs=bufs, jobs=jobs), token


def _exchange_end(h, after):
    bufs = _split_wait(h["name"] + "_w", h["bufs"], h["send"], h["recv"], h["jobs"], after)
    return bufs[h["n"]:]


def _all_gather_small(name, vec, reduce):
    r = vec.shape[0]

    def body(v_ref, o_ref, *rest):
        if reduce:
            buf, send_sems, recv_sems = rest
        else:
            buf = o_ref
            send_sems, recv_sems = rest
        x, y, c = _my_place()
        mine = 4 * x + 2 * y + c
        buf[mine] = v_ref[...]
        copies = []
        for k in range(1, N_DEV):
            px, py, pc = _flip(x, k & 4), _flip(y, k & 2), _flip(c, k & 1)
            cp = pltpu.make_async_remote_copy(
                src_ref=v_ref, dst_ref=buf.at[mine], send_sem=send_sems.at[k - 1],
                recv_sem=recv_sems.at[k - 1], device_id=(px, py, pc), device_id_type=MESH)
            cp.start()
            copies.append(cp)
        for cp in copies:
            cp.wait()
        if reduce:
            acc = buf[0]
            for q in range(1, N_DEV):
                acc = acc + buf[q]
            o_ref[...] = acc

    scratch = [pltpu.SemaphoreType.DMA((N_DEV - 1,)), pltpu.SemaphoreType.DMA((N_DEV - 1,))]
    if reduce:
        scratch = [pltpu.VMEM((N_DEV, r, LANE), F32)] + scratch
        out_shape = jax.ShapeDtypeStruct((r, LANE), F32)
    else:
        out_shape = jax.ShapeDtypeStruct((N_DEV, r, LANE), F32)
    return pl.pallas_call(
        body, name=name, in_specs=[VMEM_SPEC], out_specs=VMEM_SPEC, out_shape=out_shape,
        scratch_shapes=scratch, compiler_params=_cp(has_side_effects=True),
    )(vec)


def _pack(arrs, row_mult=8):
    flat = jnp.concatenate([a.reshape(-1).astype(F32) for a in arrs])
    n = flat.shape[0]
    rows = -(-n // LANE)
    rows = -(-rows // row_mult) * row_mult
    return jnp.pad(flat, (0, rows * LANE - n)).reshape(rows, LANE)


def _unpack(vec, shapes):
    flat = vec.reshape(-1)
    out, pos = [], 0
    for sh in shapes:
        n = 1
        for s in sh:
            n *= s
        out.append(flat[pos:pos + n].reshape(sh))
        pos += n
    return out


BIG = ["conv_w_pw1", "conv_w_pw2", "mla_w_in", "mla_w_q_up", "mla_w_kv_up", "mla_w_o", "mlp_w1", "mlp_w2"]
BIG_AXIS = {"conv_w_pw1": 2, "conv_w_pw2": 1, "mla_w_in": 1, "mla_w_q_up": 2, "mla_w_kv_up": 2,
            "mla_w_o": 1, "mlp_w1": 2, "mlp_w2": 1}
SMALL_SHARDED = ["conv_w_dw", "mla_q_norm_g", "mla_kv_norm_g"]
REPLICATED = ["norm_mixer_g", "norm_mlp_g", "conv_b_pw1", "conv_b_dw", "conv_ln_g", "conv_ln_b",
              "conv_b_pw2", "final_norm_g"]
WEIGHTS = ["norm_mixer_g", "norm_mlp_g", "conv_w_pw1", "conv_b_pw1", "conv_w_dw", "conv_b_dw",
           "conv_ln_g", "conv_ln_b", "conv_w_pw2", "conv_b_pw2", "mla_w_in", "mla_q_norm_g",
           "mla_kv_norm_g", "mla_w_q_up", "mla_w_kv_up", "mla_w_o", "mlp_w1", "mlp_w2", "final_norm_g"]


def _unshard_last(g, lead):
    nd = g.ndim
    perm = tuple(range(1, nd - 1)) + (0, nd - 1)
    return g.transpose(perm).reshape(lead + (N_DEV * g.shape[-1],))


def _step(w, m, v, x, positions, target):
    s, d = x.shape
    depth = w["norm_mixer_g"].shape[0]
    n_conv, n_mla = w["conv_w_pw1"].shape[0], w["mla_w_in"].shape[0]
    heads = (w["mla_w_q_up"].shape[-1] * N_DEV) // (HEAD_NOPE + HEAD_ROPE)
    rq, rkv = w["mla_w_q_up"].shape[1], w["mla_w_kv_up"].shape[1]
    xi, yi, ci = _my_place()
    mine = 4 * xi + 2 * yi + ci

    def mixer_units(layer):
        names = (["conv_w_pw1", "conv_w_pw2"] if layer % 2 == 0
                 else ["mla_w_in", "mla_w_q_up", "mla_w_kv_up", "mla_w_o"])
        return [(n, layer // 2) for n in names]

    def mlp_units(layer):
        return [("mlp_w1", layer), ("mlp_w2", layer)]

    def gather_begin(tag, units, after):
        h, token = _gather_begin(tag, [w[n][jl].astype(BF16) for n, jl in units],
                                 [BIG_AXIS[n] - 1 for n, _ in units], after)
        return dict(h, units=units), token

    full = {}

    def gather_end(h, after):
        full.update(zip(h["units"], _gather_end(h, after)))

    first_a, tok = gather_begin("gather_0a", mixer_units(0), x)
    first_b, tok = gather_begin("gather_0b", mlp_units(0), tok)
    pending = {}
    if depth > 1:
        pending[1], tok = gather_begin("gather_1", mixer_units(1) + mlp_units(1), tok)
    first_a, tok = _gather_mid(first_a, tok)
    gather_end(first_a, tok)

    small_shapes = [w[n].shape for n in SMALL_SHARDED]
    gathered = _all_gather_small("gather_small", _pack([w[n] for n in SMALL_SHARDED]), False)
    per_dev = [_unpack(gathered[q], small_shapes) for q in range(N_DEV)]
    w_dw = _unshard_last(jnp.stack([p[0] for p in per_dev]), (n_conv, CONV_W))
    q_gain = _unshard_last(jnp.stack([p[1] for p in per_dev]), (n_mla,))
    kv_gain = _unshard_last(jnp.stack([p[2] for p in per_dev]), (n_mla,))
    w_dw_pad = jnp.pad(w_dw, ((0, 0), (0, HALO - CONV_W), (0, 0)))

    w_in_cols = rq + rkv + HEAD_ROPE

    def pad_w_in(a):
        return jnp.pad(a, ((0, 0), (0, rq + rkv + LANE - w_in_cols)))

    def pad_wq(a):
        return jnp.pad(a.reshape(rq, heads, HEAD_NOPE + HEAD_ROPE),
                       ((0, 0), (0, 0), (0, HEAD_QK_PAD - HEAD_NOPE - HEAD_ROPE))).reshape(rq, heads * HEAD_QK_PAD)

    inv_freq = ROPE_THETA ** (-jnp.arange(0, HEAD_ROPE, 2, dtype=F32) / HEAD_ROPE)
    ang = positions.reshape(s).astype(F32)[:, None] * inv_freq
    cos, sin = jnp.cos(ang), jnp.sin(ang)
    c64 = jnp.concatenate([cos, cos], axis=1)
    s64 = jnp.concatenate([-sin, sin], axis=1)
    zeros64 = jnp.zeros((s, LANE - HEAD_ROPE), F32)
    ck = jnp.concatenate([c64, zeros64], axis=1)
    sk = jnp.concatenate([s64, zeros64], axis=1)
    scale = (HEAD_NOPE + HEAD_ROPE) ** -0.5
    cq = scale * jnp.concatenate([jnp.ones((s, HEAD_NOPE), F32), ck], axis=1)
    sq = scale * jnp.concatenate([jnp.zeros((s, HEAD_NOPE), F32), sk], axis=1)

    def vec(a):
        return a.reshape(1, -1)

    saved = []
    wpad = {}
    for layer in range(depth):
        jl = layer // 2
        h = _rms_fwd(f"rms_mixer_{layer}", x, vec(w["norm_mixer_g"][layer]) + tok[0, 0])
        if layer % 2 == 0:
            ua, ug, glu = _mm_glu(f"conv_pw1_{layer}", h, full["conv_w_pw1", jl], None, vec(w["conv_b_pw1"][jl]))
            cc, sw = _conv_fwd(f"conv_dw_{layer}", glu, w_dw_pad[jl], vec(w["conv_b_dw"][jl]),
                               vec(w["conv_ln_g"][jl]), vec(w["conv_ln_b"][jl]))
            x1 = _mm_res(f"conv_pw2_{layer}", sw, full["conv_w_pw2", jl], None, x, vec(w["conv_b_pw2"][jl]))
            mix = (h, ua, ug, glu, cc, sw)
        else:
            wpad["in", jl] = pad_w_in(full["mla_w_in", jl])
            wpad["q", jl] = pad_wq(full["mla_w_q_up", jl])
            down = _mm_plain(f"mla_down_{layer}", h, wpad["in", jl], None, "nn", F32)
            qn, kvn, kpe = _mla_mid_fwd(f"mla_mid_{layer}", down, vec(q_gain[jl]), vec(kv_gain[jl]), ck, sk)
            qf = _mm_q(f"mla_q_{layer}", qn, wpad["q", jl], None, cq, sq)
            kf, vv = _mm_kv(f"mla_kv_{layer}", kvn, full["mla_w_kv_up", jl], None, kpe)
            o, lse = _flash_fwd(f"mla_attn_{layer}", qf, kf, vv, heads)
            x1 = _mm_res(f"mla_out_{layer}", o, full["mla_w_o", jl], None, x)
            mix = (h, down, qn, kvn, qf, kf, vv, o, lse)
        if layer == 0:
            first_b, tok = _gather_mid(first_b, x1)
        if layer + 2 < depth:
            pending[layer + 2], tok = gather_begin(f"gather_{layer + 2}",
                                                   mixer_units(layer + 2) + mlp_units(layer + 2),
                                                   tok if layer == 0 else x1)
        if layer == 0:
            gather_end(first_b, tok)
        elif layer + 1 < depth:
            pending[layer + 1], tok = _gather_mid(pending[layer + 1], x1)
        h2 = _rms_fwd(f"rms_mlp_{layer}", x1, vec(w["norm_mlp_g"][layer]) + tok[0, 0])
        z, a = _mm_mlp_up(f"mlp_up_{layer}", h2, full["mlp_w1", layer], None)
        x2 = _mm_res(f"mlp_down_{layer}", a, full["mlp_w2", layer], None, x1)
        if layer + 1 < depth:
            if layer == 0:
                pending[1], tok = _gather_mid(pending[1], x2)
                gather_end(pending[1], tok)
            else:
                gather_end(pending[layer + 1], x2)
        saved.append((x, mix, x1, h2, z, a))
        x = x2

    loss_row, g, gb, d_final, _ = _final_loss("final_loss", x, vec(w["final_norm_g"]) + tok[0, 0], target)

    recv = {n: lax.empty((N_DEV,) + w[n].shape, BF16) for n in BIG}

    def exchange_begin(tag, items, after):
        names = [n for n, _, _ in items]
        h, token = _exchange_begin(tag, [gr for _, _, gr in items], [BIG_AXIS[n] - 1 for n in names],
                                   [recv[n] for n in names], [jl for _, jl, _ in items], after)
        return dict(h, names=names), token

    def exchange_end(h, after):
        recv.update(zip(h["names"], _exchange_end(h, after)))

    mix_exchange = None
    d_mixer, d_mlp = [None] * depth, [None] * depth
    d_small = {n: [None] * n_conv for n in ["conv_b_pw1", "conv_w_dw", "conv_b_dw", "conv_ln_g",
                                           "conv_ln_b", "conv_b_pw2"]}
    d_qg, d_kvg = [None] * n_mla, [None] * n_mla
    for layer in reversed(range(depth)):
        jl = layer // 2
        x0, mix, x1, h2, z, a = saved[layer]
        colsum_g = None
        dz = _mm_mlp_dz(f"mlp_dz_{layer}", gb, full["mlp_w2", layer], None, z)
        dw2 = _mm_wgrad(f"mlp_dw2_{layer}", a, gb)
        dh2 = _mm_plain(f"mlp_dh_{layer}", dz, full["mlp_w1", layer], None, "nt", F32)
        dw1 = _mm_wgrad(f"mlp_dw1_{layer}", h2, dz)
        mlp_exchange, tok = exchange_begin(f"exchange_mlp_{layer}",
                                           [("mlp_w1", layer, dw1), ("mlp_w2", layer, dw2)], dh2)
        g, gb, d_mlp[layer], colsum_g = _rms_bwd(f"rms_mlp_bwd_{layer}", x1,
                                                 vec(w["norm_mlp_g"][layer]) + tok[0, 0], dh2, g)
        if mix_exchange is not None:
            exchange_end(mix_exchange, g)
        if layer % 2 == 0:
            h, ua, ug, glu, cc, sw = mix
            d_small["conv_b_pw2"][jl] = colsum_g.reshape(-1)
            dsw = _mm_plain(f"conv_ds_{layer}", gb, full["conv_w_pw2", jl], None, "nt", F32)
            dwp2 = _mm_wgrad(f"conv_dw2_{layer}", sw, gb)
            dc, dlg, dlb, dbdw = _conv_bwd_ln(f"conv_ln_bwd_{layer}", dsw, cc, vec(w["conv_ln_g"][jl]),
                                              vec(w["conv_ln_b"][jl]))
            du, dwdw, dbu = _conv_bwd_dw(f"conv_dw_bwd_{layer}", dc, glu, ua, ug, w_dw_pad[jl])
            d_small["conv_ln_g"][jl] = dlg.reshape(-1)
            d_small["conv_ln_b"][jl] = dlb.reshape(-1)
            d_small["conv_b_dw"][jl] = dbdw.reshape(-1)
            d_small["conv_w_dw"][jl] = dwdw[:CONV_W]
            d_small["conv_b_pw1"][jl] = dbu.reshape(-1)
            dh = _mm_plain(f"conv_dh_{layer}", du, full["conv_w_pw1", jl], None, "nt", F32)
            dwp1 = _mm_wgrad(f"conv_dw1_{layer}", h, du)
            items = [("conv_w_pw1", jl, dwp1), ("conv_w_pw2", jl, dwp2)]
        else:
            h, down, qn, kvn, qf, kf, vv, o, lse = mix
            do = _mm_plain(f"mla_do_{layer}", gb, full["mla_w_o", jl], None, "nt", BF16)
            dwo = _mm_wgrad(f"mla_dwo_{layer}", o, gb)
            dq, delta = _flash_bwd_dq(f"mla_attn_dq_{layer}", qf, kf, vv, do, o, lse, cq, sq, heads)
            dkv, dkpe = _flash_bwd_dkv(f"mla_attn_dkv_{layer}", qf, kf, vv, do, lse, delta, ck, sk, heads)
            dqn = _mm_plain(f"mla_dqn_{layer}", dq, wpad["q", jl], None, "nt", F32)
            dwq = _mm_wgrad(f"mla_dwq_{layer}", qn, dq).reshape(rq, heads, HEAD_QK_PAD)[
                :, :, :HEAD_NOPE + HEAD_ROPE].reshape(rq, heads * (HEAD_NOPE + HEAD_ROPE))
            dkvn = _mm_plain(f"mla_dkvn_{layer}", dkv, full["mla_w_kv_up", jl], None, "nt", F32)
            dwkv = _mm_wgrad(f"mla_dwkv_{layer}", kvn, dkv)
            ddown, d_qg[jl], d_kvg[jl] = _mla_mid_bwd(f"mla_mid_bwd_{layer}", down, vec(q_gain[jl]),
                                                      vec(kv_gain[jl]), dqn, dkvn, dkpe)
            dh = _mm_plain(f"mla_dh_{layer}", ddown, wpad["in", jl], None, "nt", F32)
            dwin = _mm_wgrad(f"mla_dwin_{layer}", h, ddown)[:, :w_in_cols]
            items = [("mla_w_in", jl, dwin), ("mla_w_q_up", jl, dwq), ("mla_w_kv_up", jl, dwkv),
                     ("mla_w_o", jl, dwo)]
        mix_exchange, tok = exchange_begin(f"exchange_mix_{layer}", items, dh)
        g, gb, d_mixer[layer], _ = _rms_bwd(f"rms_mixer_bwd_{layer}", x0,
                                            vec(w["norm_mixer_g"][layer]) + tok[0, 0], dh, g)
        exchange_end(mlp_exchange, g)
    exchange_end(mix_exchange, g)
    grad_x = g

    out = {}
    for n in BIG:
        sh = w[n].shape
        r, c = sh[0] * sh[1], sh[2]
        res = _adamw(f"adamw_{n}", recv[n].reshape(N_DEV, r, c), w[n].reshape(r, c),
                     m[n].reshape(r, c), v[n].reshape(r, c))
        out[n] = [t.reshape(sh) for t in res]

    small_full = {
        "norm_mixer_g": jnp.concatenate(d_mixer, axis=0), "norm_mlp_g": jnp.concatenate(d_mlp, axis=0),
        "conv_b_pw1": jnp.stack(d_small["conv_b_pw1"]), "conv_b_dw": jnp.stack(d_small["conv_b_dw"]),
        "conv_ln_g": jnp.stack(d_small["conv_ln_g"]), "conv_ln_b": jnp.stack(d_small["conv_ln_b"]),
        "conv_b_pw2": jnp.stack(d_small["conv_b_pw2"]), "final_norm_g": d_final.reshape(-1),
        "conv_w_dw": jnp.stack(d_small["conv_w_dw"]),
        "mla_q_norm_g": jnp.concatenate(d_qg, axis=0), "mla_kv_norm_g": jnp.concatenate(d_kvg, axis=0),
    }
    names = REPLICATED + SMALL_SHARDED
    summed = _unpack(_all_gather_small("reduce_small", _pack([small_full[n] for n in names]), True),
                     [small_full[n].shape for n in names])
    summed = dict(zip(names, summed))
    for n in SMALL_SHARDED:
        width = w[n].shape[-1]
        summed[n] = lax.dynamic_slice_in_dim(summed[n], mine * width, width, axis=summed[n].ndim - 1)
    for group, tag in ((REPLICATED, "replicated"), (SMALL_SHARDED, "small_sharded")):
        shapes = [w[n].shape for n in group]
        res = _adamw(f"adamw_{tag}", _pack([summed[n] for n in group])[None],
                     _pack([w[n] for n in group]), _pack([m[n] for n in group]), _pack([v[n] for n in group]))
        unpacked = [_unpack(t, shapes) for t in res]
        for q, n in enumerate(group):
            out[n] = [unpacked[0][q], unpacked[1][q], unpacked[2][q], unpacked[3][q]]

    loss = lax.psum(loss_row[0, 0], ("x", "y", "c"))
    return loss, grad_x, out


def kernel(x, positions, norm_mixer_g, norm_mlp_g, conv_w_pw1, conv_b_pw1, conv_w_dw, conv_b_dw, conv_ln_g, conv_ln_b, conv_w_pw2, conv_b_pw2, mla_w_in, mla_q_norm_g, mla_kv_norm_g, mla_w_q_up, mla_w_kv_up, mla_w_o, mlp_w1, mlp_w2, final_norm_g, loss_target, m_norm_mixer_g, m_norm_mlp_g, m_conv_w_pw1, m_conv_b_pw1, m_conv_w_dw, m_conv_b_dw, m_conv_ln_g, m_conv_ln_b, m_conv_w_pw2, m_conv_b_pw2, m_mla_w_in, m_mla_q_norm_g, m_mla_kv_norm_g, m_mla_w_q_up, m_mla_w_kv_up, m_mla_w_o, m_mlp_w1, m_mlp_w2, m_final_norm_g, v_norm_mixer_g, v_norm_mlp_g, v_conv_w_pw1, v_conv_b_pw1, v_conv_w_dw, v_conv_b_dw, v_conv_ln_g, v_conv_ln_b, v_conv_w_pw2, v_conv_b_pw2, v_mla_w_in, v_mla_q_norm_g, v_mla_kv_norm_g, v_mla_w_q_up, v_mla_w_kv_up, v_mla_w_o, v_mlp_w1, v_mlp_w2, v_final_norm_g):
    ws = (norm_mixer_g, norm_mlp_g, conv_w_pw1, conv_b_pw1, conv_w_dw, conv_b_dw, conv_ln_g, conv_ln_b,
          conv_w_pw2, conv_b_pw2, mla_w_in, mla_q_norm_g, mla_kv_norm_g, mla_w_q_up, mla_w_kv_up, mla_w_o,
          mlp_w1, mlp_w2, final_norm_g)
    ms = (m_norm_mixer_g, m_norm_mlp_g, m_conv_w_pw1, m_conv_b_pw1, m_conv_w_dw, m_conv_b_dw, m_conv_ln_g,
          m_conv_ln_b, m_conv_w_pw2, m_conv_b_pw2, m_mla_w_in, m_mla_q_norm_g, m_mla_kv_norm_g,
          m_mla_w_q_up, m_mla_w_kv_up, m_mla_w_o, m_mlp_w1, m_mlp_w2, m_final_norm_g)
    vs = (v_norm_mixer_g, v_norm_mlp_g, v_conv_w_pw1, v_conv_b_pw1, v_conv_w_dw, v_conv_b_dw, v_conv_ln_g,
          v_conv_ln_b, v_conv_w_pw2, v_conv_b_pw2, v_mla_w_in, v_mla_q_norm_g, v_mla_kv_norm_g,
          v_mla_w_q_up, v_mla_w_kv_up, v_mla_w_o, v_mlp_w1, v_mlp_w2, v_final_norm_g)
    w, m, v = dict(zip(WEIGHTS, ws)), dict(zip(WEIGHTS, ms)), dict(zip(WEIGHTS, vs))
    s, d = x.shape[-2], x.shape[-1]
    loss, grad_x, out = _step(w, m, v, x.reshape(s, d), positions, loss_target.reshape(s, d))
    grads = [out[n][0] for n in WEIGHTS]
    deltas = [out[n][1] for n in WEIGHTS]
    new_m = [out[n][2] for n in WEIGHTS]
    new_v = [out[n][3] for n in WEIGHTS]
    return (loss, grad_x.reshape(x.shape), *grads, *deltas, *new_m, *new_v)
```

```python
import functools

import jax
import jax.numpy as jnp
from jax import lax
from jax.experimental import pallas as pl
from jax.experimental.pallas import tpu as pltpu

F32 = jnp.float32
BF16 = jnp.bfloat16

NORM_EPS = 1e-6
LN_EPS = 1e-5
ROPE_THETA = 10000.0
CHUNK_BITS = 6
HEAD_NOPE = 128
HEAD_ROPE = 64
HEAD_V = 128
HEAD_QK_PAD = 256
CONV_W = 31
HALO = 32
N_DEV = 8

ADAM_LR = 0.001
ADAM_B1 = 0.9
ADAM_B2 = 0.999
ADAM_EPS = 1e-08
ADAM_WD = 0.01
ADAM_STEP = 10

V7X_VMEM_BYTES = 64 * 1024 * 1024
VMEM_LIMIT = (V7X_VMEM_BYTES * 3) // 4
LANE = 128

MESH = pl.DeviceIdType.MESH
ANY = pl.BlockSpec(memory_space=pl.ANY)
VMEM_SPEC = pl.BlockSpec(memory_space=pltpu.VMEM)


def _cp(**kw):
    return pltpu.CompilerParams(vmem_limit_bytes=VMEM_LIMIT, **kw)


SUBLANE_BF16 = 16

TM_PREF = 1024
TN_PREF = 1024
TK_PREF = 2048


def _tile(n, pref, mult=SUBLANE_BF16):
    if n <= pref + pref // 2:
        return n
    t = (pref // mult) * mult
    while t >= mult:
        if n % t == 0:
            return t
        t -= mult
    return n


def _sigmoid(x):
    return 1.0 / (1.0 + jnp.exp(-x))


def _rot_half(x):
    n = x.shape[-1]
    lane = lax.broadcasted_iota(jnp.int32, x.shape, x.ndim - 1)
    first = (lane & 63) < 32
    return jnp.where(first, pltpu.roll(x, n - 32, x.ndim - 1), pltpu.roll(x, 32, x.ndim - 1))


def _rope(x, c, s):
    return x * c + _rot_half(x) * s


def _rope_t(d, c, s):
    return d * c + _rot_half(d * s)


def _chunk_mask(t):
    row = lax.broadcasted_iota(jnp.int32, (t, t), 0)
    col = lax.broadcasted_iota(jnp.int32, (t, t), 1)
    return jnp.right_shift(col, CHUNK_BITS) <= jnp.right_shift(row, CHUNK_BITS)


def _chunk_mask_t(t):
    row = lax.broadcasted_iota(jnp.int32, (t, t), 0)
    col = lax.broadcasted_iota(jnp.int32, (t, t), 1)
    return jnp.right_shift(row, CHUNK_BITS) <= jnp.right_shift(col, CHUNK_BITS)


def _rms_fwd(name, x, g):
    t, d = x.shape
    tm = _tile(t, 512)

    def body(x_ref, g_ref, o_ref):
        xf = x_ref[...]
        r = lax.rsqrt(jnp.mean(xf * xf, axis=-1, keepdims=True) + NORM_EPS)
        o_ref[...] = (xf * r * g_ref[...]).astype(o_ref.dtype)

    return pl.pallas_call(
        body, name=name, grid=(t // tm,),
        in_specs=[pl.BlockSpec((tm, d), lambda i: (i, 0)), pl.BlockSpec((1, d), lambda i: (0, 0))],
        out_specs=pl.BlockSpec((tm, d), lambda i: (i, 0)),
        out_shape=jax.ShapeDtypeStruct((t, d), BF16),
        compiler_params=_cp(),
    )(x, g)


def _rms_bwd_math(xf, g, dy):
    r = lax.rsqrt(jnp.mean(xf * xf, axis=-1, keepdims=True) + NORM_EPS)
    xh = xf * r
    dg = jnp.sum(dy * xh, axis=0, keepdims=True)
    dxh = dy * g
    dx = r * (dxh - xh * jnp.mean(dxh * xh, axis=-1, keepdims=True))
    return dx, dg


def _rms_bwd(name, x, g, dy, resid):
    t, d = x.shape
    tm = _tile(t, 256)

    def body(x_ref, g_ref, dy_ref, r_ref, dx_ref, dxb_ref, dg_ref, cs_ref):
        @pl.when(pl.program_id(0) == 0)
        def _():
            dg_ref[...] = jnp.zeros_like(dg_ref)
            cs_ref[...] = jnp.zeros_like(cs_ref)

        dx, dg = _rms_bwd_math(x_ref[...], g_ref[...], dy_ref[...])
        tot = r_ref[...] + dx
        dx_ref[...] = tot
        dxb_ref[...] = tot.astype(BF16)
        dg_ref[...] += dg
        cs_ref[...] += jnp.sum(tot, axis=0, keepdims=True)

    row = pl.BlockSpec((tm, d), lambda i: (i, 0))
    vec = pl.BlockSpec((1, d), lambda i: (0, 0))
    return pl.pallas_call(
        body, name=name, grid=(t // tm,),
        in_specs=[row, vec, row, row],
        out_specs=[row, row, vec, vec],
        out_shape=[jax.ShapeDtypeStruct((t, d), F32), jax.ShapeDtypeStruct((t, d), BF16),
                   jax.ShapeDtypeStruct((1, d), F32), jax.ShapeDtypeStruct((1, d), F32)],
        compiler_params=_cp(dimension_semantics=("arbitrary",)),
    )(x, g, dy, resid)


def _final_loss(name, x, g, target):
    t, d = x.shape
    tm = _tile(t, 256)

    def body(x_ref, g_ref, t_ref, loss_ref, dx_ref, dxb_ref, dg_ref, cs_ref):
        @pl.when(pl.program_id(0) == 0)
        def _():
            loss_ref[...] = jnp.zeros_like(loss_ref)
            dg_ref[...] = jnp.zeros_like(dg_ref)
            cs_ref[...] = jnp.zeros_like(cs_ref)

        xf = x_ref[...]
        gg = g_ref[...]
        r = lax.rsqrt(jnp.mean(xf * xf, axis=-1, keepdims=True) + NORM_EPS)
        err = xf * r * gg - t_ref[...]
        part = 0.5 * jnp.sum(jnp.mean(err * err, axis=-1, keepdims=True), axis=0, keepdims=True)
        loss_ref[...] += jnp.broadcast_to(part, loss_ref.shape)
        dx, dg = _rms_bwd_math(xf, gg, err * (1.0 / d))
        dx_ref[...] = dx
        dxb_ref[...] = dx.astype(BF16)
        dg_ref[...] += dg
        cs_ref[...] += jnp.sum(dx, axis=0, keepdims=True)

    row = pl.BlockSpec((tm, d), lambda i: (i, 0))
    vec = pl.BlockSpec((1, d), lambda i: (0, 0))
    return pl.pallas_call(
        body, name=name, grid=(t // tm,),
        in_specs=[row, vec, row],
        out_specs=[pl.BlockSpec((1, LANE), lambda i: (0, 0)), row, row, vec, vec],
        out_shape=[jax.ShapeDtypeStruct((1, LANE), F32), jax.ShapeDtypeStruct((t, d), F32),
                   jax.ShapeDtypeStruct((t, d), BF16), jax.ShapeDtypeStruct((1, d), F32),
                   jax.ShapeDtypeStruct((1, d), F32)],
        compiler_params=_cp(dimension_semantics=("arbitrary",)),
    )(x, g, target)


_DIMS = {
    "nn": (((1,), (0,)), ((), ())),
    "nt": (((1,), (1,)), ((), ())),
    "tn": (((0,), (0,)), ((), ())),
}


def _mm(name, a, bs, *, mode, m, n, k, epilogue, out_shape, out_specs, extras=(), extra_specs=(),
        a_lead=None, aliases=None, tn_div=1):
    tm, tn, tk = _tiles(m, n, k, tn_div)
    nk = k // tk
    nb, ne = len(bs), len(extras)
    no = len(out_shape)
    dims = _DIMS[mode]

    def with_lead(shape, idx, lead):
        if lead is None:
            return pl.BlockSpec(shape, idx)
        return pl.BlockSpec((None,) + shape, lambda i, j, kk: (lead,) + idx(i, j, kk))

    if mode == "tn":
        a_spec = with_lead((tk, tm), lambda i, j, kk: (kk, i), a_lead)
    else:
        a_spec = with_lead((tm, tk), lambda i, j, kk: (i, kk), a_lead)
    b_specs = []
    for _, lead, off in bs:
        if mode == "nt":
            b_specs.append(with_lead((tn, tk), lambda i, j, kk, off=off: (j + off, kk), lead))
        else:
            b_specs.append(with_lead((tk, tn), lambda i, j, kk, off=off: (kk, j + off), lead))

    def body(*refs):
        a_ref = refs[0]
        b_refs = refs[1:1 + nb]
        ex = refs[1 + nb:1 + nb + ne]
        outs = refs[1 + nb + ne:1 + nb + ne + no]
        accs = refs[1 + nb + ne + no:]

        def part(b_ref):
            return lax.dot_general(a_ref[...], b_ref[...], dims, preferred_element_type=F32)

        if nk == 1:
            epilogue([part(b_ref) for b_ref in b_refs], ex, outs)
            return
        kk = pl.program_id(2)

        @pl.when(kk == 0)
        def _():
            for acc, b_ref in zip(accs, b_refs):
                acc[...] = part(b_ref)

        @pl.when(kk > 0)
        def _():
            for acc, b_ref in zip(accs, b_refs):
                acc[...] += part(b_ref)

        @pl.when(kk == nk - 1)
        def _():
            epilogue([acc[...] for acc in accs], ex, outs)

    scratch = [pltpu.VMEM((tm, tn), F32) for _ in range(nb)] if nk > 1 else []
    return pl.pallas_call(
        body, name=name, grid=(m // tm, n // tn, nk),
        in_specs=[a_spec] + b_specs + list(extra_specs),
        out_specs=list(out_specs), out_shape=list(out_shape), scratch_shapes=scratch,
        input_output_aliases=aliases or {},
        compiler_params=_cp(dimension_semantics=("arbitrary", "arbitrary", "arbitrary")),
    )(a, *[b for b, _, _ in bs], *extras), (tm, tn, tk)


def _ij(tm, tn):
    return pl.BlockSpec((tm, tn), lambda i, j, kk: (i, j))


def _tiles(m, n, k, tn_div=1):
    return _tile(m, TM_PREF), _tile(n, TN_PREF // tn_div, LANE), _tile(k, TK_PREF, LANE)


def _mm_plain(name, a, b, b_lead, mode, out_dtype):
    m, k = a.shape
    n = b.shape[-1] if mode == "nn" else b.shape[-2]
    tm, tn, _ = _tiles(m, n, k)

    def epilogue(accs, ex, outs):
        outs[0][...] = accs[0].astype(out_dtype)

    return _mm(name, a, [(b, b_lead, 0)], mode=mode, m=m, n=n, k=k, epilogue=epilogue,
               out_shape=[jax.ShapeDtypeStruct((m, n), out_dtype)], out_specs=[_ij(tm, tn)])[0][0]


def _mm_res(name, a, b, b_lead, resid, bias=None):
    m, k = a.shape
    n = b.shape[-1]
    tm, tn, _ = _tiles(m, n, k)
    extras, specs = [resid], [_ij(tm, tn)]
    if bias is not None:
        extras.append(bias)
        specs.append(pl.BlockSpec((1, tn), lambda i, j, kk: (0, j)))

    def epilogue(accs, ex, outs):
        y = ex[0][...] + accs[0]
        if bias is not None:
            y = y + ex[1][...]
        outs[0][...] = y

    return _mm(name, a, [(b, b_lead, 0)], mode="nn", m=m, n=n, k=k, epilogue=epilogue,
               extras=extras, extra_specs=specs,
               out_shape=[jax.ShapeDtypeStruct((m, n), F32)], out_specs=[_ij(tm, tn)])[0][0]


def _mm_mlp_up(name, h, w1, lead):
    m, k = h.shape
    n = w1.shape[-1]
    tm, tn, _ = _tiles(m, n, k)

    def epilogue(accs, ex, outs):
        z = accs[0]
        outs[0][...] = z.astype(BF16)
        r = jnp.maximum(z, 0.0)
        outs[1][...] = (r * r).astype(BF16)

    sh = jax.ShapeDtypeStruct((m, n), BF16)
    return _mm(name, h, [(w1, lead, 0)], mode="nn", m=m, n=n, k=k, epilogue=epilogue,
               out_shape=[sh, sh], out_specs=[_ij(tm, tn), _ij(tm, tn)])[0]


def _mm_mlp_dz(name, g, w2, lead, z):
    m, k = g.shape
    n = w2.shape[-2]
    tm, tn, _ = _tiles(m, n, k)

    def epilogue(accs, ex, outs):
        outs[0][...] = (accs[0] * (2.0 * jnp.maximum(ex[0][...].astype(F32), 0.0))).astype(BF16)

    return _mm(name, g, [(w2, lead, 0)], mode="nt", m=m, n=n, k=k, epilogue=epilogue,
               extras=[z], extra_specs=[_ij(tm, tn)],
               out_shape=[jax.ShapeDtypeStruct((m, n), BF16)], out_specs=[_ij(tm, tn)])[0][0]


def _mm_glu(name, h, w, lead, bias):
    m, k = h.shape
    n = w.shape[-1] // 2
    tm, tn, _ = _tiles(m, n, k, 2)
    off = n // tn

    def epilogue(accs, ex, outs):
        a = accs[0] + ex[0][...]
        gate = accs[1] + ex[1][...]
        outs[0][...] = a.astype(BF16)
        outs[1][...] = gate.astype(BF16)
        outs[2][...] = a * _sigmoid(gate)

    shb = jax.ShapeDtypeStruct((m, n), BF16)
    return _mm(name, h, [(w, lead, 0), (w, lead, off)], mode="nn", m=m, n=n, k=k, epilogue=epilogue,
               extras=[bias, bias],
               extra_specs=[pl.BlockSpec((1, tn), lambda i, j, kk: (0, j)),
                            pl.BlockSpec((1, tn), lambda i, j, kk: (0, j + off))],
               out_shape=[shb, shb, jax.ShapeDtypeStruct((m, n), F32)],
               out_specs=[_ij(tm, tn)] * 3, tn_div=2)[0]


def _mm_q(name, qn, wq_pad, lead, cq, sq):
    m, k = qn.shape
    n = wq_pad.shape[-1]
    tm, tn, _ = _tiles(m, n, k)
    rep = tn // HEAD_QK_PAD

    def epilogue(accs, ex, outs):
        c = jnp.tile(ex[0][...], (1, rep))
        s = jnp.tile(ex[1][...], (1, rep))
        outs[0][...] = _rope(accs[0], c, s).astype(BF16)

    tab = pl.BlockSpec((tm, HEAD_QK_PAD), lambda i, j, kk: (i, 0))
    return _mm(name, qn, [(wq_pad, lead, 0)], mode="nn", m=m, n=n, k=k, epilogue=epilogue,
               extras=[cq, sq], extra_specs=[tab, tab],
               out_shape=[jax.ShapeDtypeStruct((m, n), BF16)], out_specs=[_ij(tm, tn)])[0][0]


def _mm_kv(name, kvn, wkv, lead, kpe):
    m, k = kvn.shape
    n = wkv.shape[-1]
    tm, tn, _ = _tiles(m, n, k)
    heads = tn // (HEAD_NOPE + HEAD_V)

    def epilogue(accs, ex, outs):
        acc = accs[0]
        pe = ex[0][...].astype(F32)
        kparts, vparts = [], []
        for hh in range(heads):
            base = hh * (HEAD_NOPE + HEAD_V)
            kparts += [acc[:, base:base + HEAD_NOPE], pe]
            vparts.append(acc[:, base + HEAD_NOPE:base + HEAD_NOPE + HEAD_V])
        outs[0][...] = jnp.concatenate(kparts, axis=1).astype(BF16)
        outs[1][...] = jnp.concatenate(vparts, axis=1).astype(BF16) if heads > 1 else vparts[0].astype(BF16)

    return _mm(name, kvn, [(wkv, lead, 0)], mode="nn", m=m, n=n, k=k, epilogue=epilogue,
               extras=[kpe], extra_specs=[pl.BlockSpec((tm, LANE), lambda i, j, kk: (i, 0))],
               out_shape=[jax.ShapeDtypeStruct((m, n), BF16),
                          jax.ShapeDtypeStruct((m, n // 2), BF16)],
               out_specs=[_ij(tm, heads * HEAD_QK_PAD), _ij(tm, tn // 2)])[0]


def _mm_wgrad(name, a, b):
    t, m = a.shape
    n = b.shape[-1]
    tm, tn, _ = _tiles(m, n, t)

    def epilogue(accs, ex, outs):
        outs[0][...] = accs[0].astype(BF16)

    return _mm(name, a, [(b, None, 0)], mode="tn", m=m, n=n, k=t, epilogue=epilogue,
               out_shape=[jax.ShapeDtypeStruct((m, n), BF16)], out_specs=[_ij(tm, tn)])[0][0]


CONV_ROWS = 256
CONV_RT = 64
CONV_CW = 256
CONV_LR = 32


def _ln_stats(c):
    mu = jnp.mean(c, axis=-1, keepdims=True)
    xc = c - mu
    rstd = lax.rsqrt(jnp.mean(xc * xc, axis=-1, keepdims=True) + LN_EPS)
    return xc * rstd, rstd


def _conv_fwd(name, glu, w_dw, b_dw, ln_g, ln_b):
    t, d = glu.shape
    tt = _tile(t, CONV_ROWS)
    rt, cw, lr = min(CONV_RT, tt), min(CONV_CW, d), min(CONV_LR, tt)
    hb = tt // HALO

    def body(gc_ref, gp_ref, w_ref, b_ref, lg_ref, lb_ref, c_ref, s_ref, buf):
        i = pl.program_id(0)
        buf[0:HALO, :] = jnp.where(i > 0, gp_ref[...], 0.0)
        buf[HALO:HALO + tt, :] = gc_ref[...]

        def chunk(cb, carry):
            col = pl.ds(pl.multiple_of(cb * cw, cw), cw)
            for r0 in range(0, tt, rt):
                acc = jnp.broadcast_to(b_ref[:, col], (rt, cw))
                for k in range(CONV_W):
                    lo = r0 + HALO - (CONV_W - 1) + k
                    acc = acc + w_ref[k:k + 1, col] * buf[lo:lo + rt, col]
                c_ref[r0:r0 + rt, col] = acc
            return carry

        lax.fori_loop(0, d // cw, chunk, 0)

        def ln(r, carry):
            rows = pl.ds(pl.multiple_of(r * lr, lr), lr)
            xh, _ = _ln_stats(c_ref[rows, :])
            y = xh * lg_ref[...] + lb_ref[...]
            s_ref[rows, :] = (y * _sigmoid(y)).astype(BF16)
            return carry

        lax.fori_loop(0, tt // lr, ln, 0)

    row = pl.BlockSpec((tt, d), lambda i: (i, 0))
    vec = pl.BlockSpec((1, d), lambda i: (0, 0))
    return pl.pallas_call(
        body, name=name, grid=(t // tt,),
        in_specs=[row, pl.BlockSpec((HALO, d), lambda i: (jnp.maximum(i * hb - 1, 0), 0)),
                  pl.BlockSpec((HALO, d), lambda i: (0, 0)), vec, vec, vec],
        out_specs=[row, row],
        out_shape=[jax.ShapeDtypeStruct((t, d), F32), jax.ShapeDtypeStruct((t, d), BF16)],
        scratch_shapes=[pltpu.VMEM((HALO + tt, d), F32)],
        compiler_params=_cp(dimension_semantics=("arbitrary",)),
    )(glu, glu, w_dw, b_dw, ln_g, ln_b)


def _conv_bwd_ln(name, ds, c, ln_g, ln_b):
    t, d = c.shape
    tt = _tile(t, CONV_ROWS)
    lr = min(CONV_LR, tt)

    def body(ds_ref, c_ref, lg_ref, lb_ref, dc_ref, dg_ref, db_ref, dbdw_ref):
        @pl.when(pl.program_id(0) == 0)
        def _():
            dg_ref[...] = jnp.zeros_like(dg_ref)
            db_ref[...] = jnp.zeros_like(db_ref)
            dbdw_ref[...] = jnp.zeros_like(dbdw_ref)

        def chunk(r, carry):
            rows = pl.ds(pl.multiple_of(r * lr, lr), lr)
            xh, rstd = _ln_stats(c_ref[rows, :])
            g = lg_ref[...]
            y = xh * g + lb_ref[...]
            sg = _sigmoid(y)
            dy = ds_ref[rows, :] * (sg * (1.0 + y * (1.0 - sg)))
            dxh = dy * g
            dc = rstd * (dxh - jnp.mean(dxh, axis=-1, keepdims=True)
                         - xh * jnp.mean(dxh * xh, axis=-1, keepdims=True))
            dc_ref[rows, :] = dc
            dg_ref[...] += jnp.sum(dy * xh, axis=0, keepdims=True)
            db_ref[...] += jnp.sum(dy, axis=0, keepdims=True)
            dbdw_ref[...] += jnp.sum(dc, axis=0, keepdims=True)
            return carry

        lax.fori_loop(0, tt // lr, chunk, 0)

    row = pl.BlockSpec((tt, d), lambda i: (i, 0))
    vec = pl.BlockSpec((1, d), lambda i: (0, 0))
    vsh = jax.ShapeDtypeStruct((1, d), F32)
    return pl.pallas_call(
        body, name=name, grid=(t // tt,),
        in_specs=[row, row, vec, vec], out_specs=[row, vec, vec, vec],
        out_shape=[jax.ShapeDtypeStruct((t, d), F32), vsh, vsh, vsh],
        compiler_params=_cp(dimension_semantics=("arbitrary",)),
    )(ds, c, ln_g, ln_b)


def _conv_bwd_dw(name, dc, glu, ua, ug, w_dw):
    t, d = dc.shape
    tt = _tile(t, CONV_ROWS)
    rt, cw = min(CONV_RT, tt), min(CONV_CW, d)
    hb = tt // HALO
    nt = t // tt

    def body(dcc_ref, dcn_ref, gc_ref, gp_ref, ua_ref, ug_ref, w_ref,
             du_ref, dw_ref, dbu_ref, dbuf, gbuf, wacc):
        i = pl.program_id(0)

        @pl.when(i == 0)
        def _():
            wacc[...] = jnp.zeros_like(wacc)
            dbu_ref[...] = jnp.zeros_like(dbu_ref)

        dbuf[0:tt, :] = dcc_ref[...]
        dbuf[tt:tt + HALO, :] = jnp.where(i < nt - 1, dcn_ref[...], 0.0)
        gbuf[0:HALO, :] = jnp.where(i > 0, gp_ref[...], 0.0)
        gbuf[HALO:HALO + tt, :] = gc_ref[...]

        def chunk(cb, carry):
            c0 = pl.multiple_of(cb * cw, cw)
            col = pl.ds(c0, cw)
            colg = pl.ds(pl.multiple_of(d + cb * cw, cw), cw)
            for r0 in range(0, tt, rt):
                dcr = dbuf[r0:r0 + rt, col]
                dgl = jnp.zeros((rt, cw), F32)
                for k in range(CONV_W):
                    hi = r0 + (CONV_W - 1) - k
                    dgl = dgl + w_ref[k:k + 1, col] * dbuf[hi:hi + rt, col]
                    lo = r0 + HALO - (CONV_W - 1) + k
                    prod = dcr * gbuf[lo:lo + rt, col]
                    part = prod[0:8, :]
                    for r in range(8, rt, 8):
                        part = part + prod[r:r + 8, :]
                    wacc[8 * k:8 * k + 8, col] += part
                a = ua_ref[r0:r0 + rt, col].astype(F32)
                sg = _sigmoid(ug_ref[r0:r0 + rt, col].astype(F32))
                da = dgl * sg
                dgate = dgl * a * sg * (1.0 - sg)
                du_ref[r0:r0 + rt, col] = da.astype(BF16)
                du_ref[r0:r0 + rt, colg] = dgate.astype(BF16)
                dbu_ref[:, col] += jnp.sum(da, axis=0, keepdims=True)
                dbu_ref[:, colg] += jnp.sum(dgate, axis=0, keepdims=True)
            return carry

        lax.fori_loop(0, d // cw, chunk, 0)

        @pl.when(i == nt - 1)
        def _():
            for k in range(CONV_W):
                dw_ref[k:k + 1, :] = jnp.sum(wacc[8 * k:8 * k + 8, :], axis=0, keepdims=True)
            dw_ref[CONV_W:HALO, :] = jnp.zeros((HALO - CONV_W, d), F32)

    row = pl.BlockSpec((tt, d), lambda i: (i, 0))
    return pl.pallas_call(
        body, name=name, grid=(nt,),
        in_specs=[row, pl.BlockSpec((HALO, d), lambda i: (jnp.minimum((i + 1) * hb, t // HALO - 1), 0)),
                  row, pl.BlockSpec((HALO, d), lambda i: (jnp.maximum(i * hb - 1, 0), 0)),
                  row, row, pl.BlockSpec((HALO, d), lambda i: (0, 0))],
        out_specs=[pl.BlockSpec((tt, 2 * d), lambda i: (i, 0)),
                   pl.BlockSpec((HALO, d), lambda i: (0, 0)),
                   pl.BlockSpec((1, 2 * d), lambda i: (0, 0))],
        out_shape=[jax.ShapeDtypeStruct((t, 2 * d), BF16), jax.ShapeDtypeStruct((HALO, d), F32),
                   jax.ShapeDtypeStruct((1, 2 * d), F32)],
        scratch_shapes=[pltpu.VMEM((tt + HALO, d), F32), pltpu.VMEM((HALO + tt, d), F32),
                        pltpu.VMEM((8 * HALO, d), F32)],
        compiler_params=_cp(dimension_semantics=("arbitrary",)),
    )(dc, dc, glu, glu, ua, ug, w_dw)


def _mla_mid_fwd(name, down, qg, kvg, ck, sk):
    t, w = down.shape
    rq, rkv = qg.shape[-1], kvg.shape[-1]
    tm = _tile(t, 512)

    def body(dn_ref, qg_ref, kvg_ref, ck_ref, sk_ref, qn_ref, kvn_ref, kpe_ref):
        cq = dn_ref[:, 0:rq]
        ckv = dn_ref[:, rq:rq + rkv]
        pe = dn_ref[:, rq + rkv:rq + rkv + LANE]
        qn_ref[...] = (cq * lax.rsqrt(jnp.mean(cq * cq, axis=-1, keepdims=True) + NORM_EPS)
                       * qg_ref[...]).astype(BF16)
        kvn_ref[...] = (ckv * lax.rsqrt(jnp.mean(ckv * ckv, axis=-1, keepdims=True) + NORM_EPS)
                        * kvg_ref[...]).astype(BF16)
        kpe_ref[...] = _rope(pe, ck_ref[...], sk_ref[...]).astype(BF16)

    def row(n):
        return pl.BlockSpec((tm, n), lambda i: (i, 0))

    def vec(n):
        return pl.BlockSpec((1, n), lambda i: (0, 0))

    return pl.pallas_call(
        body, name=name, grid=(t // tm,),
        in_specs=[row(w), vec(rq), vec(rkv), row(LANE), row(LANE)],
        out_specs=[row(rq), row(rkv), row(LANE)],
        out_shape=[jax.ShapeDtypeStruct((t, rq), BF16), jax.ShapeDtypeStruct((t, rkv), BF16),
                   jax.ShapeDtypeStruct((t, LANE), BF16)],
        compiler_params=_cp(),
    )(down, qg, kvg, ck, sk)


def _mla_mid_bwd(name, down, qg, kvg, dqn, dkvn, dkpe):
    t, w = down.shape
    rq, rkv = qg.shape[-1], kvg.shape[-1]
    tm = _tile(t, 256)

    def body(dn_ref, qg_ref, kvg_ref, dqn_ref, dkvn_ref, dkpe_ref, dd_ref, dqg_ref, dkvg_ref):
        @pl.when(pl.program_id(0) == 0)
        def _():
            dqg_ref[...] = jnp.zeros_like(dqg_ref)
            dkvg_ref[...] = jnp.zeros_like(dkvg_ref)

        dcq, dqg = _rms_bwd_math(dn_ref[:, 0:rq], qg_ref[...], dqn_ref[...])
        dckv, dkvg = _rms_bwd_math(dn_ref[:, rq:rq + rkv], kvg_ref[...], dkvn_ref[...])
        dd_ref[:, 0:rq] = dcq.astype(BF16)
        dd_ref[:, rq:rq + rkv] = dckv.astype(BF16)
        dd_ref[:, rq + rkv:rq + rkv + LANE] = dkpe_ref[...].astype(BF16)
        dqg_ref[...] += dqg
        dkvg_ref[...] += dkvg

    def row(n):
        return pl.BlockSpec((tm, n), lambda i: (i, 0))

    def vec(n):
        return pl.BlockSpec((1, n), lambda i: (0, 0))

    return pl.pallas_call(
        body, name=name, grid=(t // tm,),
        in_specs=[row(w), vec(rq), vec(rkv), row(rq), row(rkv), row(LANE)],
        out_specs=[row(w), vec(rq), vec(rkv)],
        out_shape=[jax.ShapeDtypeStruct((t, w), BF16), jax.ShapeDtypeStruct((1, rq), F32),
                   jax.ShapeDtypeStruct((1, rkv), F32)],
        compiler_params=_cp(dimension_semantics=("arbitrary",)),
    )(down, qg, kvg, dqn, dkvn, dkpe)


ATT_TILE = 512
ATT_HEADS = 2
_NT = (((1,), (1,)), ((), ()))
_TN = (((0,), (0,)), ((), ()))


def _flash_fwd(name, qf, kf, v, heads):
    s = qf.shape[0]
    t = _tile(s, ATT_TILE)
    n = s // t
    g = min(ATT_HEADS, heads)
    qw, vw = HEAD_QK_PAD, HEAD_V

    def body(q_ref, k_ref, v_ref, o_ref, lse_ref, m_sc, l_sc, acc_sc):
        i, j = pl.program_id(1), pl.program_id(2)

        @pl.when(j == 0)
        def _():
            m_sc[...] = jnp.full(m_sc.shape, -jnp.inf, F32)
            l_sc[...] = jnp.zeros_like(l_sc)
            acc_sc[...] = jnp.zeros_like(acc_sc)

        def step(diag):
            for hh in range(g):
                sc = lax.dot_general(q_ref[:, hh * qw:(hh + 1) * qw], k_ref[:, hh * qw:(hh + 1) * qw], _NT,
                                     preferred_element_type=F32)
                if diag:
                    sc = jnp.where(_chunk_mask(t), sc, -jnp.inf)
                m_old = m_sc[hh]
                m_new = jnp.maximum(m_old, jnp.max(sc, axis=1, keepdims=True))
                alpha = jnp.exp(m_old - m_new)
                p = jnp.exp(sc - m_new)
                l_sc[hh] = alpha * l_sc[hh] + jnp.sum(p, axis=1, keepdims=True)
                acc_sc[hh] = alpha * acc_sc[hh] + jnp.dot(p.astype(BF16), v_ref[:, hh * vw:(hh + 1) * vw],
                                                          preferred_element_type=F32)
                m_sc[hh] = m_new

        @pl.when(j < i)
        def _():
            step(False)

        @pl.when(j == i)
        def _():
            step(True)
            for hh in range(g):
                l = l_sc[hh]
                o_ref[:, hh * vw:(hh + 1) * vw] = (acc_sc[hh] / l).astype(BF16)
                lse_ref[hh] = m_sc[hh] + jnp.log(l)

    return pl.pallas_call(
        body, name=name, grid=(heads // g, n, n),
        in_specs=[pl.BlockSpec((t, g * qw), lambda h, i, j: (i, h)),
                  pl.BlockSpec((t, g * qw), lambda h, i, j: (jnp.minimum(j, i), h)),
                  pl.BlockSpec((t, g * vw), lambda h, i, j: (jnp.minimum(j, i), h))],
        out_specs=[pl.BlockSpec((t, g * vw), lambda h, i, j: (i, h)),
                   pl.BlockSpec((g, t, 1), lambda h, i, j: (h, i, 0))],
        out_shape=[jax.ShapeDtypeStruct((s, heads * vw), BF16),
                   jax.ShapeDtypeStruct((heads, s, 1), F32)],
        scratch_shapes=[pltpu.VMEM((g, t, 1), F32), pltpu.VMEM((g, t, 1), F32), pltpu.VMEM((g, t, vw), F32)],
        compiler_params=_cp(dimension_semantics=("arbitrary", "arbitrary", "arbitrary")),
    )(qf, kf, v)


def _flash_bwd_dq(name, qf, kf, v, do, o, lse, cq, sq, heads):
    s = qf.shape[0]
    t = _tile(s, ATT_TILE)
    n = s // t
    g = min(ATT_HEADS, heads)
    qw, vw = HEAD_QK_PAD, HEAD_V

    def body(q_ref, k_ref, v_ref, do_ref, o_ref, lse_ref, c_ref, s_ref, dq_ref, dl_ref, acc_sc):
        i, j = pl.program_id(1), pl.program_id(2)

        @pl.when(j == 0)
        def _():
            acc_sc[...] = jnp.zeros_like(acc_sc)
            for hh in range(g):
                cols = slice(hh * vw, (hh + 1) * vw)
                dl_ref[hh] = jnp.sum(do_ref[:, cols].astype(F32) * o_ref[:, cols].astype(F32),
                                     axis=1, keepdims=True)

        def step(diag):
            for hh in range(g):
                k = k_ref[:, hh * qw:(hh + 1) * qw]
                do = do_ref[:, hh * vw:(hh + 1) * vw]
                sc = lax.dot_general(q_ref[:, hh * qw:(hh + 1) * qw], k, _NT, preferred_element_type=F32)
                p = jnp.exp(sc - lse_ref[hh])
                if diag:
                    p = jnp.where(_chunk_mask(t), p, 0.0)
                dp = lax.dot_general(do, v_ref[:, hh * vw:(hh + 1) * vw], _NT, preferred_element_type=F32)
                ds = (p * (dp - dl_ref[hh])).astype(BF16)
                acc_sc[hh] += jnp.dot(ds, k, preferred_element_type=F32)

        @pl.when(j < i)
        def _():
            step(False)

        @pl.when(j == i)
        def _():
            step(True)
            for hh in range(g):
                dq_ref[:, hh * qw:(hh + 1) * qw] = _rope_t(acc_sc[hh], c_ref[...], s_ref[...]).astype(BF16)

    qspec = pl.BlockSpec((t, g * qw), lambda h, i, j: (i, h))
    ospec = pl.BlockSpec((t, g * vw), lambda h, i, j: (i, h))
    vspec = pl.BlockSpec((g, t, 1), lambda h, i, j: (h, i, 0))
    tab = pl.BlockSpec((t, qw), lambda h, i, j: (i, 0))
    return pl.pallas_call(
        body, name=name, grid=(heads // g, n, n),
        in_specs=[qspec,
                  pl.BlockSpec((t, g * qw), lambda h, i, j: (jnp.minimum(j, i), h)),
                  pl.BlockSpec((t, g * vw), lambda h, i, j: (jnp.minimum(j, i), h)),
                  ospec, ospec, vspec, tab, tab],
        out_specs=[qspec, vspec],
        out_shape=[jax.ShapeDtypeStruct(qf.shape, BF16), jax.ShapeDtypeStruct((heads, s, 1), F32)],
        scratch_shapes=[pltpu.VMEM((g, t, qw), F32)],
        compiler_params=_cp(dimension_semantics=("arbitrary", "arbitrary", "arbitrary")),
    )(qf, kf, v, do, o, lse, cq, sq)


def _flash_bwd_dkv(name, qf, kf, v, do, lse, delta, ck, sk, heads):
    s = qf.shape[0]
    t = _tile(s, ATT_TILE)
    n = s // t
    g = min(ATT_HEADS, heads)
    qw, vw = HEAD_QK_PAD, HEAD_V
    lse_rows = lse.reshape(heads, 1, s)
    delta_rows = delta.reshape(heads, 1, s)

    def body(q_ref, k_ref, v_ref, do_ref, lse_ref, dl_ref, c_ref, s_ref, dkv_ref, dpe_ref, dk_sc, dv_sc):
        j, h, i = pl.program_id(0), pl.program_id(1), pl.program_id(2)

        @pl.when(i == 0)
        def _():
            dk_sc[...] = jnp.zeros_like(dk_sc)
            dv_sc[...] = jnp.zeros_like(dv_sc)

        def step(diag):
            for hh in range(g):
                q = q_ref[:, hh * qw:(hh + 1) * qw]
                do = do_ref[:, hh * vw:(hh + 1) * vw]
                sc = lax.dot_general(k_ref[:, hh * qw:(hh + 1) * qw], q, _NT, preferred_element_type=F32)
                p = jnp.exp(sc - lse_ref[hh])
                if diag:
                    p = jnp.where(_chunk_mask_t(t), p, 0.0)
                dv_sc[hh] += jnp.dot(p.astype(BF16), do, preferred_element_type=F32)
                dp = lax.dot_general(v_ref[:, hh * vw:(hh + 1) * vw], do, _NT, preferred_element_type=F32)
                ds = (p * (dp - dl_ref[hh])).astype(BF16)
                dk_sc[hh] += jnp.dot(ds, q, preferred_element_type=F32)

        @pl.when(i > j)
        def _():
            step(False)

        @pl.when(i == j)
        def _():
            step(True)

        @pl.when(i == n - 1)
        def _():
            pe = None
            for hh in range(g):
                dk = dk_sc[hh]
                dkv_ref[:, hh * qw:(hh + 1) * qw] = jnp.concatenate([dk[:, 0:HEAD_NOPE], dv_sc[hh]],
                                                                     axis=1).astype(BF16)
                part = dk[:, HEAD_NOPE:HEAD_QK_PAD]
                pe = part if pe is None else pe + part

            @pl.when(h == 0)
            def _():
                dpe_ref[...] = pe

            @pl.when(h > 0)
            def _():
                dpe_ref[...] += pe

            @pl.when(h == heads // g - 1)
            def _():
                dpe_ref[...] = _rope_t(dpe_ref[...], c_ref[...], s_ref[...])

    qrow = lambda j, h, i: (jnp.maximum(i, j), h)
    vrow = lambda j, h, i: (h, 0, jnp.maximum(i, j))
    return pl.pallas_call(
        body, name=name, grid=(n, heads // g, n),
        in_specs=[pl.BlockSpec((t, g * qw), qrow),
                  pl.BlockSpec((t, g * qw), lambda j, h, i: (j, h)),
                  pl.BlockSpec((t, g * vw), lambda j, h, i: (j, h)),
                  pl.BlockSpec((t, g * vw), qrow),
                  pl.BlockSpec((g, 1, t), vrow),
                  pl.BlockSpec((g, 1, t), vrow),
                  pl.BlockSpec((t, LANE), lambda j, h, i: (j, 0)),
                  pl.BlockSpec((t, LANE), lambda j, h, i: (j, 0))],
        out_specs=[pl.BlockSpec((t, g * (HEAD_NOPE + HEAD_V)), lambda j, h, i: (j, h)),
                   pl.BlockSpec((t, LANE), lambda j, h, i: (j, 0))],
        out_shape=[jax.ShapeDtypeStruct((s, heads * (HEAD_NOPE + HEAD_V)), BF16),
                   jax.ShapeDtypeStruct((s, LANE), F32)],
        scratch_shapes=[pltpu.VMEM((g, t, qw), F32), pltpu.VMEM((g, t, vw), F32)],
        compiler_params=_cp(dimension_semantics=("arbitrary", "arbitrary", "arbitrary")),
    )(qf, kf, v, do, lse_rows, delta_rows, ck, sk)


def _adamw(name, parts, w, m, v):
    p, r, c = parts.shape
    tr = _tile(r, max(8, (256 * 1024) // max(c, 1)))
    bc1 = 1.0 - ADAM_B1 ** ADAM_STEP
    bc2 = 1.0 - ADAM_B2 ** ADAM_STEP

    def body(p_ref, w_ref, m_ref, v_ref, g_ref, d_ref, nm_ref, nv_ref):
        g = p_ref[0].astype(F32)
        for q in range(1, p):
            g = g + p_ref[q].astype(F32)
        nm = ADAM_B1 * m_ref[...] + (1.0 - ADAM_B1) * g
        nv = ADAM_B2 * v_ref[...] + (1.0 - ADAM_B2) * (g * g)
        g_ref[...] = g
        nm_ref[...] = nm
        nv_ref[...] = nv
        d_ref[...] = -ADAM_LR * ((nm / bc1) / (jnp.sqrt(nv / bc2) + ADAM_EPS) + ADAM_WD * w_ref[...])

    blk = pl.BlockSpec((tr, c), lambda i: (i, 0))
    sh = jax.ShapeDtypeStruct((r, c), F32)
    return pl.pallas_call(
        body, name=name, grid=(r // tr,),
        in_specs=[pl.BlockSpec((p, tr, c), lambda i: (0, i, 0)), blk, blk, blk],
        out_specs=[blk] * 4, out_shape=[sh] * 4,
        compiler_params=_cp(),
    )(parts, w, m, v)


def _my_place():
    x, y, c = lax.axis_index("x"), lax.axis_index("y"), lax.axis_index("c")
    return x, y, c


def _flip(v, bit):
    return 1 - v if bit else v


def _block(ref, axis, idx, size):
    return ref.at[(slice(None),) * axis + (pl.ds(idx * size, size),)]


HBM_SPEC = pl.BlockSpec(memory_space=pltpu.HBM)
SEM_SPEC = pl.BlockSpec(memory_space=pltpu.SEMAPHORE)
DATAFLOW = pltpu.SideEffectType.DATAFLOW_SIDE_EFFECTING


def _hbm(a):
    return pltpu.with_memory_space_constraint(a, pltpu.HBM)


def _remote_copies(jobs, bufs, send_sems, recv_sems):
    return [pltpu.make_async_remote_copy(src_ref=src, dst_ref=dst, send_sem=send_sems.at[q],
                                         recv_sem=recv_sems.at[q], device_id=dev, device_id_type=MESH)
            for q, (src, dst, dev) in enumerate(jobs(bufs))]


def _split_start(name, bufs, jobs, n_jobs, after):
    nb = len(bufs)

    def body(*refs):
        send_sems, recv_sems = refs[nb + 1], refs[nb + 2]
        for cp in _remote_copies(jobs, refs[:nb], send_sems, recv_sems):
            cp.start()
        refs[-1][...] = jnp.zeros_like(refs[-1])

    outs = pl.pallas_call(
        body, name=name,
        out_shape=(pltpu.SemaphoreType.DMA((n_jobs,)), pltpu.SemaphoreType.DMA((n_jobs,)),
                   *[pltpu.HBM(b.shape, b.dtype) for b in bufs], jax.ShapeDtypeStruct((8, LANE), F32)),
        in_specs=[HBM_SPEC] * nb + [ANY],
        out_specs=(SEM_SPEC, SEM_SPEC, *[HBM_SPEC] * nb, VMEM_SPEC),
        input_output_aliases={q: 2 + q for q in range(nb)},
        compiler_params=pltpu.CompilerParams(has_side_effects=DATAFLOW),
    )(*[_hbm(b) for b in bufs], after)
    return outs[0], outs[1], list(outs[2:2 + nb]), outs[-1]


def _split_wait(name, bufs, send_sems, recv_sems, jobs, after):
    nb = len(bufs)

    def body(*refs):
        for cp in _remote_copies(jobs, refs[:nb], refs[nb], refs[nb + 1]):
            cp.wait_send()
            cp.wait_recv()

    outs = pl.pallas_call(
        body, name=name,
        out_shape=tuple(pltpu.HBM(b.shape, b.dtype) for b in bufs),
        in_specs=[HBM_SPEC] * nb + [SEM_SPEC, SEM_SPEC, ANY],
        out_specs=tuple([HBM_SPEC] * nb),
        input_output_aliases={q: q for q in range(nb)},
        compiler_params=pltpu.CompilerParams(has_side_effects=DATAFLOW),
    )(*bufs, send_sems, recv_sems, after)
    return list(outs)


PLACE_TILE_BYTES = 2 * 1024 * 1024


def _own_block_spec(tr, c, nblk, axis):
    if axis == 0:
        return pl.BlockSpec((tr, c), lambda i, me: (me[0] * nblk + i, 0))
    return pl.BlockSpec((tr, c), lambda i, me: (i, me[0]))


def _cast_place(name, w, layer, axis, me):
    _, r, c = w.shape
    tr = _tile(r, max(SUBLANE_BF16, PLACE_TILE_BYTES // (4 * c)))
    nblk = r // tr
    full = (N_DEV * r, c) if axis == 0 else (r, N_DEV * c)

    def body(me_ref, w_ref, o_ref):
        o_ref[...] = w_ref[...].astype(BF16)

    return pl.pallas_call(
        body, name=name,
        grid_spec=pltpu.PrefetchScalarGridSpec(
            num_scalar_prefetch=1, grid=(nblk,),
            in_specs=[pl.BlockSpec((None, tr, c), lambda i, me: (layer, i, 0))],
            out_specs=_own_block_spec(tr, c, nblk, axis)),
        out_shape=jax.ShapeDtypeStruct(full, BF16), compiler_params=_cp(),
    )(me, w)


def _own_place(name, grad, land, layer, axis, me):
    _, _, r, c = land.shape
    tr = _tile(r, max(SUBLANE_BF16, PLACE_TILE_BYTES // (2 * c)))
    nblk = r // tr

    def body(me_ref, g_ref, land_ref, o_ref):
        o_ref[...] = g_ref[...]

    return pl.pallas_call(
        body, name=name,
        grid_spec=pltpu.PrefetchScalarGridSpec(
            num_scalar_prefetch=1, grid=(nblk,),
            in_specs=[_own_block_spec(tr, c, nblk, axis), ANY],
            out_specs=pl.BlockSpec((None, None, tr, c), lambda i, me: (0, layer, i, 0))),
        out_shape=jax.ShapeDtypeStruct(land.shape, land.dtype),
        input_output_aliases={2: 0}, compiler_params=_cp(),
    )(me, grad, land)


def _gather_jobs_a(axes, sizes):
    def jobs(bufs):
        x, y, c = _my_place()
        out = []
        for t, buf in enumerate(bufs):
            blk = _block(buf, axes[t], 4 * x + 2 * y + c, sizes[t])
            for dev in [(x, y, 1 - c), (1 - x, y, c), (x, 1 - y, c), (1 - x, 1 - y, c)]:
                out.append((blk, blk, dev))
        return out
    return jobs


def _gather_jobs_b(axes, sizes):
    nt = len(axes)

    def jobs(bufs):
        x, y, c = _my_place()
        out = []
        for t in range(nt):
            for px, py in [(1 - x, y), (x, 1 - y), (1 - x, 1 - y)]:
                blk = _block(bufs[t], axes[t], 4 * px + 2 * py + c, sizes[t])
                out.append((blk, blk, (x, y, 1 - c)))
        return out
    return jobs


def _exchange_jobs(axes, sizes, layers):
    nt = len(axes)

    def jobs(bufs):
        x, y, c = _my_place()
        out = []
        for k in range(1, N_DEV):
            px, py, pc = _flip(x, k & 4), _flip(y, k & 2), _flip(c, k & 1)
            for t in range(nt):
                out.append((_block(bufs[t], axes[t], 4 * px + 2 * py + pc, sizes[t]),
                            bufs[nt + t].at[k, layers[t]], (px, py, pc)))
        return out
    return jobs


def _gather_begin(name, lands, axes, after):
    sizes = [b.shape[ax] // N_DEV for b, ax in zip(lands, axes)]
    jobs = _gather_jobs_a(axes, sizes)
    send, recv, bufs, token = _split_start(name + "_a", lands, jobs, 4 * len(lands), after)
    return dict(name=name, axes=axes, sizes=sizes, send=send, recv=recv, bufs=bufs, jobs=jobs), token


def _gather_mid(h, after):
    bufs = _split_wait(h["name"] + "_aw", h["bufs"], h["send"], h["recv"], h["jobs"], after)
    jobs = _gather_jobs_b(h["axes"], h["sizes"])
    send, recv, lands, token = _split_start(h["name"] + "_b", bufs, jobs, 3 * len(bufs), after)
    return dict(h, send=send, recv=recv, bufs=lands, jobs=jobs), token


def _gather_end(h, after):
    return _split_wait(h["name"] + "_bw", h["bufs"], h["send"], h["recv"], h["jobs"], after)


def _exchange_begin(name, grads, axes, lands, layers, me, after):
    sizes = [g.shape[ax] // N_DEV for g, ax in zip(grads, axes)]
    lands = [_own_place(f"{name}_place{t}", grads[t], lands[t], layers[t], axes[t], me)
             for t in range(len(grads))]
    jobs = _exchange_jobs(axes, sizes, layers)
    send, recv, bufs, token = _split_start(name + "_s", list(grads) + lands, jobs, 7 * len(grads), after)
    return dict(name=name, n=len(grads), send=send, recv=recv, bufs=bufs, jobs=jobs), token


def _exchange_end(h, after):
    bufs = _split_wait(h["name"] + "_w", h["bufs"], h["send"], h["recv"], h["jobs"], after)
    return bufs[h["n"]:]


def _all_gather_small(name, vec, reduce):
    r = vec.shape[0]

    def body(v_ref, o_ref, *rest):
        if reduce:
            buf, send_sems, recv_sems = rest
        else:
            buf = o_ref
            send_sems, recv_sems = rest
        x, y, c = _my_place()
        mine = 4 * x + 2 * y + c
        buf[mine] = v_ref[...]
        copies = []
        for k in range(1, N_DEV):
            px, py, pc = _flip(x, k & 4), _flip(y, k & 2), _flip(c, k & 1)
            cp = pltpu.make_async_remote_copy(
                src_ref=v_ref, dst_ref=buf.at[mine], send_sem=send_sems.at[k - 1],
                recv_sem=recv_sems.at[k - 1], device_id=(px, py, pc), device_id_type=MESH)
            cp.start()
            copies.append(cp)
        for cp in copies:
            cp.wait()
        if reduce:
            acc = buf[0]
            for q in range(1, N_DEV):
                acc = acc + buf[q]
            o_ref[...] = acc

    scratch = [pltpu.SemaphoreType.DMA((N_DEV - 1,)), pltpu.SemaphoreType.DMA((N_DEV - 1,))]
    if reduce:
        scratch = [pltpu.VMEM((N_DEV, r, LANE), F32)] + scratch
        out_shape = jax.ShapeDtypeStruct((r, LANE), F32)
    else:
        out_shape = jax.ShapeDtypeStruct((N_DEV, r, LANE), F32)
    return pl.pallas_call(
        body, name=name, in_specs=[VMEM_SPEC], out_specs=VMEM_SPEC, out_shape=out_shape,
        scratch_shapes=scratch, compiler_params=_cp(has_side_effects=True),
    )(vec)


def _pack(arrs, row_mult=8):
    flat = jnp.concatenate([a.reshape(-1).astype(F32) for a in arrs])
    n = flat.shape[0]
    rows = -(-n // LANE)
    rows = -(-rows // row_mult) * row_mult
    return jnp.pad(flat, (0, rows * LANE - n)).reshape(rows, LANE)


def _unpack(vec, shapes):
    flat = vec.reshape(-1)
    out, pos = [], 0
    for sh in shapes:
        n = 1
        for s in sh:
            n *= s
        out.append(flat[pos:pos + n].reshape(sh))
        pos += n
    return out


BIG = ["conv_w_pw1", "conv_w_pw2", "mla_w_in", "mla_w_q_up", "mla_w_kv_up", "mla_w_o", "mlp_w1", "mlp_w2"]
BIG_AXIS = {"conv_w_pw1": 2, "conv_w_pw2": 1, "mla_w_in": 1, "mla_w_q_up": 2, "mla_w_kv_up": 2,
            "mla_w_o": 1, "mlp_w1": 2, "mlp_w2": 1}
SMALL_SHARDED = ["conv_w_dw", "mla_q_norm_g", "mla_kv_norm_g"]
REPLICATED = ["norm_mixer_g", "norm_mlp_g", "conv_b_pw1", "conv_b_dw", "conv_ln_g", "conv_ln_b",
              "conv_b_pw2", "final_norm_g"]
WEIGHTS = ["norm_mixer_g", "norm_mlp_g", "conv_w_pw1", "conv_b_pw1", "conv_w_dw", "conv_b_dw",
           "conv_ln_g", "conv_ln_b", "conv_w_pw2", "conv_b_pw2", "mla_w_in", "mla_q_norm_g",
           "mla_kv_norm_g", "mla_w_q_up", "mla_w_kv_up", "mla_w_o", "mlp_w1", "mlp_w2", "final_norm_g"]


def _unshard_last(g, lead):
    nd = g.ndim
    perm = tuple(range(1, nd - 1)) + (0, nd - 1)
    return g.transpose(perm).reshape(lead + (N_DEV * g.shape[-1],))


def _step(w, m, v, x, positions, target):
    s, d = x.shape
    depth = w["norm_mixer_g"].shape[0]
    n_conv, n_mla = w["conv_w_pw1"].shape[0], w["mla_w_in"].shape[0]
    heads = (w["mla_w_q_up"].shape[-1] * N_DEV) // (HEAD_NOPE + HEAD_ROPE)
    rq, rkv = w["mla_w_q_up"].shape[1], w["mla_w_kv_up"].shape[1]
    xi, yi, ci = _my_place()
    mine = 4 * xi + 2 * yi + ci

    def mixer_units(layer):
        names = (["conv_w_pw1", "conv_w_pw2"] if layer % 2 == 0
                 else ["mla_w_in", "mla_w_q_up", "mla_w_kv_up", "mla_w_o"])
        return [(n, layer // 2) for n in names]

    def mlp_units(layer):
        return [("mlp_w1", layer), ("mlp_w2", layer)]

    me_arr = mine.astype(jnp.int32).reshape(1)

    def gather_begin(tag, units, after):
        lands = [_cast_place(f"{tag}_place_{n}", w[n], jl, BIG_AXIS[n] - 1, me_arr) for n, jl in units]
        h, token = _gather_begin(tag, lands, [BIG_AXIS[n] - 1 for n, _ in units], after)
        return dict(h, units=units), token

    full = {}

    def gather_end(h, after):
        full.update(zip(h["units"], _gather_end(h, after)))

    small_shapes = [w[n].shape for n in SMALL_SHARDED]
    gathered = _all_gather_small("gather_small", _pack([w[n] for n in SMALL_SHARDED]), False)

    first_a, tok = gather_begin("gather_0a", mixer_units(0), gathered)
    first_b, tok = gather_begin("gather_0b", mlp_units(0), tok)
    pending = {}
    if depth > 1:
        pending[1], tok = gather_begin("gather_1", mixer_units(1) + mlp_units(1), tok)
    first_a, tok = _gather_mid(first_a, tok)
    gather_end(first_a, tok)

    per_dev = [_unpack(gathered[q], small_shapes) for q in range(N_DEV)]
    w_dw = _unshard_last(jnp.stack([p[0] for p in per_dev]), (n_conv, CONV_W))
    q_gain = _unshard_last(jnp.stack([p[1] for p in per_dev]), (n_mla,))
    kv_gain = _unshard_last(jnp.stack([p[2] for p in per_dev]), (n_mla,))
    w_dw_pad = jnp.pad(w_dw, ((0, 0), (0, HALO - CONV_W), (0, 0)))

    w_in_cols = rq + rkv + HEAD_ROPE

    def pad_w_in(a):
        return jnp.pad(a, ((0, 0), (0, rq + rkv + LANE - w_in_cols)))

    def pad_wq(a):
        return jnp.pad(a.reshape(rq, heads, HEAD_NOPE + HEAD_ROPE),
                       ((0, 0), (0, 0), (0, HEAD_QK_PAD - HEAD_NOPE - HEAD_ROPE))).reshape(rq, heads * HEAD_QK_PAD)

    inv_freq = ROPE_THETA ** (-jnp.arange(0, HEAD_ROPE, 2, dtype=F32) / HEAD_ROPE)
    ang = positions.reshape(s).astype(F32)[:, None] * inv_freq
    cos, sin = jnp.cos(ang), jnp.sin(ang)
    c64 = jnp.concatenate([cos, cos], axis=1)
    s64 = jnp.concatenate([-sin, sin], axis=1)
    zeros64 = jnp.zeros((s, LANE - HEAD_ROPE), F32)
    ck = jnp.concatenate([c64, zeros64], axis=1)
    sk = jnp.concatenate([s64, zeros64], axis=1)
    scale = (HEAD_NOPE + HEAD_ROPE) ** -0.5
    cq = scale * jnp.concatenate([jnp.ones((s, HEAD_NOPE), F32), ck], axis=1)
    sq = scale * jnp.concatenate([jnp.zeros((s, HEAD_NOPE), F32), sk], axis=1)

    def vec(a):
        return a.reshape(1, -1)

    saved = []
    wpad = {}
    for layer in range(depth):
        jl = layer // 2
        h = _rms_fwd(f"rms_mixer_{layer}", x, vec(w["norm_mixer_g"][layer]) + tok[0, 0])
        if layer % 2 == 0:
            ua, ug, glu = _mm_glu(f"conv_pw1_{layer}", h, full["conv_w_pw1", jl], None, vec(w["conv_b_pw1"][jl]))
            cc, sw = _conv_fwd(f"conv_dw_{layer}", glu, w_dw_pad[jl], vec(w["conv_b_dw"][jl]),
                               vec(w["conv_ln_g"][jl]), vec(w["conv_ln_b"][jl]))
            x1 = _mm_res(f"conv_pw2_{layer}", sw, full["conv_w_pw2", jl], None, x, vec(w["conv_b_pw2"][jl]))
            mix = (h, ua, ug, glu, cc, sw)
        else:
            wpad["in", jl] = pad_w_in(full["mla_w_in", jl])
            wpad["q", jl] = pad_wq(full["mla_w_q_up", jl])
            down = _mm_plain(f"mla_down_{layer}", h, wpad["in", jl], None, "nn", F32)
            qn, kvn, kpe = _mla_mid_fwd(f"mla_mid_{layer}", down, vec(q_gain[jl]), vec(kv_gain[jl]), ck, sk)
            qf = _mm_q(f"mla_q_{layer}", qn, wpad["q", jl], None, cq, sq)
            kf, vv = _mm_kv(f"mla_kv_{layer}", kvn, full["mla_w_kv_up", jl], None, kpe)
            o, lse = _flash_fwd(f"mla_attn_{layer}", qf, kf, vv, heads)
            x1 = _mm_res(f"mla_out_{layer}", o, full["mla_w_o", jl], None, x)
            mix = (h, down, qn, kvn, qf, kf, vv, o, lse)
        if layer == 0:
            first_b, tok = _gather_mid(first_b, x1)
        if layer + 2 < depth:
            pending[layer + 2], tok = gather_begin(f"gather_{layer + 2}",
                                                   mixer_units(layer + 2) + mlp_units(layer + 2),
                                                   tok if layer == 0 else x1)
        if layer == 0:
            gather_end(first_b, tok)
        elif layer + 1 < depth:
            pending[layer + 1], tok = _gather_mid(pending[layer + 1], x1)
        h2 = _rms_fwd(f"rms_mlp_{layer}", x1, vec(w["norm_mlp_g"][layer]) + tok[0, 0])
        z, a = _mm_mlp_up(f"mlp_up_{layer}", h2, full["mlp_w1", layer], None)
        x2 = _mm_res(f"mlp_down_{layer}", a, full["mlp_w2", layer], None, x1)
        if layer + 1 < depth:
            if layer == 0:
                pending[1], tok = _gather_mid(pending[1], x2)
                gather_end(pending[1], tok)
            else:
                gather_end(pending[layer + 1], x2)
        saved.append((x, mix, x1, h2, z, a))
        x = x2

    loss_row, g, gb, d_final, _ = _final_loss("final_loss", x, vec(w["final_norm_g"]) + tok[0, 0], target)

    recv = {n: lax.empty((N_DEV,) + w[n].shape, BF16) for n in BIG}

    def exchange_begin(tag, items, after):
        names = [n for n, _, _ in items]
        h, token = _exchange_begin(tag, [gr for _, _, gr in items], [BIG_AXIS[n] - 1 for n in names],
                                   [recv[n] for n in names], [jl for _, jl, _ in items], me_arr, after)
        return dict(h, names=names), token

    def exchange_end(h, after):
        recv.update(zip(h["names"], _exchange_end(h, after)))

    mix_exchange = None
    d_mixer, d_mlp = [None] * depth, [None] * depth
    d_small = {n: [None] * n_conv for n in ["conv_b_pw1", "conv_w_dw", "conv_b_dw", "conv_ln_g",
                                           "conv_ln_b", "conv_b_pw2"]}
    d_qg, d_kvg = [None] * n_mla, [None] * n_mla
    for layer in reversed(range(depth)):
        jl = layer // 2
        x0, mix, x1, h2, z, a = saved[layer]
        colsum_g = None
        dz = _mm_mlp_dz(f"mlp_dz_{layer}", gb, full["mlp_w2", layer], None, z)
        dw2 = _mm_wgrad(f"mlp_dw2_{layer}", a, gb)
        dh2 = _mm_plain(f"mlp_dh_{layer}", dz, full["mlp_w1", layer], None, "nt", F32)
        dw1 = _mm_wgrad(f"mlp_dw1_{layer}", h2, dz)
        mlp_exchange, tok = exchange_begin(f"exchange_mlp_{layer}",
                                           [("mlp_w1", layer, dw1), ("mlp_w2", layer, dw2)], dh2)
        g, gb, d_mlp[layer], colsum_g = _rms_bwd(f"rms_mlp_bwd_{layer}", x1,
                                                 vec(w["norm_mlp_g"][layer]) + tok[0, 0], dh2, g)
        if mix_exchange is not None:
            exchange_end(mix_exchange, g)
        if layer % 2 == 0:
            h, ua, ug, glu, cc, sw = mix
            d_small["conv_b_pw2"][jl] = colsum_g.reshape(-1)
            dsw = _mm_plain(f"conv_ds_{layer}", gb, full["conv_w_pw2", jl], None, "nt", F32)
            dwp2 = _mm_wgrad(f"conv_dw2_{layer}", sw, gb)
            dc, dlg, dlb, dbdw = _conv_bwd_ln(f"conv_ln_bwd_{layer}", dsw, cc, vec(w["conv_ln_g"][jl]),
                                              vec(w["conv_ln_b"][jl]))
            du, dwdw, dbu = _conv_bwd_dw(f"conv_dw_bwd_{layer}", dc, glu, ua, ug, w_dw_pad[jl])
            d_small["conv_ln_g"][jl] = dlg.reshape(-1)
            d_small["conv_ln_b"][jl] = dlb.reshape(-1)
            d_small["conv_b_dw"][jl] = dbdw.reshape(-1)
            d_small["conv_w_dw"][jl] = dwdw[:CONV_W]
            d_small["conv_b_pw1"][jl] = dbu.reshape(-1)
            dh = _mm_plain(f"conv_dh_{layer}", du, full["conv_w_pw1", jl], None, "nt", F32)
            dwp1 = _mm_wgrad(f"conv_dw1_{layer}", h, du)
            items = [("conv_w_pw1", jl, dwp1), ("conv_w_pw2", jl, dwp2)]
        else:
            h, down, qn, kvn, qf, kf, vv, o, lse = mix
            do = _mm_plain(f"mla_do_{layer}", gb, full["mla_w_o", jl], None, "nt", BF16)
            dwo = _mm_wgrad(f"mla_dwo_{layer}", o, gb)
            dq, delta = _flash_bwd_dq(f"mla_attn_dq_{layer}", qf, kf, vv, do, o, lse, cq, sq, heads)
            dkv, dkpe = _flash_bwd_dkv(f"mla_attn_dkv_{layer}", qf, kf, vv, do, lse, delta, ck, sk, heads)
            dqn = _mm_plain(f"mla_dqn_{layer}", dq, wpad["q", jl], None, "nt", F32)
            dwq = _mm_wgrad(f"mla_dwq_{layer}", qn, dq).reshape(rq, heads, HEAD_QK_PAD)[
                :, :, :HEAD_NOPE + HEAD_ROPE].reshape(rq, heads * (HEAD_NOPE + HEAD_ROPE))
            dkvn = _mm_plain(f"mla_dkvn_{layer}", dkv, full["mla_w_kv_up", jl], None, "nt", F32)
            dwkv = _mm_wgrad(f"mla_dwkv_{layer}", kvn, dkv)
            ddown, d_qg[jl], d_kvg[jl] = _mla_mid_bwd(f"mla_mid_bwd_{layer}", down, vec(q_gain[jl]),
                                                      vec(kv_gain[jl]), dqn, dkvn, dkpe)
            dh = _mm_plain(f"mla_dh_{layer}", ddown, wpad["in", jl], None, "nt", F32)
            dwin = _mm_wgrad(f"mla_dwin_{layer}", h, ddown)[:, :w_in_cols]
            items = [("mla_w_in", jl, dwin), ("mla_w_q_up", jl, dwq), ("mla_w_kv_up", jl, dwkv),
                     ("mla_w_o", jl, dwo)]
        mix_exchange, tok = exchange_begin(f"exchange_mix_{layer}", items, dh)
        g, gb, d_mixer[layer], _ = _rms_bwd(f"rms_mixer_bwd_{layer}", x0,
                                            vec(w["norm_mixer_g"][layer]) + tok[0, 0], dh, g)
        exchange_end(mlp_exchange, g)
    exchange_end(mix_exchange, g)
    grad_x = g

    out = {}
    for n in BIG:
        sh = w[n].shape
        r, c = sh[0] * sh[1], sh[2]
        res = _adamw(f"adamw_{n}", recv[n].reshape(N_DEV, r, c), w[n].reshape(r, c),
                     m[n].reshape(r, c), v[n].reshape(r, c))
        out[n] = [t.reshape(sh) for t in res]

    small_full = {
        "norm_mixer_g": jnp.concatenate(d_mixer, axis=0), "norm_mlp_g": jnp.concatenate(d_mlp, axis=0),
        "conv_b_pw1": jnp.stack(d_small["conv_b_pw1"]), "conv_b_dw": jnp.stack(d_small["conv_b_dw"]),
        "conv_ln_g": jnp.stack(d_small["conv_ln_g"]), "conv_ln_b": jnp.stack(d_small["conv_ln_b"]),
        "conv_b_pw2": jnp.stack(d_small["conv_b_pw2"]), "final_norm_g": d_final.reshape(-1),
        "conv_w_dw": jnp.stack(d_small["conv_w_dw"]),
        "mla_q_norm_g": jnp.concatenate(d_qg, axis=0), "mla_kv_norm_g": jnp.concatenate(d_kvg, axis=0),
    }
    names = REPLICATED + SMALL_SHARDED
    summed = _unpack(_all_gather_small("reduce_small", _pack([small_full[n] for n in names]), True),
                     [small_full[n].shape for n in names])
    summed = dict(zip(names, summed))
    for n in SMALL_SHARDED:
        width = w[n].shape[-1]
        summed[n] = lax.dynamic_slice_in_dim(summed[n], mine * width, width, axis=summed[n].ndim - 1)
    for group, tag in ((REPLICATED, "replicated"), (SMALL_SHARDED, "small_sharded")):
        shapes = [w[n].shape for n in group]
        res = _adamw(f"adamw_{tag}", _pack([summed[n] for n in group])[None],
                     _pack([w[n] for n in group]), _pack([m[n] for n in group]), _pack([v[n] for n in group]))
        unpacked = [_unpack(t, shapes) for t in res]
        for q, n in enumerate(group):
            out[n] = [unpacked[0][q], unpacked[1][q], unpacked[2][q], unpacked[3][q]]

    loss = lax.psum(loss_row[0, 0], ("x", "y", "c"))
    return loss, grad_x, out


def kernel(x, positions, norm_mixer_g, norm_mlp_g, conv_w_pw1, conv_b_pw1, conv_w_dw, conv_b_dw, conv_ln_g, conv_ln_b, conv_w_pw2, conv_b_pw2, mla_w_in, mla_q_norm_g, mla_kv_norm_g, mla_w_q_up, mla_w_kv_up, mla_w_o, mlp_w1, mlp_w2, final_norm_g, loss_target, m_norm_mixer_g, m_norm_mlp_g, m_conv_w_pw1, m_conv_b_pw1, m_conv_w_dw, m_conv_b_dw, m_conv_ln_g, m_conv_ln_b, m_conv_w_pw2, m_conv_b_pw2, m_mla_w_in, m_mla_q_norm_g, m_mla_kv_norm_g, m_mla_w_q_up, m_mla_w_kv_up, m_mla_w_o, m_mlp_w1, m_mlp_w2, m_final_norm_g, v_norm_mixer_g, v_norm_mlp_g, v_conv_w_pw1, v_conv_b_pw1, v_conv_w_dw, v_conv_b_dw, v_conv_ln_g, v_conv_ln_b, v_conv_w_pw2, v_conv_b_pw2, v_mla_w_in, v_mla_q_norm_g, v_mla_kv_norm_g, v_mla_w_q_up, v_mla_w_kv_up, v_mla_w_o, v_mlp_w1, v_mlp_w2, v_final_norm_g):
    ws = (norm_mixer_g, norm_mlp_g, conv_w_pw1, conv_b_pw1, conv_w_dw, conv_b_dw, conv_ln_g, conv_ln_b,
          conv_w_pw2, conv_b_pw2, mla_w_in, mla_q_norm_g, mla_kv_norm_g, mla_w_q_up, mla_w_kv_up, mla_w_o,
          mlp_w1, mlp_w2, final_norm_g)
    ms = (m_norm_mixer_g, m_norm_mlp_g, m_conv_w_pw1, m_conv_b_pw1, m_conv_w_dw, m_conv_b_dw, m_conv_ln_g,
          m_conv_ln_b, m_conv_w_pw2, m_conv_b_pw2, m_mla_w_in, m_mla_q_norm_g, m_mla_kv_norm_g,
          m_mla_w_q_up, m_mla_w_kv_up, m_mla_w_o, m_mlp_w1, m_mlp_w2, m_final_norm_g)
    vs = (v_norm_mixer_g, v_norm_mlp_g, v_conv_w_pw1, v_conv_b_pw1, v_conv_w_dw, v_conv_b_dw, v_conv_ln_g,
          v_conv_ln_b, v_conv_w_pw2, v_conv_b_pw2, v_mla_w_in, v_mla_q_norm_g, v_mla_kv_norm_g,
          v_mla_w_q_up, v_mla_w_kv_up, v_mla_w_o, v_mlp_w1, v_mlp_w2, v_final_norm_g)
    w, m, v = dict(zip(WEIGHTS, ws)), dict(zip(WEIGHTS, ms)), dict(zip(WEIGHTS, vs))
    s, d = x.shape[-2], x.shape[-1]
    loss, grad_x, out = _step(w, m, v, x.reshape(s, d), positions, loss_target.reshape(s, d))
    grads = [out[n][0] for n in WEIGHTS]
    deltas = [out[n][1] for n in WEIGHTS]
    new_m = [out[n][2] for n in WEIGHTS]
    new_v = [out[n][3] for n in WEIGHTS]
    return (loss, grad_x.reshape(x.shape), *grads, *deltas, *new_m, *new_v)
```

```python
import functools

import jax
import jax.numpy as jnp
from jax import lax
from jax.experimental import pallas as pl
from jax.experimental.pallas import tpu as pltpu

F32 = jnp.float32
BF16 = jnp.bfloat16

NORM_EPS = 1e-6
LN_EPS = 1e-5
ROPE_THETA = 10000.0
CHUNK_BITS = 6
HEAD_NOPE = 128
HEAD_ROPE = 64
HEAD_V = 128
HEAD_QK_PAD = 256
CONV_W = 31
HALO = 32
N_DEV = 8

ADAM_LR = 0.001
ADAM_B1 = 0.9
ADAM_B2 = 0.999
ADAM_EPS = 1e-08
ADAM_WD = 0.01
ADAM_STEP = 10

V7X_VMEM_BYTES = 64 * 1024 * 1024
VMEM_LIMIT = (V7X_VMEM_BYTES * 3) // 4
LANE = 128

MESH = pl.DeviceIdType.MESH
ANY = pl.BlockSpec(memory_space=pl.ANY)
VMEM_SPEC = pl.BlockSpec(memory_space=pltpu.VMEM)


def _cp(**kw):
    return pltpu.CompilerParams(vmem_limit_bytes=VMEM_LIMIT, **kw)


SUBLANE_BF16 = 16

TM_PREF = 1024
TN_PREF = 1024
TK_PREF = 2048


def _tile(n, pref, mult=SUBLANE_BF16):
    if n <= pref + pref // 2:
        return n
    t = (pref // mult) * mult
    while t >= mult:
        if n % t == 0:
            return t
        t -= mult
    return n


def _sigmoid(x):
    return 1.0 / (1.0 + jnp.exp(-x))


def _rot_half(x):
    n = x.shape[-1]
    lane = lax.broadcasted_iota(jnp.int32, x.shape, x.ndim - 1)
    first = (lane & 63) < 32
    return jnp.where(first, pltpu.roll(x, n - 32, x.ndim - 1), pltpu.roll(x, 32, x.ndim - 1))


def _rope(x, c, s):
    return x * c + _rot_half(x) * s


def _rope_t(d, c, s):
    return d * c + _rot_half(d * s)


def _chunk_mask_t(t):
    row = lax.broadcasted_iota(jnp.int32, (t, t), 0)
    col = lax.broadcasted_iota(jnp.int32, (t, t), 1)
    return jnp.right_shift(row, CHUNK_BITS) <= jnp.right_shift(col, CHUNK_BITS)


def _rms_fwd(name, x, g):
    t, d = x.shape
    tm = _tile(t, 512)

    def body(x_ref, g_ref, o_ref):
        xf = x_ref[...]
        r = lax.rsqrt(jnp.mean(xf * xf, axis=-1, keepdims=True) + NORM_EPS)
        o_ref[...] = (xf * r * g_ref[...]).astype(o_ref.dtype)

    return pl.pallas_call(
        body, name=name, grid=(t // tm,),
        in_specs=[pl.BlockSpec((tm, d), lambda i: (i, 0)), pl.BlockSpec((1, d), lambda i: (0, 0))],
        out_specs=pl.BlockSpec((tm, d), lambda i: (i, 0)),
        out_shape=jax.ShapeDtypeStruct((t, d), BF16),
        compiler_params=_cp(),
    )(x, g)


def _rms_bwd_math(xf, g, dy):
    r = lax.rsqrt(jnp.mean(xf * xf, axis=-1, keepdims=True) + NORM_EPS)
    xh = xf * r
    dg = jnp.sum(dy * xh, axis=0, keepdims=True)
    dxh = dy * g
    dx = r * (dxh - xh * jnp.mean(dxh * xh, axis=-1, keepdims=True))
    return dx, dg


def _rms_bwd(name, x, g, dy, resid):
    t, d = x.shape
    tm = _tile(t, 256)

    def body(x_ref, g_ref, dy_ref, r_ref, dx_ref, dxb_ref, dg_ref, cs_ref):
        @pl.when(pl.program_id(0) == 0)
        def _():
            dg_ref[...] = jnp.zeros_like(dg_ref)
            cs_ref[...] = jnp.zeros_like(cs_ref)

        dx, dg = _rms_bwd_math(x_ref[...], g_ref[...], dy_ref[...])
        tot = r_ref[...] + dx
        dx_ref[...] = tot
        dxb_ref[...] = tot.astype(BF16)
        dg_ref[...] += dg
        cs_ref[...] += jnp.sum(tot, axis=0, keepdims=True)

    row = pl.BlockSpec((tm, d), lambda i: (i, 0))
    vec = pl.BlockSpec((1, d), lambda i: (0, 0))
    return pl.pallas_call(
        body, name=name, grid=(t // tm,),
        in_specs=[row, vec, row, row],
        out_specs=[row, row, vec, vec],
        out_shape=[jax.ShapeDtypeStruct((t, d), F32), jax.ShapeDtypeStruct((t, d), BF16),
                   jax.ShapeDtypeStruct((1, d), F32), jax.ShapeDtypeStruct((1, d), F32)],
        compiler_params=_cp(dimension_semantics=("arbitrary",)),
    )(x, g, dy, resid)


def _final_loss(name, x, g, target):
    t, d = x.shape
    tm = _tile(t, 256)

    def body(x_ref, g_ref, t_ref, loss_ref, dx_ref, dxb_ref, dg_ref, cs_ref):
        @pl.when(pl.program_id(0) == 0)
        def _():
            loss_ref[...] = jnp.zeros_like(loss_ref)
            dg_ref[...] = jnp.zeros_like(dg_ref)
            cs_ref[...] = jnp.zeros_like(cs_ref)

        xf = x_ref[...]
        gg = g_ref[...]
        r = lax.rsqrt(jnp.mean(xf * xf, axis=-1, keepdims=True) + NORM_EPS)
        err = xf * r * gg - t_ref[...]
        part = 0.5 * jnp.sum(jnp.mean(err * err, axis=-1, keepdims=True), axis=0, keepdims=True)
        loss_ref[...] += jnp.broadcast_to(part, loss_ref.shape)
        dx, dg = _rms_bwd_math(xf, gg, err * (1.0 / d))
        dx_ref[...] = dx
        dxb_ref[...] = dx.astype(BF16)
        dg_ref[...] += dg
        cs_ref[...] += jnp.sum(dx, axis=0, keepdims=True)

    row = pl.BlockSpec((tm, d), lambda i: (i, 0))
    vec = pl.BlockSpec((1, d), lambda i: (0, 0))
    return pl.pallas_call(
        body, name=name, grid=(t // tm,),
        in_specs=[row, vec, row],
        out_specs=[pl.BlockSpec((1, LANE), lambda i: (0, 0)), row, row, vec, vec],
        out_shape=[jax.ShapeDtypeStruct((1, LANE), F32), jax.ShapeDtypeStruct((t, d), F32),
                   jax.ShapeDtypeStruct((t, d), BF16), jax.ShapeDtypeStruct((1, d), F32),
                   jax.ShapeDtypeStruct((1, d), F32)],
        compiler_params=_cp(dimension_semantics=("arbitrary",)),
    )(x, g, target)


_DIMS = {
    "nn": (((1,), (0,)), ((), ())),
    "nt": (((1,), (1,)), ((), ())),
    "tn": (((0,), (0,)), ((), ())),
}


def _mm(name, a, bs, *, mode, m, n, k, epilogue, out_shape, out_specs, extras=(), extra_specs=(),
        a_lead=None, aliases=None, tn_div=1):
    tm, tn, tk = _tiles(m, n, k, tn_div)
    nk = k // tk
    nb, ne = len(bs), len(extras)
    no = len(out_shape)
    dims = _DIMS[mode]

    def with_lead(shape, idx, lead):
        if lead is None:
            return pl.BlockSpec(shape, idx)
        return pl.BlockSpec((None,) + shape, lambda i, j, kk: (lead,) + idx(i, j, kk))

    if mode == "tn":
        a_spec = with_lead((tk, tm), lambda i, j, kk: (kk, i), a_lead)
    else:
        a_spec = with_lead((tm, tk), lambda i, j, kk: (i, kk), a_lead)
    b_specs = []
    for _, lead, off in bs:
        if mode == "nt":
            b_specs.append(with_lead((tn, tk), lambda i, j, kk, off=off: (j + off, kk), lead))
        else:
            b_specs.append(with_lead((tk, tn), lambda i, j, kk, off=off: (kk, j + off), lead))

    def body(*refs):
        a_ref = refs[0]
        b_refs = refs[1:1 + nb]
        ex = refs[1 + nb:1 + nb + ne]
        outs = refs[1 + nb + ne:1 + nb + ne + no]
        accs = refs[1 + nb + ne + no:]

        def part(b_ref):
            return lax.dot_general(a_ref[...], b_ref[...], dims, preferred_element_type=F32)

        if nk == 1:
            epilogue([part(b_ref) for b_ref in b_refs], ex, outs)
            return
        kk = pl.program_id(2)

        @pl.when(kk == 0)
        def _():
            for acc, b_ref in zip(accs, b_refs):
                acc[...] = part(b_ref)

        @pl.when(kk > 0)
        def _():
            for acc, b_ref in zip(accs, b_refs):
                acc[...] += part(b_ref)

        @pl.when(kk == nk - 1)
        def _():
            epilogue([acc[...] for acc in accs], ex, outs)

    scratch = [pltpu.VMEM((tm, tn), F32) for _ in range(nb)] if nk > 1 else []
    return pl.pallas_call(
        body, name=name, grid=(m // tm, n // tn, nk),
        in_specs=[a_spec] + b_specs + list(extra_specs),
        out_specs=list(out_specs), out_shape=list(out_shape), scratch_shapes=scratch,
        input_output_aliases=aliases or {},
        compiler_params=_cp(dimension_semantics=("arbitrary", "arbitrary", "arbitrary")),
    )(a, *[b for b, _, _ in bs], *extras), (tm, tn, tk)


def _ij(tm, tn):
    return pl.BlockSpec((tm, tn), lambda i, j, kk: (i, j))


def _tiles(m, n, k, tn_div=1):
    return _tile(m, TM_PREF), _tile(n, TN_PREF // tn_div, LANE), _tile(k, TK_PREF, LANE)


def _mm_plain(name, a, b, b_lead, mode, out_dtype):
    m, k = a.shape
    n = b.shape[-1] if mode == "nn" else b.shape[-2]
    tm, tn, _ = _tiles(m, n, k)

    def epilogue(accs, ex, outs):
        outs[0][...] = accs[0].astype(out_dtype)

    return _mm(name, a, [(b, b_lead, 0)], mode=mode, m=m, n=n, k=k, epilogue=epilogue,
               out_shape=[jax.ShapeDtypeStruct((m, n), out_dtype)], out_specs=[_ij(tm, tn)])[0][0]


def _mm_res(name, a, b, b_lead, resid, bias=None):
    m, k = a.shape
    n = b.shape[-1]
    tm, tn, _ = _tiles(m, n, k)
    extras, specs = [resid], [_ij(tm, tn)]
    if bias is not None:
        extras.append(bias)
        specs.append(pl.BlockSpec((1, tn), lambda i, j, kk: (0, j)))

    def epilogue(accs, ex, outs):
        y = ex[0][...] + accs[0]
        if bias is not None:
            y = y + ex[1][...]
        outs[0][...] = y

    return _mm(name, a, [(b, b_lead, 0)], mode="nn", m=m, n=n, k=k, epilogue=epilogue,
               extras=extras, extra_specs=specs,
               out_shape=[jax.ShapeDtypeStruct((m, n), F32)], out_specs=[_ij(tm, tn)])[0][0]


def _mm_mlp_up(name, h, w1, lead):
    m, k = h.shape
    n = w1.shape[-1]
    tm, tn, _ = _tiles(m, n, k)

    def epilogue(accs, ex, outs):
        z = accs[0]
        outs[0][...] = z.astype(BF16)
        r = jnp.maximum(z, 0.0)
        outs[1][...] = (r * r).astype(BF16)

    sh = jax.ShapeDtypeStruct((m, n), BF16)
    return _mm(name, h, [(w1, lead, 0)], mode="nn", m=m, n=n, k=k, epilogue=epilogue,
               out_shape=[sh, sh], out_specs=[_ij(tm, tn), _ij(tm, tn)])[0]


def _mm_mlp_dz(name, g, w2, lead, z):
    m, k = g.shape
    n = w2.shape[-2]
    tm, tn, _ = _tiles(m, n, k)

    def epilogue(accs, ex, outs):
        outs[0][...] = (accs[0] * (2.0 * jnp.maximum(ex[0][...].astype(F32), 0.0))).astype(BF16)

    return _mm(name, g, [(w2, lead, 0)], mode="nt", m=m, n=n, k=k, epilogue=epilogue,
               extras=[z], extra_specs=[_ij(tm, tn)],
               out_shape=[jax.ShapeDtypeStruct((m, n), BF16)], out_specs=[_ij(tm, tn)])[0][0]


def _mm_glu(name, h, w, lead, bias):
    m, k = h.shape
    n = w.shape[-1] // 2
    tm, tn, _ = _tiles(m, n, k, 2)
    off = n // tn

    def epilogue(accs, ex, outs):
        a = accs[0] + ex[0][...]
        gate = accs[1] + ex[1][...]
        outs[0][...] = a.astype(BF16)
        outs[1][...] = gate.astype(BF16)
        outs[2][...] = a * _sigmoid(gate)

    shb = jax.ShapeDtypeStruct((m, n), BF16)
    return _mm(name, h, [(w, lead, 0), (w, lead, off)], mode="nn", m=m, n=n, k=k, epilogue=epilogue,
               extras=[bias, bias],
               extra_specs=[pl.BlockSpec((1, tn), lambda i, j, kk: (0, j)),
                            pl.BlockSpec((1, tn), lambda i, j, kk: (0, j + off))],
               out_shape=[shb, shb, jax.ShapeDtypeStruct((m, n), F32)],
               out_specs=[_ij(tm, tn)] * 3, tn_div=2)[0]


def _mm_q(name, qn, wq_pad, lead, cq, sq):
    m, k = qn.shape
    n = wq_pad.shape[-1]
    tm, tn, _ = _tiles(m, n, k)
    rep = tn // HEAD_QK_PAD

    def epilogue(accs, ex, outs):
        c = jnp.tile(ex[0][...], (1, rep))
        s = jnp.tile(ex[1][...], (1, rep))
        outs[0][...] = _rope(accs[0], c, s).astype(BF16)

    tab = pl.BlockSpec((tm, HEAD_QK_PAD), lambda i, j, kk: (i, 0))
    return _mm(name, qn, [(wq_pad, lead, 0)], mode="nn", m=m, n=n, k=k, epilogue=epilogue,
               extras=[cq, sq], extra_specs=[tab, tab],
               out_shape=[jax.ShapeDtypeStruct((m, n), BF16)], out_specs=[_ij(tm, tn)])[0][0]


def _mm_kv(name, kvn, wkv, lead, kpe):
    m, k = kvn.shape
    n = wkv.shape[-1]
    tm, tn, _ = _tiles(m, n, k)
    heads = tn // (HEAD_NOPE + HEAD_V)

    def epilogue(accs, ex, outs):
        acc = accs[0]
        pe = ex[0][...].astype(F32)
        kparts, vparts = [], []
        for hh in range(heads):
            base = hh * (HEAD_NOPE + HEAD_V)
            kparts += [acc[:, base:base + HEAD_NOPE], pe]
            vparts.append(acc[:, base + HEAD_NOPE:base + HEAD_NOPE + HEAD_V])
        kf = jnp.concatenate(kparts, axis=1)
        vv = jnp.concatenate(vparts, axis=1) if heads > 1 else vparts[0]
        outs[0][...] = kf.astype(BF16)
        outs[1][...] = vv.astype(BF16)
        outs[2][...] = kf.T.astype(BF16)
        outs[3][...] = vv.T.astype(BF16)

    def ji(tn_, tm_):
        return pl.BlockSpec((tn_, tm_), lambda i, j, kk: (j, i))

    return _mm(name, kvn, [(wkv, lead, 0)], mode="nn", m=m, n=n, k=k, epilogue=epilogue,
               extras=[kpe], extra_specs=[pl.BlockSpec((tm, LANE), lambda i, j, kk: (i, 0))],
               out_shape=[jax.ShapeDtypeStruct((m, n), BF16), jax.ShapeDtypeStruct((m, n // 2), BF16),
                          jax.ShapeDtypeStruct((n, m), BF16), jax.ShapeDtypeStruct((n // 2, m), BF16)],
               out_specs=[_ij(tm, tn), _ij(tm, tn // 2), ji(tn, tm), ji(tn // 2, tm)])[0]


def _mm_wgrad(name, a, b):
    t, m = a.shape
    n = b.shape[-1]
    tm, tn, _ = _tiles(m, n, t)

    def epilogue(accs, ex, outs):
        outs[0][...] = accs[0].astype(BF16)

    return _mm(name, a, [(b, None, 0)], mode="tn", m=m, n=n, k=t, epilogue=epilogue,
               out_shape=[jax.ShapeDtypeStruct((m, n), BF16)], out_specs=[_ij(tm, tn)])[0][0]


CONV_ROWS = 256
CONV_RT = 64
CONV_CW = 256
CONV_LR = 32


def _ln_stats(c):
    mu = jnp.mean(c, axis=-1, keepdims=True)
    xc = c - mu
    rstd = lax.rsqrt(jnp.mean(xc * xc, axis=-1, keepdims=True) + LN_EPS)
    return xc * rstd, rstd


def _conv_fwd(name, glu, w_dw, b_dw, ln_g, ln_b):
    t, d = glu.shape
    tt = _tile(t, CONV_ROWS)
    rt, cw, lr = min(CONV_RT, tt), min(CONV_CW, d), min(CONV_LR, tt)
    hb = tt // HALO

    def body(gc_ref, gp_ref, w_ref, b_ref, lg_ref, lb_ref, c_ref, s_ref, buf):
        i = pl.program_id(0)
        buf[0:HALO, :] = jnp.where(i > 0, gp_ref[...], 0.0)
        buf[HALO:HALO + tt, :] = gc_ref[...]

        def chunk(cb, carry):
            col = pl.ds(pl.multiple_of(cb * cw, cw), cw)
            for r0 in range(0, tt, rt):
                acc = jnp.broadcast_to(b_ref[:, col], (rt, cw))
                for k in range(CONV_W):
                    lo = r0 + HALO - (CONV_W - 1) + k
                    acc = acc + w_ref[k:k + 1, col] * buf[lo:lo + rt, col]
                c_ref[r0:r0 + rt, col] = acc
            return carry

        lax.fori_loop(0, d // cw, chunk, 0)

        def ln(r, carry):
            rows = pl.ds(pl.multiple_of(r * lr, lr), lr)
            xh, _ = _ln_stats(c_ref[rows, :])
            y = xh * lg_ref[...] + lb_ref[...]
            s_ref[rows, :] = (y * _sigmoid(y)).astype(BF16)
            return carry

        lax.fori_loop(0, tt // lr, ln, 0)

    row = pl.BlockSpec((tt, d), lambda i: (i, 0))
    vec = pl.BlockSpec((1, d), lambda i: (0, 0))
    return pl.pallas_call(
        body, name=name, grid=(t // tt,),
        in_specs=[row, pl.BlockSpec((HALO, d), lambda i: (jnp.maximum(i * hb - 1, 0), 0)),
                  pl.BlockSpec((HALO, d), lambda i: (0, 0)), vec, vec, vec],
        out_specs=[row, row],
        out_shape=[jax.ShapeDtypeStruct((t, d), F32), jax.ShapeDtypeStruct((t, d), BF16)],
        scratch_shapes=[pltpu.VMEM((HALO + tt, d), F32)],
        compiler_params=_cp(dimension_semantics=("arbitrary",)),
    )(glu, glu, w_dw, b_dw, ln_g, ln_b)


def _conv_bwd_ln(name, ds, c, ln_g, ln_b):
    t, d = c.shape
    tt = _tile(t, CONV_ROWS)
    lr = min(CONV_LR, tt)

    def body(ds_ref, c_ref, lg_ref, lb_ref, dc_ref, dg_ref, db_ref, dbdw_ref):
        @pl.when(pl.program_id(0) == 0)
        def _():
            dg_ref[...] = jnp.zeros_like(dg_ref)
            db_ref[...] = jnp.zeros_like(db_ref)
            dbdw_ref[...] = jnp.zeros_like(dbdw_ref)

        def chunk(r, carry):
            rows = pl.ds(pl.multiple_of(r * lr, lr), lr)
            xh, rstd = _ln_stats(c_ref[rows, :])
            g = lg_ref[...]
            y = xh * g + lb_ref[...]
            sg = _sigmoid(y)
            dy = ds_ref[rows, :] * (sg * (1.0 + y * (1.0 - sg)))
            dxh = dy * g
            dc = rstd * (dxh - jnp.mean(dxh, axis=-1, keepdims=True)
                         - xh * jnp.mean(dxh * xh, axis=-1, keepdims=True))
            dc_ref[rows, :] = dc
            dg_ref[...] += jnp.sum(dy * xh, axis=0, keepdims=True)
            db_ref[...] += jnp.sum(dy, axis=0, keepdims=True)
            dbdw_ref[...] += jnp.sum(dc, axis=0, keepdims=True)
            return carry

        lax.fori_loop(0, tt // lr, chunk, 0)

    row = pl.BlockSpec((tt, d), lambda i: (i, 0))
    vec = pl.BlockSpec((1, d), lambda i: (0, 0))
    vsh = jax.ShapeDtypeStruct((1, d), F32)
    return pl.pallas_call(
        body, name=name, grid=(t // tt,),
        in_specs=[row, row, vec, vec], out_specs=[row, vec, vec, vec],
        out_shape=[jax.ShapeDtypeStruct((t, d), F32), vsh, vsh, vsh],
        compiler_params=_cp(dimension_semantics=("arbitrary",)),
    )(ds, c, ln_g, ln_b)


def _conv_bwd_dw(name, dc, glu, ua, ug, w_dw):
    t, d = dc.shape
    tt = _tile(t, CONV_ROWS)
    rt, cw = min(CONV_RT, tt), min(CONV_CW, d)
    hb = tt // HALO
    nt = t // tt

    def body(dcc_ref, dcn_ref, gc_ref, gp_ref, ua_ref, ug_ref, w_ref,
             du_ref, dw_ref, dbu_ref, dbuf, gbuf, wacc):
        i = pl.program_id(0)

        @pl.when(i == 0)
        def _():
            wacc[...] = jnp.zeros_like(wacc)
            dbu_ref[...] = jnp.zeros_like(dbu_ref)

        dbuf[0:tt, :] = dcc_ref[...]
        dbuf[tt:tt + HALO, :] = jnp.where(i < nt - 1, dcn_ref[...], 0.0)
        gbuf[0:HALO, :] = jnp.where(i > 0, gp_ref[...], 0.0)
        gbuf[HALO:HALO + tt, :] = gc_ref[...]

        def chunk(cb, carry):
            c0 = pl.multiple_of(cb * cw, cw)
            col = pl.ds(c0, cw)
            colg = pl.ds(pl.multiple_of(d + cb * cw, cw), cw)
            for r0 in range(0, tt, rt):
                dcr = dbuf[r0:r0 + rt, col]
                dgl = jnp.zeros((rt, cw), F32)
                for k in range(CONV_W):
                    hi = r0 + (CONV_W - 1) - k
                    dgl = dgl + w_ref[k:k + 1, col] * dbuf[hi:hi + rt, col]
                    lo = r0 + HALO - (CONV_W - 1) + k
                    prod = dcr * gbuf[lo:lo + rt, col]
                    part = prod[0:8, :]
                    for r in range(8, rt, 8):
                        part = part + prod[r:r + 8, :]
                    wacc[8 * k:8 * k + 8, col] += part
                a = ua_ref[r0:r0 + rt, col].astype(F32)
                sg = _sigmoid(ug_ref[r0:r0 + rt, col].astype(F32))
                da = dgl * sg
                dgate = dgl * a * sg * (1.0 - sg)
                du_ref[r0:r0 + rt, col] = da.astype(BF16)
                du_ref[r0:r0 + rt, colg] = dgate.astype(BF16)
                dbu_ref[:, col] += jnp.sum(da, axis=0, keepdims=True)
                dbu_ref[:, colg] += jnp.sum(dgate, axis=0, keepdims=True)
            return carry

        lax.fori_loop(0, d // cw, chunk, 0)

        @pl.when(i == nt - 1)
        def _():
            for k in range(CONV_W):
                dw_ref[k:k + 1, :] = jnp.sum(wacc[8 * k:8 * k + 8, :], axis=0, keepdims=True)
            dw_ref[CONV_W:HALO, :] = jnp.zeros((HALO - CONV_W, d), F32)

    row = pl.BlockSpec((tt, d), lambda i: (i, 0))
    return pl.pallas_call(
        body, name=name, grid=(nt,),
        in_specs=[row, pl.BlockSpec((HALO, d), lambda i: (jnp.minimum((i + 1) * hb, t // HALO - 1), 0)),
                  row, pl.BlockSpec((HALO, d), lambda i: (jnp.maximum(i * hb - 1, 0), 0)),
                  row, row, pl.BlockSpec((HALO, d), lambda i: (0, 0))],
        out_specs=[pl.BlockSpec((tt, 2 * d), lambda i: (i, 0)),
                   pl.BlockSpec((HALO, d), lambda i: (0, 0)),
                   pl.BlockSpec((1, 2 * d), lambda i: (0, 0))],
        out_shape=[jax.ShapeDtypeStruct((t, 2 * d), BF16), jax.ShapeDtypeStruct((HALO, d), F32),
                   jax.ShapeDtypeStruct((1, 2 * d), F32)],
        scratch_shapes=[pltpu.VMEM((tt + HALO, d), F32), pltpu.VMEM((HALO + tt, d), F32),
                        pltpu.VMEM((8 * HALO, d), F32)],
        compiler_params=_cp(dimension_semantics=("arbitrary",)),
    )(dc, dc, glu, glu, ua, ug, w_dw)


def _mla_mid_fwd(name, down, qg, kvg, ck, sk):
    t, w = down.shape
    rq, rkv = qg.shape[-1], kvg.shape[-1]
    tm = _tile(t, 512)

    def body(dn_ref, qg_ref, kvg_ref, ck_ref, sk_ref, qn_ref, kvn_ref, kpe_ref):
        cq = dn_ref[:, 0:rq]
        ckv = dn_ref[:, rq:rq + rkv]
        pe = dn_ref[:, rq + rkv:rq + rkv + LANE]
        qn_ref[...] = (cq * lax.rsqrt(jnp.mean(cq * cq, axis=-1, keepdims=True) + NORM_EPS)
                       * qg_ref[...]).astype(BF16)
        kvn_ref[...] = (ckv * lax.rsqrt(jnp.mean(ckv * ckv, axis=-1, keepdims=True) + NORM_EPS)
                        * kvg_ref[...]).astype(BF16)
        kpe_ref[...] = _rope(pe, ck_ref[...], sk_ref[...]).astype(BF16)

    def row(n):
        return pl.BlockSpec((tm, n), lambda i: (i, 0))

    def vec(n):
        return pl.BlockSpec((1, n), lambda i: (0, 0))

    return pl.pallas_call(
        body, name=name, grid=(t // tm,),
        in_specs=[row(w), vec(rq), vec(rkv), row(LANE), row(LANE)],
        out_specs=[row(rq), row(rkv), row(LANE)],
        out_shape=[jax.ShapeDtypeStruct((t, rq), BF16), jax.ShapeDtypeStruct((t, rkv), BF16),
                   jax.ShapeDtypeStruct((t, LANE), BF16)],
        compiler_params=_cp(),
    )(down, qg, kvg, ck, sk)


def _mla_mid_bwd(name, down, qg, kvg, dqn, dkvn, dkpe):
    t, w = down.shape
    rq, rkv = qg.shape[-1], kvg.shape[-1]
    tm = _tile(t, 256)

    def body(dn_ref, qg_ref, kvg_ref, dqn_ref, dkvn_ref, dkpe_ref, dd_ref, dqg_ref, dkvg_ref):
        @pl.when(pl.program_id(0) == 0)
        def _():
            dqg_ref[...] = jnp.zeros_like(dqg_ref)
            dkvg_ref[...] = jnp.zeros_like(dkvg_ref)

        dcq, dqg = _rms_bwd_math(dn_ref[:, 0:rq], qg_ref[...], dqn_ref[...])
        dckv, dkvg = _rms_bwd_math(dn_ref[:, rq:rq + rkv], kvg_ref[...], dkvn_ref[...])
        dd_ref[:, 0:rq] = dcq.astype(BF16)
        dd_ref[:, rq:rq + rkv] = dckv.astype(BF16)
        dd_ref[:, rq + rkv:rq + rkv + LANE] = dkpe_ref[...].astype(BF16)
        dqg_ref[...] += dqg
        dkvg_ref[...] += dkvg

    def row(n):
        return pl.BlockSpec((tm, n), lambda i: (i, 0))

    def vec(n):
        return pl.BlockSpec((1, n), lambda i: (0, 0))

    return pl.pallas_call(
        body, name=name, grid=(t // tm,),
        in_specs=[row(w), vec(rq), vec(rkv), row(rq), row(rkv), row(LANE)],
        out_specs=[row(w), vec(rq), vec(rkv)],
        out_shape=[jax.ShapeDtypeStruct((t, w), BF16), jax.ShapeDtypeStruct((1, rq), F32),
                   jax.ShapeDtypeStruct((1, rkv), F32)],
        compiler_params=_cp(dimension_semantics=("arbitrary",)),
    )(down, qg, kvg, dqn, dkvn, dkpe)


ATT_TILE = 512
ATT_HEADS = 2
_NT = (((1,), (1,)), ((), ()))
_TN = (((0,), (0,)), ((), ()))


def _flash_fwd(name, qf, kf, vt, heads):
    s = qf.shape[0]
    t = _tile(s, ATT_TILE)
    n = s // t
    g = min(ATT_HEADS, heads)
    qw, vw = HEAD_QK_PAD, HEAD_V

    def body(q_ref, k_ref, vt_ref, o_ref, lse_ref, m_sc, l_sc, acc_sc):
        i, j = pl.program_id(1), pl.program_id(2)

        @pl.when(j == 0)
        def _():
            m_sc[...] = jnp.full(m_sc.shape, -jnp.inf, F32)
            l_sc[...] = jnp.zeros_like(l_sc)
            acc_sc[...] = jnp.zeros_like(acc_sc)

        def step(diag):
            for hh in range(g):
                sc = lax.dot_general(k_ref[:, hh * qw:(hh + 1) * qw], q_ref[:, hh * qw:(hh + 1) * qw], _NT,
                                     preferred_element_type=F32)
                if diag:
                    sc = jnp.where(_chunk_mask_t(t), sc, -jnp.inf)
                m_old = m_sc[hh]
                m_new = jnp.maximum(m_old, jnp.max(sc, axis=0, keepdims=True))
                alpha = jnp.exp(m_old - m_new)
                p = jnp.exp(sc - m_new)
                l_sc[hh] = alpha * l_sc[hh] + jnp.sum(p, axis=0, keepdims=True)
                acc_sc[hh] = alpha * acc_sc[hh] + jnp.dot(vt_ref[hh * vw:(hh + 1) * vw, :], p.astype(BF16),
                                                          preferred_element_type=F32)
                m_sc[hh] = m_new

        @pl.when(j < i)
        def _():
            step(False)

        @pl.when(j == i)
        def _():
            step(True)
            for hh in range(g):
                l = l_sc[hh]
                o_ref[:, hh * vw:(hh + 1) * vw] = (acc_sc[hh] / l).T.astype(BF16)
                lse_ref[hh] = m_sc[hh] + jnp.log(l)

    return pl.pallas_call(
        body, name=name, grid=(heads // g, n, n),
        in_specs=[pl.BlockSpec((t, g * qw), lambda h, i, j: (i, h)),
                  pl.BlockSpec((t, g * qw), lambda h, i, j: (jnp.minimum(j, i), h)),
                  pl.BlockSpec((g * vw, t), lambda h, i, j: (h, jnp.minimum(j, i)))],
        out_specs=[pl.BlockSpec((t, g * vw), lambda h, i, j: (i, h)),
                   pl.BlockSpec((g, 1, t), lambda h, i, j: (h, 0, i))],
        out_shape=[jax.ShapeDtypeStruct((s, heads * vw), BF16),
                   jax.ShapeDtypeStruct((heads, 1, s), F32)],
        scratch_shapes=[pltpu.VMEM((g, 1, t), F32), pltpu.VMEM((g, 1, t), F32), pltpu.VMEM((g, vw, t), F32)],
        compiler_params=_cp(dimension_semantics=("arbitrary", "arbitrary", "arbitrary")),
    )(qf, kf, vt)


def _flash_bwd_dq(name, qf, kf, kft, v, do, o, lse, cq, sq, heads):
    s = qf.shape[0]
    t = _tile(s, ATT_TILE)
    n = s // t
    g = min(ATT_HEADS, heads)
    qw, vw = HEAD_QK_PAD, HEAD_V

    def body(q_ref, k_ref, kt_ref, v_ref, do_ref, o_ref, lse_ref, c_ref, s_ref, dq_ref, dl_ref, acc_sc):
        i, j = pl.program_id(1), pl.program_id(2)

        @pl.when(j == 0)
        def _():
            acc_sc[...] = jnp.zeros_like(acc_sc)
            for hh in range(g):
                cols = slice(hh * vw, (hh + 1) * vw)
                col = jnp.sum(do_ref[:, cols].astype(F32) * o_ref[:, cols].astype(F32), axis=1, keepdims=True)
                dl_ref[hh] = jnp.broadcast_to(col, (t, LANE)).T[0:1, :]

        def step(diag):
            for hh in range(g):
                sc = lax.dot_general(k_ref[:, hh * qw:(hh + 1) * qw], q_ref[:, hh * qw:(hh + 1) * qw], _NT,
                                     preferred_element_type=F32)
                p = jnp.exp(sc - lse_ref[hh])
                if diag:
                    p = jnp.where(_chunk_mask_t(t), p, 0.0)
                dp = lax.dot_general(v_ref[:, hh * vw:(hh + 1) * vw], do_ref[:, hh * vw:(hh + 1) * vw], _NT,
                                     preferred_element_type=F32)
                ds = (p * (dp - dl_ref[hh])).astype(BF16)
                acc_sc[hh] += jnp.dot(kt_ref[hh * qw:(hh + 1) * qw, :], ds, preferred_element_type=F32)

        @pl.when(j < i)
        def _():
            step(False)

        @pl.when(j == i)
        def _():
            step(True)
            for hh in range(g):
                dq_ref[:, hh * qw:(hh + 1) * qw] = _rope_t(acc_sc[hh].T, c_ref[...], s_ref[...]).astype(BF16)

    qspec = pl.BlockSpec((t, g * qw), lambda h, i, j: (i, h))
    ospec = pl.BlockSpec((t, g * vw), lambda h, i, j: (i, h))
    vspec = pl.BlockSpec((g, 1, t), lambda h, i, j: (h, 0, i))
    tab = pl.BlockSpec((t, qw), lambda h, i, j: (i, 0))
    return pl.pallas_call(
        body, name=name, grid=(heads // g, n, n),
        in_specs=[qspec,
                  pl.BlockSpec((t, g * qw), lambda h, i, j: (jnp.minimum(j, i), h)),
                  pl.BlockSpec((g * qw, t), lambda h, i, j: (h, jnp.minimum(j, i))),
                  pl.BlockSpec((t, g * vw), lambda h, i, j: (jnp.minimum(j, i), h)),
                  ospec, ospec, vspec, tab, tab],
        out_specs=[qspec, vspec],
        out_shape=[jax.ShapeDtypeStruct(qf.shape, BF16), jax.ShapeDtypeStruct((heads, 1, s), F32)],
        scratch_shapes=[pltpu.VMEM((g, qw, t), F32)],
        compiler_params=_cp(dimension_semantics=("arbitrary", "arbitrary", "arbitrary")),
    )(qf, kf, kft, v, do, o, lse, cq, sq)


def _flash_bwd_dkv(name, qf, kf, v, do, lse, delta, ck, sk, heads):
    s = qf.shape[0]
    t = _tile(s, ATT_TILE)
    n = s // t
    g = min(ATT_HEADS, heads)
    qw, vw = HEAD_QK_PAD, HEAD_V
    lse_rows, delta_rows = lse, delta

    def body(q_ref, k_ref, v_ref, do_ref, lse_ref, dl_ref, c_ref, s_ref, dkv_ref, dpe_ref, dk_sc, dv_sc):
        j, h, i = pl.program_id(0), pl.program_id(1), pl.program_id(2)

        @pl.when(i == 0)
        def _():
            dk_sc[...] = jnp.zeros_like(dk_sc)
            dv_sc[...] = jnp.zeros_like(dv_sc)

        def step(diag):
            for hh in range(g):
                q = q_ref[:, hh * qw:(hh + 1) * qw]
                do = do_ref[:, hh * vw:(hh + 1) * vw]
                sc = lax.dot_general(k_ref[:, hh * qw:(hh + 1) * qw], q, _NT, preferred_element_type=F32)
                p = jnp.exp(sc - lse_ref[hh])
                if diag:
                    p = jnp.where(_chunk_mask_t(t), p, 0.0)
                dv_sc[hh] += jnp.dot(p.astype(BF16), do, preferred_element_type=F32)
                dp = lax.dot_general(v_ref[:, hh * vw:(hh + 1) * vw], do, _NT, preferred_element_type=F32)
                ds = (p * (dp - dl_ref[hh])).astype(BF16)
                dk_sc[hh] += jnp.dot(ds, q, preferred_element_type=F32)

        @pl.when(i > j)
        def _():
            step(False)

        @pl.when(i == j)
        def _():
            step(True)

        @pl.when(i == n - 1)
        def _():
            pe = None
            for hh in range(g):
                dk = dk_sc[hh]
                dkv_ref[:, hh * qw:(hh + 1) * qw] = jnp.concatenate([dk[:, 0:HEAD_NOPE], dv_sc[hh]],
                                                                     axis=1).astype(BF16)
                part = dk[:, HEAD_NOPE:HEAD_QK_PAD]
                pe = part if pe is None else pe + part

            @pl.when(h == 0)
            def _():
                dpe_ref[...] = pe

            @pl.when(h > 0)
            def _():
                dpe_ref[...] += pe

            @pl.when(h == heads // g - 1)
            def _():
                dpe_ref[...] = _rope_t(dpe_ref[...], c_ref[...], s_ref[...])

    qrow = lambda j, h, i: (jnp.maximum(i, j), h)
    vrow = lambda j, h, i: (h, 0, jnp.maximum(i, j))
    return pl.pallas_call(
        body, name=name, grid=(n, heads // g, n),
        in_specs=[pl.BlockSpec((t, g * qw), qrow),
                  pl.BlockSpec((t, g * qw), lambda j, h, i: (j, h)),
                  pl.BlockSpec((t, g * vw), lambda j, h, i: (j, h)),
                  pl.BlockSpec((t, g * vw), qrow),
                  pl.BlockSpec((g, 1, t), vrow),
                  pl.BlockSpec((g, 1, t), vrow),
                  pl.BlockSpec((t, LANE), lambda j, h, i: (j, 0)),
                  pl.BlockSpec((t, LANE), lambda j, h, i: (j, 0))],
        out_specs=[pl.BlockSpec((t, g * (HEAD_NOPE + HEAD_V)), lambda j, h, i: (j, h)),
                   pl.BlockSpec((t, LANE), lambda j, h, i: (j, 0))],
        out_shape=[jax.ShapeDtypeStruct((s, heads * (HEAD_NOPE + HEAD_V)), BF16),
                   jax.ShapeDtypeStruct((s, LANE), F32)],
        scratch_shapes=[pltpu.VMEM((g, t, qw), F32), pltpu.VMEM((g, t, vw), F32)],
        compiler_params=_cp(dimension_semantics=("arbitrary", "arbitrary", "arbitrary")),
    )(qf, kf, v, do, lse_rows, delta_rows, ck, sk)


def _adamw(name, parts, w, m, v):
    p, r, c = parts.shape
    tr = _tile(r, max(8, (256 * 1024) // max(c, 1)))
    bc1 = 1.0 - ADAM_B1 ** ADAM_STEP
    bc2 = 1.0 - ADAM_B2 ** ADAM_STEP

    def body(p_ref, w_ref, m_ref, v_ref, g_ref, d_ref, nm_ref, nv_ref):
        g = p_ref[0].astype(F32)
        for q in range(1, p):
            g = g + p_ref[q].astype(F32)
        nm = ADAM_B1 * m_ref[...] + (1.0 - ADAM_B1) * g
        nv = ADAM_B2 * v_ref[...] + (1.0 - ADAM_B2) * (g * g)
        g_ref[...] = g
        nm_ref[...] = nm
        nv_ref[...] = nv
        d_ref[...] = -ADAM_LR * ((nm / bc1) / (jnp.sqrt(nv / bc2) + ADAM_EPS) + ADAM_WD * w_ref[...])

    blk = pl.BlockSpec((tr, c), lambda i: (i, 0))
    sh = jax.ShapeDtypeStruct((r, c), F32)
    return pl.pallas_call(
        body, name=name, grid=(r // tr,),
        in_specs=[pl.BlockSpec((p, tr, c), lambda i: (0, i, 0)), blk, blk, blk],
        out_specs=[blk] * 4, out_shape=[sh] * 4,
        compiler_params=_cp(),
    )(parts, w, m, v)


def _my_place():
    x, y, c = lax.axis_index("x"), lax.axis_index("y"), lax.axis_index("c")
    return x, y, c


def _flip(v, bit):
    return 1 - v if bit else v


def _block(ref, axis, idx, size):
    return ref.at[(slice(None),) * axis + (pl.ds(idx * size, size),)]


HBM_SPEC = pl.BlockSpec(memory_space=pltpu.HBM)
SEM_SPEC = pl.BlockSpec(memory_space=pltpu.SEMAPHORE)
DATAFLOW = pltpu.SideEffectType.DATAFLOW_SIDE_EFFECTING


def _hbm(a):
    return pltpu.with_memory_space_constraint(a, pltpu.HBM)


def _remote_copies(jobs, bufs, send_sems, recv_sems):
    return [pltpu.make_async_remote_copy(src_ref=src, dst_ref=dst, send_sem=send_sems.at[q],
                                         recv_sem=recv_sems.at[q], device_id=dev, device_id_type=MESH)
            for q, (src, dst, dev) in enumerate(jobs(bufs))]


def _split_start(name, bufs, jobs, n_jobs, after):
    nb = len(bufs)

    def body(*refs):
        send_sems, recv_sems = refs[nb + 1], refs[nb + 2]
        for cp in _remote_copies(jobs, refs[:nb], send_sems, recv_sems):
            cp.start()
        refs[-1][...] = jnp.zeros_like(refs[-1])

    outs = pl.pallas_call(
        body, name=name,
        out_shape=(pltpu.SemaphoreType.DMA((n_jobs,)), pltpu.SemaphoreType.DMA((n_jobs,)),
                   *[pltpu.HBM(b.shape, b.dtype) for b in bufs], jax.ShapeDtypeStruct((8, LANE), F32)),
        in_specs=[HBM_SPEC] * nb + [ANY],
        out_specs=(SEM_SPEC, SEM_SPEC, *[HBM_SPEC] * nb, VMEM_SPEC),
        input_output_aliases={q: 2 + q for q in range(nb)},
        compiler_params=pltpu.CompilerParams(has_side_effects=DATAFLOW),
    )(*[_hbm(b) for b in bufs], after)
    return outs[0], outs[1], list(outs[2:2 + nb]), outs[-1]


def _split_wait(name, bufs, send_sems, recv_sems, jobs, after):
    nb = len(bufs)

    def body(*refs):
        for cp in _remote_copies(jobs, refs[:nb], refs[nb], refs[nb + 1]):
            cp.wait_send()
            cp.wait_recv()

    outs = pl.pallas_call(
        body, name=name,
        out_shape=tuple(pltpu.HBM(b.shape, b.dtype) for b in bufs),
        in_specs=[HBM_SPEC] * nb + [SEM_SPEC, SEM_SPEC, ANY],
        out_specs=tuple([HBM_SPEC] * nb),
        input_output_aliases={q: q for q in range(nb)},
        compiler_params=pltpu.CompilerParams(has_side_effects=DATAFLOW),
    )(*bufs, send_sems, recv_sems, after)
    return list(outs)


PLACE_TILE_BYTES = 2 * 1024 * 1024


def _own_block_spec(tr, c, nblk, axis):
    if axis == 0:
        return pl.BlockSpec((tr, c), lambda i, me: (me[0] * nblk + i, 0))
    return pl.BlockSpec((tr, c), lambda i, me: (i, me[0]))


def _cast_place(name, w, layer, axis, me):
    _, r, c = w.shape
    tr = _tile(r, max(SUBLANE_BF16, PLACE_TILE_BYTES // (4 * c)))
    nblk = r // tr
    full = (N_DEV * r, c) if axis == 0 else (r, N_DEV * c)

    def body(me_ref, w_ref, o_ref):
        o_ref[...] = w_ref[...].astype(BF16)

    return pl.pallas_call(
        body, name=name,
        grid_spec=pltpu.PrefetchScalarGridSpec(
            num_scalar_prefetch=1, grid=(nblk,),
            in_specs=[pl.BlockSpec((None, tr, c), lambda i, me: (layer, i, 0))],
            out_specs=_own_block_spec(tr, c, nblk, axis)),
        out_shape=jax.ShapeDtypeStruct(full, BF16), compiler_params=_cp(),
    )(me, w)


def _own_place(name, grad, land, layer, axis, me):
    _, _, r, c = land.shape
    tr = _tile(r, max(SUBLANE_BF16, PLACE_TILE_BYTES // (2 * c)))
    nblk = r // tr

    def body(me_ref, g_ref, land_ref, o_ref):
        o_ref[...] = g_ref[...]

    return pl.pallas_call(
        body, name=name,
        grid_spec=pltpu.PrefetchScalarGridSpec(
            num_scalar_prefetch=1, grid=(nblk,),
            in_specs=[_own_block_spec(tr, c, nblk, axis), ANY],
            out_specs=pl.BlockSpec((None, None, tr, c), lambda i, me: (0, layer, i, 0))),
        out_shape=jax.ShapeDtypeStruct(land.shape, land.dtype),
        input_output_aliases={2: 0}, compiler_params=_cp(),
    )(me, grad, land)


def _gather_jobs_a(axes, sizes):
    def jobs(bufs):
        x, y, c = _my_place()
        out = []
        for t, buf in enumerate(bufs):
            blk = _block(buf, axes[t], 4 * x + 2 * y + c, sizes[t])
            for dev in [(x, y, 1 - c), (1 - x, y, c), (x, 1 - y, c), (1 - x, 1 - y, c)]:
                out.append((blk, blk, dev))
        return out
    return jobs


def _gather_jobs_b(axes, sizes):
    nt = len(axes)

    def jobs(bufs):
        x, y, c = _my_place()
        out = []
        for t in range(nt):
            for px, py in [(1 - x, y), (x, 1 - y), (1 - x, 1 - y)]:
                blk = _block(bufs[t], axes[t], 4 * px + 2 * py + c, sizes[t])
                out.append((blk, blk, (x, y, 1 - c)))
        return out
    return jobs


def _exchange_jobs(axes, sizes, layers):
    nt = len(axes)

    def jobs(bufs):
        x, y, c = _my_place()
        out = []
        for k in range(1, N_DEV):
            px, py, pc = _flip(x, k & 4), _flip(y, k & 2), _flip(c, k & 1)
            for t in range(nt):
                out.append((_block(bufs[t], axes[t], 4 * px + 2 * py + pc, sizes[t]),
                            bufs[nt + t].at[k, layers[t]], (px, py, pc)))
        return out
    return jobs


def _gather_begin(name, lands, axes, after):
    sizes = [b.shape[ax] // N_DEV for b, ax in zip(lands, axes)]
    jobs = _gather_jobs_a(axes, sizes)
    send, recv, bufs, token = _split_start(name + "_a", lands, jobs, 4 * len(lands), after)
    return dict(name=name, axes=axes, sizes=sizes, send=send, recv=recv, bufs=bufs, jobs=jobs), token


def _gather_mid(h, after):
    bufs = _split_wait(h["name"] + "_aw", h["bufs"], h["send"], h["recv"], h["jobs"], after)
    jobs = _gather_jobs_b(h["axes"], h["sizes"])
    send, recv, lands, token = _split_start(h["name"] + "_b", bufs, jobs, 3 * len(bufs), after)
    return dict(h, send=send, recv=recv, bufs=lands, jobs=jobs), token


def _gather_end(h, after):
    return _split_wait(h["name"] + "_bw", h["bufs"], h["send"], h["recv"], h["jobs"], after)


def _exchange_begin(name, grads, axes, lands, layers, me, after):
    sizes = [g.shape[ax] // N_DEV for g, ax in zip(grads, axes)]
    lands = [_own_place(f"{name}_place{t}", grads[t], lands[t], layers[t], axes[t], me)
             for t in range(len(grads))]
    jobs = _exchange_jobs(axes, sizes, layers)
    send, recv, bufs, token = _split_start(name + "_s", list(grads) + lands, jobs, 7 * len(grads), after)
    return dict(name=name, n=len(grads), send=send, recv=recv, bufs=bufs, jobs=jobs), token


def _exchange_end(h, after):
    bufs = _split_wait(h["name"] + "_w", h["bufs"], h["send"], h["recv"], h["jobs"], after)
    return bufs[h["n"]:]


def _all_gather_small(name, vec, reduce):
    r = vec.shape[0]

    def body(v_ref, o_ref, *rest):
        if reduce:
            buf, send_sems, recv_sems = rest
        else:
            buf = o_ref
            send_sems, recv_sems = rest
        x, y, c = _my_place()
        mine = 4 * x + 2 * y + c
        buf[mine] = v_ref[...]
        copies = []
        for k in range(1, N_DEV):
            px, py, pc = _flip(x, k & 4), _flip(y, k & 2), _flip(c, k & 1)
            cp = pltpu.make_async_remote_copy(
                src_ref=v_ref, dst_ref=buf.at[mine], send_sem=send_sems.at[k - 1],
                recv_sem=recv_sems.at[k - 1], device_id=(px, py, pc), device_id_type=MESH)
            cp.start()
            copies.append(cp)
        for cp in copies:
            cp.wait()
        if reduce:
            acc = buf[0]
            for q in range(1, N_DEV):
                acc = acc + buf[q]
            o_ref[...] = acc

    scratch = [pltpu.SemaphoreType.DMA((N_DEV - 1,)), pltpu.SemaphoreType.DMA((N_DEV - 1,))]
    if reduce:
        scratch = [pltpu.VMEM((N_DEV, r, LANE), F32)] + scratch
        out_shape = jax.ShapeDtypeStruct((r, LANE), F32)
    else:
        out_shape = jax.ShapeDtypeStruct((N_DEV, r, LANE), F32)
    return pl.pallas_call(
        body, name=name, in_specs=[VMEM_SPEC], out_specs=VMEM_SPEC, out_shape=out_shape,
        scratch_shapes=scratch, compiler_params=_cp(has_side_effects=True),
    )(vec)


def _pack(arrs, row_mult=8):
    flat = jnp.concatenate([a.reshape(-1).astype(F32) for a in arrs])
    n = flat.shape[0]
    rows = -(-n // LANE)
    rows = -(-rows // row_mult) * row_mult
    return jnp.pad(flat, (0, rows * LANE - n)).reshape(rows, LANE)


def _unpack(vec, shapes):
    flat = vec.reshape(-1)
    out, pos = [], 0
    for sh in shapes:
        n = 1
        for s in sh:
            n *= s
        out.append(flat[pos:pos + n].reshape(sh))
        pos += n
    return out


BIG = ["conv_w_pw1", "conv_w_pw2", "mla_w_in", "mla_w_q_up", "mla_w_kv_up", "mla_w_o", "mlp_w1", "mlp_w2"]
BIG_AXIS = {"conv_w_pw1": 2, "conv_w_pw2": 1, "mla_w_in": 1, "mla_w_q_up": 2, "mla_w_kv_up": 2,
            "mla_w_o": 1, "mlp_w1": 2, "mlp_w2": 1}
SMALL_SHARDED = ["conv_w_dw", "mla_q_norm_g", "mla_kv_norm_g"]
REPLICATED = ["norm_mixer_g", "norm_mlp_g", "conv_b_pw1", "conv_b_dw", "conv_ln_g", "conv_ln_b",
              "conv_b_pw2", "final_norm_g"]
WEIGHTS = ["norm_mixer_g", "norm_mlp_g", "conv_w_pw1", "conv_b_pw1", "conv_w_dw", "conv_b_dw",
           "conv_ln_g", "conv_ln_b", "conv_w_pw2", "conv_b_pw2", "mla_w_in", "mla_q_norm_g",
           "mla_kv_norm_g", "mla_w_q_up", "mla_w_kv_up", "mla_w_o", "mlp_w1", "mlp_w2", "final_norm_g"]


def _unshard_last(g, lead):
    nd = g.ndim
    perm = tuple(range(1, nd - 1)) + (0, nd - 1)
    return g.transpose(perm).reshape(lead + (N_DEV * g.shape[-1],))


def _step(w, m, v, x, positions, target):
    s, d = x.shape
    depth = w["norm_mixer_g"].shape[0]
    n_conv, n_mla = w["conv_w_pw1"].shape[0], w["mla_w_in"].shape[0]
    heads = (w["mla_w_q_up"].shape[-1] * N_DEV) // (HEAD_NOPE + HEAD_ROPE)
    rq, rkv = w["mla_w_q_up"].shape[1], w["mla_w_kv_up"].shape[1]
    xi, yi, ci = _my_place()
    mine = 4 * xi + 2 * yi + ci

    def mixer_units(layer):
        names = (["conv_w_pw1", "conv_w_pw2"] if layer % 2 == 0
                 else ["mla_w_in", "mla_w_q_up", "mla_w_kv_up", "mla_w_o"])
        return [(n, layer // 2) for n in names]

    def mlp_units(layer):
        return [("mlp_w1", layer), ("mlp_w2", layer)]

    me_arr = mine.astype(jnp.int32).reshape(1)

    def gather_begin(tag, units, after):
        lands = [_cast_place(f"{tag}_place_{n}", w[n], jl, BIG_AXIS[n] - 1, me_arr) for n, jl in units]
        h, token = _gather_begin(tag, lands, [BIG_AXIS[n] - 1 for n, _ in units], after)
        return dict(h, units=units), token

    full = {}

    def gather_end(h, after):
        full.update(zip(h["units"], _gather_end(h, after)))

    small_shapes = [w[n].shape for n in SMALL_SHARDED]
    gathered = _all_gather_small("gather_small", _pack([w[n] for n in SMALL_SHARDED]), False)

    first_a, tok = gather_begin("gather_0a", mixer_units(0), gathered)
    first_b, tok = gather_begin("gather_0b", mlp_units(0), tok)
    pending = {}
    if depth > 1:
        pending[1], tok = gather_begin("gather_1", mixer_units(1) + mlp_units(1), tok)
    first_a, tok = _gather_mid(first_a, tok)
    gather_end(first_a, tok)

    per_dev = [_unpack(gathered[q], small_shapes) for q in range(N_DEV)]
    w_dw = _unshard_last(jnp.stack([p[0] for p in per_dev]), (n_conv, CONV_W))
    q_gain = _unshard_last(jnp.stack([p[1] for p in per_dev]), (n_mla,))
    kv_gain = _unshard_last(jnp.stack([p[2] for p in per_dev]), (n_mla,))
    w_dw_pad = jnp.pad(w_dw, ((0, 0), (0, HALO - CONV_W), (0, 0)))

    w_in_cols = rq + rkv + HEAD_ROPE

    def pad_w_in(a):
        return jnp.pad(a, ((0, 0), (0, rq + rkv + LANE - w_in_cols)))

    def pad_wq(a):
        return jnp.pad(a.reshape(rq, heads, HEAD_NOPE + HEAD_ROPE),
                       ((0, 0), (0, 0), (0, HEAD_QK_PAD - HEAD_NOPE - HEAD_ROPE))).reshape(rq, heads * HEAD_QK_PAD)

    inv_freq = ROPE_THETA ** (-jnp.arange(0, HEAD_ROPE, 2, dtype=F32) / HEAD_ROPE)
    ang = positions.reshape(s).astype(F32)[:, None] * inv_freq
    cos, sin = jnp.cos(ang), jnp.sin(ang)
    c64 = jnp.concatenate([cos, cos], axis=1)
    s64 = jnp.concatenate([-sin, sin], axis=1)
    zeros64 = jnp.zeros((s, LANE - HEAD_ROPE), F32)
    ck = jnp.concatenate([c64, zeros64], axis=1)
    sk = jnp.concatenate([s64, zeros64], axis=1)
    scale = (HEAD_NOPE + HEAD_ROPE) ** -0.5
    cq = scale * jnp.concatenate([jnp.ones((s, HEAD_NOPE), F32), ck], axis=1)
    sq = scale * jnp.concatenate([jnp.zeros((s, HEAD_NOPE), F32), sk], axis=1)

    def vec(a):
        return a.reshape(1, -1)

    saved = []
    wpad = {}
    for layer in range(depth):
        jl = layer // 2
        h = _rms_fwd(f"rms_mixer_{layer}", x, vec(w["norm_mixer_g"][layer]) + tok[0, 0])
        if layer % 2 == 0:
            ua, ug, glu = _mm_glu(f"conv_pw1_{layer}", h, full["conv_w_pw1", jl], None, vec(w["conv_b_pw1"][jl]))
            cc, sw = _conv_fwd(f"conv_dw_{layer}", glu, w_dw_pad[jl], vec(w["conv_b_dw"][jl]),
                               vec(w["conv_ln_g"][jl]), vec(w["conv_ln_b"][jl]))
            x1 = _mm_res(f"conv_pw2_{layer}", sw, full["conv_w_pw2", jl], None, x, vec(w["conv_b_pw2"][jl]))
            mix = (h, ua, ug, glu, cc, sw)
        else:
            wpad["in", jl] = pad_w_in(full["mla_w_in", jl])
            wpad["q", jl] = pad_wq(full["mla_w_q_up", jl])
            down = _mm_plain(f"mla_down_{layer}", h, wpad["in", jl], None, "nn", F32)
            qn, kvn, kpe = _mla_mid_fwd(f"mla_mid_{layer}", down, vec(q_gain[jl]), vec(kv_gain[jl]), ck, sk)
            qf = _mm_q(f"mla_q_{layer}", qn, wpad["q", jl], None, cq, sq)
            kf, vv, kft, vt = _mm_kv(f"mla_kv_{layer}", kvn, full["mla_w_kv_up", jl], None, kpe)
            o, lse = _flash_fwd(f"mla_attn_{layer}", qf, kf, vt, heads)
            x1 = _mm_res(f"mla_out_{layer}", o, full["mla_w_o", jl], None, x)
            mix = (h, down, qn, kvn, qf, kf, kft, vv, o, lse)
        anchor = x1
        if layer == 0:
            first_b, anchor = _gather_mid(first_b, anchor)
        if layer + 2 < depth:
            pending[layer + 2], anchor = gather_begin(f"gather_{layer + 2}",
                                                      mixer_units(layer + 2) + mlp_units(layer + 2), anchor)
        if layer == 0:
            gather_end(first_b, anchor)
        elif layer + 1 < depth:
            pending[layer + 1], anchor = _gather_mid(pending[layer + 1], anchor)
        if anchor is not x1:
            tok = anchor
        h2 = _rms_fwd(f"rms_mlp_{layer}", x1, vec(w["norm_mlp_g"][layer]) + tok[0, 0])
        z, a = _mm_mlp_up(f"mlp_up_{layer}", h2, full["mlp_w1", layer], None)
        x2 = _mm_res(f"mlp_down_{layer}", a, full["mlp_w2", layer], None, x1)
        if layer + 1 < depth:
            if layer == 0:
                pending[1], tok = _gather_mid(pending[1], x2)
                gather_end(pending[1], tok)
            else:
                gather_end(pending[layer + 1], x2)
        saved.append((x, mix, x1, h2, z, a))
        x = x2

    loss_row, g, gb, d_final, _ = _final_loss("final_loss", x, vec(w["final_norm_g"]) + tok[0, 0], target)

    recv = {n: lax.empty((N_DEV,) + w[n].shape, BF16) for n in BIG}

    def exchange_begin(tag, items, after):
        names = [n for n, _, _ in items]
        h, token = _exchange_begin(tag, [gr for _, _, gr in items], [BIG_AXIS[n] - 1 for n in names],
                                   [recv[n] for n in names], [jl for _, jl, _ in items], me_arr, after)
        return dict(h, names=names), token

    def exchange_end(h, after):
        recv.update(zip(h["names"], _exchange_end(h, after)))

    mix_exchange = None
    d_mixer, d_mlp = [None] * depth, [None] * depth
    d_small = {n: [None] * n_conv for n in ["conv_b_pw1", "conv_w_dw", "conv_b_dw", "conv_ln_g",
                                           "conv_ln_b", "conv_b_pw2"]}
    d_qg, d_kvg = [None] * n_mla, [None] * n_mla
    for layer in reversed(range(depth)):
        jl = layer // 2
        x0, mix, x1, h2, z, a = saved[layer]
        colsum_g = None
        dz = _mm_mlp_dz(f"mlp_dz_{layer}", gb, full["mlp_w2", layer], None, z)
        dw2 = _mm_wgrad(f"mlp_dw2_{layer}", a, gb)
        dh2 = _mm_plain(f"mlp_dh_{layer}", dz, full["mlp_w1", layer], None, "nt", F32)
        dw1 = _mm_wgrad(f"mlp_dw1_{layer}", h2, dz)
        mlp_exchange, tok = exchange_begin(f"exchange_mlp_{layer}",
                                           [("mlp_w1", layer, dw1), ("mlp_w2", layer, dw2)], dh2)
        g, gb, d_mlp[layer], colsum_g = _rms_bwd(f"rms_mlp_bwd_{layer}", x1,
                                                 vec(w["norm_mlp_g"][layer]) + tok[0, 0], dh2, g)
        if mix_exchange is not None:
            exchange_end(mix_exchange, g)
        if layer % 2 == 0:
            h, ua, ug, glu, cc, sw = mix
            d_small["conv_b_pw2"][jl] = colsum_g.reshape(-1)
            dsw = _mm_plain(f"conv_ds_{layer}", gb, full["conv_w_pw2", jl], None, "nt", F32)
            dwp2 = _mm_wgrad(f"conv_dw2_{layer}", sw, gb)
            dc, dlg, dlb, dbdw = _conv_bwd_ln(f"conv_ln_bwd_{layer}", dsw, cc, vec(w["conv_ln_g"][jl]),
                                              vec(w["conv_ln_b"][jl]))
            du, dwdw, dbu = _conv_bwd_dw(f"conv_dw_bwd_{layer}", dc, glu, ua, ug, w_dw_pad[jl])
            d_small["conv_ln_g"][jl] = dlg.reshape(-1)
            d_small["conv_ln_b"][jl] = dlb.reshape(-1)
            d_small["conv_b_dw"][jl] = dbdw.reshape(-1)
            d_small["conv_w_dw"][jl] = dwdw[:CONV_W]
            d_small["conv_b_pw1"][jl] = dbu.reshape(-1)
            dh = _mm_plain(f"conv_dh_{layer}", du, full["conv_w_pw1", jl], None, "nt", F32)
            dwp1 = _mm_wgrad(f"conv_dw1_{layer}", h, du)
            items = [("conv_w_pw1", jl, dwp1), ("conv_w_pw2", jl, dwp2)]
        else:
            h, down, qn, kvn, qf, kf, kft, vv, o, lse = mix
            do = _mm_plain(f"mla_do_{layer}", gb, full["mla_w_o", jl], None, "nt", BF16)
            dwo = _mm_wgrad(f"mla_dwo_{layer}", o, gb)
            dq, delta = _flash_bwd_dq(f"mla_attn_dq_{layer}", qf, kf, kft, vv, do, o, lse, cq, sq, heads)
            dkv, dkpe = _flash_bwd_dkv(f"mla_attn_dkv_{layer}", qf, kf, vv, do, lse, delta, ck, sk, heads)
            dqn = _mm_plain(f"mla_dqn_{layer}", dq, wpad["q", jl], None, "nt", F32)
            dwq = _mm_wgrad(f"mla_dwq_{layer}", qn, dq).reshape(rq, heads, HEAD_QK_PAD)[
                :, :, :HEAD_NOPE + HEAD_ROPE].reshape(rq, heads * (HEAD_NOPE + HEAD_ROPE))
            dkvn = _mm_plain(f"mla_dkvn_{layer}", dkv, full["mla_w_kv_up", jl], None, "nt", F32)
            dwkv = _mm_wgrad(f"mla_dwkv_{layer}", kvn, dkv)
            ddown, d_qg[jl], d_kvg[jl] = _mla_mid_bwd(f"mla_mid_bwd_{layer}", down, vec(q_gain[jl]),
                                                      vec(kv_gain[jl]), dqn, dkvn, dkpe)
            dh = _mm_plain(f"mla_dh_{layer}", ddown, wpad["in", jl], None, "nt", F32)
            dwin = _mm_wgrad(f"mla_dwin_{layer}", h, ddown)[:, :w_in_cols]
            items = [("mla_w_in", jl, dwin), ("mla_w_q_up", jl, dwq), ("mla_w_kv_up", jl, dwkv),
                     ("mla_w_o", jl, dwo)]
        mix_exchange, tok = exchange_begin(f"exchange_mix_{layer}", items, dh)
        g, gb, d_mixer[layer], _ = _rms_bwd(f"rms_mixer_bwd_{layer}", x0,
                                            vec(w["norm_mixer_g"][layer]) + tok[0, 0], dh, g)
        exchange_end(mlp_exchange, g)
    exchange_end(mix_exchange, g)
    grad_x = g

    out = {}
    for n in BIG:
        sh = w[n].shape
        r, c = sh[0] * sh[1], sh[2]
        res = _adamw(f"adamw_{n}", recv[n].reshape(N_DEV, r, c), w[n].reshape(r, c),
                     m[n].reshape(r, c), v[n].reshape(r, c))
        out[n] = [t.reshape(sh) for t in res]

    small_full = {
        "norm_mixer_g": jnp.concatenate(d_mixer, axis=0), "norm_mlp_g": jnp.concatenate(d_mlp, axis=0),
        "conv_b_pw1": jnp.stack(d_small["conv_b_pw1"]), "conv_b_dw": jnp.stack(d_small["conv_b_dw"]),
        "conv_ln_g": jnp.stack(d_small["conv_ln_g"]), "conv_ln_b": jnp.stack(d_small["conv_ln_b"]),
        "conv_b_pw2": jnp.stack(d_small["conv_b_pw2"]), "final_norm_g": d_final.reshape(-1),
        "conv_w_dw": jnp.stack(d_small["conv_w_dw"]),
        "mla_q_norm_g": jnp.concatenate(d_qg, axis=0), "mla_kv_norm_g": jnp.concatenate(d_kvg, axis=0),
    }
    names = REPLICATED + SMALL_SHARDED
    summed = _unpack(_all_gather_small("reduce_small", _pack([small_full[n] for n in names]), True),
                     [small_full[n].shape for n in names])
    summed = dict(zip(names, summed))
    for n in SMALL_SHARDED:
        width = w[n].shape[-1]
        summed[n] = lax.dynamic_slice_in_dim(summed[n], mine * width, width, axis=summed[n].ndim - 1)
    for group, tag in ((REPLICATED, "replicated"), (SMALL_SHARDED, "small_sharded")):
        shapes = [w[n].shape for n in group]
        res = _adamw(f"adamw_{tag}", _pack([summed[n] for n in group])[None],
                     _pack([w[n] for n in group]), _pack([m[n] for n in group]), _pack([v[n] for n in group]))
        unpacked = [_unpack(t, shapes) for t in res]
        for q, n in enumerate(group):
            out[n] = [unpacked[0][q], unpacked[1][q], unpacked[2][q], unpacked[3][q]]

    loss = lax.psum(loss_row[0, 0], ("x", "y", "c"))
    return loss, grad_x, out


def kernel(x, positions, norm_mixer_g, norm_mlp_g, conv_w_pw1, conv_b_pw1, conv_w_dw, conv_b_dw, conv_ln_g, conv_ln_b, conv_w_pw2, conv_b_pw2, mla_w_in, mla_q_norm_g, mla_kv_norm_g, mla_w_q_up, mla_w_kv_up, mla_w_o, mlp_w1, mlp_w2, final_norm_g, loss_target, m_norm_mixer_g, m_norm_mlp_g, m_conv_w_pw1, m_conv_b_pw1, m_conv_w_dw, m_conv_b_dw, m_conv_ln_g, m_conv_ln_b, m_conv_w_pw2, m_conv_b_pw2, m_mla_w_in, m_mla_q_norm_g, m_mla_kv_norm_g, m_mla_w_q_up, m_mla_w_kv_up, m_mla_w_o, m_mlp_w1, m_mlp_w2, m_final_norm_g, v_norm_mixer_g, v_norm_mlp_g, v_conv_w_pw1, v_conv_b_pw1, v_conv_w_dw, v_conv_b_dw, v_conv_ln_g, v_conv_ln_b, v_conv_w_pw2, v_conv_b_pw2, v_mla_w_in, v_mla_q_norm_g, v_mla_kv_norm_g, v_mla_w_q_up, v_mla_w_kv_up, v_mla_w_o, v_mlp_w1, v_mlp_w2, v_final_norm_g):
    ws = (norm_mixer_g, norm_mlp_g, conv_w_pw1, conv_b_pw1, conv_w_dw, conv_b_dw, conv_ln_g, conv_ln_b,
          conv_w_pw2, conv_b_pw2, mla_w_in, mla_q_norm_g, mla_kv_norm_g, mla_w_q_up, mla_w_kv_up, mla_w_o,
          mlp_w1, mlp_w2, final_norm_g)
    ms = (m_norm_mixer_g, m_norm_mlp_g, m_conv_w_pw1, m_conv_b_pw1, m_conv_w_dw, m_conv_b_dw, m_conv_ln_g,
          m_conv_ln_b, m_conv_w_pw2, m_conv_b_pw2, m_mla_w_in, m_mla_q_norm_g, m_mla_kv_norm_g,
          m_mla_w_q_up, m_mla_w_kv_up, m_mla_w_o, m_mlp_w1, m_mlp_w2, m_final_norm_g)
    vs = (v_norm_mixer_g, v_norm_mlp_g, v_conv_w_pw1, v_conv_b_pw1, v_conv_w_dw, v_conv_b_dw, v_conv_ln_g,
          v_conv_ln_b, v_conv_w_pw2, v_conv_b_pw2, v_mla_w_in, v_mla_q_norm_g, v_mla_kv_norm_g,
          v_mla_w_q_up, v_mla_w_kv_up, v_mla_w_o, v_mlp_w1, v_mlp_w2, v_final_norm_g)
    w, m, v = dict(zip(WEIGHTS, ws)), dict(zip(WEIGHTS, ms)), dict(zip(WEIGHTS, vs))
    s, d = x.shape[-2], x.shape[-1]
    loss, grad_x, out = _step(w, m, v, x.reshape(s, d), positions, loss_target.reshape(s, d))
    grads = [out[n][0] for n in WEIGHTS]
    deltas = [out[n][1] for n in WEIGHTS]
    new_m = [out[n][2] for n in WEIGHTS]
    new_v = [out[n][3] for n in WEIGHTS]
    return (loss, grad_x.reshape(x.shape), *grads, *deltas, *new_m, *new_v)
```

```python
import functools

import jax
import jax.numpy as jnp
from jax import lax
from jax.experimental import pallas as pl
from jax.experimental.pallas import tpu as pltpu

F32 = jnp.float32
BF16 = jnp.bfloat16

NORM_EPS = 1e-6
LN_EPS = 1e-5
ROPE_THETA = 10000.0
CHUNK_BITS = 6
HEAD_NOPE = 128
HEAD_ROPE = 64
HEAD_V = 128
HEAD_QK_PAD = 256
CONV_W = 31
HALO = 32
N_DEV = 8

ADAM_LR = 0.001
ADAM_B1 = 0.9
ADAM_B2 = 0.999
ADAM_EPS = 1e-08
ADAM_WD = 0.01
ADAM_STEP = 10

V7X_VMEM_BYTES = 64 * 1024 * 1024
VMEM_LIMIT = (V7X_VMEM_BYTES * 3) // 4
LANE = 128

MESH = pl.DeviceIdType.MESH
ANY = pl.BlockSpec(memory_space=pl.ANY)
VMEM_SPEC = pl.BlockSpec(memory_space=pltpu.VMEM)


def _cp(**kw):
    return pltpu.CompilerParams(vmem_limit_bytes=VMEM_LIMIT, **kw)


SUBLANE = 8
SUBLANE_BF16 = 16

TM_PREF = 1024
TN_PREF = 1024
TK_PREF = 2048


def _tile(n, pref, mult=SUBLANE_BF16):
    if n <= pref + pref // 2:
        return n
    t = (pref // mult) * mult
    while t >= mult:
        if n % t == 0:
            return t
        t -= mult
    return n


def _sigmoid(x):
    return 1.0 / (1.0 + jnp.exp(-x))


def _rot_half(x):
    n = x.shape[-1]
    lane = lax.broadcasted_iota(jnp.int32, x.shape, x.ndim - 1)
    first = (lane & 63) < 32
    return jnp.where(first, pltpu.roll(x, n - 32, x.ndim - 1), pltpu.roll(x, 32, x.ndim - 1))


def _rope(x, c, s):
    return x * c + _rot_half(x) * s


def _rope_t(d, c, s):
    return d * c + _rot_half(d * s)


def _chunk_mask_t(t):
    row = lax.broadcasted_iota(jnp.int32, (t, t), 0)
    col = lax.broadcasted_iota(jnp.int32, (t, t), 1)
    return jnp.right_shift(row, CHUNK_BITS) <= jnp.right_shift(col, CHUNK_BITS)


def _rms_fwd(name, x, g):
    t, d = x.shape
    tm = _tile(t, 512)

    def body(x_ref, g_ref, o_ref):
        xf = x_ref[...]
        r = lax.rsqrt(jnp.mean(xf * xf, axis=-1, keepdims=True) + NORM_EPS)
        o_ref[...] = (xf * r * g_ref[...]).astype(o_ref.dtype)

    return pl.pallas_call(
        body, name=name, grid=(t // tm,),
        in_specs=[pl.BlockSpec((tm, d), lambda i: (i, 0)), pl.BlockSpec((1, d), lambda i: (0, 0))],
        out_specs=pl.BlockSpec((tm, d), lambda i: (i, 0)),
        out_shape=jax.ShapeDtypeStruct((t, d), BF16),
        compiler_params=_cp(),
    )(x, g)


def _rms_bwd_math(xf, g, dy):
    r = lax.rsqrt(jnp.mean(xf * xf, axis=-1, keepdims=True) + NORM_EPS)
    xh = xf * r
    dg = jnp.sum(dy * xh, axis=0, keepdims=True)
    dxh = dy * g
    dx = r * (dxh - xh * jnp.mean(dxh * xh, axis=-1, keepdims=True))
    return dx, dg


def _rms_bwd(name, x, g, dy, resid):
    t, d = x.shape
    tm = _tile(t, 256)

    def body(x_ref, g_ref, dy_ref, r_ref, dx_ref, dxb_ref, dg_ref, cs_ref):
        @pl.when(pl.program_id(0) == 0)
        def _():
            dg_ref[...] = jnp.zeros_like(dg_ref)
            cs_ref[...] = jnp.zeros_like(cs_ref)

        dx, dg = _rms_bwd_math(x_ref[...], g_ref[...], dy_ref[...])
        tot = r_ref[...] + dx
        dx_ref[...] = tot
        dxb_ref[...] = tot.astype(BF16)
        dg_ref[...] += dg
        cs_ref[...] += jnp.sum(tot, axis=0, keepdims=True)

    row = pl.BlockSpec((tm, d), lambda i: (i, 0))
    vec = pl.BlockSpec((1, d), lambda i: (0, 0))
    return pl.pallas_call(
        body, name=name, grid=(t // tm,),
        in_specs=[row, vec, row, row],
        out_specs=[row, row, vec, vec],
        out_shape=[jax.ShapeDtypeStruct((t, d), F32), jax.ShapeDtypeStruct((t, d), BF16),
                   jax.ShapeDtypeStruct((1, d), F32), jax.ShapeDtypeStruct((1, d), F32)],
        compiler_params=_cp(dimension_semantics=("arbitrary",)),
    )(x, g, dy, resid)


def _final_loss(name, x, g, target):
    t, d = x.shape
    tm = _tile(t, 256)

    def body(x_ref, g_ref, t_ref, loss_ref, dx_ref, dxb_ref, dg_ref, cs_ref):
        @pl.when(pl.program_id(0) == 0)
        def _():
            loss_ref[...] = jnp.zeros_like(loss_ref)
            dg_ref[...] = jnp.zeros_like(dg_ref)
            cs_ref[...] = jnp.zeros_like(cs_ref)

        xf = x_ref[...]
        gg = g_ref[...]
        r = lax.rsqrt(jnp.mean(xf * xf, axis=-1, keepdims=True) + NORM_EPS)
        err = xf * r * gg - t_ref[...]
        part = 0.5 * jnp.sum(jnp.mean(err * err, axis=-1, keepdims=True), axis=0, keepdims=True)
        loss_ref[...] += jnp.broadcast_to(part, loss_ref.shape)
        dx, dg = _rms_bwd_math(xf, gg, err * (1.0 / d))
        dx_ref[...] = dx
        dxb_ref[...] = dx.astype(BF16)
        dg_ref[...] += dg
        cs_ref[...] += jnp.sum(dx, axis=0, keepdims=True)

    row = pl.BlockSpec((tm, d), lambda i: (i, 0))
    vec = pl.BlockSpec((1, d), lambda i: (0, 0))
    return pl.pallas_call(
        body, name=name, grid=(t // tm,),
        in_specs=[row, vec, row],
        out_specs=[pl.BlockSpec((1, LANE), lambda i: (0, 0)), row, row, vec, vec],
        out_shape=[jax.ShapeDtypeStruct((1, LANE), F32), jax.ShapeDtypeStruct((t, d), F32),
                   jax.ShapeDtypeStruct((t, d), BF16), jax.ShapeDtypeStruct((1, d), F32),
                   jax.ShapeDtypeStruct((1, d), F32)],
        compiler_params=_cp(dimension_semantics=("arbitrary",)),
    )(x, g, target)


_DIMS = {
    "nn": (((1,), (0,)), ((), ())),
    "nt": (((1,), (1,)), ((), ())),
    "tn": (((0,), (0,)), ((), ())),
}


def _mm(name, a, bs, *, mode, m, n, k, epilogue, out_shape, out_specs, extras=(), extra_specs=(),
        a_lead=None, aliases=None, tn_div=1):
    tm, tn, tk = _tiles(m, n, k, tn_div)
    nk = k // tk
    nb, ne = len(bs), len(extras)
    no = len(out_shape)
    dims = _DIMS[mode]

    def with_lead(shape, idx, lead):
        if lead is None:
            return pl.BlockSpec(shape, idx)
        return pl.BlockSpec((None,) + shape, lambda i, j, kk: (lead,) + idx(i, j, kk))

    if mode == "tn":
        a_spec = with_lead((tk, tm), lambda i, j, kk: (kk, i), a_lead)
    else:
        a_spec = with_lead((tm, tk), lambda i, j, kk: (i, kk), a_lead)
    b_specs = []
    for _, lead, off in bs:
        if mode == "nt":
            b_specs.append(with_lead((tn, tk), lambda i, j, kk, off=off: (j + off, kk), lead))
        else:
            b_specs.append(with_lead((tk, tn), lambda i, j, kk, off=off: (kk, j + off), lead))

    def body(*refs):
        a_ref = refs[0]
        b_refs = refs[1:1 + nb]
        ex = refs[1 + nb:1 + nb + ne]
        outs = refs[1 + nb + ne:1 + nb + ne + no]
        accs = refs[1 + nb + ne + no:]

        def part(b_ref):
            return lax.dot_general(a_ref[...], b_ref[...], dims, preferred_element_type=F32)

        if nk == 1:
            epilogue([part(b_ref) for b_ref in b_refs], ex, outs)
            return
        kk = pl.program_id(2)

        @pl.when(kk == 0)
        def _():
            for acc, b_ref in zip(accs, b_refs):
                acc[...] = part(b_ref)

        @pl.when(kk > 0)
        def _():
            for acc, b_ref in zip(accs, b_refs):
                acc[...] += part(b_ref)

        @pl.when(kk == nk - 1)
        def _():
            epilogue([acc[...] for acc in accs], ex, outs)

    scratch = [pltpu.VMEM((tm, tn), F32) for _ in range(nb)] if nk > 1 else []
    return pl.pallas_call(
        body, name=name, grid=(m // tm, n // tn, nk),
        in_specs=[a_spec] + b_specs + list(extra_specs),
        out_specs=list(out_specs), out_shape=list(out_shape), scratch_shapes=scratch,
        input_output_aliases=aliases or {},
        compiler_params=_cp(dimension_semantics=("arbitrary", "arbitrary", "arbitrary")),
    )(a, *[b for b, _, _ in bs], *extras), (tm, tn, tk)


def _ij(tm, tn):
    return pl.BlockSpec((tm, tn), lambda i, j, kk: (i, j))


def _tiles(m, n, k, tn_div=1):
    return _tile(m, TM_PREF), _tile(n, TN_PREF // tn_div, LANE), _tile(k, TK_PREF, LANE)


def _mm_plain(name, a, b, b_lead, mode, out_dtype):
    m, k = a.shape
    n = b.shape[-1] if mode == "nn" else b.shape[-2]
    tm, tn, _ = _tiles(m, n, k)

    def epilogue(accs, ex, outs):
        outs[0][...] = accs[0].astype(out_dtype)

    return _mm(name, a, [(b, b_lead, 0)], mode=mode, m=m, n=n, k=k, epilogue=epilogue,
               out_shape=[jax.ShapeDtypeStruct((m, n), out_dtype)], out_specs=[_ij(tm, tn)])[0][0]


def _mm_res(name, a, b, b_lead, resid, bias=None):
    m, k = a.shape
    n = b.shape[-1]
    tm, tn, _ = _tiles(m, n, k)
    extras, specs = [resid], [_ij(tm, tn)]
    if bias is not None:
        extras.append(bias)
        specs.append(pl.BlockSpec((1, tn), lambda i, j, kk: (0, j)))

    def epilogue(accs, ex, outs):
        y = ex[0][...] + accs[0]
        if bias is not None:
            y = y + ex[1][...]
        outs[0][...] = y

    return _mm(name, a, [(b, b_lead, 0)], mode="nn", m=m, n=n, k=k, epilogue=epilogue,
               extras=extras, extra_specs=specs,
               out_shape=[jax.ShapeDtypeStruct((m, n), F32)], out_specs=[_ij(tm, tn)])[0][0]


def _mm_mlp_up(name, h, w1, lead):
    m, k = h.shape
    n = w1.shape[-1]
    tm, tn, _ = _tiles(m, n, k)

    def epilogue(accs, ex, outs):
        z = accs[0]
        outs[0][...] = z.astype(BF16)
        r = jnp.maximum(z, 0.0)
        outs[1][...] = (r * r).astype(BF16)

    sh = jax.ShapeDtypeStruct((m, n), BF16)
    return _mm(name, h, [(w1, lead, 0)], mode="nn", m=m, n=n, k=k, epilogue=epilogue,
               out_shape=[sh, sh], out_specs=[_ij(tm, tn), _ij(tm, tn)])[0]


def _mm_mlp_dz(name, g, w2, lead, z):
    m, k = g.shape
    n = w2.shape[-2]
    tm, tn, _ = _tiles(m, n, k)

    def epilogue(accs, ex, outs):
        outs[0][...] = (accs[0] * (2.0 * jnp.maximum(ex[0][...].astype(F32), 0.0))).astype(BF16)

    return _mm(name, g, [(w2, lead, 0)], mode="nt", m=m, n=n, k=k, epilogue=epilogue,
               extras=[z], extra_specs=[_ij(tm, tn)],
               out_shape=[jax.ShapeDtypeStruct((m, n), BF16)], out_specs=[_ij(tm, tn)])[0][0]


def _mm_glu(name, h, w, lead, bias):
    m, k = h.shape
    n = w.shape[-1] // 2
    tm, tn, _ = _tiles(m, n, k, 2)
    off = n // tn

    def epilogue(accs, ex, outs):
        a = accs[0] + ex[0][...]
        gate = accs[1] + ex[1][...]
        outs[0][...] = a.astype(BF16)
        outs[1][...] = gate.astype(BF16)
        outs[2][...] = a * _sigmoid(gate)

    shb = jax.ShapeDtypeStruct((m, n), BF16)
    return _mm(name, h, [(w, lead, 0), (w, lead, off)], mode="nn", m=m, n=n, k=k, epilogue=epilogue,
               extras=[bias, bias],
               extra_specs=[pl.BlockSpec((1, tn), lambda i, j, kk: (0, j)),
                            pl.BlockSpec((1, tn), lambda i, j, kk: (0, j + off))],
               out_shape=[shb, shb, jax.ShapeDtypeStruct((m, n), F32)],
               out_specs=[_ij(tm, tn)] * 3, tn_div=2)[0]


def _mm_q(name, qn, wq_pad, lead, cq, sq):
    m, k = qn.shape
    n = wq_pad.shape[-1]
    tm, tn, _ = _tiles(m, n, k)
    rep = tn // HEAD_QK_PAD

    def epilogue(accs, ex, outs):
        c = jnp.tile(ex[0][...], (1, rep))
        s = jnp.tile(ex[1][...], (1, rep))
        outs[0][...] = _rope(accs[0], c, s).astype(BF16)

    tab = pl.BlockSpec((tm, HEAD_QK_PAD), lambda i, j, kk: (i, 0))
    return _mm(name, qn, [(wq_pad, lead, 0)], mode="nn", m=m, n=n, k=k, epilogue=epilogue,
               extras=[cq, sq], extra_specs=[tab, tab],
               out_shape=[jax.ShapeDtypeStruct((m, n), BF16)], out_specs=[_ij(tm, tn)])[0][0]


def _mm_kv(name, kvn, wkv, lead, kpe):
    m, k = kvn.shape
    n = wkv.shape[-1]
    tm, tn, _ = _tiles(m, n, k)
    heads = tn // (HEAD_NOPE + HEAD_V)

    def epilogue(accs, ex, outs):
        acc = accs[0]
        pe = ex[0][...].astype(F32)
        kparts, vparts = [], []
        for hh in range(heads):
            base = hh * (HEAD_NOPE + HEAD_V)
            kparts += [acc[:, base:base + HEAD_NOPE], pe]
            vparts.append(acc[:, base + HEAD_NOPE:base + HEAD_NOPE + HEAD_V])
        kf = jnp.concatenate(kparts, axis=1)
        vv = jnp.concatenate(vparts, axis=1) if heads > 1 else vparts[0]
        outs[0][...] = kf.astype(BF16)
        outs[1][...] = vv.astype(BF16)
        outs[2][...] = kf.T.astype(BF16)
        outs[3][...] = vv.T.astype(BF16)

    def ji(tn_, tm_):
        return pl.BlockSpec((tn_, tm_), lambda i, j, kk: (j, i))

    return _mm(name, kvn, [(wkv, lead, 0)], mode="nn", m=m, n=n, k=k, epilogue=epilogue,
               extras=[kpe], extra_specs=[pl.BlockSpec((tm, LANE), lambda i, j, kk: (i, 0))],
               out_shape=[jax.ShapeDtypeStruct((m, n), BF16), jax.ShapeDtypeStruct((m, n // 2), BF16),
                          jax.ShapeDtypeStruct((n, m), BF16), jax.ShapeDtypeStruct((n // 2, m), BF16)],
               out_specs=[_ij(tm, tn), _ij(tm, tn // 2), ji(tn, tm), ji(tn // 2, tm)])[0]


def _mm_wgrad(name, a, b):
    t, m = a.shape
    n = b.shape[-1]
    tm, tn, _ = _tiles(m, n, t)

    def epilogue(accs, ex, outs):
        outs[0][...] = accs[0].astype(BF16)

    return _mm(name, a, [(b, None, 0)], mode="tn", m=m, n=n, k=t, epilogue=epilogue,
               out_shape=[jax.ShapeDtypeStruct((m, n), BF16)], out_specs=[_ij(tm, tn)])[0][0]


CONV_ROWS = 256
CONV_RT = 64
CONV_CW = 256
CONV_LR = 32


def _ln_stats(c):
    mu = jnp.mean(c, axis=-1, keepdims=True)
    xc = c - mu
    rstd = lax.rsqrt(jnp.mean(xc * xc, axis=-1, keepdims=True) + LN_EPS)
    return xc * rstd, rstd


def _conv_fwd(name, glu, w_dw, b_dw, ln_g, ln_b):
    t, d = glu.shape
    tt = _tile(t, CONV_ROWS)
    rt, cw, lr = min(CONV_RT, tt), min(CONV_CW, d), min(CONV_LR, tt)
    hb = tt // HALO

    def body(gc_ref, gp_ref, w_ref, b_ref, lg_ref, lb_ref, c_ref, s_ref, buf, win):
        i = pl.program_id(0)
        buf[0:HALO, :] = jnp.where(i > 0, gp_ref[...], 0.0)
        buf[HALO:HALO + tt, :] = gc_ref[...]

        def chunk(cb, carry):
            col = pl.ds(pl.multiple_of(cb * cw, cw), cw)
            for r0 in range(0, tt, rt):
                acc = jnp.broadcast_to(b_ref[:, col], (rt, cw))
                for b in range(SUBLANE):
                    amax = (CONV_W - 1 - b) // SUBLANE
                    lo = r0 + HALO - (CONV_W - 1) + b
                    rows = rt + SUBLANE * amax
                    win[0:rows, :] = buf[lo:lo + rows, col]
                    for a in range(amax + 1):
                        k = SUBLANE * a + b
                        acc = acc + w_ref[k:k + 1, col] * win[SUBLANE * a:SUBLANE * a + rt, :]
                c_ref[r0:r0 + rt, col] = acc
            return carry

        lax.fori_loop(0, d // cw, chunk, 0)

        def ln(r, carry):
            rows = pl.ds(pl.multiple_of(r * lr, lr), lr)
            xh, _ = _ln_stats(c_ref[rows, :])
            y = xh * lg_ref[...] + lb_ref[...]
            s_ref[rows, :] = (y * _sigmoid(y)).astype(BF16)
            return carry

        lax.fori_loop(0, tt // lr, ln, 0)

    row = pl.BlockSpec((tt, d), lambda i: (i, 0))
    vec = pl.BlockSpec((1, d), lambda i: (0, 0))
    return pl.pallas_call(
        body, name=name, grid=(t // tt,),
        in_specs=[row, pl.BlockSpec((HALO, d), lambda i: (jnp.maximum(i * hb - 1, 0), 0)),
                  pl.BlockSpec((HALO, d), lambda i: (0, 0)), vec, vec, vec],
        out_specs=[row, row],
        out_shape=[jax.ShapeDtypeStruct((t, d), F32), jax.ShapeDtypeStruct((t, d), BF16)],
        scratch_shapes=[pltpu.VMEM((HALO + tt, d), F32), pltpu.VMEM((rt + HALO, cw), F32)],
        compiler_params=_cp(dimension_semantics=("arbitrary",)),
    )(glu, glu, w_dw, b_dw, ln_g, ln_b)


def _conv_bwd_ln(name, ds, c, ln_g, ln_b):
    t, d = c.shape
    tt = _tile(t, CONV_ROWS)
    lr = min(CONV_LR, tt)

    def body(ds_ref, c_ref, lg_ref, lb_ref, dc_ref, dg_ref, db_ref, dbdw_ref):
        @pl.when(pl.program_id(0) == 0)
        def _():
            dg_ref[...] = jnp.zeros_like(dg_ref)
            db_ref[...] = jnp.zeros_like(db_ref)
            dbdw_ref[...] = jnp.zeros_like(dbdw_ref)

        def chunk(r, carry):
            rows = pl.ds(pl.multiple_of(r * lr, lr), lr)
            xh, rstd = _ln_stats(c_ref[rows, :])
            g = lg_ref[...]
            y = xh * g + lb_ref[...]
            sg = _sigmoid(y)
            dy = ds_ref[rows, :] * (sg * (1.0 + y * (1.0 - sg)))
            dxh = dy * g
            dc = rstd * (dxh - jnp.mean(dxh, axis=-1, keepdims=True)
                         - xh * jnp.mean(dxh * xh, axis=-1, keepdims=True))
            dc_ref[rows, :] = dc
            dg_ref[...] += jnp.sum(dy * xh, axis=0, keepdims=True)
            db_ref[...] += jnp.sum(dy, axis=0, keepdims=True)
            dbdw_ref[...] += jnp.sum(dc, axis=0, keepdims=True)
            return carry

        lax.fori_loop(0, tt // lr, chunk, 0)

    row = pl.BlockSpec((tt, d), lambda i: (i, 0))
    vec = pl.BlockSpec((1, d), lambda i: (0, 0))
    vsh = jax.ShapeDtypeStruct((1, d), F32)
    return pl.pallas_call(
        body, name=name, grid=(t // tt,),
        in_specs=[row, row, vec, vec], out_specs=[row, vec, vec, vec],
        out_shape=[jax.ShapeDtypeStruct((t, d), F32), vsh, vsh, vsh],
        compiler_params=_cp(dimension_semantics=("arbitrary",)),
    )(ds, c, ln_g, ln_b)


def _conv_bwd_dw(name, dc, glu, ua, ug, w_dw):
    t, d = dc.shape
    tt = _tile(t, CONV_ROWS)
    rt, cw = min(CONV_RT, tt), min(CONV_CW, d)
    hb = tt // HALO
    nt = t // tt

    def body(dcc_ref, dcn_ref, gc_ref, gp_ref, ua_ref, ug_ref, w_ref,
             du_ref, dw_ref, dbu_ref, dbuf, gbuf, wacc, dwin, gwin):
        i = pl.program_id(0)

        @pl.when(i == 0)
        def _():
            wacc[...] = jnp.zeros_like(wacc)
            dbu_ref[...] = jnp.zeros_like(dbu_ref)

        dbuf[0:tt, :] = dcc_ref[...]
        dbuf[tt:tt + HALO, :] = jnp.where(i < nt - 1, dcn_ref[...], 0.0)
        gbuf[0:HALO, :] = jnp.where(i > 0, gp_ref[...], 0.0)
        gbuf[HALO:HALO + tt, :] = gc_ref[...]

        def chunk(cb, carry):
            c0 = pl.multiple_of(cb * cw, cw)
            col = pl.ds(c0, cw)
            colg = pl.ds(pl.multiple_of(d + cb * cw, cw), cw)
            for r0 in range(0, tt, rt):
                dcr = dbuf[r0:r0 + rt, col]
                dgl = jnp.zeros((rt, cw), F32)
                for b in range(SUBLANE):
                    amax = (CONV_W - 1 - b) // SUBLANE
                    hi = r0 + (CONV_W - 1) - b - SUBLANE * amax
                    rows = rt + SUBLANE * amax
                    dwin[0:rows, :] = dbuf[hi:hi + rows, col]
                    lo = r0 + HALO - (CONV_W - 1) + b
                    gwin[0:rows, :] = gbuf[lo:lo + rows, col]
                    for a in range(amax + 1):
                        k = SUBLANE * a + b
                        back = SUBLANE * (amax - a)
                        dgl = dgl + w_ref[k:k + 1, col] * dwin[back:back + rt, :]
                        prod = dcr * gwin[SUBLANE * a:SUBLANE * a + rt, :]
                        part = prod[0:8, :]
                        for r in range(8, rt, 8):
                            part = part + prod[r:r + 8, :]
                        wacc[8 * k:8 * k + 8, col] += part
                a = ua_ref[r0:r0 + rt, col].astype(F32)
                sg = _sigmoid(ug_ref[r0:r0 + rt, col].astype(F32))
                da = dgl * sg
                dgate = dgl * a * sg * (1.0 - sg)
                du_ref[r0:r0 + rt, col] = da.astype(BF16)
                du_ref[r0:r0 + rt, colg] = dgate.astype(BF16)
                dbu_ref[:, col] += jnp.sum(da, axis=0, keepdims=True)
                dbu_ref[:, colg] += jnp.sum(dgate, axis=0, keepdims=True)
            return carry

        lax.fori_loop(0, d // cw, chunk, 0)

        @pl.when(i == nt - 1)
        def _():
            for k in range(CONV_W):
                dw_ref[k:k + 1, :] = jnp.sum(wacc[8 * k:8 * k + 8, :], axis=0, keepdims=True)
            dw_ref[CONV_W:HALO, :] = jnp.zeros((HALO - CONV_W, d), F32)

    row = pl.BlockSpec((tt, d), lambda i: (i, 0))
    return pl.pallas_call(
        body, name=name, grid=(nt,),
        in_specs=[row, pl.BlockSpec((HALO, d), lambda i: (jnp.minimum((i + 1) * hb, t // HALO - 1), 0)),
                  row, pl.BlockSpec((HALO, d), lambda i: (jnp.maximum(i * hb - 1, 0), 0)),
                  row, row, pl.BlockSpec((HALO, d), lambda i: (0, 0))],
        out_specs=[pl.BlockSpec((tt, 2 * d), lambda i: (i, 0)),
                   pl.BlockSpec((HALO, d), lambda i: (0, 0)),
                   pl.BlockSpec((1, 2 * d), lambda i: (0, 0))],
        out_shape=[jax.ShapeDtypeStruct((t, 2 * d), BF16), jax.ShapeDtypeStruct((HALO, d), F32),
                   jax.ShapeDtypeStruct((1, 2 * d), F32)],
        scratch_shapes=[pltpu.VMEM((tt + HALO, d), F32), pltpu.VMEM((HALO + tt, d), F32),
                        pltpu.VMEM((8 * HALO, d), F32),
                        pltpu.VMEM((rt + HALO, cw), F32), pltpu.VMEM((rt + HALO, cw), F32)],
        compiler_params=_cp(dimension_semantics=("arbitrary",)),
    )(dc, dc, glu, glu, ua, ug, w_dw)


def _mla_mid_fwd(name, down, qg, kvg, ck, sk):
    t, w = down.shape
    rq, rkv = qg.shape[-1], kvg.shape[-1]
    tm = _tile(t, 512)

    def body(dn_ref, qg_ref, kvg_ref, ck_ref, sk_ref, qn_ref, kvn_ref, kpe_ref):
        cq = dn_ref[:, 0:rq]
        ckv = dn_ref[:, rq:rq + rkv]
        pe = dn_ref[:, rq + rkv:rq + rkv + LANE]
        qn_ref[...] = (cq * lax.rsqrt(jnp.mean(cq * cq, axis=-1, keepdims=True) + NORM_EPS)
                       * qg_ref[...]).astype(BF16)
        kvn_ref[...] = (ckv * lax.rsqrt(jnp.mean(ckv * ckv, axis=-1, keepdims=True) + NORM_EPS)
                        * kvg_ref[...]).astype(BF16)
        kpe_ref[...] = _rope(pe, ck_ref[...], sk_ref[...]).astype(BF16)

    def row(n):
        return pl.BlockSpec((tm, n), lambda i: (i, 0))

    def vec(n):
        return pl.BlockSpec((1, n), lambda i: (0, 0))

    return pl.pallas_call(
        body, name=name, grid=(t // tm,),
        in_specs=[row(w), vec(rq), vec(rkv), row(LANE), row(LANE)],
        out_specs=[row(rq), row(rkv), row(LANE)],
        out_shape=[jax.ShapeDtypeStruct((t, rq), BF16), jax.ShapeDtypeStruct((t, rkv), BF16),
                   jax.ShapeDtypeStruct((t, LANE), BF16)],
        compiler_params=_cp(),
    )(down, qg, kvg, ck, sk)


def _mla_mid_bwd(name, down, qg, kvg, dqn, dkvn, dkpe):
    t, w = down.shape
    rq, rkv = qg.shape[-1], kvg.shape[-1]
    tm = _tile(t, 256)

    def body(dn_ref, qg_ref, kvg_ref, dqn_ref, dkvn_ref, dkpe_ref, dd_ref, dqg_ref, dkvg_ref):
        @pl.when(pl.program_id(0) == 0)
        def _():
            dqg_ref[...] = jnp.zeros_like(dqg_ref)
            dkvg_ref[...] = jnp.zeros_like(dkvg_ref)

        dcq, dqg = _rms_bwd_math(dn_ref[:, 0:rq], qg_ref[...], dqn_ref[...])
        dckv, dkvg = _rms_bwd_math(dn_ref[:, rq:rq + rkv], kvg_ref[...], dkvn_ref[...])
        dd_ref[:, 0:rq] = dcq.astype(BF16)
        dd_ref[:, rq:rq + rkv] = dckv.astype(BF16)
        dd_ref[:, rq + rkv:rq + rkv + LANE] = dkpe_ref[...].astype(BF16)
        dqg_ref[...] += dqg
        dkvg_ref[...] += dkvg

    def row(n):
        return pl.BlockSpec((tm, n), lambda i: (i, 0))

    def vec(n):
        return pl.BlockSpec((1, n), lambda i: (0, 0))

    return pl.pallas_call(
        body, name=name, grid=(t // tm,),
        in_specs=[row(w), vec(rq), vec(rkv), row(rq), row(rkv), row(LANE)],
        out_specs=[row(w), vec(rq), vec(rkv)],
        out_shape=[jax.ShapeDtypeStruct((t, w), BF16), jax.ShapeDtypeStruct((1, rq), F32),
                   jax.ShapeDtypeStruct((1, rkv), F32)],
        compiler_params=_cp(dimension_semantics=("arbitrary",)),
    )(down, qg, kvg, dqn, dkvn, dkpe)


ATT_TILE = 512
ATT_HEADS = 2
_NT = (((1,), (1,)), ((), ()))
_TN = (((0,), (0,)), ((), ()))


def _flash_fwd(name, qf, kf, vt, heads):
    s = qf.shape[0]
    t = _tile(s, ATT_TILE)
    n = s // t
    g = min(ATT_HEADS, heads)
    qw, vw = HEAD_QK_PAD, HEAD_V

    def body(q_ref, k_ref, vt_ref, o_ref, lse_ref, m_sc, l_sc, acc_sc):
        i, j = pl.program_id(1), pl.program_id(2)

        @pl.when(j == 0)
        def _():
            m_sc[...] = jnp.full(m_sc.shape, -jnp.inf, F32)
            l_sc[...] = jnp.zeros_like(l_sc)
            acc_sc[...] = jnp.zeros_like(acc_sc)

        def step(diag):
            for hh in range(g):
                sc = lax.dot_general(k_ref[:, hh * qw:(hh + 1) * qw], q_ref[:, hh * qw:(hh + 1) * qw], _NT,
                                     preferred_element_type=F32)
                if diag:
                    sc = jnp.where(_chunk_mask_t(t), sc, -jnp.inf)
                m_old = m_sc[hh]
                m_new = jnp.maximum(m_old, jnp.max(sc, axis=0, keepdims=True))
                alpha = jnp.exp(m_old - m_new)
                p = jnp.exp(sc - m_new)
                l_sc[hh] = alpha * l_sc[hh] + jnp.sum(p, axis=0, keepdims=True)
                acc_sc[hh] = alpha * acc_sc[hh] + jnp.dot(vt_ref[hh * vw:(hh + 1) * vw, :], p.astype(BF16),
                                                          preferred_element_type=F32)
                m_sc[hh] = m_new

        @pl.when(j < i)
        def _():
            step(False)

        @pl.when(j == i)
        def _():
            step(True)
            for hh in range(g):
                l = l_sc[hh]
                o_ref[:, hh * vw:(hh + 1) * vw] = (acc_sc[hh] / l).T.astype(BF16)
                lse_ref[hh] = m_sc[hh] + jnp.log(l)

    return pl.pallas_call(
        body, name=name, grid=(heads // g, n, n),
        in_specs=[pl.BlockSpec((t, g * qw), lambda h, i, j: (i, h)),
                  pl.BlockSpec((t, g * qw), lambda h, i, j: (jnp.minimum(j, i), h)),
                  pl.BlockSpec((g * vw, t), lambda h, i, j: (h, jnp.minimum(j, i)))],
        out_specs=[pl.BlockSpec((t, g * vw), lambda h, i, j: (i, h)),
                   pl.BlockSpec((g, 1, t), lambda h, i, j: (h, 0, i))],
        out_shape=[jax.ShapeDtypeStruct((s, heads * vw), BF16),
                   jax.ShapeDtypeStruct((heads, 1, s), F32)],
        scratch_shapes=[pltpu.VMEM((g, 1, t), F32), pltpu.VMEM((g, 1, t), F32), pltpu.VMEM((g, vw, t), F32)],
        compiler_params=_cp(dimension_semantics=("arbitrary", "arbitrary", "arbitrary")),
    )(qf, kf, vt)


def _flash_bwd_dq(name, qf, kf, kft, v, do, o, lse, cq, sq, heads):
    s = qf.shape[0]
    t = _tile(s, ATT_TILE)
    n = s // t
    g = min(ATT_HEADS, heads)
    qw, vw = HEAD_QK_PAD, HEAD_V

    def body(q_ref, k_ref, kt_ref, v_ref, do_ref, o_ref, lse_ref, c_ref, s_ref, dq_ref, dl_ref, acc_sc):
        i, j = pl.program_id(1), pl.program_id(2)

        @pl.when(j == 0)
        def _():
            acc_sc[...] = jnp.zeros_like(acc_sc)
            for hh in range(g):
                cols = slice(hh * vw, (hh + 1) * vw)
                col = jnp.sum(do_ref[:, cols].astype(F32) * o_ref[:, cols].astype(F32), axis=1, keepdims=True)
                dl_ref[hh] = jnp.broadcast_to(col, (t, LANE)).T[0:1, :]

        def step(diag):
            for hh in range(g):
                sc = lax.dot_general(k_ref[:, hh * qw:(hh + 1) * qw], q_ref[:, hh * qw:(hh + 1) * qw], _NT,
                                     preferred_element_type=F32)
                p = jnp.exp(sc - lse_ref[hh])
                if diag:
                    p = jnp.where(_chunk_mask_t(t), p, 0.0)
                dp = lax.dot_general(v_ref[:, hh * vw:(hh + 1) * vw], do_ref[:, hh * vw:(hh + 1) * vw], _NT,
                                     preferred_element_type=F32)
                ds = (p * (dp - dl_ref[hh])).astype(BF16)
                acc_sc[hh] += jnp.dot(kt_ref[hh * qw:(hh + 1) * qw, :], ds, preferred_element_type=F32)

        @pl.when(j < i)
        def _():
            step(False)

        @pl.when(j == i)
        def _():
            step(True)
            for hh in range(g):
                dq_ref[:, hh * qw:(hh + 1) * qw] = _rope_t(acc_sc[hh].T, c_ref[...], s_ref[...]).astype(BF16)

    qspec = pl.BlockSpec((t, g * qw), lambda h, i, j: (i, h))
    ospec = pl.BlockSpec((t, g * vw), lambda h, i, j: (i, h))
    vspec = pl.BlockSpec((g, 1, t), lambda h, i, j: (h, 0, i))
    tab = pl.BlockSpec((t, qw), lambda h, i, j: (i, 0))
    return pl.pallas_call(
        body, name=name, grid=(heads // g, n, n),
        in_specs=[qspec,
                  pl.BlockSpec((t, g * qw), lambda h, i, j: (jnp.minimum(j, i), h)),
                  pl.BlockSpec((g * qw, t), lambda h, i, j: (h, jnp.minimum(j, i))),
                  pl.BlockSpec((t, g * vw), lambda h, i, j: (jnp.minimum(j, i), h)),
                  ospec, ospec, vspec, tab, tab],
        out_specs=[qspec, vspec],
        out_shape=[jax.ShapeDtypeStruct(qf.shape, BF16), jax.ShapeDtypeStruct((heads, 1, s), F32)],
        scratch_shapes=[pltpu.VMEM((g, qw, t), F32)],
        compiler_params=_cp(dimension_semantics=("arbitrary", "arbitrary", "arbitrary")),
    )(qf, kf, kft, v, do, o, lse, cq, sq)


def _flash_bwd_dkv(name, qf, kf, v, do, lse, delta, ck, sk, heads):
    s = qf.shape[0]
    t = _tile(s, ATT_TILE)
    n = s // t
    g = min(ATT_HEADS, heads)
    qw, vw = HEAD_QK_PAD, HEAD_V
    lse_rows, delta_rows = lse, delta

    def body(q_ref, k_ref, v_ref, do_ref, lse_ref, dl_ref, c_ref, s_ref, dkv_ref, dpe_ref, dk_sc, dv_sc):
        j, h, i = pl.program_id(0), pl.program_id(1), pl.program_id(2)

        @pl.when(i == 0)
        def _():
            dk_sc[...] = jnp.zeros_like(dk_sc)
            dv_sc[...] = jnp.zeros_like(dv_sc)

        def step(diag):
            for hh in range(g):
                q = q_ref[:, hh * qw:(hh + 1) * qw]
                do = do_ref[:, hh * vw:(hh + 1) * vw]
                sc = lax.dot_general(k_ref[:, hh * qw:(hh + 1) * qw], q, _NT, preferred_element_type=F32)
                p = jnp.exp(sc - lse_ref[hh])
                if diag:
                    p = jnp.where(_chunk_mask_t(t), p, 0.0)
                dv_sc[hh] += jnp.dot(p.astype(BF16), do, preferred_element_type=F32)
                dp = lax.dot_general(v_ref[:, hh * vw:(hh + 1) * vw], do, _NT, preferred_element_type=F32)
                ds = (p * (dp - dl_ref[hh])).astype(BF16)
                dk_sc[hh] += jnp.dot(ds, q, preferred_element_type=F32)

        @pl.when(i > j)
        def _():
            step(False)

        @pl.when(i == j)
        def _():
            step(True)

        @pl.when(i == n - 1)
        def _():
            pe = None
            for hh in range(g):
                dk = dk_sc[hh]
                dkv_ref[:, hh * qw:(hh + 1) * qw] = jnp.concatenate([dk[:, 0:HEAD_NOPE], dv_sc[hh]],
                                                                     axis=1).astype(BF16)
                part = dk[:, HEAD_NOPE:HEAD_QK_PAD]
                pe = part if pe is None else pe + part

            @pl.when(h == 0)
            def _():
                dpe_ref[...] = pe

            @pl.when(h > 0)
            def _():
                dpe_ref[...] += pe

            @pl.when(h == heads // g - 1)
            def _():
                dpe_ref[...] = _rope_t(dpe_ref[...], c_ref[...], s_ref[...])

    qrow = lambda j, h, i: (jnp.maximum(i, j), h)
    vrow = lambda j, h, i: (h, 0, jnp.maximum(i, j))
    return pl.pallas_call(
        body, name=name, grid=(n, heads // g, n),
        in_specs=[pl.BlockSpec((t, g * qw), qrow),
                  pl.BlockSpec((t, g * qw), lambda j, h, i: (j, h)),
                  pl.BlockSpec((t, g * vw), lambda j, h, i: (j, h)),
                  pl.BlockSpec((t, g * vw), qrow),
                  pl.BlockSpec((g, 1, t), vrow),
                  pl.BlockSpec((g, 1, t), vrow),
                  pl.BlockSpec((t, LANE), lambda j, h, i: (j, 0)),
                  pl.BlockSpec((t, LANE), lambda j, h, i: (j, 0))],
        out_specs=[pl.BlockSpec((t, g * (HEAD_NOPE + HEAD_V)), lambda j, h, i: (j, h)),
                   pl.BlockSpec((t, LANE), lambda j, h, i: (j, 0))],
        out_shape=[jax.ShapeDtypeStruct((s, heads * (HEAD_NOPE + HEAD_V)), BF16),
                   jax.ShapeDtypeStruct((s, LANE), F32)],
        scratch_shapes=[pltpu.VMEM((g, t, qw), F32), pltpu.VMEM((g, t, vw), F32)],
        compiler_params=_cp(dimension_semantics=("arbitrary", "arbitrary", "arbitrary")),
    )(qf, kf, v, do, lse_rows, delta_rows, ck, sk)


def _adamw(name, parts, w, m, v):
    p, r, c = parts.shape
    tr = _tile(r, max(8, (256 * 1024) // max(c, 1)))
    bc1 = 1.0 - ADAM_B1 ** ADAM_STEP
    bc2 = 1.0 - ADAM_B2 ** ADAM_STEP

    def body(p_ref, w_ref, m_ref, v_ref, g_ref, d_ref, nm_ref, nv_ref):
        g = p_ref[0].astype(F32)
        for q in range(1, p):
            g = g + p_ref[q].astype(F32)
        nm = ADAM_B1 * m_ref[...] + (1.0 - ADAM_B1) * g
        nv = ADAM_B2 * v_ref[...] + (1.0 - ADAM_B2) * (g * g)
        g_ref[...] = g
        nm_ref[...] = nm
        nv_ref[...] = nv
        d_ref[...] = -ADAM_LR * ((nm / bc1) / (jnp.sqrt(nv / bc2) + ADAM_EPS) + ADAM_WD * w_ref[...])

    blk = pl.BlockSpec((tr, c), lambda i: (i, 0))
    sh = jax.ShapeDtypeStruct((r, c), F32)
    return pl.pallas_call(
        body, name=name, grid=(r // tr,),
        in_specs=[pl.BlockSpec((p, tr, c), lambda i: (0, i, 0)), blk, blk, blk],
        out_specs=[blk] * 4, out_shape=[sh] * 4,
        compiler_params=_cp(),
    )(parts, w, m, v)


def _my_place():
    x, y, c = lax.axis_index("x"), lax.axis_index("y"), lax.axis_index("c")
    return x, y, c


def _flip(v, bit):
    return 1 - v if bit else v


def _block(ref, axis, idx, size):
    return ref.at[(slice(None),) * axis + (pl.ds(idx * size, size),)]


HBM_SPEC = pl.BlockSpec(memory_space=pltpu.HBM)
SEM_SPEC = pl.BlockSpec(memory_space=pltpu.SEMAPHORE)
DATAFLOW = pltpu.SideEffectType.DATAFLOW_SIDE_EFFECTING


def _hbm(a):
    return pltpu.with_memory_space_constraint(a, pltpu.HBM)


def _remote_copies(jobs, bufs, send_sems, recv_sems):
    return [pltpu.make_async_remote_copy(src_ref=src, dst_ref=dst, send_sem=send_sems.at[q],
                                         recv_sem=recv_sems.at[q], device_id=dev, device_id_type=MESH)
            for q, (src, dst, dev) in enumerate(jobs(bufs))]


def _split_start(name, bufs, jobs, n_jobs, after):
    nb = len(bufs)

    def body(*refs):
        send_sems, recv_sems = refs[nb + 1], refs[nb + 2]
        for cp in _remote_copies(jobs, refs[:nb], send_sems, recv_sems):
            cp.start()
        refs[-1][...] = jnp.zeros_like(refs[-1])

    outs = pl.pallas_call(
        body, name=name,
        out_shape=(pltpu.SemaphoreType.DMA((n_jobs,)), pltpu.SemaphoreType.DMA((n_jobs,)),
                   *[pltpu.HBM(b.shape, b.dtype) for b in bufs], jax.ShapeDtypeStruct((8, LANE), F32)),
        in_specs=[HBM_SPEC] * nb + [ANY],
        out_specs=(SEM_SPEC, SEM_SPEC, *[HBM_SPEC] * nb, VMEM_SPEC),
        input_output_aliases={q: 2 + q for q in range(nb)},
        compiler_params=pltpu.CompilerParams(has_side_effects=DATAFLOW),
    )(*[_hbm(b) for b in bufs], after)
    return outs[0], outs[1], list(outs[2:2 + nb]), outs[-1]


def _split_wait(name, bufs, send_sems, recv_sems, jobs, after):
    nb = len(bufs)

    def body(*refs):
        for cp in _remote_copies(jobs, refs[:nb], refs[nb], refs[nb + 1]):
            cp.wait_send()
            cp.wait_recv()

    outs = pl.pallas_call(
        body, name=name,
        out_shape=tuple(pltpu.HBM(b.shape, b.dtype) for b in bufs),
        in_specs=[HBM_SPEC] * nb + [SEM_SPEC, SEM_SPEC, ANY],
        out_specs=tuple([HBM_SPEC] * nb),
        input_output_aliases={q: q for q in range(nb)},
        compiler_params=pltpu.CompilerParams(has_side_effects=DATAFLOW),
    )(*bufs, send_sems, recv_sems, after)
    return list(outs)


PLACE_TILE_BYTES = 2 * 1024 * 1024


def _own_block_spec(tr, c, nblk, axis):
    if axis == 0:
        return pl.BlockSpec((tr, c), lambda i, me: (me[0] * nblk + i, 0))
    return pl.BlockSpec((tr, c), lambda i, me: (i, me[0]))


def _cast_place(name, w, layer, axis, me):
    _, r, c = w.shape
    tr = _tile(r, max(SUBLANE_BF16, PLACE_TILE_BYTES // (4 * c)))
    nblk = r // tr
    full = (N_DEV * r, c) if axis == 0 else (r, N_DEV * c)

    def body(me_ref, w_ref, o_ref):
        o_ref[...] = w_ref[...].astype(BF16)

    return pl.pallas_call(
        body, name=name,
        grid_spec=pltpu.PrefetchScalarGridSpec(
            num_scalar_prefetch=1, grid=(nblk,),
            in_specs=[pl.BlockSpec((None, tr, c), lambda i, me: (layer, i, 0))],
            out_specs=_own_block_spec(tr, c, nblk, axis)),
        out_shape=jax.ShapeDtypeStruct(full, BF16), compiler_params=_cp(),
    )(me, w)


def _own_place(name, grad, land, layer, axis, me):
    _, _, r, c = land.shape
    tr = _tile(r, max(SUBLANE_BF16, PLACE_TILE_BYTES // (2 * c)))
    nblk = r // tr

    def body(me_ref, g_ref, land_ref, o_ref):
        o_ref[...] = g_ref[...]

    return pl.pallas_call(
        body, name=name,
        grid_spec=pltpu.PrefetchScalarGridSpec(
            num_scalar_prefetch=1, grid=(nblk,),
            in_specs=[_own_block_spec(tr, c, nblk, axis), ANY],
            out_specs=pl.BlockSpec((None, None, tr, c), lambda i, me: (0, layer, i, 0))),
        out_shape=jax.ShapeDtypeStruct(land.shape, land.dtype),
        input_output_aliases={2: 0}, compiler_params=_cp(),
    )(me, grad, land)


def _gather_jobs_a(axes, sizes):
    def jobs(bufs):
        x, y, c = _my_place()
        out = []
        for t, buf in enumerate(bufs):
            blk = _block(buf, axes[t], 4 * x + 2 * y + c, sizes[t])
            for dev in [(x, y, 1 - c), (1 - x, y, c), (x, 1 - y, c), (1 - x, 1 - y, c)]:
                out.append((blk, blk, dev))
        return out
    return jobs


def _gather_jobs_b(axes, sizes):
    nt = len(axes)

    def jobs(bufs):
        x, y, c = _my_place()
        out = []
        for t in range(nt):
            for px, py in [(1 - x, y), (x, 1 - y), (1 - x, 1 - y)]:
                blk = _block(bufs[t], axes[t], 4 * px + 2 * py + c, sizes[t])
                out.append((blk, blk, (x, y, 1 - c)))
        return out
    return jobs


def _exchange_jobs(axes, sizes, layers):
    nt = len(axes)

    def jobs(bufs):
        x, y, c = _my_place()
        out = []
        for k in range(1, N_DEV):
            px, py, pc = _flip(x, k & 4), _flip(y, k & 2), _flip(c, k & 1)
            for t in range(nt):
                out.append((_block(bufs[t], axes[t], 4 * px + 2 * py + pc, sizes[t]),
                            bufs[nt + t].at[k, layers[t]], (px, py, pc)))
        return out
    return jobs


def _gather_begin(name, lands, axes, after):
    sizes = [b.shape[ax] // N_DEV for b, ax in zip(lands, axes)]
    jobs = _gather_jobs_a(axes, sizes)
    send, recv, bufs, token = _split_start(name + "_a", lands, jobs, 4 * len(lands), after)
    return dict(name=name, axes=axes, sizes=sizes, send=send, recv=recv, bufs=bufs, jobs=jobs), token


def _gather_mid(h, after):
    bufs = _split_wait(h["name"] + "_aw", h["bufs"], h["send"], h["recv"], h["jobs"], after)
    jobs = _gather_jobs_b(h["axes"], h["sizes"])
    send, recv, lands, token = _split_start(h["name"] + "_b", bufs, jobs, 3 * len(bufs), after)
    return dict(h, send=send, recv=recv, bufs=lands, jobs=jobs), token


def _gather_end(h, after):
    return _split_wait(h["name"] + "_bw", h["bufs"], h["send"], h["recv"], h["jobs"], after)


def _exchange_begin(name, grads, axes, lands, layers, me, after):
    sizes = [g.shape[ax] // N_DEV for g, ax in zip(grads, axes)]
    lands = [_own_place(f"{name}_place{t}", grads[t], lands[t], layers[t], axes[t], me)
             for t in range(len(grads))]
    jobs = _exchange_jobs(axes, sizes, layers)
    send, recv, bufs, token = _split_start(name + "_s", list(grads) + lands, jobs, 7 * len(grads), after)
    return dict(name=name, n=len(grads), send=send, recv=recv, bufs=bufs, jobs=jobs), token


def _exchange_end(h, after):
    bufs = _split_wait(h["name"] + "_w", h["bufs"], h["send"], h["recv"], h["jobs"], after)
    return bufs[h["n"]:]


def _all_gather_small(name, vec, reduce):
    r = vec.shape[0]

    def body(v_ref, o_ref, *rest):
        if reduce:
            buf, send_sems, recv_sems = rest
        else:
            buf = o_ref
            send_sems, recv_sems = rest
        x, y, c = _my_place()
        mine = 4 * x + 2 * y + c
        buf[mine] = v_ref[...]
        copies = []
        for k in range(1, N_DEV):
            px, py, pc = _flip(x, k & 4), _flip(y, k & 2), _flip(c, k & 1)
            cp = pltpu.make_async_remote_copy(
                src_ref=v_ref, dst_ref=buf.at[mine], send_sem=send_sems.at[k - 1],
                recv_sem=recv_sems.at[k - 1], device_id=(px, py, pc), device_id_type=MESH)
            cp.start()
            copies.append(cp)
        for cp in copies:
            cp.wait()
        if reduce:
            acc = buf[0]
            for q in range(1, N_DEV):
                acc = acc + buf[q]
            o_ref[...] = acc

    scratch = [pltpu.SemaphoreType.DMA((N_DEV - 1,)), pltpu.SemaphoreType.DMA((N_DEV - 1,))]
    if reduce:
        scratch = [pltpu.VMEM((N_DEV, r, LANE), F32)] + scratch
        out_shape = jax.ShapeDtypeStruct((r, LANE), F32)
    else:
        out_shape = jax.ShapeDtypeStruct((N_DEV, r, LANE), F32)
    return pl.pallas_call(
        body, name=name, in_specs=[VMEM_SPEC], out_specs=VMEM_SPEC, out_shape=out_shape,
        scratch_shapes=scratch, compiler_params=_cp(has_side_effects=True),
    )(vec)


def _pack(arrs, row_mult=8):
    flat = jnp.concatenate([a.reshape(-1).astype(F32) for a in arrs])
    n = flat.shape[0]
    rows = -(-n // LANE)
    rows = -(-rows // row_mult) * row_mult
    return jnp.pad(flat, (0, rows * LANE - n)).reshape(rows, LANE)


def _unpack(vec, shapes):
    flat = vec.reshape(-1)
    out, pos = [], 0
    for sh in shapes:
        n = 1
        for s in sh:
            n *= s
        out.append(flat[pos:pos + n].reshape(sh))
        pos += n
    return out


BIG = ["conv_w_pw1", "conv_w_pw2", "mla_w_in", "mla_w_q_up", "mla_w_kv_up", "mla_w_o", "mlp_w1", "mlp_w2"]
BIG_AXIS = {"conv_w_pw1": 2, "conv_w_pw2": 1, "mla_w_in": 1, "mla_w_q_up": 2, "mla_w_kv_up": 2,
            "mla_w_o": 1, "mlp_w1": 2, "mlp_w2": 1}
SMALL_SHARDED = ["conv_w_dw", "mla_q_norm_g", "mla_kv_norm_g"]
REPLICATED = ["norm_mixer_g", "norm_mlp_g", "conv_b_pw1", "conv_b_dw", "conv_ln_g", "conv_ln_b",
              "conv_b_pw2", "final_norm_g"]
WEIGHTS = ["norm_mixer_g", "norm_mlp_g", "conv_w_pw1", "conv_b_pw1", "conv_w_dw", "conv_b_dw",
           "conv_ln_g", "conv_ln_b", "conv_w_pw2", "conv_b_pw2", "mla_w_in", "mla_q_norm_g",
           "mla_kv_norm_g", "mla_w_q_up", "mla_w_kv_up", "mla_w_o", "mlp_w1", "mlp_w2", "final_norm_g"]


def _unshard_last(g, lead):
    nd = g.ndim
    perm = tuple(range(1, nd - 1)) + (0, nd - 1)
    return g.transpose(perm).reshape(lead + (N_DEV * g.shape[-1],))


def _step(w, m, v, x, positions, target):
    s, d = x.shape
    depth = w["norm_mixer_g"].shape[0]
    n_conv, n_mla = w["conv_w_pw1"].shape[0], w["mla_w_in"].shape[0]
    heads = (w["mla_w_q_up"].shape[-1] * N_DEV) // (HEAD_NOPE + HEAD_ROPE)
    rq, rkv = w["mla_w_q_up"].shape[1], w["mla_w_kv_up"].shape[1]
    xi, yi, ci = _my_place()
    mine = 4 * xi + 2 * yi + ci

    def mixer_units(layer):
        names = (["conv_w_pw1", "conv_w_pw2"] if layer % 2 == 0
                 else ["mla_w_in", "mla_w_q_up", "mla_w_kv_up", "mla_w_o"])
        return [(n, layer // 2) for n in names]

    def mlp_units(layer):
        return [("mlp_w1", layer), ("mlp_w2", layer)]

    me_arr = mine.astype(jnp.int32).reshape(1)

    def gather_begin(tag, units, after):
        lands = [_cast_place(f"{tag}_place_{n}", w[n], jl, BIG_AXIS[n] - 1, me_arr) for n, jl in units]
        h, token = _gather_begin(tag, lands, [BIG_AXIS[n] - 1 for n, _ in units], after)
        return dict(h, units=units), token

    full = {}

    def gather_end(h, after):
        full.update(zip(h["units"], _gather_end(h, after)))

    small_shapes = [w[n].shape for n in SMALL_SHARDED]
    gathered = _all_gather_small("gather_small", _pack([w[n] for n in SMALL_SHARDED]), False)

    first_a, tok = gather_begin("gather_0a", mixer_units(0), gathered)
    first_b, tok = gather_begin("gather_0b", mlp_units(0), tok)
    pending = {}
    if depth > 1:
        pending[1], tok = gather_begin("gather_1", mixer_units(1) + mlp_units(1), tok)
    first_a, tok = _gather_mid(first_a, tok)
    gather_end(first_a, tok)

    per_dev = [_unpack(gathered[q], small_shapes) for q in range(N_DEV)]
    w_dw = _unshard_last(jnp.stack([p[0] for p in per_dev]), (n_conv, CONV_W))
    q_gain = _unshard_last(jnp.stack([p[1] for p in per_dev]), (n_mla,))
    kv_gain = _unshard_last(jnp.stack([p[2] for p in per_dev]), (n_mla,))
    w_dw_pad = jnp.pad(w_dw, ((0, 0), (0, HALO - CONV_W), (0, 0)))

    w_in_cols = rq + rkv + HEAD_ROPE

    def pad_w_in(a):
        return jnp.pad(a, ((0, 0), (0, rq + rkv + LANE - w_in_cols)))

    def pad_wq(a):
        return jnp.pad(a.reshape(rq, heads, HEAD_NOPE + HEAD_ROPE),
                       ((0, 0), (0, 0), (0, HEAD_QK_PAD - HEAD_NOPE - HEAD_ROPE))).reshape(rq, heads * HEAD_QK_PAD)

    inv_freq = ROPE_THETA ** (-jnp.arange(0, HEAD_ROPE, 2, dtype=F32) / HEAD_ROPE)
    ang = positions.reshape(s).astype(F32)[:, None] * inv_freq
    cos, sin = jnp.cos(ang), jnp.sin(ang)
    c64 = jnp.concatenate([cos, cos], axis=1)
    s64 = jnp.concatenate([-sin, sin], axis=1)
    zeros64 = jnp.zeros((s, LANE - HEAD_ROPE), F32)
    ck = jnp.concatenate([c64, zeros64], axis=1)
    sk = jnp.concatenate([s64, zeros64], axis=1)
    scale = (HEAD_NOPE + HEAD_ROPE) ** -0.5
    cq = scale * jnp.concatenate([jnp.ones((s, HEAD_NOPE), F32), ck], axis=1)
    sq = scale * jnp.concatenate([jnp.zeros((s, HEAD_NOPE), F32), sk], axis=1)

    def vec(a):
        return a.reshape(1, -1)

    saved = []
    wpad = {}
    for layer in range(depth):
        jl = layer // 2
        h = _rms_fwd(f"rms_mixer_{layer}", x, vec(w["norm_mixer_g"][layer]) + tok[0, 0])
        if layer % 2 == 0:
            ua, ug, glu = _mm_glu(f"conv_pw1_{layer}", h, full["conv_w_pw1", jl], None, vec(w["conv_b_pw1"][jl]))
            cc, sw = _conv_fwd(f"conv_dw_{layer}", glu, w_dw_pad[jl], vec(w["conv_b_dw"][jl]),
                               vec(w["conv_ln_g"][jl]), vec(w["conv_ln_b"][jl]))
            x1 = _mm_res(f"conv_pw2_{layer}", sw, full["conv_w_pw2", jl], None, x, vec(w["conv_b_pw2"][jl]))
            mix = (h, ua, ug, glu, cc, sw)
        else:
            wpad["in", jl] = pad_w_in(full["mla_w_in", jl])
            wpad["q", jl] = pad_wq(full["mla_w_q_up", jl])
            down = _mm_plain(f"mla_down_{layer}", h, wpad["in", jl], None, "nn", F32)
            qn, kvn, kpe = _mla_mid_fwd(f"mla_mid_{layer}", down, vec(q_gain[jl]), vec(kv_gain[jl]), ck, sk)
            qf = _mm_q(f"mla_q_{layer}", qn, wpad["q", jl], None, cq, sq)
            kf, vv, kft, vt = _mm_kv(f"mla_kv_{layer}", kvn, full["mla_w_kv_up", jl], None, kpe)
            o, lse = _flash_fwd(f"mla_attn_{layer}", qf, kf, vt, heads)
            x1 = _mm_res(f"mla_out_{layer}", o, full["mla_w_o", jl], None, x)
            mix = (h, down, qn, kvn, qf, kf, kft, vv, o, lse)
        anchor = x1
        if layer == 0:
            first_b, anchor = _gather_mid(first_b, anchor)
        if layer + 2 < depth:
            pending[layer + 2], anchor = gather_begin(f"gather_{layer + 2}",
                                                      mixer_units(layer + 2) + mlp_units(layer + 2), anchor)
        if layer == 0:
            gather_end(first_b, anchor)
        elif layer + 1 < depth:
            pending[layer + 1], anchor = _gather_mid(pending[layer + 1], anchor)
        if anchor is not x1:
            tok = anchor
        h2 = _rms_fwd(f"rms_mlp_{layer}", x1, vec(w["norm_mlp_g"][layer]) + tok[0, 0])
        z, a = _mm_mlp_up(f"mlp_up_{layer}", h2, full["mlp_w1", layer], None)
        x2 = _mm_res(f"mlp_down_{layer}", a, full["mlp_w2", layer], None, x1)
        if layer + 1 < depth:
            if layer == 0:
                pending[1], tok = _gather_mid(pending[1], x2)
                gather_end(pending[1], tok)
            else:
                gather_end(pending[layer + 1], x2)
        saved.append((x, mix, x1, h2, z, a))
        x = x2

    loss_row, g, gb, d_final, _ = _final_loss("final_loss", x, vec(w["final_norm_g"]) + tok[0, 0], target)

    recv = {n: lax.empty((N_DEV,) + w[n].shape, BF16) for n in BIG}

    def exchange_begin(tag, items, after):
        names = [n for n, _, _ in items]
        h, token = _exchange_begin(tag, [gr for _, _, gr in items], [BIG_AXIS[n] - 1 for n in names],
                                   [recv[n] for n in names], [jl for _, jl, _ in items], me_arr, after)
        return dict(h, names=names), token

    def exchange_end(h, after):
        recv.update(zip(h["names"], _exchange_end(h, after)))

    mix_exchange = None
    d_mixer, d_mlp = [None] * depth, [None] * depth
    d_small = {n: [None] * n_conv for n in ["conv_b_pw1", "conv_w_dw", "conv_b_dw", "conv_ln_g",
                                           "conv_ln_b", "conv_b_pw2"]}
    d_qg, d_kvg = [None] * n_mla, [None] * n_mla
    for layer in reversed(range(depth)):
        jl = layer // 2
        x0, mix, x1, h2, z, a = saved[layer]
        colsum_g = None
        dz = _mm_mlp_dz(f"mlp_dz_{layer}", gb, full["mlp_w2", layer], None, z)
        dw2 = _mm_wgrad(f"mlp_dw2_{layer}", a, gb)
        dh2 = _mm_plain(f"mlp_dh_{layer}", dz, full["mlp_w1", layer], None, "nt", F32)
        dw1 = _mm_wgrad(f"mlp_dw1_{layer}", h2, dz)
        mlp_exchange, tok = exchange_begin(f"exchange_mlp_{layer}",
                                           [("mlp_w1", layer, dw1), ("mlp_w2", layer, dw2)], dh2)
        g, gb, d_mlp[layer], colsum_g = _rms_bwd(f"rms_mlp_bwd_{layer}", x1,
                                                 vec(w["norm_mlp_g"][layer]) + tok[0, 0], dh2, g)
        if mix_exchange is not None:
            exchange_end(mix_exchange, g)
        if layer % 2 == 0:
            h, ua, ug, glu, cc, sw = mix
            d_small["conv_b_pw2"][jl] = colsum_g.reshape(-1)
            dsw = _mm_plain(f"conv_ds_{layer}", gb, full["conv_w_pw2", jl], None, "nt", F32)
            dwp2 = _mm_wgrad(f"conv_dw2_{layer}", sw, gb)
            dc, dlg, dlb, dbdw = _conv_bwd_ln(f"conv_ln_bwd_{layer}", dsw, cc, vec(w["conv_ln_g"][jl]),
                                              vec(w["conv_ln_b"][jl]))
            du, dwdw, dbu = _conv_bwd_dw(f"conv_dw_bwd_{layer}", dc, glu, ua, ug, w_dw_pad[jl])
            d_small["conv_ln_g"][jl] = dlg.reshape(-1)
            d_small["conv_ln_b"][jl] = dlb.reshape(-1)
            d_small["conv_b_dw"][jl] = dbdw.reshape(-1)
            d_small["conv_w_dw"][jl] = dwdw[:CONV_W]
            d_small["conv_b_pw1"][jl] = dbu.reshape(-1)
            dh = _mm_plain(f"conv_dh_{layer}", du, full["conv_w_pw1", jl], None, "nt", F32)
            dwp1 = _mm_wgrad(f"conv_dw1_{layer}", h, du)
            items = [("conv_w_pw1", jl, dwp1), ("conv_w_pw2", jl, dwp2)]
        else:
            h, down, qn, kvn, qf, kf, kft, vv, o, lse = mix
            do = _mm_plain(f"mla_do_{layer}", gb, full["mla_w_o", jl], None, "nt", BF16)
            dwo = _mm_wgrad(f"mla_dwo_{layer}", o, gb)
            dq, delta = _flash_bwd_dq(f"mla_attn_dq_{layer}", qf, kf, kft, vv, do, o, lse, cq, sq, heads)
            dkv, dkpe = _flash_bwd_dkv(f"mla_attn_dkv_{layer}", qf, kf, vv, do, lse, delta, ck, sk, heads)
            dqn = _mm_plain(f"mla_dqn_{layer}", dq, wpad["q", jl], None, "nt", F32)
            dwq = _mm_wgrad(f"mla_dwq_{layer}", qn, dq).reshape(rq, heads, HEAD_QK_PAD)[
                :, :, :HEAD_NOPE + HEAD_ROPE].reshape(rq, heads * (HEAD_NOPE + HEAD_ROPE))
            dkvn = _mm_plain(f"mla_dkvn_{layer}", dkv, full["mla_w_kv_up", jl], None, "nt", F32)
            dwkv = _mm_wgrad(f"mla_dwkv_{layer}", kvn, dkv)
            ddown, d_qg[jl], d_kvg[jl] = _mla_mid_bwd(f"mla_mid_bwd_{layer}", down, vec(q_gain[jl]),
                                                      vec(kv_gain[jl]), dqn, dkvn, dkpe)
            dh = _mm_plain(f"mla_dh_{layer}", ddown, wpad["in", jl], None, "nt", F32)
            dwin = _mm_wgrad(f"mla_dwin_{layer}", h, ddown)[:, :w_in_cols]
            items = [("mla_w_in", jl, dwin), ("mla_w_q_up", jl, dwq), ("mla_w_kv_up", jl, dwkv),
                     ("mla_w_o", jl, dwo)]
        mix_exchange, tok = exchange_begin(f"exchange_mix_{layer}", items, dh)
        g, gb, d_mixer[layer], _ = _rms_bwd(f"rms_mixer_bwd_{layer}", x0,
                                            vec(w["norm_mixer_g"][layer]) + tok[0, 0], dh, g)
        exchange_end(mlp_exchange, g)
    grad_x = g

    out = {}

    def adamw_big(n):
        sh = w[n].shape
        r, c = sh[0] * sh[1], sh[2]
        res = _adamw(f"adamw_{n}", recv[n].reshape(N_DEV, r, c), w[n].reshape(r, c),
                     m[n].reshape(r, c), v[n].reshape(r, c))
        out[n] = [t.reshape(sh) for t in res]

    early = [n for n in BIG if n not in mix_exchange["names"]]
    for n in early:
        adamw_big(n)
    exchange_end(mix_exchange, out[early[-1]][1])
    for n in mix_exchange["names"]:
        adamw_big(n)

    small_full = {
        "norm_mixer_g": jnp.concatenate(d_mixer, axis=0), "norm_mlp_g": jnp.concatenate(d_mlp, axis=0),
        "conv_b_pw1": jnp.stack(d_small["conv_b_pw1"]), "conv_b_dw": jnp.stack(d_small["conv_b_dw"]),
        "conv_ln_g": jnp.stack(d_small["conv_ln_g"]), "conv_ln_b": jnp.stack(d_small["conv_ln_b"]),
        "conv_b_pw2": jnp.stack(d_small["conv_b_pw2"]), "final_norm_g": d_final.reshape(-1),
        "conv_w_dw": jnp.stack(d_small["conv_w_dw"]),
        "mla_q_norm_g": jnp.concatenate(d_qg, axis=0), "mla_kv_norm_g": jnp.concatenate(d_kvg, axis=0),
    }
    names = REPLICATED + SMALL_SHARDED
    summed = _unpack(_all_gather_small("reduce_small", _pack([small_full[n] for n in names]), True),
                     [small_full[n].shape for n in names])
    summed = dict(zip(names, summed))
    for n in SMALL_SHARDED:
        width = w[n].shape[-1]
        summed[n] = lax.dynamic_slice_in_dim(summed[n], mine * width, width, axis=summed[n].ndim - 1)
    for group, tag in ((REPLICATED, "replicated"), (SMALL_SHARDED, "small_sharded")):
        shapes = [w[n].shape for n in group]
        res = _adamw(f"adamw_{tag}", _pack([summed[n] for n in group])[None],
                     _pack([w[n] for n in group]), _pack([m[n] for n in group]), _pack([v[n] for n in group]))
        unpacked = [_unpack(t, shapes) for t in res]
        for q, n in enumerate(group):
            out[n] = [unpacked[0][q], unpacked[1][q], unpacked[2][q], unpacked[3][q]]

    loss = lax.psum(loss_row[0, 0], ("x", "y", "c"))
    return loss, grad_x, out


def kernel(x, positions, norm_mixer_g, norm_mlp_g, conv_w_pw1, conv_b_pw1, conv_w_dw, conv_b_dw, conv_ln_g, conv_ln_b, conv_w_pw2, conv_b_pw2, mla_w_in, mla_q_norm_g, mla_kv_norm_g, mla_w_q_up, mla_w_kv_up, mla_w_o, mlp_w1, mlp_w2, final_norm_g, loss_target, m_norm_mixer_g, m_norm_mlp_g, m_conv_w_pw1, m_conv_b_pw1, m_conv_w_dw, m_conv_b_dw, m_conv_ln_g, m_conv_ln_b, m_conv_w_pw2, m_conv_b_pw2, m_mla_w_in, m_mla_q_norm_g, m_mla_kv_norm_g, m_mla_w_q_up, m_mla_w_kv_up, m_mla_w_o, m_mlp_w1, m_mlp_w2, m_final_norm_g, v_norm_mixer_g, v_norm_mlp_g, v_conv_w_pw1, v_conv_b_pw1, v_conv_w_dw, v_conv_b_dw, v_conv_ln_g, v_conv_ln_b, v_conv_w_pw2, v_conv_b_pw2, v_mla_w_in, v_mla_q_norm_g, v_mla_kv_norm_g, v_mla_w_q_up, v_mla_w_kv_up, v_mla_w_o, v_mlp_w1, v_mlp_w2, v_final_norm_g):
    ws = (norm_mixer_g, norm_mlp_g, conv_w_pw1, conv_b_pw1, conv_w_dw, conv_b_dw, conv_ln_g, conv_ln_b,
          conv_w_pw2, conv_b_pw2, mla_w_in, mla_q_norm_g, mla_kv_norm_g, mla_w_q_up, mla_w_kv_up, mla_w_o,
          mlp_w1, mlp_w2, final_norm_g)
    ms = (m_norm_mixer_g, m_norm_mlp_g, m_conv_w_pw1, m_conv_b_pw1, m_conv_w_dw, m_conv_b_dw, m_conv_ln_g,
          m_conv_ln_b, m_conv_w_pw2, m_conv_b_pw2, m_mla_w_in, m_mla_q_norm_g, m_mla_kv_norm_g,
          m_mla_w_q_up, m_mla_w_kv_up, m_mla_w_o, m_mlp_w1, m_mlp_w2, m_final_norm_g)
    vs = (v_norm_mixer_g, v_norm_mlp_g, v_conv_w_pw1, v_conv_b_pw1, v_conv_w_dw, v_conv_b_dw, v_conv_ln_g,
          v_conv_ln_b, v_conv_w_pw2, v_conv_b_pw2, v_mla_w_in, v_mla_q_norm_g, v_mla_kv_norm_g,
          v_mla_w_q_up, v_mla_w_kv_up, v_mla_w_o, v_mlp_w1, v_mlp_w2, v_final_norm_g)
    w, m, v = dict(zip(WEIGHTS, ws)), dict(zip(WEIGHTS, ms)), dict(zip(WEIGHTS, vs))
    s, d = x.shape[-2], x.shape[-1]
    loss, grad_x, out = _step(w, m, v, x.reshape(s, d), positions, loss_target.reshape(s, d))
    grads = [out[n][0] for n in WEIGHTS]
    deltas = [out[n][1] for n in WEIGHTS]
    new_m = [out[n][2] for n in WEIGHTS]
    new_v = [out[n][3] for n in WEIGHTS]
    return (loss, grad_x.reshape(x.shape), *grads, *deltas, *new_m, *new_v)
```

```python
import functools

import jax
import jax.numpy as jnp
from jax import lax
from jax.experimental import pallas as pl
from jax.experimental.pallas import tpu as pltpu

F32 = jnp.float32
BF16 = jnp.bfloat16

NORM_EPS = 1e-6
LN_EPS = 1e-5
ROPE_THETA = 10000.0
CHUNK_BITS = 6
HEAD_NOPE = 128
HEAD_ROPE = 64
HEAD_V = 128
HEAD_QK_PAD = 256
CONV_W = 31
HALO = 32
N_DEV = 8

ADAM_LR = 0.001
ADAM_B1 = 0.9
ADAM_B2 = 0.999
ADAM_EPS = 1e-08
ADAM_WD = 0.01
ADAM_STEP = 10

V7X_VMEM_BYTES = 64 * 1024 * 1024
VMEM_LIMIT = (V7X_VMEM_BYTES * 3) // 4
LANE = 128

MESH = pl.DeviceIdType.MESH
ANY = pl.BlockSpec(memory_space=pl.ANY)
VMEM_SPEC = pl.BlockSpec(memory_space=pltpu.VMEM)


def _cp(**kw):
    return pltpu.CompilerParams(vmem_limit_bytes=VMEM_LIMIT, **kw)


SUBLANE = 8
SUBLANE_BF16 = 16

TM_PREF = 1024
TN_PREF = 1024
TK_PREF = 2048


def _tile(n, pref, mult=SUBLANE_BF16):
    if n <= pref + pref // 2:
        return n
    t = (pref // mult) * mult
    while t >= mult:
        if n % t == 0:
            return t
        t -= mult
    return n


def _sigmoid(x):
    return 1.0 / (1.0 + jnp.exp(-x))


def _rot_half(x):
    n = x.shape[-1]
    lane = lax.broadcasted_iota(jnp.int32, x.shape, x.ndim - 1)
    first = (lane & 63) < 32
    return jnp.where(first, pltpu.roll(x, n - 32, x.ndim - 1), pltpu.roll(x, 32, x.ndim - 1))


def _rope(x, c, s):
    return x * c + _rot_half(x) * s


def _rope_t(d, c, s):
    return d * c + _rot_half(d * s)


def _chunk_mask_t(t):
    row = lax.broadcasted_iota(jnp.int32, (t, t), 0)
    col = lax.broadcasted_iota(jnp.int32, (t, t), 1)
    return jnp.right_shift(row, CHUNK_BITS) <= jnp.right_shift(col, CHUNK_BITS)


def _rms_fwd(name, x, g):
    t, d = x.shape
    tm = _tile(t, 512)

    def body(x_ref, g_ref, o_ref):
        xf = x_ref[...]
        r = lax.rsqrt(jnp.mean(xf * xf, axis=-1, keepdims=True) + NORM_EPS)
        o_ref[...] = (xf * r * g_ref[...]).astype(o_ref.dtype)

    return pl.pallas_call(
        body, name=name, grid=(t // tm,),
        in_specs=[pl.BlockSpec((tm, d), lambda i: (i, 0)), pl.BlockSpec((1, d), lambda i: (0, 0))],
        out_specs=pl.BlockSpec((tm, d), lambda i: (i, 0)),
        out_shape=jax.ShapeDtypeStruct((t, d), BF16),
        compiler_params=_cp(),
    )(x, g)


def _rms_bwd_math(xf, g, dy):
    r = lax.rsqrt(jnp.mean(xf * xf, axis=-1, keepdims=True) + NORM_EPS)
    xh = xf * r
    dg = jnp.sum(dy * xh, axis=0, keepdims=True)
    dxh = dy * g
    dx = r * (dxh - xh * jnp.mean(dxh * xh, axis=-1, keepdims=True))
    return dx, dg


def _rms_bwd(name, x, g, dy, resid):
    t, d = x.shape
    tm = _tile(t, 256)

    def body(x_ref, g_ref, dy_ref, r_ref, dx_ref, dxb_ref, dg_ref, cs_ref):
        @pl.when(pl.program_id(0) == 0)
        def _():
            dg_ref[...] = jnp.zeros_like(dg_ref)
            cs_ref[...] = jnp.zeros_like(cs_ref)

        dx, dg = _rms_bwd_math(x_ref[...], g_ref[...], dy_ref[...])
        tot = r_ref[...] + dx
        dx_ref[...] = tot
        dxb_ref[...] = tot.astype(BF16)
        dg_ref[...] += dg
        cs_ref[...] += jnp.sum(tot, axis=0, keepdims=True)

    row = pl.BlockSpec((tm, d), lambda i: (i, 0))
    vec = pl.BlockSpec((1, d), lambda i: (0, 0))
    return pl.pallas_call(
        body, name=name, grid=(t // tm,),
        in_specs=[row, vec, row, row],
        out_specs=[row, row, vec, vec],
        out_shape=[jax.ShapeDtypeStruct((t, d), F32), jax.ShapeDtypeStruct((t, d), BF16),
                   jax.ShapeDtypeStruct((1, d), F32), jax.ShapeDtypeStruct((1, d), F32)],
        compiler_params=_cp(dimension_semantics=("arbitrary",)),
    )(x, g, dy, resid)


def _final_loss(name, x, g, target):
    t, d = x.shape
    tm = _tile(t, 256)

    def body(x_ref, g_ref, t_ref, loss_ref, dx_ref, dxb_ref, dg_ref, cs_ref):
        @pl.when(pl.program_id(0) == 0)
        def _():
            loss_ref[...] = jnp.zeros_like(loss_ref)
            dg_ref[...] = jnp.zeros_like(dg_ref)
            cs_ref[...] = jnp.zeros_like(cs_ref)

        xf = x_ref[...]
        gg = g_ref[...]
        r = lax.rsqrt(jnp.mean(xf * xf, axis=-1, keepdims=True) + NORM_EPS)
        err = xf * r * gg - t_ref[...]
        part = 0.5 * jnp.sum(jnp.mean(err * err, axis=-1, keepdims=True), axis=0, keepdims=True)
        loss_ref[...] += jnp.broadcast_to(part, loss_ref.shape)
        dx, dg = _rms_bwd_math(xf, gg, err * (1.0 / d))
        dx_ref[...] = dx
        dxb_ref[...] = dx.astype(BF16)
        dg_ref[...] += dg
        cs_ref[...] += jnp.sum(dx, axis=0, keepdims=True)

    row = pl.BlockSpec((tm, d), lambda i: (i, 0))
    vec = pl.BlockSpec((1, d), lambda i: (0, 0))
    return pl.pallas_call(
        body, name=name, grid=(t // tm,),
        in_specs=[row, vec, row],
        out_specs=[pl.BlockSpec((1, LANE), lambda i: (0, 0)), row, row, vec, vec],
        out_shape=[jax.ShapeDtypeStruct((1, LANE), F32), jax.ShapeDtypeStruct((t, d), F32),
                   jax.ShapeDtypeStruct((t, d), BF16), jax.ShapeDtypeStruct((1, d), F32),
                   jax.ShapeDtypeStruct((1, d), F32)],
        compiler_params=_cp(dimension_semantics=("arbitrary",)),
    )(x, g, target)


_DIMS = {
    "nn": (((1,), (0,)), ((), ())),
    "nt": (((1,), (1,)), ((), ())),
    "tn": (((0,), (0,)), ((), ())),
}


def _mm(name, a, bs, *, mode, m, n, k, epilogue, out_shape, out_specs, extras=(), extra_specs=(),
        a_lead=None, aliases=None, tn_div=1):
    tm, tn, tk = _tiles(m, n, k, tn_div)
    nk = k // tk
    nb, ne = len(bs), len(extras)
    no = len(out_shape)
    dims = _DIMS[mode]

    def with_lead(shape, idx, lead):
        if lead is None:
            return pl.BlockSpec(shape, idx)
        return pl.BlockSpec((None,) + shape, lambda i, j, kk: (lead,) + idx(i, j, kk))

    if mode == "tn":
        a_spec = with_lead((tk, tm), lambda i, j, kk: (kk, i), a_lead)
    else:
        a_spec = with_lead((tm, tk), lambda i, j, kk: (i, kk), a_lead)
    b_specs = []
    for _, lead, off in bs:
        if mode == "nt":
            b_specs.append(with_lead((tn, tk), lambda i, j, kk, off=off: (j + off, kk), lead))
        else:
            b_specs.append(with_lead((tk, tn), lambda i, j, kk, off=off: (kk, j + off), lead))

    def body(*refs):
        a_ref = refs[0]
        b_refs = refs[1:1 + nb]
        ex = refs[1 + nb:1 + nb + ne]
        outs = refs[1 + nb + ne:1 + nb + ne + no]
        accs = refs[1 + nb + ne + no:]

        def part(b_ref):
            return lax.dot_general(a_ref[...], b_ref[...], dims, preferred_element_type=F32)

        if nk == 1:
            epilogue([part(b_ref) for b_ref in b_refs], ex, outs)
            return
        kk = pl.program_id(2)

        @pl.when(kk == 0)
        def _():
            for acc, b_ref in zip(accs, b_refs):
                acc[...] = part(b_ref)

        @pl.when(kk > 0)
        def _():
            for acc, b_ref in zip(accs, b_refs):
                acc[...] += part(b_ref)

        @pl.when(kk == nk - 1)
        def _():
            epilogue([acc[...] for acc in accs], ex, outs)

    scratch = [pltpu.VMEM((tm, tn), F32) for _ in range(nb)] if nk > 1 else []
    return pl.pallas_call(
        body, name=name, grid=(m // tm, n // tn, nk),
        in_specs=[a_spec] + b_specs + list(extra_specs),
        out_specs=list(out_specs), out_shape=list(out_shape), scratch_shapes=scratch,
        input_output_aliases=aliases or {},
        compiler_params=_cp(dimension_semantics=("arbitrary", "arbitrary", "arbitrary")),
    )(a, *[b for b, _, _ in bs], *extras), (tm, tn, tk)


def _ij(tm, tn):
    return pl.BlockSpec((tm, tn), lambda i, j, kk: (i, j))


def _tiles(m, n, k, tn_div=1):
    return _tile(m, TM_PREF), _tile(n, TN_PREF // tn_div, LANE), _tile(k, TK_PREF, LANE)


def _mm_plain(name, a, b, b_lead, mode, out_dtype):
    m, k = a.shape
    n = b.shape[-1] if mode == "nn" else b.shape[-2]
    tm, tn, _ = _tiles(m, n, k)

    def epilogue(accs, ex, outs):
        outs[0][...] = accs[0].astype(out_dtype)

    return _mm(name, a, [(b, b_lead, 0)], mode=mode, m=m, n=n, k=k, epilogue=epilogue,
               out_shape=[jax.ShapeDtypeStruct((m, n), out_dtype)], out_specs=[_ij(tm, tn)])[0][0]


def _mm_res(name, a, b, b_lead, resid, bias=None):
    m, k = a.shape
    n = b.shape[-1]
    tm, tn, _ = _tiles(m, n, k)
    extras, specs = [resid], [_ij(tm, tn)]
    if bias is not None:
        extras.append(bias)
        specs.append(pl.BlockSpec((1, tn), lambda i, j, kk: (0, j)))

    def epilogue(accs, ex, outs):
        y = ex[0][...] + accs[0]
        if bias is not None:
            y = y + ex[1][...]
        outs[0][...] = y

    return _mm(name, a, [(b, b_lead, 0)], mode="nn", m=m, n=n, k=k, epilogue=epilogue,
               extras=extras, extra_specs=specs,
               out_shape=[jax.ShapeDtypeStruct((m, n), F32)], out_specs=[_ij(tm, tn)])[0][0]


def _mm_mlp_up(name, h, w1, lead):
    m, k = h.shape
    n = w1.shape[-1]
    tm, tn, _ = _tiles(m, n, k)

    def epilogue(accs, ex, outs):
        z = accs[0]
        outs[0][...] = z.astype(BF16)
        r = jnp.maximum(z, 0.0)
        outs[1][...] = (r * r).astype(BF16)

    sh = jax.ShapeDtypeStruct((m, n), BF16)
    return _mm(name, h, [(w1, lead, 0)], mode="nn", m=m, n=n, k=k, epilogue=epilogue,
               out_shape=[sh, sh], out_specs=[_ij(tm, tn), _ij(tm, tn)])[0]


def _mm_mlp_dz(name, g, w2, lead, z):
    m, k = g.shape
    n = w2.shape[-2]
    tm, tn, _ = _tiles(m, n, k)

    def epilogue(accs, ex, outs):
        outs[0][...] = (accs[0] * (2.0 * jnp.maximum(ex[0][...].astype(F32), 0.0))).astype(BF16)

    return _mm(name, g, [(w2, lead, 0)], mode="nt", m=m, n=n, k=k, epilogue=epilogue,
               extras=[z], extra_specs=[_ij(tm, tn)],
               out_shape=[jax.ShapeDtypeStruct((m, n), BF16)], out_specs=[_ij(tm, tn)])[0][0]


def _mm_glu(name, h, w, lead, bias):
    m, k = h.shape
    n = w.shape[-1] // 2
    tm, tn, _ = _tiles(m, n, k, 2)
    off = n // tn

    def epilogue(accs, ex, outs):
        a = accs[0] + ex[0][...]
        gate = accs[1] + ex[1][...]
        outs[0][...] = a.astype(BF16)
        outs[1][...] = gate.astype(BF16)
        outs[2][...] = a * _sigmoid(gate)

    shb = jax.ShapeDtypeStruct((m, n), BF16)
    return _mm(name, h, [(w, lead, 0), (w, lead, off)], mode="nn", m=m, n=n, k=k, epilogue=epilogue,
               extras=[bias, bias],
               extra_specs=[pl.BlockSpec((1, tn), lambda i, j, kk: (0, j)),
                            pl.BlockSpec((1, tn), lambda i, j, kk: (0, j + off))],
               out_shape=[shb, shb, jax.ShapeDtypeStruct((m, n), F32)],
               out_specs=[_ij(tm, tn)] * 3, tn_div=2)[0]


def _mm_q(name, qn, wq_pad, lead, cq, sq):
    m, k = qn.shape
    n = wq_pad.shape[-1]
    tm, tn, _ = _tiles(m, n, k)
    rep = tn // HEAD_QK_PAD

    def epilogue(accs, ex, outs):
        c = jnp.tile(ex[0][...], (1, rep))
        s = jnp.tile(ex[1][...], (1, rep))
        outs[0][...] = _rope(accs[0], c, s).astype(BF16)

    tab = pl.BlockSpec((tm, HEAD_QK_PAD), lambda i, j, kk: (i, 0))
    return _mm(name, qn, [(wq_pad, lead, 0)], mode="nn", m=m, n=n, k=k, epilogue=epilogue,
               extras=[cq, sq], extra_specs=[tab, tab],
               out_shape=[jax.ShapeDtypeStruct((m, n), BF16)], out_specs=[_ij(tm, tn)])[0][0]


def _mm_kv(name, kvn, wkv, lead, kpe):
    m, k = kvn.shape
    n = wkv.shape[-1]
    tm, tn, _ = _tiles(m, n, k)
    heads = tn // (HEAD_NOPE + HEAD_V)

    def epilogue(accs, ex, outs):
        acc = accs[0]
        pe = ex[0][...].astype(F32)
        kparts, vparts = [], []
        for hh in range(heads):
            base = hh * (HEAD_NOPE + HEAD_V)
            kparts += [acc[:, base:base + HEAD_NOPE], pe]
            vparts.append(acc[:, base + HEAD_NOPE:base + HEAD_NOPE + HEAD_V])
        kf = jnp.concatenate(kparts, axis=1)
        vv = jnp.concatenate(vparts, axis=1) if heads > 1 else vparts[0]
        outs[0][...] = kf.astype(BF16)
        outs[1][...] = vv.astype(BF16)
        outs[2][...] = kf.T.astype(BF16)
        outs[3][...] = vv.T.astype(BF16)

    def ji(tn_, tm_):
        return pl.BlockSpec((tn_, tm_), lambda i, j, kk: (j, i))

    return _mm(name, kvn, [(wkv, lead, 0)], mode="nn", m=m, n=n, k=k, epilogue=epilogue,
               extras=[kpe], extra_specs=[pl.BlockSpec((tm, LANE), lambda i, j, kk: (i, 0))],
               out_shape=[jax.ShapeDtypeStruct((m, n), BF16), jax.ShapeDtypeStruct((m, n // 2), BF16),
                          jax.ShapeDtypeStruct((n, m), BF16), jax.ShapeDtypeStruct((n // 2, m), BF16)],
               out_specs=[_ij(tm, tn), _ij(tm, tn // 2), ji(tn, tm), ji(tn // 2, tm)])[0]


def _mm_wgrad(name, a, b):
    t, m = a.shape
    n = b.shape[-1]
    tm, tn, _ = _tiles(m, n, t)

    def epilogue(accs, ex, outs):
        outs[0][...] = accs[0].astype(BF16)

    return _mm(name, a, [(b, None, 0)], mode="tn", m=m, n=n, k=t, epilogue=epilogue,
               out_shape=[jax.ShapeDtypeStruct((m, n), BF16)], out_specs=[_ij(tm, tn)])[0][0]


CONV_ROWS = 256
CONV_RT = 64
CONV_CW = 256
CONV_LR = 32


def _ln_stats(c):
    mu = jnp.mean(c, axis=-1, keepdims=True)
    xc = c - mu
    rstd = lax.rsqrt(jnp.mean(xc * xc, axis=-1, keepdims=True) + LN_EPS)
    return xc * rstd, rstd


def _conv_fwd(name, glu, w_dw, b_dw, ln_g, ln_b):
    t, d = glu.shape
    tt = _tile(t, CONV_ROWS)
    rt, cw, lr = min(CONV_RT, tt), min(CONV_CW, d), min(CONV_LR, tt)
    hb = tt // HALO

    def body(gc_ref, gp_ref, w_ref, b_ref, lg_ref, lb_ref, c_ref, s_ref, buf, win):
        i = pl.program_id(0)
        buf[0:HALO, :] = jnp.where(i > 0, gp_ref[...], 0.0)
        buf[HALO:HALO + tt, :] = gc_ref[...]

        def chunk(cb, carry):
            col = pl.ds(pl.multiple_of(cb * cw, cw), cw)
            for r0 in range(0, tt, rt):
                acc = jnp.broadcast_to(b_ref[:, col], (rt, cw))
                for b in range(SUBLANE):
                    amax = (CONV_W - 1 - b) // SUBLANE
                    lo = r0 + HALO - (CONV_W - 1) + b
                    rows = rt + SUBLANE * amax
                    win[0:rows, :] = buf[lo:lo + rows, col]
                    for a in range(amax + 1):
                        k = SUBLANE * a + b
                        acc = acc + w_ref[k:k + 1, col] * win[SUBLANE * a:SUBLANE * a + rt, :]
                c_ref[r0:r0 + rt, col] = acc
            return carry

        lax.fori_loop(0, d // cw, chunk, 0)

        def ln(r, carry):
            rows = pl.ds(pl.multiple_of(r * lr, lr), lr)
            xh, _ = _ln_stats(c_ref[rows, :])
            y = xh * lg_ref[...] + lb_ref[...]
            s_ref[rows, :] = (y * _sigmoid(y)).astype(BF16)
            return carry

        lax.fori_loop(0, tt // lr, ln, 0)

    row = pl.BlockSpec((tt, d), lambda i: (i, 0))
    vec = pl.BlockSpec((1, d), lambda i: (0, 0))
    return pl.pallas_call(
        body, name=name, grid=(t // tt,),
        in_specs=[row, pl.BlockSpec((HALO, d), lambda i: (jnp.maximum(i * hb - 1, 0), 0)),
                  pl.BlockSpec((HALO, d), lambda i: (0, 0)), vec, vec, vec],
        out_specs=[row, row],
        out_shape=[jax.ShapeDtypeStruct((t, d), F32), jax.ShapeDtypeStruct((t, d), BF16)],
        scratch_shapes=[pltpu.VMEM((HALO + tt, d), F32), pltpu.VMEM((rt + HALO, cw), F32)],
        compiler_params=_cp(dimension_semantics=("arbitrary",)),
    )(glu, glu, w_dw, b_dw, ln_g, ln_b)


def _conv_bwd_ln(name, ds, c, ln_g, ln_b):
    t, d = c.shape
    tt = _tile(t, CONV_ROWS)
    lr = min(CONV_LR, tt)

    def body(ds_ref, c_ref, lg_ref, lb_ref, dc_ref, dg_ref, db_ref, dbdw_ref):
        @pl.when(pl.program_id(0) == 0)
        def _():
            dg_ref[...] = jnp.zeros_like(dg_ref)
            db_ref[...] = jnp.zeros_like(db_ref)
            dbdw_ref[...] = jnp.zeros_like(dbdw_ref)

        def chunk(r, carry):
            rows = pl.ds(pl.multiple_of(r * lr, lr), lr)
            xh, rstd = _ln_stats(c_ref[rows, :])
            g = lg_ref[...]
            y = xh * g + lb_ref[...]
            sg = _sigmoid(y)
            dy = ds_ref[rows, :] * (sg * (1.0 + y * (1.0 - sg)))
            dxh = dy * g
            dc = rstd * (dxh - jnp.mean(dxh, axis=-1, keepdims=True)
                         - xh * jnp.mean(dxh * xh, axis=-1, keepdims=True))
            dc_ref[rows, :] = dc
            dg_ref[...] += jnp.sum(dy * xh, axis=0, keepdims=True)
            db_ref[...] += jnp.sum(dy, axis=0, keepdims=True)
            dbdw_ref[...] += jnp.sum(dc, axis=0, keepdims=True)
            return carry

        lax.fori_loop(0, tt // lr, chunk, 0)

    row = pl.BlockSpec((tt, d), lambda i: (i, 0))
    vec = pl.BlockSpec((1, d), lambda i: (0, 0))
    vsh = jax.ShapeDtypeStruct((1, d), F32)
    return pl.pallas_call(
        body, name=name, grid=(t // tt,),
        in_specs=[row, row, vec, vec], out_specs=[row, vec, vec, vec],
        out_shape=[jax.ShapeDtypeStruct((t, d), F32), vsh, vsh, vsh],
        compiler_params=_cp(dimension_semantics=("arbitrary",)),
    )(ds, c, ln_g, ln_b)


def _conv_bwd_dw(name, dc, glu, ua, ug, w_dw):
    t, d = dc.shape
    tt = _tile(t, CONV_ROWS)
    rt, cw = min(CONV_RT, tt), min(CONV_CW, d)
    hb = tt // HALO
    nt = t // tt

    def body(dcc_ref, dcn_ref, gc_ref, gp_ref, ua_ref, ug_ref, w_ref,
             du_ref, dw_ref, dbu_ref, dbuf, gbuf, wacc, dwin, gwin):
        i = pl.program_id(0)

        @pl.when(i == 0)
        def _():
            wacc[...] = jnp.zeros_like(wacc)
            dbu_ref[...] = jnp.zeros_like(dbu_ref)

        dbuf[0:tt, :] = dcc_ref[...]
        dbuf[tt:tt + HALO, :] = jnp.where(i < nt - 1, dcn_ref[...], 0.0)
        gbuf[0:HALO, :] = jnp.where(i > 0, gp_ref[...], 0.0)
        gbuf[HALO:HALO + tt, :] = gc_ref[...]

        def chunk(cb, carry):
            c0 = pl.multiple_of(cb * cw, cw)
            col = pl.ds(c0, cw)
            colg = pl.ds(pl.multiple_of(d + cb * cw, cw), cw)
            for r0 in range(0, tt, rt):
                dcr = dbuf[r0:r0 + rt, col]
                dgl = jnp.zeros((rt, cw), F32)
                for b in range(SUBLANE):
                    amax = (CONV_W - 1 - b) // SUBLANE
                    hi = r0 + (CONV_W - 1) - b - SUBLANE * amax
                    rows = rt + SUBLANE * amax
                    dwin[0:rows, :] = dbuf[hi:hi + rows, col]
                    lo = r0 + HALO - (CONV_W - 1) + b
                    gwin[0:rows, :] = gbuf[lo:lo + rows, col]
                    for a in range(amax + 1):
                        k = SUBLANE * a + b
                        back = SUBLANE * (amax - a)
                        dgl = dgl + w_ref[k:k + 1, col] * dwin[back:back + rt, :]
                        prod = dcr * gwin[SUBLANE * a:SUBLANE * a + rt, :]
                        part = prod[0:8, :]
                        for r in range(8, rt, 8):
                            part = part + prod[r:r + 8, :]
                        wacc[8 * k:8 * k + 8, col] += part
                a = ua_ref[r0:r0 + rt, col].astype(F32)
                sg = _sigmoid(ug_ref[r0:r0 + rt, col].astype(F32))
                da = dgl * sg
                dgate = dgl * a * sg * (1.0 - sg)
                du_ref[r0:r0 + rt, col] = da.astype(BF16)
                du_ref[r0:r0 + rt, colg] = dgate.astype(BF16)
                dbu_ref[:, col] += jnp.sum(da, axis=0, keepdims=True)
                dbu_ref[:, colg] += jnp.sum(dgate, axis=0, keepdims=True)
            return carry

        lax.fori_loop(0, d // cw, chunk, 0)

        @pl.when(i == nt - 1)
        def _():
            for k in range(CONV_W):
                dw_ref[k:k + 1, :] = jnp.sum(wacc[8 * k:8 * k + 8, :], axis=0, keepdims=True)
            dw_ref[CONV_W:HALO, :] = jnp.zeros((HALO - CONV_W, d), F32)

    row = pl.BlockSpec((tt, d), lambda i: (i, 0))
    return pl.pallas_call(
        body, name=name, grid=(nt,),
        in_specs=[row, pl.BlockSpec((HALO, d), lambda i: (jnp.minimum((i + 1) * hb, t // HALO - 1), 0)),
                  row, pl.BlockSpec((HALO, d), lambda i: (jnp.maximum(i * hb - 1, 0), 0)),
                  row, row, pl.BlockSpec((HALO, d), lambda i: (0, 0))],
        out_specs=[pl.BlockSpec((tt, 2 * d), lambda i: (i, 0)),
                   pl.BlockSpec((HALO, d), lambda i: (0, 0)),
                   pl.BlockSpec((1, 2 * d), lambda i: (0, 0))],
        out_shape=[jax.ShapeDtypeStruct((t, 2 * d), BF16), jax.ShapeDtypeStruct((HALO, d), F32),
                   jax.ShapeDtypeStruct((1, 2 * d), F32)],
        scratch_shapes=[pltpu.VMEM((tt + HALO, d), F32), pltpu.VMEM((HALO + tt, d), F32),
                        pltpu.VMEM((8 * HALO, d), F32),
                        pltpu.VMEM((rt + HALO, cw), F32), pltpu.VMEM((rt + HALO, cw), F32)],
        compiler_params=_cp(dimension_semantics=("arbitrary",)),
    )(dc, dc, glu, glu, ua, ug, w_dw)


def _mla_mid_fwd(name, down, qg, kvg, ck, sk):
    t, w = down.shape
    rq, rkv = qg.shape[-1], kvg.shape[-1]
    tm = _tile(t, 512)

    def body(dn_ref, qg_ref, kvg_ref, ck_ref, sk_ref, qn_ref, kvn_ref, kpe_ref):
        cq = dn_ref[:, 0:rq]
        ckv = dn_ref[:, rq:rq + rkv]
        pe = dn_ref[:, rq + rkv:rq + rkv + LANE]
        qn_ref[...] = (cq * lax.rsqrt(jnp.mean(cq * cq, axis=-1, keepdims=True) + NORM_EPS)
                       * qg_ref[...]).astype(BF16)
        kvn_ref[...] = (ckv * lax.rsqrt(jnp.mean(ckv * ckv, axis=-1, keepdims=True) + NORM_EPS)
                        * kvg_ref[...]).astype(BF16)
        kpe_ref[...] = _rope(pe, ck_ref[...], sk_ref[...]).astype(BF16)

    def row(n):
        return pl.BlockSpec((tm, n), lambda i: (i, 0))

    def vec(n):
        return pl.BlockSpec((1, n), lambda i: (0, 0))

    return pl.pallas_call(
        body, name=name, grid=(t // tm,),
        in_specs=[row(w), vec(rq), vec(rkv), row(LANE), row(LANE)],
        out_specs=[row(rq), row(rkv), row(LANE)],
        out_shape=[jax.ShapeDtypeStruct((t, rq), BF16), jax.ShapeDtypeStruct((t, rkv), BF16),
                   jax.ShapeDtypeStruct((t, LANE), BF16)],
        compiler_params=_cp(),
    )(down, qg, kvg, ck, sk)


def _mla_mid_bwd(name, down, qg, kvg, dqn, dkvn, dkpe):
    t, w = down.shape
    rq, rkv = qg.shape[-1], kvg.shape[-1]
    tm = _tile(t, 256)

    def body(dn_ref, qg_ref, kvg_ref, dqn_ref, dkvn_ref, dkpe_ref, dd_ref, dqg_ref, dkvg_ref):
        @pl.when(pl.program_id(0) == 0)
        def _():
            dqg_ref[...] = jnp.zeros_like(dqg_ref)
            dkvg_ref[...] = jnp.zeros_like(dkvg_ref)

        dcq, dqg = _rms_bwd_math(dn_ref[:, 0:rq], qg_ref[...], dqn_ref[...])
        dckv, dkvg = _rms_bwd_math(dn_ref[:, rq:rq + rkv], kvg_ref[...], dkvn_ref[...])
        dd_ref[:, 0:rq] = dcq.astype(BF16)
        dd_ref[:, rq:rq + rkv] = dckv.astype(BF16)
        dd_ref[:, rq + rkv:rq + rkv + LANE] = dkpe_ref[...].astype(BF16)
        dqg_ref[...] += dqg
        dkvg_ref[...] += dkvg

    def row(n):
        return pl.BlockSpec((tm, n), lambda i: (i, 0))

    def vec(n):
        return pl.BlockSpec((1, n), lambda i: (0, 0))

    return pl.pallas_call(
        body, name=name, grid=(t // tm,),
        in_specs=[row(w), vec(rq), vec(rkv), row(rq), row(rkv), row(LANE)],
        out_specs=[row(w), vec(rq), vec(rkv)],
        out_shape=[jax.ShapeDtypeStruct((t, w), BF16), jax.ShapeDtypeStruct((1, rq), F32),
                   jax.ShapeDtypeStruct((1, rkv), F32)],
        compiler_params=_cp(dimension_semantics=("arbitrary",)),
    )(down, qg, kvg, dqn, dkvn, dkpe)


ATT_TILE = 512
ATT_HEADS = 2
ATT_HEADS_FWD = 4
_NT = (((1,), (1,)), ((), ()))
_TN = (((0,), (0,)), ((), ()))


def _flash_fwd(name, qf, kf, vt, heads):
    s = qf.shape[0]
    t = _tile(s, ATT_TILE)
    n = s // t
    g = min(ATT_HEADS_FWD, heads)
    qw, vw = HEAD_QK_PAD, HEAD_V

    def body(q_ref, k_ref, vt_ref, o_ref, lse_ref, m_sc, l_sc, acc_sc):
        i, j = pl.program_id(1), pl.program_id(2)

        @pl.when(j == 0)
        def _():
            m_sc[...] = jnp.full(m_sc.shape, -jnp.inf, F32)
            l_sc[...] = jnp.zeros_like(l_sc)
            acc_sc[...] = jnp.zeros_like(acc_sc)

        def step(diag):
            for hh in range(g):
                sc = lax.dot_general(k_ref[:, hh * qw:(hh + 1) * qw], q_ref[:, hh * qw:(hh + 1) * qw], _NT,
                                     preferred_element_type=F32)
                if diag:
                    sc = jnp.where(_chunk_mask_t(t), sc, -jnp.inf)
                m_old = m_sc[hh]
                m_new = jnp.maximum(m_old, jnp.max(sc, axis=0, keepdims=True))
                alpha = jnp.exp(m_old - m_new)
                p = jnp.exp(sc - m_new)
                l_sc[hh] = alpha * l_sc[hh] + jnp.sum(p, axis=0, keepdims=True)
                acc_sc[hh] = alpha * acc_sc[hh] + jnp.dot(vt_ref[hh * vw:(hh + 1) * vw, :], p.astype(BF16),
                                                          preferred_element_type=F32)
                m_sc[hh] = m_new

        @pl.when(j < i)
        def _():
            step(False)

        @pl.when(j == i)
        def _():
            step(True)
            for hh in range(g):
                l = l_sc[hh]
                o_ref[:, hh * vw:(hh + 1) * vw] = (acc_sc[hh] / l).T.astype(BF16)
                lse_ref[hh] = m_sc[hh] + jnp.log(l)

    return pl.pallas_call(
        body, name=name, grid=(heads // g, n, n),
        in_specs=[pl.BlockSpec((t, g * qw), lambda h, i, j: (i, h)),
                  pl.BlockSpec((t, g * qw), lambda h, i, j: (jnp.minimum(j, i), h)),
                  pl.BlockSpec((g * vw, t), lambda h, i, j: (h, jnp.minimum(j, i)))],
        out_specs=[pl.BlockSpec((t, g * vw), lambda h, i, j: (i, h)),
                   pl.BlockSpec((g, 1, t), lambda h, i, j: (h, 0, i))],
        out_shape=[jax.ShapeDtypeStruct((s, heads * vw), BF16),
                   jax.ShapeDtypeStruct((heads, 1, s), F32)],
        scratch_shapes=[pltpu.VMEM((g, 1, t), F32), pltpu.VMEM((g, 1, t), F32), pltpu.VMEM((g, vw, t), F32)],
        compiler_params=_cp(dimension_semantics=("arbitrary", "arbitrary", "arbitrary")),
    )(qf, kf, vt)


def _flash_bwd_dq(name, qf, kf, kft, v, do, o, lse, cq, sq, heads):
    s = qf.shape[0]
    t = _tile(s, ATT_TILE)
    n = s // t
    g = min(ATT_HEADS, heads)
    qw, vw = HEAD_QK_PAD, HEAD_V

    def body(q_ref, k_ref, kt_ref, v_ref, do_ref, o_ref, lse_ref, c_ref, s_ref, dq_ref, dl_ref, acc_sc):
        i, j = pl.program_id(1), pl.program_id(2)

        @pl.when(j == 0)
        def _():
            acc_sc[...] = jnp.zeros_like(acc_sc)
            for hh in range(g):
                cols = slice(hh * vw, (hh + 1) * vw)
                col = jnp.sum(do_ref[:, cols].astype(F32) * o_ref[:, cols].astype(F32), axis=1, keepdims=True)
                dl_ref[hh] = jnp.broadcast_to(col, (t, LANE)).T[0:1, :]

        def step(diag):
            for hh in range(g):
                sc = lax.dot_general(k_ref[:, hh * qw:(hh + 1) * qw], q_ref[:, hh * qw:(hh + 1) * qw], _NT,
                                     preferred_element_type=F32)
                p = jnp.exp(sc - lse_ref[hh])
                if diag:
                    p = jnp.where(_chunk_mask_t(t), p, 0.0)
                dp = lax.dot_general(v_ref[:, hh * vw:(hh + 1) * vw], do_ref[:, hh * vw:(hh + 1) * vw], _NT,
                                     preferred_element_type=F32)
                ds = (p * (dp - dl_ref[hh])).astype(BF16)
                acc_sc[hh] += jnp.dot(kt_ref[hh * qw:(hh + 1) * qw, :], ds, preferred_element_type=F32)

        @pl.when(j < i)
        def _():
            step(False)

        @pl.when(j == i)
        def _():
            step(True)
            for hh in range(g):
                dq_ref[:, hh * qw:(hh + 1) * qw] = _rope_t(acc_sc[hh].T, c_ref[...], s_ref[...]).astype(BF16)

    qspec = pl.BlockSpec((t, g * qw), lambda h, i, j: (i, h))
    ospec = pl.BlockSpec((t, g * vw), lambda h, i, j: (i, h))
    vspec = pl.BlockSpec((g, 1, t), lambda h, i, j: (h, 0, i))
    tab = pl.BlockSpec((t, qw), lambda h, i, j: (i, 0))
    return pl.pallas_call(
        body, name=name, grid=(heads // g, n, n),
        in_specs=[qspec,
                  pl.BlockSpec((t, g * qw), lambda h, i, j: (jnp.minimum(j, i), h)),
                  pl.BlockSpec((g * qw, t), lambda h, i, j: (h, jnp.minimum(j, i))),
                  pl.BlockSpec((t, g * vw), lambda h, i, j: (jnp.minimum(j, i), h)),
                  ospec, ospec, vspec, tab, tab],
        out_specs=[qspec, vspec],
        out_shape=[jax.ShapeDtypeStruct(qf.shape, BF16), jax.ShapeDtypeStruct((heads, 1, s), F32)],
        scratch_shapes=[pltpu.VMEM((g, qw, t), F32)],
        compiler_params=_cp(dimension_semantics=("arbitrary", "arbitrary", "arbitrary")),
    )(qf, kf, kft, v, do, o, lse, cq, sq)


def _flash_bwd_dkv(name, qf, kf, v, do, lse, delta, ck, sk, heads):
    s = qf.shape[0]
    t = _tile(s, ATT_TILE)
    n = s // t
    g = min(ATT_HEADS, heads)
    qw, vw = HEAD_QK_PAD, HEAD_V
    lse_rows, delta_rows = lse, delta

    def body(q_ref, k_ref, v_ref, do_ref, lse_ref, dl_ref, c_ref, s_ref, dkv_ref, dpe_ref, dk_sc, dv_sc):
        j, h, i = pl.program_id(0), pl.program_id(1), pl.program_id(2)

        @pl.when(i == 0)
        def _():
            dk_sc[...] = jnp.zeros_like(dk_sc)
            dv_sc[...] = jnp.zeros_like(dv_sc)

        def step(diag):
            for hh in range(g):
                q = q_ref[:, hh * qw:(hh + 1) * qw]
                do = do_ref[:, hh * vw:(hh + 1) * vw]
                sc = lax.dot_general(k_ref[:, hh * qw:(hh + 1) * qw], q, _NT, preferred_element_type=F32)
                p = jnp.exp(sc - lse_ref[hh])
                if diag:
                    p = jnp.where(_chunk_mask_t(t), p, 0.0)
                dv_sc[hh] += jnp.dot(p.astype(BF16), do, preferred_element_type=F32)
                dp = lax.dot_general(v_ref[:, hh * vw:(hh + 1) * vw], do, _NT, preferred_element_type=F32)
                ds = (p * (dp - dl_ref[hh])).astype(BF16)
                dk_sc[hh] += jnp.dot(ds, q, preferred_element_type=F32)

        @pl.when(i > j)
        def _():
            step(False)

        @pl.when(i == j)
        def _():
            step(True)

        @pl.when(i == n - 1)
        def _():
            pe = None
            for hh in range(g):
                dk = dk_sc[hh]
                dkv_ref[:, hh * qw:(hh + 1) * qw] = jnp.concatenate([dk[:, 0:HEAD_NOPE], dv_sc[hh]],
                                                                     axis=1).astype(BF16)
                part = dk[:, HEAD_NOPE:HEAD_QK_PAD]
                pe = part if pe is None else pe + part

            @pl.when(h == 0)
            def _():
                dpe_ref[...] = pe

            @pl.when(h > 0)
            def _():
                dpe_ref[...] += pe

            @pl.when(h == heads // g - 1)
            def _():
                dpe_ref[...] = _rope_t(dpe_ref[...], c_ref[...], s_ref[...])

    qrow = lambda j, h, i: (jnp.maximum(i, j), h)
    vrow = lambda j, h, i: (h, 0, jnp.maximum(i, j))
    return pl.pallas_call(
        body, name=name, grid=(n, heads // g, n),
        in_specs=[pl.BlockSpec((t, g * qw), qrow),
                  pl.BlockSpec((t, g * qw), lambda j, h, i: (j, h)),
                  pl.BlockSpec((t, g * vw), lambda j, h, i: (j, h)),
                  pl.BlockSpec((t, g * vw), qrow),
                  pl.BlockSpec((g, 1, t), vrow),
                  pl.BlockSpec((g, 1, t), vrow),
                  pl.BlockSpec((t, LANE), lambda j, h, i: (j, 0)),
                  pl.BlockSpec((t, LANE), lambda j, h, i: (j, 0))],
        out_specs=[pl.BlockSpec((t, g * (HEAD_NOPE + HEAD_V)), lambda j, h, i: (j, h)),
                   pl.BlockSpec((t, LANE), lambda j, h, i: (j, 0))],
        out_shape=[jax.ShapeDtypeStruct((s, heads * (HEAD_NOPE + HEAD_V)), BF16),
                   jax.ShapeDtypeStruct((s, LANE), F32)],
        scratch_shapes=[pltpu.VMEM((g, t, qw), F32), pltpu.VMEM((g, t, vw), F32)],
        compiler_params=_cp(dimension_semantics=("arbitrary", "arbitrary", "arbitrary")),
    )(qf, kf, v, do, lse_rows, delta_rows, ck, sk)


def _adamw(name, parts, w, m, v):
    p, r, c = parts.shape
    tr = _tile(r, max(8, (256 * 1024) // max(c, 1)))
    bc1 = 1.0 - ADAM_B1 ** ADAM_STEP
    bc2 = 1.0 - ADAM_B2 ** ADAM_STEP

    def body(p_ref, w_ref, m_ref, v_ref, g_ref, d_ref, nm_ref, nv_ref):
        g = p_ref[0].astype(F32)
        for q in range(1, p):
            g = g + p_ref[q].astype(F32)
        nm = ADAM_B1 * m_ref[...] + (1.0 - ADAM_B1) * g
        nv = ADAM_B2 * v_ref[...] + (1.0 - ADAM_B2) * (g * g)
        g_ref[...] = g
        nm_ref[...] = nm
        nv_ref[...] = nv
        d_ref[...] = -ADAM_LR * ((nm / bc1) / (jnp.sqrt(nv / bc2) + ADAM_EPS) + ADAM_WD * w_ref[...])

    blk = pl.BlockSpec((tr, c), lambda i: (i, 0))
    sh = jax.ShapeDtypeStruct((r, c), F32)
    return pl.pallas_call(
        body, name=name, grid=(r // tr,),
        in_specs=[pl.BlockSpec((p, tr, c), lambda i: (0, i, 0)), blk, blk, blk],
        out_specs=[blk] * 4, out_shape=[sh] * 4,
        compiler_params=_cp(),
    )(parts, w, m, v)


def _my_place():
    x, y, c = lax.axis_index("x"), lax.axis_index("y"), lax.axis_index("c")
    return x, y, c


def _flip(v, bit):
    return 1 - v if bit else v


def _block(ref, axis, idx, size):
    return ref.at[(slice(None),) * axis + (pl.ds(idx * size, size),)]


HBM_SPEC = pl.BlockSpec(memory_space=pltpu.HBM)
SEM_SPEC = pl.BlockSpec(memory_space=pltpu.SEMAPHORE)
DATAFLOW = pltpu.SideEffectType.DATAFLOW_SIDE_EFFECTING


def _hbm(a):
    return pltpu.with_memory_space_constraint(a, pltpu.HBM)


def _remote_copies(jobs, bufs, send_sems, recv_sems):
    return [pltpu.make_async_remote_copy(src_ref=src, dst_ref=dst, send_sem=send_sems.at[q],
                                         recv_sem=recv_sems.at[q], device_id=dev, device_id_type=MESH)
            for q, (src, dst, dev) in enumerate(jobs(bufs))]


def _split_start(name, bufs, jobs, n_jobs, after):
    nb = len(bufs)

    def body(*refs):
        send_sems, recv_sems = refs[nb + 1], refs[nb + 2]
        for cp in _remote_copies(jobs, refs[:nb], send_sems, recv_sems):
            cp.start()
        refs[-1][...] = jnp.zeros_like(refs[-1])

    outs = pl.pallas_call(
        body, name=name,
        out_shape=(pltpu.SemaphoreType.DMA((n_jobs,)), pltpu.SemaphoreType.DMA((n_jobs,)),
                   *[pltpu.HBM(b.shape, b.dtype) for b in bufs], jax.ShapeDtypeStruct((8, LANE), F32)),
        in_specs=[HBM_SPEC] * nb + [ANY],
        out_specs=(SEM_SPEC, SEM_SPEC, *[HBM_SPEC] * nb, VMEM_SPEC),
        input_output_aliases={q: 2 + q for q in range(nb)},
        compiler_params=pltpu.CompilerParams(has_side_effects=DATAFLOW),
    )(*[_hbm(b) for b in bufs], after)
    return outs[0], outs[1], list(outs[2:2 + nb]), outs[-1]


def _split_wait(name, bufs, send_sems, recv_sems, jobs, after):
    nb = len(bufs)

    def body(*refs):
        for cp in _remote_copies(jobs, refs[:nb], refs[nb], refs[nb + 1]):
            cp.wait_send()
            cp.wait_recv()

    outs = pl.pallas_call(
        body, name=name,
        out_shape=tuple(pltpu.HBM(b.shape, b.dtype) for b in bufs),
        in_specs=[HBM_SPEC] * nb + [SEM_SPEC, SEM_SPEC, ANY],
        out_specs=tuple([HBM_SPEC] * nb),
        input_output_aliases={q: q for q in range(nb)},
        compiler_params=pltpu.CompilerParams(has_side_effects=DATAFLOW),
    )(*bufs, send_sems, recv_sems, after)
    return list(outs)


PLACE_TILE_BYTES = 2 * 1024 * 1024


def _own_block_spec(tr, c, nblk, axis):
    if axis == 0:
        return pl.BlockSpec((tr, c), lambda i, me: (me[0] * nblk + i, 0))
    return pl.BlockSpec((tr, c), lambda i, me: (i, me[0]))


def _cast_place(name, w, layer, axis, me):
    _, r, c = w.shape
    tr = _tile(r, max(SUBLANE_BF16, PLACE_TILE_BYTES // (4 * c)))
    nblk = r // tr
    full = (N_DEV * r, c) if axis == 0 else (r, N_DEV * c)

    def body(me_ref, w_ref, o_ref):
        o_ref[...] = w_ref[...].astype(BF16)

    return pl.pallas_call(
        body, name=name,
        grid_spec=pltpu.PrefetchScalarGridSpec(
            num_scalar_prefetch=1, grid=(nblk,),
            in_specs=[pl.BlockSpec((None, tr, c), lambda i, me: (layer, i, 0))],
            out_specs=_own_block_spec(tr, c, nblk, axis)),
        out_shape=jax.ShapeDtypeStruct(full, BF16), compiler_params=_cp(),
    )(me, w)


def _own_place(name, grad, land, layer, axis, me):
    _, _, r, c = land.shape
    tr = _tile(r, max(SUBLANE_BF16, PLACE_TILE_BYTES // (2 * c)))
    nblk = r // tr

    def body(me_ref, g_ref, land_ref, o_ref):
        o_ref[...] = g_ref[...]

    return pl.pallas_call(
        body, name=name,
        grid_spec=pltpu.PrefetchScalarGridSpec(
            num_scalar_prefetch=1, grid=(nblk,),
            in_specs=[_own_block_spec(tr, c, nblk, axis), ANY],
            out_specs=pl.BlockSpec((None, None, tr, c), lambda i, me: (0, layer, i, 0))),
        out_shape=jax.ShapeDtypeStruct(land.shape, land.dtype),
        input_output_aliases={2: 0}, compiler_params=_cp(),
    )(me, grad, land)


def _gather_jobs_a(axes, sizes):
    def jobs(bufs):
        x, y, c = _my_place()
        out = []
        for t, buf in enumerate(bufs):
            blk = _block(buf, axes[t], 4 * x + 2 * y + c, sizes[t])
            for dev in [(x, y, 1 - c), (1 - x, y, c), (x, 1 - y, c), (1 - x, 1 - y, c)]:
                out.append((blk, blk, dev))
        return out
    return jobs


def _gather_jobs_b(axes, sizes):
    nt = len(axes)

    def jobs(bufs):
        x, y, c = _my_place()
        out = []
        for t in range(nt):
            for px, py in [(1 - x, y), (x, 1 - y), (1 - x, 1 - y)]:
                blk = _block(bufs[t], axes[t], 4 * px + 2 * py + c, sizes[t])
                out.append((blk, blk, (x, y, 1 - c)))
        return out
    return jobs


def _exchange_jobs(axes, sizes, layers):
    nt = len(axes)

    def jobs(bufs):
        x, y, c = _my_place()
        out = []
        for k in range(1, N_DEV):
            px, py, pc = _flip(x, k & 4), _flip(y, k & 2), _flip(c, k & 1)
            for t in range(nt):
                out.append((_block(bufs[t], axes[t], 4 * px + 2 * py + pc, sizes[t]),
                            bufs[nt + t].at[k, layers[t]], (px, py, pc)))
        return out
    return jobs


def _gather_begin(name, lands, axes, after):
    sizes = [b.shape[ax] // N_DEV for b, ax in zip(lands, axes)]
    jobs = _gather_jobs_a(axes, sizes)
    send, recv, bufs, token = _split_start(name + "_a", lands, jobs, 4 * len(lands), after)
    return dict(name=name, axes=axes, sizes=sizes, send=send, recv=recv, bufs=bufs, jobs=jobs), token


def _gather_mid(h, after):
    bufs = _split_wait(h["name"] + "_aw", h["bufs"], h["send"], h["recv"], h["jobs"], after)
    jobs = _gather_jobs_b(h["axes"], h["sizes"])
    send, recv, lands, token = _split_start(h["name"] + "_b", bufs, jobs, 3 * len(bufs), after)
    return dict(h, send=send, recv=recv, bufs=lands, jobs=jobs), token


def _gather_end(h, after):
    return _split_wait(h["name"] + "_bw", h["bufs"], h["send"], h["recv"], h["jobs"], after)


def _exchange_begin(name, grads, axes, lands, layers, me, after):
    sizes = [g.shape[ax] // N_DEV for g, ax in zip(grads, axes)]
    lands = [_own_place(f"{name}_place{t}", grads[t], lands[t], layers[t], axes[t], me)
             for t in range(len(grads))]
    jobs = _exchange_jobs(axes, sizes, layers)
    send, recv, bufs, token = _split_start(name + "_s", list(grads) + lands, jobs, 7 * len(grads), after)
    return dict(name=name, n=len(grads), send=send, recv=recv, bufs=bufs, jobs=jobs), token


def _exchange_end(h, after):
    bufs = _split_wait(h["name"] + "_w", h["bufs"], h["send"], h["recv"], h["jobs"], after)
    return bufs[h["n"]:]


def _all_gather_small(name, vec, reduce):
    r = vec.shape[0]

    def body(v_ref, o_ref, *rest):
        if reduce:
            buf, send_sems, recv_sems = rest
        else:
            buf = o_ref
            send_sems, recv_sems = rest
        x, y, c = _my_place()
        mine = 4 * x + 2 * y + c
        buf[mine] = v_ref[...]
        copies = []
        for k in range(1, N_DEV):
            px, py, pc = _flip(x, k & 4), _flip(y, k & 2), _flip(c, k & 1)
            cp = pltpu.make_async_remote_copy(
                src_ref=v_ref, dst_ref=buf.at[mine], send_sem=send_sems.at[k - 1],
                recv_sem=recv_sems.at[k - 1], device_id=(px, py, pc), device_id_type=MESH)
            cp.start()
            copies.append(cp)
        for cp in copies:
            cp.wait()
        if reduce:
            acc = buf[0]
            for q in range(1, N_DEV):
                acc = acc + buf[q]
            o_ref[...] = acc

    scratch = [pltpu.SemaphoreType.DMA((N_DEV - 1,)), pltpu.SemaphoreType.DMA((N_DEV - 1,))]
    if reduce:
        scratch = [pltpu.VMEM((N_DEV, r, LANE), F32)] + scratch
        out_shape = jax.ShapeDtypeStruct((r, LANE), F32)
    else:
        out_shape = jax.ShapeDtypeStruct((N_DEV, r, LANE), F32)
    return pl.pallas_call(
        body, name=name, in_specs=[VMEM_SPEC], out_specs=VMEM_SPEC, out_shape=out_shape,
        scratch_shapes=scratch, compiler_params=_cp(has_side_effects=True),
    )(vec)


def _pack(arrs, row_mult=8):
    flat = jnp.concatenate([a.reshape(-1).astype(F32) for a in arrs])
    n = flat.shape[0]
    rows = -(-n // LANE)
    rows = -(-rows // row_mult) * row_mult
    return jnp.pad(flat, (0, rows * LANE - n)).reshape(rows, LANE)


def _unpack(vec, shapes):
    flat = vec.reshape(-1)
    out, pos = [], 0
    for sh in shapes:
        n = 1
        for s in sh:
            n *= s
        out.append(flat[pos:pos + n].reshape(sh))
        pos += n
    return out


BIG = ["conv_w_pw1", "conv_w_pw2", "mla_w_in", "mla_w_q_up", "mla_w_kv_up", "mla_w_o", "mlp_w1", "mlp_w2"]
BIG_AXIS = {"conv_w_pw1": 2, "conv_w_pw2": 1, "mla_w_in": 1, "mla_w_q_up": 2, "mla_w_kv_up": 2,
            "mla_w_o": 1, "mlp_w1": 2, "mlp_w2": 1}
SMALL_SHARDED = ["conv_w_dw", "mla_q_norm_g", "mla_kv_norm_g"]
REPLICATED = ["norm_mixer_g", "norm_mlp_g", "conv_b_pw1", "conv_b_dw", "conv_ln_g", "conv_ln_b",
              "conv_b_pw2", "final_norm_g"]
WEIGHTS = ["norm_mixer_g", "norm_mlp_g", "conv_w_pw1", "conv_b_pw1", "conv_w_dw", "conv_b_dw",
           "conv_ln_g", "conv_ln_b", "conv_w_pw2", "conv_b_pw2", "mla_w_in", "mla_q_norm_g",
           "mla_kv_norm_g", "mla_w_q_up", "mla_w_kv_up", "mla_w_o", "mlp_w1", "mlp_w2", "final_norm_g"]


def _unshard_last(g, lead):
    nd = g.ndim
    perm = tuple(range(1, nd - 1)) + (0, nd - 1)
    return g.transpose(perm).reshape(lead + (N_DEV * g.shape[-1],))


def _step(w, m, v, x, positions, target):
    s, d = x.shape
    depth = w["norm_mixer_g"].shape[0]
    n_conv, n_mla = w["conv_w_pw1"].shape[0], w["mla_w_in"].shape[0]
    heads = (w["mla_w_q_up"].shape[-1] * N_DEV) // (HEAD_NOPE + HEAD_ROPE)
    rq, rkv = w["mla_w_q_up"].shape[1], w["mla_w_kv_up"].shape[1]
    xi, yi, ci = _my_place()
    mine = 4 * xi + 2 * yi + ci

    def mixer_units(layer):
        names = (["conv_w_pw1", "conv_w_pw2"] if layer % 2 == 0
                 else ["mla_w_in", "mla_w_q_up", "mla_w_kv_up", "mla_w_o"])
        return [(n, layer // 2) for n in names]

    def mlp_units(layer):
        return [("mlp_w1", layer), ("mlp_w2", layer)]

    me_arr = mine.astype(jnp.int32).reshape(1)

    def gather_begin(tag, units, after):
        lands = [_cast_place(f"{tag}_place_{n}", w[n], jl, BIG_AXIS[n] - 1, me_arr) for n, jl in units]
        h, token = _gather_begin(tag, lands, [BIG_AXIS[n] - 1 for n, _ in units], after)
        return dict(h, units=units), token

    full = {}

    def gather_end(h, after):
        full.update(zip(h["units"], _gather_end(h, after)))

    small_shapes = [w[n].shape for n in SMALL_SHARDED]
    gathered = _all_gather_small("gather_small", _pack([w[n] for n in SMALL_SHARDED]), False)

    first_a, tok = gather_begin("gather_0a", mixer_units(0), gathered)
    first_b, tok = gather_begin("gather_0b", mlp_units(0), tok)
    pending = {}
    if depth > 1:
        pending[1], tok = gather_begin("gather_1", mixer_units(1) + mlp_units(1), tok)
    first_a, tok = _gather_mid(first_a, tok)
    gather_end(first_a, tok)

    per_dev = [_unpack(gathered[q], small_shapes) for q in range(N_DEV)]
    w_dw = _unshard_last(jnp.stack([p[0] for p in per_dev]), (n_conv, CONV_W))
    q_gain = _unshard_last(jnp.stack([p[1] for p in per_dev]), (n_mla,))
    kv_gain = _unshard_last(jnp.stack([p[2] for p in per_dev]), (n_mla,))
    w_dw_pad = jnp.pad(w_dw, ((0, 0), (0, HALO - CONV_W), (0, 0)))

    w_in_cols = rq + rkv + HEAD_ROPE

    def pad_w_in(a):
        return jnp.pad(a, ((0, 0), (0, rq + rkv + LANE - w_in_cols)))

    def pad_wq(a):
        return jnp.pad(a.reshape(rq, heads, HEAD_NOPE + HEAD_ROPE),
                       ((0, 0), (0, 0), (0, HEAD_QK_PAD - HEAD_NOPE - HEAD_ROPE))).reshape(rq, heads * HEAD_QK_PAD)

    inv_freq = ROPE_THETA ** (-jnp.arange(0, HEAD_ROPE, 2, dtype=F32) / HEAD_ROPE)
    ang = positions.reshape(s).astype(F32)[:, None] * inv_freq
    cos, sin = jnp.cos(ang), jnp.sin(ang)
    c64 = jnp.concatenate([cos, cos], axis=1)
    s64 = jnp.concatenate([-sin, sin], axis=1)
    zeros64 = jnp.zeros((s, LANE - HEAD_ROPE), F32)
    ck = jnp.concatenate([c64, zeros64], axis=1)
    sk = jnp.concatenate([s64, zeros64], axis=1)
    scale = (HEAD_NOPE + HEAD_ROPE) ** -0.5
    cq = scale * jnp.concatenate([jnp.ones((s, HEAD_NOPE), F32), ck], axis=1)
    sq = scale * jnp.concatenate([jnp.zeros((s, HEAD_NOPE), F32), sk], axis=1)

    def vec(a):
        return a.reshape(1, -1)

    saved = []
    wpad = {}
    for layer in range(depth):
        jl = layer // 2
        h = _rms_fwd(f"rms_mixer_{layer}", x, vec(w["norm_mixer_g"][layer]) + tok[0, 0])
        if layer % 2 == 0:
            ua, ug, glu = _mm_glu(f"conv_pw1_{layer}", h, full["conv_w_pw1", jl], None, vec(w["conv_b_pw1"][jl]))
            cc, sw = _conv_fwd(f"conv_dw_{layer}", glu, w_dw_pad[jl], vec(w["conv_b_dw"][jl]),
                               vec(w["conv_ln_g"][jl]), vec(w["conv_ln_b"][jl]))
            x1 = _mm_res(f"conv_pw2_{layer}", sw, full["conv_w_pw2", jl], None, x, vec(w["conv_b_pw2"][jl]))
            mix = (h, ua, ug, glu, cc, sw)
        else:
            wpad["in", jl] = pad_w_in(full["mla_w_in", jl])
            wpad["q", jl] = pad_wq(full["mla_w_q_up", jl])
            down = _mm_plain(f"mla_down_{layer}", h, wpad["in", jl], None, "nn", F32)
            qn, kvn, kpe = _mla_mid_fwd(f"mla_mid_{layer}", down, vec(q_gain[jl]), vec(kv_gain[jl]), ck, sk)
            qf = _mm_q(f"mla_q_{layer}", qn, wpad["q", jl], None, cq, sq)
            kf, vv, kft, vt = _mm_kv(f"mla_kv_{layer}", kvn, full["mla_w_kv_up", jl], None, kpe)
            o, lse = _flash_fwd(f"mla_attn_{layer}", qf, kf, vt, heads)
            x1 = _mm_res(f"mla_out_{layer}", o, full["mla_w_o", jl], None, x)
            mix = (h, down, qn, kvn, qf, kf, kft, vv, o, lse)
        anchor = x1
        if layer == 0:
            first_b, anchor = _gather_mid(first_b, anchor)
        if layer + 2 < depth:
            pending[layer + 2], anchor = gather_begin(f"gather_{layer + 2}",
                                                      mixer_units(layer + 2) + mlp_units(layer + 2), anchor)
        if layer == 0:
            gather_end(first_b, anchor)
        elif layer + 1 < depth:
            pending[layer + 1], anchor = _gather_mid(pending[layer + 1], anchor)
        if anchor is not x1:
            tok = anchor
        h2 = _rms_fwd(f"rms_mlp_{layer}", x1, vec(w["norm_mlp_g"][layer]) + tok[0, 0])
        z, a = _mm_mlp_up(f"mlp_up_{layer}", h2, full["mlp_w1", layer], None)
        x2 = _mm_res(f"mlp_down_{layer}", a, full["mlp_w2", layer], None, x1)
        if layer + 1 < depth:
            if layer == 0:
                pending[1], tok = _gather_mid(pending[1], x2)
                gather_end(pending[1], tok)
            else:
                gather_end(pending[layer + 1], x2)
        saved.append((x, mix, x1, h2, z, a))
        x = x2

    loss_row, g, gb, d_final, _ = _final_loss("final_loss", x, vec(w["final_norm_g"]) + tok[0, 0], target)

    recv = {n: lax.empty((N_DEV,) + w[n].shape, BF16) for n in BIG}

    def exchange_begin(tag, items, after):
        names = [n for n, _, _ in items]
        h, token = _exchange_begin(tag, [gr for _, _, gr in items], [BIG_AXIS[n] - 1 for n in names],
                                   [recv[n] for n in names], [jl for _, jl, _ in items], me_arr, after)
        return dict(h, names=names), token

    def exchange_end(h, after):
        recv.update(zip(h["names"], _exchange_end(h, after)))

    mix_exchanges = []
    d_mixer, d_mlp = [None] * depth, [None] * depth
    d_small = {n: [None] * n_conv for n in ["conv_b_pw1", "conv_w_dw", "conv_b_dw", "conv_ln_g",
                                           "conv_ln_b", "conv_b_pw2"]}
    d_qg, d_kvg = [None] * n_mla, [None] * n_mla
    for layer in reversed(range(depth)):
        jl = layer // 2
        x0, mix, x1, h2, z, a = saved[layer]
        colsum_g = None
        dz = _mm_mlp_dz(f"mlp_dz_{layer}", gb, full["mlp_w2", layer], None, z)
        dw2 = _mm_wgrad(f"mlp_dw2_{layer}", a, gb)
        w2_exchange, tok = exchange_begin(f"exchange_w2_{layer}", [("mlp_w2", layer, dw2)], dz)
        tok, dz = lax.optimization_barrier((tok, dz))
        dh2 = _mm_plain(f"mlp_dh_{layer}", dz, full["mlp_w1", layer], None, "nt", F32)
        dw1 = _mm_wgrad(f"mlp_dw1_{layer}", h2, dz)
        w1_exchange, tok = exchange_begin(f"exchange_w1_{layer}", [("mlp_w1", layer, dw1)], tok)
        g, gb, d_mlp[layer], colsum_g = _rms_bwd(f"rms_mlp_bwd_{layer}", x1,
                                                 vec(w["norm_mlp_g"][layer]) + tok[0, 0], dh2, g)
        for hx in mix_exchanges:
            exchange_end(hx, g)
        if layer % 2 == 0:
            h, ua, ug, glu, cc, sw = mix
            d_small["conv_b_pw2"][jl] = colsum_g.reshape(-1)
            dsw = _mm_plain(f"conv_ds_{layer}", gb, full["conv_w_pw2", jl], None, "nt", F32)
            dwp2 = _mm_wgrad(f"conv_dw2_{layer}", sw, gb)
            hx2, tok = exchange_begin(f"exchange_pw2_{layer}", [("conv_w_pw2", jl, dwp2)], dsw)
            dc, dlg, dlb, dbdw = _conv_bwd_ln(f"conv_ln_bwd_{layer}", dsw, cc,
                                              vec(w["conv_ln_g"][jl]) + tok[0, 0], vec(w["conv_ln_b"][jl]))
            du, dwdw, dbu = _conv_bwd_dw(f"conv_dw_bwd_{layer}", dc, glu, ua, ug, w_dw_pad[jl])
            d_small["conv_ln_g"][jl] = dlg.reshape(-1)
            d_small["conv_ln_b"][jl] = dlb.reshape(-1)
            d_small["conv_b_dw"][jl] = dbdw.reshape(-1)
            d_small["conv_w_dw"][jl] = dwdw[:CONV_W]
            d_small["conv_b_pw1"][jl] = dbu.reshape(-1)
            dwp1 = _mm_wgrad(f"conv_dw1_{layer}", h, du)
            hx1, tok = exchange_begin(f"exchange_pw1_{layer}", [("conv_w_pw1", jl, dwp1)], dbu)
            tok, du = lax.optimization_barrier((tok, du))
            dh = _mm_plain(f"conv_dh_{layer}", du, full["conv_w_pw1", jl], None, "nt", F32)
            mix_exchanges = [hx2, hx1]
        else:
            h, down, qn, kvn, qf, kf, kft, vv, o, lse = mix
            do = _mm_plain(f"mla_do_{layer}", gb, full["mla_w_o", jl], None, "nt", BF16)
            dwo = _mm_wgrad(f"mla_dwo_{layer}", o, gb)
            dq, delta = _flash_bwd_dq(f"mla_attn_dq_{layer}", qf, kf, kft, vv, do, o, lse, cq, sq, heads)
            dkv, dkpe = _flash_bwd_dkv(f"mla_attn_dkv_{layer}", qf, kf, vv, do, lse, delta, ck, sk, heads)
            dqn = _mm_plain(f"mla_dqn_{layer}", dq, wpad["q", jl], None, "nt", F32)
            dwq = _mm_wgrad(f"mla_dwq_{layer}", qn, dq).reshape(rq, heads, HEAD_QK_PAD)[
                :, :, :HEAD_NOPE + HEAD_ROPE].reshape(rq, heads * (HEAD_NOPE + HEAD_ROPE))
            dkvn = _mm_plain(f"mla_dkvn_{layer}", dkv, full["mla_w_kv_up", jl], None, "nt", F32)
            dwkv = _mm_wgrad(f"mla_dwkv_{layer}", kvn, dkv)
            ddown, d_qg[jl], d_kvg[jl] = _mla_mid_bwd(f"mla_mid_bwd_{layer}", down, vec(q_gain[jl]),
                                                      vec(kv_gain[jl]), dqn, dkvn, dkpe)
            dh = _mm_plain(f"mla_dh_{layer}", ddown, wpad["in", jl], None, "nt", F32)
            dwin = _mm_wgrad(f"mla_dwin_{layer}", h, ddown)[:, :w_in_cols]
            items = [("mla_w_in", jl, dwin), ("mla_w_q_up", jl, dwq), ("mla_w_kv_up", jl, dwkv),
                     ("mla_w_o", jl, dwo)]
            hx, tok = exchange_begin(f"exchange_mix_{layer}", items, dh)
            mix_exchanges = [hx]
        g, gb, d_mixer[layer], _ = _rms_bwd(f"rms_mixer_bwd_{layer}", x0,
                                            vec(w["norm_mixer_g"][layer]) + tok[0, 0], dh, g)
        exchange_end(w2_exchange, g)
        exchange_end(w1_exchange, g)
    grad_x = g

    out = {}

    def adamw_big(n):
        sh = w[n].shape
        r, c = sh[0] * sh[1], sh[2]
        res = _adamw(f"adamw_{n}", recv[n].reshape(N_DEV, r, c), w[n].reshape(r, c),
                     m[n].reshape(r, c), v[n].reshape(r, c))
        out[n] = [t.reshape(sh) for t in res]

    late = [n for hx in mix_exchanges for n in hx["names"]]
    early = [n for n in BIG if n not in late]
    for n in early:
        adamw_big(n)
    anchor = out[early[-1]][1]
    for hx in mix_exchanges:
        exchange_end(hx, anchor)
    for n in late:
        adamw_big(n)

    small_full = {
        "norm_mixer_g": jnp.concatenate(d_mixer, axis=0), "norm_mlp_g": jnp.concatenate(d_mlp, axis=0),
        "conv_b_pw1": jnp.stack(d_small["conv_b_pw1"]), "conv_b_dw": jnp.stack(d_small["conv_b_dw"]),
        "conv_ln_g": jnp.stack(d_small["conv_ln_g"]), "conv_ln_b": jnp.stack(d_small["conv_ln_b"]),
        "conv_b_pw2": jnp.stack(d_small["conv_b_pw2"]), "final_norm_g": d_final.reshape(-1),
        "conv_w_dw": jnp.stack(d_small["conv_w_dw"]),
        "mla_q_norm_g": jnp.concatenate(d_qg, axis=0), "mla_kv_norm_g": jnp.concatenate(d_kvg, axis=0),
    }
    names = REPLICATED + SMALL_SHARDED
    packed, _ = lax.optimization_barrier((_pack([small_full[n] for n in names]), anchor))
    summed = _unpack(_all_gather_small("reduce_small", packed, True), [small_full[n].shape for n in names])
    summed = dict(zip(names, summed))
    for n in SMALL_SHARDED:
        width = w[n].shape[-1]
        summed[n] = lax.dynamic_slice_in_dim(summed[n], mine * width, width, axis=summed[n].ndim - 1)
    for group, tag in ((REPLICATED, "replicated"), (SMALL_SHARDED, "small_sharded")):
        shapes = [w[n].shape for n in group]
        res = _adamw(f"adamw_{tag}", _pack([summed[n] for n in group])[None],
                     _pack([w[n] for n in group]), _pack([m[n] for n in group]), _pack([v[n] for n in group]))
        unpacked = [_unpack(t, shapes) for t in res]
        for q, n in enumerate(group):
            out[n] = [unpacked[0][q], unpacked[1][q], unpacked[2][q], unpacked[3][q]]

    loss = lax.psum(loss_row[0, 0], ("x", "y", "c"))
    return loss, grad_x, out


def kernel(x, positions, norm_mixer_g, norm_mlp_g, conv_w_pw1, conv_b_pw1, conv_w_dw, conv_b_dw, conv_ln_g, conv_ln_b, conv_w_pw2, conv_b_pw2, mla_w_in, mla_q_norm_g, mla_kv_norm_g, mla_w_q_up, mla_w_kv_up, mla_w_o, mlp_w1, mlp_w2, final_norm_g, loss_target, m_norm_mixer_g, m_norm_mlp_g, m_conv_w_pw1, m_conv_b_pw1, m_conv_w_dw, m_conv_b_dw, m_conv_ln_g, m_conv_ln_b, m_conv_w_pw2, m_conv_b_pw2, m_mla_w_in, m_mla_q_norm_g, m_mla_kv_norm_g, m_mla_w_q_up, m_mla_w_kv_up, m_mla_w_o, m_mlp_w1, m_mlp_w2, m_final_norm_g, v_norm_mixer_g, v_norm_mlp_g, v_conv_w_pw1, v_conv_b_pw1, v_conv_w_dw, v_conv_b_dw, v_conv_ln_g, v_conv_ln_b, v_conv_w_pw2, v_conv_b_pw2, v_mla_w_in, v_mla_q_norm_g, v_mla_kv_norm_g, v_mla_w_q_up, v_mla_w_kv_up, v_mla_w_o, v_mlp_w1, v_mlp_w2, v_final_norm_g):
    ws = (norm_mixer_g, norm_mlp_g, conv_w_pw1, conv_b_pw1, conv_w_dw, conv_b_dw, conv_ln_g, conv_ln_b,
          conv_w_pw2, conv_b_pw2, mla_w_in, mla_q_norm_g, mla_kv_norm_g, mla_w_q_up, mla_w_kv_up, mla_w_o,
          mlp_w1, mlp_w2, final_norm_g)
    ms = (m_norm_mixer_g, m_norm_mlp_g, m_conv_w_pw1, m_conv_b_pw1, m_conv_w_dw, m_conv_b_dw, m_conv_ln_g,
          m_conv_ln_b, m_conv_w_pw2, m_conv_b_pw2, m_mla_w_in, m_mla_q_norm_g, m_mla_kv_norm_g,
          m_mla_w_q_up, m_mla_w_kv_up, m_mla_w_o, m_mlp_w1, m_mlp_w2, m_final_norm_g)
    vs = (v_norm_mixer_g, v_norm_mlp_g, v_conv_w_pw1, v_conv_b_pw1, v_conv_w_dw, v_conv_b_dw, v_conv_ln_g,
          v_conv_ln_b, v_conv_w_pw2, v_conv_b_pw2, v_mla_w_in, v_mla_q_norm_g, v_mla_kv_norm_g,
          v_mla_w_q_up, v_mla_w_kv_up, v_mla_w_o, v_mlp_w1, v_mlp_w2, v_final_norm_g)
    w, m, v = dict(zip(WEIGHTS, ws)), dict(zip(WEIGHTS, ms)), dict(zip(WEIGHTS, vs))
    s, d = x.shape[-2], x.shape[-1]
    loss, grad_x, out = _step(w, m, v, x.reshape(s, d), positions, loss_target.reshape(s, d))
    grads = [out[n][0] for n in WEIGHTS]
    deltas = [out[n][1] for n in WEIGHTS]
    new_m = [out[n][2] for n in WEIGHTS]
    new_v = [out[n][3] for n in WEIGHTS]
    return (loss, grad_x.reshape(x.shape), *grads, *deltas, *new_m, *new_v)
```

```python
import functools

import jax
import jax.numpy as jnp
from jax import lax
from jax.experimental import pallas as pl
from jax.experimental.pallas import tpu as pltpu

F32 = jnp.float32
BF16 = jnp.bfloat16

NORM_EPS = 1e-6
LN_EPS = 1e-5
ROPE_THETA = 10000.0
CHUNK_BITS = 6
HEAD_NOPE = 128
HEAD_ROPE = 64
HEAD_V = 128
HEAD_QK_PAD = 256
CONV_W = 31
HALO = 32
N_DEV = 8

ADAM_LR = 0.001
ADAM_B1 = 0.9
ADAM_B2 = 0.999
ADAM_EPS = 1e-08
ADAM_WD = 0.01
ADAM_STEP = 10

V7X_VMEM_BYTES = 64 * 1024 * 1024
VMEM_LIMIT = (V7X_VMEM_BYTES * 3) // 4
LANE = 128

MESH = pl.DeviceIdType.MESH
ANY = pl.BlockSpec(memory_space=pl.ANY)
VMEM_SPEC = pl.BlockSpec(memory_space=pltpu.VMEM)


def _cp(**kw):
    return pltpu.CompilerParams(vmem_limit_bytes=VMEM_LIMIT, **kw)


SUBLANE = 8
SUBLANE_BF16 = 16

TM_PREF = 1024
TN_PREF = 1024
TK_PREF = 2048


def _tile(n, pref, mult=SUBLANE_BF16):
    if n <= pref + pref // 2:
        return n
    t = (pref // mult) * mult
    while t >= mult:
        if n % t == 0:
            return t
        t -= mult
    return n


def _sigmoid(x):
    return 1.0 / (1.0 + jnp.exp(-x))


def _rot_half(x):
    n = x.shape[-1]
    lane = lax.broadcasted_iota(jnp.int32, x.shape, x.ndim - 1)
    first = (lane & 63) < 32
    return jnp.where(first, pltpu.roll(x, n - 32, x.ndim - 1), pltpu.roll(x, 32, x.ndim - 1))


def _rope(x, c, s):
    return x * c + _rot_half(x) * s


def _rope_t(d, c, s):
    return d * c + _rot_half(d * s)


def _chunk_mask_t(t):
    row = lax.broadcasted_iota(jnp.int32, (t, t), 0)
    col = lax.broadcasted_iota(jnp.int32, (t, t), 1)
    return jnp.right_shift(row, CHUNK_BITS) <= jnp.right_shift(col, CHUNK_BITS)


def _rms_fwd(name, x, g):
    t, d = x.shape
    tm = _tile(t, 512)

    def body(x_ref, g_ref, o_ref):
        xf = x_ref[...]
        r = lax.rsqrt(jnp.mean(xf * xf, axis=-1, keepdims=True) + NORM_EPS)
        o_ref[...] = (xf * r * g_ref[...]).astype(o_ref.dtype)

    return pl.pallas_call(
        body, name=name, grid=(t // tm,),
        in_specs=[pl.BlockSpec((tm, d), lambda i: (i, 0)), pl.BlockSpec((1, d), lambda i: (0, 0))],
        out_specs=pl.BlockSpec((tm, d), lambda i: (i, 0)),
        out_shape=jax.ShapeDtypeStruct((t, d), BF16),
        compiler_params=_cp(),
    )(x, g)


def _rms_bwd_math(xf, g, dy):
    r = lax.rsqrt(jnp.mean(xf * xf, axis=-1, keepdims=True) + NORM_EPS)
    xh = xf * r
    dg = jnp.sum(dy * xh, axis=0, keepdims=True)
    dxh = dy * g
    dx = r * (dxh - xh * jnp.mean(dxh * xh, axis=-1, keepdims=True))
    return dx, dg


def _rms_bwd(name, x, g, dy, resid):
    t, d = x.shape
    tm = _tile(t, 256)

    def body(x_ref, g_ref, dy_ref, r_ref, dx_ref, dxb_ref, dg_ref, cs_ref):
        @pl.when(pl.program_id(0) == 0)
        def _():
            dg_ref[...] = jnp.zeros_like(dg_ref)
            cs_ref[...] = jnp.zeros_like(cs_ref)

        dx, dg = _rms_bwd_math(x_ref[...], g_ref[...], dy_ref[...])
        tot = r_ref[...] + dx
        dx_ref[...] = tot
        dxb_ref[...] = tot.astype(BF16)
        dg_ref[...] += dg
        cs_ref[...] += jnp.sum(tot, axis=0, keepdims=True)

    row = pl.BlockSpec((tm, d), lambda i: (i, 0))
    vec = pl.BlockSpec((1, d), lambda i: (0, 0))
    return pl.pallas_call(
        body, name=name, grid=(t // tm,),
        in_specs=[row, vec, row, row],
        out_specs=[row, row, vec, vec],
        out_shape=[jax.ShapeDtypeStruct((t, d), F32), jax.ShapeDtypeStruct((t, d), BF16),
                   jax.ShapeDtypeStruct((1, d), F32), jax.ShapeDtypeStruct((1, d), F32)],
        compiler_params=_cp(dimension_semantics=("arbitrary",)),
    )(x, g, dy, resid)


def _final_loss(name, x, g, target):
    t, d = x.shape
    tm = _tile(t, 256)

    def body(x_ref, g_ref, t_ref, loss_ref, dx_ref, dxb_ref, dg_ref, cs_ref):
        @pl.when(pl.program_id(0) == 0)
        def _():
            loss_ref[...] = jnp.zeros_like(loss_ref)
            dg_ref[...] = jnp.zeros_like(dg_ref)
            cs_ref[...] = jnp.zeros_like(cs_ref)

        xf = x_ref[...]
        gg = g_ref[...]
        r = lax.rsqrt(jnp.mean(xf * xf, axis=-1, keepdims=True) + NORM_EPS)
        err = xf * r * gg - t_ref[...]
        part = 0.5 * jnp.sum(jnp.mean(err * err, axis=-1, keepdims=True), axis=0, keepdims=True)
        loss_ref[...] += jnp.broadcast_to(part, loss_ref.shape)
        dx, dg = _rms_bwd_math(xf, gg, err * (1.0 / d))
        dx_ref[...] = dx
        dxb_ref[...] = dx.astype(BF16)
        dg_ref[...] += dg
        cs_ref[...] += jnp.sum(dx, axis=0, keepdims=True)

    row = pl.BlockSpec((tm, d), lambda i: (i, 0))
    vec = pl.BlockSpec((1, d), lambda i: (0, 0))
    return pl.pallas_call(
        body, name=name, grid=(t // tm,),
        in_specs=[row, vec, row],
        out_specs=[pl.BlockSpec((1, LANE), lambda i: (0, 0)), row, row, vec, vec],
        out_shape=[jax.ShapeDtypeStruct((1, LANE), F32), jax.ShapeDtypeStruct((t, d), F32),
                   jax.ShapeDtypeStruct((t, d), BF16), jax.ShapeDtypeStruct((1, d), F32),
                   jax.ShapeDtypeStruct((1, d), F32)],
        compiler_params=_cp(dimension_semantics=("arbitrary",)),
    )(x, g, target)


_DIMS = {
    "nn": (((1,), (0,)), ((), ())),
    "nt": (((1,), (1,)), ((), ())),
    "tn": (((0,), (0,)), ((), ())),
}


def _mm(name, a, bs, *, mode, m, n, k, epilogue, out_shape, out_specs, extras=(), extra_specs=(),
        a_lead=None, aliases=None, tn_div=1):
    tm, tn, tk = _tiles(m, n, k, tn_div)
    nk = k // tk
    nb, ne = len(bs), len(extras)
    no = len(out_shape)
    dims = _DIMS[mode]

    def with_lead(shape, idx, lead):
        if lead is None:
            return pl.BlockSpec(shape, idx)
        return pl.BlockSpec((None,) + shape, lambda i, j, kk: (lead,) + idx(i, j, kk))

    if mode == "tn":
        a_spec = with_lead((tk, tm), lambda i, j, kk: (kk, i), a_lead)
    else:
        a_spec = with_lead((tm, tk), lambda i, j, kk: (i, kk), a_lead)
    b_specs = []
    for _, lead, off in bs:
        if mode == "nt":
            b_specs.append(with_lead((tn, tk), lambda i, j, kk, off=off: (j + off, kk), lead))
        else:
            b_specs.append(with_lead((tk, tn), lambda i, j, kk, off=off: (kk, j + off), lead))

    def body(*refs):
        a_ref = refs[0]
        b_refs = refs[1:1 + nb]
        ex = refs[1 + nb:1 + nb + ne]
        outs = refs[1 + nb + ne:1 + nb + ne + no]
        accs = refs[1 + nb + ne + no:]

        def part(b_ref):
            return lax.dot_general(a_ref[...], b_ref[...], dims, preferred_element_type=F32)

        if nk == 1:
            epilogue([part(b_ref) for b_ref in b_refs], ex, outs)
            return
        kk = pl.program_id(2)

        @pl.when(kk == 0)
        def _():
            for acc, b_ref in zip(accs, b_refs):
                acc[...] = part(b_ref)

        @pl.when(kk > 0)
        def _():
            for acc, b_ref in zip(accs, b_refs):
                acc[...] += part(b_ref)

        @pl.when(kk == nk - 1)
        def _():
            epilogue([acc[...] for acc in accs], ex, outs)

    scratch = [pltpu.VMEM((tm, tn), F32) for _ in range(nb)] if nk > 1 else []
    return pl.pallas_call(
        body, name=name, grid=(m // tm, n // tn, nk),
        in_specs=[a_spec] + b_specs + list(extra_specs),
        out_specs=list(out_specs), out_shape=list(out_shape), scratch_shapes=scratch,
        input_output_aliases=aliases or {},
        compiler_params=_cp(dimension_semantics=("arbitrary", "arbitrary", "arbitrary")),
    )(a, *[b for b, _, _ in bs], *extras), (tm, tn, tk)


def _ij(tm, tn):
    return pl.BlockSpec((tm, tn), lambda i, j, kk: (i, j))


def _tiles(m, n, k, tn_div=1):
    return _tile(m, TM_PREF), _tile(n, TN_PREF // tn_div, LANE), _tile(k, TK_PREF, LANE)


def _mm_plain(name, a, b, b_lead, mode, out_dtype):
    m, k = a.shape
    n = b.shape[-1] if mode == "nn" else b.shape[-2]
    tm, tn, _ = _tiles(m, n, k)

    def epilogue(accs, ex, outs):
        outs[0][...] = accs[0].astype(out_dtype)

    return _mm(name, a, [(b, b_lead, 0)], mode=mode, m=m, n=n, k=k, epilogue=epilogue,
               out_shape=[jax.ShapeDtypeStruct((m, n), out_dtype)], out_specs=[_ij(tm, tn)])[0][0]


def _mm_res(name, a, b, b_lead, resid, bias=None):
    m, k = a.shape
    n = b.shape[-1]
    tm, tn, _ = _tiles(m, n, k)
    extras, specs = [resid], [_ij(tm, tn)]
    if bias is not None:
        extras.append(bias)
        specs.append(pl.BlockSpec((1, tn), lambda i, j, kk: (0, j)))

    def epilogue(accs, ex, outs):
        y = ex[0][...] + accs[0]
        if bias is not None:
            y = y + ex[1][...]
        outs[0][...] = y

    return _mm(name, a, [(b, b_lead, 0)], mode="nn", m=m, n=n, k=k, epilogue=epilogue,
               extras=extras, extra_specs=specs,
               out_shape=[jax.ShapeDtypeStruct((m, n), F32)], out_specs=[_ij(tm, tn)])[0][0]


def _mm_mlp_up(name, h, w1, lead):
    m, k = h.shape
    n = w1.shape[-1]
    tm, tn, _ = _tiles(m, n, k)

    def epilogue(accs, ex, outs):
        z = accs[0]
        outs[0][...] = z.astype(BF16)
        r = jnp.maximum(z, 0.0)
        outs[1][...] = (r * r).astype(BF16)

    sh = jax.ShapeDtypeStruct((m, n), BF16)
    return _mm(name, h, [(w1, lead, 0)], mode="nn", m=m, n=n, k=k, epilogue=epilogue,
               out_shape=[sh, sh], out_specs=[_ij(tm, tn), _ij(tm, tn)])[0]


def _mm_mlp_dz(name, g, w2, lead, z):
    m, k = g.shape
    n = w2.shape[-2]
    tm, tn, _ = _tiles(m, n, k)

    def epilogue(accs, ex, outs):
        outs[0][...] = (accs[0] * (2.0 * jnp.maximum(ex[0][...].astype(F32), 0.0))).astype(BF16)

    return _mm(name, g, [(w2, lead, 0)], mode="nt", m=m, n=n, k=k, epilogue=epilogue,
               extras=[z], extra_specs=[_ij(tm, tn)],
               out_shape=[jax.ShapeDtypeStruct((m, n), BF16)], out_specs=[_ij(tm, tn)])[0][0]


def _mm_glu(name, h, w, lead, bias):
    m, k = h.shape
    n = w.shape[-1] // 2
    tm, tn, _ = _tiles(m, n, k, 2)
    off = n // tn

    def epilogue(accs, ex, outs):
        a = accs[0] + ex[0][...]
        gate = accs[1] + ex[1][...]
        outs[0][...] = a.astype(BF16)
        outs[1][...] = gate.astype(BF16)
        outs[2][...] = a * _sigmoid(gate)

    shb = jax.ShapeDtypeStruct((m, n), BF16)
    return _mm(name, h, [(w, lead, 0), (w, lead, off)], mode="nn", m=m, n=n, k=k, epilogue=epilogue,
               extras=[bias, bias],
               extra_specs=[pl.BlockSpec((1, tn), lambda i, j, kk: (0, j)),
                            pl.BlockSpec((1, tn), lambda i, j, kk: (0, j + off))],
               out_shape=[shb, shb, jax.ShapeDtypeStruct((m, n), F32)],
               out_specs=[_ij(tm, tn)] * 3, tn_div=2)[0]


def _mm_q(name, qn, wq_pad, lead, cq, sq):
    m, k = qn.shape
    n = wq_pad.shape[-1]
    tm, tn, _ = _tiles(m, n, k)
    rep = tn // HEAD_QK_PAD

    def epilogue(accs, ex, outs):
        c = jnp.tile(ex[0][...], (1, rep))
        s = jnp.tile(ex[1][...], (1, rep))
        outs[0][...] = _rope(accs[0], c, s).astype(BF16)

    tab = pl.BlockSpec((tm, HEAD_QK_PAD), lambda i, j, kk: (i, 0))
    return _mm(name, qn, [(wq_pad, lead, 0)], mode="nn", m=m, n=n, k=k, epilogue=epilogue,
               extras=[cq, sq], extra_specs=[tab, tab],
               out_shape=[jax.ShapeDtypeStruct((m, n), BF16)], out_specs=[_ij(tm, tn)])[0][0]


def _mm_kv(name, kvn, wkv, lead, kpe):
    m, k = kvn.shape
    n = wkv.shape[-1]
    tm, tn, _ = _tiles(m, n, k)
    heads = tn // (HEAD_NOPE + HEAD_V)

    def epilogue(accs, ex, outs):
        acc = accs[0]
        pe = ex[0][...].astype(F32)
        kparts, vparts = [], []
        for hh in range(heads):
            base = hh * (HEAD_NOPE + HEAD_V)
            kparts += [acc[:, base:base + HEAD_NOPE], pe]
            vparts.append(acc[:, base + HEAD_NOPE:base + HEAD_NOPE + HEAD_V])
        kf = jnp.concatenate(kparts, axis=1)
        vv = jnp.concatenate(vparts, axis=1) if heads > 1 else vparts[0]
        outs[0][...] = kf.astype(BF16)
        outs[1][...] = vv.astype(BF16)
        outs[2][...] = kf.T.astype(BF16)
        outs[3][...] = vv.T.astype(BF16)

    def ji(tn_, tm_):
        return pl.BlockSpec((tn_, tm_), lambda i, j, kk: (j, i))

    return _mm(name, kvn, [(wkv, lead, 0)], mode="nn", m=m, n=n, k=k, epilogue=epilogue,
               extras=[kpe], extra_specs=[pl.BlockSpec((tm, LANE), lambda i, j, kk: (i, 0))],
               out_shape=[jax.ShapeDtypeStruct((m, n), BF16), jax.ShapeDtypeStruct((m, n // 2), BF16),
                          jax.ShapeDtypeStruct((n, m), BF16), jax.ShapeDtypeStruct((n // 2, m), BF16)],
               out_specs=[_ij(tm, tn), _ij(tm, tn // 2), ji(tn, tm), ji(tn // 2, tm)])[0]


def _mm_wgrad(name, a, b):
    t, m = a.shape
    n = b.shape[-1]
    tm, tn, _ = _tiles(m, n, t)

    def epilogue(accs, ex, outs):
        outs[0][...] = accs[0].astype(BF16)

    return _mm(name, a, [(b, None, 0)], mode="tn", m=m, n=n, k=t, epilogue=epilogue,
               out_shape=[jax.ShapeDtypeStruct((m, n), BF16)], out_specs=[_ij(tm, tn)])[0][0]


CONV_ROWS = 256
CONV_RT = 64
CONV_CW = 256
CONV_LR = 32


def _ln_stats(c):
    mu = jnp.mean(c, axis=-1, keepdims=True)
    xc = c - mu
    rstd = lax.rsqrt(jnp.mean(xc * xc, axis=-1, keepdims=True) + LN_EPS)
    return xc * rstd, rstd


def _conv_fwd(name, glu, w_dw, b_dw, ln_g, ln_b):
    t, d = glu.shape
    tt = _tile(t, CONV_ROWS)
    rt, cw, lr = min(CONV_RT, tt), min(CONV_CW, d), min(CONV_LR, tt)
    hb = tt // HALO

    def body(gc_ref, gp_ref, w_ref, b_ref, lg_ref, lb_ref, c_ref, s_ref, buf, win):
        i = pl.program_id(0)
        buf[0:HALO, :] = jnp.where(i > 0, gp_ref[...], 0.0)
        buf[HALO:HALO + tt, :] = gc_ref[...]

        def chunk(cb, carry):
            col = pl.ds(pl.multiple_of(cb * cw, cw), cw)
            for r0 in range(0, tt, rt):
                acc = jnp.broadcast_to(b_ref[:, col], (rt, cw))
                for b in range(SUBLANE):
                    amax = (CONV_W - 1 - b) // SUBLANE
                    lo = r0 + HALO - (CONV_W - 1) + b
                    rows = rt + SUBLANE * amax
                    win[0:rows, :] = buf[lo:lo + rows, col]
                    for a in range(amax + 1):
                        k = SUBLANE * a + b
                        acc = acc + w_ref[k:k + 1, col] * win[SUBLANE * a:SUBLANE * a + rt, :]
                c_ref[r0:r0 + rt, col] = acc
            return carry

        lax.fori_loop(0, d // cw, chunk, 0)

        def ln(r, carry):
            rows = pl.ds(pl.multiple_of(r * lr, lr), lr)
            xh, _ = _ln_stats(c_ref[rows, :])
            y = xh * lg_ref[...] + lb_ref[...]
            s_ref[rows, :] = (y * _sigmoid(y)).astype(BF16)
            return carry

        lax.fori_loop(0, tt // lr, ln, 0)

    row = pl.BlockSpec((tt, d), lambda i: (i, 0))
    vec = pl.BlockSpec((1, d), lambda i: (0, 0))
    return pl.pallas_call(
        body, name=name, grid=(t // tt,),
        in_specs=[row, pl.BlockSpec((HALO, d), lambda i: (jnp.maximum(i * hb - 1, 0), 0)),
                  pl.BlockSpec((HALO, d), lambda i: (0, 0)), vec, vec, vec],
        out_specs=[row, row],
        out_shape=[jax.ShapeDtypeStruct((t, d), F32), jax.ShapeDtypeStruct((t, d), BF16)],
        scratch_shapes=[pltpu.VMEM((HALO + tt, d), F32), pltpu.VMEM((rt + HALO, cw), F32)],
        compiler_params=_cp(dimension_semantics=("arbitrary",)),
    )(glu, glu, w_dw, b_dw, ln_g, ln_b)


def _conv_bwd_ln(name, ds, c, ln_g, ln_b):
    t, d = c.shape
    tt = _tile(t, CONV_ROWS)
    lr = min(CONV_LR, tt)

    def body(ds_ref, c_ref, lg_ref, lb_ref, dc_ref, dg_ref, db_ref, dbdw_ref):
        @pl.when(pl.program_id(0) == 0)
        def _():
            dg_ref[...] = jnp.zeros_like(dg_ref)
            db_ref[...] = jnp.zeros_like(db_ref)
            dbdw_ref[...] = jnp.zeros_like(dbdw_ref)

        def chunk(r, carry):
            rows = pl.ds(pl.multiple_of(r * lr, lr), lr)
            xh, rstd = _ln_stats(c_ref[rows, :])
            g = lg_ref[...]
            y = xh * g + lb_ref[...]
            sg = _sigmoid(y)
            dy = ds_ref[rows, :] * (sg * (1.0 + y * (1.0 - sg)))
            dxh = dy * g
            dc = rstd * (dxh - jnp.mean(dxh, axis=-1, keepdims=True)
                         - xh * jnp.mean(dxh * xh, axis=-1, keepdims=True))
            dc_ref[rows, :] = dc
            dg_ref[...] += jnp.sum(dy * xh, axis=0, keepdims=True)
            db_ref[...] += jnp.sum(dy, axis=0, keepdims=True)
            dbdw_ref[...] += jnp.sum(dc, axis=0, keepdims=True)
            return carry

        lax.fori_loop(0, tt // lr, chunk, 0)

    row = pl.BlockSpec((tt, d), lambda i: (i, 0))
    vec = pl.BlockSpec((1, d), lambda i: (0, 0))
    vsh = jax.ShapeDtypeStruct((1, d), F32)
    return pl.pallas_call(
        body, name=name, grid=(t // tt,),
        in_specs=[row, row, vec, vec], out_specs=[row, vec, vec, vec],
        out_shape=[jax.ShapeDtypeStruct((t, d), F32), vsh, vsh, vsh],
        compiler_params=_cp(dimension_semantics=("arbitrary",)),
    )(ds, c, ln_g, ln_b)


def _conv_bwd_dw(name, dc, glu, ua, ug, w_dw):
    t, d = dc.shape
    tt = _tile(t, CONV_ROWS)
    rt, cw = min(CONV_RT, tt), min(CONV_CW, d)
    hb = tt // HALO
    nt = t // tt

    def body(dcc_ref, dcn_ref, gc_ref, gp_ref, ua_ref, ug_ref, w_ref,
             du_ref, dw_ref, dbu_ref, dbuf, gbuf, wacc, dwin, gwin):
        i = pl.program_id(0)

        @pl.when(i == 0)
        def _():
            wacc[...] = jnp.zeros_like(wacc)
            dbu_ref[...] = jnp.zeros_like(dbu_ref)

        dbuf[0:tt, :] = dcc_ref[...]
        dbuf[tt:tt + HALO, :] = jnp.where(i < nt - 1, dcn_ref[...], 0.0)
        gbuf[0:HALO, :] = jnp.where(i > 0, gp_ref[...], 0.0)
        gbuf[HALO:HALO + tt, :] = gc_ref[...]

        def chunk(cb, carry):
            c0 = pl.multiple_of(cb * cw, cw)
            col = pl.ds(c0, cw)
            colg = pl.ds(pl.multiple_of(d + cb * cw, cw), cw)
            for r0 in range(0, tt, rt):
                dcr = dbuf[r0:r0 + rt, col]
                dgl = jnp.zeros((rt, cw), F32)
                for b in range(SUBLANE):
                    amax = (CONV_W - 1 - b) // SUBLANE
                    hi = r0 + (CONV_W - 1) - b - SUBLANE * amax
                    rows = rt + SUBLANE * amax
                    dwin[0:rows, :] = dbuf[hi:hi + rows, col]
                    lo = r0 + HALO - (CONV_W - 1) + b
                    gwin[0:rows, :] = gbuf[lo:lo + rows, col]
                    for a in range(amax + 1):
                        k = SUBLANE * a + b
                        back = SUBLANE * (amax - a)
                        dgl = dgl + w_ref[k:k + 1, col] * dwin[back:back + rt, :]
                        prod = dcr * gwin[SUBLANE * a:SUBLANE * a + rt, :]
                        part = prod[0:8, :]
                        for r in range(8, rt, 8):
                            part = part + prod[r:r + 8, :]
                        wacc[8 * k:8 * k + 8, col] += part
                a = ua_ref[r0:r0 + rt, col].astype(F32)
                sg = _sigmoid(ug_ref[r0:r0 + rt, col].astype(F32))
                da = dgl * sg
                dgate = dgl * a * sg * (1.0 - sg)
                du_ref[r0:r0 + rt, col] = da.astype(BF16)
                du_ref[r0:r0 + rt, colg] = dgate.astype(BF16)
                dbu_ref[:, col] += jnp.sum(da, axis=0, keepdims=True)
                dbu_ref[:, colg] += jnp.sum(dgate, axis=0, keepdims=True)
            return carry

        lax.fori_loop(0, d // cw, chunk, 0)

        @pl.when(i == nt - 1)
        def _():
            for k in range(CONV_W):
                dw_ref[k:k + 1, :] = jnp.sum(wacc[8 * k:8 * k + 8, :], axis=0, keepdims=True)
            dw_ref[CONV_W:HALO, :] = jnp.zeros((HALO - CONV_W, d), F32)

    row = pl.BlockSpec((tt, d), lambda i: (i, 0))
    return pl.pallas_call(
        body, name=name, grid=(nt,),
        in_specs=[row, pl.BlockSpec((HALO, d), lambda i: (jnp.minimum((i + 1) * hb, t // HALO - 1), 0)),
                  row, pl.BlockSpec((HALO, d), lambda i: (jnp.maximum(i * hb - 1, 0), 0)),
                  row, row, pl.BlockSpec((HALO, d), lambda i: (0, 0))],
        out_specs=[pl.BlockSpec((tt, 2 * d), lambda i: (i, 0)),
                   pl.BlockSpec((HALO, d), lambda i: (0, 0)),
                   pl.BlockSpec((1, 2 * d), lambda i: (0, 0))],
        out_shape=[jax.ShapeDtypeStruct((t, 2 * d), BF16), jax.ShapeDtypeStruct((HALO, d), F32),
                   jax.ShapeDtypeStruct((1, 2 * d), F32)],
        scratch_shapes=[pltpu.VMEM((tt + HALO, d), F32), pltpu.VMEM((HALO + tt, d), F32),
                        pltpu.VMEM((8 * HALO, d), F32),
                        pltpu.VMEM((rt + HALO, cw), F32), pltpu.VMEM((rt + HALO, cw), F32)],
        compiler_params=_cp(dimension_semantics=("arbitrary",)),
    )(dc, dc, glu, glu, ua, ug, w_dw)


def _mla_mid_fwd(name, down, qg, kvg, ck, sk):
    t, w = down.shape
    rq, rkv = qg.shape[-1], kvg.shape[-1]
    tm = _tile(t, 512)

    def body(dn_ref, qg_ref, kvg_ref, ck_ref, sk_ref, qn_ref, kvn_ref, kpe_ref):
        cq = dn_ref[:, 0:rq]
        ckv = dn_ref[:, rq:rq + rkv]
        pe = dn_ref[:, rq + rkv:rq + rkv + LANE]
        qn_ref[...] = (cq * lax.rsqrt(jnp.mean(cq * cq, axis=-1, keepdims=True) + NORM_EPS)
                       * qg_ref[...]).astype(BF16)
        kvn_ref[...] = (ckv * lax.rsqrt(jnp.mean(ckv * ckv, axis=-1, keepdims=True) + NORM_EPS)
                        * kvg_ref[...]).astype(BF16)
        kpe_ref[...] = _rope(pe, ck_ref[...], sk_ref[...]).astype(BF16)

    def row(n):
        return pl.BlockSpec((tm, n), lambda i: (i, 0))

    def vec(n):
        return pl.BlockSpec((1, n), lambda i: (0, 0))

    return pl.pallas_call(
        body, name=name, grid=(t // tm,),
        in_specs=[row(w), vec(rq), vec(rkv), row(LANE), row(LANE)],
        out_specs=[row(rq), row(rkv), row(LANE)],
        out_shape=[jax.ShapeDtypeStruct((t, rq), BF16), jax.ShapeDtypeStruct((t, rkv), BF16),
                   jax.ShapeDtypeStruct((t, LANE), BF16)],
        compiler_params=_cp(),
    )(down, qg, kvg, ck, sk)


def _mla_mid_bwd(name, down, qg, kvg, dqn, dkvn, dkpe):
    t, w = down.shape
    rq, rkv = qg.shape[-1], kvg.shape[-1]
    tm = _tile(t, 256)

    def body(dn_ref, qg_ref, kvg_ref, dqn_ref, dkvn_ref, dkpe_ref, dd_ref, dqg_ref, dkvg_ref):
        @pl.when(pl.program_id(0) == 0)
        def _():
            dqg_ref[...] = jnp.zeros_like(dqg_ref)
            dkvg_ref[...] = jnp.zeros_like(dkvg_ref)

        dcq, dqg = _rms_bwd_math(dn_ref[:, 0:rq], qg_ref[...], dqn_ref[...])
        dckv, dkvg = _rms_bwd_math(dn_ref[:, rq:rq + rkv], kvg_ref[...], dkvn_ref[...])
        dd_ref[:, 0:rq] = dcq.astype(BF16)
        dd_ref[:, rq:rq + rkv] = dckv.astype(BF16)
        dd_ref[:, rq + rkv:rq + rkv + LANE] = dkpe_ref[...].astype(BF16)
        dqg_ref[...] += dqg
        dkvg_ref[...] += dkvg

    def row(n):
        return pl.BlockSpec((tm, n), lambda i: (i, 0))

    def vec(n):
        return pl.BlockSpec((1, n), lambda i: (0, 0))

    return pl.pallas_call(
        body, name=name, grid=(t // tm,),
        in_specs=[row(w), vec(rq), vec(rkv), row(rq), row(rkv), row(LANE)],
        out_specs=[row(w), vec(rq), vec(rkv)],
        out_shape=[jax.ShapeDtypeStruct((t, w), BF16), jax.ShapeDtypeStruct((1, rq), F32),
                   jax.ShapeDtypeStruct((1, rkv), F32)],
        compiler_params=_cp(dimension_semantics=("arbitrary",)),
    )(down, qg, kvg, dqn, dkvn, dkpe)


ATT_TILE = 512
ATT_HEADS = 2
ATT_HEADS_FWD = 4
_NT = (((1,), (1,)), ((), ()))
_TN = (((0,), (0,)), ((), ()))


def _flash_fwd(name, qf, kf, vt, heads):
    s = qf.shape[0]
    t = _tile(s, ATT_TILE)
    n = s // t
    g = min(ATT_HEADS_FWD, heads)
    qw, vw = HEAD_QK_PAD, HEAD_V

    def body(q_ref, k_ref, vt_ref, o_ref, lse_ref, m_sc, l_sc, acc_sc):
        i, j = pl.program_id(1), pl.program_id(2)

        @pl.when(j == 0)
        def _():
            m_sc[...] = jnp.full(m_sc.shape, -jnp.inf, F32)
            l_sc[...] = jnp.zeros_like(l_sc)
            acc_sc[...] = jnp.zeros_like(acc_sc)

        def step(diag):
            for hh in range(g):
                sc = lax.dot_general(k_ref[:, hh * qw:(hh + 1) * qw], q_ref[:, hh * qw:(hh + 1) * qw], _NT,
                                     preferred_element_type=F32)
                if diag:
                    sc = jnp.where(_chunk_mask_t(t), sc, -jnp.inf)
                m_old = m_sc[hh]
                m_new = jnp.maximum(m_old, jnp.max(sc, axis=0, keepdims=True))
                alpha = jnp.exp(m_old - m_new)
                p = jnp.exp(sc - m_new)
                l_sc[hh] = alpha * l_sc[hh] + jnp.sum(p, axis=0, keepdims=True)
                acc_sc[hh] = alpha * acc_sc[hh] + jnp.dot(vt_ref[hh * vw:(hh + 1) * vw, :], p.astype(BF16),
                                                          preferred_element_type=F32)
                m_sc[hh] = m_new

        @pl.when(j < i)
        def _():
            step(False)

        @pl.when(j == i)
        def _():
            step(True)
            for hh in range(g):
                l = l_sc[hh]
                o_ref[:, hh * vw:(hh + 1) * vw] = (acc_sc[hh] / l).T.astype(BF16)
                lse_ref[hh] = m_sc[hh] + jnp.log(l)

    return pl.pallas_call(
        body, name=name, grid=(heads // g, n, n),
        in_specs=[pl.BlockSpec((t, g * qw), lambda h, i, j: (i, h)),
                  pl.BlockSpec((t, g * qw), lambda h, i, j: (jnp.minimum(j, i), h)),
                  pl.BlockSpec((g * vw, t), lambda h, i, j: (h, jnp.minimum(j, i)))],
        out_specs=[pl.BlockSpec((t, g * vw), lambda h, i, j: (i, h)),
                   pl.BlockSpec((g, 1, t), lambda h, i, j: (h, 0, i))],
        out_shape=[jax.ShapeDtypeStruct((s, heads * vw), BF16),
                   jax.ShapeDtypeStruct((heads, 1, s), F32)],
        scratch_shapes=[pltpu.VMEM((g, 1, t), F32), pltpu.VMEM((g, 1, t), F32), pltpu.VMEM((g, vw, t), F32)],
        compiler_params=_cp(dimension_semantics=("arbitrary", "arbitrary", "arbitrary")),
    )(qf, kf, vt)


def _flash_bwd_dq(name, qf, kf, kft, v, do, o, lse, cq, sq, heads):
    s = qf.shape[0]
    t = _tile(s, ATT_TILE)
    n = s // t
    g = min(ATT_HEADS, heads)
    qw, vw = HEAD_QK_PAD, HEAD_V

    def body(q_ref, k_ref, kt_ref, v_ref, do_ref, o_ref, lse_ref, c_ref, s_ref, dq_ref, dl_ref, acc_sc):
        i, j = pl.program_id(1), pl.program_id(2)

        @pl.when(j == 0)
        def _():
            acc_sc[...] = jnp.zeros_like(acc_sc)
            for hh in range(g):
                cols = slice(hh * vw, (hh + 1) * vw)
                col = jnp.sum(do_ref[:, cols].astype(F32) * o_ref[:, cols].astype(F32), axis=1, keepdims=True)
                dl_ref[hh] = jnp.broadcast_to(col, (t, LANE)).T[0:1, :]

        def step(diag):
            for hh in range(g):
                sc = lax.dot_general(k_ref[:, hh * qw:(hh + 1) * qw], q_ref[:, hh * qw:(hh + 1) * qw], _NT,
                                     preferred_element_type=F32)
                p = jnp.exp(sc - lse_ref[hh])
                if diag:
                    p = jnp.where(_chunk_mask_t(t), p, 0.0)
                dp = lax.dot_general(v_ref[:, hh * vw:(hh + 1) * vw], do_ref[:, hh * vw:(hh + 1) * vw], _NT,
                                     preferred_element_type=F32)
                ds = (p * (dp - dl_ref[hh])).astype(BF16)
                acc_sc[hh] += jnp.dot(kt_ref[hh * qw:(hh + 1) * qw, :], ds, preferred_element_type=F32)

        @pl.when(j < i)
        def _():
            step(False)

        @pl.when(j == i)
        def _():
            step(True)
            for hh in range(g):
                dq_ref[:, hh * qw:(hh + 1) * qw] = _rope_t(acc_sc[hh].T, c_ref[...], s_ref[...]).astype(BF16)

    qspec = pl.BlockSpec((t, g * qw), lambda h, i, j: (i, h))
    ospec = pl.BlockSpec((t, g * vw), lambda h, i, j: (i, h))
    vspec = pl.BlockSpec((g, 1, t), lambda h, i, j: (h, 0, i))
    tab = pl.BlockSpec((t, qw), lambda h, i, j: (i, 0))
    return pl.pallas_call(
        body, name=name, grid=(heads // g, n, n),
        in_specs=[qspec,
                  pl.BlockSpec((t, g * qw), lambda h, i, j: (jnp.minimum(j, i), h)),
                  pl.BlockSpec((g * qw, t), lambda h, i, j: (h, jnp.minimum(j, i))),
                  pl.BlockSpec((t, g * vw), lambda h, i, j: (jnp.minimum(j, i), h)),
                  ospec, ospec, vspec, tab, tab],
        out_specs=[qspec, vspec],
        out_shape=[jax.ShapeDtypeStruct(qf.shape, BF16), jax.ShapeDtypeStruct((heads, 1, s), F32)],
        scratch_shapes=[pltpu.VMEM((g, qw, t), F32)],
        compiler_params=_cp(dimension_semantics=("arbitrary", "arbitrary", "arbitrary")),
    )(qf, kf, kft, v, do, o, lse, cq, sq)


def _flash_bwd_dkv(name, qf, kf, v, do, lse, delta, ck, sk, heads):
    s = qf.shape[0]
    t = _tile(s, ATT_TILE)
    n = s // t
    g = min(ATT_HEADS, heads)
    qw, vw = HEAD_QK_PAD, HEAD_V
    lse_rows, delta_rows = lse, delta

    def body(q_ref, k_ref, v_ref, do_ref, lse_ref, dl_ref, c_ref, s_ref, dkv_ref, dpe_ref, dk_sc, dv_sc):
        j, h, i = pl.program_id(0), pl.program_id(1), pl.program_id(2)

        @pl.when(i == 0)
        def _():
            dk_sc[...] = jnp.zeros_like(dk_sc)
            dv_sc[...] = jnp.zeros_like(dv_sc)

        def step(diag):
            for hh in range(g):
                q = q_ref[:, hh * qw:(hh + 1) * qw]
                do = do_ref[:, hh * vw:(hh + 1) * vw]
                sc = lax.dot_general(k_ref[:, hh * qw:(hh + 1) * qw], q, _NT, preferred_element_type=F32)
                p = jnp.exp(sc - lse_ref[hh])
                if diag:
                    p = jnp.where(_chunk_mask_t(t), p, 0.0)
                dv_sc[hh] += jnp.dot(p.astype(BF16), do, preferred_element_type=F32)
                dp = lax.dot_general(v_ref[:, hh * vw:(hh + 1) * vw], do, _NT, preferred_element_type=F32)
                ds = (p * (dp - dl_ref[hh])).astype(BF16)
                dk_sc[hh] += jnp.dot(ds, q, preferred_element_type=F32)

        @pl.when(i > j)
        def _():
            step(False)

        @pl.when(i == j)
        def _():
            step(True)

        @pl.when(i == n - 1)
        def _():
            pe = None
            for hh in range(g):
                dk = dk_sc[hh]
                dkv_ref[:, hh * qw:(hh + 1) * qw] = jnp.concatenate([dk[:, 0:HEAD_NOPE], dv_sc[hh]],
                                                                     axis=1).astype(BF16)
                part = dk[:, HEAD_NOPE:HEAD_QK_PAD]
                pe = part if pe is None else pe + part

            @pl.when(h == 0)
            def _():
                dpe_ref[...] = pe

            @pl.when(h > 0)
            def _():
                dpe_ref[...] += pe

            @pl.when(h == heads // g - 1)
            def _():
                dpe_ref[...] = _rope_t(dpe_ref[...], c_ref[...], s_ref[...])

    qrow = lambda j, h, i: (jnp.maximum(i, j), h)
    vrow = lambda j, h, i: (h, 0, jnp.maximum(i, j))
    return pl.pallas_call(
        body, name=name, grid=(n, heads // g, n),
        in_specs=[pl.BlockSpec((t, g * qw), qrow),
                  pl.BlockSpec((t, g * qw), lambda j, h, i: (j, h)),
                  pl.BlockSpec((t, g * vw), lambda j, h, i: (j, h)),
                  pl.BlockSpec((t, g * vw), qrow),
                  pl.BlockSpec((g, 1, t), vrow),
                  pl.BlockSpec((g, 1, t), vrow),
                  pl.BlockSpec((t, LANE), lambda j, h, i: (j, 0)),
                  pl.BlockSpec((t, LANE), lambda j, h, i: (j, 0))],
        out_specs=[pl.BlockSpec((t, g * (HEAD_NOPE + HEAD_V)), lambda j, h, i: (j, h)),
                   pl.BlockSpec((t, LANE), lambda j, h, i: (j, 0))],
        out_shape=[jax.ShapeDtypeStruct((s, heads * (HEAD_NOPE + HEAD_V)), BF16),
                   jax.ShapeDtypeStruct((s, LANE), F32)],
        scratch_shapes=[pltpu.VMEM((g, t, qw), F32), pltpu.VMEM((g, t, vw), F32)],
        compiler_params=_cp(dimension_semantics=("arbitrary", "arbitrary", "arbitrary")),
    )(qf, kf, v, do, lse_rows, delta_rows, ck, sk)


def _flash_bwd(name, qf, kf, kft, v, do, o, lse, cq, sq, ck, sk, heads):
    s = qf.shape[0]
    t = _tile(s, ATT_TILE)
    n = s // t
    g = min(ATT_HEADS, heads)
    ng = heads // g
    qw, vw = HEAD_QK_PAD, HEAD_V

    def body(q_ref, k_ref, kt_ref, v_ref, do_ref, o_ref, lse_ref, cq_ref, sq_ref, ck_ref, sk_ref,
             dq_hbm, dkv_ref, dpe_hbm, dk_sc, dv_sc, dq_sc, dl_sc, pe_sc, dq_stage, pe_stage, sems):
        h, j, i = pl.program_id(0), pl.program_id(1), pl.program_id(2)
        cols = pl.ds(pl.multiple_of(i * t, t), t)
        rows = pl.ds(pl.multiple_of(j * t, t), t)

        def store(stage, dst, sem):
            cp = pltpu.make_async_copy(stage, dst, sem)
            cp.start()
            cp.wait()

        @pl.when((h == 0) & (j == 0) & (i == 0))
        def _():
            pe_sc[...] = jnp.zeros_like(pe_sc)

        @pl.when(j == 0)
        def _():
            for hh in range(g):
                dq_sc[hh, :, cols] = jnp.zeros((qw, t), F32)
                hv = slice(hh * vw, (hh + 1) * vw)
                col = jnp.sum(do_ref[:, hv].astype(F32) * o_ref[:, hv].astype(F32), axis=1, keepdims=True)
                dl_sc[hh, :, cols] = jnp.broadcast_to(col, (t, LANE)).T[0:1, :]

        @pl.when(i == 0)
        def _():
            dk_sc[...] = jnp.zeros_like(dk_sc)
            dv_sc[...] = jnp.zeros_like(dv_sc)

        def step(diag):
            for hh in range(g):
                q = q_ref[:, hh * qw:(hh + 1) * qw]
                dout = do_ref[:, hh * vw:(hh + 1) * vw]
                sc = lax.dot_general(k_ref[:, hh * qw:(hh + 1) * qw], q, _NT, preferred_element_type=F32)
                p = jnp.exp(sc - lse_ref[hh])
                if diag:
                    p = jnp.where(_chunk_mask_t(t), p, 0.0)
                dv_sc[hh] += jnp.dot(p.astype(BF16), dout, preferred_element_type=F32)
                dp = lax.dot_general(v_ref[:, hh * vw:(hh + 1) * vw], dout, _NT, preferred_element_type=F32)
                ds = (p * (dp - dl_sc[hh, :, cols])).astype(BF16)
                dk_sc[hh] += jnp.dot(ds, q, preferred_element_type=F32)
                dq_sc[hh, :, cols] += jnp.dot(kt_ref[hh * qw:(hh + 1) * qw, :], ds, preferred_element_type=F32)

        @pl.when(i > j)
        def _():
            step(False)

        @pl.when(i == j)
        def _():
            step(True)
            for hh in range(g):
                dq_stage[:, hh * qw:(hh + 1) * qw] = _rope_t(dq_sc[hh, :, cols].T, cq_ref[...],
                                                             sq_ref[...]).astype(BF16)
            store(dq_stage, dq_hbm.at[cols, pl.ds(pl.multiple_of(h * (g * qw), g * qw), g * qw)], sems.at[0])

        @pl.when(i == n - 1)
        def _():
            pe = None
            for hh in range(g):
                dk = dk_sc[hh]
                dkv_ref[:, hh * qw:(hh + 1) * qw] = jnp.concatenate([dk[:, 0:HEAD_NOPE], dv_sc[hh]],
                                                                     axis=1).astype(BF16)
                part = dk[:, HEAD_NOPE:HEAD_QK_PAD]
                pe = part if pe is None else pe + part
            pe_sc[rows, :] += pe

            @pl.when(h == ng - 1)
            def _():
                pe_stage[...] = _rope_t(pe_sc[rows, :], ck_ref[...], sk_ref[...])
                store(pe_stage, dpe_hbm.at[rows, :], sems.at[1])

    qrow = lambda h, j, i: (jnp.maximum(i, j), h)
    krow = lambda h, j, i: (j, h)
    return pl.pallas_call(
        body, name=name, grid=(ng, n, n),
        in_specs=[pl.BlockSpec((t, g * qw), qrow),
                  pl.BlockSpec((t, g * qw), krow),
                  pl.BlockSpec((g * qw, t), lambda h, j, i: (h, j)),
                  pl.BlockSpec((t, g * vw), krow),
                  pl.BlockSpec((t, g * vw), qrow),
                  pl.BlockSpec((t, g * vw), lambda h, j, i: (jnp.where(j == 0, i, n - 1), h)),
                  pl.BlockSpec((g, 1, t), lambda h, j, i: (h, 0, jnp.maximum(i, j))),
                  pl.BlockSpec((t, qw), lambda h, j, i: (jnp.maximum(i, j), 0)),
                  pl.BlockSpec((t, qw), lambda h, j, i: (jnp.maximum(i, j), 0)),
                  pl.BlockSpec((t, LANE), lambda h, j, i: (j, 0)),
                  pl.BlockSpec((t, LANE), lambda h, j, i: (j, 0))],
        out_specs=[ANY, pl.BlockSpec((t, g * qw), krow), ANY],
        out_shape=[jax.ShapeDtypeStruct(qf.shape, BF16),
                   jax.ShapeDtypeStruct((s, heads * (HEAD_NOPE + HEAD_V)), BF16),
                   jax.ShapeDtypeStruct((s, LANE), F32)],
        scratch_shapes=[pltpu.VMEM((g, t, qw), F32), pltpu.VMEM((g, t, vw), F32),
                        pltpu.VMEM((g, qw, s), F32), pltpu.VMEM((g, 1, s), F32), pltpu.VMEM((s, LANE), F32),
                        pltpu.VMEM((t, g * qw), BF16), pltpu.VMEM((t, LANE), F32),
                        pltpu.SemaphoreType.DMA((2,))],
        compiler_params=_cp(dimension_semantics=("arbitrary", "arbitrary", "arbitrary")),
    )(qf, kf, kft, v, do, o, lse, cq, sq, ck, sk)


def _adamw(name, parts, w, m, v):
    p, r, c = parts.shape
    tr = _tile(r, max(8, (256 * 1024) // max(c, 1)))
    bc1 = 1.0 - ADAM_B1 ** ADAM_STEP
    bc2 = 1.0 - ADAM_B2 ** ADAM_STEP

    def body(p_ref, w_ref, m_ref, v_ref, g_ref, d_ref, nm_ref, nv_ref):
        g = p_ref[0].astype(F32)
        for q in range(1, p):
            g = g + p_ref[q].astype(F32)
        nm = ADAM_B1 * m_ref[...] + (1.0 - ADAM_B1) * g
        nv = ADAM_B2 * v_ref[...] + (1.0 - ADAM_B2) * (g * g)
        g_ref[...] = g
        nm_ref[...] = nm
        nv_ref[...] = nv
        d_ref[...] = -ADAM_LR * ((nm / bc1) / (jnp.sqrt(nv / bc2) + ADAM_EPS) + ADAM_WD * w_ref[...])

    blk = pl.BlockSpec((tr, c), lambda i: (i, 0))
    sh = jax.ShapeDtypeStruct((r, c), F32)
    return pl.pallas_call(
        body, name=name, grid=(r // tr,),
        in_specs=[pl.BlockSpec((p, tr, c), lambda i: (0, i, 0)), blk, blk, blk],
        out_specs=[blk] * 4, out_shape=[sh] * 4,
        compiler_params=_cp(),
    )(parts, w, m, v)


def _my_place():
    x, y, c = lax.axis_index("x"), lax.axis_index("y"), lax.axis_index("c")
    return x, y, c


def _flip(v, bit):
    return 1 - v if bit else v


def _block(ref, axis, idx, size):
    return ref.at[(slice(None),) * axis + (pl.ds(idx * size, size),)]


HBM_SPEC = pl.BlockSpec(memory_space=pltpu.HBM)
SEM_SPEC = pl.BlockSpec(memory_space=pltpu.SEMAPHORE)
DATAFLOW = pltpu.SideEffectType.DATAFLOW_SIDE_EFFECTING


def _hbm(a):
    return pltpu.with_memory_space_constraint(a, pltpu.HBM)


def _remote_copies(jobs, bufs, send_sems, recv_sems):
    return [pltpu.make_async_remote_copy(src_ref=src, dst_ref=dst, send_sem=send_sems.at[q],
                                         recv_sem=recv_sems.at[q], device_id=dev, device_id_type=MESH)
            for q, (src, dst, dev) in enumerate(jobs(bufs))]


def _split_start(name, bufs, jobs, n_jobs, after):
    nb = len(bufs)

    def body(*refs):
        send_sems, recv_sems = refs[nb + 1], refs[nb + 2]
        for cp in _remote_copies(jobs, refs[:nb], send_sems, recv_sems):
            cp.start()
        refs[-1][...] = jnp.zeros_like(refs[-1])

    outs = pl.pallas_call(
        body, name=name,
        out_shape=(pltpu.SemaphoreType.DMA((n_jobs,)), pltpu.SemaphoreType.DMA((n_jobs,)),
                   *[pltpu.HBM(b.shape, b.dtype) for b in bufs], jax.ShapeDtypeStruct((8, LANE), F32)),
        in_specs=[HBM_SPEC] * nb + [ANY],
        out_specs=(SEM_SPEC, SEM_SPEC, *[HBM_SPEC] * nb, VMEM_SPEC),
        input_output_aliases={q: 2 + q for q in range(nb)},
        compiler_params=pltpu.CompilerParams(has_side_effects=DATAFLOW),
    )(*[_hbm(b) for b in bufs], after)
    return outs[0], outs[1], list(outs[2:2 + nb]), outs[-1]


def _split_wait(name, bufs, send_sems, recv_sems, jobs, after):
    nb = len(bufs)

    def body(*refs):
        for cp in _remote_copies(jobs, refs[:nb], refs[nb], refs[nb + 1]):
            cp.wait_send()
            cp.wait_recv()

    outs = pl.pallas_call(
        body, name=name,
        out_shape=tuple(pltpu.HBM(b.shape, b.dtype) for b in bufs),
        in_specs=[HBM_SPEC] * nb + [SEM_SPEC, SEM_SPEC, ANY],
        out_specs=tuple([HBM_SPEC] * nb),
        input_output_aliases={q: q for q in range(nb)},
        compiler_params=pltpu.CompilerParams(has_side_effects=DATAFLOW),
    )(*bufs, send_sems, recv_sems, after)
    return list(outs)


PLACE_TILE_BYTES = 2 * 1024 * 1024


def _own_block_spec(tr, c, nblk, axis):
    if axis == 0:
        return pl.BlockSpec((tr, c), lambda i, me: (me[0] * nblk + i, 0))
    return pl.BlockSpec((tr, c), lambda i, me: (i, me[0]))


def _cast_place(name, w, layer, axis, me):
    _, r, c = w.shape
    tr = _tile(r, max(SUBLANE_BF16, PLACE_TILE_BYTES // (4 * c)))
    nblk = r // tr
    full = (N_DEV * r, c) if axis == 0 else (r, N_DEV * c)

    def body(me_ref, w_ref, o_ref):
        o_ref[...] = w_ref[...].astype(BF16)

    return pl.pallas_call(
        body, name=name,
        grid_spec=pltpu.PrefetchScalarGridSpec(
            num_scalar_prefetch=1, grid=(nblk,),
            in_specs=[pl.BlockSpec((None, tr, c), lambda i, me: (layer, i, 0))],
            out_specs=_own_block_spec(tr, c, nblk, axis)),
        out_shape=jax.ShapeDtypeStruct(full, BF16), compiler_params=_cp(),
    )(me, w)


def _own_place(name, grad, land, layer, axis, me):
    _, _, r, c = land.shape
    tr = _tile(r, max(SUBLANE_BF16, PLACE_TILE_BYTES // (2 * c)))
    nblk = r // tr

    def body(me_ref, g_ref, land_ref, o_ref):
        o_ref[...] = g_ref[...]

    return pl.pallas_call(
        body, name=name,
        grid_spec=pltpu.PrefetchScalarGridSpec(
            num_scalar_prefetch=1, grid=(nblk,),
            in_specs=[_own_block_spec(tr, c, nblk, axis), ANY],
            out_specs=pl.BlockSpec((None, None, tr, c), lambda i, me: (0, layer, i, 0))),
        out_shape=jax.ShapeDtypeStruct(land.shape, land.dtype),
        input_output_aliases={2: 0}, compiler_params=_cp(),
    )(me, grad, land)


def _gather_jobs_a(axes, sizes):
    def jobs(bufs):
        x, y, c = _my_place()
        out = []
        for t, buf in enumerate(bufs):
            blk = _block(buf, axes[t], 4 * x + 2 * y + c, sizes[t])
            for dev in [(x, y, 1 - c), (1 - x, y, c), (x, 1 - y, c), (1 - x, 1 - y, c)]:
                out.append((blk, blk, dev))
        return out
    return jobs


def _gather_jobs_b(axes, sizes):
    nt = len(axes)

    def jobs(bufs):
        x, y, c = _my_place()
        out = []
        for t in range(nt):
            for px, py in [(1 - x, y), (x, 1 - y), (1 - x, 1 - y)]:
                blk = _block(bufs[t], axes[t], 4 * px + 2 * py + c, sizes[t])
                out.append((blk, blk, (x, y, 1 - c)))
        return out
    return jobs


def _exchange_jobs(axes, sizes, layers):
    nt = len(axes)

    def jobs(bufs):
        x, y, c = _my_place()
        out = []
        for k in range(1, N_DEV):
            px, py, pc = _flip(x, k & 4), _flip(y, k & 2), _flip(c, k & 1)
            for t in range(nt):
                out.append((_block(bufs[t], axes[t], 4 * px + 2 * py + pc, sizes[t]),
                            bufs[nt + t].at[k, layers[t]], (px, py, pc)))
        return out
    return jobs


def _gather_begin(name, lands, axes, after):
    sizes = [b.shape[ax] // N_DEV for b, ax in zip(lands, axes)]
    jobs = _gather_jobs_a(axes, sizes)
    send, recv, bufs, token = _split_start(name + "_a", lands, jobs, 4 * len(lands), after)
    return dict(name=name, axes=axes, sizes=sizes, send=send, recv=recv, bufs=bufs, jobs=jobs), token


def _gather_mid(h, after):
    bufs = _split_wait(h["name"] + "_aw", h["bufs"], h["send"], h["recv"], h["jobs"], after)
    jobs = _gather_jobs_b(h["axes"], h["sizes"])
    send, recv, lands, token = _split_start(h["name"] + "_b", bufs, jobs, 3 * len(bufs), after)
    return dict(h, send=send, recv=recv, bufs=lands, jobs=jobs), token


def _gather_end(h, after):
    return _split_wait(h["name"] + "_bw", h["bufs"], h["send"], h["recv"], h["jobs"], after)


def _exchange_begin(name, grads, axes, lands, layers, me, after):
    sizes = [g.shape[ax] // N_DEV for g, ax in zip(grads, axes)]
    lands = [_own_place(f"{name}_place{t}", grads[t], lands[t], layers[t], axes[t], me)
             for t in range(len(grads))]
    jobs = _exchange_jobs(axes, sizes, layers)
    send, recv, bufs, token = _split_start(name + "_s", list(grads) + lands, jobs, 7 * len(grads), after)
    return dict(name=name, n=len(grads), send=send, recv=recv, bufs=bufs, jobs=jobs), token


def _exchange_end(h, after):
    bufs = _split_wait(h["name"] + "_w", h["bufs"], h["send"], h["recv"], h["jobs"], after)
    return bufs[h["n"]:]


def _all_gather_small(name, vec, reduce):
    r = vec.shape[0]

    def body(v_ref, o_ref, *rest):
        if reduce:
            buf, send_sems, recv_sems = rest
        else:
            buf = o_ref
            send_sems, recv_sems = rest
        x, y, c = _my_place()
        mine = 4 * x + 2 * y + c
        buf[mine] = v_ref[...]
        copies = []
        for k in range(1, N_DEV):
            px, py, pc = _flip(x, k & 4), _flip(y, k & 2), _flip(c, k & 1)
            cp = pltpu.make_async_remote_copy(
                src_ref=v_ref, dst_ref=buf.at[mine], send_sem=send_sems.at[k - 1],
                recv_sem=recv_sems.at[k - 1], device_id=(px, py, pc), device_id_type=MESH)
            cp.start()
            copies.append(cp)
        for cp in copies:
            cp.wait()
        if reduce:
            acc = buf[0]
            for q in range(1, N_DEV):
                acc = acc + buf[q]
            o_ref[...] = acc

    scratch = [pltpu.SemaphoreType.DMA((N_DEV - 1,)), pltpu.SemaphoreType.DMA((N_DEV - 1,))]
    if reduce:
        scratch = [pltpu.VMEM((N_DEV, r, LANE), F32)] + scratch
        out_shape = jax.ShapeDtypeStruct((r, LANE), F32)
    else:
        out_shape = jax.ShapeDtypeStruct((N_DEV, r, LANE), F32)
    return pl.pallas_call(
        body, name=name, in_specs=[VMEM_SPEC], out_specs=VMEM_SPEC, out_shape=out_shape,
        scratch_shapes=scratch, compiler_params=_cp(has_side_effects=True),
    )(vec)


def _pack(arrs, row_mult=8):
    flat = jnp.concatenate([a.reshape(-1).astype(F32) for a in arrs])
    n = flat.shape[0]
    rows = -(-n // LANE)
    rows = -(-rows // row_mult) * row_mult
    return jnp.pad(flat, (0, rows * LANE - n)).reshape(rows, LANE)


def _unpack(vec, shapes):
    flat = vec.reshape(-1)
    out, pos = [], 0
    for sh in shapes:
        n = 1
        for s in sh:
            n *= s
        out.append(flat[pos:pos + n].reshape(sh))
        pos += n
    return out


BIG = ["conv_w_pw1", "conv_w_pw2", "mla_w_in", "mla_w_q_up", "mla_w_kv_up", "mla_w_o", "mlp_w1", "mlp_w2"]
BIG_AXIS = {"conv_w_pw1": 2, "conv_w_pw2": 1, "mla_w_in": 1, "mla_w_q_up": 2, "mla_w_kv_up": 2,
            "mla_w_o": 1, "mlp_w1": 2, "mlp_w2": 1}
SMALL_SHARDED = ["conv_w_dw", "mla_q_norm_g", "mla_kv_norm_g"]
REPLICATED = ["norm_mixer_g", "norm_mlp_g", "conv_b_pw1", "conv_b_dw", "conv_ln_g", "conv_ln_b",
              "conv_b_pw2", "final_norm_g"]
WEIGHTS = ["norm_mixer_g", "norm_mlp_g", "conv_w_pw1", "conv_b_pw1", "conv_w_dw", "conv_b_dw",
           "conv_ln_g", "conv_ln_b", "conv_w_pw2", "conv_b_pw2", "mla_w_in", "mla_q_norm_g",
           "mla_kv_norm_g", "mla_w_q_up", "mla_w_kv_up", "mla_w_o", "mlp_w1", "mlp_w2", "final_norm_g"]


def _unshard_last(g, lead):
    nd = g.ndim
    perm = tuple(range(1, nd - 1)) + (0, nd - 1)
    return g.transpose(perm).reshape(lead + (N_DEV * g.shape[-1],))


def _step(w, m, v, x, positions, target):
    s, d = x.shape
    depth = w["norm_mixer_g"].shape[0]
    n_conv, n_mla = w["conv_w_pw1"].shape[0], w["mla_w_in"].shape[0]
    heads = (w["mla_w_q_up"].shape[-1] * N_DEV) // (HEAD_NOPE + HEAD_ROPE)
    rq, rkv = w["mla_w_q_up"].shape[1], w["mla_w_kv_up"].shape[1]
    xi, yi, ci = _my_place()
    mine = 4 * xi + 2 * yi + ci

    def mixer_units(layer):
        names = (["conv_w_pw1", "conv_w_pw2"] if layer % 2 == 0
                 else ["mla_w_in", "mla_w_q_up", "mla_w_kv_up", "mla_w_o"])
        return [(n, layer // 2) for n in names]

    def mlp_units(layer):
        return [("mlp_w1", layer), ("mlp_w2", layer)]

    me_arr = mine.astype(jnp.int32).reshape(1)

    def gather_begin(tag, units, after):
        lands = [_cast_place(f"{tag}_place_{n}", w[n], jl, BIG_AXIS[n] - 1, me_arr) for n, jl in units]
        h, token = _gather_begin(tag, lands, [BIG_AXIS[n] - 1 for n, _ in units], after)
        return dict(h, units=units), token

    full = {}

    def gather_end(h, after):
        full.update(zip(h["units"], _gather_end(h, after)))

    small_shapes = [w[n].shape for n in SMALL_SHARDED]
    gathered = _all_gather_small("gather_small", _pack([w[n] for n in SMALL_SHARDED]), False)

    first_a, tok = gather_begin("gather_0a", mixer_units(0), gathered)
    first_b, tok = gather_begin("gather_0b", mlp_units(0), tok)
    pending = {}
    if depth > 1:
        pending[1], tok = gather_begin("gather_1", mixer_units(1) + mlp_units(1), tok)
    first_a, tok = _gather_mid(first_a, tok)
    gather_end(first_a, tok)

    per_dev = [_unpack(gathered[q], small_shapes) for q in range(N_DEV)]
    w_dw = _unshard_last(jnp.stack([p[0] for p in per_dev]), (n_conv, CONV_W))
    q_gain = _unshard_last(jnp.stack([p[1] for p in per_dev]), (n_mla,))
    kv_gain = _unshard_last(jnp.stack([p[2] for p in per_dev]), (n_mla,))
    w_dw_pad = jnp.pad(w_dw, ((0, 0), (0, HALO - CONV_W), (0, 0)))

    w_in_cols = rq + rkv + HEAD_ROPE

    def pad_w_in(a):
        return jnp.pad(a, ((0, 0), (0, rq + rkv + LANE - w_in_cols)))

    def pad_wq(a):
        return jnp.pad(a.reshape(rq, heads, HEAD_NOPE + HEAD_ROPE),
                       ((0, 0), (0, 0), (0, HEAD_QK_PAD - HEAD_NOPE - HEAD_ROPE))).reshape(rq, heads * HEAD_QK_PAD)

    inv_freq = ROPE_THETA ** (-jnp.arange(0, HEAD_ROPE, 2, dtype=F32) / HEAD_ROPE)
    ang = positions.reshape(s).astype(F32)[:, None] * inv_freq
    cos, sin = jnp.cos(ang), jnp.sin(ang)
    c64 = jnp.concatenate([cos, cos], axis=1)
    s64 = jnp.concatenate([-sin, sin], axis=1)
    zeros64 = jnp.zeros((s, LANE - HEAD_ROPE), F32)
    ck = jnp.concatenate([c64, zeros64], axis=1)
    sk = jnp.concatenate([s64, zeros64], axis=1)
    scale = (HEAD_NOPE + HEAD_ROPE) ** -0.5
    cq = scale * jnp.concatenate([jnp.ones((s, HEAD_NOPE), F32), ck], axis=1)
    sq = scale * jnp.concatenate([jnp.zeros((s, HEAD_NOPE), F32), sk], axis=1)

    def vec(a):
        return a.reshape(1, -1)

    saved = []
    wpad = {}
    for layer in range(depth):
        jl = layer // 2
        h = _rms_fwd(f"rms_mixer_{layer}", x, vec(w["norm_mixer_g"][layer]) + tok[0, 0])
        if layer % 2 == 0:
            ua, ug, glu = _mm_glu(f"conv_pw1_{layer}", h, full["conv_w_pw1", jl], None, vec(w["conv_b_pw1"][jl]))
            cc, sw = _conv_fwd(f"conv_dw_{layer}", glu, w_dw_pad[jl], vec(w["conv_b_dw"][jl]),
                               vec(w["conv_ln_g"][jl]), vec(w["conv_ln_b"][jl]))
            x1 = _mm_res(f"conv_pw2_{layer}", sw, full["conv_w_pw2", jl], None, x, vec(w["conv_b_pw2"][jl]))
            mix = (h, ua, ug, glu, cc, sw)
        else:
            wpad["in", jl] = pad_w_in(full["mla_w_in", jl])
            wpad["q", jl] = pad_wq(full["mla_w_q_up", jl])
            down = _mm_plain(f"mla_down_{layer}", h, wpad["in", jl], None, "nn", F32)
            qn, kvn, kpe = _mla_mid_fwd(f"mla_mid_{layer}", down, vec(q_gain[jl]), vec(kv_gain[jl]), ck, sk)
            qf = _mm_q(f"mla_q_{layer}", qn, wpad["q", jl], None, cq, sq)
            kf, vv, kft, vt = _mm_kv(f"mla_kv_{layer}", kvn, full["mla_w_kv_up", jl], None, kpe)
            o, lse = _flash_fwd(f"mla_attn_{layer}", qf, kf, vt, heads)
            x1 = _mm_res(f"mla_out_{layer}", o, full["mla_w_o", jl], None, x)
            mix = (h, down, qn, kvn, qf, kf, kft, vv, o, lse)
        anchor = x1
        if layer == 0:
            first_b, anchor = _gather_mid(first_b, anchor)
        if layer + 2 < depth:
            pending[layer + 2], anchor = gather_begin(f"gather_{layer + 2}",
                                                      mixer_units(layer + 2) + mlp_units(layer + 2), anchor)
        if layer == 0:
            gather_end(first_b, anchor)
        elif layer + 1 < depth:
            pending[layer + 1], anchor = _gather_mid(pending[layer + 1], anchor)
        if anchor is not x1:
            tok = anchor
        h2 = _rms_fwd(f"rms_mlp_{layer}", x1, vec(w["norm_mlp_g"][layer]) + tok[0, 0])
        z, a = _mm_mlp_up(f"mlp_up_{layer}", h2, full["mlp_w1", layer], None)
        x2 = _mm_res(f"mlp_down_{layer}", a, full["mlp_w2", layer], None, x1)
        if layer + 1 < depth:
            if layer == 0:
                pending[1], tok = _gather_mid(pending[1], x2)
                gather_end(pending[1], tok)
            else:
                gather_end(pending[layer + 1], x2)
        saved.append((x, mix, x1, h2, z, a))
        x = x2

    loss_row, g, gb, d_final, _ = _final_loss("final_loss", x, vec(w["final_norm_g"]) + tok[0, 0], target)

    recv = {n: lax.empty((N_DEV,) + w[n].shape, BF16) for n in BIG}

    def exchange_begin(tag, items, after):
        names = [n for n, _, _ in items]
        h, token = _exchange_begin(tag, [gr for _, _, gr in items], [BIG_AXIS[n] - 1 for n in names],
                                   [recv[n] for n in names], [jl for _, jl, _ in items], me_arr, after)
        return dict(h, names=names), token

    def exchange_end(h, after):
        recv.update(zip(h["names"], _exchange_end(h, after)))

    mix_exchanges = []
    d_mixer, d_mlp = [None] * depth, [None] * depth
    d_small = {n: [None] * n_conv for n in ["conv_b_pw1", "conv_w_dw", "conv_b_dw", "conv_ln_g",
                                           "conv_ln_b", "conv_b_pw2"]}
    d_qg, d_kvg = [None] * n_mla, [None] * n_mla
    for layer in reversed(range(depth)):
        jl = layer // 2
        x0, mix, x1, h2, z, a = saved[layer]
        colsum_g = None
        dz = _mm_mlp_dz(f"mlp_dz_{layer}", gb, full["mlp_w2", layer], None, z)
        dw2 = _mm_wgrad(f"mlp_dw2_{layer}", a, gb)
        w2_exchange, tok = exchange_begin(f"exchange_w2_{layer}", [("mlp_w2", layer, dw2)], dz)
        tok, dz = lax.optimization_barrier((tok, dz))
        dh2 = _mm_plain(f"mlp_dh_{layer}", dz, full["mlp_w1", layer], None, "nt", F32)
        dw1 = _mm_wgrad(f"mlp_dw1_{layer}", h2, dz)
        w1_exchange, tok = exchange_begin(f"exchange_w1_{layer}", [("mlp_w1", layer, dw1)], tok)
        g, gb, d_mlp[layer], colsum_g = _rms_bwd(f"rms_mlp_bwd_{layer}", x1,
                                                 vec(w["norm_mlp_g"][layer]) + tok[0, 0], dh2, g)
        for hx in mix_exchanges:
            exchange_end(hx, g)
        if layer % 2 == 0:
            h, ua, ug, glu, cc, sw = mix
            d_small["conv_b_pw2"][jl] = colsum_g.reshape(-1)
            dsw = _mm_plain(f"conv_ds_{layer}", gb, full["conv_w_pw2", jl], None, "nt", F32)
            dwp2 = _mm_wgrad(f"conv_dw2_{layer}", sw, gb)
            hx2, tok = exchange_begin(f"exchange_pw2_{layer}", [("conv_w_pw2", jl, dwp2)], dsw)
            dc, dlg, dlb, dbdw = _conv_bwd_ln(f"conv_ln_bwd_{layer}", dsw, cc,
                                              vec(w["conv_ln_g"][jl]) + tok[0, 0], vec(w["conv_ln_b"][jl]))
            du, dwdw, dbu = _conv_bwd_dw(f"conv_dw_bwd_{layer}", dc, glu, ua, ug, w_dw_pad[jl])
            d_small["conv_ln_g"][jl] = dlg.reshape(-1)
            d_small["conv_ln_b"][jl] = dlb.reshape(-1)
            d_small["conv_b_dw"][jl] = dbdw.reshape(-1)
            d_small["conv_w_dw"][jl] = dwdw[:CONV_W]
            d_small["conv_b_pw1"][jl] = dbu.reshape(-1)
            dwp1 = _mm_wgrad(f"conv_dw1_{layer}", h, du)
            hx1, tok = exchange_begin(f"exchange_pw1_{layer}", [("conv_w_pw1", jl, dwp1)], dbu)
            tok, du = lax.optimization_barrier((tok, du))
            dh = _mm_plain(f"conv_dh_{layer}", du, full["conv_w_pw1", jl], None, "nt", F32)
            mix_exchanges = [hx2, hx1]
        else:
            h, down, qn, kvn, qf, kf, kft, vv, o, lse = mix
            do = _mm_plain(f"mla_do_{layer}", gb, full["mla_w_o", jl], None, "nt", BF16)
            dwo = _mm_wgrad(f"mla_dwo_{layer}", o, gb)
            dq, dkv, dkpe = _flash_bwd(f"mla_attn_bwd_{layer}", qf, kf, kft, vv, do, o, lse, cq, sq, ck, sk, heads)
            dqn = _mm_plain(f"mla_dqn_{layer}", dq, wpad["q", jl], None, "nt", F32)
            dwq = _mm_wgrad(f"mla_dwq_{layer}", qn, dq).reshape(rq, heads, HEAD_QK_PAD)[
                :, :, :HEAD_NOPE + HEAD_ROPE].reshape(rq, heads * (HEAD_NOPE + HEAD_ROPE))
            dkvn = _mm_plain(f"mla_dkvn_{layer}", dkv, full["mla_w_kv_up", jl], None, "nt", F32)
            dwkv = _mm_wgrad(f"mla_dwkv_{layer}", kvn, dkv)
            ddown, d_qg[jl], d_kvg[jl] = _mla_mid_bwd(f"mla_mid_bwd_{layer}", down, vec(q_gain[jl]),
                                                      vec(kv_gain[jl]), dqn, dkvn, dkpe)
            dh = _mm_plain(f"mla_dh_{layer}", ddown, wpad["in", jl], None, "nt", F32)
            dwin = _mm_wgrad(f"mla_dwin_{layer}", h, ddown)[:, :w_in_cols]
            items = [("mla_w_in", jl, dwin), ("mla_w_q_up", jl, dwq), ("mla_w_kv_up", jl, dwkv),
                     ("mla_w_o", jl, dwo)]
            hx, tok = exchange_begin(f"exchange_mix_{layer}", items, dh)
            mix_exchanges = [hx]
        g, gb, d_mixer[layer], _ = _rms_bwd(f"rms_mixer_bwd_{layer}", x0,
                                            vec(w["norm_mixer_g"][layer]) + tok[0, 0], dh, g)
        exchange_end(w2_exchange, g)
        exchange_end(w1_exchange, g)
    grad_x = g

    out = {}

    def adamw_big(n):
        sh = w[n].shape
        r, c = sh[0] * sh[1], sh[2]
        res = _adamw(f"adamw_{n}", recv[n].reshape(N_DEV, r, c), w[n].reshape(r, c),
                     m[n].reshape(r, c), v[n].reshape(r, c))
        out[n] = [t.reshape(sh) for t in res]

    late = [n for hx in mix_exchanges for n in hx["names"]]
    early = [n for n in BIG if n not in late]
    for n in early:
        adamw_big(n)
    anchor = out[early[-1]][1]
    for hx in mix_exchanges:
        exchange_end(hx, anchor)
    for n in late:
        adamw_big(n)

    small_full = {
        "norm_mixer_g": jnp.concatenate(d_mixer, axis=0), "norm_mlp_g": jnp.concatenate(d_mlp, axis=0),
        "conv_b_pw1": jnp.stack(d_small["conv_b_pw1"]), "conv_b_dw": jnp.stack(d_small["conv_b_dw"]),
        "conv_ln_g": jnp.stack(d_small["conv_ln_g"]), "conv_ln_b": jnp.stack(d_small["conv_ln_b"]),
        "conv_b_pw2": jnp.stack(d_small["conv_b_pw2"]), "final_norm_g": d_final.reshape(-1),
        "conv_w_dw": jnp.stack(d_small["conv_w_dw"]),
        "mla_q_norm_g": jnp.concatenate(d_qg, axis=0), "mla_kv_norm_g": jnp.concatenate(d_kvg, axis=0),
    }
    names = REPLICATED + SMALL_SHARDED
    packed, _ = lax.optimization_barrier((_pack([small_full[n] for n in names]), anchor))
    summed = _unpack(_all_gather_small("reduce_small", packed, True), [small_full[n].shape for n in names])
    summed = dict(zip(names, summed))
    for n in SMALL_SHARDED:
        width = w[n].shape[-1]
        summed[n] = lax.dynamic_slice_in_dim(summed[n], mine * width, width, axis=summed[n].ndim - 1)
    for group, tag in ((REPLICATED, "replicated"), (SMALL_SHARDED, "small_sharded")):
        shapes = [w[n].shape for n in group]
        res = _adamw(f"adamw_{tag}", _pack([summed[n] for n in group])[None],
                     _pack([w[n] for n in group]), _pack([m[n] for n in group]), _pack([v[n] for n in group]))
        unpacked = [_unpack(t, shapes) for t in res]
        for q, n in enumerate(group):
            out[n] = [unpacked[0][q], unpacked[1][q], unpacked[2][q], unpacked[3][q]]

    loss = lax.psum(loss_row[0, 0], ("x", "y", "c"))
    return loss, grad_x, out


def kernel(x, positions, norm_mixer_g, norm_mlp_g, conv_w_pw1, conv_b_pw1, conv_w_dw, conv_b_dw, conv_ln_g, conv_ln_b, conv_w_pw2, conv_b_pw2, mla_w_in, mla_q_norm_g, mla_kv_norm_g, mla_w_q_up, mla_w_kv_up, mla_w_o, mlp_w1, mlp_w2, final_norm_g, loss_target, m_norm_mixer_g, m_norm_mlp_g, m_conv_w_pw1, m_conv_b_pw1, m_conv_w_dw, m_conv_b_dw, m_conv_ln_g, m_conv_ln_b, m_conv_w_pw2, m_conv_b_pw2, m_mla_w_in, m_mla_q_norm_g, m_mla_kv_norm_g, m_mla_w_q_up, m_mla_w_kv_up, m_mla_w_o, m_mlp_w1, m_mlp_w2, m_final_norm_g, v_norm_mixer_g, v_norm_mlp_g, v_conv_w_pw1, v_conv_b_pw1, v_conv_w_dw, v_conv_b_dw, v_conv_ln_g, v_conv_ln_b, v_conv_w_pw2, v_conv_b_pw2, v_mla_w_in, v_mla_q_norm_g, v_mla_kv_norm_g, v_mla_w_q_up, v_mla_w_kv_up, v_mla_w_o, v_mlp_w1, v_mlp_w2, v_final_norm_g):
    ws = (norm_mixer_g, norm_mlp_g, conv_w_pw1, conv_b_pw1, conv_w_dw, conv_b_dw, conv_ln_g, conv_ln_b,
          conv_w_pw2, conv_b_pw2, mla_w_in, mla_q_norm_g, mla_kv_norm_g, mla_w_q_up, mla_w_kv_up, mla_w_o,
          mlp_w1, mlp_w2, final_norm_g)
    ms = (m_norm_mixer_g, m_norm_mlp_g, m_conv_w_pw1, m_conv_b_pw1, m_conv_w_dw, m_conv_b_dw, m_conv_ln_g,
          m_conv_ln_b, m_conv_w_pw2, m_conv_b_pw2, m_mla_w_in, m_mla_q_norm_g, m_mla_kv_norm_g,
          m_mla_w_q_up, m_mla_w_kv_up, m_mla_w_o, m_mlp_w1, m_mlp_w2, m_final_norm_g)
    vs = (v_norm_mixer_g, v_norm_mlp_g, v_conv_w_pw1, v_conv_b_pw1, v_conv_w_dw, v_conv_b_dw, v_conv_ln_g,
          v_conv_ln_b, v_conv_w_pw2, v_conv_b_pw2, v_mla_w_in, v_mla_q_norm_g, v_mla_kv_norm_g,
          v_mla_w_q_up, v_mla_w_kv_up, v_mla_w_o, v_mlp_w1, v_mlp_w2, v_final_norm_g)
    w, m, v = dict(zip(WEIGHTS, ws)), dict(zip(WEIGHTS, ms)), dict(zip(WEIGHTS, vs))
    s, d = x.shape[-2], x.shape[-1]
    loss, grad_x, out = _step(w, m, v, x.reshape(s, d), positions, loss_target.reshape(s, d))
    grads = [out[n][0] for n in WEIGHTS]
    deltas = [out[n][1] for n in WEIGHTS]
    new_m = [out[n][2] for n in WEIGHTS]
    new_v = [out[n][3] for n in WEIGHTS]
    return (loss, grad_x.reshape(x.shape), *grads, *deltas, *new_m, *new_v)
```

```python
import functools

import jax
import jax.numpy as jnp
from jax import lax
from jax.experimental import pallas as pl
from jax.experimental.pallas import tpu as pltpu

F32 = jnp.float32
BF16 = jnp.bfloat16

NORM_EPS = 1e-6
LN_EPS = 1e-5
ROPE_THETA = 10000.0
CHUNK_BITS = 6
HEAD_NOPE = 128
HEAD_ROPE = 64
HEAD_V = 128
HEAD_QK_PAD = 256
CONV_W = 31
HALO = 32
N_DEV = 8

ADAM_LR = 0.001
ADAM_B1 = 0.9
ADAM_B2 = 0.999
ADAM_EPS = 1e-08
ADAM_WD = 0.01
ADAM_STEP = 10

V7X_VMEM_BYTES = 64 * 1024 * 1024
VMEM_LIMIT = (V7X_VMEM_BYTES * 3) // 4
LANE = 128

MESH = pl.DeviceIdType.MESH
ANY = pl.BlockSpec(memory_space=pl.ANY)
VMEM_SPEC = pl.BlockSpec(memory_space=pltpu.VMEM)


def _cp(**kw):
    return pltpu.CompilerParams(vmem_limit_bytes=VMEM_LIMIT, **kw)


SUBLANE = 8
SUBLANE_BF16 = 16

TM_PREF = 1024
TN_PREF = 1024
TK_PREF = 2048


def _tile(n, pref, mult=SUBLANE_BF16):
    if n <= pref + pref // 2:
        return n
    t = (pref // mult) * mult
    while t >= mult:
        if n % t == 0:
            return t
        t -= mult
    return n


def _sigmoid(x):
    return 1.0 / (1.0 + jnp.exp(-x))


def _rot_half(x):
    n = x.shape[-1]
    lane = lax.broadcasted_iota(jnp.int32, x.shape, x.ndim - 1)
    first = (lane & 63) < 32
    return jnp.where(first, pltpu.roll(x, n - 32, x.ndim - 1), pltpu.roll(x, 32, x.ndim - 1))


def _rope(x, c, s):
    return x * c + _rot_half(x) * s


def _rope_t(d, c, s):
    return d * c + _rot_half(d * s)


def _chunk_mask_t(t):
    row = lax.broadcasted_iota(jnp.int32, (t, t), 0)
    col = lax.broadcasted_iota(jnp.int32, (t, t), 1)
    return jnp.right_shift(row, CHUNK_BITS) <= jnp.right_shift(col, CHUNK_BITS)


def _rms_fwd(name, x, g):
    t, d = x.shape
    tm = _tile(t, 512)

    def body(x_ref, g_ref, o_ref):
        xf = x_ref[...]
        r = lax.rsqrt(jnp.mean(xf * xf, axis=-1, keepdims=True) + NORM_EPS)
        o_ref[...] = (xf * r * g_ref[...]).astype(o_ref.dtype)

    return pl.pallas_call(
        body, name=name, grid=(t // tm,),
        in_specs=[pl.BlockSpec((tm, d), lambda i: (i, 0)), pl.BlockSpec((1, d), lambda i: (0, 0))],
        out_specs=pl.BlockSpec((tm, d), lambda i: (i, 0)),
        out_shape=jax.ShapeDtypeStruct((t, d), BF16),
        compiler_params=_cp(),
    )(x, g)


def _rms_bwd_math(xf, g, dy):
    r = lax.rsqrt(jnp.mean(xf * xf, axis=-1, keepdims=True) + NORM_EPS)
    xh = xf * r
    dg = jnp.sum(dy * xh, axis=0, keepdims=True)
    dxh = dy * g
    dx = r * (dxh - xh * jnp.mean(dxh * xh, axis=-1, keepdims=True))
    return dx, dg


def _rms_bwd(name, x, g, dy, resid):
    t, d = x.shape
    tm = _tile(t, 256)

    def body(x_ref, g_ref, dy_ref, r_ref, dx_ref, dxb_ref, dg_ref, cs_ref):
        @pl.when(pl.program_id(0) == 0)
        def _():
            dg_ref[...] = jnp.zeros_like(dg_ref)
            cs_ref[...] = jnp.zeros_like(cs_ref)

        dx, dg = _rms_bwd_math(x_ref[...], g_ref[...], dy_ref[...])
        tot = r_ref[...] + dx
        dx_ref[...] = tot
        dxb_ref[...] = tot.astype(BF16)
        dg_ref[...] += dg
        cs_ref[...] += jnp.sum(tot, axis=0, keepdims=True)

    row = pl.BlockSpec((tm, d), lambda i: (i, 0))
    vec = pl.BlockSpec((1, d), lambda i: (0, 0))
    return pl.pallas_call(
        body, name=name, grid=(t // tm,),
        in_specs=[row, vec, row, row],
        out_specs=[row, row, vec, vec],
        out_shape=[jax.ShapeDtypeStruct((t, d), F32), jax.ShapeDtypeStruct((t, d), BF16),
                   jax.ShapeDtypeStruct((1, d), F32), jax.ShapeDtypeStruct((1, d), F32)],
        compiler_params=_cp(dimension_semantics=("arbitrary",)),
    )(x, g, dy, resid)


def _final_loss(name, x, g, target):
    t, d = x.shape
    tm = _tile(t, 256)

    def body(x_ref, g_ref, t_ref, loss_ref, dx_ref, dxb_ref, dg_ref, cs_ref):
        @pl.when(pl.program_id(0) == 0)
        def _():
            loss_ref[...] = jnp.zeros_like(loss_ref)
            dg_ref[...] = jnp.zeros_like(dg_ref)
            cs_ref[...] = jnp.zeros_like(cs_ref)

        xf = x_ref[...]
        gg = g_ref[...]
        r = lax.rsqrt(jnp.mean(xf * xf, axis=-1, keepdims=True) + NORM_EPS)
        err = xf * r * gg - t_ref[...]
        part = 0.5 * jnp.sum(jnp.mean(err * err, axis=-1, keepdims=True), axis=0, keepdims=True)
        loss_ref[...] += jnp.broadcast_to(part, loss_ref.shape)
        dx, dg = _rms_bwd_math(xf, gg, err * (1.0 / d))
        dx_ref[...] = dx
        dxb_ref[...] = dx.astype(BF16)
        dg_ref[...] += dg
        cs_ref[...] += jnp.sum(dx, axis=0, keepdims=True)

    row = pl.BlockSpec((tm, d), lambda i: (i, 0))
    vec = pl.BlockSpec((1, d), lambda i: (0, 0))
    return pl.pallas_call(
        body, name=name, grid=(t // tm,),
        in_specs=[row, vec, row],
        out_specs=[pl.BlockSpec((1, LANE), lambda i: (0, 0)), row, row, vec, vec],
        out_shape=[jax.ShapeDtypeStruct((1, LANE), F32), jax.ShapeDtypeStruct((t, d), F32),
                   jax.ShapeDtypeStruct((t, d), BF16), jax.ShapeDtypeStruct((1, d), F32),
                   jax.ShapeDtypeStruct((1, d), F32)],
        compiler_params=_cp(dimension_semantics=("arbitrary",)),
    )(x, g, target)


_DIMS = {
    "nn": (((1,), (0,)), ((), ())),
    "nt": (((1,), (1,)), ((), ())),
    "tn": (((0,), (0,)), ((), ())),
}


def _mm(name, a, bs, *, mode, m, n, k, epilogue, out_shape, out_specs, extras=(), extra_specs=(),
        a_lead=None, aliases=None, tn_div=1):
    tm, tn, tk = _tiles(m, n, k, tn_div)
    nk = k // tk
    nb, ne = len(bs), len(extras)
    no = len(out_shape)
    dims = _DIMS[mode]

    def with_lead(shape, idx, lead):
        if lead is None:
            return pl.BlockSpec(shape, idx)
        return pl.BlockSpec((None,) + shape, lambda i, j, kk: (lead,) + idx(i, j, kk))

    if mode == "tn":
        a_spec = with_lead((tk, tm), lambda i, j, kk: (kk, i), a_lead)
    else:
        a_spec = with_lead((tm, tk), lambda i, j, kk: (i, kk), a_lead)
    b_specs = []
    for _, lead, off in bs:
        if mode == "nt":
            b_specs.append(with_lead((tn, tk), lambda i, j, kk, off=off: (j + off, kk), lead))
        else:
            b_specs.append(with_lead((tk, tn), lambda i, j, kk, off=off: (kk, j + off), lead))

    def body(*refs):
        a_ref = refs[0]
        b_refs = refs[1:1 + nb]
        ex = refs[1 + nb:1 + nb + ne]
        outs = refs[1 + nb + ne:1 + nb + ne + no]
        accs = refs[1 + nb + ne + no:]

        def part(b_ref):
            return lax.dot_general(a_ref[...], b_ref[...], dims, preferred_element_type=F32)

        if nk == 1:
            epilogue([part(b_ref) for b_ref in b_refs], ex, outs)
            return
        kk = pl.program_id(2)

        @pl.when(kk == 0)
        def _():
            for acc, b_ref in zip(accs, b_refs):
                acc[...] = part(b_ref)

        @pl.when(kk > 0)
        def _():
            for acc, b_ref in zip(accs, b_refs):
                acc[...] += part(b_ref)

        @pl.when(kk == nk - 1)
        def _():
            epilogue([acc[...] for acc in accs], ex, outs)

    scratch = [pltpu.VMEM((tm, tn), F32) for _ in range(nb)] if nk > 1 else []
    return pl.pallas_call(
        body, name=name, grid=(m // tm, n // tn, nk),
        in_specs=[a_spec] + b_specs + list(extra_specs),
        out_specs=list(out_specs), out_shape=list(out_shape), scratch_shapes=scratch,
        input_output_aliases=aliases or {},
        compiler_params=_cp(dimension_semantics=("arbitrary", "arbitrary", "arbitrary")),
    )(a, *[b for b, _, _ in bs], *extras), (tm, tn, tk)


def _ij(tm, tn):
    return pl.BlockSpec((tm, tn), lambda i, j, kk: (i, j))


def _tiles(m, n, k, tn_div=1):
    return _tile(m, TM_PREF), _tile(n, TN_PREF // tn_div, LANE), _tile(k, TK_PREF, LANE)


def _mm_plain(name, a, b, b_lead, mode, out_dtype):
    m, k = a.shape
    n = b.shape[-1] if mode == "nn" else b.shape[-2]
    tm, tn, _ = _tiles(m, n, k)

    def epilogue(accs, ex, outs):
        outs[0][...] = accs[0].astype(out_dtype)

    return _mm(name, a, [(b, b_lead, 0)], mode=mode, m=m, n=n, k=k, epilogue=epilogue,
               out_shape=[jax.ShapeDtypeStruct((m, n), out_dtype)], out_specs=[_ij(tm, tn)])[0][0]


def _mm_res(name, a, b, b_lead, resid, bias=None):
    m, k = a.shape
    n = b.shape[-1]
    tm, tn, _ = _tiles(m, n, k)
    extras, specs = [resid], [_ij(tm, tn)]
    if bias is not None:
        extras.append(bias)
        specs.append(pl.BlockSpec((1, tn), lambda i, j, kk: (0, j)))

    def epilogue(accs, ex, outs):
        y = ex[0][...] + accs[0]
        if bias is not None:
            y = y + ex[1][...]
        outs[0][...] = y

    return _mm(name, a, [(b, b_lead, 0)], mode="nn", m=m, n=n, k=k, epilogue=epilogue,
               extras=extras, extra_specs=specs,
               out_shape=[jax.ShapeDtypeStruct((m, n), F32)], out_specs=[_ij(tm, tn)])[0][0]


def _mm_mlp_up(name, h, w1, lead):
    m, k = h.shape
    n = w1.shape[-1]
    tm, tn, _ = _tiles(m, n, k)

    def epilogue(accs, ex, outs):
        z = accs[0]
        outs[0][...] = z.astype(BF16)
        r = jnp.maximum(z, 0.0)
        outs[1][...] = (r * r).astype(BF16)

    sh = jax.ShapeDtypeStruct((m, n), BF16)
    return _mm(name, h, [(w1, lead, 0)], mode="nn", m=m, n=n, k=k, epilogue=epilogue,
               out_shape=[sh, sh], out_specs=[_ij(tm, tn), _ij(tm, tn)])[0]


def _mm_mlp_dz(name, g, w2, lead, z):
    m, k = g.shape
    n = w2.shape[-2]
    tm, tn, _ = _tiles(m, n, k)

    def epilogue(accs, ex, outs):
        outs[0][...] = (accs[0] * (2.0 * jnp.maximum(ex[0][...].astype(F32), 0.0))).astype(BF16)

    return _mm(name, g, [(w2, lead, 0)], mode="nt", m=m, n=n, k=k, epilogue=epilogue,
               extras=[z], extra_specs=[_ij(tm, tn)],
               out_shape=[jax.ShapeDtypeStruct((m, n), BF16)], out_specs=[_ij(tm, tn)])[0][0]


def _mm_glu(name, h, w, lead, bias):
    m, k = h.shape
    n = w.shape[-1] // 2
    tm, tn, _ = _tiles(m, n, k, 2)
    off = n // tn

    def epilogue(accs, ex, outs):
        a = accs[0] + ex[0][...]
        gate = accs[1] + ex[1][...]
        outs[0][...] = a.astype(BF16)
        outs[1][...] = gate.astype(BF16)
        outs[2][...] = a * _sigmoid(gate)

    shb = jax.ShapeDtypeStruct((m, n), BF16)
    return _mm(name, h, [(w, lead, 0), (w, lead, off)], mode="nn", m=m, n=n, k=k, epilogue=epilogue,
               extras=[bias, bias],
               extra_specs=[pl.BlockSpec((1, tn), lambda i, j, kk: (0, j)),
                            pl.BlockSpec((1, tn), lambda i, j, kk: (0, j + off))],
               out_shape=[shb, shb, jax.ShapeDtypeStruct((m, n), F32)],
               out_specs=[_ij(tm, tn)] * 3, tn_div=2)[0]


def _mm_q(name, qn, wq_pad, lead, cq, sq):
    m, k = qn.shape
    n = wq_pad.shape[-1]
    tm, tn, _ = _tiles(m, n, k)
    scale = (HEAD_NOPE + HEAD_ROPE) ** -0.5

    def epilogue(accs, ex, outs):
        c, s = ex[0][:, HEAD_NOPE:], ex[1][:, HEAD_NOPE:]
        for hh in range(tn // HEAD_QK_PAD):
            base = hh * HEAD_QK_PAD
            outs[0][:, base:base + HEAD_NOPE] = (accs[0][:, base:base + HEAD_NOPE] * scale).astype(BF16)
            outs[0][:, base + HEAD_NOPE:base + HEAD_QK_PAD] = _rope(
                accs[0][:, base + HEAD_NOPE:base + HEAD_QK_PAD], c, s).astype(BF16)

    tab = pl.BlockSpec((tm, HEAD_QK_PAD), lambda i, j, kk: (i, 0))
    return _mm(name, qn, [(wq_pad, lead, 0)], mode="nn", m=m, n=n, k=k, epilogue=epilogue,
               extras=[cq, sq], extra_specs=[tab, tab],
               out_shape=[jax.ShapeDtypeStruct((m, n), BF16)], out_specs=[_ij(tm, tn)])[0][0]


def _mm_kv(name, kvn, wkv, lead, kpe):
    m, k = kvn.shape
    n = wkv.shape[-1]
    tm, tn, _ = _tiles(m, n, k)
    heads = tn // (HEAD_NOPE + HEAD_V)

    def epilogue(accs, ex, outs):
        acc = accs[0]
        pe = ex[0][...].astype(F32)
        kparts, vparts = [], []
        for hh in range(heads):
            base = hh * (HEAD_NOPE + HEAD_V)
            kparts += [acc[:, base:base + HEAD_NOPE], pe]
            vparts.append(acc[:, base + HEAD_NOPE:base + HEAD_NOPE + HEAD_V])
        kf = jnp.concatenate(kparts, axis=1)
        vv = jnp.concatenate(vparts, axis=1) if heads > 1 else vparts[0]
        outs[0][...] = kf.astype(BF16)
        outs[1][...] = vv.astype(BF16)
        outs[2][...] = kf.T.astype(BF16)
        outs[3][...] = vv.T.astype(BF16)

    def ji(tn_, tm_):
        return pl.BlockSpec((tn_, tm_), lambda i, j, kk: (j, i))

    return _mm(name, kvn, [(wkv, lead, 0)], mode="nn", m=m, n=n, k=k, epilogue=epilogue,
               extras=[kpe], extra_specs=[pl.BlockSpec((tm, LANE), lambda i, j, kk: (i, 0))],
               out_shape=[jax.ShapeDtypeStruct((m, n), BF16), jax.ShapeDtypeStruct((m, n // 2), BF16),
                          jax.ShapeDtypeStruct((n, m), BF16), jax.ShapeDtypeStruct((n // 2, m), BF16)],
               out_specs=[_ij(tm, tn), _ij(tm, tn // 2), ji(tn, tm), ji(tn // 2, tm)])[0]


def _mm_wgrad(name, a, b):
    t, m = a.shape
    n = b.shape[-1]
    tm, tn, _ = _tiles(m, n, t)

    def epilogue(accs, ex, outs):
        outs[0][...] = accs[0].astype(BF16)

    return _mm(name, a, [(b, None, 0)], mode="tn", m=m, n=n, k=t, epilogue=epilogue,
               out_shape=[jax.ShapeDtypeStruct((m, n), BF16)], out_specs=[_ij(tm, tn)])[0][0]


CONV_ROWS = 256
CONV_RT = 64
CONV_CW = 256
CONV_LR = 32


def _ln_stats(c):
    mu = jnp.mean(c, axis=-1, keepdims=True)
    xc = c - mu
    rstd = lax.rsqrt(jnp.mean(xc * xc, axis=-1, keepdims=True) + LN_EPS)
    return xc * rstd, rstd


def _conv_fwd(name, glu, w_dw, b_dw, ln_g, ln_b):
    t, d = glu.shape
    tt = _tile(t, CONV_ROWS)
    rt, cw, lr = min(CONV_RT, tt), min(CONV_CW, d), min(CONV_LR, tt)
    hb = tt // HALO

    def body(gc_ref, gp_ref, w_ref, b_ref, lg_ref, lb_ref, c_ref, s_ref, buf, win):
        i = pl.program_id(0)
        buf[0:HALO, :] = jnp.where(i > 0, gp_ref[...], 0.0)
        buf[HALO:HALO + tt, :] = gc_ref[...]

        def chunk(cb, carry):
            col = pl.ds(pl.multiple_of(cb * cw, cw), cw)
            for r0 in range(0, tt, rt):
                acc = jnp.broadcast_to(b_ref[:, col], (rt, cw))
                for b in range(SUBLANE):
                    amax = (CONV_W - 1 - b) // SUBLANE
                    lo = r0 + HALO - (CONV_W - 1) + b
                    rows = rt + SUBLANE * amax
                    win[0:rows, :] = buf[lo:lo + rows, col]
                    for a in range(amax + 1):
                        k = SUBLANE * a + b
                        acc = acc + w_ref[k:k + 1, col] * win[SUBLANE * a:SUBLANE * a + rt, :]
                c_ref[r0:r0 + rt, col] = acc
            return carry

        lax.fori_loop(0, d // cw, chunk, 0)

        def ln(r, carry):
            rows = pl.ds(pl.multiple_of(r * lr, lr), lr)
            xh, _ = _ln_stats(c_ref[rows, :])
            y = xh * lg_ref[...] + lb_ref[...]
            s_ref[rows, :] = (y * _sigmoid(y)).astype(BF16)
            return carry

        lax.fori_loop(0, tt // lr, ln, 0)

    row = pl.BlockSpec((tt, d), lambda i: (i, 0))
    vec = pl.BlockSpec((1, d), lambda i: (0, 0))
    return pl.pallas_call(
        body, name=name, grid=(t // tt,),
        in_specs=[row, pl.BlockSpec((HALO, d), lambda i: (jnp.maximum(i * hb - 1, 0), 0)),
                  pl.BlockSpec((HALO, d), lambda i: (0, 0)), vec, vec, vec],
        out_specs=[row, row],
        out_shape=[jax.ShapeDtypeStruct((t, d), F32), jax.ShapeDtypeStruct((t, d), BF16)],
        scratch_shapes=[pltpu.VMEM((HALO + tt, d), F32), pltpu.VMEM((rt + HALO, cw), F32)],
        compiler_params=_cp(dimension_semantics=("arbitrary",)),
    )(glu, glu, w_dw, b_dw, ln_g, ln_b)


def _conv_bwd_ln(name, ds, c, ln_g, ln_b):
    t, d = c.shape
    tt = _tile(t, CONV_ROWS)
    lr = min(CONV_LR, tt)

    def body(ds_ref, c_ref, lg_ref, lb_ref, dc_ref, dg_ref, db_ref, dbdw_ref):
        @pl.when(pl.program_id(0) == 0)
        def _():
            dg_ref[...] = jnp.zeros_like(dg_ref)
            db_ref[...] = jnp.zeros_like(db_ref)
            dbdw_ref[...] = jnp.zeros_like(dbdw_ref)

        def chunk(r, carry):
            rows = pl.ds(pl.multiple_of(r * lr, lr), lr)
            xh, rstd = _ln_stats(c_ref[rows, :])
            g = lg_ref[...]
            y = xh * g + lb_ref[...]
            sg = _sigmoid(y)
            dy = ds_ref[rows, :] * (sg * (1.0 + y * (1.0 - sg)))
            dxh = dy * g
            dc = rstd * (dxh - jnp.mean(dxh, axis=-1, keepdims=True)
                         - xh * jnp.mean(dxh * xh, axis=-1, keepdims=True))
            dc_ref[rows, :] = dc
            dg_ref[...] += jnp.sum(dy * xh, axis=0, keepdims=True)
            db_ref[...] += jnp.sum(dy, axis=0, keepdims=True)
            dbdw_ref[...] += jnp.sum(dc, axis=0, keepdims=True)
            return carry

        lax.fori_loop(0, tt // lr, chunk, 0)

    row = pl.BlockSpec((tt, d), lambda i: (i, 0))
    vec = pl.BlockSpec((1, d), lambda i: (0, 0))
    vsh = jax.ShapeDtypeStruct((1, d), F32)
    return pl.pallas_call(
        body, name=name, grid=(t // tt,),
        in_specs=[row, row, vec, vec], out_specs=[row, vec, vec, vec],
        out_shape=[jax.ShapeDtypeStruct((t, d), F32), vsh, vsh, vsh],
        compiler_params=_cp(dimension_semantics=("arbitrary",)),
    )(ds, c, ln_g, ln_b)


def _conv_bwd_dw(name, dc, glu, ua, ug, w_dw):
    t, d = dc.shape
    tt = _tile(t, CONV_ROWS)
    rt, cw = min(CONV_RT, tt), min(CONV_CW, d)
    hb = tt // HALO
    nt = t // tt

    def body(dcc_ref, dcn_ref, gc_ref, gp_ref, ua_ref, ug_ref, w_ref,
             du_ref, dw_ref, dbu_ref, dbuf, gbuf, wacc, dwin, gwin):
        i = pl.program_id(0)

        @pl.when(i == 0)
        def _():
            wacc[...] = jnp.zeros_like(wacc)
            dbu_ref[...] = jnp.zeros_like(dbu_ref)

        dbuf[0:tt, :] = dcc_ref[...]
        dbuf[tt:tt + HALO, :] = jnp.where(i < nt - 1, dcn_ref[...], 0.0)
        gbuf[0:HALO, :] = jnp.where(i > 0, gp_ref[...], 0.0)
        gbuf[HALO:HALO + tt, :] = gc_ref[...]

        def chunk(cb, carry):
            c0 = pl.multiple_of(cb * cw, cw)
            col = pl.ds(c0, cw)
            colg = pl.ds(pl.multiple_of(d + cb * cw, cw), cw)
            for r0 in range(0, tt, rt):
                dcr = dbuf[r0:r0 + rt, col]
                dgl = jnp.zeros((rt, cw), F32)
                for b in range(SUBLANE):
                    amax = (CONV_W - 1 - b) // SUBLANE
                    hi = r0 + (CONV_W - 1) - b - SUBLANE * amax
                    rows = rt + SUBLANE * amax
                    dwin[0:rows, :] = dbuf[hi:hi + rows, col]
                    lo = r0 + HALO - (CONV_W - 1) + b
                    gwin[0:rows, :] = gbuf[lo:lo + rows, col]
                    for a in range(amax + 1):
                        k = SUBLANE * a + b
                        back = SUBLANE * (amax - a)
                        dgl = dgl + w_ref[k:k + 1, col] * dwin[back:back + rt, :]
                        prod = dcr * gwin[SUBLANE * a:SUBLANE * a + rt, :]
                        part = prod[0:8, :]
                        for r in range(8, rt, 8):
                            part = part + prod[r:r + 8, :]
                        wacc[8 * k:8 * k + 8, col] += part
                a = ua_ref[r0:r0 + rt, col].astype(F32)
                sg = _sigmoid(ug_ref[r0:r0 + rt, col].astype(F32))
                da = dgl * sg
                dgate = dgl * a * sg * (1.0 - sg)
                du_ref[r0:r0 + rt, col] = da.astype(BF16)
                du_ref[r0:r0 + rt, colg] = dgate.astype(BF16)
                dbu_ref[:, col] += jnp.sum(da, axis=0, keepdims=True)
                dbu_ref[:, colg] += jnp.sum(dgate, axis=0, keepdims=True)
            return carry

        lax.fori_loop(0, d // cw, chunk, 0)

        @pl.when(i == nt - 1)
        def _():
            for k in range(CONV_W):
                dw_ref[k:k + 1, :] = jnp.sum(wacc[8 * k:8 * k + 8, :], axis=0, keepdims=True)
            dw_ref[CONV_W:HALO, :] = jnp.zeros((HALO - CONV_W, d), F32)

    row = pl.BlockSpec((tt, d), lambda i: (i, 0))
    return pl.pallas_call(
        body, name=name, grid=(nt,),
        in_specs=[row, pl.BlockSpec((HALO, d), lambda i: (jnp.minimum((i + 1) * hb, t // HALO - 1), 0)),
                  row, pl.BlockSpec((HALO, d), lambda i: (jnp.maximum(i * hb - 1, 0), 0)),
                  row, row, pl.BlockSpec((HALO, d), lambda i: (0, 0))],
        out_specs=[pl.BlockSpec((tt, 2 * d), lambda i: (i, 0)),
                   pl.BlockSpec((HALO, d), lambda i: (0, 0)),
                   pl.BlockSpec((1, 2 * d), lambda i: (0, 0))],
        out_shape=[jax.ShapeDtypeStruct((t, 2 * d), BF16), jax.ShapeDtypeStruct((HALO, d), F32),
                   jax.ShapeDtypeStruct((1, 2 * d), F32)],
        scratch_shapes=[pltpu.VMEM((tt + HALO, d), F32), pltpu.VMEM((HALO + tt, d), F32),
                        pltpu.VMEM((8 * HALO, d), F32),
                        pltpu.VMEM((rt + HALO, cw), F32), pltpu.VMEM((rt + HALO, cw), F32)],
        compiler_params=_cp(dimension_semantics=("arbitrary",)),
    )(dc, dc, glu, glu, ua, ug, w_dw)


def _mla_mid_fwd(name, down, qg, kvg, ck, sk):
    t, w = down.shape
    rq, rkv = qg.shape[-1], kvg.shape[-1]
    tm = _tile(t, 512)

    def body(dn_ref, qg_ref, kvg_ref, ck_ref, sk_ref, qn_ref, kvn_ref, kpe_ref):
        cq = dn_ref[:, 0:rq]
        ckv = dn_ref[:, rq:rq + rkv]
        pe = dn_ref[:, rq + rkv:rq + rkv + LANE]
        qn_ref[...] = (cq * lax.rsqrt(jnp.mean(cq * cq, axis=-1, keepdims=True) + NORM_EPS)
                       * qg_ref[...]).astype(BF16)
        kvn_ref[...] = (ckv * lax.rsqrt(jnp.mean(ckv * ckv, axis=-1, keepdims=True) + NORM_EPS)
                        * kvg_ref[...]).astype(BF16)
        kpe_ref[...] = _rope(pe, ck_ref[...], sk_ref[...]).astype(BF16)

    def row(n):
        return pl.BlockSpec((tm, n), lambda i: (i, 0))

    def vec(n):
        return pl.BlockSpec((1, n), lambda i: (0, 0))

    return pl.pallas_call(
        body, name=name, grid=(t // tm,),
        in_specs=[row(w), vec(rq), vec(rkv), row(LANE), row(LANE)],
        out_specs=[row(rq), row(rkv), row(LANE)],
        out_shape=[jax.ShapeDtypeStruct((t, rq), BF16), jax.ShapeDtypeStruct((t, rkv), BF16),
                   jax.ShapeDtypeStruct((t, LANE), BF16)],
        compiler_params=_cp(),
    )(down, qg, kvg, ck, sk)


def _mla_mid_bwd(name, down, qg, kvg, dqn, dkvn, dkpe):
    t, w = down.shape
    rq, rkv = qg.shape[-1], kvg.shape[-1]
    tm = _tile(t, 256)

    def body(dn_ref, qg_ref, kvg_ref, dqn_ref, dkvn_ref, dkpe_ref, dd_ref, dqg_ref, dkvg_ref):
        @pl.when(pl.program_id(0) == 0)
        def _():
            dqg_ref[...] = jnp.zeros_like(dqg_ref)
            dkvg_ref[...] = jnp.zeros_like(dkvg_ref)

        dcq, dqg = _rms_bwd_math(dn_ref[:, 0:rq], qg_ref[...], dqn_ref[...])
        dckv, dkvg = _rms_bwd_math(dn_ref[:, rq:rq + rkv], kvg_ref[...], dkvn_ref[...])
        dd_ref[:, 0:rq] = dcq.astype(BF16)
        dd_ref[:, rq:rq + rkv] = dckv.astype(BF16)
        dd_ref[:, rq + rkv:rq + rkv + LANE] = dkpe_ref[...].astype(BF16)
        dqg_ref[...] += dqg
        dkvg_ref[...] += dkvg

    def row(n):
        return pl.BlockSpec((tm, n), lambda i: (i, 0))

    def vec(n):
        return pl.BlockSpec((1, n), lambda i: (0, 0))

    return pl.pallas_call(
        body, name=name, grid=(t // tm,),
        in_specs=[row(w), vec(rq), vec(rkv), row(rq), row(rkv), row(LANE)],
        out_specs=[row(w), vec(rq), vec(rkv)],
        out_shape=[jax.ShapeDtypeStruct((t, w), BF16), jax.ShapeDtypeStruct((1, rq), F32),
                   jax.ShapeDtypeStruct((1, rkv), F32)],
        compiler_params=_cp(dimension_semantics=("arbitrary",)),
    )(down, qg, kvg, dqn, dkvn, dkpe)


ATT_TILE = 512
ATT_HEADS = 2
ATT_HEADS_FWD = 4
_NT = (((1,), (1,)), ((), ()))


def _flash_fwd(name, qf, kf, vt, heads):
    s = qf.shape[0]
    t = _tile(s, ATT_TILE)
    n = s // t
    g = min(ATT_HEADS_FWD, heads)
    qw, vw = HEAD_QK_PAD, HEAD_V

    def body(q_ref, k_ref, vt_ref, o_ref, lse_ref, m_sc, l_sc, acc_sc):
        i, j = pl.program_id(1), pl.program_id(2)

        @pl.when(j == 0)
        def _():
            m_sc[...] = jnp.full(m_sc.shape, -jnp.inf, F32)
            l_sc[...] = jnp.zeros_like(l_sc)
            acc_sc[...] = jnp.zeros_like(acc_sc)

        def step(diag):
            for hh in range(g):
                sc = lax.dot_general(k_ref[:, hh * qw:(hh + 1) * qw], q_ref[:, hh * qw:(hh + 1) * qw], _NT,
                                     preferred_element_type=F32)
                if diag:
                    sc = jnp.where(_chunk_mask_t(t), sc, -jnp.inf)
                m_old = m_sc[hh]
                m_new = jnp.maximum(m_old, jnp.max(sc, axis=0, keepdims=True))
                alpha = jnp.exp(m_old - m_new)
                p = jnp.exp(sc - m_new)
                l_sc[hh] = alpha * l_sc[hh] + jnp.sum(p, axis=0, keepdims=True)
                acc_sc[hh] = alpha * acc_sc[hh] + jnp.dot(vt_ref[hh * vw:(hh + 1) * vw, :], p.astype(BF16),
                                                          preferred_element_type=F32)
                m_sc[hh] = m_new

        @pl.when(j < i)
        def _():
            step(False)

        @pl.when(j == i)
        def _():
            step(True)
            for hh in range(g):
                l = l_sc[hh]
                o_ref[:, hh * vw:(hh + 1) * vw] = (acc_sc[hh] / l).T.astype(BF16)
                lse_ref[hh] = m_sc[hh] + jnp.log(l)

    return pl.pallas_call(
        body, name=name, grid=(heads // g, n, n),
        in_specs=[pl.BlockSpec((t, g * qw), lambda h, i, j: (i, h)),
                  pl.BlockSpec((t, g * qw), lambda h, i, j: (jnp.minimum(j, i), h)),
                  pl.BlockSpec((g * vw, t), lambda h, i, j: (h, jnp.minimum(j, i)))],
        out_specs=[pl.BlockSpec((t, g * vw), lambda h, i, j: (i, h)),
                   pl.BlockSpec((g, 1, t), lambda h, i, j: (h, 0, i))],
        out_shape=[jax.ShapeDtypeStruct((s, heads * vw), BF16),
                   jax.ShapeDtypeStruct((heads, 1, s), F32)],
        scratch_shapes=[pltpu.VMEM((g, 1, t), F32), pltpu.VMEM((g, 1, t), F32), pltpu.VMEM((g, vw, t), F32)],
        compiler_params=_cp(dimension_semantics=("arbitrary", "arbitrary", "arbitrary")),
    )(qf, kf, vt)


def _flash_bwd(name, qf, kf, kft, v, do, o, lse, cq, sq, ck, sk, heads):
    s = qf.shape[0]
    t = _tile(s, ATT_TILE)
    n = s // t
    g = min(ATT_HEADS, heads)
    ng = heads // g
    qw, vw = HEAD_QK_PAD, HEAD_V

    def body(q_ref, k_ref, kt_ref, v_ref, do_ref, o_ref, lse_ref, cq_ref, sq_ref, ck_ref, sk_ref,
             dq_hbm, dkv_ref, dpe_hbm, dk_sc, dv_sc, dq_sc, dl_sc, pe_sc, dq_stage, pe_stage, sems):
        h, j, i = pl.program_id(0), pl.program_id(1), pl.program_id(2)
        cols = pl.ds(pl.multiple_of(i * t, t), t)
        rows = pl.ds(pl.multiple_of(j * t, t), t)

        def store(stage, dst, sem):
            cp = pltpu.make_async_copy(stage, dst, sem)
            cp.start()
            cp.wait()

        @pl.when((h == 0) & (j == 0) & (i == 0))
        def _():
            pe_sc[...] = jnp.zeros_like(pe_sc)

        @pl.when(j == 0)
        def _():
            for hh in range(g):
                dq_sc[hh, :, cols] = jnp.zeros((qw, t), F32)
                hv = slice(hh * vw, (hh + 1) * vw)
                col = jnp.sum(do_ref[:, hv].astype(F32) * o_ref[:, hv].astype(F32), axis=1, keepdims=True)
                dl_sc[hh, :, cols] = jnp.broadcast_to(col, (t, LANE)).T[0:1, :]

        @pl.when(i == 0)
        def _():
            dk_sc[...] = jnp.zeros_like(dk_sc)
            dv_sc[...] = jnp.zeros_like(dv_sc)

        def step(diag):
            for hh in range(g):
                q = q_ref[:, hh * qw:(hh + 1) * qw]
                dout = do_ref[:, hh * vw:(hh + 1) * vw]
                sc = lax.dot_general(k_ref[:, hh * qw:(hh + 1) * qw], q, _NT, preferred_element_type=F32)
                p = jnp.exp(sc - lse_ref[hh])
                if diag:
                    p = jnp.where(_chunk_mask_t(t), p, 0.0)
                dv_sc[hh] += jnp.dot(p.astype(BF16), dout, preferred_element_type=F32)
                dp = lax.dot_general(v_ref[:, hh * vw:(hh + 1) * vw], dout, _NT, preferred_element_type=F32)
                ds = (p * (dp - dl_sc[hh, :, cols])).astype(BF16)
                dk_sc[hh] += jnp.dot(ds, q, preferred_element_type=F32)
                dq_sc[hh, :, cols] += jnp.dot(kt_ref[hh * qw:(hh + 1) * qw, :], ds, preferred_element_type=F32)

        @pl.when(i > j)
        def _():
            step(False)

        @pl.when(i == j)
        def _():
            step(True)
            for hh in range(g):
                dq_stage[:, hh * qw:(hh + 1) * qw] = _rope_t(dq_sc[hh, :, cols].T, cq_ref[...],
                                                             sq_ref[...]).astype(BF16)
            store(dq_stage, dq_hbm.at[cols, pl.ds(pl.multiple_of(h * (g * qw), g * qw), g * qw)], sems.at[0])

        @pl.when(i == n - 1)
        def _():
            pe = None
            for hh in range(g):
                dk = dk_sc[hh]
                dkv_ref[:, hh * qw:(hh + 1) * qw] = jnp.concatenate([dk[:, 0:HEAD_NOPE], dv_sc[hh]],
                                                                     axis=1).astype(BF16)
                part = dk[:, HEAD_NOPE:HEAD_QK_PAD]
                pe = part if pe is None else pe + part
            pe_sc[rows, :] += pe

            @pl.when(h == ng - 1)
            def _():
                pe_stage[...] = _rope_t(pe_sc[rows, :], ck_ref[...], sk_ref[...])
                store(pe_stage, dpe_hbm.at[rows, :], sems.at[1])

    qrow = lambda h, j, i: (jnp.maximum(i, j), h)
    krow = lambda h, j, i: (j, h)
    return pl.pallas_call(
        body, name=name, grid=(ng, n, n),
        in_specs=[pl.BlockSpec((t, g * qw), qrow),
                  pl.BlockSpec((t, g * qw), krow),
                  pl.BlockSpec((g * qw, t), lambda h, j, i: (h, j)),
                  pl.BlockSpec((t, g * vw), krow),
                  pl.BlockSpec((t, g * vw), qrow),
                  pl.BlockSpec((t, g * vw), lambda h, j, i: (jnp.where(j == 0, i, n - 1), h)),
                  pl.BlockSpec((g, 1, t), lambda h, j, i: (h, 0, jnp.maximum(i, j))),
                  pl.BlockSpec((t, qw), lambda h, j, i: (jnp.maximum(i, j), 0)),
                  pl.BlockSpec((t, qw), lambda h, j, i: (jnp.maximum(i, j), 0)),
                  pl.BlockSpec((t, LANE), lambda h, j, i: (j, 0)),
                  pl.BlockSpec((t, LANE), lambda h, j, i: (j, 0))],
        out_specs=[ANY, pl.BlockSpec((t, g * qw), krow), ANY],
        out_shape=[jax.ShapeDtypeStruct(qf.shape, BF16),
                   jax.ShapeDtypeStruct((s, heads * (HEAD_NOPE + HEAD_V)), BF16),
                   jax.ShapeDtypeStruct((s, LANE), F32)],
        scratch_shapes=[pltpu.VMEM((g, t, qw), F32), pltpu.VMEM((g, t, vw), F32),
                        pltpu.VMEM((g, qw, s), F32), pltpu.VMEM((g, 1, s), F32), pltpu.VMEM((s, LANE), F32),
                        pltpu.VMEM((t, g * qw), BF16), pltpu.VMEM((t, LANE), F32),
                        pltpu.SemaphoreType.DMA((2,))],
        compiler_params=_cp(dimension_semantics=("arbitrary", "arbitrary", "arbitrary")),
    )(qf, kf, kft, v, do, o, lse, cq, sq, ck, sk)


def _adamw(name, parts, w, m, v):
    p, r, c = parts.shape
    tr = _tile(r, max(8, (256 * 1024) // max(c, 1)))
    bc1 = 1.0 - ADAM_B1 ** ADAM_STEP
    bc2 = 1.0 - ADAM_B2 ** ADAM_STEP

    def body(p_ref, w_ref, m_ref, v_ref, g_ref, d_ref, nm_ref, nv_ref):
        g = p_ref[0].astype(F32)
        for q in range(1, p):
            g = g + p_ref[q].astype(F32)
        nm = ADAM_B1 * m_ref[...] + (1.0 - ADAM_B1) * g
        nv = ADAM_B2 * v_ref[...] + (1.0 - ADAM_B2) * (g * g)
        g_ref[...] = g
        nm_ref[...] = nm
        nv_ref[...] = nv
        d_ref[...] = -ADAM_LR * ((nm / bc1) / (jnp.sqrt(nv / bc2) + ADAM_EPS) + ADAM_WD * w_ref[...])

    blk = pl.BlockSpec((tr, c), lambda i: (i, 0))
    sh = jax.ShapeDtypeStruct((r, c), F32)
    return pl.pallas_call(
        body, name=name, grid=(r // tr,),
        in_specs=[pl.BlockSpec((p, tr, c), lambda i: (0, i, 0)), blk, blk, blk],
        out_specs=[blk] * 4, out_shape=[sh] * 4,
        compiler_params=_cp(),
    )(parts, w, m, v)


def _my_place():
    x, y, c = lax.axis_index("x"), lax.axis_index("y"), lax.axis_index("c")
    return x, y, c


def _flip(v, bit):
    return 1 - v if bit else v


def _block(ref, axis, idx, size):
    return ref.at[(slice(None),) * axis + (pl.ds(idx * size, size),)]


HBM_SPEC = pl.BlockSpec(memory_space=pltpu.HBM)
SEM_SPEC = pl.BlockSpec(memory_space=pltpu.SEMAPHORE)
DATAFLOW = pltpu.SideEffectType.DATAFLOW_SIDE_EFFECTING


def _hbm(a):
    return pltpu.with_memory_space_constraint(a, pltpu.HBM)


def _remote_copies(jobs, bufs, send_sems, recv_sems):
    return [pltpu.make_async_remote_copy(src_ref=src, dst_ref=dst, send_sem=send_sems.at[q],
                                         recv_sem=recv_sems.at[q], device_id=dev, device_id_type=MESH)
            for q, (src, dst, dev) in enumerate(jobs(bufs))]


def _split_start(name, bufs, jobs, n_jobs, after):
    nb = len(bufs)

    def body(*refs):
        send_sems, recv_sems = refs[nb + 1], refs[nb + 2]
        for cp in _remote_copies(jobs, refs[:nb], send_sems, recv_sems):
            cp.start()
        refs[-1][...] = jnp.zeros_like(refs[-1])

    outs = pl.pallas_call(
        body, name=name,
        out_shape=(pltpu.SemaphoreType.DMA((n_jobs,)), pltpu.SemaphoreType.DMA((n_jobs,)),
                   *[pltpu.HBM(b.shape, b.dtype) for b in bufs], jax.ShapeDtypeStruct((8, LANE), F32)),
        in_specs=[HBM_SPEC] * nb + [ANY],
        out_specs=(SEM_SPEC, SEM_SPEC, *[HBM_SPEC] * nb, VMEM_SPEC),
        input_output_aliases={q: 2 + q for q in range(nb)},
        compiler_params=pltpu.CompilerParams(has_side_effects=DATAFLOW),
    )(*[_hbm(b) for b in bufs], after)
    return outs[0], outs[1], list(outs[2:2 + nb]), outs[-1]


def _split_wait(name, bufs, send_sems, recv_sems, jobs, after):
    nb = len(bufs)

    def body(*refs):
        for cp in _remote_copies(jobs, refs[:nb], refs[nb], refs[nb + 1]):
            cp.wait_send()
            cp.wait_recv()

    outs = pl.pallas_call(
        body, name=name,
        out_shape=tuple(pltpu.HBM(b.shape, b.dtype) for b in bufs),
        in_specs=[HBM_SPEC] * nb + [SEM_SPEC, SEM_SPEC, ANY],
        out_specs=tuple([HBM_SPEC] * nb),
        input_output_aliases={q: q for q in range(nb)},
        compiler_params=pltpu.CompilerParams(has_side_effects=DATAFLOW),
    )(*bufs, send_sems, recv_sems, after)
    return list(outs)


PLACE_TILE_BYTES = 2 * 1024 * 1024


def _own_block_spec(tr, c, nblk, axis):
    if axis == 0:
        return pl.BlockSpec((tr, c), lambda i, me: (me[0] * nblk + i, 0))
    return pl.BlockSpec((tr, c), lambda i, me: (i, me[0]))


def _cast_place(name, w, layer, axis, me):
    _, r, c = w.shape
    tr = _tile(r, max(SUBLANE_BF16, PLACE_TILE_BYTES // (4 * c)))
    nblk = r // tr
    full = (N_DEV * r, c) if axis == 0 else (r, N_DEV * c)

    def body(me_ref, w_ref, o_ref):
        o_ref[...] = w_ref[...].astype(BF16)

    return pl.pallas_call(
        body, name=name,
        grid_spec=pltpu.PrefetchScalarGridSpec(
            num_scalar_prefetch=1, grid=(nblk,),
            in_specs=[pl.BlockSpec((None, tr, c), lambda i, me: (layer, i, 0))],
            out_specs=_own_block_spec(tr, c, nblk, axis)),
        out_shape=jax.ShapeDtypeStruct(full, BF16), compiler_params=_cp(),
    )(me, w)


def _own_place(name, grad, land, layer, axis, me):
    _, _, r, c = land.shape
    tr = _tile(r, max(SUBLANE_BF16, PLACE_TILE_BYTES // (2 * c)))
    nblk = r // tr

    def body(me_ref, g_ref, land_ref, o_ref):
        o_ref[...] = g_ref[...]

    return pl.pallas_call(
        body, name=name,
        grid_spec=pltpu.PrefetchScalarGridSpec(
            num_scalar_prefetch=1, grid=(nblk,),
            in_specs=[_own_block_spec(tr, c, nblk, axis), ANY],
            out_specs=pl.BlockSpec((None, None, tr, c), lambda i, me: (0, layer, i, 0))),
        out_shape=jax.ShapeDtypeStruct(land.shape, land.dtype),
        input_output_aliases={2: 0}, compiler_params=_cp(),
    )(me, grad, land)


def _gather_jobs_a(axes, sizes):
    def jobs(bufs):
        x, y, c = _my_place()
        out = []
        for t, buf in enumerate(bufs):
            blk = _block(buf, axes[t], 4 * x + 2 * y + c, sizes[t])
            for dev in [(x, y, 1 - c), (1 - x, y, c), (x, 1 - y, c), (1 - x, 1 - y, c)]:
                out.append((blk, blk, dev))
        return out
    return jobs


def _gather_jobs_b(axes, sizes):
    nt = len(axes)

    def jobs(bufs):
        x, y, c = _my_place()
        out = []
        for t in range(nt):
            for px, py in [(1 - x, y), (x, 1 - y), (1 - x, 1 - y)]:
                blk = _block(bufs[t], axes[t], 4 * px + 2 * py + c, sizes[t])
                out.append((blk, blk, (x, y, 1 - c)))
        return out
    return jobs


def _exchange_jobs(axes, sizes, layers):
    nt = len(axes)

    def jobs(bufs):
        x, y, c = _my_place()
        out = []
        for k in range(1, N_DEV):
            px, py, pc = _flip(x, k & 4), _flip(y, k & 2), _flip(c, k & 1)
            for t in range(nt):
                out.append((_block(bufs[t], axes[t], 4 * px + 2 * py + pc, sizes[t]),
                            bufs[nt + t].at[k, layers[t]], (px, py, pc)))
        return out
    return jobs


def _gather_begin(name, lands, axes, after):
    sizes = [b.shape[ax] // N_DEV for b, ax in zip(lands, axes)]
    jobs = _gather_jobs_a(axes, sizes)
    send, recv, bufs, token = _split_start(name + "_a", lands, jobs, 4 * len(lands), after)
    return dict(name=name, axes=axes, sizes=sizes, send=send, recv=recv, bufs=bufs, jobs=jobs), token


def _gather_mid(h, after):
    bufs = _split_wait(h["name"] + "_aw", h["bufs"], h["send"], h["recv"], h["jobs"], after)
    jobs = _gather_jobs_b(h["axes"], h["sizes"])
    send, recv, lands, token = _split_start(h["name"] + "_b", bufs, jobs, 3 * len(bufs), after)
    return dict(h, send=send, recv=recv, bufs=lands, jobs=jobs), token


def _gather_end(h, after):
    return _split_wait(h["name"] + "_bw", h["bufs"], h["send"], h["recv"], h["jobs"], after)


def _exchange_begin(name, grads, axes, lands, layers, me, after):
    sizes = [g.shape[ax] // N_DEV for g, ax in zip(grads, axes)]
    lands = [_own_place(f"{name}_place{t}", grads[t], lands[t], layers[t], axes[t], me)
             for t in range(len(grads))]
    jobs = _exchange_jobs(axes, sizes, layers)
    send, recv, bufs, token = _split_start(name + "_s", list(grads) + lands, jobs, 7 * len(grads), after)
    return dict(name=name, n=len(grads), send=send, recv=recv, bufs=bufs, jobs=jobs), token


def _exchange_end(h, after):
    bufs = _split_wait(h["name"] + "_w", h["bufs"], h["send"], h["recv"], h["jobs"], after)
    return bufs[h["n"]:]


def _all_gather_small(name, vec, reduce):
    r = vec.shape[0]

    def body(v_ref, o_ref, *rest):
        if reduce:
            buf, send_sems, recv_sems = rest
        else:
            buf = o_ref
            send_sems, recv_sems = rest
        x, y, c = _my_place()
        mine = 4 * x + 2 * y + c
        buf[mine] = v_ref[...]
        copies = []
        for k in range(1, N_DEV):
            px, py, pc = _flip(x, k & 4), _flip(y, k & 2), _flip(c, k & 1)
            cp = pltpu.make_async_remote_copy(
                src_ref=v_ref, dst_ref=buf.at[mine], send_sem=send_sems.at[k - 1],
                recv_sem=recv_sems.at[k - 1], device_id=(px, py, pc), device_id_type=MESH)
            cp.start()
            copies.append(cp)
        for cp in copies:
            cp.wait()
        if reduce:
            acc = buf[0]
            for q in range(1, N_DEV):
                acc = acc + buf[q]
            o_ref[...] = acc

    scratch = [pltpu.SemaphoreType.DMA((N_DEV - 1,)), pltpu.SemaphoreType.DMA((N_DEV - 1,))]
    if reduce:
        scratch = [pltpu.VMEM((N_DEV, r, LANE), F32)] + scratch
        out_shape = jax.ShapeDtypeStruct((r, LANE), F32)
    else:
        out_shape = jax.ShapeDtypeStruct((N_DEV, r, LANE), F32)
    return pl.pallas_call(
        body, name=name, in_specs=[VMEM_SPEC], out_specs=VMEM_SPEC, out_shape=out_shape,
        scratch_shapes=scratch, compiler_params=_cp(has_side_effects=True),
    )(vec)


def _pack(arrs, row_mult=8):
    flat = jnp.concatenate([a.reshape(-1).astype(F32) for a in arrs])
    n = flat.shape[0]
    rows = -(-n // LANE)
    rows = -(-rows // row_mult) * row_mult
    return jnp.pad(flat, (0, rows * LANE - n)).reshape(rows, LANE)


def _unpack(vec, shapes):
    flat = vec.reshape(-1)
    out, pos = [], 0
    for sh in shapes:
        n = 1
        for s in sh:
            n *= s
        out.append(flat[pos:pos + n].reshape(sh))
        pos += n
    return out


BIG = ["conv_w_pw1", "conv_w_pw2", "mla_w_in", "mla_w_q_up", "mla_w_kv_up", "mla_w_o", "mlp_w1", "mlp_w2"]
BIG_AXIS = {"conv_w_pw1": 2, "conv_w_pw2": 1, "mla_w_in": 1, "mla_w_q_up": 2, "mla_w_kv_up": 2,
            "mla_w_o": 1, "mlp_w1": 2, "mlp_w2": 1}
SMALL_SHARDED = ["conv_w_dw", "mla_q_norm_g", "mla_kv_norm_g"]
REPLICATED = ["norm_mixer_g", "norm_mlp_g", "conv_b_pw1", "conv_b_dw", "conv_ln_g", "conv_ln_b",
              "conv_b_pw2", "final_norm_g"]
WEIGHTS = ["norm_mixer_g", "norm_mlp_g", "conv_w_pw1", "conv_b_pw1", "conv_w_dw", "conv_b_dw",
           "conv_ln_g", "conv_ln_b", "conv_w_pw2", "conv_b_pw2", "mla_w_in", "mla_q_norm_g",
           "mla_kv_norm_g", "mla_w_q_up", "mla_w_kv_up", "mla_w_o", "mlp_w1", "mlp_w2", "final_norm_g"]


def _unshard_last(g, lead):
    nd = g.ndim
    perm = tuple(range(1, nd - 1)) + (0, nd - 1)
    return g.transpose(perm).reshape(lead + (N_DEV * g.shape[-1],))


def _step(w, m, v, x, positions, target):
    s, d = x.shape
    depth = w["norm_mixer_g"].shape[0]
    n_conv, n_mla = w["conv_w_pw1"].shape[0], w["mla_w_in"].shape[0]
    heads = (w["mla_w_q_up"].shape[-1] * N_DEV) // (HEAD_NOPE + HEAD_ROPE)
    rq, rkv = w["mla_w_q_up"].shape[1], w["mla_w_kv_up"].shape[1]
    xi, yi, ci = _my_place()
    mine = 4 * xi + 2 * yi + ci

    def mixer_units(layer):
        names = (["conv_w_pw1", "conv_w_pw2"] if layer % 2 == 0
                 else ["mla_w_in", "mla_w_q_up", "mla_w_kv_up", "mla_w_o"])
        return [(n, layer // 2) for n in names]

    def mlp_units(layer):
        return [("mlp_w1", layer), ("mlp_w2", layer)]

    me_arr = mine.astype(jnp.int32).reshape(1)

    def gather_begin(tag, units, after):
        lands = [_cast_place(f"{tag}_place_{n}", w[n], jl, BIG_AXIS[n] - 1, me_arr) for n, jl in units]
        h, token = _gather_begin(tag, lands, [BIG_AXIS[n] - 1 for n, _ in units], after)
        return dict(h, units=units), token

    full = {}

    def gather_end(h, after):
        full.update(zip(h["units"], _gather_end(h, after)))

    small_shapes = [w[n].shape for n in SMALL_SHARDED]
    gathered = _all_gather_small("gather_small", _pack([w[n] for n in SMALL_SHARDED]), False)

    first_a, tok = gather_begin("gather_0a", mixer_units(0), gathered)
    first_b, tok = gather_begin("gather_0b", mlp_units(0), tok)
    pending, pending_mlp = {}, {}
    if depth > 1:
        pending[1], tok = gather_begin("gather_1", mixer_units(1), tok)
        pending_mlp[1], tok = gather_begin("gather_1b", mlp_units(1), tok)
    h_first = _rms_fwd("rms_mixer_0", x, w["norm_mixer_g"][0].reshape(1, -1) + tok[0, 0])
    first_a, tok = _gather_mid(first_a, h_first)
    gather_end(first_a, tok)

    per_dev = [_unpack(gathered[q], small_shapes) for q in range(N_DEV)]
    w_dw = _unshard_last(jnp.stack([p[0] for p in per_dev]), (n_conv, CONV_W))
    q_gain = _unshard_last(jnp.stack([p[1] for p in per_dev]), (n_mla,))
    kv_gain = _unshard_last(jnp.stack([p[2] for p in per_dev]), (n_mla,))
    w_dw_pad = jnp.pad(w_dw, ((0, 0), (0, HALO - CONV_W), (0, 0)))

    w_in_cols = rq + rkv + HEAD_ROPE

    def pad_w_in(a):
        return jnp.pad(a, ((0, 0), (0, rq + rkv + LANE - w_in_cols)))

    def pad_wq(a):
        return jnp.pad(a.reshape(rq, heads, HEAD_NOPE + HEAD_ROPE),
                       ((0, 0), (0, 0), (0, HEAD_QK_PAD - HEAD_NOPE - HEAD_ROPE))).reshape(rq, heads * HEAD_QK_PAD)

    inv_freq = ROPE_THETA ** (-jnp.arange(0, HEAD_ROPE, 2, dtype=F32) / HEAD_ROPE)
    ang = positions.reshape(s).astype(F32)[:, None] * inv_freq
    cos, sin = jnp.cos(ang), jnp.sin(ang)
    c64 = jnp.concatenate([cos, cos], axis=1)
    s64 = jnp.concatenate([-sin, sin], axis=1)
    zeros64 = jnp.zeros((s, LANE - HEAD_ROPE), F32)
    ck = jnp.concatenate([c64, zeros64], axis=1)
    sk = jnp.concatenate([s64, zeros64], axis=1)
    scale = (HEAD_NOPE + HEAD_ROPE) ** -0.5
    cq = scale * jnp.concatenate([jnp.ones((s, HEAD_NOPE), F32), ck], axis=1)
    sq = scale * jnp.concatenate([jnp.zeros((s, HEAD_NOPE), F32), sk], axis=1)

    def vec(a):
        return a.reshape(1, -1)

    saved = []
    wpad = {}
    for layer in range(depth):
        jl = layer // 2
        h = h_first if layer == 0 else _rms_fwd(f"rms_mixer_{layer}", x,
                                                 vec(w["norm_mixer_g"][layer]) + tok[0, 0])
        if layer % 2 == 0:
            ua, ug, glu = _mm_glu(f"conv_pw1_{layer}", h, full["conv_w_pw1", jl], None, vec(w["conv_b_pw1"][jl]))
            cc, sw = _conv_fwd(f"conv_dw_{layer}", glu, w_dw_pad[jl], vec(w["conv_b_dw"][jl]),
                               vec(w["conv_ln_g"][jl]), vec(w["conv_ln_b"][jl]))
            x1 = _mm_res(f"conv_pw2_{layer}", sw, full["conv_w_pw2", jl], None, x, vec(w["conv_b_pw2"][jl]))
            mix = (h, ua, ug, glu, cc, sw)
        else:
            wpad["in", jl] = pad_w_in(full["mla_w_in", jl])
            wpad["q", jl] = pad_wq(full["mla_w_q_up", jl])
            down = _mm_plain(f"mla_down_{layer}", h, wpad["in", jl], None, "nn", F32)
            qn, kvn, kpe = _mla_mid_fwd(f"mla_mid_{layer}", down, vec(q_gain[jl]), vec(kv_gain[jl]), ck, sk)
            qf = _mm_q(f"mla_q_{layer}", qn, wpad["q", jl], None, cq, sq)
            kf, vv, kft, vt = _mm_kv(f"mla_kv_{layer}", kvn, full["mla_w_kv_up", jl], None, kpe)
            o, lse = _flash_fwd(f"mla_attn_{layer}", qf, kf, vt, heads)
            if layer in pending_mlp:
                pending_mlp[layer], tok = _gather_mid(pending_mlp[layer], o)
                tok, o = lax.optimization_barrier((tok, o))
            x1 = _mm_res(f"mla_out_{layer}", o, full["mla_w_o", jl], None, x)
            if layer in pending_mlp:
                gather_end(pending_mlp.pop(layer), x1)
            mix = (h, down, qn, kvn, qf, kf, kft, vv, o, lse)
        anchor = x1
        if layer == 0:
            first_b, anchor = _gather_mid(first_b, anchor)
        if layer + 2 < depth:
            pending[layer + 2], anchor = gather_begin(f"gather_{layer + 2}",
                                                      mixer_units(layer + 2) + mlp_units(layer + 2), anchor)
        if layer == 0:
            gather_end(first_b, anchor)
        elif layer + 1 < depth:
            pending[layer + 1], anchor = _gather_mid(pending[layer + 1], anchor)
        if anchor is not x1:
            tok = anchor
        h2 = _rms_fwd(f"rms_mlp_{layer}", x1, vec(w["norm_mlp_g"][layer]) + tok[0, 0])
        z, a = _mm_mlp_up(f"mlp_up_{layer}", h2, full["mlp_w1", layer], None)
        x2 = _mm_res(f"mlp_down_{layer}", a, full["mlp_w2", layer], None, x1)
        if layer + 1 < depth:
            if layer == 0:
                pending[1], tok = _gather_mid(pending[1], x2)
                gather_end(pending[1], tok)
            else:
                gather_end(pending[layer + 1], x2)
        saved.append((x, mix, x1, h2, z, a))
        x = x2

    loss_row, g, gb, d_final, _ = _final_loss("final_loss", x, vec(w["final_norm_g"]) + tok[0, 0], target)

    recv = {n: lax.empty((N_DEV,) + w[n].shape, BF16) for n in BIG}

    def exchange_begin(tag, items, after):
        names = [n for n, _, _ in items]
        h, token = _exchange_begin(tag, [gr for _, _, gr in items], [BIG_AXIS[n] - 1 for n in names],
                                   [recv[n] for n in names], [jl for _, jl, _ in items], me_arr, after)
        return dict(h, names=names), token

    def exchange_end(h, after):
        recv.update(zip(h["names"], _exchange_end(h, after)))

    mix_exchanges = []
    d_mixer, d_mlp = [None] * depth, [None] * depth
    d_small = {n: [None] * n_conv for n in ["conv_b_pw1", "conv_w_dw", "conv_b_dw", "conv_ln_g",
                                           "conv_ln_b", "conv_b_pw2"]}
    d_qg, d_kvg = [None] * n_mla, [None] * n_mla
    for layer in reversed(range(depth)):
        jl = layer // 2
        x0, mix, x1, h2, z, a = saved[layer]
        colsum_g = None
        dz = _mm_mlp_dz(f"mlp_dz_{layer}", gb, full["mlp_w2", layer], None, z)
        dw2 = _mm_wgrad(f"mlp_dw2_{layer}", a, gb)
        w2_exchange, tok = exchange_begin(f"exchange_w2_{layer}", [("mlp_w2", layer, dw2)], dz)
        tok, dz = lax.optimization_barrier((tok, dz))
        dh2 = _mm_plain(f"mlp_dh_{layer}", dz, full["mlp_w1", layer], None, "nt", F32)
        dw1 = _mm_wgrad(f"mlp_dw1_{layer}", h2, dz)
        w1_exchange, tok = exchange_begin(f"exchange_w1_{layer}", [("mlp_w1", layer, dw1)], tok)
        g, gb, d_mlp[layer], colsum_g = _rms_bwd(f"rms_mlp_bwd_{layer}", x1,
                                                 vec(w["norm_mlp_g"][layer]) + tok[0, 0], dh2, g)
        for hx in mix_exchanges:
            exchange_end(hx, g)
        if layer % 2 == 0:
            h, ua, ug, glu, cc, sw = mix
            d_small["conv_b_pw2"][jl] = colsum_g.reshape(-1)
            dsw = _mm_plain(f"conv_ds_{layer}", gb, full["conv_w_pw2", jl], None, "nt", F32)
            dwp2 = _mm_wgrad(f"conv_dw2_{layer}", sw, gb)
            hx2, tok = exchange_begin(f"exchange_pw2_{layer}", [("conv_w_pw2", jl, dwp2)], dsw)
            dc, dlg, dlb, dbdw = _conv_bwd_ln(f"conv_ln_bwd_{layer}", dsw, cc,
                                              vec(w["conv_ln_g"][jl]) + tok[0, 0], vec(w["conv_ln_b"][jl]))
            du, dwdw, dbu = _conv_bwd_dw(f"conv_dw_bwd_{layer}", dc, glu, ua, ug, w_dw_pad[jl])
            d_small["conv_ln_g"][jl] = dlg.reshape(-1)
            d_small["conv_ln_b"][jl] = dlb.reshape(-1)
            d_small["conv_b_dw"][jl] = dbdw.reshape(-1)
            d_small["conv_w_dw"][jl] = dwdw[:CONV_W]
            d_small["conv_b_pw1"][jl] = dbu.reshape(-1)
            dwp1 = _mm_wgrad(f"conv_dw1_{layer}", h, du)
            hx1, tok = exchange_begin(f"exchange_pw1_{layer}", [("conv_w_pw1", jl, dwp1)], dbu)
            tok, du = lax.optimization_barrier((tok, du))
            dh = _mm_plain(f"conv_dh_{layer}", du, full["conv_w_pw1", jl], None, "nt", F32)
            mix_exchanges = [hx2, hx1]
        else:
            h, down, qn, kvn, qf, kf, kft, vv, o, lse = mix
            do = _mm_plain(f"mla_do_{layer}", gb, full["mla_w_o", jl], None, "nt", BF16)
            dwo = _mm_wgrad(f"mla_dwo_{layer}", o, gb)
            dq, dkv, dkpe = _flash_bwd(f"mla_attn_bwd_{layer}", qf, kf, kft, vv, do, o, lse, cq, sq, ck, sk, heads)
            dqn = _mm_plain(f"mla_dqn_{layer}", dq, wpad["q", jl], None, "nt", F32)
            dwq = _mm_wgrad(f"mla_dwq_{layer}", qn, dq).reshape(rq, heads, HEAD_QK_PAD)[
                :, :, :HEAD_NOPE + HEAD_ROPE].reshape(rq, heads * (HEAD_NOPE + HEAD_ROPE))
            dkvn = _mm_plain(f"mla_dkvn_{layer}", dkv, full["mla_w_kv_up", jl], None, "nt", F32)
            dwkv = _mm_wgrad(f"mla_dwkv_{layer}", kvn, dkv)
            ddown, d_qg[jl], d_kvg[jl] = _mla_mid_bwd(f"mla_mid_bwd_{layer}", down, vec(q_gain[jl]),
                                                      vec(kv_gain[jl]), dqn, dkvn, dkpe)
            dh = _mm_plain(f"mla_dh_{layer}", ddown, wpad["in", jl], None, "nt", F32)
            dwin = _mm_wgrad(f"mla_dwin_{layer}", h, ddown)[:, :w_in_cols]
            items = [("mla_w_in", jl, dwin), ("mla_w_q_up", jl, dwq), ("mla_w_kv_up", jl, dwkv),
                     ("mla_w_o", jl, dwo)]
            hx, tok = exchange_begin(f"exchange_mix_{layer}", items, dh)
            mix_exchanges = [hx]
        g, gb, d_mixer[layer], _ = _rms_bwd(f"rms_mixer_bwd_{layer}", x0,
                                            vec(w["norm_mixer_g"][layer]) + tok[0, 0], dh, g)
        exchange_end(w2_exchange, g)
        exchange_end(w1_exchange, g)
    grad_x = g

    out = {}

    def adamw_big(n):
        sh = w[n].shape
        r, c = sh[0] * sh[1], sh[2]
        res = _adamw(f"adamw_{n}", recv[n].reshape(N_DEV, r, c), w[n].reshape(r, c),
                     m[n].reshape(r, c), v[n].reshape(r, c))
        out[n] = [t.reshape(sh) for t in res]

    late = [n for hx in mix_exchanges for n in hx["names"]]
    early = [n for n in BIG if n not in late]
    for n in early:
        adamw_big(n)
    anchor = out[early[-1]][1]
    for hx in mix_exchanges:
        exchange_end(hx, anchor)
    for n in late:
        adamw_big(n)

    small_full = {
        "norm_mixer_g": jnp.concatenate(d_mixer, axis=0), "norm_mlp_g": jnp.concatenate(d_mlp, axis=0),
        "conv_b_pw1": jnp.stack(d_small["conv_b_pw1"]), "conv_b_dw": jnp.stack(d_small["conv_b_dw"]),
        "conv_ln_g": jnp.stack(d_small["conv_ln_g"]), "conv_ln_b": jnp.stack(d_small["conv_ln_b"]),
        "conv_b_pw2": jnp.stack(d_small["conv_b_pw2"]), "final_norm_g": d_final.reshape(-1),
        "conv_w_dw": jnp.stack(d_small["conv_w_dw"]),
        "mla_q_norm_g": jnp.concatenate(d_qg, axis=0), "mla_kv_norm_g": jnp.concatenate(d_kvg, axis=0),
    }
    names = REPLICATED + SMALL_SHARDED
    packed, _ = lax.optimization_barrier((_pack([small_full[n] for n in names]), anchor))
    summed = _unpack(_all_gather_small("reduce_small", packed, True), [small_full[n].shape for n in names])
    summed = dict(zip(names, summed))
    for n in SMALL_SHARDED:
        width = w[n].shape[-1]
        summed[n] = lax.dynamic_slice_in_dim(summed[n], mine * width, width, axis=summed[n].ndim - 1)
    for group, tag in ((REPLICATED, "replicated"), (SMALL_SHARDED, "small_sharded")):
        shapes = [w[n].shape for n in group]
        res = _adamw(f"adamw_{tag}", _pack([summed[n] for n in group])[None],
                     _pack([w[n] for n in group]), _pack([m[n] for n in group]), _pack([v[n] for n in group]))
        unpacked = [_unpack(t, shapes) for t in res]
        for q, n in enumerate(group):
            out[n] = [unpacked[0][q], unpacked[1][q], unpacked[2][q], unpacked[3][q]]

    loss = lax.psum(loss_row[0, 0], ("x", "y", "c"))
    return loss, grad_x, out


def kernel(x, positions, norm_mixer_g, norm_mlp_g, conv_w_pw1, conv_b_pw1, conv_w_dw, conv_b_dw, conv_ln_g, conv_ln_b, conv_w_pw2, conv_b_pw2, mla_w_in, mla_q_norm_g, mla_kv_norm_g, mla_w_q_up, mla_w_kv_up, mla_w_o, mlp_w1, mlp_w2, final_norm_g, loss_target, m_norm_mixer_g, m_norm_mlp_g, m_conv_w_pw1, m_conv_b_pw1, m_conv_w_dw, m_conv_b_dw, m_conv_ln_g, m_conv_ln_b, m_conv_w_pw2, m_conv_b_pw2, m_mla_w_in, m_mla_q_norm_g, m_mla_kv_norm_g, m_mla_w_q_up, m_mla_w_kv_up, m_mla_w_o, m_mlp_w1, m_mlp_w2, m_final_norm_g, v_norm_mixer_g, v_norm_mlp_g, v_conv_w_pw1, v_conv_b_pw1, v_conv_w_dw, v_conv_b_dw, v_conv_ln_g, v_conv_ln_b, v_conv_w_pw2, v_conv_b_pw2, v_mla_w_in, v_mla_q_norm_g, v_mla_kv_norm_g, v_mla_w_q_up, v_mla_w_kv_up, v_mla_w_o, v_mlp_w1, v_mlp_w2, v_final_norm_g):
    ws = (norm_mixer_g, norm_mlp_g, conv_w_pw1, conv_b_pw1, conv_w_dw, conv_b_dw, conv_ln_g, conv_ln_b,
          conv_w_pw2, conv_b_pw2, mla_w_in, mla_q_norm_g, mla_kv_norm_g, mla_w_q_up, mla_w_kv_up, mla_w_o,
          mlp_w1, mlp_w2, final_norm_g)
    ms = (m_norm_mixer_g, m_norm_mlp_g, m_conv_w_pw1, m_conv_b_pw1, m_conv_w_dw, m_conv_b_dw, m_conv_ln_g,
          m_conv_ln_b, m_conv_w_pw2, m_conv_b_pw2, m_mla_w_in, m_mla_q_norm_g, m_mla_kv_norm_g,
          m_mla_w_q_up, m_mla_w_kv_up, m_mla_w_o, m_mlp_w1, m_mlp_w2, m_final_norm_g)
    vs = (v_norm_mixer_g, v_norm_mlp_g, v_conv_w_pw1, v_conv_b_pw1, v_conv_w_dw, v_conv_b_dw, v_conv_ln_g,
          v_conv_ln_b, v_conv_w_pw2, v_conv_b_pw2, v_mla_w_in, v_mla_q_norm_g, v_mla_kv_norm_g,
          v_mla_w_q_up, v_mla_w_kv_up, v_mla_w_o, v_mlp_w1, v_mlp_w2, v_final_norm_g)
    w, m, v = dict(zip(WEIGHTS, ws)), dict(zip(WEIGHTS, ms)), dict(zip(WEIGHTS, vs))
    s, d = x.shape[-2], x.shape[-1]
    loss, grad_x, out = _step(w, m, v, x.reshape(s, d), positions, loss_target.reshape(s, d))
    grads = [out[n][0] for n in WEIGHTS]
    deltas = [out[n][1] for n in WEIGHTS]
    new_m = [out[n][2] for n in WEIGHTS]
    new_v = [out[n][3] for n in WEIGHTS]
    return (loss, grad_x.reshape(x.shape), *grads, *deltas, *new_m, *new_v)
```

```python
import functools

import jax
import jax.numpy as jnp
from jax import lax
from jax.experimental import pallas as pl
from jax.experimental.pallas import tpu as pltpu

F32 = jnp.float32
BF16 = jnp.bfloat16

NORM_EPS = 1e-6
LN_EPS = 1e-5
ROPE_THETA = 10000.0
CHUNK_BITS = 6
HEAD_NOPE = 128
HEAD_ROPE = 64
HEAD_V = 128
HEAD_QK_PAD = 256
CONV_W = 31
HALO = 32
N_DEV = 8

ADAM_LR = 0.001
ADAM_B1 = 0.9
ADAM_B2 = 0.999
ADAM_EPS = 1e-08
ADAM_WD = 0.01
ADAM_STEP = 10

V7X_VMEM_BYTES = 64 * 1024 * 1024
VMEM_LIMIT = (V7X_VMEM_BYTES * 3) // 4
LANE = 128

MESH = pl.DeviceIdType.MESH
ANY = pl.BlockSpec(memory_space=pl.ANY)
VMEM_SPEC = pl.BlockSpec(memory_space=pltpu.VMEM)


def _cp(**kw):
    return pltpu.CompilerParams(vmem_limit_bytes=VMEM_LIMIT, **kw)


SUBLANE = 8
SUBLANE_BF16 = 16

TM_PREF = 1024
TN_PREF = 1024
TK_PREF = 2048


def _tile(n, pref, mult=SUBLANE_BF16):
    if n <= pref + pref // 2:
        return n
    t = (pref // mult) * mult
    while t >= mult:
        if n % t == 0:
            return t
        t -= mult
    return n


def _sigmoid(x):
    return 1.0 / (1.0 + jnp.exp(-x))


def _rot_half(x):
    n = x.shape[-1]
    lane = lax.broadcasted_iota(jnp.int32, x.shape, x.ndim - 1)
    first = (lane & 63) < 32
    return jnp.where(first, pltpu.roll(x, n - 32, x.ndim - 1), pltpu.roll(x, 32, x.ndim - 1))


def _rope(x, c, s):
    return x * c + _rot_half(x) * s


def _rope_t(d, c, s):
    return d * c + _rot_half(d * s)


def _chunk_mask_t(t):
    row = lax.broadcasted_iota(jnp.int32, (t, t), 0)
    col = lax.broadcasted_iota(jnp.int32, (t, t), 1)
    return jnp.right_shift(row, CHUNK_BITS) <= jnp.right_shift(col, CHUNK_BITS)


def _rms_fwd(name, x, g):
    t, d = x.shape
    tm = _tile(t, 512)

    def body(x_ref, g_ref, o_ref):
        xf = x_ref[...]
        r = lax.rsqrt(jnp.mean(xf * xf, axis=-1, keepdims=True) + NORM_EPS)
        o_ref[...] = (xf * r * g_ref[...]).astype(o_ref.dtype)

    return pl.pallas_call(
        body, name=name, grid=(t // tm,),
        in_specs=[pl.BlockSpec((tm, d), lambda i: (i, 0)), pl.BlockSpec((1, d), lambda i: (0, 0))],
        out_specs=pl.BlockSpec((tm, d), lambda i: (i, 0)),
        out_shape=jax.ShapeDtypeStruct((t, d), BF16),
        compiler_params=_cp(),
    )(x, g)


def _rms_bwd_math(xf, g, dy):
    r = lax.rsqrt(jnp.mean(xf * xf, axis=-1, keepdims=True) + NORM_EPS)
    xh = xf * r
    dg = jnp.sum(dy * xh, axis=0, keepdims=True)
    dxh = dy * g
    dx = r * (dxh - xh * jnp.mean(dxh * xh, axis=-1, keepdims=True))
    return dx, dg


def _rms_bwd(name, x, g, dy, resid):
    t, d = x.shape
    tm = _tile(t, 256)

    def body(x_ref, g_ref, dy_ref, r_ref, dx_ref, dxb_ref, dg_ref, cs_ref):
        @pl.when(pl.program_id(0) == 0)
        def _():
            dg_ref[...] = jnp.zeros_like(dg_ref)
            cs_ref[...] = jnp.zeros_like(cs_ref)

        dx, dg = _rms_bwd_math(x_ref[...], g_ref[...], dy_ref[...])
        tot = r_ref[...] + dx
        dx_ref[...] = tot
        dxb_ref[...] = tot.astype(BF16)
        dg_ref[...] += dg
        cs_ref[...] += jnp.sum(tot, axis=0, keepdims=True)

    row = pl.BlockSpec((tm, d), lambda i: (i, 0))
    vec = pl.BlockSpec((1, d), lambda i: (0, 0))
    return pl.pallas_call(
        body, name=name, grid=(t // tm,),
        in_specs=[row, vec, row, row],
        out_specs=[row, row, vec, vec],
        out_shape=[jax.ShapeDtypeStruct((t, d), F32), jax.ShapeDtypeStruct((t, d), BF16),
                   jax.ShapeDtypeStruct((1, d), F32), jax.ShapeDtypeStruct((1, d), F32)],
        compiler_params=_cp(dimension_semantics=("arbitrary",)),
    )(x, g, dy, resid)


def _final_loss(name, x, g, target):
    t, d = x.shape
    tm = _tile(t, 256)

    def body(x_ref, g_ref, t_ref, loss_ref, dx_ref, dxb_ref, dg_ref, cs_ref):
        @pl.when(pl.program_id(0) == 0)
        def _():
            loss_ref[...] = jnp.zeros_like(loss_ref)
            dg_ref[...] = jnp.zeros_like(dg_ref)
            cs_ref[...] = jnp.zeros_like(cs_ref)

        xf = x_ref[...]
        gg = g_ref[...]
        r = lax.rsqrt(jnp.mean(xf * xf, axis=-1, keepdims=True) + NORM_EPS)
        err = xf * r * gg - t_ref[...]
        part = 0.5 * jnp.sum(jnp.mean(err * err, axis=-1, keepdims=True), axis=0, keepdims=True)
        loss_ref[...] += jnp.broadcast_to(part, loss_ref.shape)
        dx, dg = _rms_bwd_math(xf, gg, err * (1.0 / d))
        dx_ref[...] = dx
        dxb_ref[...] = dx.astype(BF16)
        dg_ref[...] += dg
        cs_ref[...] += jnp.sum(dx, axis=0, keepdims=True)

    row = pl.BlockSpec((tm, d), lambda i: (i, 0))
    vec = pl.BlockSpec((1, d), lambda i: (0, 0))
    return pl.pallas_call(
        body, name=name, grid=(t // tm,),
        in_specs=[row, vec, row],
        out_specs=[pl.BlockSpec((1, LANE), lambda i: (0, 0)), row, row, vec, vec],
        out_shape=[jax.ShapeDtypeStruct((1, LANE), F32), jax.ShapeDtypeStruct((t, d), F32),
                   jax.ShapeDtypeStruct((t, d), BF16), jax.ShapeDtypeStruct((1, d), F32),
                   jax.ShapeDtypeStruct((1, d), F32)],
        compiler_params=_cp(dimension_semantics=("arbitrary",)),
    )(x, g, target)


_DIMS = {
    "nn": (((1,), (0,)), ((), ())),
    "nt": (((1,), (1,)), ((), ())),
    "tn": (((0,), (0,)), ((), ())),
}


def _mm(name, a, bs, *, mode, m, n, k, epilogue, out_shape, out_specs, extras=(), extra_specs=(),
        a_lead=None, aliases=None, tn_div=1):
    tm, tn, tk = _tiles(m, n, k, tn_div)
    nk = k // tk
    nb, ne = len(bs), len(extras)
    no = len(out_shape)
    dims = _DIMS[mode]

    def with_lead(shape, idx, lead):
        if lead is None:
            return pl.BlockSpec(shape, idx)
        return pl.BlockSpec((None,) + shape, lambda i, j, kk: (lead,) + idx(i, j, kk))

    if mode == "tn":
        a_spec = with_lead((tk, tm), lambda i, j, kk: (kk, i), a_lead)
    else:
        a_spec = with_lead((tm, tk), lambda i, j, kk: (i, kk), a_lead)
    b_specs = []
    for _, lead, off in bs:
        if mode == "nt":
            b_specs.append(with_lead((tn, tk), lambda i, j, kk, off=off: (j + off, kk), lead))
        else:
            b_specs.append(with_lead((tk, tn), lambda i, j, kk, off=off: (kk, j + off), lead))

    def body(*refs):
        a_ref = refs[0]
        b_refs = refs[1:1 + nb]
        ex = refs[1 + nb:1 + nb + ne]
        outs = refs[1 + nb + ne:1 + nb + ne + no]
        accs = refs[1 + nb + ne + no:]

        def part(b_ref):
            return lax.dot_general(a_ref[...], b_ref[...], dims, preferred_element_type=F32)

        if nk == 1:
            epilogue([part(b_ref) for b_ref in b_refs], ex, outs)
            return
        kk = pl.program_id(2)

        @pl.when(kk == 0)
        def _():
            for acc, b_ref in zip(accs, b_refs):
                acc[...] = part(b_ref)

        @pl.when(kk > 0)
        def _():
            for acc, b_ref in zip(accs, b_refs):
                acc[...] += part(b_ref)

        @pl.when(kk == nk - 1)
        def _():
            epilogue([acc[...] for acc in accs], ex, outs)

    scratch = [pltpu.VMEM((tm, tn), F32) for _ in range(nb)] if nk > 1 else []
    return pl.pallas_call(
        body, name=name, grid=(m // tm, n // tn, nk),
        in_specs=[a_spec] + b_specs + list(extra_specs),
        out_specs=list(out_specs), out_shape=list(out_shape), scratch_shapes=scratch,
        input_output_aliases=aliases or {},
        compiler_params=_cp(dimension_semantics=("arbitrary", "arbitrary", "arbitrary")),
    )(a, *[b for b, _, _ in bs], *extras), (tm, tn, tk)


def _ij(tm, tn):
    return pl.BlockSpec((tm, tn), lambda i, j, kk: (i, j))


def _tiles(m, n, k, tn_div=1):
    return _tile(m, TM_PREF), _tile(n, TN_PREF // tn_div, LANE), _tile(k, TK_PREF, LANE)


def _mm_plain(name, a, b, b_lead, mode, out_dtype):
    m, k = a.shape
    n = b.shape[-1] if mode == "nn" else b.shape[-2]
    tm, tn, _ = _tiles(m, n, k)

    def epilogue(accs, ex, outs):
        outs[0][...] = accs[0].astype(out_dtype)

    return _mm(name, a, [(b, b_lead, 0)], mode=mode, m=m, n=n, k=k, epilogue=epilogue,
               out_shape=[jax.ShapeDtypeStruct((m, n), out_dtype)], out_specs=[_ij(tm, tn)])[0][0]


def _mm_res(name, a, b, b_lead, resid, bias=None):
    m, k = a.shape
    n = b.shape[-1]
    tm, tn, _ = _tiles(m, n, k)
    extras, specs = [resid], [_ij(tm, tn)]
    if bias is not None:
        extras.append(bias)
        specs.append(pl.BlockSpec((1, tn), lambda i, j, kk: (0, j)))

    def epilogue(accs, ex, outs):
        y = ex[0][...] + accs[0]
        if bias is not None:
            y = y + ex[1][...]
        outs[0][...] = y

    return _mm(name, a, [(b, b_lead, 0)], mode="nn", m=m, n=n, k=k, epilogue=epilogue,
               extras=extras, extra_specs=specs,
               out_shape=[jax.ShapeDtypeStruct((m, n), F32)], out_specs=[_ij(tm, tn)])[0][0]


def _mm_mlp_up(name, h, w1, lead):
    m, k = h.shape
    n = w1.shape[-1]
    tm, tn, _ = _tiles(m, n, k)

    def epilogue(accs, ex, outs):
        z = accs[0]
        outs[0][...] = z.astype(BF16)
        r = jnp.maximum(z, 0.0)
        outs[1][...] = (r * r).astype(BF16)

    sh = jax.ShapeDtypeStruct((m, n), BF16)
    return _mm(name, h, [(w1, lead, 0)], mode="nn", m=m, n=n, k=k, epilogue=epilogue,
               out_shape=[sh, sh], out_specs=[_ij(tm, tn), _ij(tm, tn)])[0]


def _mm_mlp_dz(name, g, w2, lead, z):
    m, k = g.shape
    n = w2.shape[-2]
    tm, tn, _ = _tiles(m, n, k)

    def epilogue(accs, ex, outs):
        outs[0][...] = (accs[0] * (2.0 * jnp.maximum(ex[0][...].astype(F32), 0.0))).astype(BF16)

    return _mm(name, g, [(w2, lead, 0)], mode="nt", m=m, n=n, k=k, epilogue=epilogue,
               extras=[z], extra_specs=[_ij(tm, tn)],
               out_shape=[jax.ShapeDtypeStruct((m, n), BF16)], out_specs=[_ij(tm, tn)])[0][0]


def _mm_glu(name, h, w, lead, bias):
    m, k = h.shape
    n = w.shape[-1] // 2
    tm, tn, _ = _tiles(m, n, k, 2)
    off = n // tn

    def epilogue(accs, ex, outs):
        a = accs[0] + ex[0][...]
        gate = accs[1] + ex[1][...]
        outs[0][...] = a.astype(BF16)
        outs[1][...] = gate.astype(BF16)
        outs[2][...] = a * _sigmoid(gate)

    shb = jax.ShapeDtypeStruct((m, n), BF16)
    return _mm(name, h, [(w, lead, 0), (w, lead, off)], mode="nn", m=m, n=n, k=k, epilogue=epilogue,
               extras=[bias, bias],
               extra_specs=[pl.BlockSpec((1, tn), lambda i, j, kk: (0, j)),
                            pl.BlockSpec((1, tn), lambda i, j, kk: (0, j + off))],
               out_shape=[shb, shb, jax.ShapeDtypeStruct((m, n), F32)],
               out_specs=[_ij(tm, tn)] * 3, tn_div=2)[0]


def _mm_q(name, qn, wq_pad, lead, cq, sq):
    m, k = qn.shape
    n = wq_pad.shape[-1]
    tm, tn, _ = _tiles(m, n, k)
    scale = (HEAD_NOPE + HEAD_ROPE) ** -0.5

    def epilogue(accs, ex, outs):
        c, s = ex[0][:, HEAD_NOPE:], ex[1][:, HEAD_NOPE:]
        for hh in range(tn // HEAD_QK_PAD):
            base = hh * HEAD_QK_PAD
            outs[0][:, base:base + HEAD_NOPE] = (accs[0][:, base:base + HEAD_NOPE] * scale).astype(BF16)
            outs[0][:, base + HEAD_NOPE:base + HEAD_QK_PAD] = _rope(
                accs[0][:, base + HEAD_NOPE:base + HEAD_QK_PAD], c, s).astype(BF16)

    tab = pl.BlockSpec((tm, HEAD_QK_PAD), lambda i, j, kk: (i, 0))
    return _mm(name, qn, [(wq_pad, lead, 0)], mode="nn", m=m, n=n, k=k, epilogue=epilogue,
               extras=[cq, sq], extra_specs=[tab, tab],
               out_shape=[jax.ShapeDtypeStruct((m, n), BF16)], out_specs=[_ij(tm, tn)])[0][0]


def _mm_kv(name, kvn, wkv, lead, kpe):
    m, k = kvn.shape
    n = wkv.shape[-1]
    tm, tn, _ = _tiles(m, n, k)
    heads = tn // (HEAD_NOPE + HEAD_V)

    def epilogue(accs, ex, outs):
        acc = accs[0]
        pe = ex[0][...].astype(F32)
        kparts, vparts = [], []
        for hh in range(heads):
            base = hh * (HEAD_NOPE + HEAD_V)
            kparts += [acc[:, base:base + HEAD_NOPE], pe]
            vparts.append(acc[:, base + HEAD_NOPE:base + HEAD_NOPE + HEAD_V])
        kf = jnp.concatenate(kparts, axis=1)
        vv = jnp.concatenate(vparts, axis=1) if heads > 1 else vparts[0]
        outs[0][...] = kf.astype(BF16)
        outs[1][...] = vv.astype(BF16)
        outs[2][...] = kf.T.astype(BF16)
        outs[3][...] = vv.T.astype(BF16)

    def ji(tn_, tm_):
        return pl.BlockSpec((tn_, tm_), lambda i, j, kk: (j, i))

    return _mm(name, kvn, [(wkv, lead, 0)], mode="nn", m=m, n=n, k=k, epilogue=epilogue,
               extras=[kpe], extra_specs=[pl.BlockSpec((tm, LANE), lambda i, j, kk: (i, 0))],
               out_shape=[jax.ShapeDtypeStruct((m, n), BF16), jax.ShapeDtypeStruct((m, n // 2), BF16),
                          jax.ShapeDtypeStruct((n, m), BF16), jax.ShapeDtypeStruct((n // 2, m), BF16)],
               out_specs=[_ij(tm, tn), _ij(tm, tn // 2), ji(tn, tm), ji(tn // 2, tm)])[0]


def _mm_wgrad(name, a, b):
    t, m = a.shape
    n = b.shape[-1]
    tm, tn, _ = _tiles(m, n, t)

    def epilogue(accs, ex, outs):
        outs[0][...] = accs[0].astype(BF16)

    return _mm(name, a, [(b, None, 0)], mode="tn", m=m, n=n, k=t, epilogue=epilogue,
               out_shape=[jax.ShapeDtypeStruct((m, n), BF16)], out_specs=[_ij(tm, tn)])[0][0]


CONV_ROWS = 256
CONV_RT = 64
CONV_CW = 256
CONV_LR = 32


def _ln_stats(c):
    mu = jnp.mean(c, axis=-1, keepdims=True)
    xc = c - mu
    rstd = lax.rsqrt(jnp.mean(xc * xc, axis=-1, keepdims=True) + LN_EPS)
    return xc * rstd, rstd


def _conv_fwd(name, glu, w_dw, b_dw, ln_g, ln_b):
    t, d = glu.shape
    tt = _tile(t, CONV_ROWS)
    rt, cw, lr = min(CONV_RT, tt), min(CONV_CW, d), min(CONV_LR, tt)
    hb = tt // HALO

    def body(gc_ref, gp_ref, w_ref, b_ref, lg_ref, lb_ref, c_ref, s_ref, buf, win):
        i = pl.program_id(0)
        buf[0:HALO, :] = jnp.where(i > 0, gp_ref[...], 0.0)
        buf[HALO:HALO + tt, :] = gc_ref[...]

        def chunk(cb, carry):
            col = pl.ds(pl.multiple_of(cb * cw, cw), cw)
            for r0 in range(0, tt, rt):
                acc = jnp.broadcast_to(b_ref[:, col], (rt, cw))
                for b in range(SUBLANE):
                    amax = (CONV_W - 1 - b) // SUBLANE
                    lo = r0 + HALO - (CONV_W - 1) + b
                    rows = rt + SUBLANE * amax
                    win[0:rows, :] = buf[lo:lo + rows, col]
                    for a in range(amax + 1):
                        k = SUBLANE * a + b
                        acc = acc + w_ref[k:k + 1, col] * win[SUBLANE * a:SUBLANE * a + rt, :]
                c_ref[r0:r0 + rt, col] = acc
            return carry

        lax.fori_loop(0, d // cw, chunk, 0)

        def ln(r, carry):
            rows = pl.ds(pl.multiple_of(r * lr, lr), lr)
            xh, _ = _ln_stats(c_ref[rows, :])
            y = xh * lg_ref[...] + lb_ref[...]
            s_ref[rows, :] = (y * _sigmoid(y)).astype(BF16)
            return carry

        lax.fori_loop(0, tt // lr, ln, 0)

    row = pl.BlockSpec((tt, d), lambda i: (i, 0))
    vec = pl.BlockSpec((1, d), lambda i: (0, 0))
    return pl.pallas_call(
        body, name=name, grid=(t // tt,),
        in_specs=[row, pl.BlockSpec((HALO, d), lambda i: (jnp.maximum(i * hb - 1, 0), 0)),
                  pl.BlockSpec((HALO, d), lambda i: (0, 0)), vec, vec, vec],
        out_specs=[row, row],
        out_shape=[jax.ShapeDtypeStruct((t, d), F32), jax.ShapeDtypeStruct((t, d), BF16)],
        scratch_shapes=[pltpu.VMEM((HALO + tt, d), F32), pltpu.VMEM((rt + HALO, cw), F32)],
        compiler_params=_cp(dimension_semantics=("arbitrary",)),
    )(glu, glu, w_dw, b_dw, ln_g, ln_b)


def _conv_bwd_ln(name, ds, c, ln_g, ln_b):
    t, d = c.shape
    tt = _tile(t, CONV_ROWS)
    lr = min(CONV_LR, tt)

    def body(ds_ref, c_ref, lg_ref, lb_ref, dc_ref, dg_ref, db_ref, dbdw_ref):
        @pl.when(pl.program_id(0) == 0)
        def _():
            dg_ref[...] = jnp.zeros_like(dg_ref)
            db_ref[...] = jnp.zeros_like(db_ref)
            dbdw_ref[...] = jnp.zeros_like(dbdw_ref)

        def chunk(r, carry):
            rows = pl.ds(pl.multiple_of(r * lr, lr), lr)
            xh, rstd = _ln_stats(c_ref[rows, :])
            g = lg_ref[...]
            y = xh * g + lb_ref[...]
            sg = _sigmoid(y)
            dy = ds_ref[rows, :] * (sg * (1.0 + y * (1.0 - sg)))
            dxh = dy * g
            dc = rstd * (dxh - jnp.mean(dxh, axis=-1, keepdims=True)
                         - xh * jnp.mean(dxh * xh, axis=-1, keepdims=True))
            dc_ref[rows, :] = dc
            dg_ref[...] += jnp.sum(dy * xh, axis=0, keepdims=True)
            db_ref[...] += jnp.sum(dy, axis=0, keepdims=True)
            dbdw_ref[...] += jnp.sum(dc, axis=0, keepdims=True)
            return carry

        lax.fori_loop(0, tt // lr, chunk, 0)

    row = pl.BlockSpec((tt, d), lambda i: (i, 0))
    vec = pl.BlockSpec((1, d), lambda i: (0, 0))
    vsh = jax.ShapeDtypeStruct((1, d), F32)
    return pl.pallas_call(
        body, name=name, grid=(t // tt,),
        in_specs=[row, row, vec, vec], out_specs=[row, vec, vec, vec],
        out_shape=[jax.ShapeDtypeStruct((t, d), F32), vsh, vsh, vsh],
        compiler_params=_cp(dimension_semantics=("arbitrary",)),
    )(ds, c, ln_g, ln_b)


def _conv_bwd_dw(name, dc, glu, ua, ug, w_dw):
    t, d = dc.shape
    tt = _tile(t, CONV_ROWS)
    rt, cw = min(CONV_RT, tt), min(CONV_CW, d)
    hb = tt // HALO
    nt = t // tt

    def body(dcc_ref, dcn_ref, gc_ref, gp_ref, ua_ref, ug_ref, w_ref,
             du_ref, dw_ref, dbu_ref, dbuf, gbuf, wacc, dwin, gwin):
        i = pl.program_id(0)

        @pl.when(i == 0)
        def _():
            wacc[...] = jnp.zeros_like(wacc)
            dbu_ref[...] = jnp.zeros_like(dbu_ref)

        dbuf[0:tt, :] = dcc_ref[...]
        dbuf[tt:tt + HALO, :] = jnp.where(i < nt - 1, dcn_ref[...], 0.0)
        gbuf[0:HALO, :] = jnp.where(i > 0, gp_ref[...], 0.0)
        gbuf[HALO:HALO + tt, :] = gc_ref[...]

        def chunk(cb, carry):
            c0 = pl.multiple_of(cb * cw, cw)
            col = pl.ds(c0, cw)
            colg = pl.ds(pl.multiple_of(d + cb * cw, cw), cw)
            for r0 in range(0, tt, rt):
                dcr = dbuf[r0:r0 + rt, col]
                dgl = jnp.zeros((rt, cw), F32)
                for b in range(SUBLANE):
                    amax = (CONV_W - 1 - b) // SUBLANE
                    hi = r0 + (CONV_W - 1) - b - SUBLANE * amax
                    rows = rt + SUBLANE * amax
                    dwin[0:rows, :] = dbuf[hi:hi + rows, col]
                    lo = r0 + HALO - (CONV_W - 1) + b
                    gwin[0:rows, :] = gbuf[lo:lo + rows, col]
                    for a in range(amax + 1):
                        k = SUBLANE * a + b
                        back = SUBLANE * (amax - a)
                        dgl = dgl + w_ref[k:k + 1, col] * dwin[back:back + rt, :]
                        prod = dcr * gwin[SUBLANE * a:SUBLANE * a + rt, :]
                        part = prod[0:8, :]
                        for r in range(8, rt, 8):
                            part = part + prod[r:r + 8, :]
                        wacc[8 * k:8 * k + 8, col] += part
                a = ua_ref[r0:r0 + rt, col].astype(F32)
                sg = _sigmoid(ug_ref[r0:r0 + rt, col].astype(F32))
                da = dgl * sg
                dgate = dgl * a * sg * (1.0 - sg)
                du_ref[r0:r0 + rt, col] = da.astype(BF16)
                du_ref[r0:r0 + rt, colg] = dgate.astype(BF16)
                dbu_ref[:, col] += jnp.sum(da, axis=0, keepdims=True)
                dbu_ref[:, colg] += jnp.sum(dgate, axis=0, keepdims=True)
            return carry

        lax.fori_loop(0, d // cw, chunk, 0)

        @pl.when(i == nt - 1)
        def _():
            for k in range(CONV_W):
                dw_ref[k:k + 1, :] = jnp.sum(wacc[8 * k:8 * k + 8, :], axis=0, keepdims=True)
            dw_ref[CONV_W:HALO, :] = jnp.zeros((HALO - CONV_W, d), F32)

    row = pl.BlockSpec((tt, d), lambda i: (i, 0))
    return pl.pallas_call(
        body, name=name, grid=(nt,),
        in_specs=[row, pl.BlockSpec((HALO, d), lambda i: (jnp.minimum((i + 1) * hb, t // HALO - 1), 0)),
                  row, pl.BlockSpec((HALO, d), lambda i: (jnp.maximum(i * hb - 1, 0), 0)),
                  row, row, pl.BlockSpec((HALO, d), lambda i: (0, 0))],
        out_specs=[pl.BlockSpec((tt, 2 * d), lambda i: (i, 0)),
                   pl.BlockSpec((HALO, d), lambda i: (0, 0)),
                   pl.BlockSpec((1, 2 * d), lambda i: (0, 0))],
        out_shape=[jax.ShapeDtypeStruct((t, 2 * d), BF16), jax.ShapeDtypeStruct((HALO, d), F32),
                   jax.ShapeDtypeStruct((1, 2 * d), F32)],
        scratch_shapes=[pltpu.VMEM((tt + HALO, d), F32), pltpu.VMEM((HALO + tt, d), F32),
                        pltpu.VMEM((8 * HALO, d), F32),
                        pltpu.VMEM((rt + HALO, cw), F32), pltpu.VMEM((rt + HALO, cw), F32)],
        compiler_params=_cp(dimension_semantics=("arbitrary",)),
    )(dc, dc, glu, glu, ua, ug, w_dw)


def _mla_mid_fwd(name, down, qg, kvg, ck, sk):
    t, w = down.shape
    rq, rkv = qg.shape[-1], kvg.shape[-1]
    tm = _tile(t, 512)

    def body(dn_ref, qg_ref, kvg_ref, ck_ref, sk_ref, qn_ref, kvn_ref, kpe_ref):
        cq = dn_ref[:, 0:rq]
        ckv = dn_ref[:, rq:rq + rkv]
        pe = dn_ref[:, rq + rkv:rq + rkv + LANE]
        qn_ref[...] = (cq * lax.rsqrt(jnp.mean(cq * cq, axis=-1, keepdims=True) + NORM_EPS)
                       * qg_ref[...]).astype(BF16)
        kvn_ref[...] = (ckv * lax.rsqrt(jnp.mean(ckv * ckv, axis=-1, keepdims=True) + NORM_EPS)
                        * kvg_ref[...]).astype(BF16)
        kpe_ref[...] = _rope(pe, ck_ref[...], sk_ref[...]).astype(BF16)

    def row(n):
        return pl.BlockSpec((tm, n), lambda i: (i, 0))

    def vec(n):
        return pl.BlockSpec((1, n), lambda i: (0, 0))

    return pl.pallas_call(
        body, name=name, grid=(t // tm,),
        in_specs=[row(w), vec(rq), vec(rkv), row(LANE), row(LANE)],
        out_specs=[row(rq), row(rkv), row(LANE)],
        out_shape=[jax.ShapeDtypeStruct((t, rq), BF16), jax.ShapeDtypeStruct((t, rkv), BF16),
                   jax.ShapeDtypeStruct((t, LANE), BF16)],
        compiler_params=_cp(),
    )(down, qg, kvg, ck, sk)


def _mla_mid_bwd(name, down, qg, kvg, dqn, dkvn, dkpe):
    t, w = down.shape
    rq, rkv = qg.shape[-1], kvg.shape[-1]
    tm = _tile(t, 256)

    def body(dn_ref, qg_ref, kvg_ref, dqn_ref, dkvn_ref, dkpe_ref, dd_ref, dqg_ref, dkvg_ref):
        @pl.when(pl.program_id(0) == 0)
        def _():
            dqg_ref[...] = jnp.zeros_like(dqg_ref)
            dkvg_ref[...] = jnp.zeros_like(dkvg_ref)

        dcq, dqg = _rms_bwd_math(dn_ref[:, 0:rq], qg_ref[...], dqn_ref[...])
        dckv, dkvg = _rms_bwd_math(dn_ref[:, rq:rq + rkv], kvg_ref[...], dkvn_ref[...])
        dd_ref[:, 0:rq] = dcq.astype(BF16)
        dd_ref[:, rq:rq + rkv] = dckv.astype(BF16)
        dd_ref[:, rq + rkv:rq + rkv + LANE] = dkpe_ref[...].astype(BF16)
        dqg_ref[...] += dqg
        dkvg_ref[...] += dkvg

    def row(n):
        return pl.BlockSpec((tm, n), lambda i: (i, 0))

    def vec(n):
        return pl.BlockSpec((1, n), lambda i: (0, 0))

    return pl.pallas_call(
        body, name=name, grid=(t // tm,),
        in_specs=[row(w), vec(rq), vec(rkv), row(rq), row(rkv), row(LANE)],
        out_specs=[row(w), vec(rq), vec(rkv)],
        out_shape=[jax.ShapeDtypeStruct((t, w), BF16), jax.ShapeDtypeStruct((1, rq), F32),
                   jax.ShapeDtypeStruct((1, rkv), F32)],
        compiler_params=_cp(dimension_semantics=("arbitrary",)),
    )(down, qg, kvg, dqn, dkvn, dkpe)


ATT_TILE = 512
ATT_HEADS = 2
ATT_HEADS_FWD = 4
_NT = (((1,), (1,)), ((), ()))


def _flash_fwd(name, qf, kf, vt, heads):
    s = qf.shape[0]
    t = _tile(s, ATT_TILE)
    n = s // t
    g = min(ATT_HEADS_FWD, heads)
    qw, vw = HEAD_QK_PAD, HEAD_V

    pairs = [(i, j) for i in range(n) for j in range(i + 1)]
    i_tab = jnp.asarray([p[0] for p in pairs], jnp.int32)
    j_tab = jnp.asarray([p[1] for p in pairs], jnp.int32)

    def body(it_ref, jt_ref, q_ref, k_ref, vt_ref, o_ref, lse_ref, m_sc, l_sc, acc_sc):
        i, j = it_ref[pl.program_id(1)], jt_ref[pl.program_id(1)]

        @pl.when(j == 0)
        def _():
            m_sc[...] = jnp.full(m_sc.shape, -jnp.inf, F32)
            l_sc[...] = jnp.zeros_like(l_sc)
            acc_sc[...] = jnp.zeros_like(acc_sc)

        def step(diag):
            for hh in range(g):
                sc = lax.dot_general(k_ref[:, hh * qw:(hh + 1) * qw], q_ref[:, hh * qw:(hh + 1) * qw], _NT,
                                     preferred_element_type=F32)
                if diag:
                    sc = jnp.where(_chunk_mask_t(t), sc, -jnp.inf)
                m_old = m_sc[hh]
                m_new = jnp.maximum(m_old, jnp.max(sc, axis=0, keepdims=True))
                alpha = jnp.exp(m_old - m_new)
                p = jnp.exp(sc - m_new)
                l_sc[hh] = alpha * l_sc[hh] + jnp.sum(p, axis=0, keepdims=True)
                acc_sc[hh] = alpha * acc_sc[hh] + jnp.dot(vt_ref[hh * vw:(hh + 1) * vw, :], p.astype(BF16),
                                                          preferred_element_type=F32)
                m_sc[hh] = m_new

        @pl.when(j < i)
        def _():
            step(False)

        @pl.when(j == i)
        def _():
            step(True)
            for hh in range(g):
                l = l_sc[hh]
                o_ref[:, hh * vw:(hh + 1) * vw] = (acc_sc[hh] / l).T.astype(BF16)
                lse_ref[hh] = m_sc[hh] + jnp.log(l)

    return pl.pallas_call(
        body, name=name,
        grid_spec=pltpu.PrefetchScalarGridSpec(
            num_scalar_prefetch=2, grid=(heads // g, len(pairs)),
            in_specs=[pl.BlockSpec((t, g * qw), lambda h, st, it, jt: (it[st], h)),
                      pl.BlockSpec((t, g * qw), lambda h, st, it, jt: (jt[st], h)),
                      pl.BlockSpec((g * vw, t), lambda h, st, it, jt: (h, jt[st]))],
            out_specs=[pl.BlockSpec((t, g * vw), lambda h, st, it, jt: (it[st], h)),
                       pl.BlockSpec((g, 1, t), lambda h, st, it, jt: (h, 0, it[st]))],
            scratch_shapes=[pltpu.VMEM((g, 1, t), F32), pltpu.VMEM((g, 1, t), F32),
                            pltpu.VMEM((g, vw, t), F32)]),
        out_shape=[jax.ShapeDtypeStruct((s, heads * vw), BF16),
                   jax.ShapeDtypeStruct((heads, 1, s), F32)],
        compiler_params=_cp(dimension_semantics=("arbitrary", "arbitrary")),
    )(i_tab, j_tab, qf, kf, vt)


def _flash_bwd(name, qf, kf, kft, v, do, o, lse, cq, sq, ck, sk, heads):
    s = qf.shape[0]
    t = _tile(s, ATT_TILE)
    n = s // t
    g = min(ATT_HEADS, heads)
    ng = heads // g
    qw, vw = HEAD_QK_PAD, HEAD_V

    pairs = [(j, i) for j in range(n) for i in range(j, n)]
    j_tab = jnp.asarray([p[0] for p in pairs], jnp.int32)
    i_tab = jnp.asarray([p[1] for p in pairs], jnp.int32)

    def body(jt_ref, it_ref, q_ref, k_ref, kt_ref, v_ref, do_ref, o_ref, lse_ref, cq_ref, sq_ref, ck_ref, sk_ref,
             dq_hbm, dkv_ref, dpe_hbm, dk_sc, dv_sc, dq_sc, dl_sc, pe_sc, dq_stage, pe_stage, sems):
        h, st = pl.program_id(0), pl.program_id(1)
        j, i = jt_ref[st], it_ref[st]
        cols = pl.ds(pl.multiple_of(i * t, t), t)
        rows = pl.ds(pl.multiple_of(j * t, t), t)

        def store(stage, dst, sem):
            cp = pltpu.make_async_copy(stage, dst, sem)
            cp.start()
            cp.wait()

        @pl.when((h == 0) & (st == 0))
        def _():
            pe_sc[...] = jnp.zeros_like(pe_sc)

        @pl.when(j == 0)
        def _():
            for hh in range(g):
                dq_sc[hh, :, cols] = jnp.zeros((qw, t), F32)
                hv = slice(hh * vw, (hh + 1) * vw)
                col = jnp.sum(do_ref[:, hv].astype(F32) * o_ref[:, hv].astype(F32), axis=1, keepdims=True)
                dl_sc[hh, :, cols] = jnp.broadcast_to(col, (t, LANE)).T[0:1, :]

        @pl.when(i == j)
        def _():
            dk_sc[...] = jnp.zeros_like(dk_sc)
            dv_sc[...] = jnp.zeros_like(dv_sc)

        def step(diag):
            for hh in range(g):
                q = q_ref[:, hh * qw:(hh + 1) * qw]
                dout = do_ref[:, hh * vw:(hh + 1) * vw]
                sc = lax.dot_general(k_ref[:, hh * qw:(hh + 1) * qw], q, _NT, preferred_element_type=F32)
                p = jnp.exp(sc - lse_ref[hh])
                if diag:
                    p = jnp.where(_chunk_mask_t(t), p, 0.0)
                dv_sc[hh] += jnp.dot(p.astype(BF16), dout, preferred_element_type=F32)
                dp = lax.dot_general(v_ref[:, hh * vw:(hh + 1) * vw], dout, _NT, preferred_element_type=F32)
                ds = (p * (dp - dl_sc[hh, :, cols])).astype(BF16)
                dk_sc[hh] += jnp.dot(ds, q, preferred_element_type=F32)
                dq_sc[hh, :, cols] += jnp.dot(kt_ref[hh * qw:(hh + 1) * qw, :], ds, preferred_element_type=F32)

        @pl.when(i > j)
        def _():
            step(False)

        @pl.when(i == j)
        def _():
            step(True)
            for hh in range(g):
                dq_stage[:, hh * qw:(hh + 1) * qw] = _rope_t(dq_sc[hh, :, cols].T, cq_ref[...],
                                                             sq_ref[...]).astype(BF16)
            store(dq_stage, dq_hbm.at[cols, pl.ds(pl.multiple_of(h * (g * qw), g * qw), g * qw)], sems.at[0])

        @pl.when(i == n - 1)
        def _():
            pe = None
            for hh in range(g):
                dk = dk_sc[hh]
                dkv_ref[:, hh * qw:(hh + 1) * qw] = jnp.concatenate([dk[:, 0:HEAD_NOPE], dv_sc[hh]],
                                                                     axis=1).astype(BF16)
                part = dk[:, HEAD_NOPE:HEAD_QK_PAD]
                pe = part if pe is None else pe + part
            pe_sc[rows, :] += pe

            @pl.when(h == ng - 1)
            def _():
                pe_stage[...] = _rope_t(pe_sc[rows, :], ck_ref[...], sk_ref[...])
                store(pe_stage, dpe_hbm.at[rows, :], sems.at[1])

    qrow = lambda h, st, jt, it: (it[st], h)
    krow = lambda h, st, jt, it: (jt[st], h)
    qtab = lambda h, st, jt, it: (it[st], 0)
    ktab = lambda h, st, jt, it: (jt[st], 0)
    return pl.pallas_call(
        body, name=name,
        grid_spec=pltpu.PrefetchScalarGridSpec(
            num_scalar_prefetch=2, grid=(ng, len(pairs)),
            in_specs=[pl.BlockSpec((t, g * qw), qrow),
                      pl.BlockSpec((t, g * qw), krow),
                      pl.BlockSpec((g * qw, t), lambda h, st, jt, it: (h, jt[st])),
                      pl.BlockSpec((t, g * vw), krow),
                      pl.BlockSpec((t, g * vw), qrow),
                      pl.BlockSpec((t, g * vw), lambda h, st, jt, it: (jnp.where(jt[st] == 0, it[st], n - 1), h)),
                      pl.BlockSpec((g, 1, t), lambda h, st, jt, it: (h, 0, it[st])),
                      pl.BlockSpec((t, qw), qtab), pl.BlockSpec((t, qw), qtab),
                      pl.BlockSpec((t, LANE), ktab), pl.BlockSpec((t, LANE), ktab)],
            out_specs=[ANY, pl.BlockSpec((t, g * qw), krow), ANY],
            scratch_shapes=[pltpu.VMEM((g, t, qw), F32), pltpu.VMEM((g, t, vw), F32),
                            pltpu.VMEM((g, qw, s), F32), pltpu.VMEM((g, 1, s), F32), pltpu.VMEM((s, LANE), F32),
                            pltpu.VMEM((t, g * qw), BF16), pltpu.VMEM((t, LANE), F32),
                            pltpu.SemaphoreType.DMA((2,))]),
        out_shape=[jax.ShapeDtypeStruct(qf.shape, BF16),
                   jax.ShapeDtypeStruct((s, heads * (HEAD_NOPE + HEAD_V)), BF16),
                   jax.ShapeDtypeStruct((s, LANE), F32)],
        compiler_params=_cp(dimension_semantics=("arbitrary", "arbitrary")),
    )(j_tab, i_tab, qf, kf, kft, v, do, o, lse, cq, sq, ck, sk)


def _adamw(name, parts, w, m, v):
    p, r, c = parts.shape
    tr = _tile(r, max(8, (256 * 1024) // max(c, 1)))
    bc1 = 1.0 - ADAM_B1 ** ADAM_STEP
    bc2 = 1.0 - ADAM_B2 ** ADAM_STEP

    def body(p_ref, w_ref, m_ref, v_ref, g_ref, d_ref, nm_ref, nv_ref):
        g = p_ref[0].astype(F32)
        for q in range(1, p):
            g = g + p_ref[q].astype(F32)
        nm = ADAM_B1 * m_ref[...] + (1.0 - ADAM_B1) * g
        nv = ADAM_B2 * v_ref[...] + (1.0 - ADAM_B2) * (g * g)
        g_ref[...] = g
        nm_ref[...] = nm
        nv_ref[...] = nv
        d_ref[...] = -ADAM_LR * ((nm / bc1) / (jnp.sqrt(nv / bc2) + ADAM_EPS) + ADAM_WD * w_ref[...])

    blk = pl.BlockSpec((tr, c), lambda i: (i, 0))
    sh = jax.ShapeDtypeStruct((r, c), F32)
    return pl.pallas_call(
        body, name=name, grid=(r // tr,),
        in_specs=[pl.BlockSpec((p, tr, c), lambda i: (0, i, 0)), blk, blk, blk],
        out_specs=[blk] * 4, out_shape=[sh] * 4,
        compiler_params=_cp(),
    )(parts, w, m, v)


def _my_place():
    x, y, c = lax.axis_index("x"), lax.axis_index("y"), lax.axis_index("c")
    return x, y, c


def _flip(v, bit):
    return 1 - v if bit else v


def _block(ref, axis, idx, size):
    return ref.at[(slice(None),) * axis + (pl.ds(idx * size, size),)]


HBM_SPEC = pl.BlockSpec(memory_space=pltpu.HBM)
SEM_SPEC = pl.BlockSpec(memory_space=pltpu.SEMAPHORE)
DATAFLOW = pltpu.SideEffectType.DATAFLOW_SIDE_EFFECTING


def _hbm(a):
    return pltpu.with_memory_space_constraint(a, pltpu.HBM)


def _remote_copies(jobs, bufs, send_sems, recv_sems):
    return [pltpu.make_async_remote_copy(src_ref=src, dst_ref=dst, send_sem=send_sems.at[q],
                                         recv_sem=recv_sems.at[q], device_id=dev, device_id_type=MESH)
            for q, (src, dst, dev) in enumerate(jobs(bufs))]


def _split_start(name, bufs, jobs, n_jobs, after):
    nb = len(bufs)

    def body(*refs):
        send_sems, recv_sems = refs[nb + 1], refs[nb + 2]
        for cp in _remote_copies(jobs, refs[:nb], send_sems, recv_sems):
            cp.start()
        refs[-1][...] = jnp.zeros_like(refs[-1])

    outs = pl.pallas_call(
        body, name=name,
        out_shape=(pltpu.SemaphoreType.DMA((n_jobs,)), pltpu.SemaphoreType.DMA((n_jobs,)),
                   *[pltpu.HBM(b.shape, b.dtype) for b in bufs], jax.ShapeDtypeStruct((8, LANE), F32)),
        in_specs=[HBM_SPEC] * nb + [ANY],
        out_specs=(SEM_SPEC, SEM_SPEC, *[HBM_SPEC] * nb, VMEM_SPEC),
        input_output_aliases={q: 2 + q for q in range(nb)},
        compiler_params=pltpu.CompilerParams(has_side_effects=DATAFLOW),
    )(*[_hbm(b) for b in bufs], after)
    return outs[0], outs[1], list(outs[2:2 + nb]), outs[-1]


def _split_wait(name, bufs, send_sems, recv_sems, jobs, after):
    nb = len(bufs)

    def body(*refs):
        for cp in _remote_copies(jobs, refs[:nb], refs[nb], refs[nb + 1]):
            cp.wait_send()
            cp.wait_recv()

    outs = pl.pallas_call(
        body, name=name,
        out_shape=tuple(pltpu.HBM(b.shape, b.dtype) for b in bufs),
        in_specs=[HBM_SPEC] * nb + [SEM_SPEC, SEM_SPEC, ANY],
        out_specs=tuple([HBM_SPEC] * nb),
        input_output_aliases={q: q for q in range(nb)},
        compiler_params=pltpu.CompilerParams(has_side_effects=DATAFLOW),
    )(*bufs, send_sems, recv_sems, after)
    return list(outs)


PLACE_TILE_BYTES = 2 * 1024 * 1024


def _own_block_spec(tr, c, nblk, axis):
    if axis == 0:
        return pl.BlockSpec((tr, c), lambda i, me: (me[0] * nblk + i, 0))
    return pl.BlockSpec((tr, c), lambda i, me: (i, me[0]))


def _cast_place(name, w, layer, axis, me):
    _, r, c = w.shape
    tr = _tile(r, max(SUBLANE_BF16, PLACE_TILE_BYTES // (4 * c)))
    nblk = r // tr
    full = (N_DEV * r, c) if axis == 0 else (r, N_DEV * c)

    def body(me_ref, w_ref, o_ref):
        o_ref[...] = w_ref[...].astype(BF16)

    return pl.pallas_call(
        body, name=name,
        grid_spec=pltpu.PrefetchScalarGridSpec(
            num_scalar_prefetch=1, grid=(nblk,),
            in_specs=[pl.BlockSpec((None, tr, c), lambda i, me: (layer, i, 0))],
            out_specs=_own_block_spec(tr, c, nblk, axis)),
        out_shape=jax.ShapeDtypeStruct(full, BF16), compiler_params=_cp(),
    )(me, w)


def _own_place(name, grad, land, layer, axis, me):
    _, _, r, c = land.shape
    tr = _tile(r, max(SUBLANE_BF16, PLACE_TILE_BYTES // (2 * c)))
    nblk = r // tr

    def body(me_ref, g_ref, land_ref, o_ref):
        o_ref[...] = g_ref[...]

    return pl.pallas_call(
        body, name=name,
        grid_spec=pltpu.PrefetchScalarGridSpec(
            num_scalar_prefetch=1, grid=(nblk,),
            in_specs=[_own_block_spec(tr, c, nblk, axis), ANY],
            out_specs=pl.BlockSpec((None, None, tr, c), lambda i, me: (0, layer, i, 0))),
        out_shape=jax.ShapeDtypeStruct(land.shape, land.dtype),
        input_output_aliases={2: 0}, compiler_params=_cp(),
    )(me, grad, land)


def _gather_jobs_a(axes, sizes):
    def jobs(bufs):
        x, y, c = _my_place()
        out = []
        for t, buf in enumerate(bufs):
            blk = _block(buf, axes[t], 4 * x + 2 * y + c, sizes[t])
            for dev in [(x, y, 1 - c), (1 - x, y, c), (x, 1 - y, c), (1 - x, 1 - y, c)]:
                out.append((blk, blk, dev))
        return out
    return jobs


def _gather_jobs_b(axes, sizes):
    nt = len(axes)

    def jobs(bufs):
        x, y, c = _my_place()
        out = []
        for t in range(nt):
            for px, py in [(1 - x, y), (x, 1 - y), (1 - x, 1 - y)]:
                blk = _block(bufs[t], axes[t], 4 * px + 2 * py + c, sizes[t])
                out.append((blk, blk, (x, y, 1 - c)))
        return out
    return jobs


def _exchange_jobs(axes, sizes, layers):
    nt = len(axes)

    def jobs(bufs):
        x, y, c = _my_place()
        out = []
        for k in range(1, N_DEV):
            px, py, pc = _flip(x, k & 4), _flip(y, k & 2), _flip(c, k & 1)
            for t in range(nt):
                out.append((_block(bufs[t], axes[t], 4 * px + 2 * py + pc, sizes[t]),
                            bufs[nt + t].at[k, layers[t]], (px, py, pc)))
        return out
    return jobs


def _gather_begin(name, lands, axes, after):
    sizes = [b.shape[ax] // N_DEV for b, ax in zip(lands, axes)]
    jobs = _gather_jobs_a(axes, sizes)
    send, recv, bufs, token = _split_start(name + "_a", lands, jobs, 4 * len(lands), after)
    return dict(name=name, axes=axes, sizes=sizes, send=send, recv=recv, bufs=bufs, jobs=jobs), token


def _gather_mid(h, after):
    bufs = _split_wait(h["name"] + "_aw", h["bufs"], h["send"], h["recv"], h["jobs"], after)
    jobs = _gather_jobs_b(h["axes"], h["sizes"])
    send, recv, lands, token = _split_start(h["name"] + "_b", bufs, jobs, 3 * len(bufs), after)
    return dict(h, send=send, recv=recv, bufs=lands, jobs=jobs), token


def _gather_end(h, after):
    return _split_wait(h["name"] + "_bw", h["bufs"], h["send"], h["recv"], h["jobs"], after)


def _exchange_begin(name, grads, axes, lands, layers, me, after):
    sizes = [g.shape[ax] // N_DEV for g, ax in zip(grads, axes)]
    lands = [_own_place(f"{name}_place{t}", grads[t], lands[t], layers[t], axes[t], me)
             for t in range(len(grads))]
    jobs = _exchange_jobs(axes, sizes, layers)
    send, recv, bufs, token = _split_start(name + "_s", list(grads) + lands, jobs, 7 * len(grads), after)
    return dict(name=name, n=len(grads), send=send, recv=recv, bufs=bufs, jobs=jobs), token


def _exchange_end(h, after):
    bufs = _split_wait(h["name"] + "_w", h["bufs"], h["send"], h["recv"], h["jobs"], after)
    return bufs[h["n"]:]


def _all_gather_small(name, vec, reduce):
    r = vec.shape[0]

    def body(v_ref, o_ref, *rest):
        if reduce:
            buf, send_sems, recv_sems = rest
        else:
            buf = o_ref
            send_sems, recv_sems = rest
        x, y, c = _my_place()
        mine = 4 * x + 2 * y + c
        buf[mine] = v_ref[...]
        copies = []
        for k in range(1, N_DEV):
            px, py, pc = _flip(x, k & 4), _flip(y, k & 2), _flip(c, k & 1)
            cp = pltpu.make_async_remote_copy(
                src_ref=v_ref, dst_ref=buf.at[mine], send_sem=send_sems.at[k - 1],
                recv_sem=recv_sems.at[k - 1], device_id=(px, py, pc), device_id_type=MESH)
            cp.start()
            copies.append(cp)
        for cp in copies:
            cp.wait()
        if reduce:
            acc = buf[0]
            for q in range(1, N_DEV):
                acc = acc + buf[q]
            o_ref[...] = acc

    scratch = [pltpu.SemaphoreType.DMA((N_DEV - 1,)), pltpu.SemaphoreType.DMA((N_DEV - 1,))]
    if reduce:
        scratch = [pltpu.VMEM((N_DEV, r, LANE), F32)] + scratch
        out_shape = jax.ShapeDtypeStruct((r, LANE), F32)
    else:
        out_shape = jax.ShapeDtypeStruct((N_DEV, r, LANE), F32)
    return pl.pallas_call(
        body, name=name, in_specs=[VMEM_SPEC], out_specs=VMEM_SPEC, out_shape=out_shape,
        scratch_shapes=scratch, compiler_params=_cp(has_side_effects=True),
    )(vec)


def _pack(arrs, row_mult=8):
    flat = jnp.concatenate([a.reshape(-1).astype(F32) for a in arrs])
    n = flat.shape[0]
    rows = -(-n // LANE)
    rows = -(-rows // row_mult) * row_mult
    return jnp.pad(flat, (0, rows * LANE - n)).reshape(rows, LANE)


def _unpack(vec, shapes):
    flat = vec.reshape(-1)
    out, pos = [], 0
    for sh in shapes:
        n = 1
        for s in sh:
            n *= s
        out.append(flat[pos:pos + n].reshape(sh))
        pos += n
    return out


BIG = ["conv_w_pw1", "conv_w_pw2", "mla_w_in", "mla_w_q_up", "mla_w_kv_up", "mla_w_o", "mlp_w1", "mlp_w2"]
BIG_AXIS = {"conv_w_pw1": 2, "conv_w_pw2": 1, "mla_w_in": 1, "mla_w_q_up": 2, "mla_w_kv_up": 2,
            "mla_w_o": 1, "mlp_w1": 2, "mlp_w2": 1}
SMALL_SHARDED = ["conv_w_dw", "mla_q_norm_g", "mla_kv_norm_g"]
REPLICATED = ["norm_mixer_g", "norm_mlp_g", "conv_b_pw1", "conv_b_dw", "conv_ln_g", "conv_ln_b",
              "conv_b_pw2", "final_norm_g"]
WEIGHTS = ["norm_mixer_g", "norm_mlp_g", "conv_w_pw1", "conv_b_pw1", "conv_w_dw", "conv_b_dw",
           "conv_ln_g", "conv_ln_b", "conv_w_pw2", "conv_b_pw2", "mla_w_in", "mla_q_norm_g",
           "mla_kv_norm_g", "mla_w_q_up", "mla_w_kv_up", "mla_w_o", "mlp_w1", "mlp_w2", "final_norm_g"]


def _unshard_last(g, lead):
    nd = g.ndim
    perm = tuple(range(1, nd - 1)) + (0, nd - 1)
    return g.transpose(perm).reshape(lead + (N_DEV * g.shape[-1],))


def _step(w, m, v, x, positions, target):
    s, d = x.shape
    depth = w["norm_mixer_g"].shape[0]
    n_conv, n_mla = w["conv_w_pw1"].shape[0], w["mla_w_in"].shape[0]
    heads = (w["mla_w_q_up"].shape[-1] * N_DEV) // (HEAD_NOPE + HEAD_ROPE)
    rq, rkv = w["mla_w_q_up"].shape[1], w["mla_w_kv_up"].shape[1]
    xi, yi, ci = _my_place()
    mine = 4 * xi + 2 * yi + ci

    def mixer_units(layer):
        names = (["conv_w_pw1", "conv_w_pw2"] if layer % 2 == 0
                 else ["mla_w_in", "mla_w_q_up", "mla_w_kv_up", "mla_w_o"])
        return [(n, layer // 2) for n in names]

    def mlp_units(layer):
        return [("mlp_w1", layer), ("mlp_w2", layer)]

    me_arr = mine.astype(jnp.int32).reshape(1)

    def gather_begin(tag, units, after):
        lands = [_cast_place(f"{tag}_place_{n}", w[n], jl, BIG_AXIS[n] - 1, me_arr) for n, jl in units]
        h, token = _gather_begin(tag, lands, [BIG_AXIS[n] - 1 for n, _ in units], after)
        return dict(h, units=units), token

    full = {}

    def gather_end(h, after):
        full.update(zip(h["units"], _gather_end(h, after)))

    small_shapes = [w[n].shape for n in SMALL_SHARDED]
    gathered = _all_gather_small("gather_small", _pack([w[n] for n in SMALL_SHARDED]), False)

    first_a, tok = gather_begin("gather_0a", mixer_units(0), gathered)
    first_b, tok = gather_begin("gather_0b", mlp_units(0), tok)
    pending, pending_mlp = {}, {}
    if depth > 1:
        pending[1], tok = gather_begin("gather_1", mixer_units(1), tok)
        pending_mlp[1], tok = gather_begin("gather_1b", mlp_units(1), tok)
    h_first = _rms_fwd("rms_mixer_0", x, w["norm_mixer_g"][0].reshape(1, -1) + tok[0, 0])
    first_a, tok = _gather_mid(first_a, h_first)
    gather_end(first_a, tok)

    per_dev = [_unpack(gathered[q], small_shapes) for q in range(N_DEV)]
    w_dw = _unshard_last(jnp.stack([p[0] for p in per_dev]), (n_conv, CONV_W))
    q_gain = _unshard_last(jnp.stack([p[1] for p in per_dev]), (n_mla,))
    kv_gain = _unshard_last(jnp.stack([p[2] for p in per_dev]), (n_mla,))
    w_dw_pad = jnp.pad(w_dw, ((0, 0), (0, HALO - CONV_W), (0, 0)))

    w_in_cols = rq + rkv + HEAD_ROPE

    def pad_w_in(a):
        return jnp.pad(a, ((0, 0), (0, rq + rkv + LANE - w_in_cols)))

    def pad_wq(a):
        return jnp.pad(a.reshape(rq, heads, HEAD_NOPE + HEAD_ROPE),
                       ((0, 0), (0, 0), (0, HEAD_QK_PAD - HEAD_NOPE - HEAD_ROPE))).reshape(rq, heads * HEAD_QK_PAD)

    inv_freq = ROPE_THETA ** (-jnp.arange(0, HEAD_ROPE, 2, dtype=F32) / HEAD_ROPE)
    ang = positions.reshape(s).astype(F32)[:, None] * inv_freq
    cos, sin = jnp.cos(ang), jnp.sin(ang)
    c64 = jnp.concatenate([cos, cos], axis=1)
    s64 = jnp.concatenate([-sin, sin], axis=1)
    zeros64 = jnp.zeros((s, LANE - HEAD_ROPE), F32)
    ck = jnp.concatenate([c64, zeros64], axis=1)
    sk = jnp.concatenate([s64, zeros64], axis=1)
    scale = (HEAD_NOPE + HEAD_ROPE) ** -0.5
    cq = scale * jnp.concatenate([jnp.ones((s, HEAD_NOPE), F32), ck], axis=1)
    sq = scale * jnp.concatenate([jnp.zeros((s, HEAD_NOPE), F32), sk], axis=1)

    def vec(a):
        return a.reshape(1, -1)

    saved = []
    wpad = {}
    for layer in range(depth):
        jl = layer // 2
        h = h_first if layer == 0 else _rms_fwd(f"rms_mixer_{layer}", x,
                                                 vec(w["norm_mixer_g"][layer]) + tok[0, 0])
        if layer % 2 == 0:
            ua, ug, glu = _mm_glu(f"conv_pw1_{layer}", h, full["conv_w_pw1", jl], None, vec(w["conv_b_pw1"][jl]))
            cc, sw = _conv_fwd(f"conv_dw_{layer}", glu, w_dw_pad[jl], vec(w["conv_b_dw"][jl]),
                               vec(w["conv_ln_g"][jl]), vec(w["conv_ln_b"][jl]))
            x1 = _mm_res(f"conv_pw2_{layer}", sw, full["conv_w_pw2", jl], None, x, vec(w["conv_b_pw2"][jl]))
            mix = (h, ua, ug, glu, cc, sw)
        else:
            wpad["in", jl] = pad_w_in(full["mla_w_in", jl])
            wpad["q", jl] = pad_wq(full["mla_w_q_up", jl])
            down = _mm_plain(f"mla_down_{layer}", h, wpad["in", jl], None, "nn", F32)
            qn, kvn, kpe = _mla_mid_fwd(f"mla_mid_{layer}", down, vec(q_gain[jl]), vec(kv_gain[jl]), ck, sk)
            qf = _mm_q(f"mla_q_{layer}", qn, wpad["q", jl], None, cq, sq)
            kf, vv, kft, vt = _mm_kv(f"mla_kv_{layer}", kvn, full["mla_w_kv_up", jl], None, kpe)
            o, lse = _flash_fwd(f"mla_attn_{layer}", qf, kf, vt, heads)
            if layer in pending_mlp:
                pending_mlp[layer], tok = _gather_mid(pending_mlp[layer], o)
                tok, o = lax.optimization_barrier((tok, o))
            x1 = _mm_res(f"mla_out_{layer}", o, full["mla_w_o", jl], None, x)
            if layer in pending_mlp:
                gather_end(pending_mlp.pop(layer), x1)
            mix = (h, down, qn, kvn, qf, kf, kft, vv, o, lse)
        anchor = x1
        if layer == 0:
            first_b, anchor = _gather_mid(first_b, anchor)
        if layer + 2 < depth:
            pending[layer + 2], anchor = gather_begin(f"gather_{layer + 2}",
                                                      mixer_units(layer + 2) + mlp_units(layer + 2), anchor)
        if layer == 0:
            gather_end(first_b, anchor)
        elif layer + 1 < depth:
            pending[layer + 1], anchor = _gather_mid(pending[layer + 1], anchor)
        if anchor is not x1:
            tok = anchor
        h2 = _rms_fwd(f"rms_mlp_{layer}", x1, vec(w["norm_mlp_g"][layer]) + tok[0, 0])
        z, a = _mm_mlp_up(f"mlp_up_{layer}", h2, full["mlp_w1", layer], None)
        x2 = _mm_res(f"mlp_down_{layer}", a, full["mlp_w2", layer], None, x1)
        if layer + 1 < depth:
            if layer == 0:
                pending[1], tok = _gather_mid(pending[1], x2)
                gather_end(pending[1], tok)
            else:
                gather_end(pending[layer + 1], x2)
        saved.append((x, mix, x1, h2, z, a))
        x = x2

    loss_row, g, gb, d_final, _ = _final_loss("final_loss", x, vec(w["final_norm_g"]) + tok[0, 0], target)

    recv = {n: lax.empty((N_DEV,) + w[n].shape, BF16) for n in BIG}

    def exchange_begin(tag, items, after):
        names = [n for n, _, _ in items]
        h, token = _exchange_begin(tag, [gr for _, _, gr in items], [BIG_AXIS[n] - 1 for n in names],
                                   [recv[n] for n in names], [jl for _, jl, _ in items], me_arr, after)
        return dict(h, names=names), token

    def exchange_end(h, after):
        recv.update(zip(h["names"], _exchange_end(h, after)))

    mix_exchanges = []
    d_mixer, d_mlp = [None] * depth, [None] * depth
    d_small = {n: [None] * n_conv for n in ["conv_b_pw1", "conv_w_dw", "conv_b_dw", "conv_ln_g",
                                           "conv_ln_b", "conv_b_pw2"]}
    d_qg, d_kvg = [None] * n_mla, [None] * n_mla
    for layer in reversed(range(depth)):
        jl = layer // 2
        x0, mix, x1, h2, z, a = saved[layer]
        colsum_g = None
        dz = _mm_mlp_dz(f"mlp_dz_{layer}", gb, full["mlp_w2", layer], None, z)
        dw2 = _mm_wgrad(f"mlp_dw2_{layer}", a, gb)
        w2_exchange, tok = exchange_begin(f"exchange_w2_{layer}", [("mlp_w2", layer, dw2)], dz)
        tok, dz = lax.optimization_barrier((tok, dz))
        dh2 = _mm_plain(f"mlp_dh_{layer}", dz, full["mlp_w1", layer], None, "nt", F32)
        dw1 = _mm_wgrad(f"mlp_dw1_{layer}", h2, dz)
        w1_exchange, tok = exchange_begin(f"exchange_w1_{layer}", [("mlp_w1", layer, dw1)], tok)
        g, gb, d_mlp[layer], colsum_g = _rms_bwd(f"rms_mlp_bwd_{layer}", x1,
                                                 vec(w["norm_mlp_g"][layer]) + tok[0, 0], dh2, g)
        for hx in mix_exchanges:
            exchange_end(hx, g)
        if layer % 2 == 0:
            h, ua, ug, glu, cc, sw = mix
            d_small["conv_b_pw2"][jl] = colsum_g.reshape(-1)
            dsw = _mm_plain(f"conv_ds_{layer}", gb, full["conv_w_pw2", jl], None, "nt", F32)
            dwp2 = _mm_wgrad(f"conv_dw2_{layer}", sw, gb)
            hx2, tok = exchange_begin(f"exchange_pw2_{layer}", [("conv_w_pw2", jl, dwp2)], dsw)
            dc, dlg, dlb, dbdw = _conv_bwd_ln(f"conv_ln_bwd_{layer}", dsw, cc,
                                              vec(w["conv_ln_g"][jl]) + tok[0, 0], vec(w["conv_ln_b"][jl]))
            du, dwdw, dbu = _conv_bwd_dw(f"conv_dw_bwd_{layer}", dc, glu, ua, ug, w_dw_pad[jl])
            d_small["conv_ln_g"][jl] = dlg.reshape(-1)
            d_small["conv_ln_b"][jl] = dlb.reshape(-1)
            d_small["conv_b_dw"][jl] = dbdw.reshape(-1)
            d_small["conv_w_dw"][jl] = dwdw[:CONV_W]
            d_small["conv_b_pw1"][jl] = dbu.reshape(-1)
            dwp1 = _mm_wgrad(f"conv_dw1_{layer}", h, du)
            hx1, tok = exchange_begin(f"exchange_pw1_{layer}", [("conv_w_pw1", jl, dwp1)], dbu)
            tok, du = lax.optimization_barrier((tok, du))
            dh = _mm_plain(f"conv_dh_{layer}", du, full["conv_w_pw1", jl], None, "nt", F32)
            mix_exchanges = [hx2, hx1]
        else:
            h, down, qn, kvn, qf, kf, kft, vv, o, lse = mix
            do = _mm_plain(f"mla_do_{layer}", gb, full["mla_w_o", jl], None, "nt", BF16)
            dwo = _mm_wgrad(f"mla_dwo_{layer}", o, gb)
            dq, dkv, dkpe = _flash_bwd(f"mla_attn_bwd_{layer}", qf, kf, kft, vv, do, o, lse, cq, sq, ck, sk, heads)
            dqn = _mm_plain(f"mla_dqn_{layer}", dq, wpad["q", jl], None, "nt", F32)
            dwq = _mm_wgrad(f"mla_dwq_{layer}", qn, dq).reshape(rq, heads, HEAD_QK_PAD)[
                :, :, :HEAD_NOPE + HEAD_ROPE].reshape(rq, heads * (HEAD_NOPE + HEAD_ROPE))
            dkvn = _mm_plain(f"mla_dkvn_{layer}", dkv, full["mla_w_kv_up", jl], None, "nt", F32)
            dwkv = _mm_wgrad(f"mla_dwkv_{layer}", kvn, dkv)
            ddown, d_qg[jl], d_kvg[jl] = _mla_mid_bwd(f"mla_mid_bwd_{layer}", down, vec(q_gain[jl]),
                                                      vec(kv_gain[jl]), dqn, dkvn, dkpe)
            dh = _mm_plain(f"mla_dh_{layer}", ddown, wpad["in", jl], None, "nt", F32)
            dwin = _mm_wgrad(f"mla_dwin_{layer}", h, ddown)[:, :w_in_cols]
            items = [("mla_w_in", jl, dwin), ("mla_w_q_up", jl, dwq), ("mla_w_kv_up", jl, dwkv),
                     ("mla_w_o", jl, dwo)]
            hx, tok = exchange_begin(f"exchange_mix_{layer}", items, dh)
            mix_exchanges = [hx]
        g, gb, d_mixer[layer], _ = _rms_bwd(f"rms_mixer_bwd_{layer}", x0,
                                            vec(w["norm_mixer_g"][layer]) + tok[0, 0], dh, g)
        exchange_end(w2_exchange, g)
        exchange_end(w1_exchange, g)
    grad_x = g

    out = {}

    def adamw_big(n):
        sh = w[n].shape
        r, c = sh[0] * sh[1], sh[2]
        res = _adamw(f"adamw_{n}", recv[n].reshape(N_DEV, r, c), w[n].reshape(r, c),
                     m[n].reshape(r, c), v[n].reshape(r, c))
        out[n] = [t.reshape(sh) for t in res]

    late = [n for hx in mix_exchanges for n in hx["names"]]
    early = [n for n in BIG if n not in late]
    for n in early:
        adamw_big(n)
    anchor = out[early[-1]][1]
    for hx in mix_exchanges:
        exchange_end(hx, anchor)
    for n in late:
        adamw_big(n)

    small_full = {
        "norm_mixer_g": jnp.concatenate(d_mixer, axis=0), "norm_mlp_g": jnp.concatenate(d_mlp, axis=0),
        "conv_b_pw1": jnp.stack(d_small["conv_b_pw1"]), "conv_b_dw": jnp.stack(d_small["conv_b_dw"]),
        "conv_ln_g": jnp.stack(d_small["conv_ln_g"]), "conv_ln_b": jnp.stack(d_small["conv_ln_b"]),
        "conv_b_pw2": jnp.stack(d_small["conv_b_pw2"]), "final_norm_g": d_final.reshape(-1),
        "conv_w_dw": jnp.stack(d_small["conv_w_dw"]),
        "mla_q_norm_g": jnp.concatenate(d_qg, axis=0), "mla_kv_norm_g": jnp.concatenate(d_kvg, axis=0),
    }
    names = REPLICATED + SMALL_SHARDED
    packed, _ = lax.optimization_barrier((_pack([small_full[n] for n in names]), anchor))
    summed = _unpack(_all_gather_small("reduce_small", packed, True), [small_full[n].shape for n in names])
    summed = dict(zip(names, summed))
    for n in SMALL_SHARDED:
        width = w[n].shape[-1]
        summed[n] = lax.dynamic_slice_in_dim(summed[n], mine * width, width, axis=summed[n].ndim - 1)
    for group, tag in ((REPLICATED, "replicated"), (SMALL_SHARDED, "small_sharded")):
        shapes = [w[n].shape for n in group]
        res = _adamw(f"adamw_{tag}", _pack([summed[n] for n in group])[None],
                     _pack([w[n] for n in group]), _pack([m[n] for n in group]), _pack([v[n] for n in group]))
        unpacked = [_unpack(t, shapes) for t in res]
        for q, n in enumerate(group):
            out[n] = [unpacked[0][q], unpacked[1][q], unpacked[2][q], unpacked[3][q]]

    loss = lax.psum(loss_row[0, 0], ("x", "y", "c"))
    return loss, grad_x, out


def kernel(x, positions, norm_mixer_g, norm_mlp_g, conv_w_pw1, conv_b_pw1, conv_w_dw, conv_b_dw, conv_ln_g, conv_ln_b, conv_w_pw2, conv_b_pw2, mla_w_in, mla_q_norm_g, mla_kv_norm_g, mla_w_q_up, mla_w_kv_up, mla_w_o, mlp_w1, mlp_w2, final_norm_g, loss_target, m_norm_mixer_g, m_norm_mlp_g, m_conv_w_pw1, m_conv_b_pw1, m_conv_w_dw, m_conv_b_dw, m_conv_ln_g, m_conv_ln_b, m_conv_w_pw2, m_conv_b_pw2, m_mla_w_in, m_mla_q_norm_g, m_mla_kv_norm_g, m_mla_w_q_up, m_mla_w_kv_up, m_mla_w_o, m_mlp_w1, m_mlp_w2, m_final_norm_g, v_norm_mixer_g, v_norm_mlp_g, v_conv_w_pw1, v_conv_b_pw1, v_conv_w_dw, v_conv_b_dw, v_conv_ln_g, v_conv_ln_b, v_conv_w_pw2, v_conv_b_pw2, v_mla_w_in, v_mla_q_norm_g, v_mla_kv_norm_g, v_mla_w_q_up, v_mla_w_kv_up, v_mla_w_o, v_mlp_w1, v_mlp_w2, v_final_norm_g):
    ws = (norm_mixer_g, norm_mlp_g, conv_w_pw1, conv_b_pw1, conv_w_dw, conv_b_dw, conv_ln_g, conv_ln_b,
          conv_w_pw2, conv_b_pw2, mla_w_in, mla_q_norm_g, mla_kv_norm_g, mla_w_q_up, mla_w_kv_up, mla_w_o,
          mlp_w1, mlp_w2, final_norm_g)
    ms = (m_norm_mixer_g, m_norm_mlp_g, m_conv_w_pw1, m_conv_b_pw1, m_conv_w_dw, m_conv_b_dw, m_conv_ln_g,
          m_conv_ln_b, m_conv_w_pw2, m_conv_b_pw2, m_mla_w_in, m_mla_q_norm_g, m_mla_kv_norm_g,
          m_mla_w_q_up, m_mla_w_kv_up, m_mla_w_o, m_mlp_w1, m_mlp_w2, m_final_norm_g)
    vs = (v_norm_mixer_g, v_norm_mlp_g, v_conv_w_pw1, v_conv_b_pw1, v_conv_w_dw, v_conv_b_dw, v_conv_ln_g,
          v_conv_ln_b, v_conv_w_pw2, v_conv_b_pw2, v_mla_w_in, v_mla_q_norm_g, v_mla_kv_norm_g,
          v_mla_w_q_up, v_mla_w_kv_up, v_mla_w_o, v_mlp_w1, v_mlp_w2, v_final_norm_g)
    w, m, v = dict(zip(WEIGHTS, ws)), dict(zip(WEIGHTS, ms)), dict(zip(WEIGHTS, vs))
    s, d = x.shape[-2], x.shape[-1]
    loss, grad_x, out = _step(w, m, v, x.reshape(s, d), positions, loss_target.reshape(s, d))
    grads = [out[n][0] for n in WEIGHTS]
    deltas = [out[n][1] for n in WEIGHTS]
    new_m = [out[n][2] for n in WEIGHTS]
    new_v = [out[n][3] for n in WEIGHTS]
    return (loss, grad_x.reshape(x.shape), *grads, *deltas, *new_m, *new_v)
```

```python
import functools

import jax
import jax.numpy as jnp
from jax import lax
from jax.experimental import pallas as pl
from jax.experimental.pallas import tpu as pltpu

F32 = jnp.float32
BF16 = jnp.bfloat16

NORM_EPS = 1e-6
LN_EPS = 1e-5
ROPE_THETA = 10000.0
CHUNK_BITS = 6
HEAD_NOPE = 128
HEAD_ROPE = 64
HEAD_V = 128
HEAD_QK_PAD = 256
CONV_W = 31
HALO = 32
N_DEV = 8

ADAM_LR = 0.001
ADAM_B1 = 0.9
ADAM_B2 = 0.999
ADAM_EPS = 1e-08
ADAM_WD = 0.01
ADAM_STEP = 10

V7X_VMEM_BYTES = 64 * 1024 * 1024
VMEM_LIMIT = (V7X_VMEM_BYTES * 3) // 4
LANE = 128

MESH = pl.DeviceIdType.MESH
ANY = pl.BlockSpec(memory_space=pl.ANY)
VMEM_SPEC = pl.BlockSpec(memory_space=pltpu.VMEM)


def _cp(**kw):
    return pltpu.CompilerParams(vmem_limit_bytes=VMEM_LIMIT, **kw)


SUBLANE = 8
SUBLANE_BF16 = 16

TM_PREF = 1024
TN_PREF = 1024
TK_PREF = 2048


def _tile(n, pref, mult=SUBLANE_BF16):
    if n <= pref + pref // 2:
        return n
    t = (pref // mult) * mult
    while t >= mult:
        if n % t == 0:
            return t
        t -= mult
    return n


def _sigmoid(x):
    return 1.0 / (1.0 + jnp.exp(-x))


def _rot_half(x):
    n = x.shape[-1]
    lane = lax.broadcasted_iota(jnp.int32, x.shape, x.ndim - 1)
    first = (lane & 63) < 32
    return jnp.where(first, pltpu.roll(x, n - 32, x.ndim - 1), pltpu.roll(x, 32, x.ndim - 1))


def _rope(x, c, s):
    return x * c + _rot_half(x) * s


def _rope_t(d, c, s):
    return d * c + _rot_half(d * s)


def _chunk_mask_t(t):
    row = lax.broadcasted_iota(jnp.int32, (t, t), 0)
    col = lax.broadcasted_iota(jnp.int32, (t, t), 1)
    return jnp.right_shift(row, CHUNK_BITS) <= jnp.right_shift(col, CHUNK_BITS)


def _rms_fwd(name, x, g):
    t, d = x.shape
    tm = _tile(t, 512)

    def body(x_ref, g_ref, o_ref):
        xf = x_ref[...]
        r = lax.rsqrt(jnp.mean(xf * xf, axis=-1, keepdims=True) + NORM_EPS)
        o_ref[...] = (xf * r * g_ref[...]).astype(o_ref.dtype)

    return pl.pallas_call(
        body, name=name, grid=(t // tm,),
        in_specs=[pl.BlockSpec((tm, d), lambda i: (i, 0)), pl.BlockSpec((1, d), lambda i: (0, 0))],
        out_specs=pl.BlockSpec((tm, d), lambda i: (i, 0)),
        out_shape=jax.ShapeDtypeStruct((t, d), BF16),
        compiler_params=_cp(),
    )(x, g)


def _rms_bwd_math(xf, g, dy):
    r = lax.rsqrt(jnp.mean(xf * xf, axis=-1, keepdims=True) + NORM_EPS)
    xh = xf * r
    dg = jnp.sum(dy * xh, axis=0, keepdims=True)
    dxh = dy * g
    dx = r * (dxh - xh * jnp.mean(dxh * xh, axis=-1, keepdims=True))
    return dx, dg


def _final_loss(name, x, g, target):
    t, d = x.shape
    tm = _tile(t, 256)

    def body(x_ref, g_ref, t_ref, loss_ref, dx_ref, dxb_ref, dg_ref, cs_ref):
        @pl.when(pl.program_id(0) == 0)
        def _():
            loss_ref[...] = jnp.zeros_like(loss_ref)
            dg_ref[...] = jnp.zeros_like(dg_ref)
            cs_ref[...] = jnp.zeros_like(cs_ref)

        xf = x_ref[...]
        gg = g_ref[...]
        r = lax.rsqrt(jnp.mean(xf * xf, axis=-1, keepdims=True) + NORM_EPS)
        err = xf * r * gg - t_ref[...]
        part = 0.5 * jnp.sum(jnp.mean(err * err, axis=-1, keepdims=True), axis=0, keepdims=True)
        loss_ref[...] += jnp.broadcast_to(part, loss_ref.shape)
        dx, dg = _rms_bwd_math(xf, gg, err * (1.0 / d))
        dx_ref[...] = dx
        dxb_ref[...] = dx.astype(BF16)
        dg_ref[...] += dg
        cs_ref[...] += jnp.sum(dx, axis=0, keepdims=True)

    row = pl.BlockSpec((tm, d), lambda i: (i, 0))
    vec = pl.BlockSpec((1, d), lambda i: (0, 0))
    return pl.pallas_call(
        body, name=name, grid=(t // tm,),
        in_specs=[row, vec, row],
        out_specs=[pl.BlockSpec((1, LANE), lambda i: (0, 0)), row, row, vec, vec],
        out_shape=[jax.ShapeDtypeStruct((1, LANE), F32), jax.ShapeDtypeStruct((t, d), F32),
                   jax.ShapeDtypeStruct((t, d), BF16), jax.ShapeDtypeStruct((1, d), F32),
                   jax.ShapeDtypeStruct((1, d), F32)],
        compiler_params=_cp(dimension_semantics=("arbitrary",)),
    )(x, g, target)


_DIMS = {
    "nn": (((1,), (0,)), ((), ())),
    "nt": (((1,), (1,)), ((), ())),
    "tn": (((0,), (0,)), ((), ())),
}


def _mm(name, a, bs, *, mode, m, n, k, epilogue, out_shape, out_specs, extras=(), extra_specs=(),
        a_lead=None, aliases=None, tn_div=1, tiles=None, acc_refs=False):
    tm, tn, tk = tiles or _tiles(m, n, k, tn_div)
    nk = k // tk
    nb, ne = len(bs), len(extras)
    no = len(out_shape)
    dims = _DIMS[mode]

    def with_lead(shape, idx, lead):
        if lead is None:
            return pl.BlockSpec(shape, idx)
        return pl.BlockSpec((None,) + shape, lambda i, j, kk: (lead,) + idx(i, j, kk))

    if mode == "tn":
        a_spec = with_lead((tk, tm), lambda i, j, kk: (kk, i), a_lead)
    else:
        a_spec = with_lead((tm, tk), lambda i, j, kk: (i, kk), a_lead)
    b_specs = []
    for _, lead, off in bs:
        if mode == "nt":
            b_specs.append(with_lead((tn, tk), lambda i, j, kk, off=off: (j + off, kk), lead))
        else:
            b_specs.append(with_lead((tk, tn), lambda i, j, kk, off=off: (kk, j + off), lead))

    def body(*refs):
        a_ref = refs[0]
        b_refs = refs[1:1 + nb]
        ex = refs[1 + nb:1 + nb + ne]
        outs = refs[1 + nb + ne:1 + nb + ne + no]
        accs = refs[1 + nb + ne + no:]

        def part(b_ref):
            return lax.dot_general(a_ref[...], b_ref[...], dims, preferred_element_type=F32)

        if nk == 1 and acc_refs:
            for acc, b_ref in zip(accs, b_refs):
                acc[...] = part(b_ref)
            epilogue(accs, ex, outs)
            return
        if nk == 1:
            epilogue([part(b_ref) for b_ref in b_refs], ex, outs)
            return
        kk = pl.program_id(2)

        @pl.when(kk == 0)
        def _():
            for acc, b_ref in zip(accs, b_refs):
                acc[...] = part(b_ref)

        @pl.when(kk > 0)
        def _():
            for acc, b_ref in zip(accs, b_refs):
                acc[...] += part(b_ref)

        @pl.when(kk == nk - 1)
        def _():
            epilogue(accs if acc_refs else [acc[...] for acc in accs], ex, outs)

    scratch = [pltpu.VMEM((tm, tn), F32) for _ in range(nb)] if nk > 1 or acc_refs else []
    return pl.pallas_call(
        body, name=name, grid=(m // tm, n // tn, nk),
        in_specs=[a_spec] + b_specs + list(extra_specs),
        out_specs=list(out_specs), out_shape=list(out_shape), scratch_shapes=scratch,
        input_output_aliases=aliases or {},
        compiler_params=_cp(dimension_semantics=("arbitrary", "arbitrary", "arbitrary")),
    )(a, *[b for b, _, _ in bs], *extras), (tm, tn, tk)


def _ij(tm, tn):
    return pl.BlockSpec((tm, tn), lambda i, j, kk: (i, j))


def _tiles(m, n, k, tn_div=1):
    return _tile(m, TM_PREF), _tile(n, TN_PREF // tn_div, LANE), _tile(k, TK_PREF, LANE)


def _mm_plain(name, a, b, b_lead, mode, out_dtype):
    m, k = a.shape
    n = b.shape[-1] if mode == "nn" else b.shape[-2]
    tm, tn, _ = _tiles(m, n, k)

    def epilogue(accs, ex, outs):
        outs[0][...] = accs[0].astype(out_dtype)

    return _mm(name, a, [(b, b_lead, 0)], mode=mode, m=m, n=n, k=k, epilogue=epilogue,
               out_shape=[jax.ShapeDtypeStruct((m, n), out_dtype)], out_specs=[_ij(tm, tn)])[0][0]


def _mm_res(name, a, b, b_lead, resid, bias=None):
    m, k = a.shape
    n = b.shape[-1]
    tm, tn, _ = _tiles(m, n, k)
    extras, specs = [resid], [_ij(tm, tn)]
    if bias is not None:
        extras.append(bias)
        specs.append(pl.BlockSpec((1, tn), lambda i, j, kk: (0, j)))

    def epilogue(accs, ex, outs):
        y = ex[0][...] + accs[0]
        if bias is not None:
            y = y + ex[1][...]
        outs[0][...] = y

    return _mm(name, a, [(b, b_lead, 0)], mode="nn", m=m, n=n, k=k, epilogue=epilogue,
               extras=extras, extra_specs=specs,
               out_shape=[jax.ShapeDtypeStruct((m, n), F32)], out_specs=[_ij(tm, tn)])[0][0]


def _mm_mlp_up(name, h, w1, lead):
    m, k = h.shape
    n = w1.shape[-1]
    tm, tn, _ = _tiles(m, n, k)

    def epilogue(accs, ex, outs):
        z = accs[0]
        outs[0][...] = z.astype(BF16)
        r = jnp.maximum(z, 0.0)
        outs[1][...] = (r * r).astype(BF16)

    sh = jax.ShapeDtypeStruct((m, n), BF16)
    return _mm(name, h, [(w1, lead, 0)], mode="nn", m=m, n=n, k=k, epilogue=epilogue,
               out_shape=[sh, sh], out_specs=[_ij(tm, tn), _ij(tm, tn)])[0]


def _mm_mlp_dz(name, g, w2, lead, z):
    m, k = g.shape
    n = w2.shape[-2]
    tm, tn, _ = _tiles(m, n, k)

    def epilogue(accs, ex, outs):
        outs[0][...] = (accs[0] * (2.0 * jnp.maximum(ex[0][...].astype(F32), 0.0))).astype(BF16)

    return _mm(name, g, [(w2, lead, 0)], mode="nt", m=m, n=n, k=k, epilogue=epilogue,
               extras=[z], extra_specs=[_ij(tm, tn)],
               out_shape=[jax.ShapeDtypeStruct((m, n), BF16)], out_specs=[_ij(tm, tn)])[0][0]


def _mm_glu(name, h, w, lead, bias):
    m, k = h.shape
    n = w.shape[-1] // 2
    tm, tn, _ = _tiles(m, n, k, 2)
    off = n // tn

    def epilogue(accs, ex, outs):
        a = accs[0] + ex[0][...]
        gate = accs[1] + ex[1][...]
        outs[0][...] = a.astype(BF16)
        outs[1][...] = gate.astype(BF16)
        outs[2][...] = a * _sigmoid(gate)

    shb = jax.ShapeDtypeStruct((m, n), BF16)
    return _mm(name, h, [(w, lead, 0), (w, lead, off)], mode="nn", m=m, n=n, k=k, epilogue=epilogue,
               extras=[bias, bias],
               extra_specs=[pl.BlockSpec((1, tn), lambda i, j, kk: (0, j)),
                            pl.BlockSpec((1, tn), lambda i, j, kk: (0, j + off))],
               out_shape=[shb, shb, jax.ShapeDtypeStruct((m, n), F32)],
               out_specs=[_ij(tm, tn)] * 3, tn_div=2)[0]


def _mm_q(name, qn, wq_pad, lead, cq, sq):
    m, k = qn.shape
    n = wq_pad.shape[-1]
    tm, tn, _ = _tiles(m, n, k)
    scale = (HEAD_NOPE + HEAD_ROPE) ** -0.5

    def epilogue(accs, ex, outs):
        c, s = ex[0][:, HEAD_NOPE:], ex[1][:, HEAD_NOPE:]
        for hh in range(tn // HEAD_QK_PAD):
            base = hh * HEAD_QK_PAD
            outs[0][:, base:base + HEAD_NOPE] = (accs[0][:, base:base + HEAD_NOPE] * scale).astype(BF16)
            outs[0][:, base + HEAD_NOPE:base + HEAD_QK_PAD] = _rope(
                accs[0][:, base + HEAD_NOPE:base + HEAD_QK_PAD], c, s).astype(BF16)

    tab = pl.BlockSpec((tm, HEAD_QK_PAD), lambda i, j, kk: (i, 0))
    return _mm(name, qn, [(wq_pad, lead, 0)], mode="nn", m=m, n=n, k=k, epilogue=epilogue,
               extras=[cq, sq], extra_specs=[tab, tab],
               out_shape=[jax.ShapeDtypeStruct((m, n), BF16)], out_specs=[_ij(tm, tn)])[0][0]


def _mm_kv(name, kvn, wkv, lead, kpe):
    m, k = kvn.shape
    n = wkv.shape[-1]
    tm, tn, _ = _tiles(m, n, k)
    heads = tn // (HEAD_NOPE + HEAD_V)

    def epilogue(accs, ex, outs):
        acc = accs[0]
        pe = ex[0][...].astype(F32)
        kparts, vparts = [], []
        for hh in range(heads):
            base = hh * (HEAD_NOPE + HEAD_V)
            kparts += [acc[:, base:base + HEAD_NOPE], pe]
            vparts.append(acc[:, base + HEAD_NOPE:base + HEAD_NOPE + HEAD_V])
        kf = jnp.concatenate(kparts, axis=1)
        vv = jnp.concatenate(vparts, axis=1) if heads > 1 else vparts[0]
        outs[0][...] = kf.astype(BF16)
        outs[1][...] = vv.astype(BF16)
        outs[2][...] = kf.T.astype(BF16)
        outs[3][...] = vv.T.astype(BF16)

    def ji(tn_, tm_):
        return pl.BlockSpec((tn_, tm_), lambda i, j, kk: (j, i))

    return _mm(name, kvn, [(wkv, lead, 0)], mode="nn", m=m, n=n, k=k, epilogue=epilogue,
               extras=[kpe], extra_specs=[pl.BlockSpec((tm, LANE), lambda i, j, kk: (i, 0))],
               out_shape=[jax.ShapeDtypeStruct((m, n), BF16), jax.ShapeDtypeStruct((m, n // 2), BF16),
                          jax.ShapeDtypeStruct((n, m), BF16), jax.ShapeDtypeStruct((n // 2, m), BF16)],
               out_specs=[_ij(tm, tn), _ij(tm, tn // 2), ji(tn, tm), ji(tn // 2, tm)])[0]


def _mm_wgrad(name, a, b):
    t, m = a.shape
    n = b.shape[-1]
    tm, tn, _ = _tiles(m, n, t)

    def epilogue(accs, ex, outs):
        outs[0][...] = accs[0].astype(BF16)

    return _mm(name, a, [(b, None, 0)], mode="tn", m=m, n=n, k=t, epilogue=epilogue,
               out_shape=[jax.ShapeDtypeStruct((m, n), BF16)], out_specs=[_ij(tm, tn)])[0][0]


RMS_BWD_ROWS = 512
RMS_BWD_CHUNK = 64
RMS_BWD_TK = 1024


def _mm_rms_bwd(name, a, b, x, gain, resid):
    m, k = a.shape
    d = b.shape[-2]
    tm, tk = _tile(m, RMS_BWD_ROWS), _tile(k, RMS_BWD_TK, LANE)
    rc = min(RMS_BWD_CHUNK, tm)

    def epilogue(accs, ex, outs):
        x_ref, g_ref, r_ref = ex
        dx_ref, dxb_ref, dg_ref, cs_ref = outs

        @pl.when(pl.program_id(0) == 0)
        def _():
            dg_ref[...] = jnp.zeros_like(dg_ref)
            cs_ref[...] = jnp.zeros_like(cs_ref)

        def chunk(r, carry):
            rows = pl.ds(pl.multiple_of(r * rc, rc), rc)
            dx, dg = _rms_bwd_math(x_ref[rows, :], g_ref[...], accs[0][rows, :])
            tot = r_ref[rows, :] + dx
            dx_ref[rows, :] = tot
            dxb_ref[rows, :] = tot.astype(BF16)
            dg_ref[...] += dg
            cs_ref[...] += jnp.sum(tot, axis=0, keepdims=True)
            return carry

        lax.fori_loop(0, tm // rc, chunk, 0)

    row = pl.BlockSpec((tm, d), lambda i, j, kk: (i, 0))
    vec = pl.BlockSpec((1, d), lambda i, j, kk: (0, 0))
    return _mm(name, a, [(b, None, 0)], mode="nt", m=m, n=d, k=k, epilogue=epilogue,
               extras=[x, gain, resid], extra_specs=[row, vec, row],
               out_shape=[jax.ShapeDtypeStruct((m, d), F32), jax.ShapeDtypeStruct((m, d), BF16),
                          jax.ShapeDtypeStruct((1, d), F32), jax.ShapeDtypeStruct((1, d), F32)],
               out_specs=[row, row, vec, vec], tiles=(tm, d, tk), acc_refs=True)[0]


CONV_ROWS = 256
CONV_RT = 64
CONV_CW = 256
CONV_LR = 32


def _ln_stats(c):
    mu = jnp.mean(c, axis=-1, keepdims=True)
    xc = c - mu
    rstd = lax.rsqrt(jnp.mean(xc * xc, axis=-1, keepdims=True) + LN_EPS)
    return xc * rstd, rstd


def _conv_fwd(name, glu, w_dw, b_dw, ln_g, ln_b):
    t, d = glu.shape
    tt = _tile(t, CONV_ROWS)
    rt, cw, lr = min(CONV_RT, tt), min(CONV_CW, d), min(CONV_LR, tt)
    hb = tt // HALO

    def body(gc_ref, gp_ref, w_ref, b_ref, lg_ref, lb_ref, c_ref, s_ref, buf, win):
        i = pl.program_id(0)
        buf[0:HALO, :] = jnp.where(i > 0, gp_ref[...], 0.0)
        buf[HALO:HALO + tt, :] = gc_ref[...]

        def chunk(cb, carry):
            col = pl.ds(pl.multiple_of(cb * cw, cw), cw)
            for r0 in range(0, tt, rt):
                acc = jnp.broadcast_to(b_ref[:, col], (rt, cw))
                for b in range(SUBLANE):
                    amax = (CONV_W - 1 - b) // SUBLANE
                    lo = r0 + HALO - (CONV_W - 1) + b
                    rows = rt + SUBLANE * amax
                    win[0:rows, :] = buf[lo:lo + rows, col]
                    for a in range(amax + 1):
                        k = SUBLANE * a + b
                        acc = acc + w_ref[k:k + 1, col] * win[SUBLANE * a:SUBLANE * a + rt, :]
                c_ref[r0:r0 + rt, col] = acc
            return carry

        lax.fori_loop(0, d // cw, chunk, 0)

        def ln(r, carry):
            rows = pl.ds(pl.multiple_of(r * lr, lr), lr)
            xh, _ = _ln_stats(c_ref[rows, :])
            y = xh * lg_ref[...] + lb_ref[...]
            s_ref[rows, :] = (y * _sigmoid(y)).astype(BF16)
            return carry

        lax.fori_loop(0, tt // lr, ln, 0)

    row = pl.BlockSpec((tt, d), lambda i: (i, 0))
    vec = pl.BlockSpec((1, d), lambda i: (0, 0))
    return pl.pallas_call(
        body, name=name, grid=(t // tt,),
        in_specs=[row, pl.BlockSpec((HALO, d), lambda i: (jnp.maximum(i * hb - 1, 0), 0)),
                  pl.BlockSpec((HALO, d), lambda i: (0, 0)), vec, vec, vec],
        out_specs=[row, row],
        out_shape=[jax.ShapeDtypeStruct((t, d), F32), jax.ShapeDtypeStruct((t, d), BF16)],
        scratch_shapes=[pltpu.VMEM((HALO + tt, d), F32), pltpu.VMEM((rt + HALO, cw), F32)],
        compiler_params=_cp(dimension_semantics=("arbitrary",)),
    )(glu, glu, w_dw, b_dw, ln_g, ln_b)


def _conv_bwd_ln(name, ds, c, ln_g, ln_b):
    t, d = c.shape
    tt = _tile(t, CONV_ROWS)
    lr = min(CONV_LR, tt)

    def body(ds_ref, c_ref, lg_ref, lb_ref, dc_ref, dg_ref, db_ref, dbdw_ref):
        @pl.when(pl.program_id(0) == 0)
        def _():
            dg_ref[...] = jnp.zeros_like(dg_ref)
            db_ref[...] = jnp.zeros_like(db_ref)
            dbdw_ref[...] = jnp.zeros_like(dbdw_ref)

        def chunk(r, carry):
            rows = pl.ds(pl.multiple_of(r * lr, lr), lr)
            xh, rstd = _ln_stats(c_ref[rows, :])
            g = lg_ref[...]
            y = xh * g + lb_ref[...]
            sg = _sigmoid(y)
            dy = ds_ref[rows, :] * (sg * (1.0 + y * (1.0 - sg)))
            dxh = dy * g
            dc = rstd * (dxh - jnp.mean(dxh, axis=-1, keepdims=True)
                         - xh * jnp.mean(dxh * xh, axis=-1, keepdims=True))
            dc_ref[rows, :] = dc
            dg_ref[...] += jnp.sum(dy * xh, axis=0, keepdims=True)
            db_ref[...] += jnp.sum(dy, axis=0, keepdims=True)
            dbdw_ref[...] += jnp.sum(dc, axis=0, keepdims=True)
            return carry

        lax.fori_loop(0, tt // lr, chunk, 0)

    row = pl.BlockSpec((tt, d), lambda i: (i, 0))
    vec = pl.BlockSpec((1, d), lambda i: (0, 0))
    vsh = jax.ShapeDtypeStruct((1, d), F32)
    return pl.pallas_call(
        body, name=name, grid=(t // tt,),
        in_specs=[row, row, vec, vec], out_specs=[row, vec, vec, vec],
        out_shape=[jax.ShapeDtypeStruct((t, d), F32), vsh, vsh, vsh],
        compiler_params=_cp(dimension_semantics=("arbitrary",)),
    )(ds, c, ln_g, ln_b)


def _conv_bwd_dw(name, dc, glu, ua, ug, w_dw):
    t, d = dc.shape
    tt = _tile(t, CONV_ROWS)
    rt, cw = min(CONV_RT, tt), min(CONV_CW, d)
    hb = tt // HALO
    nt = t // tt

    def body(dcc_ref, dcn_ref, gc_ref, gp_ref, ua_ref, ug_ref, w_ref,
             du_ref, dw_ref, dbu_ref, dbuf, gbuf, wacc, dwin, gwin):
        i = pl.program_id(0)

        @pl.when(i == 0)
        def _():
            wacc[...] = jnp.zeros_like(wacc)
            dbu_ref[...] = jnp.zeros_like(dbu_ref)

        dbuf[0:tt, :] = dcc_ref[...]
        dbuf[tt:tt + HALO, :] = jnp.where(i < nt - 1, dcn_ref[...], 0.0)
        gbuf[0:HALO, :] = jnp.where(i > 0, gp_ref[...], 0.0)
        gbuf[HALO:HALO + tt, :] = gc_ref[...]

        def chunk(cb, carry):
            c0 = pl.multiple_of(cb * cw, cw)
            col = pl.ds(c0, cw)
            colg = pl.ds(pl.multiple_of(d + cb * cw, cw), cw)
            for r0 in range(0, tt, rt):
                dcr = dbuf[r0:r0 + rt, col]
                dgl = jnp.zeros((rt, cw), F32)
                for b in range(SUBLANE):
                    amax = (CONV_W - 1 - b) // SUBLANE
                    hi = r0 + (CONV_W - 1) - b - SUBLANE * amax
                    rows = rt + SUBLANE * amax
                    dwin[0:rows, :] = dbuf[hi:hi + rows, col]
                    lo = r0 + HALO - (CONV_W - 1) + b
                    gwin[0:rows, :] = gbuf[lo:lo + rows, col]
                    for a in range(amax + 1):
                        k = SUBLANE * a + b
                        back = SUBLANE * (amax - a)
                        dgl = dgl + w_ref[k:k + 1, col] * dwin[back:back + rt, :]
                        prod = dcr * gwin[SUBLANE * a:SUBLANE * a + rt, :]
                        part = prod[0:8, :]
                        for r in range(8, rt, 8):
                            part = part + prod[r:r + 8, :]
                        wacc[8 * k:8 * k + 8, col] += part
                a = ua_ref[r0:r0 + rt, col].astype(F32)
                sg = _sigmoid(ug_ref[r0:r0 + rt, col].astype(F32))
                da = dgl * sg
                dgate = dgl * a * sg * (1.0 - sg)
                du_ref[r0:r0 + rt, col] = da.astype(BF16)
                du_ref[r0:r0 + rt, colg] = dgate.astype(BF16)
                dbu_ref[:, col] += jnp.sum(da, axis=0, keepdims=True)
                dbu_ref[:, colg] += jnp.sum(dgate, axis=0, keepdims=True)
            return carry

        lax.fori_loop(0, d // cw, chunk, 0)

        @pl.when(i == nt - 1)
        def _():
            for k in range(CONV_W):
                dw_ref[k:k + 1, :] = jnp.sum(wacc[8 * k:8 * k + 8, :], axis=0, keepdims=True)
            dw_ref[CONV_W:HALO, :] = jnp.zeros((HALO - CONV_W, d), F32)

    row = pl.BlockSpec((tt, d), lambda i: (i, 0))
    return pl.pallas_call(
        body, name=name, grid=(nt,),
        in_specs=[row, pl.BlockSpec((HALO, d), lambda i: (jnp.minimum((i + 1) * hb, t // HALO - 1), 0)),
                  row, pl.BlockSpec((HALO, d), lambda i: (jnp.maximum(i * hb - 1, 0), 0)),
                  row, row, pl.BlockSpec((HALO, d), lambda i: (0, 0))],
        out_specs=[pl.BlockSpec((tt, 2 * d), lambda i: (i, 0)),
                   pl.BlockSpec((HALO, d), lambda i: (0, 0)),
                   pl.BlockSpec((1, 2 * d), lambda i: (0, 0))],
        out_shape=[jax.ShapeDtypeStruct((t, 2 * d), BF16), jax.ShapeDtypeStruct((HALO, d), F32),
                   jax.ShapeDtypeStruct((1, 2 * d), F32)],
        scratch_shapes=[pltpu.VMEM((tt + HALO, d), F32), pltpu.VMEM((HALO + tt, d), F32),
                        pltpu.VMEM((8 * HALO, d), F32),
                        pltpu.VMEM((rt + HALO, cw), F32), pltpu.VMEM((rt + HALO, cw), F32)],
        compiler_params=_cp(dimension_semantics=("arbitrary",)),
    )(dc, dc, glu, glu, ua, ug, w_dw)


def _mla_mid_fwd(name, down, qg, kvg, ck, sk):
    t, w = down.shape
    rq, rkv = qg.shape[-1], kvg.shape[-1]
    tm = _tile(t, 512)

    def body(dn_ref, qg_ref, kvg_ref, ck_ref, sk_ref, qn_ref, kvn_ref, kpe_ref):
        cq = dn_ref[:, 0:rq]
        ckv = dn_ref[:, rq:rq + rkv]
        pe = dn_ref[:, rq + rkv:rq + rkv + LANE]
        qn_ref[...] = (cq * lax.rsqrt(jnp.mean(cq * cq, axis=-1, keepdims=True) + NORM_EPS)
                       * qg_ref[...]).astype(BF16)
        kvn_ref[...] = (ckv * lax.rsqrt(jnp.mean(ckv * ckv, axis=-1, keepdims=True) + NORM_EPS)
                        * kvg_ref[...]).astype(BF16)
        kpe_ref[...] = _rope(pe, ck_ref[...], sk_ref[...]).astype(BF16)

    def row(n):
        return pl.BlockSpec((tm, n), lambda i: (i, 0))

    def vec(n):
        return pl.BlockSpec((1, n), lambda i: (0, 0))

    return pl.pallas_call(
        body, name=name, grid=(t // tm,),
        in_specs=[row(w), vec(rq), vec(rkv), row(LANE), row(LANE)],
        out_specs=[row(rq), row(rkv), row(LANE)],
        out_shape=[jax.ShapeDtypeStruct((t, rq), BF16), jax.ShapeDtypeStruct((t, rkv), BF16),
                   jax.ShapeDtypeStruct((t, LANE), BF16)],
        compiler_params=_cp(),
    )(down, qg, kvg, ck, sk)


def _mla_mid_bwd(name, down, qg, kvg, dqn, dkvn, dkpe):
    t, w = down.shape
    rq, rkv = qg.shape[-1], kvg.shape[-1]
    tm = _tile(t, 256)

    def body(dn_ref, qg_ref, kvg_ref, dqn_ref, dkvn_ref, dkpe_ref, dd_ref, dqg_ref, dkvg_ref):
        @pl.when(pl.program_id(0) == 0)
        def _():
            dqg_ref[...] = jnp.zeros_like(dqg_ref)
            dkvg_ref[...] = jnp.zeros_like(dkvg_ref)

        dcq, dqg = _rms_bwd_math(dn_ref[:, 0:rq], qg_ref[...], dqn_ref[...])
        dckv, dkvg = _rms_bwd_math(dn_ref[:, rq:rq + rkv], kvg_ref[...], dkvn_ref[...])
        dd_ref[:, 0:rq] = dcq.astype(BF16)
        dd_ref[:, rq:rq + rkv] = dckv.astype(BF16)
        dd_ref[:, rq + rkv:rq + rkv + LANE] = dkpe_ref[...].astype(BF16)
        dqg_ref[...] += dqg
        dkvg_ref[...] += dkvg

    def row(n):
        return pl.BlockSpec((tm, n), lambda i: (i, 0))

    def vec(n):
        return pl.BlockSpec((1, n), lambda i: (0, 0))

    return pl.pallas_call(
        body, name=name, grid=(t // tm,),
        in_specs=[row(w), vec(rq), vec(rkv), row(rq), row(rkv), row(LANE)],
        out_specs=[row(w), vec(rq), vec(rkv)],
        out_shape=[jax.ShapeDtypeStruct((t, w), BF16), jax.ShapeDtypeStruct((1, rq), F32),
                   jax.ShapeDtypeStruct((1, rkv), F32)],
        compiler_params=_cp(dimension_semantics=("arbitrary",)),
    )(down, qg, kvg, dqn, dkvn, dkpe)


ATT_TILE = 512
ATT_HEADS = 2
ATT_HEADS_FWD = 4
_NT = (((1,), (1,)), ((), ()))


def _flash_fwd(name, qf, kf, vt, heads):
    s = qf.shape[0]
    t = _tile(s, ATT_TILE)
    n = s // t
    g = min(ATT_HEADS_FWD, heads)
    qw, vw = HEAD_QK_PAD, HEAD_V

    pairs = [(i, j) for i in range(n) for j in range(i + 1)]
    i_tab = jnp.asarray([p[0] for p in pairs], jnp.int32)
    j_tab = jnp.asarray([p[1] for p in pairs], jnp.int32)

    def body(it_ref, jt_ref, q_ref, k_ref, vt_ref, o_ref, lse_ref, m_sc, l_sc, acc_sc):
        i, j = it_ref[pl.program_id(1)], jt_ref[pl.program_id(1)]

        @pl.when(j == 0)
        def _():
            m_sc[...] = jnp.full(m_sc.shape, -jnp.inf, F32)
            l_sc[...] = jnp.zeros_like(l_sc)
            acc_sc[...] = jnp.zeros_like(acc_sc)

        def step(diag):
            for hh in range(g):
                sc = lax.dot_general(k_ref[:, hh * qw:(hh + 1) * qw], q_ref[:, hh * qw:(hh + 1) * qw], _NT,
                                     preferred_element_type=F32)
                if diag:
                    sc = jnp.where(_chunk_mask_t(t), sc, -jnp.inf)
                m_old = m_sc[hh]
                m_new = jnp.maximum(m_old, jnp.max(sc, axis=0, keepdims=True))
                alpha = jnp.exp(m_old - m_new)
                p = jnp.exp(sc - m_new)
                l_sc[hh] = alpha * l_sc[hh] + jnp.sum(p, axis=0, keepdims=True)
                acc_sc[hh] = alpha * acc_sc[hh] + jnp.dot(vt_ref[hh * vw:(hh + 1) * vw, :], p.astype(BF16),
                                                          preferred_element_type=F32)
                m_sc[hh] = m_new

        @pl.when(j < i)
        def _():
            step(False)

        @pl.when(j == i)
        def _():
            step(True)
            for hh in range(g):
                l = l_sc[hh]
                o_ref[:, hh * vw:(hh + 1) * vw] = (acc_sc[hh] / l).T.astype(BF16)
                lse_ref[hh] = m_sc[hh] + jnp.log(l)

    return pl.pallas_call(
        body, name=name,
        grid_spec=pltpu.PrefetchScalarGridSpec(
            num_scalar_prefetch=2, grid=(heads // g, len(pairs)),
            in_specs=[pl.BlockSpec((t, g * qw), lambda h, st, it, jt: (it[st], h)),
                      pl.BlockSpec((t, g * qw), lambda h, st, it, jt: (jt[st], h)),
                      pl.BlockSpec((g * vw, t), lambda h, st, it, jt: (h, jt[st]))],
            out_specs=[pl.BlockSpec((t, g * vw), lambda h, st, it, jt: (it[st], h)),
                       pl.BlockSpec((g, 1, t), lambda h, st, it, jt: (h, 0, it[st]))],
            scratch_shapes=[pltpu.VMEM((g, 1, t), F32), pltpu.VMEM((g, 1, t), F32),
                            pltpu.VMEM((g, vw, t), F32)]),
        out_shape=[jax.ShapeDtypeStruct((s, heads * vw), BF16),
                   jax.ShapeDtypeStruct((heads, 1, s), F32)],
        compiler_params=_cp(dimension_semantics=("arbitrary", "arbitrary")),
    )(i_tab, j_tab, qf, kf, vt)


def _flash_bwd(name, qf, kf, kft, v, do, o, lse, cq, sq, ck, sk, heads):
    s = qf.shape[0]
    t = _tile(s, ATT_TILE)
    n = s // t
    g = min(ATT_HEADS, heads)
    ng = heads // g
    qw, vw = HEAD_QK_PAD, HEAD_V

    pairs = [(j, i) for j in range(n) for i in range(j, n)]
    j_tab = jnp.asarray([p[0] for p in pairs], jnp.int32)
    i_tab = jnp.asarray([p[1] for p in pairs], jnp.int32)

    def body(jt_ref, it_ref, q_ref, k_ref, kt_ref, v_ref, do_ref, o_ref, lse_ref, cq_ref, sq_ref, ck_ref, sk_ref,
             dq_hbm, dkv_ref, dpe_hbm, dk_sc, dv_sc, dq_sc, dl_sc, pe_sc, dq_stage, pe_stage, sems):
        h, st = pl.program_id(0), pl.program_id(1)
        j, i = jt_ref[st], it_ref[st]
        cols = pl.ds(pl.multiple_of(i * t, t), t)
        rows = pl.ds(pl.multiple_of(j * t, t), t)

        def store(stage, dst, sem):
            cp = pltpu.make_async_copy(stage, dst, sem)
            cp.start()
            cp.wait()

        @pl.when((h == 0) & (st == 0))
        def _():
            pe_sc[...] = jnp.zeros_like(pe_sc)

        @pl.when(j == 0)
        def _():
            for hh in range(g):
                dq_sc[hh, :, cols] = jnp.zeros((qw, t), F32)
                hv = slice(hh * vw, (hh + 1) * vw)
                col = jnp.sum(do_ref[:, hv].astype(F32) * o_ref[:, hv].astype(F32), axis=1, keepdims=True)
                dl_sc[hh, :, cols] = jnp.broadcast_to(col, (t, LANE)).T[0:1, :]

        @pl.when(i == j)
        def _():
            dk_sc[...] = jnp.zeros_like(dk_sc)
            dv_sc[...] = jnp.zeros_like(dv_sc)

        def step(diag):
            for hh in range(g):
                q = q_ref[:, hh * qw:(hh + 1) * qw]
                dout = do_ref[:, hh * vw:(hh + 1) * vw]
                sc = lax.dot_general(k_ref[:, hh * qw:(hh + 1) * qw], q, _NT, preferred_element_type=F32)
                p = jnp.exp(sc - lse_ref[hh])
                if diag:
                    p = jnp.where(_chunk_mask_t(t), p, 0.0)
                dv_sc[hh] += jnp.dot(p.astype(BF16), dout, preferred_element_type=F32)
                dp = lax.dot_general(v_ref[:, hh * vw:(hh + 1) * vw], dout, _NT, preferred_element_type=F32)
                ds = (p * (dp - dl_sc[hh, :, cols])).astype(BF16)
                dk_sc[hh] += jnp.dot(ds, q, preferred_element_type=F32)
                dq_sc[hh, :, cols] += jnp.dot(kt_ref[hh * qw:(hh + 1) * qw, :], ds, preferred_element_type=F32)

        @pl.when(i > j)
        def _():
            step(False)

        @pl.when(i == j)
        def _():
            step(True)
            for hh in range(g):
                dq_stage[:, hh * qw:(hh + 1) * qw] = _rope_t(dq_sc[hh, :, cols].T, cq_ref[...],
                                                             sq_ref[...]).astype(BF16)
            store(dq_stage, dq_hbm.at[cols, pl.ds(pl.multiple_of(h * (g * qw), g * qw), g * qw)], sems.at[0])

        @pl.when(i == n - 1)
        def _():
            pe = None
            for hh in range(g):
                dk = dk_sc[hh]
                dkv_ref[:, hh * qw:(hh + 1) * qw] = jnp.concatenate([dk[:, 0:HEAD_NOPE], dv_sc[hh]],
                                                                     axis=1).astype(BF16)
                part = dk[:, HEAD_NOPE:HEAD_QK_PAD]
                pe = part if pe is None else pe + part
            pe_sc[rows, :] += pe

            @pl.when(h == ng - 1)
            def _():
                pe_stage[...] = _rope_t(pe_sc[rows, :], ck_ref[...], sk_ref[...])
                store(pe_stage, dpe_hbm.at[rows, :], sems.at[1])

    qrow = lambda h, st, jt, it: (it[st], h)
    krow = lambda h, st, jt, it: (jt[st], h)
    qtab = lambda h, st, jt, it: (it[st], 0)
    ktab = lambda h, st, jt, it: (jt[st], 0)
    return pl.pallas_call(
        body, name=name,
        grid_spec=pltpu.PrefetchScalarGridSpec(
            num_scalar_prefetch=2, grid=(ng, len(pairs)),
            in_specs=[pl.BlockSpec((t, g * qw), qrow),
                      pl.BlockSpec((t, g * qw), krow),
                      pl.BlockSpec((g * qw, t), lambda h, st, jt, it: (h, jt[st])),
                      pl.BlockSpec((t, g * vw), krow),
                      pl.BlockSpec((t, g * vw), qrow),
                      pl.BlockSpec((t, g * vw), lambda h, st, jt, it: (jnp.where(jt[st] == 0, it[st], n - 1), h)),
                      pl.BlockSpec((g, 1, t), lambda h, st, jt, it: (h, 0, it[st])),
                      pl.BlockSpec((t, qw), qtab), pl.BlockSpec((t, qw), qtab),
                      pl.BlockSpec((t, LANE), ktab), pl.BlockSpec((t, LANE), ktab)],
            out_specs=[ANY, pl.BlockSpec((t, g * qw), krow), ANY],
            scratch_shapes=[pltpu.VMEM((g, t, qw), F32), pltpu.VMEM((g, t, vw), F32),
                            pltpu.VMEM((g, qw, s), F32), pltpu.VMEM((g, 1, s), F32), pltpu.VMEM((s, LANE), F32),
                            pltpu.VMEM((t, g * qw), BF16), pltpu.VMEM((t, LANE), F32),
                            pltpu.SemaphoreType.DMA((2,))]),
        out_shape=[jax.ShapeDtypeStruct(qf.shape, BF16),
                   jax.ShapeDtypeStruct((s, heads * (HEAD_NOPE + HEAD_V)), BF16),
                   jax.ShapeDtypeStruct((s, LANE), F32)],
        compiler_params=_cp(dimension_semantics=("arbitrary", "arbitrary")),
    )(j_tab, i_tab, qf, kf, kft, v, do, o, lse, cq, sq, ck, sk)


def _adamw(name, parts, w, m, v):
    p, r, c = parts.shape
    tr = _tile(r, max(8, (256 * 1024) // max(c, 1)))
    bc1 = 1.0 - ADAM_B1 ** ADAM_STEP
    bc2 = 1.0 - ADAM_B2 ** ADAM_STEP

    def body(p_ref, w_ref, m_ref, v_ref, g_ref, d_ref, nm_ref, nv_ref):
        g = p_ref[0].astype(F32)
        for q in range(1, p):
            g = g + p_ref[q].astype(F32)
        nm = ADAM_B1 * m_ref[...] + (1.0 - ADAM_B1) * g
        nv = ADAM_B2 * v_ref[...] + (1.0 - ADAM_B2) * (g * g)
        g_ref[...] = g
        nm_ref[...] = nm
        nv_ref[...] = nv
        d_ref[...] = -ADAM_LR * ((nm / bc1) / (jnp.sqrt(nv / bc2) + ADAM_EPS) + ADAM_WD * w_ref[...])

    blk = pl.BlockSpec((tr, c), lambda i: (i, 0))
    sh = jax.ShapeDtypeStruct((r, c), F32)
    return pl.pallas_call(
        body, name=name, grid=(r // tr,),
        in_specs=[pl.BlockSpec((p, tr, c), lambda i: (0, i, 0)), blk, blk, blk],
        out_specs=[blk] * 4, out_shape=[sh] * 4,
        compiler_params=_cp(),
    )(parts, w, m, v)


def _my_place():
    x, y, c = lax.axis_index("x"), lax.axis_index("y"), lax.axis_index("c")
    return x, y, c


def _flip(v, bit):
    return 1 - v if bit else v


def _block(ref, axis, idx, size):
    return ref.at[(slice(None),) * axis + (pl.ds(idx * size, size),)]


HBM_SPEC = pl.BlockSpec(memory_space=pltpu.HBM)
SEM_SPEC = pl.BlockSpec(memory_space=pltpu.SEMAPHORE)
DATAFLOW = pltpu.SideEffectType.DATAFLOW_SIDE_EFFECTING


def _hbm(a):
    return pltpu.with_memory_space_constraint(a, pltpu.HBM)


def _remote_copies(jobs, bufs, send_sems, recv_sems):
    return [pltpu.make_async_remote_copy(src_ref=src, dst_ref=dst, send_sem=send_sems.at[q],
                                         recv_sem=recv_sems.at[q], device_id=dev, device_id_type=MESH)
            for q, (src, dst, dev) in enumerate(jobs(bufs))]


def _split_start(name, bufs, jobs, n_jobs, after):
    nb = len(bufs)

    def body(*refs):
        send_sems, recv_sems = refs[nb + 1], refs[nb + 2]
        for cp in _remote_copies(jobs, refs[:nb], send_sems, recv_sems):
            cp.start()
        refs[-1][...] = jnp.zeros_like(refs[-1])

    outs = pl.pallas_call(
        body, name=name,
        out_shape=(pltpu.SemaphoreType.DMA((n_jobs,)), pltpu.SemaphoreType.DMA((n_jobs,)),
                   *[pltpu.HBM(b.shape, b.dtype) for b in bufs], jax.ShapeDtypeStruct((8, LANE), F32)),
        in_specs=[HBM_SPEC] * nb + [ANY],
        out_specs=(SEM_SPEC, SEM_SPEC, *[HBM_SPEC] * nb, VMEM_SPEC),
        input_output_aliases={q: 2 + q for q in range(nb)},
        compiler_params=pltpu.CompilerParams(has_side_effects=DATAFLOW),
    )(*[_hbm(b) for b in bufs], after)
    return outs[0], outs[1], list(outs[2:2 + nb]), outs[-1]


def _split_wait(name, bufs, send_sems, recv_sems, jobs, after):
    nb = len(bufs)

    def body(*refs):
        for cp in _remote_copies(jobs, refs[:nb], refs[nb], refs[nb + 1]):
            cp.wait_send()
            cp.wait_recv()

    outs = pl.pallas_call(
        body, name=name,
        out_shape=tuple(pltpu.HBM(b.shape, b.dtype) for b in bufs),
        in_specs=[HBM_SPEC] * nb + [SEM_SPEC, SEM_SPEC, ANY],
        out_specs=tuple([HBM_SPEC] * nb),
        input_output_aliases={q: q for q in range(nb)},
        compiler_params=pltpu.CompilerParams(has_side_effects=DATAFLOW),
    )(*bufs, send_sems, recv_sems, after)
    return list(outs)


PLACE_TILE_BYTES = 2 * 1024 * 1024


def _own_block_spec(tr, c, nblk, axis):
    if axis == 0:
        return pl.BlockSpec((tr, c), lambda i, me: (me[0] * nblk + i, 0))
    return pl.BlockSpec((tr, c), lambda i, me: (i, me[0]))


def _cast_place(name, w, layer, axis, me):
    _, r, c = w.shape
    tr = _tile(r, max(SUBLANE_BF16, PLACE_TILE_BYTES // (4 * c)))
    nblk = r // tr
    full = (N_DEV * r, c) if axis == 0 else (r, N_DEV * c)

    def body(me_ref, w_ref, o_ref):
        o_ref[...] = w_ref[...].astype(BF16)

    return pl.pallas_call(
        body, name=name,
        grid_spec=pltpu.PrefetchScalarGridSpec(
            num_scalar_prefetch=1, grid=(nblk,),
            in_specs=[pl.BlockSpec((None, tr, c), lambda i, me: (layer, i, 0))],
            out_specs=_own_block_spec(tr, c, nblk, axis)),
        out_shape=jax.ShapeDtypeStruct(full, BF16), compiler_params=_cp(),
    )(me, w)


def _own_place(name, grad, land, layer, axis, me):
    _, _, r, c = land.shape
    tr = _tile(r, max(SUBLANE_BF16, PLACE_TILE_BYTES // (2 * c)))
    nblk = r // tr

    def body(me_ref, g_ref, land_ref, o_ref):
        o_ref[...] = g_ref[...]

    return pl.pallas_call(
        body, name=name,
        grid_spec=pltpu.PrefetchScalarGridSpec(
            num_scalar_prefetch=1, grid=(nblk,),
            in_specs=[_own_block_spec(tr, c, nblk, axis), ANY],
            out_specs=pl.BlockSpec((None, None, tr, c), lambda i, me: (0, layer, i, 0))),
        out_shape=jax.ShapeDtypeStruct(land.shape, land.dtype),
        input_output_aliases={2: 0}, compiler_params=_cp(),
    )(me, grad, land)


def _gather_jobs_a(axes, sizes):
    def jobs(bufs):
        x, y, c = _my_place()
        out = []
        for t, buf in enumerate(bufs):
            blk = _block(buf, axes[t], 4 * x + 2 * y + c, sizes[t])
            for dev in [(x, y, 1 - c), (1 - x, y, c), (x, 1 - y, c), (1 - x, 1 - y, c)]:
                out.append((blk, blk, dev))
        return out
    return jobs


def _gather_jobs_b(axes, sizes):
    nt = len(axes)

    def jobs(bufs):
        x, y, c = _my_place()
        out = []
        for t in range(nt):
            for px, py in [(1 - x, y), (x, 1 - y), (1 - x, 1 - y)]:
                blk = _block(bufs[t], axes[t], 4 * px + 2 * py + c, sizes[t])
                out.append((blk, blk, (x, y, 1 - c)))
        return out
    return jobs


def _exchange_jobs(axes, sizes, layers):
    nt = len(axes)

    def jobs(bufs):
        x, y, c = _my_place()
        out = []
        for k in range(1, N_DEV):
            px, py, pc = _flip(x, k & 4), _flip(y, k & 2), _flip(c, k & 1)
            for t in range(nt):
                out.append((_block(bufs[t], axes[t], 4 * px + 2 * py + pc, sizes[t]),
                            bufs[nt + t].at[k, layers[t]], (px, py, pc)))
        return out
    return jobs


def _gather_begin(name, lands, axes, after):
    sizes = [b.shape[ax] // N_DEV for b, ax in zip(lands, axes)]
    jobs = _gather_jobs_a(axes, sizes)
    send, recv, bufs, token = _split_start(name + "_a", lands, jobs, 4 * len(lands), after)
    return dict(name=name, axes=axes, sizes=sizes, send=send, recv=recv, bufs=bufs, jobs=jobs), token


def _gather_mid(h, after):
    bufs = _split_wait(h["name"] + "_aw", h["bufs"], h["send"], h["recv"], h["jobs"], after)
    jobs = _gather_jobs_b(h["axes"], h["sizes"])
    send, recv, lands, token = _split_start(h["name"] + "_b", bufs, jobs, 3 * len(bufs), after)
    return dict(h, send=send, recv=recv, bufs=lands, jobs=jobs), token


def _gather_end(h, after):
    return _split_wait(h["name"] + "_bw", h["bufs"], h["send"], h["recv"], h["jobs"], after)


def _exchange_begin(name, grads, axes, lands, layers, me, after):
    sizes = [g.shape[ax] // N_DEV for g, ax in zip(grads, axes)]
    lands = [_own_place(f"{name}_place{t}", grads[t], lands[t], layers[t], axes[t], me)
             for t in range(len(grads))]
    jobs = _exchange_jobs(axes, sizes, layers)
    send, recv, bufs, token = _split_start(name + "_s", list(grads) + lands, jobs, 7 * len(grads), after)
    return dict(name=name, n=len(grads), send=send, recv=recv, bufs=bufs, jobs=jobs), token


def _exchange_end(h, after):
    bufs = _split_wait(h["name"] + "_w", h["bufs"], h["send"], h["recv"], h["jobs"], after)
    return bufs[h["n"]:]


def _all_gather_small(name, vec, reduce):
    r = vec.shape[0]

    def body(v_ref, o_ref, *rest):
        if reduce:
            buf, send_sems, recv_sems = rest
        else:
            buf = o_ref
            send_sems, recv_sems = rest
        x, y, c = _my_place()
        mine = 4 * x + 2 * y + c
        buf[mine] = v_ref[...]
        copies = []
        for k in range(1, N_DEV):
            px, py, pc = _flip(x, k & 4), _flip(y, k & 2), _flip(c, k & 1)
            cp = pltpu.make_async_remote_copy(
                src_ref=v_ref, dst_ref=buf.at[mine], send_sem=send_sems.at[k - 1],
                recv_sem=recv_sems.at[k - 1], device_id=(px, py, pc), device_id_type=MESH)
            cp.start()
            copies.append(cp)
        for cp in copies:
            cp.wait()
        if reduce:
            acc = buf[0]
            for q in range(1, N_DEV):
                acc = acc + buf[q]
            o_ref[...] = acc

    scratch = [pltpu.SemaphoreType.DMA((N_DEV - 1,)), pltpu.SemaphoreType.DMA((N_DEV - 1,))]
    if reduce:
        scratch = [pltpu.VMEM((N_DEV, r, LANE), F32)] + scratch
        out_shape = jax.ShapeDtypeStruct((r, LANE), F32)
    else:
        out_shape = jax.ShapeDtypeStruct((N_DEV, r, LANE), F32)
    return pl.pallas_call(
        body, name=name, in_specs=[VMEM_SPEC], out_specs=VMEM_SPEC, out_shape=out_shape,
        scratch_shapes=scratch, compiler_params=_cp(has_side_effects=True),
    )(vec)


def _pack(arrs, row_mult=8):
    flat = jnp.concatenate([a.reshape(-1).astype(F32) for a in arrs])
    n = flat.shape[0]
    rows = -(-n // LANE)
    rows = -(-rows // row_mult) * row_mult
    return jnp.pad(flat, (0, rows * LANE - n)).reshape(rows, LANE)


def _unpack(vec, shapes):
    flat = vec.reshape(-1)
    out, pos = [], 0
    for sh in shapes:
        n = 1
        for s in sh:
            n *= s
        out.append(flat[pos:pos + n].reshape(sh))
        pos += n
    return out


BIG = ["conv_w_pw1", "conv_w_pw2", "mla_w_in", "mla_w_q_up", "mla_w_kv_up", "mla_w_o", "mlp_w1", "mlp_w2"]
BIG_AXIS = {"conv_w_pw1": 2, "conv_w_pw2": 1, "mla_w_in": 1, "mla_w_q_up": 2, "mla_w_kv_up": 2,
            "mla_w_o": 1, "mlp_w1": 2, "mlp_w2": 1}
SMALL_SHARDED = ["conv_w_dw", "mla_q_norm_g", "mla_kv_norm_g"]
REPLICATED = ["norm_mixer_g", "norm_mlp_g", "conv_b_pw1", "conv_b_dw", "conv_ln_g", "conv_ln_b",
              "conv_b_pw2", "final_norm_g"]
WEIGHTS = ["norm_mixer_g", "norm_mlp_g", "conv_w_pw1", "conv_b_pw1", "conv_w_dw", "conv_b_dw",
           "conv_ln_g", "conv_ln_b", "conv_w_pw2", "conv_b_pw2", "mla_w_in", "mla_q_norm_g",
           "mla_kv_norm_g", "mla_w_q_up", "mla_w_kv_up", "mla_w_o", "mlp_w1", "mlp_w2", "final_norm_g"]


def _unshard_last(g, lead):
    nd = g.ndim
    perm = tuple(range(1, nd - 1)) + (0, nd - 1)
    return g.transpose(perm).reshape(lead + (N_DEV * g.shape[-1],))


def _step(w, m, v, x, positions, target):
    s, d = x.shape
    depth = w["norm_mixer_g"].shape[0]
    n_conv, n_mla = w["conv_w_pw1"].shape[0], w["mla_w_in"].shape[0]
    heads = (w["mla_w_q_up"].shape[-1] * N_DEV) // (HEAD_NOPE + HEAD_ROPE)
    rq, rkv = w["mla_w_q_up"].shape[1], w["mla_w_kv_up"].shape[1]
    xi, yi, ci = _my_place()
    mine = 4 * xi + 2 * yi + ci

    def mixer_units(layer):
        names = (["conv_w_pw1", "conv_w_pw2"] if layer % 2 == 0
                 else ["mla_w_in", "mla_w_q_up", "mla_w_kv_up", "mla_w_o"])
        return [(n, layer // 2) for n in names]

    def mlp_units(layer):
        return [("mlp_w1", layer), ("mlp_w2", layer)]

    me_arr = mine.astype(jnp.int32).reshape(1)

    def gather_begin(tag, units, after):
        lands = [_cast_place(f"{tag}_place_{n}", w[n], jl, BIG_AXIS[n] - 1, me_arr) for n, jl in units]
        h, token = _gather_begin(tag, lands, [BIG_AXIS[n] - 1 for n, _ in units], after)
        return dict(h, units=units), token

    full = {}

    def gather_end(h, after):
        full.update(zip(h["units"], _gather_end(h, after)))

    small_shapes = [w[n].shape for n in SMALL_SHARDED]
    gathered = _all_gather_small("gather_small", _pack([w[n] for n in SMALL_SHARDED]), False)

    first_a, tok = gather_begin("gather_0a", mixer_units(0), gathered)
    first_b, tok = gather_begin("gather_0b", mlp_units(0), tok)
    pending, pending_mlp = {}, {}
    if depth > 1:
        pending[1], tok = gather_begin("gather_1", mixer_units(1), tok)
        pending_mlp[1], tok = gather_begin("gather_1b", mlp_units(1), tok)
    h_first = _rms_fwd("rms_mixer_0", x, w["norm_mixer_g"][0].reshape(1, -1) + tok[0, 0])
    first_a, tok = _gather_mid(first_a, h_first)
    gather_end(first_a, tok)

    per_dev = [_unpack(gathered[q], small_shapes) for q in range(N_DEV)]
    w_dw = _unshard_last(jnp.stack([p[0] for p in per_dev]), (n_conv, CONV_W))
    q_gain = _unshard_last(jnp.stack([p[1] for p in per_dev]), (n_mla,))
    kv_gain = _unshard_last(jnp.stack([p[2] for p in per_dev]), (n_mla,))
    w_dw_pad = jnp.pad(w_dw, ((0, 0), (0, HALO - CONV_W), (0, 0)))

    w_in_cols = rq + rkv + HEAD_ROPE

    def pad_w_in(a):
        return jnp.pad(a, ((0, 0), (0, rq + rkv + LANE - w_in_cols)))

    def pad_wq(a):
        return jnp.pad(a.reshape(rq, heads, HEAD_NOPE + HEAD_ROPE),
                       ((0, 0), (0, 0), (0, HEAD_QK_PAD - HEAD_NOPE - HEAD_ROPE))).reshape(rq, heads * HEAD_QK_PAD)

    inv_freq = ROPE_THETA ** (-jnp.arange(0, HEAD_ROPE, 2, dtype=F32) / HEAD_ROPE)
    ang = positions.reshape(s).astype(F32)[:, None] * inv_freq
    cos, sin = jnp.cos(ang), jnp.sin(ang)
    c64 = jnp.concatenate([cos, cos], axis=1)
    s64 = jnp.concatenate([-sin, sin], axis=1)
    zeros64 = jnp.zeros((s, LANE - HEAD_ROPE), F32)
    ck = jnp.concatenate([c64, zeros64], axis=1)
    sk = jnp.concatenate([s64, zeros64], axis=1)
    scale = (HEAD_NOPE + HEAD_ROPE) ** -0.5
    cq = scale * jnp.concatenate([jnp.ones((s, HEAD_NOPE), F32), ck], axis=1)
    sq = scale * jnp.concatenate([jnp.zeros((s, HEAD_NOPE), F32), sk], axis=1)

    def vec(a):
        return a.reshape(1, -1)

    saved = []
    wpad = {}
    for layer in range(depth):
        jl = layer // 2
        h = h_first if layer == 0 else _rms_fwd(f"rms_mixer_{layer}", x,
                                                 vec(w["norm_mixer_g"][layer]) + tok[0, 0])
        if layer % 2 == 0:
            ua, ug, glu = _mm_glu(f"conv_pw1_{layer}", h, full["conv_w_pw1", jl], None, vec(w["conv_b_pw1"][jl]))
            cc, sw = _conv_fwd(f"conv_dw_{layer}", glu, w_dw_pad[jl], vec(w["conv_b_dw"][jl]),
                               vec(w["conv_ln_g"][jl]), vec(w["conv_ln_b"][jl]))
            x1 = _mm_res(f"conv_pw2_{layer}", sw, full["conv_w_pw2", jl], None, x, vec(w["conv_b_pw2"][jl]))
            mix = (h, ua, ug, glu, cc, sw)
        else:
            wpad["in", jl] = pad_w_in(full["mla_w_in", jl])
            wpad["q", jl] = pad_wq(full["mla_w_q_up", jl])
            down = _mm_plain(f"mla_down_{layer}", h, wpad["in", jl], None, "nn", F32)
            qn, kvn, kpe = _mla_mid_fwd(f"mla_mid_{layer}", down, vec(q_gain[jl]), vec(kv_gain[jl]), ck, sk)
            qf = _mm_q(f"mla_q_{layer}", qn, wpad["q", jl], None, cq, sq)
            kf, vv, kft, vt = _mm_kv(f"mla_kv_{layer}", kvn, full["mla_w_kv_up", jl], None, kpe)
            o, lse = _flash_fwd(f"mla_attn_{layer}", qf, kf, vt, heads)
            if layer in pending_mlp:
                pending_mlp[layer], tok = _gather_mid(pending_mlp[layer], o)
                tok, o = lax.optimization_barrier((tok, o))
            x1 = _mm_res(f"mla_out_{layer}", o, full["mla_w_o", jl], None, x)
            if layer in pending_mlp:
                gather_end(pending_mlp.pop(layer), x1)
            mix = (h, down, qn, kvn, qf, kf, kft, vv, o, lse)
        anchor = x1
        if layer == 0:
            first_b, anchor = _gather_mid(first_b, anchor)
        if layer + 2 < depth:
            pending[layer + 2], anchor = gather_begin(f"gather_{layer + 2}",
                                                      mixer_units(layer + 2) + mlp_units(layer + 2), anchor)
        if layer == 0:
            gather_end(first_b, anchor)
        elif layer + 1 < depth:
            pending[layer + 1], anchor = _gather_mid(pending[layer + 1], anchor)
        if anchor is not x1:
            tok = anchor
        h2 = _rms_fwd(f"rms_mlp_{layer}", x1, vec(w["norm_mlp_g"][layer]) + tok[0, 0])
        z, a = _mm_mlp_up(f"mlp_up_{layer}", h2, full["mlp_w1", layer], None)
        x2 = _mm_res(f"mlp_down_{layer}", a, full["mlp_w2", layer], None, x1)
        if layer + 1 < depth:
            if layer == 0:
                pending[1], tok = _gather_mid(pending[1], x2)
                gather_end(pending[1], tok)
            else:
                gather_end(pending[layer + 1], x2)
        saved.append((x, mix, x1, h2, z, a))
        x = x2

    loss_row, g, gb, d_final, _ = _final_loss("final_loss", x, vec(w["final_norm_g"]) + tok[0, 0], target)

    recv = {n: lax.empty((N_DEV,) + w[n].shape, BF16) for n in BIG}

    def exchange_begin(tag, items, after):
        names = [n for n, _, _ in items]
        h, token = _exchange_begin(tag, [gr for _, _, gr in items], [BIG_AXIS[n] - 1 for n in names],
                                   [recv[n] for n in names], [jl for _, jl, _ in items], me_arr, after)
        return dict(h, names=names), token

    def exchange_end(h, after):
        recv.update(zip(h["names"], _exchange_end(h, after)))

    mix_exchanges = []
    d_mixer, d_mlp = [None] * depth, [None] * depth
    d_small = {n: [None] * n_conv for n in ["conv_b_pw1", "conv_w_dw", "conv_b_dw", "conv_ln_g",
                                           "conv_ln_b", "conv_b_pw2"]}
    d_qg, d_kvg = [None] * n_mla, [None] * n_mla
    for layer in reversed(range(depth)):
        jl = layer // 2
        x0, mix, x1, h2, z, a = saved[layer]
        colsum_g = None
        dz = _mm_mlp_dz(f"mlp_dz_{layer}", gb, full["mlp_w2", layer], None, z)
        dw2 = _mm_wgrad(f"mlp_dw2_{layer}", a, gb)
        w2_exchange, tok = exchange_begin(f"exchange_w2_{layer}", [("mlp_w2", layer, dw2)], dz)
        tok, dz = lax.optimization_barrier((tok, dz))
        dw1 = _mm_wgrad(f"mlp_dw1_{layer}", h2, dz)
        w1_exchange, tok = exchange_begin(f"exchange_w1_{layer}", [("mlp_w1", layer, dw1)], tok)
        g, gb, d_mlp[layer], colsum_g = _mm_rms_bwd(f"mlp_dh_{layer}", dz, full["mlp_w1", layer], x1,
                                                    vec(w["norm_mlp_g"][layer]) + tok[0, 0], g)
        for hx in mix_exchanges:
            exchange_end(hx, g)
        if layer % 2 == 0:
            h, ua, ug, glu, cc, sw = mix
            d_small["conv_b_pw2"][jl] = colsum_g.reshape(-1)
            dsw = _mm_plain(f"conv_ds_{layer}", gb, full["conv_w_pw2", jl], None, "nt", F32)
            dwp2 = _mm_wgrad(f"conv_dw2_{layer}", sw, gb)
            hx2, tok = exchange_begin(f"exchange_pw2_{layer}", [("conv_w_pw2", jl, dwp2)], dsw)
            dc, dlg, dlb, dbdw = _conv_bwd_ln(f"conv_ln_bwd_{layer}", dsw, cc,
                                              vec(w["conv_ln_g"][jl]) + tok[0, 0], vec(w["conv_ln_b"][jl]))
            du, dwdw, dbu = _conv_bwd_dw(f"conv_dw_bwd_{layer}", dc, glu, ua, ug, w_dw_pad[jl])
            d_small["conv_ln_g"][jl] = dlg.reshape(-1)
            d_small["conv_ln_b"][jl] = dlb.reshape(-1)
            d_small["conv_b_dw"][jl] = dbdw.reshape(-1)
            d_small["conv_w_dw"][jl] = dwdw[:CONV_W]
            d_small["conv_b_pw1"][jl] = dbu.reshape(-1)
            dwp1 = _mm_wgrad(f"conv_dw1_{layer}", h, du)
            hx1, tok = exchange_begin(f"exchange_pw1_{layer}", [("conv_w_pw1", jl, dwp1)], dbu)
            dh_a, dh_b = du, full["conv_w_pw1", jl]
            mix_exchanges = [hx2, hx1]
        else:
            h, down, qn, kvn, qf, kf, kft, vv, o, lse = mix
            do = _mm_plain(f"mla_do_{layer}", gb, full["mla_w_o", jl], None, "nt", BF16)
            dwo = _mm_wgrad(f"mla_dwo_{layer}", o, gb)
            dq, dkv, dkpe = _flash_bwd(f"mla_attn_bwd_{layer}", qf, kf, kft, vv, do, o, lse, cq, sq, ck, sk, heads)
            dqn = _mm_plain(f"mla_dqn_{layer}", dq, wpad["q", jl], None, "nt", F32)
            dwq = _mm_wgrad(f"mla_dwq_{layer}", qn, dq).reshape(rq, heads, HEAD_QK_PAD)[
                :, :, :HEAD_NOPE + HEAD_ROPE].reshape(rq, heads * (HEAD_NOPE + HEAD_ROPE))
            dkvn = _mm_plain(f"mla_dkvn_{layer}", dkv, full["mla_w_kv_up", jl], None, "nt", F32)
            dwkv = _mm_wgrad(f"mla_dwkv_{layer}", kvn, dkv)
            ddown, d_qg[jl], d_kvg[jl] = _mla_mid_bwd(f"mla_mid_bwd_{layer}", down, vec(q_gain[jl]),
                                                      vec(kv_gain[jl]), dqn, dkvn, dkpe)
            dwin = _mm_wgrad(f"mla_dwin_{layer}", h, ddown)[:, :w_in_cols]
            items = [("mla_w_in", jl, dwin), ("mla_w_q_up", jl, dwq), ("mla_w_kv_up", jl, dwkv),
                     ("mla_w_o", jl, dwo)]
            hx, tok = exchange_begin(f"exchange_mix_{layer}", items, ddown)
            dh_a, dh_b = ddown, wpad["in", jl]
            mix_exchanges = [hx]
        g, gb, d_mixer[layer], _ = _mm_rms_bwd(f"mixer_dh_{layer}", dh_a, dh_b, x0,
                                               vec(w["norm_mixer_g"][layer]) + tok[0, 0], g)
        exchange_end(w2_exchange, g)
        exchange_end(w1_exchange, g)
    grad_x = g

    out = {}

    def adamw_big(n):
        sh = w[n].shape
        r, c = sh[0] * sh[1], sh[2]
        res = _adamw(f"adamw_{n}", recv[n].reshape(N_DEV, r, c), w[n].reshape(r, c),
                     m[n].reshape(r, c), v[n].reshape(r, c))
        out[n] = [t.reshape(sh) for t in res]

    late = [n for hx in mix_exchanges for n in hx["names"]]
    early = [n for n in BIG if n not in late]
    for n in early:
        adamw_big(n)
    anchor = out[early[-1]][1]
    for hx in mix_exchanges:
        exchange_end(hx, anchor)
    for n in late:
        adamw_big(n)

    small_full = {
        "norm_mixer_g": jnp.concatenate(d_mixer, axis=0), "norm_mlp_g": jnp.concatenate(d_mlp, axis=0),
        "conv_b_pw1": jnp.stack(d_small["conv_b_pw1"]), "conv_b_dw": jnp.stack(d_small["conv_b_dw"]),
        "conv_ln_g": jnp.stack(d_small["conv_ln_g"]), "conv_ln_b": jnp.stack(d_small["conv_ln_b"]),
        "conv_b_pw2": jnp.stack(d_small["conv_b_pw2"]), "final_norm_g": d_final.reshape(-1),
        "conv_w_dw": jnp.stack(d_small["conv_w_dw"]),
        "mla_q_norm_g": jnp.concatenate(d_qg, axis=0), "mla_kv_norm_g": jnp.concatenate(d_kvg, axis=0),
    }
    names = REPLICATED + SMALL_SHARDED
    packed, _ = lax.optimization_barrier((_pack([small_full[n] for n in names]), anchor))
    summed = _unpack(_all_gather_small("reduce_small", packed, True), [small_full[n].shape for n in names])
    summed = dict(zip(names, summed))
    for n in SMALL_SHARDED:
        width = w[n].shape[-1]
        summed[n] = lax.dynamic_slice_in_dim(summed[n], mine * width, width, axis=summed[n].ndim - 1)
    for group, tag in ((REPLICATED, "replicated"), (SMALL_SHARDED, "small_sharded")):
        shapes = [w[n].shape for n in group]
        res = _adamw(f"adamw_{tag}", _pack([summed[n] for n in group])[None],
                     _pack([w[n] for n in group]), _pack([m[n] for n in group]), _pack([v[n] for n in group]))
        unpacked = [_unpack(t, shapes) for t in res]
        for q, n in enumerate(group):
            out[n] = [unpacked[0][q], unpacked[1][q], unpacked[2][q], unpacked[3][q]]

    loss = lax.psum(loss_row[0, 0], ("x", "y", "c"))
    return loss, grad_x, out


def kernel(x, positions, norm_mixer_g, norm_mlp_g, conv_w_pw1, conv_b_pw1, conv_w_dw, conv_b_dw, conv_ln_g, conv_ln_b, conv_w_pw2, conv_b_pw2, mla_w_in, mla_q_norm_g, mla_kv_norm_g, mla_w_q_up, mla_w_kv_up, mla_w_o, mlp_w1, mlp_w2, final_norm_g, loss_target, m_norm_mixer_g, m_norm_mlp_g, m_conv_w_pw1, m_conv_b_pw1, m_conv_w_dw, m_conv_b_dw, m_conv_ln_g, m_conv_ln_b, m_conv_w_pw2, m_conv_b_pw2, m_mla_w_in, m_mla_q_norm_g, m_mla_kv_norm_g, m_mla_w_q_up, m_mla_w_kv_up, m_mla_w_o, m_mlp_w1, m_mlp_w2, m_final_norm_g, v_norm_mixer_g, v_norm_mlp_g, v_conv_w_pw1, v_conv_b_pw1, v_conv_w_dw, v_conv_b_dw, v_conv_ln_g, v_conv_ln_b, v_conv_w_pw2, v_conv_b_pw2, v_mla_w_in, v_mla_q_norm_g, v_mla_kv_norm_g, v_mla_w_q_up, v_mla_w_kv_up, v_mla_w_o, v_mlp_w1, v_mlp_w2, v_final_norm_g):
    ws = (norm_mixer_g, norm_mlp_g, conv_w_pw1, conv_b_pw1, conv_w_dw, conv_b_dw, conv_ln_g, conv_ln_b,
          conv_w_pw2, conv_b_pw2, mla_w_in, mla_q_norm_g, mla_kv_norm_g, mla_w_q_up, mla_w_kv_up, mla_w_o,
          mlp_w1, mlp_w2, final_norm_g)
    ms = (m_norm_mixer_g, m_norm_mlp_g, m_conv_w_pw1, m_conv_b_pw1, m_conv_w_dw, m_conv_b_dw, m_conv_ln_g,
          m_conv_ln_b, m_conv_w_pw2, m_conv_b_pw2, m_mla_w_in, m_mla_q_norm_g, m_mla_kv_norm_g,
          m_mla_w_q_up, m_mla_w_kv_up, m_mla_w_o, m_mlp_w1, m_mlp_w2, m_final_norm_g)
    vs = (v_norm_mixer_g, v_norm_mlp_g, v_conv_w_pw1, v_conv_b_pw1, v_conv_w_dw, v_conv_b_dw, v_conv_ln_g,
          v_conv_ln_b, v_conv_w_pw2, v_conv_b_pw2, v_mla_w_in, v_mla_q_norm_g, v_mla_kv_norm_g,
          v_mla_w_q_up, v_mla_w_kv_up, v_mla_w_o, v_mlp_w1, v_mlp_w2, v_final_norm_g)
    w, m, v = dict(zip(WEIGHTS, ws)), dict(zip(WEIGHTS, ms)), dict(zip(WEIGHTS, vs))
    s, d = x.shape[-2], x.shape[-1]
    loss, grad_x, out = _step(w, m, v, x.reshape(s, d), positions, loss_target.reshape(s, d))
    grads = [out[n][0] for n in WEIGHTS]
    deltas = [out[n][1] for n in WEIGHTS]
    new_m = [out[n][2] for n in WEIGHTS]
    new_v = [out[n][3] for n in WEIGHTS]
    return (loss, grad_x.reshape(x.shape), *grads, *deltas, *new_m, *new_v)
```

```python
import functools

import jax
import jax.numpy as jnp
from jax import lax
from jax.experimental import pallas as pl
from jax.experimental.pallas import tpu as pltpu

F32 = jnp.float32
BF16 = jnp.bfloat16

NORM_EPS = 1e-6
LN_EPS = 1e-5
ROPE_THETA = 10000.0
CHUNK_BITS = 6
HEAD_NOPE = 128
HEAD_ROPE = 64
HEAD_V = 128
HEAD_QK_PAD = 256
CONV_W = 31
HALO = 32
N_DEV = 8

ADAM_LR = 0.001
ADAM_B1 = 0.9
ADAM_B2 = 0.999
ADAM_EPS = 1e-08
ADAM_WD = 0.01
ADAM_STEP = 10

V7X_VMEM_BYTES = 64 * 1024 * 1024
VMEM_LIMIT = (V7X_VMEM_BYTES * 3) // 4
LANE = 128

MESH = pl.DeviceIdType.MESH
ANY = pl.BlockSpec(memory_space=pl.ANY)
VMEM_SPEC = pl.BlockSpec(memory_space=pltpu.VMEM)


def _cp(**kw):
    return pltpu.CompilerParams(vmem_limit_bytes=VMEM_LIMIT, **kw)


SUBLANE = 8
SUBLANE_BF16 = 16

TM_PREF = 1024
TN_PREF = 1024
TK_PREF = 2048


def _tile(n, pref, mult=SUBLANE_BF16):
    if n <= pref + pref // 2:
        return n
    t = (pref // mult) * mult
    while t >= mult:
        if n % t == 0:
            return t
        t -= mult
    return n


def _sigmoid(x):
    return 1.0 / (1.0 + jnp.exp(-x))


def _rot_half(x):
    n = x.shape[-1]
    lane = lax.broadcasted_iota(jnp.int32, x.shape, x.ndim - 1)
    first = (lane & 63) < 32
    return jnp.where(first, pltpu.roll(x, n - 32, x.ndim - 1), pltpu.roll(x, 32, x.ndim - 1))


def _rope(x, c, s):
    return x * c + _rot_half(x) * s


def _rope_t(d, c, s):
    return d * c + _rot_half(d * s)


def _chunk_mask_t(t):
    row = lax.broadcasted_iota(jnp.int32, (t, t), 0)
    col = lax.broadcasted_iota(jnp.int32, (t, t), 1)
    return jnp.right_shift(row, CHUNK_BITS) <= jnp.right_shift(col, CHUNK_BITS)


def _rms_fwd(name, x, g):
    t, d = x.shape
    tm = _tile(t, 512)

    def body(x_ref, g_ref, o_ref):
        xf = x_ref[...]
        r = lax.rsqrt(jnp.mean(xf * xf, axis=-1, keepdims=True) + NORM_EPS)
        o_ref[...] = (xf * r * g_ref[...]).astype(o_ref.dtype)

    return pl.pallas_call(
        body, name=name, grid=(t // tm,),
        in_specs=[pl.BlockSpec((tm, d), lambda i: (i, 0)), pl.BlockSpec((1, d), lambda i: (0, 0))],
        out_specs=pl.BlockSpec((tm, d), lambda i: (i, 0)),
        out_shape=jax.ShapeDtypeStruct((t, d), BF16),
        compiler_params=_cp(),
    )(x, g)


def _rms_bwd_math(xf, g, dy):
    r = lax.rsqrt(jnp.mean(xf * xf, axis=-1, keepdims=True) + NORM_EPS)
    xh = xf * r
    dg = jnp.sum(dy * xh, axis=0, keepdims=True)
    dxh = dy * g
    dx = r * (dxh - xh * jnp.mean(dxh * xh, axis=-1, keepdims=True))
    return dx, dg


def _rms_bwd(name, x, g, dy, resid):
    t, d = x.shape
    tm = _tile(t, 256)

    def body(x_ref, g_ref, dy_ref, r_ref, dx_ref, dxb_ref, dg_ref, cs_ref):
        @pl.when(pl.program_id(0) == 0)
        def _():
            dg_ref[...] = jnp.zeros_like(dg_ref)
            cs_ref[...] = jnp.zeros_like(cs_ref)

        dx, dg = _rms_bwd_math(x_ref[...], g_ref[...], dy_ref[...])
        tot = r_ref[...] + dx
        dx_ref[...] = tot
        dxb_ref[...] = tot.astype(BF16)
        dg_ref[...] += dg
        cs_ref[...] += jnp.sum(tot, axis=0, keepdims=True)

    row = pl.BlockSpec((tm, d), lambda i: (i, 0))
    vec = pl.BlockSpec((1, d), lambda i: (0, 0))
    return pl.pallas_call(
        body, name=name, grid=(t // tm,),
        in_specs=[row, vec, row, row],
        out_specs=[row, row, vec, vec],
        out_shape=[jax.ShapeDtypeStruct((t, d), F32), jax.ShapeDtypeStruct((t, d), BF16),
                   jax.ShapeDtypeStruct((1, d), F32), jax.ShapeDtypeStruct((1, d), F32)],
        compiler_params=_cp(dimension_semantics=("arbitrary",)),
    )(x, g, dy, resid)


def _final_loss(name, x, g, target):
    t, d = x.shape
    tm = _tile(t, 256)

    def body(x_ref, g_ref, t_ref, loss_ref, dx_ref, dxb_ref, dg_ref, cs_ref):
        @pl.when(pl.program_id(0) == 0)
        def _():
            loss_ref[...] = jnp.zeros_like(loss_ref)
            dg_ref[...] = jnp.zeros_like(dg_ref)
            cs_ref[...] = jnp.zeros_like(cs_ref)

        xf = x_ref[...]
        gg = g_ref[...]
        r = lax.rsqrt(jnp.mean(xf * xf, axis=-1, keepdims=True) + NORM_EPS)
        err = xf * r * gg - t_ref[...]
        part = 0.5 * jnp.sum(jnp.mean(err * err, axis=-1, keepdims=True), axis=0, keepdims=True)
        loss_ref[...] += jnp.broadcast_to(part, loss_ref.shape)
        dx, dg = _rms_bwd_math(xf, gg, err * (1.0 / d))
        dx_ref[...] = dx
        dxb_ref[...] = dx.astype(BF16)
        dg_ref[...] += dg
        cs_ref[...] += jnp.sum(dx, axis=0, keepdims=True)

    row = pl.BlockSpec((tm, d), lambda i: (i, 0))
    vec = pl.BlockSpec((1, d), lambda i: (0, 0))
    return pl.pallas_call(
        body, name=name, grid=(t // tm,),
        in_specs=[row, vec, row],
        out_specs=[pl.BlockSpec((1, LANE), lambda i: (0, 0)), row, row, vec, vec],
        out_shape=[jax.ShapeDtypeStruct((1, LANE), F32), jax.ShapeDtypeStruct((t, d), F32),
                   jax.ShapeDtypeStruct((t, d), BF16), jax.ShapeDtypeStruct((1, d), F32),
                   jax.ShapeDtypeStruct((1, d), F32)],
        compiler_params=_cp(dimension_semantics=("arbitrary",)),
    )(x, g, target)


_DIMS = {
    "nn": (((1,), (0,)), ((), ())),
    "nt": (((1,), (1,)), ((), ())),
    "tn": (((0,), (0,)), ((), ())),
}


def _mm(name, a, bs, *, mode, m, n, k, epilogue, out_shape, out_specs, extras=(), extra_specs=(),
        a_lead=None, aliases=None, tn_div=1, tiles=None, acc_refs=False):
    tm, tn, tk = tiles or _tiles(m, n, k, tn_div)
    nk = k // tk
    nb, ne = len(bs), len(extras)
    no = len(out_shape)
    dims = _DIMS[mode]

    def with_lead(shape, idx, lead):
        if lead is None:
            return pl.BlockSpec(shape, idx)
        return pl.BlockSpec((None,) + shape, lambda i, j, kk: (lead,) + idx(i, j, kk))

    if mode == "tn":
        a_spec = with_lead((tk, tm), lambda i, j, kk: (kk, i), a_lead)
    else:
        a_spec = with_lead((tm, tk), lambda i, j, kk: (i, kk), a_lead)
    b_specs = []
    for _, lead, off in bs:
        if mode == "nt":
            b_specs.append(with_lead((tn, tk), lambda i, j, kk, off=off: (j + off, kk), lead))
        else:
            b_specs.append(with_lead((tk, tn), lambda i, j, kk, off=off: (kk, j + off), lead))

    def body(*refs):
        a_ref = refs[0]
        b_refs = refs[1:1 + nb]
        ex = refs[1 + nb:1 + nb + ne]
        outs = refs[1 + nb + ne:1 + nb + ne + no]
        accs = refs[1 + nb + ne + no:]

        def part(b_ref):
            return lax.dot_general(a_ref[...], b_ref[...], dims, preferred_element_type=F32)

        if nk == 1 and acc_refs:
            for acc, b_ref in zip(accs, b_refs):
                acc[...] = part(b_ref)
            epilogue(accs, ex, outs)
            return
        if nk == 1:
            epilogue([part(b_ref) for b_ref in b_refs], ex, outs)
            return
        kk = pl.program_id(2)

        @pl.when(kk == 0)
        def _():
            for acc, b_ref in zip(accs, b_refs):
                acc[...] = part(b_ref)

        @pl.when(kk > 0)
        def _():
            for acc, b_ref in zip(accs, b_refs):
                acc[...] += part(b_ref)

        @pl.when(kk == nk - 1)
        def _():
            epilogue(accs if acc_refs else [acc[...] for acc in accs], ex, outs)

    scratch = [pltpu.VMEM((tm, tn), F32) for _ in range(nb)] if nk > 1 or acc_refs else []
    return pl.pallas_call(
        body, name=name, grid=(m // tm, n // tn, nk),
        in_specs=[a_spec] + b_specs + list(extra_specs),
        out_specs=list(out_specs), out_shape=list(out_shape), scratch_shapes=scratch,
        input_output_aliases=aliases or {},
        compiler_params=_cp(dimension_semantics=("arbitrary", "arbitrary", "arbitrary")),
    )(a, *[b for b, _, _ in bs], *extras), (tm, tn, tk)


def _ij(tm, tn):
    return pl.BlockSpec((tm, tn), lambda i, j, kk: (i, j))


def _tiles(m, n, k, tn_div=1):
    return _tile(m, TM_PREF), _tile(n, TN_PREF // tn_div, LANE), _tile(k, TK_PREF, LANE)


def _mm_plain(name, a, b, b_lead, mode, out_dtype):
    m, k = a.shape
    n = b.shape[-1] if mode == "nn" else b.shape[-2]
    tm, tn, _ = _tiles(m, n, k)

    def epilogue(accs, ex, outs):
        outs[0][...] = accs[0].astype(out_dtype)

    return _mm(name, a, [(b, b_lead, 0)], mode=mode, m=m, n=n, k=k, epilogue=epilogue,
               out_shape=[jax.ShapeDtypeStruct((m, n), out_dtype)], out_specs=[_ij(tm, tn)])[0][0]


def _mm_res(name, a, b, b_lead, resid, bias=None):
    m, k = a.shape
    n = b.shape[-1]
    tm, tn, _ = _tiles(m, n, k)
    extras, specs = [resid], [_ij(tm, tn)]
    if bias is not None:
        extras.append(bias)
        specs.append(pl.BlockSpec((1, tn), lambda i, j, kk: (0, j)))

    def epilogue(accs, ex, outs):
        y = ex[0][...] + accs[0]
        if bias is not None:
            y = y + ex[1][...]
        outs[0][...] = y

    return _mm(name, a, [(b, b_lead, 0)], mode="nn", m=m, n=n, k=k, epilogue=epilogue,
               extras=extras, extra_specs=specs,
               out_shape=[jax.ShapeDtypeStruct((m, n), F32)], out_specs=[_ij(tm, tn)])[0][0]


def _mm_mlp_up(name, h, w1, lead):
    m, k = h.shape
    n = w1.shape[-1]
    tm, tn, _ = _tiles(m, n, k)

    def epilogue(accs, ex, outs):
        z = accs[0]
        outs[0][...] = z.astype(BF16)
        r = jnp.maximum(z, 0.0)
        outs[1][...] = (r * r).astype(BF16)

    sh = jax.ShapeDtypeStruct((m, n), BF16)
    return _mm(name, h, [(w1, lead, 0)], mode="nn", m=m, n=n, k=k, epilogue=epilogue,
               out_shape=[sh, sh], out_specs=[_ij(tm, tn), _ij(tm, tn)])[0]


def _mm_mlp_dz(name, g, w2, lead, z):
    m, k = g.shape
    n = w2.shape[-2]
    tm, tn, _ = _tiles(m, n, k)

    def epilogue(accs, ex, outs):
        outs[0][...] = (accs[0] * (2.0 * jnp.maximum(ex[0][...].astype(F32), 0.0))).astype(BF16)

    return _mm(name, g, [(w2, lead, 0)], mode="nt", m=m, n=n, k=k, epilogue=epilogue,
               extras=[z], extra_specs=[_ij(tm, tn)],
               out_shape=[jax.ShapeDtypeStruct((m, n), BF16)], out_specs=[_ij(tm, tn)])[0][0]


def _mm_glu(name, h, w, lead, bias):
    m, k = h.shape
    n = w.shape[-1] // 2
    tm, tn, _ = _tiles(m, n, k, 2)
    off = n // tn

    def epilogue(accs, ex, outs):
        a = accs[0] + ex[0][...]
        gate = accs[1] + ex[1][...]
        outs[0][...] = a.astype(BF16)
        outs[1][...] = gate.astype(BF16)
        outs[2][...] = a * _sigmoid(gate)

    shb = jax.ShapeDtypeStruct((m, n), BF16)
    return _mm(name, h, [(w, lead, 0), (w, lead, off)], mode="nn", m=m, n=n, k=k, epilogue=epilogue,
               extras=[bias, bias],
               extra_specs=[pl.BlockSpec((1, tn), lambda i, j, kk: (0, j)),
                            pl.BlockSpec((1, tn), lambda i, j, kk: (0, j + off))],
               out_shape=[shb, shb, jax.ShapeDtypeStruct((m, n), F32)],
               out_specs=[_ij(tm, tn)] * 3, tn_div=2)[0]


def _mm_q(name, qn, wq_pad, lead, cq, sq):
    m, k = qn.shape
    n = wq_pad.shape[-1]
    tm, tn, _ = _tiles(m, n, k)
    scale = (HEAD_NOPE + HEAD_ROPE) ** -0.5

    def epilogue(accs, ex, outs):
        c, s = ex[0][:, HEAD_NOPE:], ex[1][:, HEAD_NOPE:]
        for hh in range(tn // HEAD_QK_PAD):
            base = hh * HEAD_QK_PAD
            outs[0][:, base:base + HEAD_NOPE] = (accs[0][:, base:base + HEAD_NOPE] * scale).astype(BF16)
            outs[0][:, base + HEAD_NOPE:base + HEAD_QK_PAD] = _rope(
                accs[0][:, base + HEAD_NOPE:base + HEAD_QK_PAD], c, s).astype(BF16)

    tab = pl.BlockSpec((tm, HEAD_QK_PAD), lambda i, j, kk: (i, 0))
    return _mm(name, qn, [(wq_pad, lead, 0)], mode="nn", m=m, n=n, k=k, epilogue=epilogue,
               extras=[cq, sq], extra_specs=[tab, tab],
               out_shape=[jax.ShapeDtypeStruct((m, n), BF16)], out_specs=[_ij(tm, tn)])[0][0]


def _mm_kv(name, kvn, wkv, lead, kpe):
    m, k = kvn.shape
    n = wkv.shape[-1]
    tm, tn, _ = _tiles(m, n, k)
    heads = tn // (HEAD_NOPE + HEAD_V)

    def epilogue(accs, ex, outs):
        acc = accs[0]
        pe = ex[0][...].astype(F32)
        kparts, vparts = [], []
        for hh in range(heads):
            base = hh * (HEAD_NOPE + HEAD_V)
            kparts += [acc[:, base:base + HEAD_NOPE], pe]
            vparts.append(acc[:, base + HEAD_NOPE:base + HEAD_NOPE + HEAD_V])
        kf = jnp.concatenate(kparts, axis=1)
        vv = jnp.concatenate(vparts, axis=1) if heads > 1 else vparts[0]
        outs[0][...] = kf.astype(BF16)
        outs[1][...] = vv.astype(BF16)
        outs[2][...] = kf.T.astype(BF16)
        outs[3][...] = vv.T.astype(BF16)

    def ji(tn_, tm_):
        return pl.BlockSpec((tn_, tm_), lambda i, j, kk: (j, i))

    return _mm(name, kvn, [(wkv, lead, 0)], mode="nn", m=m, n=n, k=k, epilogue=epilogue,
               extras=[kpe], extra_specs=[pl.BlockSpec((tm, LANE), lambda i, j, kk: (i, 0))],
               out_shape=[jax.ShapeDtypeStruct((m, n), BF16), jax.ShapeDtypeStruct((m, n // 2), BF16),
                          jax.ShapeDtypeStruct((n, m), BF16), jax.ShapeDtypeStruct((n // 2, m), BF16)],
               out_specs=[_ij(tm, tn), _ij(tm, tn // 2), ji(tn, tm), ji(tn // 2, tm)])[0]


def _mm_wgrad(name, a, b):
    t, m = a.shape
    n = b.shape[-1]
    tm, tn, _ = _tiles(m, n, t)

    def epilogue(accs, ex, outs):
        outs[0][...] = accs[0].astype(BF16)

    return _mm(name, a, [(b, None, 0)], mode="tn", m=m, n=n, k=t, epilogue=epilogue,
               out_shape=[jax.ShapeDtypeStruct((m, n), BF16)], out_specs=[_ij(tm, tn)])[0][0]


RMS_BWD_ROWS = 512
RMS_BWD_CHUNK = 64
RMS_BWD_TK = 1024


def _mm_rms_bwd(name, a, b, x, gain, resid):
    m, k = a.shape
    d = b.shape[-2]
    tm, tk = _tile(m, RMS_BWD_ROWS), _tile(k, RMS_BWD_TK, LANE)
    rc = min(RMS_BWD_CHUNK, tm)

    def epilogue(accs, ex, outs):
        x_ref, g_ref, r_ref = ex
        dx_ref, dxb_ref, dg_ref, cs_ref = outs

        @pl.when(pl.program_id(0) == 0)
        def _():
            dg_ref[...] = jnp.zeros_like(dg_ref)
            cs_ref[...] = jnp.zeros_like(cs_ref)

        def chunk(r, carry):
            rows = pl.ds(pl.multiple_of(r * rc, rc), rc)
            dx, dg = _rms_bwd_math(x_ref[rows, :], g_ref[...], accs[0][rows, :])
            tot = r_ref[rows, :] + dx
            dx_ref[rows, :] = tot
            dxb_ref[rows, :] = tot.astype(BF16)
            dg_ref[...] += dg
            cs_ref[...] += jnp.sum(tot, axis=0, keepdims=True)
            return carry

        lax.fori_loop(0, tm // rc, chunk, 0)

    row = pl.BlockSpec((tm, d), lambda i, j, kk: (i, 0))
    vec = pl.BlockSpec((1, d), lambda i, j, kk: (0, 0))
    return _mm(name, a, [(b, None, 0)], mode="nt", m=m, n=d, k=k, epilogue=epilogue,
               extras=[x, gain, resid], extra_specs=[row, vec, row],
               out_shape=[jax.ShapeDtypeStruct((m, d), F32), jax.ShapeDtypeStruct((m, d), BF16),
                          jax.ShapeDtypeStruct((1, d), F32), jax.ShapeDtypeStruct((1, d), F32)],
               out_specs=[row, row, vec, vec], tiles=(tm, d, tk), acc_refs=True)[0]


CONV_ROWS = 256
CONV_RT = 64
CONV_CW = 256
CONV_LR = 32


def _ln_stats(c):
    mu = jnp.mean(c, axis=-1, keepdims=True)
    xc = c - mu
    rstd = lax.rsqrt(jnp.mean(xc * xc, axis=-1, keepdims=True) + LN_EPS)
    return xc * rstd, rstd


def _conv_fwd(name, glu, w_dw, b_dw, ln_g, ln_b):
    t, d = glu.shape
    tt = _tile(t, CONV_ROWS)
    rt, cw, lr = min(CONV_RT, tt), min(CONV_CW, d), min(CONV_LR, tt)
    hb = tt // HALO

    def body(gc_ref, gp_ref, w_ref, b_ref, lg_ref, lb_ref, c_ref, s_ref, buf, win):
        i = pl.program_id(0)
        buf[0:HALO, :] = jnp.where(i > 0, gp_ref[...], 0.0)
        buf[HALO:HALO + tt, :] = gc_ref[...]

        def chunk(cb, carry):
            col = pl.ds(pl.multiple_of(cb * cw, cw), cw)
            for r0 in range(0, tt, rt):
                acc = jnp.broadcast_to(b_ref[:, col], (rt, cw))
                for b in range(SUBLANE):
                    amax = (CONV_W - 1 - b) // SUBLANE
                    lo = r0 + HALO - (CONV_W - 1) + b
                    rows = rt + SUBLANE * amax
                    win[0:rows, :] = buf[lo:lo + rows, col]
                    for a in range(amax + 1):
                        k = SUBLANE * a + b
                        acc = acc + w_ref[k:k + 1, col] * win[SUBLANE * a:SUBLANE * a + rt, :]
                c_ref[r0:r0 + rt, col] = acc
            return carry

        lax.fori_loop(0, d // cw, chunk, 0)

        def ln(r, carry):
            rows = pl.ds(pl.multiple_of(r * lr, lr), lr)
            xh, _ = _ln_stats(c_ref[rows, :])
            y = xh * lg_ref[...] + lb_ref[...]
            s_ref[rows, :] = (y * _sigmoid(y)).astype(BF16)
            return carry

        lax.fori_loop(0, tt // lr, ln, 0)

    row = pl.BlockSpec((tt, d), lambda i: (i, 0))
    vec = pl.BlockSpec((1, d), lambda i: (0, 0))
    return pl.pallas_call(
        body, name=name, grid=(t // tt,),
        in_specs=[row, pl.BlockSpec((HALO, d), lambda i: (jnp.maximum(i * hb - 1, 0), 0)),
                  pl.BlockSpec((HALO, d), lambda i: (0, 0)), vec, vec, vec],
        out_specs=[row, row],
        out_shape=[jax.ShapeDtypeStruct((t, d), F32), jax.ShapeDtypeStruct((t, d), BF16)],
        scratch_shapes=[pltpu.VMEM((HALO + tt, d), F32), pltpu.VMEM((rt + HALO, cw), F32)],
        compiler_params=_cp(dimension_semantics=("arbitrary",)),
    )(glu, glu, w_dw, b_dw, ln_g, ln_b)


def _conv_bwd_ln(name, ds, c, ln_g, ln_b):
    t, d = c.shape
    tt = _tile(t, CONV_ROWS)
    lr = min(CONV_LR, tt)

    def body(ds_ref, c_ref, lg_ref, lb_ref, dc_ref, dg_ref, db_ref, dbdw_ref):
        @pl.when(pl.program_id(0) == 0)
        def _():
            dg_ref[...] = jnp.zeros_like(dg_ref)
            db_ref[...] = jnp.zeros_like(db_ref)
            dbdw_ref[...] = jnp.zeros_like(dbdw_ref)

        def chunk(r, carry):
            rows = pl.ds(pl.multiple_of(r * lr, lr), lr)
            xh, rstd = _ln_stats(c_ref[rows, :])
            g = lg_ref[...]
            y = xh * g + lb_ref[...]
            sg = _sigmoid(y)
            dy = ds_ref[rows, :] * (sg * (1.0 + y * (1.0 - sg)))
            dxh = dy * g
            dc = rstd * (dxh - jnp.mean(dxh, axis=-1, keepdims=True)
                         - xh * jnp.mean(dxh * xh, axis=-1, keepdims=True))
            dc_ref[rows, :] = dc
            dg_ref[...] += jnp.sum(dy * xh, axis=0, keepdims=True)
            db_ref[...] += jnp.sum(dy, axis=0, keepdims=True)
            dbdw_ref[...] += jnp.sum(dc, axis=0, keepdims=True)
            return carry

        lax.fori_loop(0, tt // lr, chunk, 0)

    row = pl.BlockSpec((tt, d), lambda i: (i, 0))
    vec = pl.BlockSpec((1, d), lambda i: (0, 0))
    vsh = jax.ShapeDtypeStruct((1, d), F32)
    return pl.pallas_call(
        body, name=name, grid=(t // tt,),
        in_specs=[row, row, vec, vec], out_specs=[row, vec, vec, vec],
        out_shape=[jax.ShapeDtypeStruct((t, d), F32), vsh, vsh, vsh],
        compiler_params=_cp(dimension_semantics=("arbitrary",)),
    )(ds, c, ln_g, ln_b)


def _conv_bwd_dw(name, dc, glu, ua, ug, w_dw):
    t, d = dc.shape
    tt = _tile(t, CONV_ROWS)
    rt, cw = min(CONV_RT, tt), min(CONV_CW, d)
    hb = tt // HALO
    nt = t // tt

    def body(dcc_ref, dcn_ref, gc_ref, gp_ref, ua_ref, ug_ref, w_ref,
             du_ref, dw_ref, dbu_ref, dbuf, gbuf, wacc, dwin, gwin):
        i = pl.program_id(0)

        @pl.when(i == 0)
        def _():
            wacc[...] = jnp.zeros_like(wacc)
            dbu_ref[...] = jnp.zeros_like(dbu_ref)

        dbuf[0:tt, :] = dcc_ref[...]
        dbuf[tt:tt + HALO, :] = jnp.where(i < nt - 1, dcn_ref[...], 0.0)
        gbuf[0:HALO, :] = jnp.where(i > 0, gp_ref[...], 0.0)
        gbuf[HALO:HALO + tt, :] = gc_ref[...]

        def chunk(cb, carry):
            c0 = pl.multiple_of(cb * cw, cw)
            col = pl.ds(c0, cw)
            colg = pl.ds(pl.multiple_of(d + cb * cw, cw), cw)
            for r0 in range(0, tt, rt):
                dcr = dbuf[r0:r0 + rt, col]
                dgl = jnp.zeros((rt, cw), F32)
                for b in range(SUBLANE):
                    amax = (CONV_W - 1 - b) // SUBLANE
                    hi = r0 + (CONV_W - 1) - b - SUBLANE * amax
                    rows = rt + SUBLANE * amax
                    dwin[0:rows, :] = dbuf[hi:hi + rows, col]
                    lo = r0 + HALO - (CONV_W - 1) + b
                    gwin[0:rows, :] = gbuf[lo:lo + rows, col]
                    for a in range(amax + 1):
                        k = SUBLANE * a + b
                        back = SUBLANE * (amax - a)
                        dgl = dgl + w_ref[k:k + 1, col] * dwin[back:back + rt, :]
                        prod = dcr * gwin[SUBLANE * a:SUBLANE * a + rt, :]
                        part = prod[0:8, :]
                        for r in range(8, rt, 8):
                            part = part + prod[r:r + 8, :]
                        wacc[8 * k:8 * k + 8, col] += part
                a = ua_ref[r0:r0 + rt, col].astype(F32)
                sg = _sigmoid(ug_ref[r0:r0 + rt, col].astype(F32))
                da = dgl * sg
                dgate = dgl * a * sg * (1.0 - sg)
                du_ref[r0:r0 + rt, col] = da.astype(BF16)
                du_ref[r0:r0 + rt, colg] = dgate.astype(BF16)
                dbu_ref[:, col] += jnp.sum(da, axis=0, keepdims=True)
                dbu_ref[:, colg] += jnp.sum(dgate, axis=0, keepdims=True)
            return carry

        lax.fori_loop(0, d // cw, chunk, 0)

        @pl.when(i == nt - 1)
        def _():
            for k in range(CONV_W):
                dw_ref[k:k + 1, :] = jnp.sum(wacc[8 * k:8 * k + 8, :], axis=0, keepdims=True)
            dw_ref[CONV_W:HALO, :] = jnp.zeros((HALO - CONV_W, d), F32)

    row = pl.BlockSpec((tt, d), lambda i: (i, 0))
    return pl.pallas_call(
        body, name=name, grid=(nt,),
        in_specs=[row, pl.BlockSpec((HALO, d), lambda i: (jnp.minimum((i + 1) * hb, t // HALO - 1), 0)),
                  row, pl.BlockSpec((HALO, d), lambda i: (jnp.maximum(i * hb - 1, 0), 0)),
                  row, row, pl.BlockSpec((HALO, d), lambda i: (0, 0))],
        out_specs=[pl.BlockSpec((tt, 2 * d), lambda i: (i, 0)),
                   pl.BlockSpec((HALO, d), lambda i: (0, 0)),
                   pl.BlockSpec((1, 2 * d), lambda i: (0, 0))],
        out_shape=[jax.ShapeDtypeStruct((t, 2 * d), BF16), jax.ShapeDtypeStruct((HALO, d), F32),
                   jax.ShapeDtypeStruct((1, 2 * d), F32)],
        scratch_shapes=[pltpu.VMEM((tt + HALO, d), F32), pltpu.VMEM((HALO + tt, d), F32),
                        pltpu.VMEM((8 * HALO, d), F32),
                        pltpu.VMEM((rt + HALO, cw), F32), pltpu.VMEM((rt + HALO, cw), F32)],
        compiler_params=_cp(dimension_semantics=("arbitrary",)),
    )(dc, dc, glu, glu, ua, ug, w_dw)


def _mla_mid_fwd(name, down, qg, kvg, ck, sk):
    t, w = down.shape
    rq, rkv = qg.shape[-1], kvg.shape[-1]
    tm = _tile(t, 512)

    def body(dn_ref, qg_ref, kvg_ref, ck_ref, sk_ref, qn_ref, kvn_ref, kpe_ref):
        cq = dn_ref[:, 0:rq]
        ckv = dn_ref[:, rq:rq + rkv]
        pe = dn_ref[:, rq + rkv:rq + rkv + LANE]
        qn_ref[...] = (cq * lax.rsqrt(jnp.mean(cq * cq, axis=-1, keepdims=True) + NORM_EPS)
                       * qg_ref[...]).astype(BF16)
        kvn_ref[...] = (ckv * lax.rsqrt(jnp.mean(ckv * ckv, axis=-1, keepdims=True) + NORM_EPS)
                        * kvg_ref[...]).astype(BF16)
        kpe_ref[...] = _rope(pe, ck_ref[...], sk_ref[...]).astype(BF16)

    def row(n):
        return pl.BlockSpec((tm, n), lambda i: (i, 0))

    def vec(n):
        return pl.BlockSpec((1, n), lambda i: (0, 0))

    return pl.pallas_call(
        body, name=name, grid=(t // tm,),
        in_specs=[row(w), vec(rq), vec(rkv), row(LANE), row(LANE)],
        out_specs=[row(rq), row(rkv), row(LANE)],
        out_shape=[jax.ShapeDtypeStruct((t, rq), BF16), jax.ShapeDtypeStruct((t, rkv), BF16),
                   jax.ShapeDtypeStruct((t, LANE), BF16)],
        compiler_params=_cp(),
    )(down, qg, kvg, ck, sk)


def _mla_mid_bwd(name, down, qg, kvg, dqn, dkvn, dkpe):
    t, w = down.shape
    rq, rkv = qg.shape[-1], kvg.shape[-1]
    tm = _tile(t, 256)

    def body(dn_ref, qg_ref, kvg_ref, dqn_ref, dkvn_ref, dkpe_ref, dd_ref, dqg_ref, dkvg_ref):
        @pl.when(pl.program_id(0) == 0)
        def _():
            dqg_ref[...] = jnp.zeros_like(dqg_ref)
            dkvg_ref[...] = jnp.zeros_like(dkvg_ref)

        dcq, dqg = _rms_bwd_math(dn_ref[:, 0:rq], qg_ref[...], dqn_ref[...])
        dckv, dkvg = _rms_bwd_math(dn_ref[:, rq:rq + rkv], kvg_ref[...], dkvn_ref[...])
        dd_ref[:, 0:rq] = dcq.astype(BF16)
        dd_ref[:, rq:rq + rkv] = dckv.astype(BF16)
        dd_ref[:, rq + rkv:rq + rkv + LANE] = dkpe_ref[...].astype(BF16)
        dqg_ref[...] += dqg
        dkvg_ref[...] += dkvg

    def row(n):
        return pl.BlockSpec((tm, n), lambda i: (i, 0))

    def vec(n):
        return pl.BlockSpec((1, n), lambda i: (0, 0))

    return pl.pallas_call(
        body, name=name, grid=(t // tm,),
        in_specs=[row(w), vec(rq), vec(rkv), row(rq), row(rkv), row(LANE)],
        out_specs=[row(w), vec(rq), vec(rkv)],
        out_shape=[jax.ShapeDtypeStruct((t, w), BF16), jax.ShapeDtypeStruct((1, rq), F32),
                   jax.ShapeDtypeStruct((1, rkv), F32)],
        compiler_params=_cp(dimension_semantics=("arbitrary",)),
    )(down, qg, kvg, dqn, dkvn, dkpe)


ATT_TILE = 512
ATT_HEADS = 2
ATT_HEADS_FWD = 4
_NT = (((1,), (1,)), ((), ()))


def _flash_fwd(name, qf, kf, vt, heads):
    s = qf.shape[0]
    t = _tile(s, ATT_TILE)
    n = s // t
    g = min(ATT_HEADS_FWD, heads)
    qw, vw = HEAD_QK_PAD, HEAD_V

    pairs = [(i, j) for i in range(n) for j in range(i + 1)]
    i_tab = jnp.asarray([p[0] for p in pairs], jnp.int32)
    j_tab = jnp.asarray([p[1] for p in pairs], jnp.int32)

    def body(it_ref, jt_ref, q_ref, k_ref, vt_ref, o_ref, lse_ref, m_sc, l_sc, acc_sc):
        i, j = it_ref[pl.program_id(1)], jt_ref[pl.program_id(1)]

        @pl.when(j == 0)
        def _():
            m_sc[...] = jnp.full(m_sc.shape, -jnp.inf, F32)
            l_sc[...] = jnp.zeros_like(l_sc)
            acc_sc[...] = jnp.zeros_like(acc_sc)

        def step(diag):
            for hh in range(g):
                sc = lax.dot_general(k_ref[:, hh * qw:(hh + 1) * qw], q_ref[:, hh * qw:(hh + 1) * qw], _NT,
                                     preferred_element_type=F32)
                if diag:
                    sc = jnp.where(_chunk_mask_t(t), sc, -jnp.inf)
                m_old = m_sc[hh]
                m_new = jnp.maximum(m_old, jnp.max(sc, axis=0, keepdims=True))
                alpha = jnp.exp(m_old - m_new)
                p = jnp.exp(sc - m_new)
                l_sc[hh] = alpha * l_sc[hh] + jnp.sum(p, axis=0, keepdims=True)
                acc_sc[hh] = alpha * acc_sc[hh] + jnp.dot(vt_ref[hh * vw:(hh + 1) * vw, :], p.astype(BF16),
                                                          preferred_element_type=F32)
                m_sc[hh] = m_new

        @pl.when(j < i)
        def _():
            step(False)

        @pl.when(j == i)
        def _():
            step(True)
            for hh in range(g):
                l = l_sc[hh]
                o_ref[:, hh * vw:(hh + 1) * vw] = (acc_sc[hh] / l).T.astype(BF16)
                lse_ref[hh] = m_sc[hh] + jnp.log(l)

    return pl.pallas_call(
        body, name=name,
        grid_spec=pltpu.PrefetchScalarGridSpec(
            num_scalar_prefetch=2, grid=(heads // g, len(pairs)),
            in_specs=[pl.BlockSpec((t, g * qw), lambda h, st, it, jt: (it[st], h)),
                      pl.BlockSpec((t, g * qw), lambda h, st, it, jt: (jt[st], h)),
                      pl.BlockSpec((g * vw, t), lambda h, st, it, jt: (h, jt[st]))],
            out_specs=[pl.BlockSpec((t, g * vw), lambda h, st, it, jt: (it[st], h)),
                       pl.BlockSpec((g, 1, t), lambda h, st, it, jt: (h, 0, it[st]))],
            scratch_shapes=[pltpu.VMEM((g, 1, t), F32), pltpu.VMEM((g, 1, t), F32),
                            pltpu.VMEM((g, vw, t), F32)]),
        out_shape=[jax.ShapeDtypeStruct((s, heads * vw), BF16),
                   jax.ShapeDtypeStruct((heads, 1, s), F32)],
        compiler_params=_cp(dimension_semantics=("arbitrary", "arbitrary")),
    )(i_tab, j_tab, qf, kf, vt)


def _flash_bwd(name, qf, kf, kft, v, do, o, lse, cq, sq, ck, sk, heads):
    s = qf.shape[0]
    t = _tile(s, ATT_TILE)
    n = s // t
    g = min(ATT_HEADS, heads)
    ng = heads // g
    qw, vw = HEAD_QK_PAD, HEAD_V

    pairs = [(j, i) for j in range(n) for i in range(j, n)]
    j_tab = jnp.asarray([p[0] for p in pairs], jnp.int32)
    i_tab = jnp.asarray([p[1] for p in pairs], jnp.int32)

    def body(jt_ref, it_ref, q_ref, k_ref, kt_ref, v_ref, do_ref, o_ref, lse_ref, cq_ref, sq_ref, ck_ref, sk_ref,
             dq_hbm, dkv_ref, dpe_hbm, dk_sc, dv_sc, dq_sc, dl_sc, pe_sc, dq_stage, pe_stage, sems):
        h, st = pl.program_id(0), pl.program_id(1)
        j, i = jt_ref[st], it_ref[st]
        cols = pl.ds(pl.multiple_of(i * t, t), t)
        rows = pl.ds(pl.multiple_of(j * t, t), t)

        def store(stage, dst, sem):
            cp = pltpu.make_async_copy(stage, dst, sem)
            cp.start()
            cp.wait()

        @pl.when((h == 0) & (st == 0))
        def _():
            pe_sc[...] = jnp.zeros_like(pe_sc)

        @pl.when(j == 0)
        def _():
            for hh in range(g):
                dq_sc[hh, :, cols] = jnp.zeros((qw, t), F32)
                hv = slice(hh * vw, (hh + 1) * vw)
                col = jnp.sum(do_ref[:, hv].astype(F32) * o_ref[:, hv].astype(F32), axis=1, keepdims=True)
                dl_sc[hh, :, cols] = jnp.broadcast_to(col, (t, LANE)).T[0:1, :]

        @pl.when(i == j)
        def _():
            dk_sc[...] = jnp.zeros_like(dk_sc)
            dv_sc[...] = jnp.zeros_like(dv_sc)

        def step(diag):
            for hh in range(g):
                q = q_ref[:, hh * qw:(hh + 1) * qw]
                dout = do_ref[:, hh * vw:(hh + 1) * vw]
                sc = lax.dot_general(k_ref[:, hh * qw:(hh + 1) * qw], q, _NT, preferred_element_type=F32)
                p = jnp.exp(sc - lse_ref[hh])
                if diag:
                    p = jnp.where(_chunk_mask_t(t), p, 0.0)
                dv_sc[hh] += jnp.dot(p.astype(BF16), dout, preferred_element_type=F32)
                dp = lax.dot_general(v_ref[:, hh * vw:(hh + 1) * vw], dout, _NT, preferred_element_type=F32)
                ds = (p * (dp - dl_sc[hh, :, cols])).astype(BF16)
                dk_sc[hh] += jnp.dot(ds, q, preferred_element_type=F32)
                dq_sc[hh, :, cols] += jnp.dot(kt_ref[hh * qw:(hh + 1) * qw, :], ds, preferred_element_type=F32)

        @pl.when(i > j)
        def _():
            step(False)

        @pl.when(i == j)
        def _():
            step(True)
            for hh in range(g):
                dq_stage[:, hh * qw:(hh + 1) * qw] = _rope_t(dq_sc[hh, :, cols].T, cq_ref[...],
                                                             sq_ref[...]).astype(BF16)
            store(dq_stage, dq_hbm.at[cols, pl.ds(pl.multiple_of(h * (g * qw), g * qw), g * qw)], sems.at[0])

        @pl.when(i == n - 1)
        def _():
            pe = None
            for hh in range(g):
                dk = dk_sc[hh]
                dkv_ref[:, hh * qw:(hh + 1) * qw] = jnp.concatenate([dk[:, 0:HEAD_NOPE], dv_sc[hh]],
                                                                     axis=1).astype(BF16)
                part = dk[:, HEAD_NOPE:HEAD_QK_PAD]
                pe = part if pe is None else pe + part
            pe_sc[rows, :] += pe

            @pl.when(h == ng - 1)
            def _():
                pe_stage[...] = _rope_t(pe_sc[rows, :], ck_ref[...], sk_ref[...])
                store(pe_stage, dpe_hbm.at[rows, :], sems.at[1])

    qrow = lambda h, st, jt, it: (it[st], h)
    krow = lambda h, st, jt, it: (jt[st], h)
    qtab = lambda h, st, jt, it: (it[st], 0)
    ktab = lambda h, st, jt, it: (jt[st], 0)
    return pl.pallas_call(
        body, name=name,
        grid_spec=pltpu.PrefetchScalarGridSpec(
            num_scalar_prefetch=2, grid=(ng, len(pairs)),
            in_specs=[pl.BlockSpec((t, g * qw), qrow),
                      pl.BlockSpec((t, g * qw), krow),
                      pl.BlockSpec((g * qw, t), lambda h, st, jt, it: (h, jt[st])),
                      pl.BlockSpec((t, g * vw), krow),
                      pl.BlockSpec((t, g * vw), qrow),
                      pl.BlockSpec((t, g * vw), lambda h, st, jt, it: (jnp.where(jt[st] == 0, it[st], n - 1), h)),
                      pl.BlockSpec((g, 1, t), lambda h, st, jt, it: (h, 0, it[st])),
                      pl.BlockSpec((t, qw), qtab), pl.BlockSpec((t, qw), qtab),
                      pl.BlockSpec((t, LANE), ktab), pl.BlockSpec((t, LANE), ktab)],
            out_specs=[ANY, pl.BlockSpec((t, g * qw), krow), ANY],
            scratch_shapes=[pltpu.VMEM((g, t, qw), F32), pltpu.VMEM((g, t, vw), F32),
                            pltpu.VMEM((g, qw, s), F32), pltpu.VMEM((g, 1, s), F32), pltpu.VMEM((s, LANE), F32),
                            pltpu.VMEM((t, g * qw), BF16), pltpu.VMEM((t, LANE), F32),
                            pltpu.SemaphoreType.DMA((2,))]),
        out_shape=[jax.ShapeDtypeStruct(qf.shape, BF16),
                   jax.ShapeDtypeStruct((s, heads * (HEAD_NOPE + HEAD_V)), BF16),
                   jax.ShapeDtypeStruct((s, LANE), F32)],
        compiler_params=_cp(dimension_semantics=("arbitrary", "arbitrary")),
    )(j_tab, i_tab, qf, kf, kft, v, do, o, lse, cq, sq, ck, sk)


def _adamw(name, parts, w, m, v):
    p, r, c = parts.shape
    tr = _tile(r, max(8, (256 * 1024) // max(c, 1)))
    bc1 = 1.0 - ADAM_B1 ** ADAM_STEP
    bc2 = 1.0 - ADAM_B2 ** ADAM_STEP

    def body(p_ref, w_ref, m_ref, v_ref, g_ref, d_ref, nm_ref, nv_ref):
        g = p_ref[0].astype(F32)
        for q in range(1, p):
            g = g + p_ref[q].astype(F32)
        nm = ADAM_B1 * m_ref[...] + (1.0 - ADAM_B1) * g
        nv = ADAM_B2 * v_ref[...] + (1.0 - ADAM_B2) * (g * g)
        g_ref[...] = g
        nm_ref[...] = nm
        nv_ref[...] = nv
        d_ref[...] = -ADAM_LR * ((nm / bc1) / (jnp.sqrt(nv / bc2) + ADAM_EPS) + ADAM_WD * w_ref[...])

    blk = pl.BlockSpec((tr, c), lambda i: (i, 0))
    sh = jax.ShapeDtypeStruct((r, c), F32)
    return pl.pallas_call(
        body, name=name, grid=(r // tr,),
        in_specs=[pl.BlockSpec((p, tr, c), lambda i: (0, i, 0)), blk, blk, blk],
        out_specs=[blk] * 4, out_shape=[sh] * 4,
        compiler_params=_cp(),
    )(parts, w, m, v)


def _my_place():
    x, y, c = lax.axis_index("x"), lax.axis_index("y"), lax.axis_index("c")
    return x, y, c


def _flip(v, bit):
    return 1 - v if bit else v


def _block(ref, axis, idx, size):
    return ref.at[(slice(None),) * axis + (pl.ds(idx * size, size),)]


HBM_SPEC = pl.BlockSpec(memory_space=pltpu.HBM)
SEM_SPEC = pl.BlockSpec(memory_space=pltpu.SEMAPHORE)
DATAFLOW = pltpu.SideEffectType.DATAFLOW_SIDE_EFFECTING


def _hbm(a):
    return pltpu.with_memory_space_constraint(a, pltpu.HBM)


def _remote_copies(jobs, bufs, send_sems, recv_sems):
    return [pltpu.make_async_remote_copy(src_ref=src, dst_ref=dst, send_sem=send_sems.at[q],
                                         recv_sem=recv_sems.at[q], device_id=dev, device_id_type=MESH)
            for q, (src, dst, dev) in enumerate(jobs(bufs))]


def _split_start(name, bufs, jobs, n_jobs, after):
    nb = len(bufs)

    def body(*refs):
        send_sems, recv_sems = refs[nb + 1], refs[nb + 2]
        for cp in _remote_copies(jobs, refs[:nb], send_sems, recv_sems):
            cp.start()
        refs[-1][...] = jnp.zeros_like(refs[-1])

    outs = pl.pallas_call(
        body, name=name,
        out_shape=(pltpu.SemaphoreType.DMA((n_jobs,)), pltpu.SemaphoreType.DMA((n_jobs,)),
                   *[pltpu.HBM(b.shape, b.dtype) for b in bufs], jax.ShapeDtypeStruct((8, LANE), F32)),
        in_specs=[HBM_SPEC] * nb + [ANY],
        out_specs=(SEM_SPEC, SEM_SPEC, *[HBM_SPEC] * nb, VMEM_SPEC),
        input_output_aliases={q: 2 + q for q in range(nb)},
        compiler_params=pltpu.CompilerParams(has_side_effects=DATAFLOW),
    )(*[_hbm(b) for b in bufs], after)
    return outs[0], outs[1], list(outs[2:2 + nb]), outs[-1]


def _split_wait(name, bufs, send_sems, recv_sems, jobs, after):
    nb = len(bufs)

    def body(*refs):
        for cp in _remote_copies(jobs, refs[:nb], refs[nb], refs[nb + 1]):
            cp.wait_send()
            cp.wait_recv()

    outs = pl.pallas_call(
        body, name=name,
        out_shape=tuple(pltpu.HBM(b.shape, b.dtype) for b in bufs),
        in_specs=[HBM_SPEC] * nb + [SEM_SPEC, SEM_SPEC, ANY],
        out_specs=tuple([HBM_SPEC] * nb),
        input_output_aliases={q: q for q in range(nb)},
        compiler_params=pltpu.CompilerParams(has_side_effects=DATAFLOW),
    )(*bufs, send_sems, recv_sems, after)
    return list(outs)


PLACE_TILE_BYTES = 2 * 1024 * 1024


def _own_block_spec(tr, c, nblk, axis):
    if axis == 0:
        return pl.BlockSpec((tr, c), lambda i, me: (me[0] * nblk + i, 0))
    return pl.BlockSpec((tr, c), lambda i, me: (i, me[0]))


def _cast_place(name, w, layer, axis, me):
    _, r, c = w.shape
    tr = _tile(r, max(SUBLANE_BF16, PLACE_TILE_BYTES // (4 * c)))
    nblk = r // tr
    full = (N_DEV * r, c) if axis == 0 else (r, N_DEV * c)

    def body(me_ref, w_ref, o_ref):
        o_ref[...] = w_ref[...].astype(BF16)

    return pl.pallas_call(
        body, name=name,
        grid_spec=pltpu.PrefetchScalarGridSpec(
            num_scalar_prefetch=1, grid=(nblk,),
            in_specs=[pl.BlockSpec((None, tr, c), lambda i, me: (layer, i, 0))],
            out_specs=_own_block_spec(tr, c, nblk, axis)),
        out_shape=jax.ShapeDtypeStruct(full, BF16), compiler_params=_cp(),
    )(me, w)


def _own_place(name, grad, land, layer, axis, me):
    _, _, r, c = land.shape
    tr = _tile(r, max(SUBLANE_BF16, PLACE_TILE_BYTES // (2 * c)))
    nblk = r // tr

    def body(me_ref, g_ref, land_ref, o_ref):
        o_ref[...] = g_ref[...]

    return pl.pallas_call(
        body, name=name,
        grid_spec=pltpu.PrefetchScalarGridSpec(
            num_scalar_prefetch=1, grid=(nblk,),
            in_specs=[_own_block_spec(tr, c, nblk, axis), ANY],
            out_specs=pl.BlockSpec((None, None, tr, c), lambda i, me: (0, layer, i, 0))),
        out_shape=jax.ShapeDtypeStruct(land.shape, land.dtype),
        input_output_aliases={2: 0}, compiler_params=_cp(),
    )(me, grad, land)


def _gather_jobs_a(axes, sizes):
    def jobs(bufs):
        x, y, c = _my_place()
        out = []
        for t, buf in enumerate(bufs):
            blk = _block(buf, axes[t], 4 * x + 2 * y + c, sizes[t])
            for dev in [(x, y, 1 - c), (1 - x, y, c), (x, 1 - y, c), (1 - x, 1 - y, c)]:
                out.append((blk, blk, dev))
        return out
    return jobs


def _gather_jobs_b(axes, sizes):
    nt = len(axes)

    def jobs(bufs):
        x, y, c = _my_place()
        out = []
        for t in range(nt):
            for px, py in [(1 - x, y), (x, 1 - y), (1 - x, 1 - y)]:
                blk = _block(bufs[t], axes[t], 4 * px + 2 * py + c, sizes[t])
                out.append((blk, blk, (x, y, 1 - c)))
        return out
    return jobs


def _exchange_jobs(axes, sizes, layers):
    nt = len(axes)

    def jobs(bufs):
        x, y, c = _my_place()
        out = []
        for k in range(1, N_DEV):
            px, py, pc = _flip(x, k & 4), _flip(y, k & 2), _flip(c, k & 1)
            for t in range(nt):
                out.append((_block(bufs[t], axes[t], 4 * px + 2 * py + pc, sizes[t]),
                            bufs[nt + t].at[k, layers[t]], (px, py, pc)))
        return out
    return jobs


def _gather_begin(name, lands, axes, after):
    sizes = [b.shape[ax] // N_DEV for b, ax in zip(lands, axes)]
    jobs = _gather_jobs_a(axes, sizes)
    send, recv, bufs, token = _split_start(name + "_a", lands, jobs, 4 * len(lands), after)
    return dict(name=name, axes=axes, sizes=sizes, send=send, recv=recv, bufs=bufs, jobs=jobs), token


def _gather_mid(h, after):
    bufs = _split_wait(h["name"] + "_aw", h["bufs"], h["send"], h["recv"], h["jobs"], after)
    jobs = _gather_jobs_b(h["axes"], h["sizes"])
    send, recv, lands, token = _split_start(h["name"] + "_b", bufs, jobs, 3 * len(bufs), after)
    return dict(h, send=send, recv=recv, bufs=lands, jobs=jobs), token


def _gather_end(h, after):
    return _split_wait(h["name"] + "_bw", h["bufs"], h["send"], h["recv"], h["jobs"], after)


def _exchange_begin(name, grads, axes, lands, layers, me, after):
    sizes = [g.shape[ax] // N_DEV for g, ax in zip(grads, axes)]
    lands = [_own_place(f"{name}_place{t}", grads[t], lands[t], layers[t], axes[t], me)
             for t in range(len(grads))]
    jobs = _exchange_jobs(axes, sizes, layers)
    send, recv, bufs, token = _split_start(name + "_s", list(grads) + lands, jobs, 7 * len(grads), after)
    return dict(name=name, n=len(grads), send=send, recv=recv, bufs=bufs, jobs=jobs), token


def _exchange_end(h, after):
    bufs = _split_wait(h["name"] + "_w", h["bufs"], h["send"], h["recv"], h["jobs"], after)
    return bufs[h["n"]:]


def _all_gather_small(name, vec, reduce):
    r = vec.shape[0]

    def body(v_ref, o_ref, *rest):
        if reduce:
            buf, send_sems, recv_sems = rest
        else:
            buf = o_ref
            send_sems, recv_sems = rest
        x, y, c = _my_place()
        mine = 4 * x + 2 * y + c
        buf[mine] = v_ref[...]
        copies = []
        for k in range(1, N_DEV):
            px, py, pc = _flip(x, k & 4), _flip(y, k & 2), _flip(c, k & 1)
            cp = pltpu.make_async_remote_copy(
                src_ref=v_ref, dst_ref=buf.at[mine], send_sem=send_sems.at[k - 1],
                recv_sem=recv_sems.at[k - 1], device_id=(px, py, pc), device_id_type=MESH)
            cp.start()
            copies.append(cp)
        for cp in copies:
            cp.wait()
        if reduce:
            acc = buf[0]
            for q in range(1, N_DEV):
                acc = acc + buf[q]
            o_ref[...] = acc

    scratch = [pltpu.SemaphoreType.DMA((N_DEV - 1,)), pltpu.SemaphoreType.DMA((N_DEV - 1,))]
    if reduce:
        scratch = [pltpu.VMEM((N_DEV, r, LANE), F32)] + scratch
        out_shape = jax.ShapeDtypeStruct((r, LANE), F32)
    else:
        out_shape = jax.ShapeDtypeStruct((N_DEV, r, LANE), F32)
    return pl.pallas_call(
        body, name=name, in_specs=[VMEM_SPEC], out_specs=VMEM_SPEC, out_shape=out_shape,
        scratch_shapes=scratch, compiler_params=_cp(has_side_effects=True),
    )(vec)


def _pack(arrs, row_mult=8):
    flat = jnp.concatenate([a.reshape(-1).astype(F32) for a in arrs])
    n = flat.shape[0]
    rows = -(-n // LANE)
    rows = -(-rows // row_mult) * row_mult
    return jnp.pad(flat, (0, rows * LANE - n)).reshape(rows, LANE)


def _unpack(vec, shapes):
    flat = vec.reshape(-1)
    out, pos = [], 0
    for sh in shapes:
        n = 1
        for s in sh:
            n *= s
        out.append(flat[pos:pos + n].reshape(sh))
        pos += n
    return out


BIG = ["conv_w_pw1", "conv_w_pw2", "mla_w_in", "mla_w_q_up", "mla_w_kv_up", "mla_w_o", "mlp_w1", "mlp_w2"]
BIG_AXIS = {"conv_w_pw1": 2, "conv_w_pw2": 1, "mla_w_in": 1, "mla_w_q_up": 2, "mla_w_kv_up": 2,
            "mla_w_o": 1, "mlp_w1": 2, "mlp_w2": 1}
SMALL_SHARDED = ["conv_w_dw", "mla_q_norm_g", "mla_kv_norm_g"]
REPLICATED = ["norm_mixer_g", "norm_mlp_g", "conv_b_pw1", "conv_b_dw", "conv_ln_g", "conv_ln_b",
              "conv_b_pw2", "final_norm_g"]
WEIGHTS = ["norm_mixer_g", "norm_mlp_g", "conv_w_pw1", "conv_b_pw1", "conv_w_dw", "conv_b_dw",
           "conv_ln_g", "conv_ln_b", "conv_w_pw2", "conv_b_pw2", "mla_w_in", "mla_q_norm_g",
           "mla_kv_norm_g", "mla_w_q_up", "mla_w_kv_up", "mla_w_o", "mlp_w1", "mlp_w2", "final_norm_g"]


def _unshard_last(g, lead):
    nd = g.ndim
    perm = tuple(range(1, nd - 1)) + (0, nd - 1)
    return g.transpose(perm).reshape(lead + (N_DEV * g.shape[-1],))


def _step(w, m, v, x, positions, target):
    s, d = x.shape
    depth = w["norm_mixer_g"].shape[0]
    n_conv, n_mla = w["conv_w_pw1"].shape[0], w["mla_w_in"].shape[0]
    heads = (w["mla_w_q_up"].shape[-1] * N_DEV) // (HEAD_NOPE + HEAD_ROPE)
    rq, rkv = w["mla_w_q_up"].shape[1], w["mla_w_kv_up"].shape[1]
    xi, yi, ci = _my_place()
    mine = 4 * xi + 2 * yi + ci

    def mixer_units(layer):
        names = (["conv_w_pw1", "conv_w_pw2"] if layer % 2 == 0
                 else ["mla_w_in", "mla_w_q_up", "mla_w_kv_up", "mla_w_o"])
        return [(n, layer // 2) for n in names]

    def mlp_units(layer):
        return [("mlp_w1", layer), ("mlp_w2", layer)]

    me_arr = mine.astype(jnp.int32).reshape(1)

    def gather_begin(tag, units, after):
        lands = [_cast_place(f"{tag}_place_{n}", w[n], jl, BIG_AXIS[n] - 1, me_arr) for n, jl in units]
        h, token = _gather_begin(tag, lands, [BIG_AXIS[n] - 1 for n, _ in units], after)
        return dict(h, units=units), token

    full = {}

    def gather_end(h, after):
        full.update(zip(h["units"], _gather_end(h, after)))

    small_shapes = [w[n].shape for n in SMALL_SHARDED]
    gathered = _all_gather_small("gather_small", _pack([w[n] for n in SMALL_SHARDED]), False)

    first_a, tok = gather_begin("gather_0a", mixer_units(0), gathered)
    first_b, tok = gather_begin("gather_0b", mlp_units(0), tok)
    pending, pending_mlp = {}, {}
    if depth > 1:
        pending[1], tok = gather_begin("gather_1", mixer_units(1), tok)
        pending_mlp[1], tok = gather_begin("gather_1b", mlp_units(1), tok)
    h_first = _rms_fwd("rms_mixer_0", x, w["norm_mixer_g"][0].reshape(1, -1) + tok[0, 0])
    first_a, tok = _gather_mid(first_a, h_first)
    gather_end(first_a, tok)

    per_dev = [_unpack(gathered[q], small_shapes) for q in range(N_DEV)]
    w_dw = _unshard_last(jnp.stack([p[0] for p in per_dev]), (n_conv, CONV_W))
    q_gain = _unshard_last(jnp.stack([p[1] for p in per_dev]), (n_mla,))
    kv_gain = _unshard_last(jnp.stack([p[2] for p in per_dev]), (n_mla,))
    w_dw_pad = jnp.pad(w_dw, ((0, 0), (0, HALO - CONV_W), (0, 0)))

    w_in_cols = rq + rkv + HEAD_ROPE

    def pad_w_in(a):
        return jnp.pad(a, ((0, 0), (0, rq + rkv + LANE - w_in_cols)))

    def pad_wq(a):
        return jnp.pad(a.reshape(rq, heads, HEAD_NOPE + HEAD_ROPE),
                       ((0, 0), (0, 0), (0, HEAD_QK_PAD - HEAD_NOPE - HEAD_ROPE))).reshape(rq, heads * HEAD_QK_PAD)

    inv_freq = ROPE_THETA ** (-jnp.arange(0, HEAD_ROPE, 2, dtype=F32) / HEAD_ROPE)
    ang = positions.reshape(s).astype(F32)[:, None] * inv_freq
    cos, sin = jnp.cos(ang), jnp.sin(ang)
    c64 = jnp.concatenate([cos, cos], axis=1)
    s64 = jnp.concatenate([-sin, sin], axis=1)
    zeros64 = jnp.zeros((s, LANE - HEAD_ROPE), F32)
    ck = jnp.concatenate([c64, zeros64], axis=1)
    sk = jnp.concatenate([s64, zeros64], axis=1)
    scale = (HEAD_NOPE + HEAD_ROPE) ** -0.5
    cq = scale * jnp.concatenate([jnp.ones((s, HEAD_NOPE), F32), ck], axis=1)
    sq = scale * jnp.concatenate([jnp.zeros((s, HEAD_NOPE), F32), sk], axis=1)

    def vec(a):
        return a.reshape(1, -1)

    saved = []
    wpad = {}
    for layer in range(depth):
        jl = layer // 2
        h = h_first if layer == 0 else _rms_fwd(f"rms_mixer_{layer}", x,
                                                 vec(w["norm_mixer_g"][layer]) + tok[0, 0])
        if layer % 2 == 0:
            ua, ug, glu = _mm_glu(f"conv_pw1_{layer}", h, full["conv_w_pw1", jl], None, vec(w["conv_b_pw1"][jl]))
            cc, sw = _conv_fwd(f"conv_dw_{layer}", glu, w_dw_pad[jl], vec(w["conv_b_dw"][jl]),
                               vec(w["conv_ln_g"][jl]), vec(w["conv_ln_b"][jl]))
            x1 = _mm_res(f"conv_pw2_{layer}", sw, full["conv_w_pw2", jl], None, x, vec(w["conv_b_pw2"][jl]))
            mix = (h, ua, ug, glu, cc, sw)
        else:
            wpad["in", jl] = pad_w_in(full["mla_w_in", jl])
            wpad["q", jl] = pad_wq(full["mla_w_q_up", jl])
            down = _mm_plain(f"mla_down_{layer}", h, wpad["in", jl], None, "nn", F32)
            qn, kvn, kpe = _mla_mid_fwd(f"mla_mid_{layer}", down, vec(q_gain[jl]), vec(kv_gain[jl]), ck, sk)
            qf = _mm_q(f"mla_q_{layer}", qn, wpad["q", jl], None, cq, sq)
            kf, vv, kft, vt = _mm_kv(f"mla_kv_{layer}", kvn, full["mla_w_kv_up", jl], None, kpe)
            o, lse = _flash_fwd(f"mla_attn_{layer}", qf, kf, vt, heads)
            if layer in pending_mlp:
                pending_mlp[layer], tok = _gather_mid(pending_mlp[layer], o)
                tok, o = lax.optimization_barrier((tok, o))
            x1 = _mm_res(f"mla_out_{layer}", o, full["mla_w_o", jl], None, x)
            if layer in pending_mlp:
                gather_end(pending_mlp.pop(layer), x1)
            mix = (h, down, qn, kvn, qf, kf, kft, vv, o, lse)
        anchor = x1
        if layer == 0:
            first_b, anchor = _gather_mid(first_b, anchor)
        if layer + 2 < depth:
            pending[layer + 2], anchor = gather_begin(f"gather_{layer + 2}",
                                                      mixer_units(layer + 2) + mlp_units(layer + 2), anchor)
        if layer == 0:
            gather_end(first_b, anchor)
        elif layer + 1 < depth:
            pending[layer + 1], anchor = _gather_mid(pending[layer + 1], anchor)
        if anchor is not x1:
            tok = anchor
        h2 = _rms_fwd(f"rms_mlp_{layer}", x1, vec(w["norm_mlp_g"][layer]) + tok[0, 0])
        z, a = _mm_mlp_up(f"mlp_up_{layer}", h2, full["mlp_w1", layer], None)
        x2 = _mm_res(f"mlp_down_{layer}", a, full["mlp_w2", layer], None, x1)
        if layer + 1 < depth:
            if layer == 0:
                pending[1], tok = _gather_mid(pending[1], x2)
                gather_end(pending[1], tok)
            else:
                gather_end(pending[layer + 1], x2)
        saved.append((x, mix, x1, h2, z, a))
        x = x2

    loss_row, g, gb, d_final, _ = _final_loss("final_loss", x, vec(w["final_norm_g"]) + tok[0, 0], target)

    recv = {n: lax.empty((N_DEV,) + w[n].shape, BF16) for n in BIG}

    def exchange_begin(tag, items, after):
        names = [n for n, _, _ in items]
        h, token = _exchange_begin(tag, [gr for _, _, gr in items], [BIG_AXIS[n] - 1 for n in names],
                                   [recv[n] for n in names], [jl for _, jl, _ in items], me_arr, after)
        return dict(h, names=names), token

    def exchange_end(h, after):
        recv.update(zip(h["names"], _exchange_end(h, after)))

    mix_exchanges = []
    d_mixer, d_mlp = [None] * depth, [None] * depth
    d_small = {n: [None] * n_conv for n in ["conv_b_pw1", "conv_w_dw", "conv_b_dw", "conv_ln_g",
                                           "conv_ln_b", "conv_b_pw2"]}
    d_qg, d_kvg = [None] * n_mla, [None] * n_mla
    for layer in reversed(range(depth)):
        jl = layer // 2
        x0, mix, x1, h2, z, a = saved[layer]
        colsum_g = None
        dz = _mm_mlp_dz(f"mlp_dz_{layer}", gb, full["mlp_w2", layer], None, z)
        dw2 = _mm_wgrad(f"mlp_dw2_{layer}", a, gb)
        w2_exchange, tok = exchange_begin(f"exchange_w2_{layer}", [("mlp_w2", layer, dw2)], dz)
        tok, dz = lax.optimization_barrier((tok, dz))
        dw1 = _mm_wgrad(f"mlp_dw1_{layer}", h2, dz)
        w1_exchange, tok = exchange_begin(f"exchange_w1_{layer}", [("mlp_w1", layer, dw1)], tok)
        tok, dz = lax.optimization_barrier((tok, dz))
        dh2 = _mm_plain(f"mlp_dh_{layer}", dz, full["mlp_w1", layer], None, "nt", F32)
        g, gb, d_mlp[layer], colsum_g = _rms_bwd(f"rms_mlp_bwd_{layer}", x1,
                                                 vec(w["norm_mlp_g"][layer]) + tok[0, 0], dh2, g)
        for hx in mix_exchanges:
            exchange_end(hx, g)
        if layer % 2 == 0:
            h, ua, ug, glu, cc, sw = mix
            d_small["conv_b_pw2"][jl] = colsum_g.reshape(-1)
            dsw = _mm_plain(f"conv_ds_{layer}", gb, full["conv_w_pw2", jl], None, "nt", F32)
            dwp2 = _mm_wgrad(f"conv_dw2_{layer}", sw, gb)
            hx2, tok = exchange_begin(f"exchange_pw2_{layer}", [("conv_w_pw2", jl, dwp2)], dsw)
            dc, dlg, dlb, dbdw = _conv_bwd_ln(f"conv_ln_bwd_{layer}", dsw, cc,
                                              vec(w["conv_ln_g"][jl]) + tok[0, 0], vec(w["conv_ln_b"][jl]))
            du, dwdw, dbu = _conv_bwd_dw(f"conv_dw_bwd_{layer}", dc, glu, ua, ug, w_dw_pad[jl])
            d_small["conv_ln_g"][jl] = dlg.reshape(-1)
            d_small["conv_ln_b"][jl] = dlb.reshape(-1)
            d_small["conv_b_dw"][jl] = dbdw.reshape(-1)
            d_small["conv_w_dw"][jl] = dwdw[:CONV_W]
            d_small["conv_b_pw1"][jl] = dbu.reshape(-1)
            dwp1 = _mm_wgrad(f"conv_dw1_{layer}", h, du)
            hx1, tok = exchange_begin(f"exchange_pw1_{layer}", [("conv_w_pw1", jl, dwp1)], dbu)
            dh_a, dh_b = du, full["conv_w_pw1", jl]
            mix_exchanges = [hx2, hx1]
        else:
            h, down, qn, kvn, qf, kf, kft, vv, o, lse = mix
            do = _mm_plain(f"mla_do_{layer}", gb, full["mla_w_o", jl], None, "nt", BF16)
            dwo = _mm_wgrad(f"mla_dwo_{layer}", o, gb)
            dq, dkv, dkpe = _flash_bwd(f"mla_attn_bwd_{layer}", qf, kf, kft, vv, do, o, lse, cq, sq, ck, sk, heads)
            dqn = _mm_plain(f"mla_dqn_{layer}", dq, wpad["q", jl], None, "nt", F32)
            dwq = _mm_wgrad(f"mla_dwq_{layer}", qn, dq).reshape(rq, heads, HEAD_QK_PAD)[
                :, :, :HEAD_NOPE + HEAD_ROPE].reshape(rq, heads * (HEAD_NOPE + HEAD_ROPE))
            dkvn = _mm_plain(f"mla_dkvn_{layer}", dkv, full["mla_w_kv_up", jl], None, "nt", F32)
            dwkv = _mm_wgrad(f"mla_dwkv_{layer}", kvn, dkv)
            ddown, d_qg[jl], d_kvg[jl] = _mla_mid_bwd(f"mla_mid_bwd_{layer}", down, vec(q_gain[jl]),
                                                      vec(kv_gain[jl]), dqn, dkvn, dkpe)
            dwin = _mm_wgrad(f"mla_dwin_{layer}", h, ddown)[:, :w_in_cols]
            items = [("mla_w_in", jl, dwin), ("mla_w_q_up", jl, dwq), ("mla_w_kv_up", jl, dwkv),
                     ("mla_w_o", jl, dwo)]
            hx, tok = exchange_begin(f"exchange_mix_{layer}", items, ddown)
            dh_a, dh_b = ddown, wpad["in", jl]
            mix_exchanges = [hx]
        g, gb, d_mixer[layer], _ = _mm_rms_bwd(f"mixer_dh_{layer}", dh_a, dh_b, x0,
                                               vec(w["norm_mixer_g"][layer]) + tok[0, 0], g)
        exchange_end(w2_exchange, g)
        exchange_end(w1_exchange, g)
    grad_x = g

    out = {}

    def adamw_big(n):
        sh = w[n].shape
        r, c = sh[0] * sh[1], sh[2]
        res = _adamw(f"adamw_{n}", recv[n].reshape(N_DEV, r, c), w[n].reshape(r, c),
                     m[n].reshape(r, c), v[n].reshape(r, c))
        out[n] = [t.reshape(sh) for t in res]

    late = [n for hx in mix_exchanges for n in hx["names"]]
    early = [n for n in BIG if n not in late]
    for n in early:
        adamw_big(n)
    anchor = out[early[-1]][1]
    for hx in mix_exchanges:
        exchange_end(hx, anchor)
    for n in late:
        adamw_big(n)

    small_full = {
        "norm_mixer_g": jnp.concatenate(d_mixer, axis=0), "norm_mlp_g": jnp.concatenate(d_mlp, axis=0),
        "conv_b_pw1": jnp.stack(d_small["conv_b_pw1"]), "conv_b_dw": jnp.stack(d_small["conv_b_dw"]),
        "conv_ln_g": jnp.stack(d_small["conv_ln_g"]), "conv_ln_b": jnp.stack(d_small["conv_ln_b"]),
        "conv_b_pw2": jnp.stack(d_small["conv_b_pw2"]), "final_norm_g": d_final.reshape(-1),
        "conv_w_dw": jnp.stack(d_small["conv_w_dw"]),
        "mla_q_norm_g": jnp.concatenate(d_qg, axis=0), "mla_kv_norm_g": jnp.concatenate(d_kvg, axis=0),
    }
    names = REPLICATED + SMALL_SHARDED
    packed, _ = lax.optimization_barrier((_pack([small_full[n] for n in names]), anchor))
    summed = _unpack(_all_gather_small("reduce_small", packed, True), [small_full[n].shape for n in names])
    summed = dict(zip(names, summed))
    for n in SMALL_SHARDED:
        width = w[n].shape[-1]
        summed[n] = lax.dynamic_slice_in_dim(summed[n], mine * width, width, axis=summed[n].ndim - 1)
    for group, tag in ((REPLICATED, "replicated"), (SMALL_SHARDED, "small_sharded")):
        shapes = [w[n].shape for n in group]
        res = _adamw(f"adamw_{tag}", _pack([summed[n] for n in group])[None],
                     _pack([w[n] for n in group]), _pack([m[n] for n in group]), _pack([v[n] for n in group]))
        unpacked = [_unpack(t, shapes) for t in res]
        for q, n in enumerate(group):
            out[n] = [unpacked[0][q], unpacked[1][q], unpacked[2][q], unpacked[3][q]]

    loss = lax.psum(loss_row[0, 0], ("x", "y", "c"))
    return loss, grad_x, out


def kernel(x, positions, norm_mixer_g, norm_mlp_g, conv_w_pw1, conv_b_pw1, conv_w_dw, conv_b_dw, conv_ln_g, conv_ln_b, conv_w_pw2, conv_b_pw2, mla_w_in, mla_q_norm_g, mla_kv_norm_g, mla_w_q_up, mla_w_kv_up, mla_w_o, mlp_w1, mlp_w2, final_norm_g, loss_target, m_norm_mixer_g, m_norm_mlp_g, m_conv_w_pw1, m_conv_b_pw1, m_conv_w_dw, m_conv_b_dw, m_conv_ln_g, m_conv_ln_b, m_conv_w_pw2, m_conv_b_pw2, m_mla_w_in, m_mla_q_norm_g, m_mla_kv_norm_g, m_mla_w_q_up, m_mla_w_kv_up, m_mla_w_o, m_mlp_w1, m_mlp_w2, m_final_norm_g, v_norm_mixer_g, v_norm_mlp_g, v_conv_w_pw1, v_conv_b_pw1, v_conv_w_dw, v_conv_b_dw, v_conv_ln_g, v_conv_ln_b, v_conv_w_pw2, v_conv_b_pw2, v_mla_w_in, v_mla_q_norm_g, v_mla_kv_norm_g, v_mla_w_q_up, v_mla_w_kv_up, v_mla_w_o, v_mlp_w1, v_mlp_w2, v_final_norm_g):
    ws = (norm_mixer_g, norm_mlp_g, conv_w_pw1, conv_b_pw1, conv_w_dw, conv_b_dw, conv_ln_g, conv_ln_b,
          conv_w_pw2, conv_b_pw2, mla_w_in, mla_q_norm_g, mla_kv_norm_g, mla_w_q_up, mla_w_kv_up, mla_w_o,
          mlp_w1, mlp_w2, final_norm_g)
    ms = (m_norm_mixer_g, m_norm_mlp_g, m_conv_w_pw1, m_conv_b_pw1, m_conv_w_dw, m_conv_b_dw, m_conv_ln_g,
          m_conv_ln_b, m_conv_w_pw2, m_conv_b_pw2, m_mla_w_in, m_mla_q_norm_g, m_mla_kv_norm_g,
          m_mla_w_q_up, m_mla_w_kv_up, m_mla_w_o, m_mlp_w1, m_mlp_w2, m_final_norm_g)
    vs = (v_norm_mixer_g, v_norm_mlp_g, v_conv_w_pw1, v_conv_b_pw1, v_conv_w_dw, v_conv_b_dw, v_conv_ln_g,
          v_conv_ln_b, v_conv_w_pw2, v_conv_b_pw2, v_mla_w_in, v_mla_q_norm_g, v_mla_kv_norm_g,
          v_mla_w_q_up, v_mla_w_kv_up, v_mla_w_o, v_mlp_w1, v_mlp_w2, v_final_norm_g)
    w, m, v = dict(zip(WEIGHTS, ws)), dict(zip(WEIGHTS, ms)), dict(zip(WEIGHTS, vs))
    s, d = x.shape[-2], x.shape[-1]
    loss, grad_x, out = _step(w, m, v, x.reshape(s, d), positions, loss_target.reshape(s, d))
    grads = [out[n][0] for n in WEIGHTS]
    deltas = [out[n][1] for n in WEIGHTS]
    new_m = [out[n][2] for n in WEIGHTS]
    new_v = [out[n][3] for n in WEIGHTS]
    return (loss, grad_x.reshape(x.shape), *grads, *deltas, *new_m, *new_v)
```

```python
import functools

import jax
import jax.numpy as jnp
from jax import lax
from jax.experimental import pallas as pl
from jax.experimental.pallas import tpu as pltpu

F32 = jnp.float32
BF16 = jnp.bfloat16

NORM_EPS = 1e-6
LN_EPS = 1e-5
ROPE_THETA = 10000.0
CHUNK_BITS = 6
HEAD_NOPE = 128
HEAD_ROPE = 64
HEAD_V = 128
HEAD_QK_PAD = 256
CONV_W = 31
HALO = 32
N_DEV = 8

ADAM_LR = 0.001
ADAM_B1 = 0.9
ADAM_B2 = 0.999
ADAM_EPS = 1e-08
ADAM_WD = 0.01
ADAM_STEP = 10

V7X_VMEM_BYTES = 64 * 1024 * 1024
VMEM_LIMIT = (V7X_VMEM_BYTES * 3) // 4
LANE = 128

MESH = pl.DeviceIdType.MESH
ANY = pl.BlockSpec(memory_space=pl.ANY)
VMEM_SPEC = pl.BlockSpec(memory_space=pltpu.VMEM)


def _cp(**kw):
    return pltpu.CompilerParams(vmem_limit_bytes=VMEM_LIMIT, **kw)


SUBLANE = 8
SUBLANE_BF16 = 16

TM_PREF = 1024
TN_PREF = 1024
TK_PREF = 2048


def _tile(n, pref, mult=SUBLANE_BF16):
    if n <= pref + pref // 2:
        return n
    t = (pref // mult) * mult
    while t >= mult:
        if n % t == 0:
            return t
        t -= mult
    return n


def _sigmoid(x):
    return 1.0 / (1.0 + jnp.exp(-x))


def _rot_half(x):
    n = x.shape[-1]
    lane = lax.broadcasted_iota(jnp.int32, x.shape, x.ndim - 1)
    first = (lane & 63) < 32
    return jnp.where(first, pltpu.roll(x, n - 32, x.ndim - 1), pltpu.roll(x, 32, x.ndim - 1))


def _rope(x, c, s):
    return x * c + _rot_half(x) * s


def _rope_t(d, c, s):
    return d * c + _rot_half(d * s)


def _chunk_mask_t(nk, nq, q_off):
    row = lax.broadcasted_iota(jnp.int32, (nk, nq), 0)
    col = lax.broadcasted_iota(jnp.int32, (nk, nq), 1) + q_off
    return jnp.right_shift(row, CHUNK_BITS) <= jnp.right_shift(col, CHUNK_BITS)


def _rms_fwd(name, x, g):
    t, d = x.shape
    tm = _tile(t, 512)

    def body(x_ref, g_ref, o_ref):
        xf = x_ref[...]
        r = lax.rsqrt(jnp.mean(xf * xf, axis=-1, keepdims=True) + NORM_EPS)
        o_ref[...] = (xf * r * g_ref[...]).astype(o_ref.dtype)

    return pl.pallas_call(
        body, name=name, grid=(t // tm,),
        in_specs=[pl.BlockSpec((tm, d), lambda i: (i, 0)), pl.BlockSpec((1, d), lambda i: (0, 0))],
        out_specs=pl.BlockSpec((tm, d), lambda i: (i, 0)),
        out_shape=jax.ShapeDtypeStruct((t, d), BF16),
        compiler_params=_cp(),
    )(x, g)


def _rms_bwd_math(xf, g, dy):
    r = lax.rsqrt(jnp.mean(xf * xf, axis=-1, keepdims=True) + NORM_EPS)
    xh = xf * r
    dg = jnp.sum(dy * xh, axis=0, keepdims=True)
    dxh = dy * g
    dx = r * (dxh - xh * jnp.mean(dxh * xh, axis=-1, keepdims=True))
    return dx, dg


def _rms_bwd(name, x, g, dy, resid):
    t, d = x.shape
    tm = _tile(t, 256)

    def body(x_ref, g_ref, dy_ref, r_ref, dx_ref, dxb_ref, dg_ref, cs_ref):
        @pl.when(pl.program_id(0) == 0)
        def _():
            dg_ref[...] = jnp.zeros_like(dg_ref)
            cs_ref[...] = jnp.zeros_like(cs_ref)

        dx, dg = _rms_bwd_math(x_ref[...], g_ref[...], dy_ref[...])
        tot = r_ref[...] + dx
        dx_ref[...] = tot
        dxb_ref[...] = tot.astype(BF16)
        dg_ref[...] += dg
        cs_ref[...] += jnp.sum(tot, axis=0, keepdims=True)

    row = pl.BlockSpec((tm, d), lambda i: (i, 0))
    vec = pl.BlockSpec((1, d), lambda i: (0, 0))
    return pl.pallas_call(
        body, name=name, grid=(t // tm,),
        in_specs=[row, vec, row, row],
        out_specs=[row, row, vec, vec],
        out_shape=[jax.ShapeDtypeStruct((t, d), F32), jax.ShapeDtypeStruct((t, d), BF16),
                   jax.ShapeDtypeStruct((1, d), F32), jax.ShapeDtypeStruct((1, d), F32)],
        compiler_params=_cp(dimension_semantics=("arbitrary",)),
    )(x, g, dy, resid)


def _final_loss(name, x, g, target):
    t, d = x.shape
    tm = _tile(t, 256)

    def body(x_ref, g_ref, t_ref, loss_ref, dx_ref, dxb_ref, dg_ref, cs_ref):
        @pl.when(pl.program_id(0) == 0)
        def _():
            loss_ref[...] = jnp.zeros_like(loss_ref)
            dg_ref[...] = jnp.zeros_like(dg_ref)
            cs_ref[...] = jnp.zeros_like(cs_ref)

        xf = x_ref[...]
        gg = g_ref[...]
        r = lax.rsqrt(jnp.mean(xf * xf, axis=-1, keepdims=True) + NORM_EPS)
        err = xf * r * gg - t_ref[...]
        part = 0.5 * jnp.sum(jnp.mean(err * err, axis=-1, keepdims=True), axis=0, keepdims=True)
        loss_ref[...] += jnp.broadcast_to(part, loss_ref.shape)
        dx, dg = _rms_bwd_math(xf, gg, err * (1.0 / d))
        dx_ref[...] = dx
        dxb_ref[...] = dx.astype(BF16)
        dg_ref[...] += dg
        cs_ref[...] += jnp.sum(dx, axis=0, keepdims=True)

    row = pl.BlockSpec((tm, d), lambda i: (i, 0))
    vec = pl.BlockSpec((1, d), lambda i: (0, 0))
    return pl.pallas_call(
        body, name=name, grid=(t // tm,),
        in_specs=[row, vec, row],
        out_specs=[pl.BlockSpec((1, LANE), lambda i: (0, 0)), row, row, vec, vec],
        out_shape=[jax.ShapeDtypeStruct((1, LANE), F32), jax.ShapeDtypeStruct((t, d), F32),
                   jax.ShapeDtypeStruct((t, d), BF16), jax.ShapeDtypeStruct((1, d), F32),
                   jax.ShapeDtypeStruct((1, d), F32)],
        compiler_params=_cp(dimension_semantics=("arbitrary",)),
    )(x, g, target)


_DIMS = {
    "nn": (((1,), (0,)), ((), ())),
    "nt": (((1,), (1,)), ((), ())),
    "tn": (((0,), (0,)), ((), ())),
}


def _mm(name, a, bs, *, mode, m, n, k, epilogue, out_shape, out_specs, extras=(), extra_specs=(),
        a_lead=None, aliases=None, tn_div=1):
    tm, tn, tk = _tiles(m, n, k, tn_div)
    nk = k // tk
    nb, ne = len(bs), len(extras)
    no = len(out_shape)
    dims = _DIMS[mode]

    def with_lead(shape, idx, lead):
        if lead is None:
            return pl.BlockSpec(shape, idx)
        return pl.BlockSpec((None,) + shape, lambda i, j, kk: (lead,) + idx(i, j, kk))

    if mode == "tn":
        a_spec = with_lead((tk, tm), lambda i, j, kk: (kk, i), a_lead)
    else:
        a_spec = with_lead((tm, tk), lambda i, j, kk: (i, kk), a_lead)
    b_specs = []
    for _, lead, off in bs:
        if mode == "nt":
            b_specs.append(with_lead((tn, tk), lambda i, j, kk, off=off: (j + off, kk), lead))
        else:
            b_specs.append(with_lead((tk, tn), lambda i, j, kk, off=off: (kk, j + off), lead))

    def body(*refs):
        a_ref = refs[0]
        b_refs = refs[1:1 + nb]
        ex = refs[1 + nb:1 + nb + ne]
        outs = refs[1 + nb + ne:1 + nb + ne + no]
        accs = refs[1 + nb + ne + no:]

        def part(b_ref):
            return lax.dot_general(a_ref[...], b_ref[...], dims, preferred_element_type=F32)

        if nk == 1:
            epilogue([part(b_ref) for b_ref in b_refs], ex, outs)
            return
        kk = pl.program_id(2)

        @pl.when(kk == 0)
        def _():
            for acc, b_ref in zip(accs, b_refs):
                acc[...] = part(b_ref)

        @pl.when(kk > 0)
        def _():
            for acc, b_ref in zip(accs, b_refs):
                acc[...] += part(b_ref)

        @pl.when(kk == nk - 1)
        def _():
            epilogue([acc[...] for acc in accs], ex, outs)

    scratch = [pltpu.VMEM((tm, tn), F32) for _ in range(nb)] if nk > 1 else []
    return pl.pallas_call(
        body, name=name, grid=(m // tm, n // tn, nk),
        in_specs=[a_spec] + b_specs + list(extra_specs),
        out_specs=list(out_specs), out_shape=list(out_shape), scratch_shapes=scratch,
        input_output_aliases=aliases or {},
        compiler_params=_cp(dimension_semantics=("arbitrary", "arbitrary", "arbitrary")),
    )(a, *[b for b, _, _ in bs], *extras), (tm, tn, tk)


def _ij(tm, tn):
    return pl.BlockSpec((tm, tn), lambda i, j, kk: (i, j))


def _tiles(m, n, k, tn_div=1):
    return _tile(m, TM_PREF), _tile(n, TN_PREF // tn_div, LANE), _tile(k, TK_PREF, LANE)


def _mm_plain(name, a, b, b_lead, mode, out_dtype):
    m, k = a.shape
    n = b.shape[-1] if mode == "nn" else b.shape[-2]
    tm, tn, _ = _tiles(m, n, k)

    def epilogue(accs, ex, outs):
        outs[0][...] = accs[0].astype(out_dtype)

    return _mm(name, a, [(b, b_lead, 0)], mode=mode, m=m, n=n, k=k, epilogue=epilogue,
               out_shape=[jax.ShapeDtypeStruct((m, n), out_dtype)], out_specs=[_ij(tm, tn)])[0][0]


def _mm_res(name, a, b, b_lead, resid, bias=None):
    m, k = a.shape
    n = b.shape[-1]
    tm, tn, _ = _tiles(m, n, k)
    extras, specs = [resid], [_ij(tm, tn)]
    if bias is not None:
        extras.append(bias)
        specs.append(pl.BlockSpec((1, tn), lambda i, j, kk: (0, j)))

    def epilogue(accs, ex, outs):
        y = ex[0][...] + accs[0]
        if bias is not None:
            y = y + ex[1][...]
        outs[0][...] = y

    return _mm(name, a, [(b, b_lead, 0)], mode="nn", m=m, n=n, k=k, epilogue=epilogue,
               extras=extras, extra_specs=specs,
               out_shape=[jax.ShapeDtypeStruct((m, n), F32)], out_specs=[_ij(tm, tn)])[0][0]


def _mm_mlp_up(name, h, w1, lead):
    m, k = h.shape
    n = w1.shape[-1]
    tm, tn, _ = _tiles(m, n, k)

    def epilogue(accs, ex, outs):
        z = accs[0]
        outs[0][...] = z.astype(BF16)
        r = jnp.maximum(z, 0.0)
        outs[1][...] = (r * r).astype(BF16)

    sh = jax.ShapeDtypeStruct((m, n), BF16)
    return _mm(name, h, [(w1, lead, 0)], mode="nn", m=m, n=n, k=k, epilogue=epilogue,
               out_shape=[sh, sh], out_specs=[_ij(tm, tn), _ij(tm, tn)])[0]


def _mm_mlp_dz(name, g, w2, lead, z):
    m, k = g.shape
    n = w2.shape[-2]
    tm, tn, _ = _tiles(m, n, k)

    def epilogue(accs, ex, outs):
        outs[0][...] = (accs[0] * (2.0 * jnp.maximum(ex[0][...].astype(F32), 0.0))).astype(BF16)

    return _mm(name, g, [(w2, lead, 0)], mode="nt", m=m, n=n, k=k, epilogue=epilogue,
               extras=[z], extra_specs=[_ij(tm, tn)],
               out_shape=[jax.ShapeDtypeStruct((m, n), BF16)], out_specs=[_ij(tm, tn)])[0][0]


def _mm_glu(name, h, w, lead, bias):
    m, k = h.shape
    n = w.shape[-1] // 2
    tm, tn, _ = _tiles(m, n, k, 2)
    off = n // tn

    def epilogue(accs, ex, outs):
        a = accs[0] + ex[0][...]
        gate = accs[1] + ex[1][...]
        outs[0][...] = a.astype(BF16)
        outs[1][...] = gate.astype(BF16)
        outs[2][...] = a * _sigmoid(gate)

    shb = jax.ShapeDtypeStruct((m, n), BF16)
    return _mm(name, h, [(w, lead, 0), (w, lead, off)], mode="nn", m=m, n=n, k=k, epilogue=epilogue,
               extras=[bias, bias],
               extra_specs=[pl.BlockSpec((1, tn), lambda i, j, kk: (0, j)),
                            pl.BlockSpec((1, tn), lambda i, j, kk: (0, j + off))],
               out_shape=[shb, shb, jax.ShapeDtypeStruct((m, n), F32)],
               out_specs=[_ij(tm, tn)] * 3, tn_div=2)[0]


def _mm_q(name, qn, wq_pad, lead, cq, sq):
    m, k = qn.shape
    n = wq_pad.shape[-1]
    tm, tn, _ = _tiles(m, n, k)
    scale = (HEAD_NOPE + HEAD_ROPE) ** -0.5

    def epilogue(accs, ex, outs):
        c, s = ex[0][:, HEAD_NOPE:], ex[1][:, HEAD_NOPE:]
        for hh in range(tn // HEAD_QK_PAD):
            base = hh * HEAD_QK_PAD
            outs[0][:, base:base + HEAD_NOPE] = (accs[0][:, base:base + HEAD_NOPE] * scale).astype(BF16)
            outs[0][:, base + HEAD_NOPE:base + HEAD_QK_PAD] = _rope(
                accs[0][:, base + HEAD_NOPE:base + HEAD_QK_PAD], c, s).astype(BF16)

    tab = pl.BlockSpec((tm, HEAD_QK_PAD), lambda i, j, kk: (i, 0))
    return _mm(name, qn, [(wq_pad, lead, 0)], mode="nn", m=m, n=n, k=k, epilogue=epilogue,
               extras=[cq, sq], extra_specs=[tab, tab],
               out_shape=[jax.ShapeDtypeStruct((m, n), BF16)], out_specs=[_ij(tm, tn)])[0][0]


def _mm_kv(name, kvn, wkv, lead, kpe):
    m, k = kvn.shape
    n = wkv.shape[-1]
    tm, tn, _ = _tiles(m, n, k)
    heads = tn // (HEAD_NOPE + HEAD_V)

    def epilogue(accs, ex, outs):
        acc = accs[0]
        pe = ex[0][...].astype(F32)
        kparts, vparts = [], []
        for hh in range(heads):
            base = hh * (HEAD_NOPE + HEAD_V)
            kparts += [acc[:, base:base + HEAD_NOPE], pe]
            vparts.append(acc[:, base + HEAD_NOPE:base + HEAD_NOPE + HEAD_V])
        kf = jnp.concatenate(kparts, axis=1)
        vv = jnp.concatenate(vparts, axis=1) if heads > 1 else vparts[0]
        outs[0][...] = kf.astype(BF16)
        outs[1][...] = vv.astype(BF16)
        outs[2][...] = kf.T.astype(BF16)
        outs[3][...] = vv.T.astype(BF16)

    def ji(tn_, tm_):
        return pl.BlockSpec((tn_, tm_), lambda i, j, kk: (j, i))

    return _mm(name, kvn, [(wkv, lead, 0)], mode="nn", m=m, n=n, k=k, epilogue=epilogue,
               extras=[kpe], extra_specs=[pl.BlockSpec((tm, LANE), lambda i, j, kk: (i, 0))],
               out_shape=[jax.ShapeDtypeStruct((m, n), BF16), jax.ShapeDtypeStruct((m, n // 2), BF16),
                          jax.ShapeDtypeStruct((n, m), BF16), jax.ShapeDtypeStruct((n // 2, m), BF16)],
               out_specs=[_ij(tm, tn), _ij(tm, tn // 2), ji(tn, tm), ji(tn // 2, tm)])[0]


def _mm_wgrad(name, a, b):
    t, m = a.shape
    n = b.shape[-1]
    tm, tn, _ = _tiles(m, n, t)

    def epilogue(accs, ex, outs):
        outs[0][...] = accs[0].astype(BF16)

    return _mm(name, a, [(b, None, 0)], mode="tn", m=m, n=n, k=t, epilogue=epilogue,
               out_shape=[jax.ShapeDtypeStruct((m, n), BF16)], out_specs=[_ij(tm, tn)])[0][0]


CONV_ROWS = 256
CONV_RT = 64
CONV_CW = 256
CONV_LR = 32


def _ln_stats(c):
    mu = jnp.mean(c, axis=-1, keepdims=True)
    xc = c - mu
    rstd = lax.rsqrt(jnp.mean(xc * xc, axis=-1, keepdims=True) + LN_EPS)
    return xc * rstd, rstd


def _conv_fwd(name, glu, w_dw, b_dw, ln_g, ln_b):
    t, d = glu.shape
    tt = _tile(t, CONV_ROWS)
    rt, cw, lr = min(CONV_RT, tt), min(CONV_CW, d), min(CONV_LR, tt)
    hb = tt // HALO

    def body(gc_ref, gp_ref, w_ref, b_ref, lg_ref, lb_ref, c_ref, s_ref, buf, win):
        i = pl.program_id(0)
        buf[0:HALO, :] = jnp.where(i > 0, gp_ref[...], 0.0)
        buf[HALO:HALO + tt, :] = gc_ref[...]

        def chunk(cb, carry):
            col = pl.ds(pl.multiple_of(cb * cw, cw), cw)
            for r0 in range(0, tt, rt):
                acc = jnp.broadcast_to(b_ref[:, col], (rt, cw))
                for b in range(SUBLANE):
                    amax = (CONV_W - 1 - b) // SUBLANE
                    lo = r0 + HALO - (CONV_W - 1) + b
                    rows = rt + SUBLANE * amax
                    win[0:rows, :] = buf[lo:lo + rows, col]
                    for a in range(amax + 1):
                        k = SUBLANE * a + b
                        acc = acc + w_ref[k:k + 1, col] * win[SUBLANE * a:SUBLANE * a + rt, :]
                c_ref[r0:r0 + rt, col] = acc
            return carry

        lax.fori_loop(0, d // cw, chunk, 0)

        def ln(r, carry):
            rows = pl.ds(pl.multiple_of(r * lr, lr), lr)
            xh, _ = _ln_stats(c_ref[rows, :])
            y = xh * lg_ref[...] + lb_ref[...]
            s_ref[rows, :] = (y * _sigmoid(y)).astype(BF16)
            return carry

        lax.fori_loop(0, tt // lr, ln, 0)

    row = pl.BlockSpec((tt, d), lambda i: (i, 0))
    vec = pl.BlockSpec((1, d), lambda i: (0, 0))
    return pl.pallas_call(
        body, name=name, grid=(t // tt,),
        in_specs=[row, pl.BlockSpec((HALO, d), lambda i: (jnp.maximum(i * hb - 1, 0), 0)),
                  pl.BlockSpec((HALO, d), lambda i: (0, 0)), vec, vec, vec],
        out_specs=[row, row],
        out_shape=[jax.ShapeDtypeStruct((t, d), F32), jax.ShapeDtypeStruct((t, d), BF16)],
        scratch_shapes=[pltpu.VMEM((HALO + tt, d), F32), pltpu.VMEM((rt + HALO, cw), F32)],
        compiler_params=_cp(dimension_semantics=("arbitrary",)),
    )(glu, glu, w_dw, b_dw, ln_g, ln_b)


def _conv_bwd_ln(name, ds, c, ln_g, ln_b):
    t, d = c.shape
    tt = _tile(t, CONV_ROWS)
    lr = min(CONV_LR, tt)

    def body(ds_ref, c_ref, lg_ref, lb_ref, dc_ref, dg_ref, db_ref, dbdw_ref):
        @pl.when(pl.program_id(0) == 0)
        def _():
            dg_ref[...] = jnp.zeros_like(dg_ref)
            db_ref[...] = jnp.zeros_like(db_ref)
            dbdw_ref[...] = jnp.zeros_like(dbdw_ref)

        def chunk(r, carry):
            rows = pl.ds(pl.multiple_of(r * lr, lr), lr)
            xh, rstd = _ln_stats(c_ref[rows, :])
            g = lg_ref[...]
            y = xh * g + lb_ref[...]
            sg = _sigmoid(y)
            dy = ds_ref[rows, :] * (sg * (1.0 + y * (1.0 - sg)))
            dxh = dy * g
            dc = rstd * (dxh - jnp.mean(dxh, axis=-1, keepdims=True)
                         - xh * jnp.mean(dxh * xh, axis=-1, keepdims=True))
            dc_ref[rows, :] = dc
            dg_ref[...] += jnp.sum(dy * xh, axis=0, keepdims=True)
            db_ref[...] += jnp.sum(dy, axis=0, keepdims=True)
            dbdw_ref[...] += jnp.sum(dc, axis=0, keepdims=True)
            return carry

        lax.fori_loop(0, tt // lr, chunk, 0)

    row = pl.BlockSpec((tt, d), lambda i: (i, 0))
    vec = pl.BlockSpec((1, d), lambda i: (0, 0))
    vsh = jax.ShapeDtypeStruct((1, d), F32)
    return pl.pallas_call(
        body, name=name, grid=(t // tt,),
        in_specs=[row, row, vec, vec], out_specs=[row, vec, vec, vec],
        out_shape=[jax.ShapeDtypeStruct((t, d), F32), vsh, vsh, vsh],
        compiler_params=_cp(dimension_semantics=("arbitrary",)),
    )(ds, c, ln_g, ln_b)


def _conv_bwd_dw(name, dc, glu, ua, ug, w_dw):
    t, d = dc.shape
    tt = _tile(t, CONV_ROWS)
    rt, cw = min(CONV_RT, tt), min(CONV_CW, d)
    hb = tt // HALO
    nt = t // tt

    def body(dcc_ref, dcn_ref, gc_ref, gp_ref, ua_ref, ug_ref, w_ref,
             du_ref, dw_ref, dbu_ref, dbuf, gbuf, wacc, dwin, gwin):
        i = pl.program_id(0)

        @pl.when(i == 0)
        def _():
            wacc[...] = jnp.zeros_like(wacc)
            dbu_ref[...] = jnp.zeros_like(dbu_ref)

        dbuf[0:tt, :] = dcc_ref[...]
        dbuf[tt:tt + HALO, :] = jnp.where(i < nt - 1, dcn_ref[...], 0.0)
        gbuf[0:HALO, :] = jnp.where(i > 0, gp_ref[...], 0.0)
        gbuf[HALO:HALO + tt, :] = gc_ref[...]

        def chunk(cb, carry):
            c0 = pl.multiple_of(cb * cw, cw)
            col = pl.ds(c0, cw)
            colg = pl.ds(pl.multiple_of(d + cb * cw, cw), cw)
            for r0 in range(0, tt, rt):
                dcr = dbuf[r0:r0 + rt, col]
                dgl = jnp.zeros((rt, cw), F32)
                for b in range(SUBLANE):
                    amax = (CONV_W - 1 - b) // SUBLANE
                    hi = r0 + (CONV_W - 1) - b - SUBLANE * amax
                    rows = rt + SUBLANE * amax
                    dwin[0:rows, :] = dbuf[hi:hi + rows, col]
                    lo = r0 + HALO - (CONV_W - 1) + b
                    gwin[0:rows, :] = gbuf[lo:lo + rows, col]
                    for a in range(amax + 1):
                        k = SUBLANE * a + b
                        back = SUBLANE * (amax - a)
                        dgl = dgl + w_ref[k:k + 1, col] * dwin[back:back + rt, :]
                        prod = dcr * gwin[SUBLANE * a:SUBLANE * a + rt, :]
                        part = prod[0:8, :]
                        for r in range(8, rt, 8):
                            part = part + prod[r:r + 8, :]
                        wacc[8 * k:8 * k + 8, col] += part
                a = ua_ref[r0:r0 + rt, col].astype(F32)
                sg = _sigmoid(ug_ref[r0:r0 + rt, col].astype(F32))
                da = dgl * sg
                dgate = dgl * a * sg * (1.0 - sg)
                du_ref[r0:r0 + rt, col] = da.astype(BF16)
                du_ref[r0:r0 + rt, colg] = dgate.astype(BF16)
                dbu_ref[:, col] += jnp.sum(da, axis=0, keepdims=True)
                dbu_ref[:, colg] += jnp.sum(dgate, axis=0, keepdims=True)
            return carry

        lax.fori_loop(0, d // cw, chunk, 0)

        @pl.when(i == nt - 1)
        def _():
            for k in range(CONV_W):
                dw_ref[k:k + 1, :] = jnp.sum(wacc[8 * k:8 * k + 8, :], axis=0, keepdims=True)
            dw_ref[CONV_W:HALO, :] = jnp.zeros((HALO - CONV_W, d), F32)

    row = pl.BlockSpec((tt, d), lambda i: (i, 0))
    return pl.pallas_call(
        body, name=name, grid=(nt,),
        in_specs=[row, pl.BlockSpec((HALO, d), lambda i: (jnp.minimum((i + 1) * hb, t // HALO - 1), 0)),
                  row, pl.BlockSpec((HALO, d), lambda i: (jnp.maximum(i * hb - 1, 0), 0)),
                  row, row, pl.BlockSpec((HALO, d), lambda i: (0, 0))],
        out_specs=[pl.BlockSpec((tt, 2 * d), lambda i: (i, 0)),
                   pl.BlockSpec((HALO, d), lambda i: (0, 0)),
                   pl.BlockSpec((1, 2 * d), lambda i: (0, 0))],
        out_shape=[jax.ShapeDtypeStruct((t, 2 * d), BF16), jax.ShapeDtypeStruct((HALO, d), F32),
                   jax.ShapeDtypeStruct((1, 2 * d), F32)],
        scratch_shapes=[pltpu.VMEM((tt + HALO, d), F32), pltpu.VMEM((HALO + tt, d), F32),
                        pltpu.VMEM((8 * HALO, d), F32),
                        pltpu.VMEM((rt + HALO, cw), F32), pltpu.VMEM((rt + HALO, cw), F32)],
        compiler_params=_cp(dimension_semantics=("arbitrary",)),
    )(dc, dc, glu, glu, ua, ug, w_dw)


def _mla_mid_fwd(name, down, qg, kvg, ck, sk):
    t, w = down.shape
    rq, rkv = qg.shape[-1], kvg.shape[-1]
    tm = _tile(t, 512)

    def body(dn_ref, qg_ref, kvg_ref, ck_ref, sk_ref, qn_ref, kvn_ref, kpe_ref):
        cq = dn_ref[:, 0:rq]
        ckv = dn_ref[:, rq:rq + rkv]
        pe = dn_ref[:, rq + rkv:rq + rkv + LANE]
        qn_ref[...] = (cq * lax.rsqrt(jnp.mean(cq * cq, axis=-1, keepdims=True) + NORM_EPS)
                       * qg_ref[...]).astype(BF16)
        kvn_ref[...] = (ckv * lax.rsqrt(jnp.mean(ckv * ckv, axis=-1, keepdims=True) + NORM_EPS)
                        * kvg_ref[...]).astype(BF16)
        kpe_ref[...] = _rope(pe, ck_ref[...], sk_ref[...]).astype(BF16)

    def row(n):
        return pl.BlockSpec((tm, n), lambda i: (i, 0))

    def vec(n):
        return pl.BlockSpec((1, n), lambda i: (0, 0))

    return pl.pallas_call(
        body, name=name, grid=(t // tm,),
        in_specs=[row(w), vec(rq), vec(rkv), row(LANE), row(LANE)],
        out_specs=[row(rq), row(rkv), row(LANE)],
        out_shape=[jax.ShapeDtypeStruct((t, rq), BF16), jax.ShapeDtypeStruct((t, rkv), BF16),
                   jax.ShapeDtypeStruct((t, LANE), BF16)],
        compiler_params=_cp(),
    )(down, qg, kvg, ck, sk)


def _mla_mid_bwd(name, down, qg, kvg, dqn, dkvn, dkpe):
    t, w = down.shape
    rq, rkv = qg.shape[-1], kvg.shape[-1]
    tm = _tile(t, 256)

    def body(dn_ref, qg_ref, kvg_ref, dqn_ref, dkvn_ref, dkpe_ref, dd_ref, dqg_ref, dkvg_ref):
        @pl.when(pl.program_id(0) == 0)
        def _():
            dqg_ref[...] = jnp.zeros_like(dqg_ref)
            dkvg_ref[...] = jnp.zeros_like(dkvg_ref)

        dcq, dqg = _rms_bwd_math(dn_ref[:, 0:rq], qg_ref[...], dqn_ref[...])
        dckv, dkvg = _rms_bwd_math(dn_ref[:, rq:rq + rkv], kvg_ref[...], dkvn_ref[...])
        dd_ref[:, 0:rq] = dcq.astype(BF16)
        dd_ref[:, rq:rq + rkv] = dckv.astype(BF16)
        dd_ref[:, rq + rkv:rq + rkv + LANE] = dkpe_ref[...].astype(BF16)
        dqg_ref[...] += dqg
        dkvg_ref[...] += dkvg

    def row(n):
        return pl.BlockSpec((tm, n), lambda i: (i, 0))

    def vec(n):
        return pl.BlockSpec((1, n), lambda i: (0, 0))

    return pl.pallas_call(
        body, name=name, grid=(t // tm,),
        in_specs=[row(w), vec(rq), vec(rkv), row(rq), row(rkv), row(LANE)],
        out_specs=[row(w), vec(rq), vec(rkv)],
        out_shape=[jax.ShapeDtypeStruct((t, w), BF16), jax.ShapeDtypeStruct((1, rq), F32),
                   jax.ShapeDtypeStruct((1, rkv), F32)],
        compiler_params=_cp(dimension_semantics=("arbitrary",)),
    )(down, qg, kvg, dqn, dkvn, dkpe)


ATT_TILE = 512
ATT_HEADS = 2
ATT_HEADS_FWD = 4
_NT = (((1,), (1,)), ((), ()))


def _flash_fwd(name, qf, kf, vt, heads):
    s = qf.shape[0]
    t = _tile(s, ATT_TILE)
    n = s // t
    g = min(ATT_HEADS_FWD, heads)
    qw, vw = HEAD_QK_PAD, HEAD_V

    pairs = [(i, j) for i in range(n) for j in range(i + 1)]
    i_tab = jnp.asarray([p[0] for p in pairs], jnp.int32)
    j_tab = jnp.asarray([p[1] for p in pairs], jnp.int32)

    def body(it_ref, jt_ref, q_ref, k_ref, vt_ref, o_ref, lse_ref, m_sc, l_sc, acc_sc):
        i, j = it_ref[pl.program_id(1)], jt_ref[pl.program_id(1)]

        @pl.when(j == 0)
        def _():
            m_sc[...] = jnp.full(m_sc.shape, -jnp.inf, F32)
            l_sc[...] = jnp.zeros_like(l_sc)
            acc_sc[...] = jnp.zeros_like(acc_sc)

        def step(q0, nq, nk, diag):
            qs = slice(q0, q0 + nq)
            for hh in range(g):
                sc = lax.dot_general(k_ref[0:nk, hh * qw:(hh + 1) * qw], q_ref[qs, hh * qw:(hh + 1) * qw], _NT,
                                     preferred_element_type=F32)
                if diag:
                    sc = jnp.where(_chunk_mask_t(nk, nq, q0), sc, -jnp.inf)
                m_old = m_sc[hh, :, qs]
                m_new = jnp.maximum(m_old, jnp.max(sc, axis=0, keepdims=True))
                alpha = jnp.exp(m_old - m_new)
                p = jnp.exp(sc - m_new)
                l_sc[hh, :, qs] = alpha * l_sc[hh, :, qs] + jnp.sum(p, axis=0, keepdims=True)
                acc_sc[hh, :, qs] = alpha * acc_sc[hh, :, qs] + jnp.dot(
                    vt_ref[hh * vw:(hh + 1) * vw, 0:nk], p.astype(BF16), preferred_element_type=F32)
                m_sc[hh, :, qs] = m_new

        @pl.when(j < i)
        def _():
            step(0, t, t, False)

        @pl.when(j == i)
        def _():
            step(0, t // 2, t // 2, True)
            step(t // 2, t // 2, t, True)
            for hh in range(g):
                l = l_sc[hh]
                o_ref[:, hh * vw:(hh + 1) * vw] = (acc_sc[hh] / l).T.astype(BF16)
                lse_ref[hh] = m_sc[hh] + jnp.log(l)

    return pl.pallas_call(
        body, name=name,
        grid_spec=pltpu.PrefetchScalarGridSpec(
            num_scalar_prefetch=2, grid=(heads // g, len(pairs)),
            in_specs=[pl.BlockSpec((t, g * qw), lambda h, st, it, jt: (it[st], h)),
                      pl.BlockSpec((t, g * qw), lambda h, st, it, jt: (jt[st], h)),
                      pl.BlockSpec((g * vw, t), lambda h, st, it, jt: (h, jt[st]))],
            out_specs=[pl.BlockSpec((t, g * vw), lambda h, st, it, jt: (it[st], h)),
                       pl.BlockSpec((g, 1, t), lambda h, st, it, jt: (h, 0, it[st]))],
            scratch_shapes=[pltpu.VMEM((g, 1, t), F32), pltpu.VMEM((g, 1, t), F32),
                            pltpu.VMEM((g, vw, t), F32)]),
        out_shape=[jax.ShapeDtypeStruct((s, heads * vw), BF16),
                   jax.ShapeDtypeStruct((heads, 1, s), F32)],
        compiler_params=_cp(dimension_semantics=("arbitrary", "arbitrary")),
    )(i_tab, j_tab, qf, kf, vt)


def _flash_bwd(name, qf, kf, kft, v, do, o, lse, cq, sq, ck, sk, heads):
    s = qf.shape[0]
    t = _tile(s, ATT_TILE)
    n = s // t
    g = min(ATT_HEADS, heads)
    ng = heads // g
    qw, vw = HEAD_QK_PAD, HEAD_V

    pairs = [(j, i) for j in range(n) for i in range(j, n)]
    j_tab = jnp.asarray([p[0] for p in pairs], jnp.int32)
    i_tab = jnp.asarray([p[1] for p in pairs], jnp.int32)

    def body(jt_ref, it_ref, q_ref, k_ref, kt_ref, v_ref, do_ref, o_ref, lse_ref, cq_ref, sq_ref, ck_ref, sk_ref,
             dq_hbm, dkv_ref, dpe_hbm, dk_sc, dv_sc, dq_sc, dl_sc, pe_sc, dq_stage, pe_stage, sems):
        h, st = pl.program_id(0), pl.program_id(1)
        j, i = jt_ref[st], it_ref[st]
        cols = pl.ds(pl.multiple_of(i * t, t), t)
        rows = pl.ds(pl.multiple_of(j * t, t), t)

        def store(stage, dst, sem):
            cp = pltpu.make_async_copy(stage, dst, sem)
            cp.start()
            cp.wait()

        @pl.when((h == 0) & (st == 0))
        def _():
            pe_sc[...] = jnp.zeros_like(pe_sc)

        @pl.when(j == 0)
        def _():
            for hh in range(g):
                dq_sc[hh, :, cols] = jnp.zeros((qw, t), F32)
                hv = slice(hh * vw, (hh + 1) * vw)
                col = jnp.sum(do_ref[:, hv].astype(F32) * o_ref[:, hv].astype(F32), axis=1, keepdims=True)
                dl_sc[hh, :, cols] = jnp.broadcast_to(col, (t, LANE)).T[0:1, :]

        @pl.when(i == j)
        def _():
            dk_sc[...] = jnp.zeros_like(dk_sc)
            dv_sc[...] = jnp.zeros_like(dv_sc)

        def step(q0, nq, nk, diag):
            qs = slice(q0, q0 + nq)
            qcols = pl.ds(pl.multiple_of(i * t + q0, nq), nq)
            for hh in range(g):
                q = q_ref[qs, hh * qw:(hh + 1) * qw]
                dout = do_ref[qs, hh * vw:(hh + 1) * vw]
                sc = lax.dot_general(k_ref[0:nk, hh * qw:(hh + 1) * qw], q, _NT, preferred_element_type=F32)
                p = jnp.exp(sc - lse_ref[hh, :, qs])
                if diag:
                    p = jnp.where(_chunk_mask_t(nk, nq, q0), p, 0.0)
                dv_sc[hh, 0:nk, :] += jnp.dot(p.astype(BF16), dout, preferred_element_type=F32)
                dp = lax.dot_general(v_ref[0:nk, hh * vw:(hh + 1) * vw], dout, _NT, preferred_element_type=F32)
                ds = (p * (dp - dl_sc[hh, :, qcols])).astype(BF16)
                dk_sc[hh, 0:nk, :] += jnp.dot(ds, q, preferred_element_type=F32)
                dq_sc[hh, :, qcols] += jnp.dot(kt_ref[hh * qw:(hh + 1) * qw, 0:nk], ds,
                                               preferred_element_type=F32)

        @pl.when(i > j)
        def _():
            step(0, t, t, False)

        @pl.when(i == j)
        def _():
            step(0, t // 2, t // 2, True)
            step(t // 2, t // 2, t, True)
            for hh in range(g):
                dq_stage[:, hh * qw:(hh + 1) * qw] = _rope_t(dq_sc[hh, :, cols].T, cq_ref[...],
                                                             sq_ref[...]).astype(BF16)
            store(dq_stage, dq_hbm.at[cols, pl.ds(pl.multiple_of(h * (g * qw), g * qw), g * qw)], sems.at[0])

        @pl.when(i == n - 1)
        def _():
            pe = None
            for hh in range(g):
                dk = dk_sc[hh]
                dkv_ref[:, hh * qw:(hh + 1) * qw] = jnp.concatenate([dk[:, 0:HEAD_NOPE], dv_sc[hh]],
                                                                     axis=1).astype(BF16)
                part = dk[:, HEAD_NOPE:HEAD_QK_PAD]
                pe = part if pe is None else pe + part
            pe_sc[rows, :] += pe

            @pl.when(h == ng - 1)
            def _():
                pe_stage[...] = _rope_t(pe_sc[rows, :], ck_ref[...], sk_ref[...])
                store(pe_stage, dpe_hbm.at[rows, :], sems.at[1])

    qrow = lambda h, st, jt, it: (it[st], h)
    krow = lambda h, st, jt, it: (jt[st], h)
    qtab = lambda h, st, jt, it: (it[st], 0)
    ktab = lambda h, st, jt, it: (jt[st], 0)
    return pl.pallas_call(
        body, name=name,
        grid_spec=pltpu.PrefetchScalarGridSpec(
            num_scalar_prefetch=2, grid=(ng, len(pairs)),
            in_specs=[pl.BlockSpec((t, g * qw), qrow),
                      pl.BlockSpec((t, g * qw), krow),
                      pl.BlockSpec((g * qw, t), lambda h, st, jt, it: (h, jt[st])),
                      pl.BlockSpec((t, g * vw), krow),
                      pl.BlockSpec((t, g * vw), qrow),
                      pl.BlockSpec((t, g * vw), lambda h, st, jt, it: (jnp.where(jt[st] == 0, it[st], n - 1), h)),
                      pl.BlockSpec((g, 1, t), lambda h, st, jt, it: (h, 0, it[st])),
                      pl.BlockSpec((t, qw), qtab), pl.BlockSpec((t, qw), qtab),
                      pl.BlockSpec((t, LANE), ktab), pl.BlockSpec((t, LANE), ktab)],
            out_specs=[ANY, pl.BlockSpec((t, g * qw), krow), ANY],
            scratch_shapes=[pltpu.VMEM((g, t, qw), F32), pltpu.VMEM((g, t, vw), F32),
                            pltpu.VMEM((g, qw, s), F32), pltpu.VMEM((g, 1, s), F32), pltpu.VMEM((s, LANE), F32),
                            pltpu.VMEM((t, g * qw), BF16), pltpu.VMEM((t, LANE), F32),
                            pltpu.SemaphoreType.DMA((2,))]),
        out_shape=[jax.ShapeDtypeStruct(qf.shape, BF16),
                   jax.ShapeDtypeStruct((s, heads * (HEAD_NOPE + HEAD_V)), BF16),
                   jax.ShapeDtypeStruct((s, LANE), F32)],
        compiler_params=_cp(dimension_semantics=("arbitrary", "arbitrary")),
    )(j_tab, i_tab, qf, kf, kft, v, do, o, lse, cq, sq, ck, sk)


def _adamw(name, parts, w, m, v):
    p, r, c = parts.shape
    tr = _tile(r, max(8, (256 * 1024) // max(c, 1)))
    bc1 = 1.0 - ADAM_B1 ** ADAM_STEP
    bc2 = 1.0 - ADAM_B2 ** ADAM_STEP

    def body(p_ref, w_ref, m_ref, v_ref, g_ref, d_ref, nm_ref, nv_ref):
        g = p_ref[0].astype(F32)
        for q in range(1, p):
            g = g + p_ref[q].astype(F32)
        nm = ADAM_B1 * m_ref[...] + (1.0 - ADAM_B1) * g
        nv = ADAM_B2 * v_ref[...] + (1.0 - ADAM_B2) * (g * g)
        g_ref[...] = g
        nm_ref[...] = nm
        nv_ref[...] = nv
        d_ref[...] = -ADAM_LR * ((nm / bc1) / (jnp.sqrt(nv / bc2) + ADAM_EPS) + ADAM_WD * w_ref[...])

    blk = pl.BlockSpec((tr, c), lambda i: (i, 0))
    sh = jax.ShapeDtypeStruct((r, c), F32)
    return pl.pallas_call(
        body, name=name, grid=(r // tr,),
        in_specs=[pl.BlockSpec((p, tr, c), lambda i: (0, i, 0)), blk, blk, blk],
        out_specs=[blk] * 4, out_shape=[sh] * 4,
        compiler_params=_cp(),
    )(parts, w, m, v)


def _my_place():
    x, y, c = lax.axis_index("x"), lax.axis_index("y"), lax.axis_index("c")
    return x, y, c


def _flip(v, bit):
    return 1 - v if bit else v


def _block(ref, axis, idx, size):
    return ref.at[(slice(None),) * axis + (pl.ds(idx * size, size),)]


HBM_SPEC = pl.BlockSpec(memory_space=pltpu.HBM)
SEM_SPEC = pl.BlockSpec(memory_space=pltpu.SEMAPHORE)
DATAFLOW = pltpu.SideEffectType.DATAFLOW_SIDE_EFFECTING


def _hbm(a):
    return pltpu.with_memory_space_constraint(a, pltpu.HBM)


def _remote_copies(jobs, bufs, send_sems, recv_sems):
    return [pltpu.make_async_remote_copy(src_ref=src, dst_ref=dst, send_sem=send_sems.at[q],
                                         recv_sem=recv_sems.at[q], device_id=dev, device_id_type=MESH)
            for q, (src, dst, dev) in enumerate(jobs(bufs))]


def _split_start(name, bufs, jobs, n_jobs, after):
    nb = len(bufs)

    def body(*refs):
        send_sems, recv_sems = refs[nb + 1], refs[nb + 2]
        for cp in _remote_copies(jobs, refs[:nb], send_sems, recv_sems):
            cp.start()
        refs[-1][...] = jnp.zeros_like(refs[-1])

    outs = pl.pallas_call(
        body, name=name,
        out_shape=(pltpu.SemaphoreType.DMA((n_jobs,)), pltpu.SemaphoreType.DMA((n_jobs,)),
                   *[pltpu.HBM(b.shape, b.dtype) for b in bufs], jax.ShapeDtypeStruct((8, LANE), F32)),
        in_specs=[HBM_SPEC] * nb + [ANY],
        out_specs=(SEM_SPEC, SEM_SPEC, *[HBM_SPEC] * nb, VMEM_SPEC),
        input_output_aliases={q: 2 + q for q in range(nb)},
        compiler_params=pltpu.CompilerParams(has_side_effects=DATAFLOW),
    )(*[_hbm(b) for b in bufs], after)
    return outs[0], outs[1], list(outs[2:2 + nb]), outs[-1]


def _split_wait(name, bufs, send_sems, recv_sems, jobs, after):
    nb = len(bufs)

    def body(*refs):
        for cp in _remote_copies(jobs, refs[:nb], refs[nb], refs[nb + 1]):
            cp.wait_send()
            cp.wait_recv()

    outs = pl.pallas_call(
        body, name=name,
        out_shape=tuple(pltpu.HBM(b.shape, b.dtype) for b in bufs),
        in_specs=[HBM_SPEC] * nb + [SEM_SPEC, SEM_SPEC, ANY],
        out_specs=tuple([HBM_SPEC] * nb),
        input_output_aliases={q: q for q in range(nb)},
        compiler_params=pltpu.CompilerParams(has_side_effects=DATAFLOW),
    )(*bufs, send_sems, recv_sems, after)
    return list(outs)


PLACE_TILE_BYTES = 2 * 1024 * 1024


def _own_block_spec(tr, c, nblk, axis):
    if axis == 0:
        return pl.BlockSpec((tr, c), lambda i, me: (me[0] * nblk + i, 0))
    return pl.BlockSpec((tr, c), lambda i, me: (i, me[0]))


def _cast_place(name, w, layer, axis, me):
    _, r, c = w.shape
    tr = _tile(r, max(SUBLANE_BF16, PLACE_TILE_BYTES // (4 * c)))
    nblk = r // tr
    full = (N_DEV * r, c) if axis == 0 else (r, N_DEV * c)

    def body(me_ref, w_ref, o_ref):
        o_ref[...] = w_ref[...].astype(BF16)

    return pl.pallas_call(
        body, name=name,
        grid_spec=pltpu.PrefetchScalarGridSpec(
            num_scalar_prefetch=1, grid=(nblk,),
            in_specs=[pl.BlockSpec((None, tr, c), lambda i, me: (layer, i, 0))],
            out_specs=_own_block_spec(tr, c, nblk, axis)),
        out_shape=jax.ShapeDtypeStruct(full, BF16), compiler_params=_cp(),
    )(me, w)


def _own_place(name, grad, land, layer, axis, me):
    _, _, r, c = land.shape
    tr = _tile(r, max(SUBLANE_BF16, PLACE_TILE_BYTES // (2 * c)))
    nblk = r // tr

    def body(me_ref, g_ref, land_ref, o_ref):
        o_ref[...] = g_ref[...]

    return pl.pallas_call(
        body, name=name,
        grid_spec=pltpu.PrefetchScalarGridSpec(
            num_scalar_prefetch=1, grid=(nblk,),
            in_specs=[_own_block_spec(tr, c, nblk, axis), ANY],
            out_specs=pl.BlockSpec((None, None, tr, c), lambda i, me: (0, layer, i, 0))),
        out_shape=jax.ShapeDtypeStruct(land.shape, land.dtype),
        input_output_aliases={2: 0}, compiler_params=_cp(),
    )(me, grad, land)


def _gather_jobs_a(axes, sizes):
    def jobs(bufs):
        x, y, c = _my_place()
        out = []
        for t, buf in enumerate(bufs):
            blk = _block(buf, axes[t], 4 * x + 2 * y + c, sizes[t])
            for dev in [(x, y, 1 - c), (1 - x, y, c), (x, 1 - y, c), (1 - x, 1 - y, c)]:
                out.append((blk, blk, dev))
        return out
    return jobs


def _gather_jobs_b(axes, sizes):
    nt = len(axes)

    def jobs(bufs):
        x, y, c = _my_place()
        out = []
        for t in range(nt):
            for px, py in [(1 - x, y), (x, 1 - y), (1 - x, 1 - y)]:
                blk = _block(bufs[t], axes[t], 4 * px + 2 * py + c, sizes[t])
                out.append((blk, blk, (x, y, 1 - c)))
        return out
    return jobs


def _exchange_jobs(axes, sizes, layers):
    nt = len(axes)

    def jobs(bufs):
        x, y, c = _my_place()
        out = []
        for k in range(1, N_DEV):
            px, py, pc = _flip(x, k & 4), _flip(y, k & 2), _flip(c, k & 1)
            for t in range(nt):
                out.append((_block(bufs[t], axes[t], 4 * px + 2 * py + pc, sizes[t]),
                            bufs[nt + t].at[k, layers[t]], (px, py, pc)))
        return out
    return jobs


def _gather_begin(name, lands, axes, after):
    sizes = [b.shape[ax] // N_DEV for b, ax in zip(lands, axes)]
    jobs = _gather_jobs_a(axes, sizes)
    send, recv, bufs, token = _split_start(name + "_a", lands, jobs, 4 * len(lands), after)
    return dict(name=name, axes=axes, sizes=sizes, send=send, recv=recv, bufs=bufs, jobs=jobs), token


def _gather_mid(h, after):
    bufs = _split_wait(h["name"] + "_aw", h["bufs"], h["send"], h["recv"], h["jobs"], after)
    jobs = _gather_jobs_b(h["axes"], h["sizes"])
    send, recv, lands, token = _split_start(h["name"] + "_b", bufs, jobs, 3 * len(bufs), after)
    return dict(h, send=send, recv=recv, bufs=lands, jobs=jobs), token


def _gather_end(h, after):
    return _split_wait(h["name"] + "_bw", h["bufs"], h["send"], h["recv"], h["jobs"], after)


def _exchange_begin(name, grads, axes, lands, layers, me, after):
    sizes = [g.shape[ax] // N_DEV for g, ax in zip(grads, axes)]
    lands = [_own_place(f"{name}_place{t}", grads[t], lands[t], layers[t], axes[t], me)
             for t in range(len(grads))]
    jobs = _exchange_jobs(axes, sizes, layers)
    send, recv, bufs, token = _split_start(name + "_s", list(grads) + lands, jobs, 7 * len(grads), after)
    return dict(name=name, n=len(grads), send=send, recv=recv, bufs=bufs, jobs=jobs), token


def _exchange_end(h, after):
    bufs = _split_wait(h["name"] + "_w", h["bufs"], h["send"], h["recv"], h["jobs"], after)
    return bufs[h["n"]:]


def _all_gather_small(name, vec, reduce):
    r = vec.shape[0]

    def body(v_ref, o_ref, *rest):
        if reduce:
            buf, send_sems, recv_sems = rest
        else:
            buf = o_ref
            send_sems, recv_sems = rest
        x, y, c = _my_place()
        mine = 4 * x + 2 * y + c
        buf[mine] = v_ref[...]
        copies = []
        for k in range(1, N_DEV):
            px, py, pc = _flip(x, k & 4), _flip(y, k & 2), _flip(c, k & 1)
            cp = pltpu.make_async_remote_copy(
                src_ref=v_ref, dst_ref=buf.at[mine], send_sem=send_sems.at[k - 1],
                recv_sem=recv_sems.at[k - 1], device_id=(px, py, pc), device_id_type=MESH)
            cp.start()
            copies.append(cp)
        for cp in copies:
            cp.wait()
        if reduce:
            acc = buf[0]
            for q in range(1, N_DEV):
                acc = acc + buf[q]
            o_ref[...] = acc

    scratch = [pltpu.SemaphoreType.DMA((N_DEV - 1,)), pltpu.SemaphoreType.DMA((N_DEV - 1,))]
    if reduce:
        scratch = [pltpu.VMEM((N_DEV, r, LANE), F32)] + scratch
        out_shape = jax.ShapeDtypeStruct((r, LANE), F32)
    else:
        out_shape = jax.ShapeDtypeStruct((N_DEV, r, LANE), F32)
    return pl.pallas_call(
        body, name=name, in_specs=[VMEM_SPEC], out_specs=VMEM_SPEC, out_shape=out_shape,
        scratch_shapes=scratch, compiler_params=_cp(has_side_effects=True),
    )(vec)


def _pack(arrs, row_mult=8):
    flat = jnp.concatenate([a.reshape(-1).astype(F32) for a in arrs])
    n = flat.shape[0]
    rows = -(-n // LANE)
    rows = -(-rows // row_mult) * row_mult
    return jnp.pad(flat, (0, rows * LANE - n)).reshape(rows, LANE)


def _unpack(vec, shapes):
    flat = vec.reshape(-1)
    out, pos = [], 0
    for sh in shapes:
        n = 1
        for s in sh:
            n *= s
        out.append(flat[pos:pos + n].reshape(sh))
        pos += n
    return out


BIG = ["conv_w_pw1", "conv_w_pw2", "mla_w_in", "mla_w_q_up", "mla_w_kv_up", "mla_w_o", "mlp_w1", "mlp_w2"]
BIG_AXIS = {"conv_w_pw1": 2, "conv_w_pw2": 1, "mla_w_in": 1, "mla_w_q_up": 2, "mla_w_kv_up": 2,
            "mla_w_o": 1, "mlp_w1": 2, "mlp_w2": 1}
SMALL_SHARDED = ["conv_w_dw", "mla_q_norm_g", "mla_kv_norm_g"]
REPLICATED = ["norm_mixer_g", "norm_mlp_g", "conv_b_pw1", "conv_b_dw", "conv_ln_g", "conv_ln_b",
              "conv_b_pw2", "final_norm_g"]
WEIGHTS = ["norm_mixer_g", "norm_mlp_g", "conv_w_pw1", "conv_b_pw1", "conv_w_dw", "conv_b_dw",
           "conv_ln_g", "conv_ln_b", "conv_w_pw2", "conv_b_pw2", "mla_w_in", "mla_q_norm_g",
           "mla_kv_norm_g", "mla_w_q_up", "mla_w_kv_up", "mla_w_o", "mlp_w1", "mlp_w2", "final_norm_g"]


def _unshard_last(g, lead):
    nd = g.ndim
    perm = tuple(range(1, nd - 1)) + (0, nd - 1)
    return g.transpose(perm).reshape(lead + (N_DEV * g.shape[-1],))


def _step(w, m, v, x, positions, target):
    s, d = x.shape
    depth = w["norm_mixer_g"].shape[0]
    n_conv, n_mla = w["conv_w_pw1"].shape[0], w["mla_w_in"].shape[0]
    heads = (w["mla_w_q_up"].shape[-1] * N_DEV) // (HEAD_NOPE + HEAD_ROPE)
    rq, rkv = w["mla_w_q_up"].shape[1], w["mla_w_kv_up"].shape[1]
    xi, yi, ci = _my_place()
    mine = 4 * xi + 2 * yi + ci

    def mixer_units(layer):
        names = (["conv_w_pw1", "conv_w_pw2"] if layer % 2 == 0
                 else ["mla_w_in", "mla_w_q_up", "mla_w_kv_up", "mla_w_o"])
        return [(n, layer // 2) for n in names]

    def mlp_units(layer):
        return [("mlp_w1", layer), ("mlp_w2", layer)]

    me_arr = mine.astype(jnp.int32).reshape(1)

    def gather_begin(tag, units, after):
        lands = [_cast_place(f"{tag}_place_{n}", w[n], jl, BIG_AXIS[n] - 1, me_arr) for n, jl in units]
        h, token = _gather_begin(tag, lands, [BIG_AXIS[n] - 1 for n, _ in units], after)
        return dict(h, units=units), token

    full = {}

    def gather_end(h, after):
        full.update(zip(h["units"], _gather_end(h, after)))

    small_shapes = [w[n].shape for n in SMALL_SHARDED]
    gathered = _all_gather_small("gather_small", _pack([w[n] for n in SMALL_SHARDED]), False)

    first_a, tok = gather_begin("gather_0a", mixer_units(0), gathered)
    first_b, tok = gather_begin("gather_0b", mlp_units(0), tok)
    pending, pending_mlp = {}, {}
    if depth > 1:
        pending[1], tok = gather_begin("gather_1", mixer_units(1), tok)
        pending_mlp[1], tok = gather_begin("gather_1b", mlp_units(1), tok)
    h_first = _rms_fwd("rms_mixer_0", x, w["norm_mixer_g"][0].reshape(1, -1) + tok[0, 0])
    first_a, tok = _gather_mid(first_a, h_first)
    gather_end(first_a, tok)

    per_dev = [_unpack(gathered[q], small_shapes) for q in range(N_DEV)]
    w_dw = _unshard_last(jnp.stack([p[0] for p in per_dev]), (n_conv, CONV_W))
    q_gain = _unshard_last(jnp.stack([p[1] for p in per_dev]), (n_mla,))
    kv_gain = _unshard_last(jnp.stack([p[2] for p in per_dev]), (n_mla,))
    w_dw_pad = jnp.pad(w_dw, ((0, 0), (0, HALO - CONV_W), (0, 0)))

    w_in_cols = rq + rkv + HEAD_ROPE

    def pad_w_in(a):
        return jnp.pad(a, ((0, 0), (0, rq + rkv + LANE - w_in_cols)))

    def pad_wq(a):
        return jnp.pad(a.reshape(rq, heads, HEAD_NOPE + HEAD_ROPE),
                       ((0, 0), (0, 0), (0, HEAD_QK_PAD - HEAD_NOPE - HEAD_ROPE))).reshape(rq, heads * HEAD_QK_PAD)

    inv_freq = ROPE_THETA ** (-jnp.arange(0, HEAD_ROPE, 2, dtype=F32) / HEAD_ROPE)
    ang = positions.reshape(s).astype(F32)[:, None] * inv_freq
    cos, sin = jnp.cos(ang), jnp.sin(ang)
    c64 = jnp.concatenate([cos, cos], axis=1)
    s64 = jnp.concatenate([-sin, sin], axis=1)
    zeros64 = jnp.zeros((s, LANE - HEAD_ROPE), F32)
    ck = jnp.concatenate([c64, zeros64], axis=1)
    sk = jnp.concatenate([s64, zeros64], axis=1)
    scale = (HEAD_NOPE + HEAD_ROPE) ** -0.5
    cq = scale * jnp.concatenate([jnp.ones((s, HEAD_NOPE), F32), ck], axis=1)
    sq = scale * jnp.concatenate([jnp.zeros((s, HEAD_NOPE), F32), sk], axis=1)

    def vec(a):
        return a.reshape(1, -1)

    saved = []
    wpad = {}
    for layer in range(depth):
        jl = layer // 2
        h = h_first if layer == 0 else _rms_fwd(f"rms_mixer_{layer}", x,
                                                 vec(w["norm_mixer_g"][layer]) + tok[0, 0])
        if layer % 2 == 0:
            ua, ug, glu = _mm_glu(f"conv_pw1_{layer}", h, full["conv_w_pw1", jl], None, vec(w["conv_b_pw1"][jl]))
            cc, sw = _conv_fwd(f"conv_dw_{layer}", glu, w_dw_pad[jl], vec(w["conv_b_dw"][jl]),
                               vec(w["conv_ln_g"][jl]), vec(w["conv_ln_b"][jl]))
            x1 = _mm_res(f"conv_pw2_{layer}", sw, full["conv_w_pw2", jl], None, x, vec(w["conv_b_pw2"][jl]))
            mix = (h, ua, ug, glu, cc, sw)
        else:
            wpad["in", jl] = pad_w_in(full["mla_w_in", jl])
            wpad["q", jl] = pad_wq(full["mla_w_q_up", jl])
            down = _mm_plain(f"mla_down_{layer}", h, wpad["in", jl], None, "nn", F32)
            qn, kvn, kpe = _mla_mid_fwd(f"mla_mid_{layer}", down, vec(q_gain[jl]), vec(kv_gain[jl]), ck, sk)
            qf = _mm_q(f"mla_q_{layer}", qn, wpad["q", jl], None, cq, sq)
            kf, vv, kft, vt = _mm_kv(f"mla_kv_{layer}", kvn, full["mla_w_kv_up", jl], None, kpe)
            o, lse = _flash_fwd(f"mla_attn_{layer}", qf, kf, vt, heads)
            if layer in pending_mlp:
                pending_mlp[layer], tok = _gather_mid(pending_mlp[layer], o)
                tok, o = lax.optimization_barrier((tok, o))
            x1 = _mm_res(f"mla_out_{layer}", o, full["mla_w_o", jl], None, x)
            if layer in pending_mlp:
                gather_end(pending_mlp.pop(layer), x1)
            mix = (h, down, qn, kvn, qf, kf, kft, vv, o, lse)
        anchor = x1
        if layer == 0:
            first_b, anchor = _gather_mid(first_b, anchor)
        if layer + 2 < depth:
            pending[layer + 2], anchor = gather_begin(f"gather_{layer + 2}",
                                                      mixer_units(layer + 2) + mlp_units(layer + 2), anchor)
        if layer == 0:
            gather_end(first_b, anchor)
        elif layer + 1 < depth:
            pending[layer + 1], anchor = _gather_mid(pending[layer + 1], anchor)
        if anchor is not x1:
            tok = anchor
        h2 = _rms_fwd(f"rms_mlp_{layer}", x1, vec(w["norm_mlp_g"][layer]) + tok[0, 0])
        z, a = _mm_mlp_up(f"mlp_up_{layer}", h2, full["mlp_w1", layer], None)
        x2 = _mm_res(f"mlp_down_{layer}", a, full["mlp_w2", layer], None, x1)
        if layer + 1 < depth:
            if layer == 0:
                pending[1], tok = _gather_mid(pending[1], x2)
                gather_end(pending[1], tok)
            else:
                gather_end(pending[layer + 1], x2)
        saved.append((x, mix, x1, h2, z, a))
        x = x2

    loss_row, g, gb, d_final, _ = _final_loss("final_loss", x, vec(w["final_norm_g"]) + tok[0, 0], target)

    recv = {n: lax.empty((N_DEV,) + w[n].shape, BF16) for n in BIG}

    def exchange_begin(tag, items, after):
        names = [n for n, _, _ in items]
        h, token = _exchange_begin(tag, [gr for _, _, gr in items], [BIG_AXIS[n] - 1 for n in names],
                                   [recv[n] for n in names], [jl for _, jl, _ in items], me_arr, after)
        return dict(h, names=names), token

    def exchange_end(h, after):
        recv.update(zip(h["names"], _exchange_end(h, after)))

    mix_exchanges = []
    d_mixer, d_mlp = [None] * depth, [None] * depth
    d_small = {n: [None] * n_conv for n in ["conv_b_pw1", "conv_w_dw", "conv_b_dw", "conv_ln_g",
                                           "conv_ln_b", "conv_b_pw2"]}
    d_qg, d_kvg = [None] * n_mla, [None] * n_mla
    for layer in reversed(range(depth)):
        jl = layer // 2
        x0, mix, x1, h2, z, a = saved[layer]
        colsum_g = None
        dz = _mm_mlp_dz(f"mlp_dz_{layer}", gb, full["mlp_w2", layer], None, z)
        dw2 = _mm_wgrad(f"mlp_dw2_{layer}", a, gb)
        w2_exchange, tok = exchange_begin(f"exchange_w2_{layer}", [("mlp_w2", layer, dw2)], dz)
        tok, dz = lax.optimization_barrier((tok, dz))
        dh2 = _mm_plain(f"mlp_dh_{layer}", dz, full["mlp_w1", layer], None, "nt", F32)
        dw1 = _mm_wgrad(f"mlp_dw1_{layer}", h2, dz)
        w1_exchange, tok = exchange_begin(f"exchange_w1_{layer}", [("mlp_w1", layer, dw1)], tok)
        g, gb, d_mlp[layer], colsum_g = _rms_bwd(f"rms_mlp_bwd_{layer}", x1,
                                                 vec(w["norm_mlp_g"][layer]) + tok[0, 0], dh2, g)
        for hx in mix_exchanges:
            exchange_end(hx, g)
        if layer % 2 == 0:
            h, ua, ug, glu, cc, sw = mix
            d_small["conv_b_pw2"][jl] = colsum_g.reshape(-1)
            dsw = _mm_plain(f"conv_ds_{layer}", gb, full["conv_w_pw2", jl], None, "nt", F32)
            dwp2 = _mm_wgrad(f"conv_dw2_{layer}", sw, gb)
            hx2, tok = exchange_begin(f"exchange_pw2_{layer}", [("conv_w_pw2", jl, dwp2)], dsw)
            dc, dlg, dlb, dbdw = _conv_bwd_ln(f"conv_ln_bwd_{layer}", dsw, cc,
                                              vec(w["conv_ln_g"][jl]) + tok[0, 0], vec(w["conv_ln_b"][jl]))
            du, dwdw, dbu = _conv_bwd_dw(f"conv_dw_bwd_{layer}", dc, glu, ua, ug, w_dw_pad[jl])
            d_small["conv_ln_g"][jl] = dlg.reshape(-1)
            d_small["conv_ln_b"][jl] = dlb.reshape(-1)
            d_small["conv_b_dw"][jl] = dbdw.reshape(-1)
            d_small["conv_w_dw"][jl] = dwdw[:CONV_W]
            d_small["conv_b_pw1"][jl] = dbu.reshape(-1)
            dwp1 = _mm_wgrad(f"conv_dw1_{layer}", h, du)
            hx1, tok = exchange_begin(f"exchange_pw1_{layer}", [("conv_w_pw1", jl, dwp1)], dbu)
            tok, du = lax.optimization_barrier((tok, du))
            dh = _mm_plain(f"conv_dh_{layer}", du, full["conv_w_pw1", jl], None, "nt", F32)
            mix_exchanges = [hx2, hx1]
        else:
            h, down, qn, kvn, qf, kf, kft, vv, o, lse = mix
            do = _mm_plain(f"mla_do_{layer}", gb, full["mla_w_o", jl], None, "nt", BF16)
            dwo = _mm_wgrad(f"mla_dwo_{layer}", o, gb)
            dq, dkv, dkpe = _flash_bwd(f"mla_attn_bwd_{layer}", qf, kf, kft, vv, do, o, lse, cq, sq, ck, sk, heads)
            dqn = _mm_plain(f"mla_dqn_{layer}", dq, wpad["q", jl], None, "nt", F32)
            dwq = _mm_wgrad(f"mla_dwq_{layer}", qn, dq).reshape(rq, heads, HEAD_QK_PAD)[
                :, :, :HEAD_NOPE + HEAD_ROPE].reshape(rq, heads * (HEAD_NOPE + HEAD_ROPE))
            dkvn = _mm_plain(f"mla_dkvn_{layer}", dkv, full["mla_w_kv_up", jl], None, "nt", F32)
            dwkv = _mm_wgrad(f"mla_dwkv_{layer}", kvn, dkv)
            ddown, d_qg[jl], d_kvg[jl] = _mla_mid_bwd(f"mla_mid_bwd_{layer}", down, vec(q_gain[jl]),
                                                      vec(kv_gain[jl]), dqn, dkvn, dkpe)
            dh = _mm_plain(f"mla_dh_{layer}", ddown, wpad["in", jl], None, "nt", F32)
            dwin = _mm_wgrad(f"mla_dwin_{layer}", h, ddown)[:, :w_in_cols]
            items = [("mla_w_in", jl, dwin), ("mla_w_q_up", jl, dwq), ("mla_w_kv_up", jl, dwkv),
                     ("mla_w_o", jl, dwo)]
            hx, tok = exchange_begin(f"exchange_mix_{layer}", items, dh)
            mix_exchanges = [hx]
        g, gb, d_mixer[layer], _ = _rms_bwd(f"rms_mixer_bwd_{layer}", x0,
                                            vec(w["norm_mixer_g"][layer]) + tok[0, 0], dh, g)
        exchange_end(w2_exchange, g)
        exchange_end(w1_exchange, g)
    grad_x = g

    out = {}

    def adamw_big(n):
        sh = w[n].shape
        r, c = sh[0] * sh[1], sh[2]
        res = _adamw(f"adamw_{n}", recv[n].reshape(N_DEV, r, c), w[n].reshape(r, c),
                     m[n].reshape(r, c), v[n].reshape(r, c))
        out[n] = [t.reshape(sh) for t in res]

    late = [n for hx in mix_exchanges for n in hx["names"]]
    early = [n for n in BIG if n not in late]
    for n in early:
        adamw_big(n)
    anchor = out[early[-1]][1]
    for hx in mix_exchanges:
        exchange_end(hx, anchor)
    for n in late:
        adamw_big(n)

    small_full = {
        "norm_mixer_g": jnp.concatenate(d_mixer, axis=0), "norm_mlp_g": jnp.concatenate(d_mlp, axis=0),
        "conv_b_pw1": jnp.stack(d_small["conv_b_pw1"]), "conv_b_dw": jnp.stack(d_small["conv_b_dw"]),
        "conv_ln_g": jnp.stack(d_small["conv_ln_g"]), "conv_ln_b": jnp.stack(d_small["conv_ln_b"]),
        "conv_b_pw2": jnp.stack(d_small["conv_b_pw2"]), "final_norm_g": d_final.reshape(-1),
        "conv_w_dw": jnp.stack(d_small["conv_w_dw"]),
        "mla_q_norm_g": jnp.concatenate(d_qg, axis=0), "mla_kv_norm_g": jnp.concatenate(d_kvg, axis=0),
    }
    names = REPLICATED + SMALL_SHARDED
    packed, _ = lax.optimization_barrier((_pack([small_full[n] for n in names]), anchor))
    summed = _unpack(_all_gather_small("reduce_small", packed, True), [small_full[n].shape for n in names])
    summed = dict(zip(names, summed))
    for n in SMALL_SHARDED:
        width = w[n].shape[-1]
        summed[n] = lax.dynamic_slice_in_dim(summed[n], mine * width, width, axis=summed[n].ndim - 1)
    for group, tag in ((REPLICATED, "replicated"), (SMALL_SHARDED, "small_sharded")):
        shapes = [w[n].shape for n in group]
        res = _adamw(f"adamw_{tag}", _pack([summed[n] for n in group])[None],
                     _pack([w[n] for n in group]), _pack([m[n] for n in group]), _pack([v[n] for n in group]))
        unpacked = [_unpack(t, shapes) for t in res]
        for q, n in enumerate(group):
            out[n] = [unpacked[0][q], unpacked[1][q], unpacked[2][q], unpacked[3][q]]

    loss = lax.psum(loss_row[0, 0], ("x", "y", "c"))
    return loss, grad_x, out


def kernel(x, positions, norm_mixer_g, norm_mlp_g, conv_w_pw1, conv_b_pw1, conv_w_dw, conv_b_dw, conv_ln_g, conv_ln_b, conv_w_pw2, conv_b_pw2, mla_w_in, mla_q_norm_g, mla_kv_norm_g, mla_w_q_up, mla_w_kv_up, mla_w_o, mlp_w1, mlp_w2, final_norm_g, loss_target, m_norm_mixer_g, m_norm_mlp_g, m_conv_w_pw1, m_conv_b_pw1, m_conv_w_dw, m_conv_b_dw, m_conv_ln_g, m_conv_ln_b, m_conv_w_pw2, m_conv_b_pw2, m_mla_w_in, m_mla_q_norm_g, m_mla_kv_norm_g, m_mla_w_q_up, m_mla_w_kv_up, m_mla_w_o, m_mlp_w1, m_mlp_w2, m_final_norm_g, v_norm_mixer_g, v_norm_mlp_g, v_conv_w_pw1, v_conv_b_pw1, v_conv_w_dw, v_conv_b_dw, v_conv_ln_g, v_conv_ln_b, v_conv_w_pw2, v_conv_b_pw2, v_mla_w_in, v_mla_q_norm_g, v_mla_kv_norm_g, v_mla_w_q_up, v_mla_w_kv_up, v_mla_w_o, v_mlp_w1, v_mlp_w2, v_final_norm_g):
    ws = (norm_mixer_g, norm_mlp_g, conv_w_pw1, conv_b_pw1, conv_w_dw, conv_b_dw, conv_ln_g, conv_ln_b,
          conv_w_pw2, conv_b_pw2, mla_w_in, mla_q_norm_g, mla_kv_norm_g, mla_w_q_up, mla_w_kv_up, mla_w_o,
          mlp_w1, mlp_w2, final_norm_g)
    ms = (m_norm_mixer_g, m_norm_mlp_g, m_conv_w_pw1, m_conv_b_pw1, m_conv_w_dw, m_conv_b_dw, m_conv_ln_g,
          m_conv_ln_b, m_conv_w_pw2, m_conv_b_pw2, m_mla_w_in, m_mla_q_norm_g, m_mla_kv_norm_g,
          m_mla_w_q_up, m_mla_w_kv_up, m_mla_w_o, m_mlp_w1, m_mlp_w2, m_final_norm_g)
    vs = (v_norm_mixer_g, v_norm_mlp_g, v_conv_w_pw1, v_conv_b_pw1, v_conv_w_dw, v_conv_b_dw, v_conv_ln_g,
          v_conv_ln_b, v_conv_w_pw2, v_conv_b_pw2, v_mla_w_in, v_mla_q_norm_g, v_mla_kv_norm_g,
          v_mla_w_q_up, v_mla_w_kv_up, v_mla_w_o, v_mlp_w1, v_mlp_w2, v_final_norm_g)
    w, m, v = dict(zip(WEIGHTS, ws)), dict(zip(WEIGHTS, ms)), dict(zip(WEIGHTS, vs))
    s, d = x.shape[-2], x.shape[-1]
    loss, grad_x, out = _step(w, m, v, x.reshape(s, d), positions, loss_target.reshape(s, d))
    grads = [out[n][0] for n in WEIGHTS]
    deltas = [out[n][1] for n in WEIGHTS]
    new_m = [out[n][2] for n in WEIGHTS]
    new_v = [out[n][3] for n in WEIGHTS]
    return (loss, grad_x.reshape(x.shape), *grads, *deltas, *new_m, *new_v)
```
